```python
import jax, jax.numpy as jnp
from jax import lax
import numpy as np

D_MODEL = 1024
BATCH = 8
SEQ = 4096
DEPTH = 1

CHUNK = 64
HG_HEADS = 8
HG_DK = 64
HG_DV = 64
HG_WIDTH = HG_HEADS * HG_DK
CONV_WIDTH = 512
CONV_K = 3
N_BRANCH = 2
PEER_HEADS = 8
PEER_NKEYS = 128
PEER_N = PEER_NKEYS * PEER_NKEYS
PEER_QDIM = 256
PEER_HALF = PEER_QDIM // 2
PEER_TOPK = 16
PEER_BLOCK = 128
EPS = 1e-6
IN_SPLITS = [HG_WIDTH] * 4 + [CONV_WIDTH] * 3 + [N_BRANCH * D_MODEL]
IN_COLS = sum(IN_SPLITS)

kernel_name = "hybrid_hgrn2_shortconv_peer_block"


def rmsnorm(x, g):
    xf = x.astype(jnp.float32)
    y = xf * lax.rsqrt(jnp.mean(xf * xf, axis=-1, keepdims=True) + EPS)
    return (y * g.astype(jnp.float32)).astype(x.dtype)


def hgrn2_chunkwise(q, k, log_f, v):
    b_, s_, h_, dk = q.shape
    dv = v.shape[-1]
    nc = s_ // CHUNK

    def to_chunks(t):
        return t.astype(jnp.float32).reshape(b_, nc, CHUNK, h_, t.shape[-1]).transpose(1, 0, 3, 2, 4)

    qc, kc, fc, vc = to_chunks(q), to_chunks(k), to_chunks(log_f), to_chunks(v)
    causal = jnp.tril(jnp.ones((CHUNK, CHUNK), dtype=bool))

    def step(state, inp):
        q_, k_, lf, v_ = inp
        b = jnp.cumsum(lf, axis=-2)
        b_last = b[..., -1:, :]
        o_inter = jnp.einsum('bhtd,bhde->bhte', q_ * jnp.exp(b), state)
        rel = jnp.where(causal[:, :, None], b[..., :, None, :] - b[..., None, :, :], -jnp.inf)
        scores = jnp.einsum('bhtd,bhsd,bhtsd->bhts', q_, k_, jnp.exp(rel))
        o = o_inter + jnp.einsum('bhts,bhse->bhte', scores, v_)
        new_state = (jnp.exp(b_last[..., 0, :])[..., None] * state
                     + jnp.einsum('bhsd,bhse->bhde', k_ * jnp.exp(b_last - b), v_))
        return new_state, o

    s0 = jnp.zeros((b_, h_, dk, dv), jnp.float32)
    _, o = lax.scan(step, s0, (qc, kc, fc, vc))
    return o.transpose(1, 0, 3, 2, 4).reshape(b_, s_, h_, dv)


def causal_depthwise_conv(u, w):
    c = u.shape[-1]
    return lax.conv_general_dilated(
        u, w[:, None, :].astype(u.dtype), window_strides=(1,), padding=[(CONV_K - 1, 0)],
        dimension_numbers=('NWC', 'WIO', 'NWC'), feature_group_count=c)


def peer_ffn(h, w_query, keys1, keys2, expert_u, expert_v):
    b_, s_, d = h.shape
    t = b_ * s_
    hf = h.reshape(t, d)
    q = (hf @ w_query).reshape(t, PEER_HEADS, 2, PEER_HALF)
    s1 = jnp.einsum('thc,hnc->thn', q[:, :, 0], keys1)
    s2 = jnp.einsum('thc,hnc->thn', q[:, :, 1], keys2)
    v1, i1 = lax.top_k(s1, PEER_TOPK)
    v2, i2 = lax.top_k(s2, PEER_TOPK)
    cand_s = (v1[..., :, None] + v2[..., None, :]).reshape(t, PEER_HEADS, PEER_TOPK * PEER_TOPK)
    cand_i = (i1[..., :, None] * PEER_NKEYS + i2[..., None, :]).reshape(t, PEER_HEADS, PEER_TOPK * PEER_TOPK)
    top_s, top_pos = lax.top_k(cand_s, PEER_TOPK)
    idx = jnp.take_along_axis(cand_i, top_pos, axis=-1)
    gate = jax.nn.softmax(top_s.astype(jnp.float32), axis=-1).astype(h.dtype)
    nb = t // PEER_BLOCK

    def block(args):
        hb, ib, gb = args
        act = jax.nn.gelu(jnp.einsum('thkd,td->thk', expert_u[ib], hb), approximate=False)
        return jnp.einsum('thk,thkd->td', gb * act, expert_v[ib])

    out = lax.map(block, (hf.reshape(nb, PEER_BLOCK, d),
                          idx.reshape(nb, PEER_BLOCK, PEER_HEADS, PEER_TOPK),
                          gate.reshape(nb, PEER_BLOCK, PEER_HEADS, PEER_TOPK)))
    return out.reshape(b_, s_, d)


def setup_inputs(seed: int = 0) -> dict:
    key = jax.random.key(seed)
    ks = jax.random.split(key, 18)
    f32 = jnp.float32
    nrm = lambda k, shape, scale: jax.random.normal(k, shape, f32) * scale
    gain = lambda k, shape: 1.0 + 0.02 * jax.random.normal(k, shape, f32)
    return {
        "x": jax.random.normal(ks[0], (BATCH, SEQ, D_MODEL), f32),
        "norm_mix_g": gain(ks[1], (DEPTH, D_MODEL)),
        "w_in": nrm(ks[2], (DEPTH, D_MODEL, IN_COLS), D_MODEL ** -0.5),
        "hg_lb_logits": nrm(ks[3], (DEPTH + 1, HG_WIDTH), 0.1),
        "hg_out_norm_g": gain(ks[4], (DEPTH, HG_WIDTH)),
        "conv_w": nrm(ks[5], (DEPTH, CONV_K, CONV_WIDTH), CONV_K ** -0.5),
        "w_branch_hg": nrm(ks[6], (DEPTH, HG_WIDTH, D_MODEL), HG_WIDTH ** -0.5),
        "w_branch_conv": nrm(ks[7], (DEPTH, CONV_WIDTH, D_MODEL), CONV_WIDTH ** -0.5),
        "w_out": nrm(ks[8], (DEPTH, D_MODEL, D_MODEL), D_MODEL ** -0.5),
        "norm_ffn_g": gain(ks[9], (DEPTH, D_MODEL)),
        "peer_w_query": nrm(ks[10], (DEPTH, D_MODEL, PEER_HEADS * PEER_QDIM), D_MODEL ** -0.5),
        "peer_keys1": nrm(ks[11], (DEPTH, PEER_HEADS, PEER_NKEYS, PEER_HALF), PEER_HALF ** -0.5),
        "peer_keys2": nrm(ks[12], (DEPTH, PEER_HEADS, PEER_NKEYS, PEER_HALF), PEER_HALF ** -0.5),
        "peer_u": nrm(ks[13], (DEPTH, PEER_N, D_MODEL), D_MODEL ** -0.5),
        "peer_v": nrm(ks[14], (DEPTH, PEER_N, D_MODEL), PEER_HEADS ** -0.5),
        "norm_final_g": gain(ks[15], (D_MODEL,)),
    }


def reference(x, norm_mix_g, w_in, hg_lb_logits, hg_out_norm_g, conv_w, w_branch_hg,
              w_branch_conv, w_out, norm_ffn_g, peer_w_query, peer_keys1, peer_keys2,
              peer_u, peer_v, norm_final_g):
    b_, s_, _ = x.shape
    lb_all = jnp.cumsum(jax.nn.softmax(hg_lb_logits.astype(jnp.float32), axis=0), axis=0)
    split_at = list(np.cumsum(IN_SPLITS)[:-1])
    for l in range(DEPTH):
        h = rmsnorm(x, norm_mix_g[l])
        proj = h @ w_in[l]
        hq, hf, hi, hg, cb, cc, ch, gates = jnp.split(proj, split_at, axis=-1)

        lb = lb_all[l]
        q = jax.nn.silu(hq.astype(jnp.float32)) * (HG_DK ** -0.5)
        sig = jax.nn.sigmoid(hf.astype(jnp.float32))
        forget = lb + (1.0 - lb) * sig
        k = 1.0 - forget
        log_f = jnp.log(forget)
        rs = lambda t: t.reshape(b_, s_, HG_HEADS, -1)
        o = hgrn2_chunkwise(rs(q), rs(k), rs(log_f), rs(hi))
        o = o * lax.rsqrt(jnp.mean(o * o, axis=-1, keepdims=True) + EPS)
        o = o * hg_out_norm_g[l].astype(jnp.float32).reshape(HG_HEADS, HG_DV)
        y_a = (o.reshape(b_, s_, HG_WIDTH) * jax.nn.silu(hg.astype(jnp.float32))).astype(x.dtype)

        y_b = cb * causal_depthwise_conv(cc * ch, conv_w[l])

        g_a, g_b = jnp.split(jax.nn.sigmoid(gates), 2, axis=-1)
        merged = g_a * (y_a @ w_branch_hg[l]) + g_b * (y_b @ w_branch_conv[l])
        x = x + merged @ w_out[l]

        x = x + peer_ffn(rmsnorm(x, norm_ffn_g[l]), peer_w_query[l], peer_keys1[l],
                         peer_keys2[l], peer_u[l], peer_v[l])
    return rmsnorm(x, norm_final_g)
```

```python
import functools

import jax
import jax.numpy as jnp
from jax import lax
from jax.experimental import pallas as pl
from jax.experimental.pallas import tpu as pltpu

EPS = 1e-6
CHUNK = 64
SUB = 16
HEADS = 8
HEAD_DIM = 64
HG_WIDTH = HEADS * HEAD_DIM
GROUP = 256
N_GROUPS = HG_WIDTH // GROUP
CONV_K = 3
PEER_HEADS = 8
PEER_NKEYS = 128
PEER_HALF = 128
PEER_TOPK = 16
PEER_SLOTS = PEER_HEADS * PEER_TOPK

VMEM_LIMIT_BYTES = 56 * 1024 * 1024

MIX_TILE = 256
ROUTE_TILE = 256
EXPERT_TILE = 128
FINAL_TILE = 512

_f32 = jnp.float32
_bf16 = jnp.bfloat16


def _dot(a, b):
    return jnp.dot(a, b, preferred_element_type=_f32)


def _dot_nt(a, b):
    return lax.dot_general(a, b, (((1,), (1,)), ((), ())), preferred_element_type=_f32)


def _dot_tn(a, b):
    return lax.dot_general(a, b, (((0,), (0,)), ((), ())), preferred_element_type=_f32)


def _split3(x):
    hi = x.astype(_bf16)
    r1 = x - hi.astype(_f32)
    mid = r1.astype(_bf16)
    lo = (r1 - mid.astype(_f32)).astype(_bf16)
    return hi, mid, lo


def _dot_exact_rhs01(x, m01):
    hi, mid, lo = _split3(x)
    return _dot(hi, m01) + _dot(mid, m01) + _dot(lo, m01)


def _dot_exact_lhs01(m01, x):
    hi, mid, lo = _split3(x)
    return _dot(m01, hi) + _dot(m01, mid) + _dot(m01, lo)


def _iota(shape, dim):
    return lax.broadcasted_iota(jnp.int32, shape, dim)


def _hgrn2_chunk(q, k, lf, v, state_ref):
    n_sub = CHUNK // SUB
    row = _iota((CHUNK, CHUNK), 0)
    col = _iota((CHUNK, CHUNK), 1)
    tril = (col <= row).astype(_bf16)
    b = _dot_exact_lhs01(tril, lf)

    b_end = [b[(j + 1) * SUB - 1:(j + 1) * SUB, :] for j in range(n_sub)]
    b_end_rows = jnp.concatenate([jnp.broadcast_to(e, (SUB, HG_WIDTH)) for e in b_end], axis=0)
    b_last = b_end[-1]

    q_in = (q * jnp.exp(b)).astype(_bf16)
    k_sub = (k * jnp.exp(b_end_rows - b)).astype(_bf16)
    k_out = (k * jnp.exp(b_last - b)).astype(_bf16)
    q_from = [(q * jnp.exp(jnp.minimum(b - b_end[j], 0.0))).astype(_bf16) for j in range(n_sub - 1)]
    v_b = v.astype(_bf16)

    gr = _iota((GROUP, GROUP), 0) // HEAD_DIM
    gc = _iota((GROUP, GROUP), 1) // HEAD_DIM
    head_mask = gr == gc
    t_blk = _iota((CHUNK, GROUP), 0) // SUB
    s_blk = (_iota((CHUNK, GROUP), 1) % HEAD_DIM) // SUB

    outs = []
    for g in range(N_GROUPS):
        sl = slice(g * GROUP, (g + 1) * GROUP)
        st = state_ref[g]
        o_g = _dot_nt(q_in[:, sl], st.astype(_bf16))

        zero_b = jnp.zeros((), _bf16)
        k_bd = jnp.where(head_mask, jnp.concatenate([k_sub[:, sl]] * (GROUP // CHUNK), axis=0), zero_b)
        v_bd = jnp.where(head_mask, jnp.concatenate([v_b[:, sl]] * (GROUP // CHUNK), axis=0), zero_b)
        q_stack = jnp.concatenate([qf[:, sl] for qf in q_from], axis=0)
        r = _dot_nt(q_stack, k_bd)
        scores = jnp.zeros((CHUNK, GROUP), _f32)
        for j in range(n_sub - 1):
            sel = (s_blk == j) & (t_blk > j)
            scores = jnp.where(sel, r[j * CHUNK:(j + 1) * CHUNK, :], scores)
        o_g = o_g + _dot(scores.astype(_bf16), v_bd)
        outs.append(o_g)

        upd = _dot_tn(v_b[:, sl], k_out[:, sl])
        decay = jnp.exp(b_last[:, sl])
        state_ref[g] = st * decay + jnp.where(head_mask, upd, 0.0)
    o = jnp.concatenate(outs, axis=1)

    ones_bd = ((_iota((HG_WIDTH, HG_WIDTH), 0) // HEAD_DIM)
               == (_iota((HG_WIDTH, HG_WIDTH), 1) // HEAD_DIM)).astype(_bf16)
    t_in_sub = _iota((CHUNK, HG_WIDTH), 0) % SUB
    for lag in range(SUB):
        if lag == 0:
            p = q * k
            v_l = v
        else:
            valid = t_in_sub >= lag
            k_l = pltpu.roll(k, lag, 0)
            b_l = pltpu.roll(b, lag, 0)
            v_l = pltpu.roll(v, lag, 0)
            p = jnp.where(valid, q * k_l * jnp.exp(jnp.minimum(b - b_l, 0.0)), 0.0)
        s_l = _dot(p.astype(_bf16), ones_bd)
        o = o + s_l * v_l
    return o


def _mix_kernel(x_ref, g_ref, win_ref, lb_ref, hgn_ref, convw_ref, pa_ref, pb_ref, wo_ref,
                out_ref, state_ref, carry_ref, q_s, k_s, lf_s, v_s, o_s):
    ts = x_ref.shape[0]
    d_model = x_ref.shape[1]
    w = HG_WIDTH

    @pl.when(pl.program_id(1) == 0)
    def _():
        state_ref[...] = jnp.zeros_like(state_ref)
        carry_ref[...] = jnp.zeros_like(carry_ref)

    x = x_ref[...]
    h = x * lax.rsqrt(jnp.mean(x * x, axis=-1, keepdims=True) + EPS) * g_ref[...]
    hb = h.astype(_bf16)

    def proj(i, width=w):
        return _dot(hb, win_ref[:, i * w:i * w + width])

    lb = lb_ref[...]
    q_s[...] = jax.nn.silu(proj(0)) * (HEAD_DIM ** -0.5)
    forget = lb + (1.0 - lb) * jax.nn.sigmoid(proj(1))
    k_s[...] = 1.0 - forget
    lf_s[...] = jnp.log(forget)
    v_s[...] = proj(2)

    def chunk_body(c, carry):
        rows = pl.ds(pl.multiple_of(c * CHUNK, CHUNK), CHUNK)
        o_s[rows, :] = _hgrn2_chunk(q_s[rows, :], k_s[rows, :], lf_s[rows, :], v_s[rows, :], state_ref)
        return carry

    lax.fori_loop(0, ts // CHUNK, chunk_body, 0)

    o = o_s[...]
    ones_bd = ((_iota((w, w), 0) // HEAD_DIM) == (_iota((w, w), 1) // HEAD_DIM)).astype(_bf16)
    ms = _dot_exact_rhs01(o * o, ones_bd) * (1.0 / HEAD_DIM)
    o = o * lax.rsqrt(ms + EPS) * hgn_ref[...]
    y_a = (o * jax.nn.silu(proj(3))).astype(_bf16)

    u = proj(5) * proj(6)
    prev = carry_ref[...]
    rowi = _iota((ts, w), 0)
    u1 = jnp.where(rowi >= 1, pltpu.roll(u, 1, 0), jnp.broadcast_to(prev[7:8, :], (ts, w)))
    u2 = jnp.where(rowi >= 2, pltpu.roll(u, 2, 0),
                   jnp.where(rowi == 1, jnp.broadcast_to(prev[7:8, :], (ts, w)),
                             jnp.broadcast_to(prev[6:7, :], (ts, w))))
    carry_ref[...] = u[ts - 8:, :]
    cw = convw_ref[...]
    y_b = (proj(4) * (cw[0:1, :] * u2 + cw[1:2, :] * u1 + cw[2:3, :] * u)).astype(_bf16)

    g_a = jax.nn.sigmoid(proj(7, d_model))
    g_b = jax.nn.sigmoid(_dot(hb, win_ref[:, 7 * w + d_model:7 * w + 2 * d_model]))
    merged = g_a * _dot(y_a, pa_ref[...]) + g_b * _dot(y_b, pb_ref[...])
    out_ref[...] = x + _dot(merged.astype(_bf16), wo_ref[...])


def _const_spec(shape):
    nd = len(shape)
    return pl.BlockSpec(shape, lambda *_: (0,) * nd, pipeline_mode=pl.Buffered(1))


def _mix(x, norm_g, w_in, lb, hg_norm_g, conv_w, w_a, w_b, w_o, *, ts):
    b_, s_, d = x.shape
    in_cols = w_in.shape[1]
    w = HG_WIDTH
    grid = (b_, s_ // ts)
    return pl.pallas_call(
        _mix_kernel,
        grid=grid,
        in_specs=[
            pl.BlockSpec((None, ts, d), lambda b, s: (b, s, 0)),
            _const_spec((1, d)),
            _const_spec((d, in_cols)),
            _const_spec((1, w)),
            _const_spec((1, w)),
            _const_spec((CONV_K, w)),
            _const_spec((w, d)),
            _const_spec((w, d)),
            _const_spec((d, d)),
        ],
        out_specs=pl.BlockSpec((None, ts, d), lambda b, s: (b, s, 0)),
        out_shape=jax.ShapeDtypeStruct((b_, s_, d), _f32),
        scratch_shapes=[
            pltpu.VMEM((N_GROUPS, GROUP, GROUP), _f32),
            pltpu.VMEM((8, w), _f32),
            pltpu.VMEM((ts, w), _f32),
            pltpu.VMEM((ts, w), _f32),
            pltpu.VMEM((ts, w), _f32),
            pltpu.VMEM((ts, w), _f32),
            pltpu.VMEM((ts, w), _f32),
        ],
        compiler_params=pltpu.CompilerParams(
            dimension_semantics=("parallel", "arbitrary"),
            vmem_limit_bytes=VMEM_LIMIT_BYTES),
        name="mix",
    )(x, norm_g, w_in, lb, hg_norm_g, conv_w, w_a, w_b, w_o)


def _stair_pairs():
    pairs = [(a, c) for a in range(PEER_TOPK) for c in range(PEER_TOPK) if (a + 1) * (c + 1) <= PEER_TOPK]
    rows = -(-len(pairs) // 8) * 8
    pairs = pairs + [(-1, -1)] * (rows - len(pairs))
    arr = jnp.asarray(pairs, _f32)
    return arr[:, 0:1], arr[:, 1:2]


def _route_kernel(x_ref, g_ref, wq_ref, k1_ref, k2_ref, ca_ref, cc_ref, h_ref, idx_ref, gate_ref,
                  idx_t, e_t):
    tr = x_ref.shape[0]
    n_cand = ca_ref.shape[0]
    x = x_ref[...]
    h = x * lax.rsqrt(jnp.mean(x * x, axis=-1, keepdims=True) + EPS) * g_ref[...]
    h_ref[...] = h
    hb = h.astype(_bf16)

    key_row = _iota((PEER_NKEYS, tr), 0).astype(_f32)
    cand_row = _iota((n_cand, tr), 0).astype(_f32)
    ca = ca_ref[...]
    cc = cc_ref[...]
    neg_inf = jnp.float32(-jnp.inf)

    def extract_max(s):
        m = jnp.max(s, axis=0, keepdims=True)
        i = jnp.min(jnp.where(s == m, key_row, float(PEER_NKEYS)), axis=0, keepdims=True)
        return m, i, jnp.where(key_row == i, neg_inf, s)

    def head_body(hd, carry):
        q1 = _dot(hb, wq_ref[hd, 0]).astype(_bf16)
        q2 = _dot(hb, wq_ref[hd, 1]).astype(_bf16)
        s1 = _dot_nt(k1_ref[hd], q1)
        s2 = _dot_nt(k2_ref[hd], q2)

        def half_body(k, c):
            s1, s2, a_s, a_i, c_s, c_i = c
            kf = k.astype(_f32)
            m1, i1, s1 = extract_max(s1)
            m2, i2, s2 = extract_max(s2)
            sel_a = ca == kf
            sel_c = cc == kf
            return (s1, s2, jnp.where(sel_a, m1, a_s), jnp.where(sel_a, i1, a_i),
                    jnp.where(sel_c, m2, c_s), jnp.where(sel_c, i2, c_i))

        zeros = jnp.zeros((n_cand, tr), _f32)
        _, _, a_s, a_i, c_s, c_i = lax.fori_loop(
            0, PEER_TOPK, half_body, (s1, s2, jnp.full((n_cand, tr), neg_inf), zeros, zeros, zeros))
        cand_i = a_i * float(PEER_NKEYS) + c_i

        def pick_body(k, c):
            cand_s, denom, m_first = c
            m = jnp.max(cand_s, axis=0, keepdims=True)
            pos = jnp.min(jnp.where(cand_s == m, cand_row, float(n_cand)), axis=0, keepdims=True)
            hit = cand_row == pos
            eid = jnp.max(jnp.where(hit, cand_i, -1.0), axis=0, keepdims=True)
            m_first = jnp.where(k == 0, m, m_first)
            e = jnp.exp(m - m_first)
            slot = hd * PEER_TOPK + k
            idx_t[pl.ds(slot, 1), :] = eid
            e_t[pl.ds(slot, 1), :] = e
            return jnp.where(hit, neg_inf, cand_s), denom + e, m_first

        zero_row = jnp.zeros((1, tr), _f32)
        _, denom, _ = lax.fori_loop(0, PEER_TOPK, pick_body, (a_s + c_s, zero_row, zero_row))
        rows = pl.ds(pl.multiple_of(hd * PEER_TOPK, PEER_TOPK), PEER_TOPK)
        e_t[rows, :] = e_t[rows, :] / denom
        return carry

    lax.fori_loop(0, PEER_HEADS, head_body, 0)
    idx_ref[...] = idx_t[...].T.astype(jnp.int32)
    gate_ref[...] = e_t[...].T


def _route(x, norm_g, w_query, keys1, keys2, *, tr):
    t, d = x.shape
    ca, cc = _stair_pairs()
    ca = jnp.broadcast_to(ca, (ca.shape[0], tr))
    cc = jnp.broadcast_to(cc, (cc.shape[0], tr))
    return pl.pallas_call(
        _route_kernel,
        grid=(t // tr,),
        in_specs=[
            pl.BlockSpec((tr, d), lambda i: (i, 0)),
            _const_spec((1, d)),
            _const_spec(w_query.shape),
            _const_spec(keys1.shape),
            _const_spec(keys2.shape),
            _const_spec(ca.shape),
            _const_spec(cc.shape),
        ],
        out_specs=[
            pl.BlockSpec((tr, d), lambda i: (i, 0)),
            pl.BlockSpec((tr, PEER_SLOTS), lambda i: (i, 0)),
            pl.BlockSpec((tr, PEER_SLOTS), lambda i: (i, 0)),
        ],
        out_shape=[
            jax.ShapeDtypeStruct((t, d), _f32),
            jax.ShapeDtypeStruct((t, PEER_SLOTS), jnp.int32),
            jax.ShapeDtypeStruct((t, PEER_SLOTS), _f32),
        ],
        scratch_shapes=[
            pltpu.VMEM((PEER_SLOTS, tr), _f32),
            pltpu.VMEM((PEER_SLOTS, tr), _f32),
        ],
        compiler_params=pltpu.CompilerParams(
            dimension_semantics=("parallel",),
            vmem_limit_bytes=VMEM_LIMIT_BYTES),
        name="route",
    )(x, norm_g, w_query, keys1, keys2, ca, cc)


PACK_ROWS = 4
HI_MASK = -65536


def _pack_table(tab):
    n, d = tab.shape
    half = d // 2
    bits = lax.bitcast_convert_type(tab.astype(_bf16), jnp.uint16).astype(jnp.uint32)
    word = bits[:, :half] | (bits[:, half:] << 16)
    return lax.bitcast_convert_type(word, jnp.int32).reshape(n, PACK_ROWS, half // PACK_ROWS)


def _unpack_words(wv):
    lo = pltpu.bitcast(wv << 16, _f32)
    hi = pltpu.bitcast(wv & HI_MASK, _f32)
    return lo, hi


def _udot_kernel(idx_ref, tab_ref, h_ref, gate_ref, w_ref, q_ref, acts_ref):
    tb = h_ref.shape[0]
    ones8 = jnp.ones((8, 128), _bf16)

    def token_body(t, carry):
        hv = h_ref[t]
        h_lo = hv[0:PACK_ROWS]
        h_hi = hv[PACK_ROWS:]
        for k in range(PEER_SLOTS):
            lo, hi = _unpack_words(tab_ref[idx_ref[t, k]])
            p = lo * h_lo + hi * h_hi
            q_ref[k:k + 1, :] = jnp.sum(p, axis=0, keepdims=True)
        q_hi, q_mid, q_lo = _split3(q_ref[...])
        r = _dot_nt(ones8, q_hi) + _dot_nt(ones8, q_mid) + _dot_nt(ones8, q_lo)
        acts_ref[pl.ds(t, 1), :] = r[0:1, :]
        return carry

    lax.fori_loop(0, tb, token_body, 0)
    a = acts_ref[...]
    gelu = 0.5 * a * (1.0 + lax.erf(a * (2.0 ** -0.5)))
    w_ref[...] = gate_ref[...] * gelu


def _udot(idx, tab, h3, gate, *, tb):
    t = idx.shape[0]
    return pl.pallas_call(
        _udot_kernel,
        grid=(t // tb,),
        in_specs=[
            pl.BlockSpec((tb, PEER_SLOTS), lambda i: (i, 0), memory_space=pltpu.SMEM),
            _const_spec(tab.shape),
            pl.BlockSpec((tb, 8, 128), lambda i: (i, 0, 0)),
            pl.BlockSpec((tb, PEER_SLOTS), lambda i: (i, 0)),
        ],
        out_specs=pl.BlockSpec((tb, PEER_SLOTS), lambda i: (i, 0)),
        out_shape=jax.ShapeDtypeStruct((t, PEER_SLOTS), _f32),
        scratch_shapes=[
            pltpu.VMEM((PEER_SLOTS, 128), _f32),
            pltpu.VMEM((tb, PEER_SLOTS), _f32),
        ],
        compiler_params=pltpu.CompilerParams(
            dimension_semantics=("parallel",),
            vmem_limit_bytes=VMEM_LIMIT_BYTES),
        name="udot",
    )(idx, tab, h3, gate)


N_ACC = 4


def _vaxpy_kernel(idx_ref, w_ref, tab_ref, out_ref):
    tb = out_ref.shape[0]

    def token_body(t, carry):
        acc_lo = [jnp.zeros((PACK_ROWS, 128), _f32) for _ in range(N_ACC)]
        acc_hi = [jnp.zeros((PACK_ROWS, 128), _f32) for _ in range(N_ACC)]
        for k in range(PEER_SLOTS):
            lo, hi = _unpack_words(tab_ref[idx_ref[t, k]])
            w = w_ref[t, k]
            acc_lo[k % N_ACC] = acc_lo[k % N_ACC] + w * lo
            acc_hi[k % N_ACC] = acc_hi[k % N_ACC] + w * hi
        out_ref[t, 0:PACK_ROWS, :] = functools.reduce(lambda a, b: a + b, acc_lo)
        out_ref[t, PACK_ROWS:, :] = functools.reduce(lambda a, b: a + b, acc_hi)
        return carry

    lax.fori_loop(0, tb, token_body, 0)


def _vaxpy(idx, w, tab, *, tb):
    t = idx.shape[0]
    return pl.pallas_call(
        _vaxpy_kernel,
        grid=(t // tb,),
        in_specs=[
            pl.BlockSpec((tb, PEER_SLOTS), lambda i: (i, 0), memory_space=pltpu.SMEM),
            pl.BlockSpec((tb, PEER_SLOTS), lambda i: (i, 0), memory_space=pltpu.SMEM),
            _const_spec(tab.shape),
        ],
        out_specs=pl.BlockSpec((tb, 2 * PACK_ROWS, 128), lambda i: (i, 0, 0)),
        out_shape=jax.ShapeDtypeStruct((t, 2 * PACK_ROWS, 128), _f32),
        compiler_params=pltpu.CompilerParams(
            dimension_semantics=("parallel",),
            vmem_limit_bytes=VMEM_LIMIT_BYTES),
        name="vaxpy",
    )(idx, w, tab)


def _final_kernel(x_ref, p_ref, g_ref, out_ref, *, normalize):
    x = x_ref[...] + p_ref[...]
    if normalize:
        x = x * lax.rsqrt(jnp.mean(x * x, axis=-1, keepdims=True) + EPS) * g_ref[...]
    out_ref[...] = x


def _final(x, peer, g, *, normalize, tf):
    t, d = x.shape
    return pl.pallas_call(
        functools.partial(_final_kernel, normalize=normalize),
        grid=(t // tf,),
        in_specs=[
            pl.BlockSpec((tf, d), lambda i: (i, 0)),
            pl.BlockSpec((tf, d), lambda i: (i, 0)),
            _const_spec((1, d)),
        ],
        out_specs=pl.BlockSpec((tf, d), lambda i: (i, 0)),
        out_shape=jax.ShapeDtypeStruct((t, d), _f32),
        compiler_params=pltpu.CompilerParams(dimension_semantics=("parallel",)),
        name="final_norm",
    )(x, peer, g)


def kernel(x, norm_mix_g, w_in, hg_lb_logits, hg_out_norm_g, conv_w, w_branch_hg, w_branch_conv, w_out, norm_ffn_g, peer_w_query, peer_keys1, peer_keys2, peer_u, peer_v, norm_final_g):
    b_, s_, d = x.shape
    t = b_ * s_
    depth = w_in.shape[0]
    lb_all = jnp.cumsum(jax.nn.softmax(hg_lb_logits.astype(_f32), axis=0), axis=0)
    for l in range(depth):
        x = _mix(x, norm_mix_g[l][None], w_in[l].astype(_bf16), lb_all[l][None], hg_out_norm_g[l][None],
                 conv_w[l], w_branch_hg[l].astype(_bf16), w_branch_conv[l].astype(_bf16),
                 w_out[l].astype(_bf16), ts=MIX_TILE)
        xf = x.reshape(t, d)
        wq = peer_w_query[l].astype(_bf16).reshape(d, PEER_HEADS, 2, PEER_HALF).transpose(1, 2, 0, 3)
        h, idx, gate = _route(xf, norm_ffn_g[l][None], wq,
                              peer_keys1[l].astype(_bf16), peer_keys2[l].astype(_bf16), tr=ROUTE_TILE)
        w = _udot(idx, _pack_table(peer_u[l]), h.reshape(t, 2 * PACK_ROWS, 128), gate, tb=EXPERT_TILE)
        peer = _vaxpy(idx, w, _pack_table(peer_v[l]), tb=EXPERT_TILE).reshape(t, d)
        last = l == depth - 1
        g = norm_final_g[None] if last else jnp.ones((1, d), _f32)
        x = _final(xf, peer, g, normalize=last, tf=FINAL_TILE).reshape(b_, s_, d)
    return x
```

```python
import functools

import jax
import jax.numpy as jnp
from jax import lax
from jax.experimental import pallas as pl
from jax.experimental.pallas import tpu as pltpu

EPS = 1e-6
CHUNK = 64
SUB = 16
HEADS = 8
HEAD_DIM = 64
HG_WIDTH = HEADS * HEAD_DIM
GROUP = 256
N_GROUPS = HG_WIDTH // GROUP
CONV_K = 3
PEER_HEADS = 8
PEER_NKEYS = 128
PEER_HALF = 128
PEER_TOPK = 16
PEER_SLOTS = PEER_HEADS * PEER_TOPK

VMEM_LIMIT_BYTES = 56 * 1024 * 1024

MIX_TILE = 256
ROUTE_TILE = 256
EXPERT_TILE = 128
FINAL_TILE = 512

_f32 = jnp.float32
_bf16 = jnp.bfloat16


def _dot(a, b):
    return jnp.dot(a, b, preferred_element_type=_f32)


def _dot_nt(a, b):
    return lax.dot_general(a, b, (((1,), (1,)), ((), ())), preferred_element_type=_f32)


def _dot_tn(a, b):
    return lax.dot_general(a, b, (((0,), (0,)), ((), ())), preferred_element_type=_f32)


def _split3(x):
    hi = x.astype(_bf16)
    r1 = x - hi.astype(_f32)
    mid = r1.astype(_bf16)
    lo = (r1 - mid.astype(_f32)).astype(_bf16)
    return hi, mid, lo


def _dot_exact_rhs01(x, m01):
    hi, mid, lo = _split3(x)
    return _dot(hi, m01) + _dot(mid, m01) + _dot(lo, m01)


def _dot_exact_lhs01(m01, x):
    hi, mid, lo = _split3(x)
    return _dot(m01, hi) + _dot(m01, mid) + _dot(m01, lo)


def _iota(shape, dim):
    return lax.broadcasted_iota(jnp.int32, shape, dim)


def _hgrn2_chunk(q, k, lf, v, state_ref):
    n_sub = CHUNK // SUB
    row = _iota((CHUNK, CHUNK), 0)
    col = _iota((CHUNK, CHUNK), 1)
    tril = (col <= row).astype(_bf16)
    b = _dot_exact_lhs01(tril, lf)

    b_end = [b[(j + 1) * SUB - 1:(j + 1) * SUB, :] for j in range(n_sub)]
    b_end_rows = jnp.concatenate([jnp.broadcast_to(e, (SUB, HG_WIDTH)) for e in b_end], axis=0)
    b_last = b_end[-1]

    q_in = (q * jnp.exp(b)).astype(_bf16)
    k_sub = (k * jnp.exp(b_end_rows - b)).astype(_bf16)
    k_out = (k * jnp.exp(b_last - b)).astype(_bf16)
    q_from = [(q * jnp.exp(jnp.minimum(b - b_end[j], 0.0))).astype(_bf16) for j in range(n_sub - 1)]
    v_b = v.astype(_bf16)

    gr = _iota((GROUP, GROUP), 0) // HEAD_DIM
    gc = _iota((GROUP, GROUP), 1) // HEAD_DIM
    head_mask = gr == gc
    t_blk = _iota((CHUNK, GROUP), 0) // SUB
    s_blk = (_iota((CHUNK, GROUP), 1) % HEAD_DIM) // SUB

    outs = []
    for g in range(N_GROUPS):
        sl = slice(g * GROUP, (g + 1) * GROUP)
        st = state_ref[g]
        o_g = _dot_nt(q_in[:, sl], st.astype(_bf16))

        zero_b = jnp.zeros((), _bf16)
        k_bd = jnp.where(head_mask, jnp.concatenate([k_sub[:, sl]] * (GROUP // CHUNK), axis=0), zero_b)
        v_bd = jnp.where(head_mask, jnp.concatenate([v_b[:, sl]] * (GROUP // CHUNK), axis=0), zero_b)
        q_stack = jnp.concatenate([qf[:, sl] for qf in q_from], axis=0)
        r = _dot_nt(q_stack, k_bd)
        scores = jnp.zeros((CHUNK, GROUP), _f32)
        for j in range(n_sub - 1):
            sel = (s_blk == j) & (t_blk > j)
            scores = jnp.where(sel, r[j * CHUNK:(j + 1) * CHUNK, :], scores)
        o_g = o_g + _dot(scores.astype(_bf16), v_bd)
        outs.append(o_g)

        upd = _dot_tn(v_b[:, sl], k_out[:, sl])
        decay = jnp.exp(b_last[:, sl])
        state_ref[g] = st * decay + jnp.where(head_mask, upd, 0.0)
    o = jnp.concatenate(outs, axis=1)

    ones_bd = ((_iota((HG_WIDTH, HG_WIDTH), 0) // HEAD_DIM)
               == (_iota((HG_WIDTH, HG_WIDTH), 1) // HEAD_DIM)).astype(_bf16)
    t_in_sub = _iota((CHUNK, HG_WIDTH), 0) % SUB
    for lag in range(SUB):
        if lag == 0:
            p = q * k
            v_l = v
        else:
            valid = t_in_sub >= lag
            k_l = pltpu.roll(k, lag, 0)
            b_l = pltpu.roll(b, lag, 0)
            v_l = pltpu.roll(v, lag, 0)
            p = jnp.where(valid, q * k_l * jnp.exp(jnp.minimum(b - b_l, 0.0)), 0.0)
        s_l = _dot(p.astype(_bf16), ones_bd)
        o = o + s_l * v_l
    return o


def _mix_kernel(x_ref, g_ref, win_ref, lb_ref, hgn_ref, convw_ref, pa_ref, pb_ref, wo_ref,
                out_ref, state_ref, carry_ref, q_s, k_s, lf_s, v_s, o_s):
    ts = x_ref.shape[0]
    d_model = x_ref.shape[1]
    w = HG_WIDTH

    @pl.when(pl.program_id(1) == 0)
    def _():
        state_ref[...] = jnp.zeros_like(state_ref)
        carry_ref[...] = jnp.zeros_like(carry_ref)

    x = x_ref[...]
    h = x * lax.rsqrt(jnp.mean(x * x, axis=-1, keepdims=True) + EPS) * g_ref[...]
    hb = h.astype(_bf16)

    def proj(i, width=w):
        return _dot(hb, win_ref[:, i * w:i * w + width])

    lb = lb_ref[...]
    q_s[...] = jax.nn.silu(proj(0)) * (HEAD_DIM ** -0.5)
    forget = lb + (1.0 - lb) * jax.nn.sigmoid(proj(1))
    k_s[...] = 1.0 - forget
    lf_s[...] = jnp.log(forget)
    v_s[...] = proj(2)

    def chunk_body(c, carry):
        rows = pl.ds(pl.multiple_of(c * CHUNK, CHUNK), CHUNK)
        o_s[rows, :] = _hgrn2_chunk(q_s[rows, :], k_s[rows, :], lf_s[rows, :], v_s[rows, :], state_ref)
        return carry

    lax.fori_loop(0, ts // CHUNK, chunk_body, 0)

    o = o_s[...]
    ones_bd = ((_iota((w, w), 0) // HEAD_DIM) == (_iota((w, w), 1) // HEAD_DIM)).astype(_bf16)
    ms = _dot_exact_rhs01(o * o, ones_bd) * (1.0 / HEAD_DIM)
    o = o * lax.rsqrt(ms + EPS) * hgn_ref[...]
    y_a = (o * jax.nn.silu(proj(3))).astype(_bf16)

    u = proj(5) * proj(6)
    prev = carry_ref[...]
    rowi = _iota((ts, w), 0)
    u1 = jnp.where(rowi >= 1, pltpu.roll(u, 1, 0), jnp.broadcast_to(prev[7:8, :], (ts, w)))
    u2 = jnp.where(rowi >= 2, pltpu.roll(u, 2, 0),
                   jnp.where(rowi == 1, jnp.broadcast_to(prev[7:8, :], (ts, w)),
                             jnp.broadcast_to(prev[6:7, :], (ts, w))))
    carry_ref[...] = u[ts - 8:, :]
    cw = convw_ref[...]
    y_b = (proj(4) * (cw[0:1, :] * u2 + cw[1:2, :] * u1 + cw[2:3, :] * u)).astype(_bf16)

    g_a = jax.nn.sigmoid(proj(7, d_model))
    g_b = jax.nn.sigmoid(_dot(hb, win_ref[:, 7 * w + d_model:7 * w + 2 * d_model]))
    merged = g_a * _dot(y_a, pa_ref[...]) + g_b * _dot(y_b, pb_ref[...])
    out_ref[...] = x + _dot(merged.astype(_bf16), wo_ref[...])


def _const_spec(shape):
    nd = len(shape)
    return pl.BlockSpec(shape, lambda *_: (0,) * nd, pipeline_mode=pl.Buffered(1))


def _mix(x, norm_g, w_in, lb, hg_norm_g, conv_w, w_a, w_b, w_o, *, ts):
    b_, s_, d = x.shape
    in_cols = w_in.shape[1]
    w = HG_WIDTH
    grid = (b_, s_ // ts)
    return pl.pallas_call(
        _mix_kernel,
        grid=grid,
        in_specs=[
            pl.BlockSpec((None, ts, d), lambda b, s: (b, s, 0)),
            _const_spec((1, d)),
            _const_spec((d, in_cols)),
            _const_spec((1, w)),
            _const_spec((1, w)),
            _const_spec((CONV_K, w)),
            _const_spec((w, d)),
            _const_spec((w, d)),
            _const_spec((d, d)),
        ],
        out_specs=pl.BlockSpec((None, ts, d), lambda b, s: (b, s, 0)),
        out_shape=jax.ShapeDtypeStruct((b_, s_, d), _f32),
        scratch_shapes=[
            pltpu.VMEM((N_GROUPS, GROUP, GROUP), _f32),
            pltpu.VMEM((8, w), _f32),
            pltpu.VMEM((ts, w), _f32),
            pltpu.VMEM((ts, w), _f32),
            pltpu.VMEM((ts, w), _f32),
            pltpu.VMEM((ts, w), _f32),
            pltpu.VMEM((ts, w), _f32),
        ],
        compiler_params=pltpu.CompilerParams(
            dimension_semantics=("parallel", "arbitrary"),
            vmem_limit_bytes=VMEM_LIMIT_BYTES),
        name="mix",
    )(x, norm_g, w_in, lb, hg_norm_g, conv_w, w_a, w_b, w_o)


def _stair_pairs():
    pairs = [(a, c) for a in range(PEER_TOPK) for c in range(PEER_TOPK) if (a + 1) * (c + 1) <= PEER_TOPK]
    rows = -(-len(pairs) // 8) * 8
    pairs = pairs + [(-1, -1)] * (rows - len(pairs))
    arr = jnp.asarray(pairs, _f32)
    return arr[:, 0:1], arr[:, 1:2]


def _route_kernel(x_ref, g_ref, wq_ref, k1_ref, k2_ref, ca_ref, cc_ref, h_ref, idx_ref, gate_ref,
                  idx_t, e_t):
    tr = x_ref.shape[0]
    n_cand = ca_ref.shape[0]
    x = x_ref[...]
    h = x * lax.rsqrt(jnp.mean(x * x, axis=-1, keepdims=True) + EPS) * g_ref[...]
    h_ref[...] = h
    hb = h.astype(_bf16)

    key_row = _iota((PEER_NKEYS, tr), 0).astype(_f32)
    cand_row = _iota((n_cand, tr), 0).astype(_f32)
    ca = ca_ref[...]
    cc = cc_ref[...]
    neg_inf = jnp.float32(-jnp.inf)

    def extract_max(s):
        m = jnp.max(s, axis=0, keepdims=True)
        i = jnp.min(jnp.where(s == m, key_row, float(PEER_NKEYS)), axis=0, keepdims=True)
        return m, i, jnp.where(key_row == i, neg_inf, s)

    def head_body(hd, carry):
        q1 = _dot(hb, wq_ref[hd, 0]).astype(_bf16)
        q2 = _dot(hb, wq_ref[hd, 1]).astype(_bf16)
        s1 = _dot_nt(k1_ref[hd], q1)
        s2 = _dot_nt(k2_ref[hd], q2)

        def half_body(k, c):
            s1, s2, a_s, a_i, c_s, c_i = c
            kf = k.astype(_f32)
            m1, i1, s1 = extract_max(s1)
            m2, i2, s2 = extract_max(s2)
            sel_a = ca == kf
            sel_c = cc == kf
            return (s1, s2, jnp.where(sel_a, m1, a_s), jnp.where(sel_a, i1, a_i),
                    jnp.where(sel_c, m2, c_s), jnp.where(sel_c, i2, c_i))

        zeros = jnp.zeros((n_cand, tr), _f32)
        _, _, a_s, a_i, c_s, c_i = lax.fori_loop(
            0, PEER_TOPK, half_body, (s1, s2, jnp.full((n_cand, tr), neg_inf), zeros, zeros, zeros))
        cand_i = a_i * float(PEER_NKEYS) + c_i

        def pick_body(k, c):
            cand_s, denom, m_first = c
            m = jnp.max(cand_s, axis=0, keepdims=True)
            pos = jnp.min(jnp.where(cand_s == m, cand_row, float(n_cand)), axis=0, keepdims=True)
            hit = cand_row == pos
            eid = jnp.max(jnp.where(hit, cand_i, -1.0), axis=0, keepdims=True)
            m_first = jnp.where(k == 0, m, m_first)
            e = jnp.exp(m - m_first)
            slot = hd * PEER_TOPK + k
            idx_t[pl.ds(slot, 1), :] = eid
            e_t[pl.ds(slot, 1), :] = e
            return jnp.where(hit, neg_inf, cand_s), denom + e, m_first

        zero_row = jnp.zeros((1, tr), _f32)
        _, denom, _ = lax.fori_loop(0, PEER_TOPK, pick_body, (a_s + c_s, zero_row, zero_row))
        rows = pl.ds(pl.multiple_of(hd * PEER_TOPK, PEER_TOPK), PEER_TOPK)
        e_t[rows, :] = e_t[rows, :] / denom
        return carry

    lax.fori_loop(0, PEER_HEADS, head_body, 0)
    idx_ref[...] = idx_t[...].T.astype(jnp.int32)
    gate_ref[...] = e_t[...].T


def _route(x, norm_g, w_query, keys1, keys2, *, tr):
    t, d = x.shape
    ca, cc = _stair_pairs()
    ca = jnp.broadcast_to(ca, (ca.shape[0], tr))
    cc = jnp.broadcast_to(cc, (cc.shape[0], tr))
    return pl.pallas_call(
        _route_kernel,
        grid=(t // tr,),
        in_specs=[
            pl.BlockSpec((tr, d), lambda i: (i, 0)),
            _const_spec((1, d)),
            _const_spec(w_query.shape),
            _const_spec(keys1.shape),
            _const_spec(keys2.shape),
            _const_spec(ca.shape),
            _const_spec(cc.shape),
        ],
        out_specs=[
            pl.BlockSpec((tr, d), lambda i: (i, 0)),
            pl.BlockSpec((tr, PEER_SLOTS), lambda i: (i, 0)),
            pl.BlockSpec((tr, PEER_SLOTS), lambda i: (i, 0)),
        ],
        out_shape=[
            jax.ShapeDtypeStruct((t, d), _f32),
            jax.ShapeDtypeStruct((t, PEER_SLOTS), jnp.int32),
            jax.ShapeDtypeStruct((t, PEER_SLOTS), _f32),
        ],
        scratch_shapes=[
            pltpu.VMEM((PEER_SLOTS, tr), _f32),
            pltpu.VMEM((PEER_SLOTS, tr), _f32),
        ],
        compiler_params=pltpu.CompilerParams(
            dimension_semantics=("parallel",),
            vmem_limit_bytes=VMEM_LIMIT_BYTES),
        name="route",
    )(x, norm_g, w_query, keys1, keys2, ca, cc)


ROW_CHUNKS = 8
PACK_ROWS = ROW_CHUNKS // 2


def _pack_table(tab):
    n, d = tab.shape
    bits = lax.bitcast_convert_type(tab.astype(_bf16), jnp.uint16).astype(jnp.uint32)
    bits = bits.reshape(n, PACK_ROWS, 2, d // ROW_CHUNKS)
    word = bits[:, :, 0, :] | (bits[:, :, 1, :] << 16)
    return lax.bitcast_convert_type(word, jnp.int32)


def _split2(x):
    hi = x.astype(_bf16)
    lo = (x - hi.astype(_f32)).astype(_bf16)
    return hi, lo


def _gather_rows(idx_ref, t, tab_ref, rows_ref):
    for k in range(PEER_SLOTS):
        rows_ref[k * PACK_ROWS:(k + 1) * PACK_ROWS, :] = tab_ref[idx_ref[t, k]]


def _rows_matrix(rows_ref):
    return pltpu.bitcast(rows_ref[...], _bf16)


def _token_loop(tb, idx_ref, tab_ref, rows_a, rows_b, compute):
    _gather_rows(idx_ref, 0, tab_ref, rows_a)

    def pair_body(i, carry):
        t0 = 2 * i
        _gather_rows(idx_ref, t0 + 1, tab_ref, rows_b)
        compute(t0, _rows_matrix(rows_a))
        _gather_rows(idx_ref, jnp.minimum(t0 + 2, tb - 1), tab_ref, rows_a)
        compute(t0 + 1, _rows_matrix(rows_b))
        return carry

    lax.fori_loop(0, tb // 2, pair_body, 0)


def _chunk_diag_mask():
    shape = (ROW_CHUNKS, PEER_SLOTS * ROW_CHUNKS)
    return (_iota(shape, 1) % ROW_CHUNKS) == _iota(shape, 0)


def _udot_kernel(idx_ref, tab_ref, h_ref, gate_ref, w_ref, rows_a, rows_b, part_ref):
    tb = h_ref.shape[0]
    diag = _chunk_diag_mask()

    def compute(t, m):
        h_hi, h_lo = _split2(h_ref[t])
        r = _dot_nt(h_hi, m) + _dot_nt(h_lo, m)
        part_ref[pl.ds(t, 1), :] = jnp.sum(jnp.where(diag, r, 0.0), axis=0, keepdims=True)

    _token_loop(tb, idx_ref, tab_ref, rows_a, rows_b, compute)
    shape = (PEER_SLOTS * ROW_CHUNKS, PEER_SLOTS)
    fold = ((_iota(shape, 0) // ROW_CHUNKS) == _iota(shape, 1)).astype(_bf16)
    a = _dot_exact_rhs01(part_ref[...], fold)
    gelu = 0.5 * a * (1.0 + lax.erf(a * (2.0 ** -0.5)))
    w_ref[...] = gate_ref[...] * gelu


def _udot(idx, tab, h3, gate, *, tb):
    t = idx.shape[0]
    return pl.pallas_call(
        _udot_kernel,
        grid=(t // tb,),
        in_specs=[
            pl.BlockSpec((tb, PEER_SLOTS), lambda i: (i, 0), memory_space=pltpu.SMEM),
            _const_spec(tab.shape),
            pl.BlockSpec((tb, 8, 128), lambda i: (i, 0, 0)),
            pl.BlockSpec((tb, PEER_SLOTS), lambda i: (i, 0)),
        ],
        out_specs=pl.BlockSpec((tb, PEER_SLOTS), lambda i: (i, 0)),
        out_shape=jax.ShapeDtypeStruct((t, PEER_SLOTS), _f32),
        scratch_shapes=[
            pltpu.VMEM((PEER_SLOTS * PACK_ROWS, 128), jnp.int32),
            pltpu.VMEM((PEER_SLOTS * PACK_ROWS, 128), jnp.int32),
            pltpu.VMEM((tb, PEER_SLOTS * ROW_CHUNKS), _f32),
        ],
        compiler_params=pltpu.CompilerParams(
            dimension_semantics=("parallel",),
            vmem_limit_bytes=VMEM_LIMIT_BYTES),
        name="udot",
    )(idx, tab, h3, gate)


def _vaxpy_kernel(idx_ref, w_ref, tab_ref, out_ref, rows_a, rows_b, wrep_ref):
    tb = out_ref.shape[0]
    diag = _chunk_diag_mask()
    shape = (PEER_SLOTS, PEER_SLOTS * ROW_CHUNKS)
    spread = (_iota(shape, 0) == (_iota(shape, 1) // ROW_CHUNKS)).astype(_bf16)
    wrep_ref[...] = _dot_exact_rhs01(w_ref[...], spread)

    def compute(t, m):
        w_row = jnp.broadcast_to(wrep_ref[pl.ds(t, 1), :], diag.shape)
        w_hi, w_lo = _split2(jnp.where(diag, w_row, 0.0))
        out_ref[t] = _dot(w_hi, m) + _dot(w_lo, m)

    _token_loop(tb, idx_ref, tab_ref, rows_a, rows_b, compute)


def _vaxpy(idx, w, tab, *, tb):
    t = idx.shape[0]
    return pl.pallas_call(
        _vaxpy_kernel,
        grid=(t // tb,),
        in_specs=[
            pl.BlockSpec((tb, PEER_SLOTS), lambda i: (i, 0), memory_space=pltpu.SMEM),
            pl.BlockSpec((tb, PEER_SLOTS), lambda i: (i, 0)),
            _const_spec(tab.shape),
        ],
        out_specs=pl.BlockSpec((tb, ROW_CHUNKS, 128), lambda i: (i, 0, 0)),
        out_shape=jax.ShapeDtypeStruct((t, ROW_CHUNKS, 128), _f32),
        scratch_shapes=[
            pltpu.VMEM((PEER_SLOTS * PACK_ROWS, 128), jnp.int32),
            pltpu.VMEM((PEER_SLOTS * PACK_ROWS, 128), jnp.int32),
            pltpu.VMEM((tb, PEER_SLOTS * ROW_CHUNKS), _f32),
        ],
        compiler_params=pltpu.CompilerParams(
            dimension_semantics=("parallel",),
            vmem_limit_bytes=VMEM_LIMIT_BYTES),
        name="vaxpy",
    )(idx, w, tab)


def _final_kernel(x_ref, p_ref, g_ref, out_ref, *, normalize):
    x = x_ref[...] + p_ref[...]
    if normalize:
        x = x * lax.rsqrt(jnp.mean(x * x, axis=-1, keepdims=True) + EPS) * g_ref[...]
    out_ref[...] = x


def _final(x, peer, g, *, normalize, tf):
    t, d = x.shape
    return pl.pallas_call(
        functools.partial(_final_kernel, normalize=normalize),
        grid=(t // tf,),
        in_specs=[
            pl.BlockSpec((tf, d), lambda i: (i, 0)),
            pl.BlockSpec((tf, d), lambda i: (i, 0)),
            _const_spec((1, d)),
        ],
        out_specs=pl.BlockSpec((tf, d), lambda i: (i, 0)),
        out_shape=jax.ShapeDtypeStruct((t, d), _f32),
        compiler_params=pltpu.CompilerParams(dimension_semantics=("parallel",)),
        name="final_norm",
    )(x, peer, g)


def kernel(x, norm_mix_g, w_in, hg_lb_logits, hg_out_norm_g, conv_w, w_branch_hg, w_branch_conv, w_out, norm_ffn_g, peer_w_query, peer_keys1, peer_keys2, peer_u, peer_v, norm_final_g):
    b_, s_, d = x.shape
    t = b_ * s_
    depth = w_in.shape[0]
    lb_all = jnp.cumsum(jax.nn.softmax(hg_lb_logits.astype(_f32), axis=0), axis=0)
    for l in range(depth):
        x = _mix(x, norm_mix_g[l][None], w_in[l].astype(_bf16), lb_all[l][None], hg_out_norm_g[l][None],
                 conv_w[l], w_branch_hg[l].astype(_bf16), w_branch_conv[l].astype(_bf16),
                 w_out[l].astype(_bf16), ts=MIX_TILE)
        xf = x.reshape(t, d)
        wq = peer_w_query[l].astype(_bf16).reshape(d, PEER_HEADS, 2, PEER_HALF).transpose(1, 2, 0, 3)
        h, idx, gate = _route(xf, norm_ffn_g[l][None], wq,
                              peer_keys1[l].astype(_bf16), peer_keys2[l].astype(_bf16), tr=ROUTE_TILE)
        w = _udot(idx, _pack_table(peer_u[l]), h.reshape(t, ROW_CHUNKS, 128), gate, tb=EXPERT_TILE)
        peer = _vaxpy(idx, w, _pack_table(peer_v[l]), tb=EXPERT_TILE).reshape(t, d)
        last = l == depth - 1
        g = norm_final_g[None] if last else jnp.ones((1, d), _f32)
        x = _final(xf, peer, g, normalize=last, tf=FINAL_TILE).reshape(b_, s_, d)
    return x
```

```python
import functools

import jax
import jax.numpy as jnp
from jax import lax
from jax.experimental import pallas as pl
from jax.experimental.pallas import tpu as pltpu
from jax.experimental.pallas import tpu_sc as plsc

EPS = 1e-6
CHUNK = 64
SUB = 16
HEADS = 8
HEAD_DIM = 64
HG_WIDTH = HEADS * HEAD_DIM
GROUP = 256
N_GROUPS = HG_WIDTH // GROUP
CONV_K = 3
PEER_HEADS = 8
PEER_NKEYS = 128
PEER_HALF = 128
PEER_TOPK = 16
PEER_SLOTS = PEER_HEADS * PEER_TOPK

VMEM_LIMIT_BYTES = 56 * 1024 * 1024

MIX_TILE = 256
ROUTE_TILE = 256
EXPERT_TILE = 128
FINAL_TILE = 512

_f32 = jnp.float32
_bf16 = jnp.bfloat16


def _dot(a, b):
    return jnp.dot(a, b, preferred_element_type=_f32)


def _dot_nt(a, b):
    return lax.dot_general(a, b, (((1,), (1,)), ((), ())), preferred_element_type=_f32)


def _dot_tn(a, b):
    return lax.dot_general(a, b, (((0,), (0,)), ((), ())), preferred_element_type=_f32)


def _split3(x):
    hi = x.astype(_bf16)
    r1 = x - hi.astype(_f32)
    mid = r1.astype(_bf16)
    lo = (r1 - mid.astype(_f32)).astype(_bf16)
    return hi, mid, lo


def _dot_exact_rhs01(x, m01):
    hi, mid, lo = _split3(x)
    return _dot(hi, m01) + _dot(mid, m01) + _dot(lo, m01)


def _dot_exact_lhs01(m01, x):
    hi, mid, lo = _split3(x)
    return _dot(m01, hi) + _dot(m01, mid) + _dot(m01, lo)


def _iota(shape, dim):
    return lax.broadcasted_iota(jnp.int32, shape, dim)


def _hgrn2_chunk(q, k, lf, v, state_ref):
    n_sub = CHUNK // SUB
    row = _iota((CHUNK, CHUNK), 0)
    col = _iota((CHUNK, CHUNK), 1)
    tril = (col <= row).astype(_bf16)
    b = _dot_exact_lhs01(tril, lf)

    b_end = [b[(j + 1) * SUB - 1:(j + 1) * SUB, :] for j in range(n_sub)]
    b_end_rows = jnp.concatenate([jnp.broadcast_to(e, (SUB, HG_WIDTH)) for e in b_end], axis=0)
    b_last = b_end[-1]

    q_in = (q * jnp.exp(b)).astype(_bf16)
    k_sub = (k * jnp.exp(b_end_rows - b)).astype(_bf16)
    k_out = (k * jnp.exp(b_last - b)).astype(_bf16)
    q_from = [(q * jnp.exp(jnp.minimum(b - b_end[j], 0.0))).astype(_bf16) for j in range(n_sub - 1)]
    v_b = v.astype(_bf16)

    gr = _iota((GROUP, GROUP), 0) // HEAD_DIM
    gc = _iota((GROUP, GROUP), 1) // HEAD_DIM
    head_mask = gr == gc
    t_blk = _iota((CHUNK, GROUP), 0) // SUB
    s_blk = (_iota((CHUNK, GROUP), 1) % HEAD_DIM) // SUB

    outs = []
    for g in range(N_GROUPS):
        sl = slice(g * GROUP, (g + 1) * GROUP)
        st = state_ref[g]
        o_g = _dot_nt(q_in[:, sl], st.astype(_bf16))

        zero_b = jnp.zeros((), _bf16)
        k_bd = jnp.where(head_mask, jnp.concatenate([k_sub[:, sl]] * (GROUP // CHUNK), axis=0), zero_b)
        v_bd = jnp.where(head_mask, jnp.concatenate([v_b[:, sl]] * (GROUP // CHUNK), axis=0), zero_b)
        q_stack = jnp.concatenate([qf[:, sl] for qf in q_from], axis=0)
        r = _dot_nt(q_stack, k_bd)
        scores = jnp.zeros((CHUNK, GROUP), _f32)
        for j in range(n_sub - 1):
            sel = (s_blk == j) & (t_blk > j)
            scores = jnp.where(sel, r[j * CHUNK:(j + 1) * CHUNK, :], scores)
        o_g = o_g + _dot(scores.astype(_bf16), v_bd)
        outs.append(o_g)

        upd = _dot_tn(v_b[:, sl], k_out[:, sl])
        decay = jnp.exp(b_last[:, sl])
        state_ref[g] = st * decay + jnp.where(head_mask, upd, 0.0)
    o = jnp.concatenate(outs, axis=1)

    ones_bd = ((_iota((HG_WIDTH, HG_WIDTH), 0) // HEAD_DIM)
               == (_iota((HG_WIDTH, HG_WIDTH), 1) // HEAD_DIM)).astype(_bf16)
    t_in_sub = _iota((CHUNK, HG_WIDTH), 0) % SUB
    for lag in range(SUB):
        if lag == 0:
            p = q * k
            v_l = v
        else:
            valid = t_in_sub >= lag
            k_l = pltpu.roll(k, lag, 0)
            b_l = pltpu.roll(b, lag, 0)
            v_l = pltpu.roll(v, lag, 0)
            p = jnp.where(valid, q * k_l * jnp.exp(jnp.minimum(b - b_l, 0.0)), 0.0)
        s_l = _dot(p.astype(_bf16), ones_bd)
        o = o + s_l * v_l
    return o


def _mix_kernel(x_ref, g_ref, win_ref, lb_ref, hgn_ref, convw_ref, pa_ref, pb_ref, wo_ref,
                out_ref, state_ref, carry_ref, q_s, k_s, lf_s, v_s, o_s):
    ts = x_ref.shape[0]
    d_model = x_ref.shape[1]
    w = HG_WIDTH

    @pl.when(pl.program_id(1) == 0)
    def _():
        state_ref[...] = jnp.zeros_like(state_ref)
        carry_ref[...] = jnp.zeros_like(carry_ref)

    x = x_ref[...]
    h = x * lax.rsqrt(jnp.mean(x * x, axis=-1, keepdims=True) + EPS) * g_ref[...]
    hb = h.astype(_bf16)

    def proj(i, width=w):
        return _dot(hb, win_ref[:, i * w:i * w + width])

    lb = lb_ref[...]
    q_s[...] = jax.nn.silu(proj(0)) * (HEAD_DIM ** -0.5)
    forget = lb + (1.0 - lb) * jax.nn.sigmoid(proj(1))
    k_s[...] = 1.0 - forget
    lf_s[...] = jnp.log(forget)
    v_s[...] = proj(2)

    def chunk_body(c, carry):
        rows = pl.ds(pl.multiple_of(c * CHUNK, CHUNK), CHUNK)
        o_s[rows, :] = _hgrn2_chunk(q_s[rows, :], k_s[rows, :], lf_s[rows, :], v_s[rows, :], state_ref)
        return carry

    lax.fori_loop(0, ts // CHUNK, chunk_body, 0)

    o = o_s[...]
    ones_bd = ((_iota((w, w), 0) // HEAD_DIM) == (_iota((w, w), 1) // HEAD_DIM)).astype(_bf16)
    ms = _dot_exact_rhs01(o * o, ones_bd) * (1.0 / HEAD_DIM)
    o = o * lax.rsqrt(ms + EPS) * hgn_ref[...]
    y_a = (o * jax.nn.silu(proj(3))).astype(_bf16)

    u = proj(5) * proj(6)
    prev = carry_ref[...]
    rowi = _iota((ts, w), 0)
    u1 = jnp.where(rowi >= 1, pltpu.roll(u, 1, 0), jnp.broadcast_to(prev[7:8, :], (ts, w)))
    u2 = jnp.where(rowi >= 2, pltpu.roll(u, 2, 0),
                   jnp.where(rowi == 1, jnp.broadcast_to(prev[7:8, :], (ts, w)),
                             jnp.broadcast_to(prev[6:7, :], (ts, w))))
    carry_ref[...] = u[ts - 8:, :]
    cw = convw_ref[...]
    y_b = (proj(4) * (cw[0:1, :] * u2 + cw[1:2, :] * u1 + cw[2:3, :] * u)).astype(_bf16)

    g_a = jax.nn.sigmoid(proj(7, d_model))
    g_b = jax.nn.sigmoid(_dot(hb, win_ref[:, 7 * w + d_model:7 * w + 2 * d_model]))
    merged = g_a * _dot(y_a, pa_ref[...]) + g_b * _dot(y_b, pb_ref[...])
    out_ref[...] = x + _dot(merged.astype(_bf16), wo_ref[...])


def _const_spec(shape):
    nd = len(shape)
    return pl.BlockSpec(shape, lambda *_: (0,) * nd, pipeline_mode=pl.Buffered(1))


def _mix(x, norm_g, w_in, lb, hg_norm_g, conv_w, w_a, w_b, w_o, *, ts):
    b_, s_, d = x.shape
    in_cols = w_in.shape[1]
    w = HG_WIDTH
    grid = (b_, s_ // ts)
    return pl.pallas_call(
        _mix_kernel,
        grid=grid,
        in_specs=[
            pl.BlockSpec((None, ts, d), lambda b, s: (b, s, 0)),
            _const_spec((1, d)),
            _const_spec((d, in_cols)),
            _const_spec((1, w)),
            _const_spec((1, w)),
            _const_spec((CONV_K, w)),
            _const_spec((w, d)),
            _const_spec((w, d)),
            _const_spec((d, d)),
        ],
        out_specs=pl.BlockSpec((None, ts, d), lambda b, s: (b, s, 0)),
        out_shape=jax.ShapeDtypeStruct((b_, s_, d), _f32),
        scratch_shapes=[
            pltpu.VMEM((N_GROUPS, GROUP, GROUP), _f32),
            pltpu.VMEM((8, w), _f32),
            pltpu.VMEM((ts, w), _f32),
            pltpu.VMEM((ts, w), _f32),
            pltpu.VMEM((ts, w), _f32),
            pltpu.VMEM((ts, w), _f32),
            pltpu.VMEM((ts, w), _f32),
        ],
        compiler_params=pltpu.CompilerParams(
            dimension_semantics=("parallel", "arbitrary"),
            vmem_limit_bytes=VMEM_LIMIT_BYTES),
        name="mix",
    )(x, norm_g, w_in, lb, hg_norm_g, conv_w, w_a, w_b, w_o)


def _stair_pairs():
    pairs = [(a, c) for a in range(PEER_TOPK) for c in range(PEER_TOPK) if (a + 1) * (c + 1) <= PEER_TOPK]
    rows = -(-len(pairs) // 8) * 8
    pairs = pairs + [(-1, -1)] * (rows - len(pairs))
    arr = jnp.asarray(pairs, _f32)
    return arr[:, 0:1], arr[:, 1:2]


def _route_kernel(x_ref, g_ref, wq_ref, k1_ref, k2_ref, ca_ref, cc_ref, h_ref, idx_ref, gate_ref,
                  idx_t, e_t):
    tr = x_ref.shape[0]
    n_cand = ca_ref.shape[0]
    x = x_ref[...]
    h = x * lax.rsqrt(jnp.mean(x * x, axis=-1, keepdims=True) + EPS) * g_ref[...]
    h_ref[...] = h
    hb = h.astype(_bf16)

    key_row = _iota((PEER_NKEYS, tr), 0).astype(_f32)
    cand_row = _iota((n_cand, tr), 0).astype(_f32)
    ca = ca_ref[...]
    cc = cc_ref[...]
    neg_inf = jnp.float32(-jnp.inf)

    def extract_max(s):
        m = jnp.max(s, axis=0, keepdims=True)
        i = jnp.min(jnp.where(s == m, key_row, float(PEER_NKEYS)), axis=0, keepdims=True)
        return m, i, jnp.where(key_row == i, neg_inf, s)

    def head_body(hd, carry):
        q1 = _dot(hb, wq_ref[hd, 0]).astype(_bf16)
        q2 = _dot(hb, wq_ref[hd, 1]).astype(_bf16)
        s1 = _dot_nt(k1_ref[hd], q1)
        s2 = _dot_nt(k2_ref[hd], q2)

        def half_body(k, c):
            s1, s2, a_s, a_i, c_s, c_i = c
            kf = jnp.asarray(k, _f32)
            m1, i1, s1 = extract_max(s1)
            m2, i2, s2 = extract_max(s2)
            sel_a = ca == kf
            sel_c = cc == kf
            return (s1, s2, jnp.where(sel_a, m1, a_s), jnp.where(sel_a, i1, a_i),
                    jnp.where(sel_c, m2, c_s), jnp.where(sel_c, i2, c_i))

        zeros = jnp.zeros((n_cand, tr), _f32)
        _, _, a_s, a_i, c_s, c_i = lax.fori_loop(
            0, PEER_TOPK, half_body, (s1, s2, jnp.full((n_cand, tr), neg_inf), zeros, zeros, zeros))
        cand_i = a_i * float(PEER_NKEYS) + c_i

        def pick_body(k, c):
            cand_s, denom, m_first = c
            m = jnp.max(cand_s, axis=0, keepdims=True)
            pos = jnp.min(jnp.where(cand_s == m, cand_row, float(n_cand)), axis=0, keepdims=True)
            hit = cand_row == pos
            eid = jnp.max(jnp.where(hit, cand_i, -1.0), axis=0, keepdims=True)
            m_first = jnp.where(k == 0, m, m_first)
            e = jnp.exp(m - m_first)
            slot = hd * PEER_TOPK + k
            idx_t[pl.ds(slot, 1), :] = eid
            e_t[pl.ds(slot, 1), :] = e
            return jnp.where(hit, neg_inf, cand_s), denom + e, m_first

        zero_row = jnp.zeros((1, tr), _f32)
        _, denom, _ = lax.fori_loop(0, PEER_TOPK, pick_body, (a_s + c_s, zero_row, zero_row))
        rows = pl.ds(pl.multiple_of(hd * PEER_TOPK, PEER_TOPK), PEER_TOPK)
        e_t[rows, :] = e_t[rows, :] / denom
        return carry

    lax.fori_loop(0, PEER_HEADS, head_body, 0)
    idx_ref[...] = idx_t[...].T.astype(jnp.int32)
    gate_ref[...] = e_t[...].T


def _route(x, norm_g, w_query, keys1, keys2, *, tr):
    t, d = x.shape
    ca, cc = _stair_pairs()
    ca = jnp.broadcast_to(ca, (ca.shape[0], tr))
    cc = jnp.broadcast_to(cc, (cc.shape[0], tr))
    return pl.pallas_call(
        _route_kernel,
        grid=(t // tr,),
        in_specs=[
            pl.BlockSpec((tr, d), lambda i: (i, 0)),
            _const_spec((1, d)),
            _const_spec(w_query.shape),
            _const_spec(keys1.shape),
            _const_spec(keys2.shape),
            _const_spec(ca.shape),
            _const_spec(cc.shape),
        ],
        out_specs=[
            pl.BlockSpec((tr, d), lambda i: (i, 0)),
            pl.BlockSpec((tr, PEER_SLOTS), lambda i: (i, 0)),
            pl.BlockSpec((tr, PEER_SLOTS), lambda i: (i, 0)),
        ],
        out_shape=[
            jax.ShapeDtypeStruct((t, d), _f32),
            jax.ShapeDtypeStruct((t, PEER_SLOTS), jnp.int32),
            jax.ShapeDtypeStruct((t, PEER_SLOTS), _f32),
        ],
        scratch_shapes=[
            pltpu.VMEM((PEER_SLOTS, tr), _f32),
            pltpu.VMEM((PEER_SLOTS, tr), _f32),
        ],
        compiler_params=pltpu.CompilerParams(
            dimension_semantics=("parallel",),
            vmem_limit_bytes=VMEM_LIMIT_BYTES),
        name="route",
    )(x, norm_g, w_query, keys1, keys2, ca, cc)


ROW_CHUNKS = 8
PACK_ROWS = ROW_CHUNKS // 2


def _pack_table(tab):
    n, d = tab.shape
    bits = lax.bitcast_convert_type(tab.astype(_bf16), jnp.uint16).astype(jnp.uint32)
    bits = bits.reshape(n, PACK_ROWS, 2, d // ROW_CHUNKS)
    word = bits[:, :, 0, :] | (bits[:, :, 1, :] << 16)
    return lax.bitcast_convert_type(word, jnp.int32)


def _split2(x):
    hi = x.astype(_bf16)
    lo = (x - hi.astype(_f32)).astype(_bf16)
    return hi, lo


def _gather_rows(idx_ref, t, tab_ref, rows_ref):
    for k in range(PEER_SLOTS):
        rows_ref[k * PACK_ROWS:(k + 1) * PACK_ROWS, :] = tab_ref[idx_ref[t, k]]


def _rows_matrix(rows_ref):
    return pltpu.bitcast(rows_ref[...], _bf16)


def _token_loop(tb, idx_ref, tab_ref, rows_a, rows_b, compute):
    _gather_rows(idx_ref, 0, tab_ref, rows_a)

    def pair_body(i, carry):
        t0 = 2 * i
        _gather_rows(idx_ref, t0 + 1, tab_ref, rows_b)
        compute(t0, _rows_matrix(rows_a))
        _gather_rows(idx_ref, jnp.minimum(t0 + 2, tb - 1), tab_ref, rows_a)
        compute(t0 + 1, _rows_matrix(rows_b))
        return carry

    lax.fori_loop(0, tb // 2, pair_body, 0)


def _chunk_diag_mask():
    shape = (ROW_CHUNKS, PEER_SLOTS * ROW_CHUNKS)
    return (_iota(shape, 1) % ROW_CHUNKS) == _iota(shape, 0)


def _udot_kernel(idx_ref, tab_ref, h_ref, gate_ref, w_ref, rows_a, rows_b, part_ref):
    tb = h_ref.shape[0]
    diag = _chunk_diag_mask()

    def compute(t, m):
        h_hi, h_lo = _split2(h_ref[t])
        r = _dot_nt(h_hi, m) + _dot_nt(h_lo, m)
        part_ref[pl.ds(t, 1), :] = jnp.sum(jnp.where(diag, r, 0.0), axis=0, keepdims=True)

    _token_loop(tb, idx_ref, tab_ref, rows_a, rows_b, compute)
    shape = (PEER_SLOTS * ROW_CHUNKS, PEER_SLOTS)
    fold = ((_iota(shape, 0) // ROW_CHUNKS) == _iota(shape, 1)).astype(_bf16)
    a = _dot_exact_rhs01(part_ref[...], fold)
    gelu = 0.5 * a * (1.0 + lax.erf(a * (2.0 ** -0.5)))
    w_ref[...] = gate_ref[...] * gelu


def _udot(idx, tab, h3, gate, *, tb):
    t = idx.shape[0]
    return pl.pallas_call(
        _udot_kernel,
        grid=(t // tb,),
        in_specs=[
            pl.BlockSpec((tb, PEER_SLOTS), lambda i: (i, 0), memory_space=pltpu.SMEM),
            _const_spec(tab.shape),
            pl.BlockSpec((tb, 8, 128), lambda i: (i, 0, 0)),
            pl.BlockSpec((tb, PEER_SLOTS), lambda i: (i, 0)),
        ],
        out_specs=pl.BlockSpec((tb, PEER_SLOTS), lambda i: (i, 0)),
        out_shape=jax.ShapeDtypeStruct((t, PEER_SLOTS), _f32),
        scratch_shapes=[
            pltpu.VMEM((PEER_SLOTS * PACK_ROWS, 128), jnp.int32),
            pltpu.VMEM((PEER_SLOTS * PACK_ROWS, 128), jnp.int32),
            pltpu.VMEM((tb, PEER_SLOTS * ROW_CHUNKS), _f32),
        ],
        compiler_params=pltpu.CompilerParams(
            dimension_semantics=("parallel",),
            vmem_limit_bytes=VMEM_LIMIT_BYTES),
        name="udot",
    )(idx, tab, h3, gate)


SC_CORES = 2
SC_SUBCORES = 16
SC_LANES = 16
SC_WORKERS = SC_CORES * SC_SUBCORES
SC_TOKENS = 8
SC_GATHER = 32
SC_ACCS = 4


def _sc_udot(idx, tab, h):
    t, d = h.shape
    per_worker = t // SC_WORKERS
    n_parts = PEER_SLOTS // SC_GATHER
    n_chunks = d // SC_LANES
    mesh = plsc.VectorSubcoreMesh(core_axis_name="c", subcore_axis_name="s")

    @functools.partial(
        pl.kernel, mesh=mesh,
        out_type=jax.ShapeDtypeStruct((t, PEER_SLOTS), _f32),
        scratch_types=[
            pltpu.VMEM((SC_TOKENS, PEER_SLOTS), jnp.int32),
            pltpu.VMEM((SC_TOKENS, d), _f32),
            pltpu.VMEM((SC_TOKENS, PEER_SLOTS), _f32),
            pltpu.VMEM((SC_GATHER, d), _f32),
            pltpu.VMEM((SC_GATHER, d), _f32),
            pltpu.SemaphoreType.DMA,
            pltpu.SemaphoreType.DMA,
        ],
        compiler_params=pltpu.CompilerParams(needs_layout_passes=False),
        name="sc_udot",
    )
    def body(idx_hbm, tab_hbm, h_hbm, out_hbm, idx_v, h_v, acts_v, rows0, rows1, sem0, sem1):
        wid = lax.axis_index("s") * SC_CORES + lax.axis_index("c")
        base = wid * per_worker
        bufs = ((rows0, sem0), (rows1, sem1))
        lane = lax.iota(jnp.int32, SC_LANES)

        def gather(i, part):
            rows, sem = bufs[part % 2]
            return pltpu.make_async_copy(
                tab_hbm.at[idx_v.at[i, pl.ds(part * SC_GATHER, SC_GATHER)]], rows, sem)

        def dots(i, part, rows):
            for blk in range(SC_GATHER // SC_LANES):
                def expert_body(e, outv):
                    row = blk * SC_LANES + e
                    accs = [jnp.zeros((SC_LANES,), _f32) for _ in range(SC_ACCS)]
                    for j in range(n_chunks):
                        sl = pl.ds(j * SC_LANES, SC_LANES)
                        accs[j % SC_ACCS] = accs[j % SC_ACCS] + rows[row, sl] * h_v[i, sl]
                    total = jnp.sum(functools.reduce(lambda a, b: a + b, accs))
                    return jnp.where(lane == e, total, outv)

                outv = lax.fori_loop(0, SC_LANES, expert_body, jnp.zeros((SC_LANES,), _f32))
                acts_v[i, pl.ds(part * SC_GATHER + blk * SC_LANES, SC_LANES)] = outv

        @pl.loop(0, per_worker // SC_TOKENS)
        def _(step):
            tok = pl.multiple_of(base + step * SC_TOKENS, SC_TOKENS)
            pltpu.sync_copy(idx_hbm.at[pl.ds(tok, SC_TOKENS)], idx_v)
            pltpu.sync_copy(h_hbm.at[pl.ds(tok, SC_TOKENS)], h_v)
            gather(0, 0).start()

            @pl.loop(0, SC_TOKENS)
            def _(i):
                for part in range(n_parts):
                    if part + 1 < n_parts:
                        gather(i, part + 1).start()
                    else:
                        @pl.when(i + 1 < SC_TOKENS)
                        def _():
                            gather(i + 1, 0).start()
                    gather(i, part).wait()
                    dots(i, part, bufs[part % 2][0])

            pltpu.sync_copy(acts_v, out_hbm.at[pl.ds(tok, SC_TOKENS)])

    return body(idx, tab, h)


def _vaxpy_kernel(idx_ref, acts_ref, gate_ref, tab_ref, out_ref, rows_a, rows_b, wrep_ref):
    tb = out_ref.shape[0]
    diag = _chunk_diag_mask()
    a = acts_ref[...]
    gelu = 0.5 * a * (1.0 + lax.erf(a * (2.0 ** -0.5)))
    w = gate_ref[...] * gelu
    shape = (PEER_SLOTS, PEER_SLOTS * ROW_CHUNKS)
    spread = (_iota(shape, 0) == (_iota(shape, 1) // ROW_CHUNKS)).astype(_bf16)
    wrep_ref[...] = _dot_exact_rhs01(w, spread)

    def compute(t, m):
        w_row = jnp.broadcast_to(wrep_ref[pl.ds(t, 1), :], diag.shape)
        w_hi, w_lo = _split2(jnp.where(diag, w_row, 0.0))
        out_ref[t] = _dot(w_hi, m) + _dot(w_lo, m)

    _token_loop(tb, idx_ref, tab_ref, rows_a, rows_b, compute)


def _vaxpy(idx, acts, gate, tab, *, tb):
    t = idx.shape[0]
    return pl.pallas_call(
        _vaxpy_kernel,
        grid=(t // tb,),
        in_specs=[
            pl.BlockSpec((tb, PEER_SLOTS), lambda i: (i, 0), memory_space=pltpu.SMEM),
            pl.BlockSpec((tb, PEER_SLOTS), lambda i: (i, 0)),
            pl.BlockSpec((tb, PEER_SLOTS), lambda i: (i, 0)),
            _const_spec(tab.shape),
        ],
        out_specs=pl.BlockSpec((tb, ROW_CHUNKS, 128), lambda i: (i, 0, 0)),
        out_shape=jax.ShapeDtypeStruct((t, ROW_CHUNKS, 128), _f32),
        scratch_shapes=[
            pltpu.VMEM((PEER_SLOTS * PACK_ROWS, 128), jnp.int32),
            pltpu.VMEM((PEER_SLOTS * PACK_ROWS, 128), jnp.int32),
            pltpu.VMEM((tb, PEER_SLOTS * ROW_CHUNKS), _f32),
        ],
        compiler_params=pltpu.CompilerParams(
            dimension_semantics=("parallel",),
            vmem_limit_bytes=VMEM_LIMIT_BYTES),
        name="vaxpy",
    )(idx, acts, gate, tab)


def _final_kernel(x_ref, p_ref, g_ref, out_ref, *, normalize):
    x = x_ref[...] + p_ref[...]
    if normalize:
        x = x * lax.rsqrt(jnp.mean(x * x, axis=-1, keepdims=True) + EPS) * g_ref[...]
    out_ref[...] = x


def _final(x, peer, g, *, normalize, tf):
    t, d = x.shape
    return pl.pallas_call(
        functools.partial(_final_kernel, normalize=normalize),
        grid=(t // tf,),
        in_specs=[
            pl.BlockSpec((tf, d), lambda i: (i, 0)),
            pl.BlockSpec((tf, d), lambda i: (i, 0)),
            _const_spec((1, d)),
        ],
        out_specs=pl.BlockSpec((tf, d), lambda i: (i, 0)),
        out_shape=jax.ShapeDtypeStruct((t, d), _f32),
        compiler_params=pltpu.CompilerParams(dimension_semantics=("parallel",)),
        name="final_norm",
    )(x, peer, g)


def kernel(x, norm_mix_g, w_in, hg_lb_logits, hg_out_norm_g, conv_w, w_branch_hg, w_branch_conv, w_out, norm_ffn_g, peer_w_query, peer_keys1, peer_keys2, peer_u, peer_v, norm_final_g):
    b_, s_, d = x.shape
    t = b_ * s_
    depth = w_in.shape[0]
    lb_all = jnp.cumsum(jax.nn.softmax(hg_lb_logits.astype(_f32), axis=0), axis=0)
    for l in range(depth):
        x = _mix(x, norm_mix_g[l][None], w_in[l].astype(_bf16), lb_all[l][None], hg_out_norm_g[l][None],
                 conv_w[l], w_branch_hg[l].astype(_bf16), w_branch_conv[l].astype(_bf16),
                 w_out[l].astype(_bf16), ts=MIX_TILE)
        xf = x.reshape(t, d)
        wq = peer_w_query[l].astype(_bf16).reshape(d, PEER_HEADS, 2, PEER_HALF).transpose(1, 2, 0, 3)
        h, idx, gate = _route(xf, norm_ffn_g[l][None], wq,
                              peer_keys1[l].astype(_bf16), peer_keys2[l].astype(_bf16), tr=ROUTE_TILE)
        acts = _sc_udot(idx, peer_u[l], h)
        peer = _vaxpy(idx, acts, gate, _pack_table(peer_v[l]), tb=EXPERT_TILE).reshape(t, d)
        last = l == depth - 1
        g = norm_final_g[None] if last else jnp.ones((1, d), _f32)
        x = _final(xf, peer, g, normalize=last, tf=FINAL_TILE).reshape(b_, s_, d)
    return x
```

```python
import functools

import jax
import jax.numpy as jnp
from jax import lax
from jax.experimental import pallas as pl
from jax.experimental.pallas import tpu as pltpu
from jax.experimental.pallas import tpu_sc as plsc

EPS = 1e-6
CHUNK = 64
SUB = 16
HEADS = 8
HEAD_DIM = 64
HG_WIDTH = HEADS * HEAD_DIM
GROUP = 256
N_GROUPS = HG_WIDTH // GROUP
CONV_K = 3
PEER_HEADS = 8
PEER_NKEYS = 128
PEER_HALF = 128
PEER_TOPK = 16
PEER_SLOTS = PEER_HEADS * PEER_TOPK

VMEM_LIMIT_BYTES = 56 * 1024 * 1024

MIX_TILE = 256
ROUTE_TILE = 256
EXPERT_TILE = 128
FINAL_TILE = 512
BATCH_GROUPS = 4
SC_SHARE = 3072

_f32 = jnp.float32
_bf16 = jnp.bfloat16


def _dot(a, b):
    return jnp.dot(a, b, preferred_element_type=_f32)


def _dot_nt(a, b):
    return lax.dot_general(a, b, (((1,), (1,)), ((), ())), preferred_element_type=_f32)


def _dot_tn(a, b):
    return lax.dot_general(a, b, (((0,), (0,)), ((), ())), preferred_element_type=_f32)


def _split3(x):
    hi = x.astype(_bf16)
    r1 = x - hi.astype(_f32)
    mid = r1.astype(_bf16)
    lo = (r1 - mid.astype(_f32)).astype(_bf16)
    return hi, mid, lo


def _split2(x):
    hi = x.astype(_bf16)
    lo = (x - hi.astype(_f32)).astype(_bf16)
    return hi, lo


def _dot_exact_rhs01(x, m01):
    hi, mid, lo = _split3(x)
    return _dot(hi, m01) + _dot(mid, m01) + _dot(lo, m01)


def _dot_exact_lhs01(m01, x):
    hi, mid, lo = _split3(x)
    return _dot(m01, hi) + _dot(m01, mid) + _dot(m01, lo)


def _iota(shape, dim):
    return lax.broadcasted_iota(jnp.int32, shape, dim)


def _hgrn2_chunk(q, k, lf, v, state_ref):
    n_sub = CHUNK // SUB
    row = _iota((CHUNK, CHUNK), 0)
    col = _iota((CHUNK, CHUNK), 1)
    tril = (col <= row).astype(_bf16)
    b = _dot_exact_lhs01(tril, lf)

    b_end = [b[(j + 1) * SUB - 1:(j + 1) * SUB, :] for j in range(n_sub)]
    b_end_rows = jnp.concatenate([jnp.broadcast_to(e, (SUB, HG_WIDTH)) for e in b_end], axis=0)
    b_last = b_end[-1]

    q_in = (q * jnp.exp(b)).astype(_bf16)
    k_sub = (k * jnp.exp(b_end_rows - b)).astype(_bf16)
    k_out = (k * jnp.exp(b_last - b)).astype(_bf16)
    q_from = [(q * jnp.exp(jnp.minimum(b - b_end[j], 0.0))).astype(_bf16) for j in range(n_sub - 1)]
    v_b = v.astype(_bf16)

    gr = _iota((GROUP, GROUP), 0) // HEAD_DIM
    gc = _iota((GROUP, GROUP), 1) // HEAD_DIM
    head_mask = gr == gc
    t_blk = _iota((CHUNK, GROUP), 0) // SUB
    s_blk = (_iota((CHUNK, GROUP), 1) % HEAD_DIM) // SUB

    outs = []
    for g in range(N_GROUPS):
        sl = slice(g * GROUP, (g + 1) * GROUP)
        st = state_ref[g]
        o_g = _dot_nt(q_in[:, sl], st.astype(_bf16))

        zero_b = jnp.zeros((), _bf16)
        k_bd = jnp.where(head_mask, jnp.concatenate([k_sub[:, sl]] * (GROUP // CHUNK), axis=0), zero_b)
        v_bd = jnp.where(head_mask, jnp.concatenate([v_b[:, sl]] * (GROUP // CHUNK), axis=0), zero_b)
        q_stack = jnp.concatenate([qf[:, sl] for qf in q_from], axis=0)
        r = _dot_nt(q_stack, k_bd)
        scores = jnp.zeros((CHUNK, GROUP), _f32)
        for j in range(n_sub - 1):
            sel = (s_blk == j) & (t_blk > j)
            scores = jnp.where(sel, r[j * CHUNK:(j + 1) * CHUNK, :], scores)
        o_g = o_g + _dot(scores.astype(_bf16), v_bd)
        outs.append(o_g)

        upd = _dot_tn(v_b[:, sl], k_out[:, sl])
        decay = jnp.exp(b_last[:, sl])
        state_ref[g] = st * decay + jnp.where(head_mask, upd, 0.0)
    o = jnp.concatenate(outs, axis=1)

    ones_bd = ((_iota((HG_WIDTH, HG_WIDTH), 0) // HEAD_DIM)
               == (_iota((HG_WIDTH, HG_WIDTH), 1) // HEAD_DIM)).astype(_bf16)
    t_in_sub = _iota((CHUNK, HG_WIDTH), 0) % SUB
    for lag in range(SUB):
        if lag == 0:
            p = q * k
            v_l = v
        else:
            valid = t_in_sub >= lag
            k_l = pltpu.roll(k, lag, 0)
            b_l = pltpu.roll(b, lag, 0)
            v_l = pltpu.roll(v, lag, 0)
            p = jnp.where(valid, q * k_l * jnp.exp(jnp.minimum(b - b_l, 0.0)), 0.0)
        s_l = _dot(p.astype(_bf16), ones_bd)
        o = o + s_l * v_l
    return o


def _mix_kernel(x_ref, g_ref, win_ref, lb_ref, hgn_ref, convw_ref, pa_ref, pb_ref, wo_ref,
                out_ref, state_ref, carry_ref, q_s, k_s, lf_s, v_s, o_s):
    ts = x_ref.shape[0]
    d_model = x_ref.shape[1]
    w = HG_WIDTH

    @pl.when(pl.program_id(1) == 0)
    def _():
        state_ref[...] = jnp.zeros_like(state_ref)
        carry_ref[...] = jnp.zeros_like(carry_ref)

    x = x_ref[...]
    h = x * lax.rsqrt(jnp.mean(x * x, axis=-1, keepdims=True) + EPS) * g_ref[...]
    hb = h.astype(_bf16)

    def proj(i, width=w):
        return _dot(hb, win_ref[:, i * w:i * w + width])

    lb = lb_ref[...]
    q_s[...] = jax.nn.silu(proj(0)) * (HEAD_DIM ** -0.5)
    forget = lb + (1.0 - lb) * jax.nn.sigmoid(proj(1))
    k_s[...] = 1.0 - forget
    lf_s[...] = jnp.log(forget)
    v_s[...] = proj(2)

    def chunk_body(c, carry):
        rows = pl.ds(pl.multiple_of(c * CHUNK, CHUNK), CHUNK)
        o_s[rows, :] = _hgrn2_chunk(q_s[rows, :], k_s[rows, :], lf_s[rows, :], v_s[rows, :], state_ref)
        return carry

    lax.fori_loop(0, ts // CHUNK, chunk_body, 0)

    o = o_s[...]
    ones_bd = ((_iota((w, w), 0) // HEAD_DIM) == (_iota((w, w), 1) // HEAD_DIM)).astype(_bf16)
    ms = _dot_exact_rhs01(o * o, ones_bd) * (1.0 / HEAD_DIM)
    o = o * lax.rsqrt(ms + EPS) * hgn_ref[...]
    y_a = (o * jax.nn.silu(proj(3))).astype(_bf16)

    u = proj(5) * proj(6)
    prev = carry_ref[...]
    rowi = _iota((ts, w), 0)
    u1 = jnp.where(rowi >= 1, pltpu.roll(u, 1, 0), jnp.broadcast_to(prev[7:8, :], (ts, w)))
    u2 = jnp.where(rowi >= 2, pltpu.roll(u, 2, 0),
                   jnp.where(rowi == 1, jnp.broadcast_to(prev[7:8, :], (ts, w)),
                             jnp.broadcast_to(prev[6:7, :], (ts, w))))
    carry_ref[...] = u[ts - 8:, :]
    cw = convw_ref[...]
    y_b = (proj(4) * (cw[0:1, :] * u2 + cw[1:2, :] * u1 + cw[2:3, :] * u)).astype(_bf16)

    g_a = jax.nn.sigmoid(proj(7, d_model))
    g_b = jax.nn.sigmoid(_dot(hb, win_ref[:, 7 * w + d_model:7 * w + 2 * d_model]))
    merged = g_a * _dot(y_a, pa_ref[...]) + g_b * _dot(y_b, pb_ref[...])
    out_ref[...] = x + _dot(merged.astype(_bf16), wo_ref[...])


def _const_spec(shape):
    nd = len(shape)
    return pl.BlockSpec(shape, lambda *_: (0,) * nd, pipeline_mode=pl.Buffered(1))


def _mix(x, norm_g, w_in, lb, hg_norm_g, conv_w, w_a, w_b, w_o, *, ts):
    b_, s_, d = x.shape
    in_cols = w_in.shape[1]
    w = HG_WIDTH
    grid = (b_, s_ // ts)
    return pl.pallas_call(
        _mix_kernel,
        grid=grid,
        in_specs=[
            pl.BlockSpec((None, ts, d), lambda b, s: (b, s, 0)),
            _const_spec((1, d)),
            _const_spec((d, in_cols)),
            _const_spec((1, w)),
            _const_spec((1, w)),
            _const_spec((CONV_K, w)),
            _const_spec((w, d)),
            _const_spec((w, d)),
            _const_spec((d, d)),
        ],
        out_specs=pl.BlockSpec((None, ts, d), lambda b, s: (b, s, 0)),
        out_shape=jax.ShapeDtypeStruct((b_, s_, d), _f32),
        scratch_shapes=[
            pltpu.VMEM((N_GROUPS, GROUP, GROUP), _f32),
            pltpu.VMEM((8, w), _f32),
            pltpu.VMEM((ts, w), _f32),
            pltpu.VMEM((ts, w), _f32),
            pltpu.VMEM((ts, w), _f32),
            pltpu.VMEM((ts, w), _f32),
            pltpu.VMEM((ts, w), _f32),
        ],
        compiler_params=pltpu.CompilerParams(
            dimension_semantics=("parallel", "arbitrary"),
            vmem_limit_bytes=VMEM_LIMIT_BYTES),
        name="mix",
    )(x, norm_g, w_in, lb, hg_norm_g, conv_w, w_a, w_b, w_o)


def _stair_pairs():
    pairs = [(a, c) for a in range(PEER_TOPK) for c in range(PEER_TOPK) if (a + 1) * (c + 1) <= PEER_TOPK]
    rows = -(-len(pairs) // 8) * 8
    pairs = pairs + [(-1, -1)] * (rows - len(pairs))
    arr = jnp.asarray(pairs, _f32)
    return arr[:, 0:1], arr[:, 1:2]


def _route_kernel(x_ref, g_ref, wq_ref, k1_ref, k2_ref, ca_ref, cc_ref, h_ref, idx_ref, gate_ref,
                  idx_t, e_t):
    tr = x_ref.shape[0]
    n_cand = ca_ref.shape[0]
    x = x_ref[...]
    h = x * lax.rsqrt(jnp.mean(x * x, axis=-1, keepdims=True) + EPS) * g_ref[...]
    h_ref[...] = h
    hb = h.astype(_bf16)

    key_row = _iota((PEER_NKEYS, tr), 0).astype(_f32)
    cand_row = _iota((n_cand, tr), 0).astype(_f32)
    ca = ca_ref[...]
    cc = cc_ref[...]
    neg_inf = jnp.float32(-jnp.inf)

    def extract_max(s):
        m = jnp.max(s, axis=0, keepdims=True)
        i = jnp.min(jnp.where(s == m, key_row, float(PEER_NKEYS)), axis=0, keepdims=True)
        return m, i, jnp.where(key_row == i, neg_inf, s)

    def head_body(hd, carry):
        q1 = _dot(hb, wq_ref[hd, 0]).astype(_bf16)
        q2 = _dot(hb, wq_ref[hd, 1]).astype(_bf16)
        s1 = _dot_nt(k1_ref[hd], q1)
        s2 = _dot_nt(k2_ref[hd], q2)

        def half_body(k, c):
            s1, s2, a_s, a_i, c_s, c_i = c
            kf = jnp.asarray(k, _f32)
            m1, i1, s1 = extract_max(s1)
            m2, i2, s2 = extract_max(s2)
            sel_a = ca == kf
            sel_c = cc == kf
            return (s1, s2, jnp.where(sel_a, m1, a_s), jnp.where(sel_a, i1, a_i),
                    jnp.where(sel_c, m2, c_s), jnp.where(sel_c, i2, c_i))

        zeros = jnp.zeros((n_cand, tr), _f32)
        _, _, a_s, a_i, c_s, c_i = lax.fori_loop(
            0, PEER_TOPK, half_body, (s1, s2, jnp.full((n_cand, tr), neg_inf), zeros, zeros, zeros))
        cand_i = a_i * float(PEER_NKEYS) + c_i

        def pick_body(k, c):
            cand_s, denom, m_first = c
            m = jnp.max(cand_s, axis=0, keepdims=True)
            pos = jnp.min(jnp.where(cand_s == m, cand_row, float(n_cand)), axis=0, keepdims=True)
            hit = cand_row == pos
            eid = jnp.max(jnp.where(hit, cand_i, -1.0), axis=0, keepdims=True)
            m_first = jnp.where(k == 0, m, m_first)
            e = jnp.exp(m - m_first)
            slot = hd * PEER_TOPK + k
            idx_t[pl.ds(slot, 1), :] = eid
            e_t[pl.ds(slot, 1), :] = e
            return jnp.where(hit, neg_inf, cand_s), denom + e, m_first

        zero_row = jnp.zeros((1, tr), _f32)
        _, denom, _ = lax.fori_loop(0, PEER_TOPK, pick_body, (a_s + c_s, zero_row, zero_row))
        rows = pl.ds(pl.multiple_of(hd * PEER_TOPK, PEER_TOPK), PEER_TOPK)
        e_t[rows, :] = e_t[rows, :] / denom
        return carry

    lax.fori_loop(0, PEER_HEADS, head_body, 0)
    idx_ref[...] = idx_t[...].T.astype(jnp.int32)
    gate_ref[...] = e_t[...].T


def _route(x, norm_g, w_query, keys1, keys2, *, tr):
    t, d = x.shape
    ca, cc = _stair_pairs()
    ca = jnp.broadcast_to(ca, (ca.shape[0], tr))
    cc = jnp.broadcast_to(cc, (cc.shape[0], tr))
    return pl.pallas_call(
        _route_kernel,
        grid=(t // tr,),
        in_specs=[
            pl.BlockSpec((tr, d), lambda i: (i, 0)),
            _const_spec((1, d)),
            _const_spec(w_query.shape),
            _const_spec(keys1.shape),
            _const_spec(keys2.shape),
            _const_spec(ca.shape),
            _const_spec(cc.shape),
        ],
        out_specs=[
            pl.BlockSpec((tr, d), lambda i: (i, 0)),
            pl.BlockSpec((tr, PEER_SLOTS), lambda i: (i, 0)),
            pl.BlockSpec((tr, PEER_SLOTS), lambda i: (i, 0)),
        ],
        out_shape=[
            jax.ShapeDtypeStruct((t, d), _f32),
            jax.ShapeDtypeStruct((t, PEER_SLOTS), jnp.int32),
            jax.ShapeDtypeStruct((t, PEER_SLOTS), _f32),
        ],
        scratch_shapes=[
            pltpu.VMEM((PEER_SLOTS, tr), _f32),
            pltpu.VMEM((PEER_SLOTS, tr), _f32),
        ],
        compiler_params=pltpu.CompilerParams(
            dimension_semantics=("parallel",),
            vmem_limit_bytes=VMEM_LIMIT_BYTES),
        name="route",
    )(x, norm_g, w_query, keys1, keys2, ca, cc)


SC_CORES = 2
SC_SUBCORES = 16
SC_LANES = 16
SC_WORKERS = SC_CORES * SC_SUBCORES
SC_TOKENS = 8
SC_GATHER = 32
SC_BLOCK = 8
SC_UNROLL = 2
SC_COLS = 16


def _sc_mesh():
    return plsc.VectorSubcoreMesh(core_axis_name="c", subcore_axis_name="s")


def _sc_worker_base(per_worker):
    return (lax.axis_index("s") * SC_CORES + lax.axis_index("c")) * per_worker


def _sc_gather_loop(tab_hbm, idx_v, bufs, consume):
    n_parts = PEER_SLOTS // SC_GATHER
    n_gathers = SC_TOKENS * n_parts

    def gather(g, parity):
        rows, sem = bufs[parity]
        i = g // n_parts
        col = pl.multiple_of((g % n_parts) * SC_GATHER, SC_GATHER)
        return pltpu.make_async_copy(tab_hbm.at[idx_v.at[i, pl.ds(col, SC_GATHER)]], rows, sem)

    gather(0, 0).start()

    @pl.loop(0, n_gathers // 2)
    def _(pair):
        g = 2 * pair
        gather(g + 1, 1).start()
        gather(g, 0).wait()
        consume(g // n_parts, g % n_parts, bufs[0][0])

        @pl.when(g + 2 < n_gathers)
        def _():
            gather(g + 2, 0).start()

        gather(g + 1, 1).wait()
        consume((g + 1) // n_parts, (g + 1) % n_parts, bufs[1][0])


def _sc_udot(idx, tab, h):
    t, d = h.shape
    per_worker = t // SC_WORKERS
    n_chunks = d // SC_LANES

    @functools.partial(
        pl.kernel, mesh=_sc_mesh(),
        out_type=jax.ShapeDtypeStruct((t, PEER_SLOTS), _f32),
        scratch_types=[
            pltpu.VMEM((SC_TOKENS, PEER_SLOTS), jnp.int32),
            pltpu.VMEM((SC_TOKENS, d), _f32),
            pltpu.VMEM((SC_TOKENS, PEER_SLOTS), _f32),
            pltpu.VMEM((SC_GATHER, d), _f32),
            pltpu.VMEM((SC_GATHER, d), _f32),
            pltpu.SemaphoreType.DMA,
            pltpu.SemaphoreType.DMA,
        ],
        compiler_params=pltpu.CompilerParams(needs_layout_passes=False),
        name="sc_udot",
    )
    def body(idx_hbm, tab_hbm, h_hbm, out_hbm, idx_v, h_v, acts_v, rows0, rows1, sem0, sem1):
        base = _sc_worker_base(per_worker)
        lane = lax.iota(jnp.int32, SC_LANES)

        def dots(i, part, rows):
            def block_body(blk, carry):
                row0 = blk * SC_LANES
                outv = jnp.zeros((SC_LANES,), _f32)
                for sub in range(SC_LANES // SC_BLOCK):
                    def chunk_body(jj, accs):
                        accs = list(accs)
                        for u in range(SC_UNROLL):
                            sl = pl.ds(pl.multiple_of((jj * SC_UNROLL + u) * SC_LANES, SC_LANES), SC_LANES)
                            hj = h_v[i, sl]
                            for e in range(SC_BLOCK):
                                accs[e] = accs[e] + rows[row0 + sub * SC_BLOCK + e, sl] * hj
                        return tuple(accs)

                    accs = lax.fori_loop(0, n_chunks // SC_UNROLL, chunk_body,
                                         tuple(jnp.zeros((SC_LANES,), _f32) for _ in range(SC_BLOCK)))
                    for e in range(SC_BLOCK):
                        outv = jnp.where(lane == sub * SC_BLOCK + e, jnp.sum(accs[e]), outv)
                col = pl.multiple_of(part * SC_GATHER + row0, SC_LANES)
                acts_v[i, pl.ds(col, SC_LANES)] = outv
                return carry

            lax.fori_loop(0, SC_GATHER // SC_LANES, block_body, 0)

        @pl.loop(0, per_worker // SC_TOKENS)
        def _(step):
            tok = pl.multiple_of(base + step * SC_TOKENS, SC_TOKENS)
            pltpu.sync_copy(idx_hbm.at[pl.ds(tok, SC_TOKENS)], idx_v)
            pltpu.sync_copy(h_hbm.at[pl.ds(tok, SC_TOKENS)], h_v)
            _sc_gather_loop(tab_hbm, idx_v, ((rows0, sem0), (rows1, sem1)), dots)
            pltpu.sync_copy(acts_v, out_hbm.at[pl.ds(tok, SC_TOKENS)])

    return body(idx, tab, h)


def _sc_vaxpy(idx, w, tab, *, first, count):
    d = tab.shape[1]
    per_worker = count // SC_WORKERS
    span = SC_COLS * SC_LANES

    @functools.partial(
        pl.kernel, mesh=_sc_mesh(),
        out_type=jax.ShapeDtypeStruct((count, d), _f32),
        scratch_types=[
            pltpu.VMEM((SC_TOKENS, PEER_SLOTS), jnp.int32),
            pltpu.VMEM((SC_TOKENS, PEER_SLOTS), _f32),
            pltpu.VMEM((SC_TOKENS, d), _f32),
            pltpu.VMEM((SC_GATHER, d), _f32),
            pltpu.VMEM((SC_GATHER, d), _f32),
            pltpu.SemaphoreType.DMA,
            pltpu.SemaphoreType.DMA,
        ],
        compiler_params=pltpu.CompilerParams(needs_layout_passes=False),
        name="sc_vaxpy",
    )
    def body(idx_hbm, w_hbm, tab_hbm, out_hbm, idx_v, w_v, out_v, rows0, rows1, sem0, sem1):
        base = _sc_worker_base(per_worker)

        def accumulate(i, part, rows):
            i_vec = jnp.full((SC_LANES,), i, jnp.int32)

            def span_body(cq, carry):
                def cols(c):
                    return pl.ds(pl.multiple_of(cq * span + c * SC_LANES, SC_LANES), SC_LANES)

                def expert_body(e, accs):
                    k_vec = jnp.full((SC_LANES,), part * SC_GATHER + e, jnp.int32)
                    wv = plsc.load_gather(w_v, [i_vec, k_vec])
                    return tuple(accs[c] + rows[e, cols(c)] * wv for c in range(SC_COLS))

                accs = lax.fori_loop(0, SC_GATHER, expert_body,
                                     tuple(out_v[i, cols(c)] for c in range(SC_COLS)))
                for c in range(SC_COLS):
                    out_v[i, cols(c)] = accs[c]
                return carry

            lax.fori_loop(0, d // span, span_body, 0)

        @pl.loop(0, per_worker // SC_TOKENS)
        def _(step):
            off = pl.multiple_of(base + step * SC_TOKENS, SC_TOKENS)
            pltpu.sync_copy(idx_hbm.at[pl.ds(first + off, SC_TOKENS)], idx_v)
            pltpu.sync_copy(w_hbm.at[pl.ds(first + off, SC_TOKENS)], w_v)

            @pl.loop(0, SC_TOKENS)
            def _(i):
                @pl.loop(0, d // SC_LANES)
                def _(j):
                    out_v[i, pl.ds(pl.multiple_of(j * SC_LANES, SC_LANES), SC_LANES)] = (
                        jnp.zeros((SC_LANES,), _f32))

            _sc_gather_loop(tab_hbm, idx_v, ((rows0, sem0), (rows1, sem1)), accumulate)
            pltpu.sync_copy(out_v, out_hbm.at[pl.ds(off, SC_TOKENS)])

    return body(idx, w, tab)


ROW_CHUNKS = 8
PACK_ROWS = ROW_CHUNKS // 2


def _pack_table(tab):
    n, d = tab.shape
    bits = lax.bitcast_convert_type(tab.astype(_bf16), jnp.uint16).astype(jnp.uint32)
    bits = bits.reshape(n, PACK_ROWS, 2, d // ROW_CHUNKS)
    word = bits[:, :, 0, :] | (bits[:, :, 1, :] << 16)
    return lax.bitcast_convert_type(word, jnp.int32)


def _gate_weights_kernel(acts_ref, gate_ref, w_ref):
    a = acts_ref[...]
    gelu = 0.5 * a * (1.0 + lax.erf(a * (2.0 ** -0.5)))
    w_ref[...] = gate_ref[...] * gelu


def _gate_weights(acts, gate, *, tw):
    t, n = acts.shape
    spec = pl.BlockSpec((tw, n), lambda i: (i, 0))
    return pl.pallas_call(
        _gate_weights_kernel,
        grid=(t // tw,),
        in_specs=[spec, spec],
        out_specs=spec,
        out_shape=jax.ShapeDtypeStruct((t, n), _f32),
        compiler_params=pltpu.CompilerParams(dimension_semantics=("parallel",)),
        name="gate_weights",
    )(acts, gate)


def _gather_rows(idx_ref, t, tab_ref, rows_ref):
    for k in range(PEER_SLOTS):
        rows_ref[k * PACK_ROWS:(k + 1) * PACK_ROWS, :] = tab_ref[idx_ref[t, k]]


def _rows_matrix(rows_ref):
    return pltpu.bitcast(rows_ref[...], _bf16)


def _token_loop(tb, idx_ref, tab_ref, rows_a, rows_b, compute):
    _gather_rows(idx_ref, 0, tab_ref, rows_a)

    def pair_body(i, carry):
        t0 = 2 * i
        _gather_rows(idx_ref, t0 + 1, tab_ref, rows_b)
        compute(t0, _rows_matrix(rows_a))
        _gather_rows(idx_ref, jnp.minimum(t0 + 2, tb - 1), tab_ref, rows_a)
        compute(t0 + 1, _rows_matrix(rows_b))
        return carry

    lax.fori_loop(0, tb // 2, pair_body, 0)


def _chunk_diag_mask():
    shape = (ROW_CHUNKS, PEER_SLOTS * ROW_CHUNKS)
    return (_iota(shape, 1) % ROW_CHUNKS) == _iota(shape, 0)


def _vaxpy_kernel(idx_ref, w_ref, tab_ref, out_ref, rows_a, rows_b, wrep_ref):
    tb = out_ref.shape[0]
    diag = _chunk_diag_mask()
    shape = (PEER_SLOTS, PEER_SLOTS * ROW_CHUNKS)
    spread = (_iota(shape, 0) == (_iota(shape, 1) // ROW_CHUNKS)).astype(_bf16)
    wrep_ref[...] = _dot_exact_rhs01(w_ref[...], spread)

    def compute(t, m):
        w_row = jnp.broadcast_to(wrep_ref[pl.ds(t, 1), :], diag.shape)
        w_hi, w_lo = _split2(jnp.where(diag, w_row, 0.0))
        out_ref[t] = _dot(w_hi, m) + _dot(w_lo, m)

    _token_loop(tb, idx_ref, tab_ref, rows_a, rows_b, compute)


def _vaxpy(idx, w, tab, *, tb, count):
    return pl.pallas_call(
        _vaxpy_kernel,
        grid=(count // tb,),
        in_specs=[
            pl.BlockSpec((tb, PEER_SLOTS), lambda i: (i, 0), memory_space=pltpu.SMEM),
            pl.BlockSpec((tb, PEER_SLOTS), lambda i: (i, 0)),
            _const_spec(tab.shape),
        ],
        out_specs=pl.BlockSpec((tb, ROW_CHUNKS, 128), lambda i: (i, 0, 0)),
        out_shape=jax.ShapeDtypeStruct((count, ROW_CHUNKS, 128), _f32),
        scratch_shapes=[
            pltpu.VMEM((PEER_SLOTS * PACK_ROWS, 128), jnp.int32),
            pltpu.VMEM((PEER_SLOTS * PACK_ROWS, 128), jnp.int32),
            pltpu.VMEM((tb, PEER_SLOTS * ROW_CHUNKS), _f32),
        ],
        compiler_params=pltpu.CompilerParams(
            dimension_semantics=("parallel",),
            vmem_limit_bytes=VMEM_LIMIT_BYTES),
        name="vaxpy",
    )(idx, w, tab)


def _final_kernel(x_ref, p_ref, g_ref, out_ref, *, normalize):
    x = x_ref[...] + p_ref[...]
    if normalize:
        x = x * lax.rsqrt(jnp.mean(x * x, axis=-1, keepdims=True) + EPS) * g_ref[...]
    out_ref[...] = x


def _final(x, peer, g, *, normalize, tf, first):
    t, d = peer.shape
    off = first // tf
    return pl.pallas_call(
        functools.partial(_final_kernel, normalize=normalize),
        grid=(t // tf,),
        in_specs=[
            pl.BlockSpec((tf, d), lambda i: (i + off, 0)),
            pl.BlockSpec((tf, d), lambda i: (i, 0)),
            _const_spec((1, d)),
        ],
        out_specs=pl.BlockSpec((tf, d), lambda i: (i, 0)),
        out_shape=jax.ShapeDtypeStruct((t, d), _f32),
        compiler_params=pltpu.CompilerParams(dimension_semantics=("parallel",)),
        name="final_norm",
    )(x, peer, g)


def kernel(x, norm_mix_g, w_in, hg_lb_logits, hg_out_norm_g, conv_w, w_branch_hg, w_branch_conv, w_out, norm_ffn_g, peer_w_query, peer_keys1, peer_keys2, peer_u, peer_v, norm_final_g):
    b_, s_, d = x.shape
    depth = w_in.shape[0]
    lb_all = jnp.cumsum(jax.nn.softmax(hg_lb_logits.astype(_f32), axis=0), axis=0)
    n_groups = BATCH_GROUPS if b_ % BATCH_GROUPS == 0 else 1
    bg = b_ // n_groups
    tg = bg * s_
    n_sc = SC_SHARE if SC_SHARE < tg else 0
    n_tc = tg - n_sc
    for l in range(depth):
        wq = peer_w_query[l].astype(_bf16).reshape(d, PEER_HEADS, 2, PEER_HALF).transpose(1, 2, 0, 3)
        v_tab = _pack_table(peer_v[l])
        last = l == depth - 1
        g = norm_final_g[None] if last else jnp.ones((1, d), _f32)
        outs = []
        for c in range(n_groups):
            xc = _mix(x[c * bg:(c + 1) * bg], norm_mix_g[l][None], w_in[l].astype(_bf16), lb_all[l][None],
                      hg_out_norm_g[l][None], conv_w[l], w_branch_hg[l].astype(_bf16),
                      w_branch_conv[l].astype(_bf16), w_out[l].astype(_bf16), ts=MIX_TILE)
            xf = xc.reshape(tg, d)
            h, idx, gate = _route(xf, norm_ffn_g[l][None], wq,
                                  peer_keys1[l].astype(_bf16), peer_keys2[l].astype(_bf16), tr=ROUTE_TILE)
            w = _gate_weights(_sc_udot(idx, peer_u[l], h), gate, tw=FINAL_TILE)
            peer_tc = _vaxpy(idx, w, v_tab, tb=EXPERT_TILE, count=n_tc).reshape(n_tc, d)
            outs.append(_final(xf, peer_tc, g, normalize=last, tf=FINAL_TILE, first=0))
            if n_sc:
                peer_sc = _sc_vaxpy(idx, w, peer_v[l], first=n_tc, count=n_sc)
                outs.append(_final(xf, peer_sc, g, normalize=last, tf=FINAL_TILE, first=n_tc))
        x = jnp.concatenate(outs, axis=0).reshape(b_, s_, d)
    return x
```

```python
import functools

import jax
import jax.numpy as jnp
from jax import lax
from jax.experimental import pallas as pl
from jax.experimental.pallas import tpu as pltpu
from jax.experimental.pallas import tpu_sc as plsc

EPS = 1e-6
CHUNK = 64
SUB = 16
HEADS = 8
HEAD_DIM = 64
HG_WIDTH = HEADS * HEAD_DIM
GROUP = 256
N_GROUPS = HG_WIDTH // GROUP
CONV_K = 3
PEER_HEADS = 8
PEER_NKEYS = 128
PEER_HALF = 128
PEER_TOPK = 16
PEER_SLOTS = PEER_HEADS * PEER_TOPK

VMEM_LIMIT_BYTES = 56 * 1024 * 1024

MIX_TILE = 256
ROUTE_TILE = 256
EXPERT_TILE = 128
FINAL_TILE = 512
BATCH_GROUPS = 4
SC_SHARE = 3072
GROUPS_AHEAD = 2

_f32 = jnp.float32
_bf16 = jnp.bfloat16


def _dot(a, b):
    return jnp.dot(a, b, preferred_element_type=_f32)


def _dot_nt(a, b):
    return lax.dot_general(a, b, (((1,), (1,)), ((), ())), preferred_element_type=_f32)


def _dot_tn(a, b):
    return lax.dot_general(a, b, (((0,), (0,)), ((), ())), preferred_element_type=_f32)


def _split3(x):
    hi = x.astype(_bf16)
    r1 = x - hi.astype(_f32)
    mid = r1.astype(_bf16)
    lo = (r1 - mid.astype(_f32)).astype(_bf16)
    return hi, mid, lo


def _split2(x):
    hi = x.astype(_bf16)
    lo = (x - hi.astype(_f32)).astype(_bf16)
    return hi, lo


def _dot_exact_rhs01(x, m01):
    hi, mid, lo = _split3(x)
    return _dot(hi, m01) + _dot(mid, m01) + _dot(lo, m01)


def _dot_exact_lhs01(m01, x):
    hi, mid, lo = _split3(x)
    return _dot(m01, hi) + _dot(m01, mid) + _dot(m01, lo)


def _iota(shape, dim):
    return lax.broadcasted_iota(jnp.int32, shape, dim)


def _hgrn2_chunk(q, k, lf, v, state_ref):
    n_sub = CHUNK // SUB
    row = _iota((CHUNK, CHUNK), 0)
    col = _iota((CHUNK, CHUNK), 1)
    tril = (col <= row).astype(_bf16)
    b = _dot_exact_lhs01(tril, lf)

    b_end = [b[(j + 1) * SUB - 1:(j + 1) * SUB, :] for j in range(n_sub)]
    b_end_rows = jnp.concatenate([jnp.broadcast_to(e, (SUB, HG_WIDTH)) for e in b_end], axis=0)
    b_last = b_end[-1]

    q_in = (q * jnp.exp(b)).astype(_bf16)
    k_sub = (k * jnp.exp(b_end_rows - b)).astype(_bf16)
    k_out = (k * jnp.exp(b_last - b)).astype(_bf16)
    q_from = [(q * jnp.exp(jnp.minimum(b - b_end[j], 0.0))).astype(_bf16) for j in range(n_sub - 1)]
    v_b = v.astype(_bf16)

    gr = _iota((GROUP, GROUP), 0) // HEAD_DIM
    gc = _iota((GROUP, GROUP), 1) // HEAD_DIM
    head_mask = gr == gc
    t_blk = _iota((CHUNK, GROUP), 0) // SUB
    s_blk = (_iota((CHUNK, GROUP), 1) % HEAD_DIM) // SUB

    outs = []
    for g in range(N_GROUPS):
        sl = slice(g * GROUP, (g + 1) * GROUP)
        st = state_ref[g]
        o_g = _dot_nt(q_in[:, sl], st.astype(_bf16))

        zero_b = jnp.zeros((), _bf16)
        k_bd = jnp.where(head_mask, jnp.concatenate([k_sub[:, sl]] * (GROUP // CHUNK), axis=0), zero_b)
        v_bd = jnp.where(head_mask, jnp.concatenate([v_b[:, sl]] * (GROUP // CHUNK), axis=0), zero_b)
        q_stack = jnp.concatenate([qf[:, sl] for qf in q_from], axis=0)
        r = _dot_nt(q_stack, k_bd)
        scores = jnp.zeros((CHUNK, GROUP), _f32)
        for j in range(n_sub - 1):
            sel = (s_blk == j) & (t_blk > j)
            scores = jnp.where(sel, r[j * CHUNK:(j + 1) * CHUNK, :], scores)
        o_g = o_g + _dot(scores.astype(_bf16), v_bd)
        outs.append(o_g)

        upd = _dot_tn(v_b[:, sl], k_out[:, sl])
        decay = jnp.exp(b_last[:, sl])
        state_ref[g] = st * decay + jnp.where(head_mask, upd, 0.0)
    o = jnp.concatenate(outs, axis=1)

    ones_bd = ((_iota((HG_WIDTH, HG_WIDTH), 0) // HEAD_DIM)
               == (_iota((HG_WIDTH, HG_WIDTH), 1) // HEAD_DIM)).astype(_bf16)
    t_in_sub = _iota((CHUNK, HG_WIDTH), 0) % SUB
    for lag in range(SUB):
        if lag == 0:
            p = q * k
            v_l = v
        else:
            valid = t_in_sub >= lag
            k_l = pltpu.roll(k, lag, 0)
            b_l = pltpu.roll(b, lag, 0)
            v_l = pltpu.roll(v, lag, 0)
            p = jnp.where(valid, q * k_l * jnp.exp(jnp.minimum(b - b_l, 0.0)), 0.0)
        s_l = _dot(p.astype(_bf16), ones_bd)
        o = o + s_l * v_l
    return o


def _mix_kernel(x_ref, g_ref, win_ref, lb_ref, hgn_ref, convw_ref, pa_ref, pb_ref, wo_ref,
                out_ref, state_ref, carry_ref, q_s, k_s, lf_s, v_s, o_s):
    ts = x_ref.shape[0]
    d_model = x_ref.shape[1]
    w = HG_WIDTH

    @pl.when(pl.program_id(1) == 0)
    def _():
        state_ref[...] = jnp.zeros_like(state_ref)
        carry_ref[...] = jnp.zeros_like(carry_ref)

    x = x_ref[...]
    h = x * lax.rsqrt(jnp.mean(x * x, axis=-1, keepdims=True) + EPS) * g_ref[...]
    hb = h.astype(_bf16)

    def proj(i, width=w):
        return _dot(hb, win_ref[:, i * w:i * w + width])

    lb = lb_ref[...]
    q_s[...] = jax.nn.silu(proj(0)) * (HEAD_DIM ** -0.5)
    forget = lb + (1.0 - lb) * jax.nn.sigmoid(proj(1))
    k_s[...] = 1.0 - forget
    lf_s[...] = jnp.log(forget)
    v_s[...] = proj(2)

    def chunk_body(c, carry):
        rows = pl.ds(pl.multiple_of(c * CHUNK, CHUNK), CHUNK)
        o_s[rows, :] = _hgrn2_chunk(q_s[rows, :], k_s[rows, :], lf_s[rows, :], v_s[rows, :], state_ref)
        return carry

    lax.fori_loop(0, ts // CHUNK, chunk_body, 0)

    o = o_s[...]
    ones_bd = ((_iota((w, w), 0) // HEAD_DIM) == (_iota((w, w), 1) // HEAD_DIM)).astype(_bf16)
    ms = _dot_exact_rhs01(o * o, ones_bd) * (1.0 / HEAD_DIM)
    o = o * lax.rsqrt(ms + EPS) * hgn_ref[...]
    y_a = (o * jax.nn.silu(proj(3))).astype(_bf16)

    u = proj(5) * proj(6)
    prev = carry_ref[...]
    rowi = _iota((ts, w), 0)
    u1 = jnp.where(rowi >= 1, pltpu.roll(u, 1, 0), jnp.broadcast_to(prev[7:8, :], (ts, w)))
    u2 = jnp.where(rowi >= 2, pltpu.roll(u, 2, 0),
                   jnp.where(rowi == 1, jnp.broadcast_to(prev[7:8, :], (ts, w)),
                             jnp.broadcast_to(prev[6:7, :], (ts, w))))
    carry_ref[...] = u[ts - 8:, :]
    cw = convw_ref[...]
    y_b = (proj(4) * (cw[0:1, :] * u2 + cw[1:2, :] * u1 + cw[2:3, :] * u)).astype(_bf16)

    g_a = jax.nn.sigmoid(proj(7, d_model))
    g_b = jax.nn.sigmoid(_dot(hb, win_ref[:, 7 * w + d_model:7 * w + 2 * d_model]))
    merged = g_a * _dot(y_a, pa_ref[...]) + g_b * _dot(y_b, pb_ref[...])
    out_ref[...] = x + _dot(merged.astype(_bf16), wo_ref[...])


def _const_spec(shape):
    nd = len(shape)
    return pl.BlockSpec(shape, lambda *_: (0,) * nd, pipeline_mode=pl.Buffered(1))


def _mix(x, norm_g, w_in, lb, hg_norm_g, conv_w, w_a, w_b, w_o, *, ts):
    b_, s_, d = x.shape
    in_cols = w_in.shape[1]
    w = HG_WIDTH
    grid = (b_, s_ // ts)
    return pl.pallas_call(
        _mix_kernel,
        grid=grid,
        in_specs=[
            pl.BlockSpec((None, ts, d), lambda b, s: (b, s, 0)),
            _const_spec((1, d)),
            _const_spec((d, in_cols)),
            _const_spec((1, w)),
            _const_spec((1, w)),
            _const_spec((CONV_K, w)),
            _const_spec((w, d)),
            _const_spec((w, d)),
            _const_spec((d, d)),
        ],
        out_specs=pl.BlockSpec((None, ts, d), lambda b, s: (b, s, 0)),
        out_shape=jax.ShapeDtypeStruct((b_, s_, d), _f32),
        scratch_shapes=[
            pltpu.VMEM((N_GROUPS, GROUP, GROUP), _f32),
            pltpu.VMEM((8, w), _f32),
            pltpu.VMEM((ts, w), _f32),
            pltpu.VMEM((ts, w), _f32),
            pltpu.VMEM((ts, w), _f32),
            pltpu.VMEM((ts, w), _f32),
            pltpu.VMEM((ts, w), _f32),
        ],
        compiler_params=pltpu.CompilerParams(
            dimension_semantics=("parallel", "arbitrary"),
            vmem_limit_bytes=VMEM_LIMIT_BYTES),
        name="mix",
    )(x, norm_g, w_in, lb, hg_norm_g, conv_w, w_a, w_b, w_o)


def _stair_pairs():
    pairs = [(a, c) for a in range(PEER_TOPK) for c in range(PEER_TOPK) if (a + 1) * (c + 1) <= PEER_TOPK]
    rows = -(-len(pairs) // 8) * 8
    pairs = pairs + [(-1, -1)] * (rows - len(pairs))
    arr = jnp.asarray(pairs, _f32)
    return arr[:, 0:1], arr[:, 1:2]


def _route_kernel(x_ref, g_ref, wq_ref, k1_ref, k2_ref, ca_ref, cc_ref, h_ref, idx_ref, gate_ref,
                  idx_t, e_t):
    tr = x_ref.shape[0]
    n_cand = ca_ref.shape[0]
    x = x_ref[...]
    h = x * lax.rsqrt(jnp.mean(x * x, axis=-1, keepdims=True) + EPS) * g_ref[...]
    h_ref[...] = h
    hb = h.astype(_bf16)

    key_row = _iota((PEER_NKEYS, tr), 0).astype(_f32)
    cand_row = _iota((n_cand, tr), 0).astype(_f32)
    ca = ca_ref[...]
    cc = cc_ref[...]
    neg_inf = jnp.float32(-jnp.inf)

    def extract_max(s):
        m = jnp.max(s, axis=0, keepdims=True)
        i = jnp.min(jnp.where(s == m, key_row, float(PEER_NKEYS)), axis=0, keepdims=True)
        return m, i, jnp.where(key_row == i, neg_inf, s)

    def head_body(hd, carry):
        q1 = _dot(hb, wq_ref[hd, 0]).astype(_bf16)
        q2 = _dot(hb, wq_ref[hd, 1]).astype(_bf16)
        s1 = _dot_nt(k1_ref[hd], q1)
        s2 = _dot_nt(k2_ref[hd], q2)

        def half_body(k, c):
            s1, s2, a_s, a_i, c_s, c_i = c
            kf = jnp.asarray(k, _f32)
            m1, i1, s1 = extract_max(s1)
            m2, i2, s2 = extract_max(s2)
            sel_a = ca == kf
            sel_c = cc == kf
            return (s1, s2, jnp.where(sel_a, m1, a_s), jnp.where(sel_a, i1, a_i),
                    jnp.where(sel_c, m2, c_s), jnp.where(sel_c, i2, c_i))

        zeros = jnp.zeros((n_cand, tr), _f32)
        _, _, a_s, a_i, c_s, c_i = lax.fori_loop(
            0, PEER_TOPK, half_body, (s1, s2, jnp.full((n_cand, tr), neg_inf), zeros, zeros, zeros))
        cand_i = a_i * float(PEER_NKEYS) + c_i

        def pick_body(k, c):
            cand_s, denom, m_first = c
            m = jnp.max(cand_s, axis=0, keepdims=True)
            pos = jnp.min(jnp.where(cand_s == m, cand_row, float(n_cand)), axis=0, keepdims=True)
            hit = cand_row == pos
            eid = jnp.max(jnp.where(hit, cand_i, -1.0), axis=0, keepdims=True)
            m_first = jnp.where(k == 0, m, m_first)
            e = jnp.exp(m - m_first)
            slot = hd * PEER_TOPK + k
            idx_t[pl.ds(slot, 1), :] = eid
            e_t[pl.ds(slot, 1), :] = e
            return jnp.where(hit, neg_inf, cand_s), denom + e, m_first

        zero_row = jnp.zeros((1, tr), _f32)
        _, denom, _ = lax.fori_loop(0, PEER_TOPK, pick_body, (a_s + c_s, zero_row, zero_row))
        rows = pl.ds(pl.multiple_of(hd * PEER_TOPK, PEER_TOPK), PEER_TOPK)
        e_t[rows, :] = e_t[rows, :] / denom
        return carry

    lax.fori_loop(0, PEER_HEADS, head_body, 0)
    idx_ref[...] = idx_t[...].T.astype(jnp.int32)
    gate_ref[...] = e_t[...].T


def _route(x, norm_g, w_query, keys1, keys2, *, tr):
    t, d = x.shape
    ca, cc = _stair_pairs()
    ca = jnp.broadcast_to(ca, (ca.shape[0], tr))
    cc = jnp.broadcast_to(cc, (cc.shape[0], tr))
    return pl.pallas_call(
        _route_kernel,
        grid=(t // tr,),
        in_specs=[
            pl.BlockSpec((tr, d), lambda i: (i, 0)),
            _const_spec((1, d)),
            _const_spec(w_query.shape),
            _const_spec(keys1.shape),
            _const_spec(keys2.shape),
            _const_spec(ca.shape),
            _const_spec(cc.shape),
        ],
        out_specs=[
            pl.BlockSpec((tr, d), lambda i: (i, 0)),
            pl.BlockSpec((tr, PEER_SLOTS), lambda i: (i, 0)),
            pl.BlockSpec((tr, PEER_SLOTS), lambda i: (i, 0)),
        ],
        out_shape=[
            jax.ShapeDtypeStruct((t, d), _f32),
            jax.ShapeDtypeStruct((t, PEER_SLOTS), jnp.int32),
            jax.ShapeDtypeStruct((t, PEER_SLOTS), _f32),
        ],
        scratch_shapes=[
            pltpu.VMEM((PEER_SLOTS, tr), _f32),
            pltpu.VMEM((PEER_SLOTS, tr), _f32),
        ],
        compiler_params=pltpu.CompilerParams(
            dimension_semantics=("parallel",),
            vmem_limit_bytes=VMEM_LIMIT_BYTES),
        name="route",
    )(x, norm_g, w_query, keys1, keys2, ca, cc)


SC_CORES = 2
SC_SUBCORES = 16
SC_LANES = 16
SC_WORKERS = SC_CORES * SC_SUBCORES
SC_TOKENS = 8
SC_GATHER = 32
SC_BLOCK = 8
SC_UNROLL = 2
SC_COLS = 8
HI_MASK = -65536


def _sc_mesh():
    return plsc.VectorSubcoreMesh(core_axis_name="c", subcore_axis_name="s")


def _sc_worker_base(per_worker):
    return (lax.axis_index("s") * SC_CORES + lax.axis_index("c")) * per_worker


def _sc_gather_loop(tab_hbm, idx_v, bufs, consume):
    n_parts = PEER_SLOTS // SC_GATHER
    n_gathers = SC_TOKENS * n_parts

    def gather(g, parity):
        rows, sem = bufs[parity]
        i = g // n_parts
        col = pl.multiple_of((g % n_parts) * SC_GATHER, SC_GATHER)
        return pltpu.make_async_copy(tab_hbm.at[idx_v.at[i, pl.ds(col, SC_GATHER)]], rows, sem)

    gather(0, 0).start()

    @pl.loop(0, n_gathers // 2)
    def _(pair):
        g = 2 * pair
        gather(g + 1, 1).start()
        gather(g, 0).wait()
        consume(g // n_parts, g % n_parts, bufs[0][0])

        @pl.when(g + 2 < n_gathers)
        def _():
            gather(g + 2, 0).start()

        gather(g + 1, 1).wait()
        consume((g + 1) // n_parts, (g + 1) % n_parts, bufs[1][0])


def _pack_halves(tab):
    half = tab.shape[1] // 2
    bits = lax.bitcast_convert_type(tab.astype(_bf16), jnp.uint16).astype(jnp.uint32)
    return lax.bitcast_convert_type(bits[:, :half] | (bits[:, half:] << 16), jnp.int32)


def _sc_unpack(words):
    lo = lax.bitcast_convert_type(words << 16, _f32)
    hi = lax.bitcast_convert_type(words & HI_MASK, _f32)
    return lo, hi


def _sc_udot(idx, tab, h):
    t, d = h.shape
    per_worker = t // SC_WORKERS
    half = d // 2

    @functools.partial(
        pl.kernel, mesh=_sc_mesh(),
        out_type=jax.ShapeDtypeStruct((t, PEER_SLOTS), _f32),
        scratch_types=[
            pltpu.VMEM((SC_TOKENS, PEER_SLOTS), jnp.int32),
            pltpu.VMEM((SC_TOKENS, d), _f32),
            pltpu.VMEM((SC_TOKENS, PEER_SLOTS), _f32),
            pltpu.VMEM((SC_GATHER, half), jnp.int32),
            pltpu.VMEM((SC_GATHER, half), jnp.int32),
            pltpu.SemaphoreType.DMA,
            pltpu.SemaphoreType.DMA,
        ],
        compiler_params=pltpu.CompilerParams(needs_layout_passes=False),
        name="sc_udot",
    )
    def body(idx_hbm, tab_hbm, h_hbm, out_hbm, idx_v, h_v, acts_v, rows0, rows1, sem0, sem1):
        base = _sc_worker_base(per_worker)
        lane = lax.iota(jnp.int32, SC_LANES)

        def dots(i, part, rows):
            def block_body(blk, carry):
                row0 = blk * SC_LANES
                outv = jnp.zeros((SC_LANES,), _f32)
                for sub in range(SC_LANES // SC_BLOCK):
                    def chunk_body(jj, accs):
                        accs = list(accs)
                        for u in range(SC_UNROLL):
                            off = pl.multiple_of((jj * SC_UNROLL + u) * SC_LANES, SC_LANES)
                            h_lo = h_v[i, pl.ds(off, SC_LANES)]
                            h_hi = h_v[i, pl.ds(half + off, SC_LANES)]
                            for e in range(SC_BLOCK):
                                lo, hi = _sc_unpack(rows[row0 + sub * SC_BLOCK + e, pl.ds(off, SC_LANES)])
                                accs[e] = accs[e] + (lo * h_lo + hi * h_hi)
                        return tuple(accs)

                    accs = lax.fori_loop(0, half // SC_LANES // SC_UNROLL, chunk_body,
                                         tuple(jnp.zeros((SC_LANES,), _f32) for _ in range(SC_BLOCK)))
                    for e in range(SC_BLOCK):
                        outv = jnp.where(lane == sub * SC_BLOCK + e, jnp.sum(accs[e]), outv)
                col = pl.multiple_of(part * SC_GATHER + row0, SC_LANES)
                acts_v[i, pl.ds(col, SC_LANES)] = outv
                return carry

            lax.fori_loop(0, SC_GATHER // SC_LANES, block_body, 0)

        @pl.loop(0, per_worker // SC_TOKENS)
        def _(step):
            tok = pl.multiple_of(base + step * SC_TOKENS, SC_TOKENS)
            pltpu.sync_copy(idx_hbm.at[pl.ds(tok, SC_TOKENS)], idx_v)
            pltpu.sync_copy(h_hbm.at[pl.ds(tok, SC_TOKENS)], h_v)
            _sc_gather_loop(tab_hbm, idx_v, ((rows0, sem0), (rows1, sem1)), dots)
            pltpu.sync_copy(acts_v, out_hbm.at[pl.ds(tok, SC_TOKENS)])

    return body(idx, tab, h)


def _sc_vaxpy(idx, w, tab, *, first, count):
    half = tab.shape[1]
    d = 2 * half
    per_worker = count // SC_WORKERS
    span = SC_COLS * SC_LANES

    @functools.partial(
        pl.kernel, mesh=_sc_mesh(),
        out_type=jax.ShapeDtypeStruct((count, d), _f32),
        scratch_types=[
            pltpu.VMEM((SC_TOKENS, PEER_SLOTS), jnp.int32),
            pltpu.VMEM((SC_TOKENS, PEER_SLOTS), _f32),
            pltpu.VMEM((SC_TOKENS, d), _f32),
            pltpu.VMEM((SC_GATHER, half), jnp.int32),
            pltpu.VMEM((SC_GATHER, half), jnp.int32),
            pltpu.SemaphoreType.DMA,
            pltpu.SemaphoreType.DMA,
        ],
        compiler_params=pltpu.CompilerParams(needs_layout_passes=False),
        name="sc_vaxpy",
    )
    def body(idx_hbm, w_hbm, tab_hbm, out_hbm, idx_v, w_v, out_v, rows0, rows1, sem0, sem1):
        base = _sc_worker_base(per_worker)

        def accumulate(i, part, rows):
            i_vec = jnp.full((SC_LANES,), i, jnp.int32)

            def span_body(cq, carry):
                def cols(c, offset=0):
                    return pl.ds(pl.multiple_of(offset + cq * span + c * SC_LANES, SC_LANES), SC_LANES)

                def expert_body(e, accs):
                    k_vec = jnp.full((SC_LANES,), part * SC_GATHER + e, jnp.int32)
                    wv = plsc.load_gather(w_v, [i_vec, k_vec])
                    new = []
                    for c in range(SC_COLS):
                        lo, hi = _sc_unpack(rows[e, cols(c)])
                        new += [accs[2 * c] + lo * wv, accs[2 * c + 1] + hi * wv]
                    return tuple(new)

                init = []
                for c in range(SC_COLS):
                    init += [out_v[i, cols(c)], out_v[i, cols(c, half)]]
                accs = lax.fori_loop(0, SC_GATHER, expert_body, tuple(init))
                for c in range(SC_COLS):
                    out_v[i, cols(c)] = accs[2 * c]
                    out_v[i, cols(c, half)] = accs[2 * c + 1]
                return carry

            lax.fori_loop(0, half // span, span_body, 0)

        @pl.loop(0, per_worker // SC_TOKENS)
        def _(step):
            off = pl.multiple_of(base + step * SC_TOKENS, SC_TOKENS)
            pltpu.sync_copy(idx_hbm.at[pl.ds(first + off, SC_TOKENS)], idx_v)
            pltpu.sync_copy(w_hbm.at[pl.ds(first + off, SC_TOKENS)], w_v)

            @pl.loop(0, SC_TOKENS)
            def _(i):
                @pl.loop(0, d // SC_LANES)
                def _(j):
                    out_v[i, pl.ds(pl.multiple_of(j * SC_LANES, SC_LANES), SC_LANES)] = (
                        jnp.zeros((SC_LANES,), _f32))

            _sc_gather_loop(tab_hbm, idx_v, ((rows0, sem0), (rows1, sem1)), accumulate)
            pltpu.sync_copy(out_v, out_hbm.at[pl.ds(off, SC_TOKENS)])

    return body(idx, w, tab)


ROW_CHUNKS = 8
PACK_ROWS = ROW_CHUNKS // 2


def _pack_table(tab):
    n, d = tab.shape
    bits = lax.bitcast_convert_type(tab.astype(_bf16), jnp.uint16).astype(jnp.uint32)
    bits = bits.reshape(n, PACK_ROWS, 2, d // ROW_CHUNKS)
    word = bits[:, :, 0, :] | (bits[:, :, 1, :] << 16)
    return lax.bitcast_convert_type(word, jnp.int32)


def _gate_weights_kernel(acts_ref, gate_ref, w_ref):
    a = acts_ref[...]
    gelu = 0.5 * a * (1.0 + lax.erf(a * (2.0 ** -0.5)))
    w_ref[...] = gate_ref[...] * gelu


def _gate_weights(acts, gate, *, tw):
    t, n = acts.shape
    spec = pl.BlockSpec((tw, n), lambda i: (i, 0))
    return pl.pallas_call(
        _gate_weights_kernel,
        grid=(t // tw,),
        in_specs=[spec, spec],
        out_specs=spec,
        out_shape=jax.ShapeDtypeStruct((t, n), _f32),
        compiler_params=pltpu.CompilerParams(dimension_semantics=("parallel",)),
        name="gate_weights",
    )(acts, gate)


def _gather_rows(idx_ref, t, tab_ref, rows_ref):
    for k in range(PEER_SLOTS):
        rows_ref[k * PACK_ROWS:(k + 1) * PACK_ROWS, :] = tab_ref[idx_ref[t, k]]


def _rows_matrix(rows_ref):
    return pltpu.bitcast(rows_ref[...], _bf16)


def _token_loop(tb, idx_ref, tab_ref, rows_a, rows_b, compute):
    _gather_rows(idx_ref, 0, tab_ref, rows_a)

    def pair_body(i, carry):
        t0 = 2 * i
        _gather_rows(idx_ref, t0 + 1, tab_ref, rows_b)
        compute(t0, _rows_matrix(rows_a))
        _gather_rows(idx_ref, jnp.minimum(t0 + 2, tb - 1), tab_ref, rows_a)
        compute(t0 + 1, _rows_matrix(rows_b))
        return carry

    lax.fori_loop(0, tb // 2, pair_body, 0)


def _chunk_diag_mask():
    shape = (ROW_CHUNKS, PEER_SLOTS * ROW_CHUNKS)
    return (_iota(shape, 1) % ROW_CHUNKS) == _iota(shape, 0)


def _vaxpy_kernel(idx_ref, w_ref, tab_ref, out_ref, rows_a, rows_b, wrep_ref):
    tb = out_ref.shape[0]
    diag = _chunk_diag_mask()
    shape = (PEER_SLOTS, PEER_SLOTS * ROW_CHUNKS)
    spread = (_iota(shape, 0) == (_iota(shape, 1) // ROW_CHUNKS)).astype(_bf16)
    wrep_ref[...] = _dot_exact_rhs01(w_ref[...], spread)

    def compute(t, m):
        w_row = jnp.broadcast_to(wrep_ref[pl.ds(t, 1), :], diag.shape)
        w_hi, w_lo = _split2(jnp.where(diag, w_row, 0.0))
        out_ref[t] = _dot(w_hi, m) + _dot(w_lo, m)

    _token_loop(tb, idx_ref, tab_ref, rows_a, rows_b, compute)


def _vaxpy(idx, w, tab, *, tb, count):
    return pl.pallas_call(
        _vaxpy_kernel,
        grid=(count // tb,),
        in_specs=[
            pl.BlockSpec((tb, PEER_SLOTS), lambda i: (i, 0), memory_space=pltpu.SMEM),
            pl.BlockSpec((tb, PEER_SLOTS), lambda i: (i, 0)),
            _const_spec(tab.shape),
        ],
        out_specs=pl.BlockSpec((tb, ROW_CHUNKS, 128), lambda i: (i, 0, 0)),
        out_shape=jax.ShapeDtypeStruct((count, ROW_CHUNKS, 128), _f32),
        scratch_shapes=[
            pltpu.VMEM((PEER_SLOTS * PACK_ROWS, 128), jnp.int32),
            pltpu.VMEM((PEER_SLOTS * PACK_ROWS, 128), jnp.int32),
            pltpu.VMEM((tb, PEER_SLOTS * ROW_CHUNKS), _f32),
        ],
        compiler_params=pltpu.CompilerParams(
            dimension_semantics=("parallel",),
            vmem_limit_bytes=VMEM_LIMIT_BYTES),
        name="vaxpy",
    )(idx, w, tab)


def _final_kernel(x_ref, p_ref, g_ref, out_ref, *, normalize):
    x = x_ref[...] + p_ref[...]
    if normalize:
        x = x * lax.rsqrt(jnp.mean(x * x, axis=-1, keepdims=True) + EPS) * g_ref[...]
    out_ref[...] = x


def _final(x, peer, g, *, normalize, tf, first):
    t, d = peer.shape
    off = first // tf
    return pl.pallas_call(
        functools.partial(_final_kernel, normalize=normalize),
        grid=(t // tf,),
        in_specs=[
            pl.BlockSpec((tf, d), lambda i: (i + off, 0)),
            pl.BlockSpec((tf, d), lambda i: (i, 0)),
            _const_spec((1, d)),
        ],
        out_specs=pl.BlockSpec((tf, d), lambda i: (i, 0)),
        out_shape=jax.ShapeDtypeStruct((t, d), _f32),
        compiler_params=pltpu.CompilerParams(dimension_semantics=("parallel",)),
        name="final_norm",
    )(x, peer, g)


def kernel(x, norm_mix_g, w_in, hg_lb_logits, hg_out_norm_g, conv_w, w_branch_hg, w_branch_conv, w_out, norm_ffn_g, peer_w_query, peer_keys1, peer_keys2, peer_u, peer_v, norm_final_g):
    b_, s_, d = x.shape
    depth = w_in.shape[0]
    lb_all = jnp.cumsum(jax.nn.softmax(hg_lb_logits.astype(_f32), axis=0), axis=0)
    n_groups = BATCH_GROUPS if b_ % BATCH_GROUPS == 0 else 1
    bg = b_ // n_groups
    tg = bg * s_
    n_sc = SC_SHARE if SC_SHARE < tg else 0
    n_tc = tg - n_sc
    for l in range(depth):
        wq = peer_w_query[l].astype(_bf16).reshape(d, PEER_HEADS, 2, PEER_HALF).transpose(1, 2, 0, 3)
        v_tab = _pack_table(peer_v[l])
        u_sc = _pack_halves(peer_u[l])
        v_sc = _pack_halves(peer_v[l])
        last = l == depth - 1
        g = norm_final_g[None] if last else jnp.ones((1, d), _f32)
        def front(c, x_in):
            xc = _mix(x_in, norm_mix_g[l][None], w_in[l].astype(_bf16), lb_all[l][None],
                      hg_out_norm_g[l][None], conv_w[l], w_branch_hg[l].astype(_bf16),
                      w_branch_conv[l].astype(_bf16), w_out[l].astype(_bf16), ts=MIX_TILE)
            xf = xc.reshape(tg, d)
            h, idx, gate = _route(xf, norm_ffn_g[l][None], wq,
                                  peer_keys1[l].astype(_bf16), peer_keys2[l].astype(_bf16), tr=ROUTE_TILE)
            w = _gate_weights(_sc_udot(idx, u_sc, h), gate, tw=FINAL_TILE)
            return xf, idx, w

        def back(xf, idx, w):
            peer_tc = _vaxpy(idx, w, v_tab, tb=EXPERT_TILE, count=n_tc).reshape(n_tc, d)
            parts = [_final(xf, peer_tc, g, normalize=last, tf=FINAL_TILE, first=0)]
            peer_sc = None
            if n_sc:
                peer_sc = _sc_vaxpy(idx, w, v_sc, first=n_tc, count=n_sc)
                parts.append(_final(xf, peer_sc, g, normalize=last, tf=FINAL_TILE, first=n_tc))
            return peer_tc, peer_sc, parts

        groups = [x[c * bg:(c + 1) * bg] for c in range(n_groups)]
        fronts = {c: front(c, groups[c]) for c in range(min(GROUPS_AHEAD + 1, n_groups))}
        outs, prev_sc = [], None
        for c in range(n_groups):
            xf, idx, w = fronts.pop(c)
            after = [fronts[c + GROUPS_AHEAD][1]] if c + GROUPS_AHEAD in fronts else []
            after += [prev_sc] if prev_sc is not None else []
            if after:
                w = lax.optimization_barrier((w, *after))[0]
            peer_tc, peer_sc, parts = back(xf, idx, w)
            outs.extend(parts)
            prev_sc = peer_sc
            nxt = c + GROUPS_AHEAD + 1
            if nxt < n_groups:
                fronts[nxt] = front(nxt, lax.optimization_barrier((groups[nxt], peer_tc))[0])
        x = jnp.concatenate(outs, axis=0).reshape(b_, s_, d)
    return x
```

```python
import functools

import jax
import jax.numpy as jnp
from jax import lax
from jax.experimental import pallas as pl
from jax.experimental.pallas import tpu as pltpu
from jax.experimental.pallas import tpu_sc as plsc

EPS = 1e-6
CHUNK = 64
SUB = 16
HEADS = 8
HEAD_DIM = 64
HG_WIDTH = HEADS * HEAD_DIM
GROUP = 256
N_GROUPS = HG_WIDTH // GROUP
CONV_K = 3
PEER_HEADS = 8
PEER_NKEYS = 128
PEER_HALF = 128
PEER_TOPK = 16
PEER_SLOTS = PEER_HEADS * PEER_TOPK

VMEM_LIMIT_BYTES = 56 * 1024 * 1024

MIX_TILE = 256
ROUTE_TILE = 256
EXPERT_TILE = 128
FINAL_TILE = 512
BATCH_GROUPS = 4
SC_SHARE = 3072
GROUPS_AHEAD = 2

_f32 = jnp.float32
_bf16 = jnp.bfloat16


def _dot(a, b):
    return jnp.dot(a, b, preferred_element_type=_f32)


def _dot_nt(a, b):
    return lax.dot_general(a, b, (((1,), (1,)), ((), ())), preferred_element_type=_f32)


def _dot_tn(a, b):
    return lax.dot_general(a, b, (((0,), (0,)), ((), ())), preferred_element_type=_f32)


def _split3(x):
    hi = x.astype(_bf16)
    r1 = x - hi.astype(_f32)
    mid = r1.astype(_bf16)
    lo = (r1 - mid.astype(_f32)).astype(_bf16)
    return hi, mid, lo


def _split2(x):
    hi = x.astype(_bf16)
    lo = (x - hi.astype(_f32)).astype(_bf16)
    return hi, lo


def _dot_exact_rhs01(x, m01):
    hi, mid, lo = _split3(x)
    return _dot(hi, m01) + _dot(mid, m01) + _dot(lo, m01)


def _dot_exact_lhs01(m01, x):
    hi, mid, lo = _split3(x)
    return _dot(m01, hi) + _dot(m01, mid) + _dot(m01, lo)


def _iota(shape, dim):
    return lax.broadcasted_iota(jnp.int32, shape, dim)


def _hgrn2_chunk(q, k, lf, v, state_ref):
    n_sub = CHUNK // SUB
    row = _iota((CHUNK, CHUNK), 0)
    col = _iota((CHUNK, CHUNK), 1)
    tril = (col <= row).astype(_bf16)
    b = _dot_exact_lhs01(tril, lf)

    b_end = [b[(j + 1) * SUB - 1:(j + 1) * SUB, :] for j in range(n_sub)]
    b_end_rows = jnp.concatenate([jnp.broadcast_to(e, (SUB, HG_WIDTH)) for e in b_end], axis=0)
    b_last = b_end[-1]

    q_in = (q * jnp.exp(b)).astype(_bf16)
    k_sub = (k * jnp.exp(b_end_rows - b)).astype(_bf16)
    k_out = (k * jnp.exp(b_last - b)).astype(_bf16)
    q_from = [(q * jnp.exp(jnp.minimum(b - b_end[j], 0.0))).astype(_bf16) for j in range(n_sub - 1)]
    v_b = v.astype(_bf16)

    gr = _iota((GROUP, GROUP), 0) // HEAD_DIM
    gc = _iota((GROUP, GROUP), 1) // HEAD_DIM
    head_mask = gr == gc
    t_blk = _iota((CHUNK, GROUP), 0) // SUB
    s_blk = (_iota((CHUNK, GROUP), 1) % HEAD_DIM) // SUB

    outs = []
    for g in range(N_GROUPS):
        sl = slice(g * GROUP, (g + 1) * GROUP)
        st = state_ref[g]
        o_g = _dot_nt(q_in[:, sl], st.astype(_bf16))

        zero_b = jnp.zeros((), _bf16)
        k_bd = jnp.where(head_mask, jnp.concatenate([k_sub[:, sl]] * (GROUP // CHUNK), axis=0), zero_b)
        v_bd = jnp.where(head_mask, jnp.concatenate([v_b[:, sl]] * (GROUP // CHUNK), axis=0), zero_b)
        q_stack = jnp.concatenate([qf[:, sl] for qf in q_from], axis=0)
        r = _dot_nt(q_stack, k_bd)
        scores = jnp.zeros((CHUNK, GROUP), _f32)
        for j in range(n_sub - 1):
            sel = (s_blk == j) & (t_blk > j)
            scores = jnp.where(sel, r[j * CHUNK:(j + 1) * CHUNK, :], scores)
        o_g = o_g + _dot(scores.astype(_bf16), v_bd)
        outs.append(o_g)

        upd = _dot_tn(v_b[:, sl], k_out[:, sl])
        decay = jnp.exp(b_last[:, sl])
        state_ref[g] = st * decay + jnp.where(head_mask, upd, 0.0)
    o = jnp.concatenate(outs, axis=1)

    ones_bd = ((_iota((HG_WIDTH, HG_WIDTH), 0) // HEAD_DIM)
               == (_iota((HG_WIDTH, HG_WIDTH), 1) // HEAD_DIM)).astype(_bf16)
    t_in_sub = _iota((CHUNK, HG_WIDTH), 0) % SUB
    for lag in range(SUB):
        if lag == 0:
            p = q * k
            v_l = v
        else:
            valid = t_in_sub >= lag
            k_l = pltpu.roll(k, lag, 0)
            b_l = pltpu.roll(b, lag, 0)
            v_l = pltpu.roll(v, lag, 0)
            p = jnp.where(valid, q * k_l * jnp.exp(jnp.minimum(b - b_l, 0.0)), 0.0)
        s_l = _dot(p.astype(_bf16), ones_bd)
        o = o + s_l * v_l
    return o


def _mix_kernel(x_ref, g_ref, win_ref, lb_ref, hgn_ref, convw_ref, pa_ref, pb_ref, wo_ref,
                out_ref, state_ref, carry_ref, q_s, k_s, lf_s, v_s, o_s):
    ts = x_ref.shape[0]
    d_model = x_ref.shape[1]
    w = HG_WIDTH

    @pl.when(pl.program_id(1) == 0)
    def _():
        state_ref[...] = jnp.zeros_like(state_ref)
        carry_ref[...] = jnp.zeros_like(carry_ref)

    x = x_ref[...]
    h = x * lax.rsqrt(jnp.mean(x * x, axis=-1, keepdims=True) + EPS) * g_ref[...]
    hb = h.astype(_bf16)

    def proj(i, width=w):
        return _dot(hb, win_ref[:, i * w:i * w + width])

    lb = lb_ref[...]
    q_s[...] = jax.nn.silu(proj(0)) * (HEAD_DIM ** -0.5)
    forget = lb + (1.0 - lb) * jax.nn.sigmoid(proj(1))
    k_s[...] = 1.0 - forget
    lf_s[...] = jnp.log(forget)
    v_s[...] = proj(2)

    def chunk_body(c, carry):
        rows = pl.ds(pl.multiple_of(c * CHUNK, CHUNK), CHUNK)
        o_s[rows, :] = _hgrn2_chunk(q_s[rows, :], k_s[rows, :], lf_s[rows, :], v_s[rows, :], state_ref)
        return carry

    lax.fori_loop(0, ts // CHUNK, chunk_body, 0)

    o = o_s[...]
    ones_bd = ((_iota((w, w), 0) // HEAD_DIM) == (_iota((w, w), 1) // HEAD_DIM)).astype(_bf16)
    ms = _dot_exact_rhs01(o * o, ones_bd) * (1.0 / HEAD_DIM)
    o = o * lax.rsqrt(ms + EPS) * hgn_ref[...]
    y_a = (o * jax.nn.silu(proj(3))).astype(_bf16)

    u = proj(5) * proj(6)
    prev = carry_ref[...]
    rowi = _iota((ts, w), 0)
    u1 = jnp.where(rowi >= 1, pltpu.roll(u, 1, 0), jnp.broadcast_to(prev[7:8, :], (ts, w)))
    u2 = jnp.where(rowi >= 2, pltpu.roll(u, 2, 0),
                   jnp.where(rowi == 1, jnp.broadcast_to(prev[7:8, :], (ts, w)),
                             jnp.broadcast_to(prev[6:7, :], (ts, w))))
    carry_ref[...] = u[ts - 8:, :]
    cw = convw_ref[...]
    y_b = (proj(4) * (cw[0:1, :] * u2 + cw[1:2, :] * u1 + cw[2:3, :] * u)).astype(_bf16)

    g_a = jax.nn.sigmoid(proj(7, d_model))
    g_b = jax.nn.sigmoid(_dot(hb, win_ref[:, 7 * w + d_model:7 * w + 2 * d_model]))
    merged = g_a * _dot(y_a, pa_ref[...]) + g_b * _dot(y_b, pb_ref[...])
    out_ref[...] = x + _dot(merged.astype(_bf16), wo_ref[...])


def _const_spec(shape):
    nd = len(shape)
    return pl.BlockSpec(shape, lambda *_: (0,) * nd, pipeline_mode=pl.Buffered(1))


def _mix(x, norm_g, w_in, lb, hg_norm_g, conv_w, w_a, w_b, w_o, *, ts):
    b_, s_, d = x.shape
    in_cols = w_in.shape[1]
    w = HG_WIDTH
    grid = (b_, s_ // ts)
    return pl.pallas_call(
        _mix_kernel,
        grid=grid,
        in_specs=[
            pl.BlockSpec((None, ts, d), lambda b, s: (b, s, 0)),
            _const_spec((1, d)),
            _const_spec((d, in_cols)),
            _const_spec((1, w)),
            _const_spec((1, w)),
            _const_spec((CONV_K, w)),
            _const_spec((w, d)),
            _const_spec((w, d)),
            _const_spec((d, d)),
        ],
        out_specs=pl.BlockSpec((None, ts, d), lambda b, s: (b, s, 0)),
        out_shape=jax.ShapeDtypeStruct((b_, s_, d), _f32),
        scratch_shapes=[
            pltpu.VMEM((N_GROUPS, GROUP, GROUP), _f32),
            pltpu.VMEM((8, w), _f32),
            pltpu.VMEM((ts, w), _f32),
            pltpu.VMEM((ts, w), _f32),
            pltpu.VMEM((ts, w), _f32),
            pltpu.VMEM((ts, w), _f32),
            pltpu.VMEM((ts, w), _f32),
        ],
        compiler_params=pltpu.CompilerParams(
            dimension_semantics=("parallel", "arbitrary"),
            vmem_limit_bytes=VMEM_LIMIT_BYTES),
        name="mix",
    )(x, norm_g, w_in, lb, hg_norm_g, conv_w, w_a, w_b, w_o)


def _stair_pairs():
    pairs = [(a, c) for a in range(PEER_TOPK) for c in range(PEER_TOPK) if (a + 1) * (c + 1) <= PEER_TOPK]
    rows = -(-len(pairs) // 8) * 8
    pairs = pairs + [(-1, -1)] * (rows - len(pairs))
    arr = jnp.asarray(pairs, _f32)
    return arr[:, 0:1], arr[:, 1:2]


def _route_kernel(x_ref, g_ref, wq_ref, k1_ref, k2_ref, ca_ref, cc_ref, h_ref, idx_ref, gate_ref,
                  idx_t, e_t):
    tr = x_ref.shape[0]
    n_cand = ca_ref.shape[0]
    x = x_ref[...]
    h = x * lax.rsqrt(jnp.mean(x * x, axis=-1, keepdims=True) + EPS) * g_ref[...]
    h_ref[...] = h
    hb = h.astype(_bf16)

    key_row = _iota((PEER_NKEYS, tr), 0).astype(_f32)
    cand_row = _iota((n_cand, tr), 0).astype(_f32)
    ca = ca_ref[...]
    cc = cc_ref[...]
    neg_inf = jnp.float32(-jnp.inf)

    def extract_max(s):
        m = jnp.max(s, axis=0, keepdims=True)
        i = jnp.min(jnp.where(s == m, key_row, float(PEER_NKEYS)), axis=0, keepdims=True)
        return m, i, jnp.where(key_row == i, neg_inf, s)

    def head_body(hd, carry):
        q1 = _dot(hb, wq_ref[hd, 0]).astype(_bf16)
        q2 = _dot(hb, wq_ref[hd, 1]).astype(_bf16)
        s1 = _dot_nt(k1_ref[hd], q1)
        s2 = _dot_nt(k2_ref[hd], q2)

        def half_body(k, c):
            s1, s2, a_s, a_i, c_s, c_i = c
            kf = jnp.asarray(k, _f32)
            m1, i1, s1 = extract_max(s1)
            m2, i2, s2 = extract_max(s2)
            sel_a = ca == kf
            sel_c = cc == kf
            return (s1, s2, jnp.where(sel_a, m1, a_s), jnp.where(sel_a, i1, a_i),
                    jnp.where(sel_c, m2, c_s), jnp.where(sel_c, i2, c_i))

        zeros = jnp.zeros((n_cand, tr), _f32)
        _, _, a_s, a_i, c_s, c_i = lax.fori_loop(
            0, PEER_TOPK, half_body, (s1, s2, jnp.full((n_cand, tr), neg_inf), zeros, zeros, zeros))
        cand_i = a_i * float(PEER_NKEYS) + c_i

        def pick_body(k, c):
            cand_s, denom, m_first = c
            m = jnp.max(cand_s, axis=0, keepdims=True)
            pos = jnp.min(jnp.where(cand_s == m, cand_row, float(n_cand)), axis=0, keepdims=True)
            hit = cand_row == pos
            eid = jnp.max(jnp.where(hit, cand_i, -1.0), axis=0, keepdims=True)
            m_first = jnp.where(k == 0, m, m_first)
            e = jnp.exp(m - m_first)
            slot = hd * PEER_TOPK + k
            idx_t[pl.ds(slot, 1), :] = eid
            e_t[pl.ds(slot, 1), :] = e
            return jnp.where(hit, neg_inf, cand_s), denom + e, m_first

        zero_row = jnp.zeros((1, tr), _f32)
        _, denom, _ = lax.fori_loop(0, PEER_TOPK, pick_body, (a_s + c_s, zero_row, zero_row))
        rows = pl.ds(pl.multiple_of(hd * PEER_TOPK, PEER_TOPK), PEER_TOPK)
        e_t[rows, :] = e_t[rows, :] / denom
        return carry

    lax.fori_loop(0, PEER_HEADS, head_body, 0)
    idx_ref[...] = idx_t[...].T.astype(jnp.int32)
    gate_ref[...] = e_t[...].T


def _route(x, norm_g, w_query, keys1, keys2, *, tr):
    t, d = x.shape
    ca, cc = _stair_pairs()
    ca = jnp.broadcast_to(ca, (ca.shape[0], tr))
    cc = jnp.broadcast_to(cc, (cc.shape[0], tr))
    return pl.pallas_call(
        _route_kernel,
        grid=(t // tr,),
        in_specs=[
            pl.BlockSpec((tr, d), lambda i: (i, 0)),
            _const_spec((1, d)),
            _const_spec(w_query.shape),
            _const_spec(keys1.shape),
            _const_spec(keys2.shape),
            _const_spec(ca.shape),
            _const_spec(cc.shape),
        ],
        out_specs=[
            pl.BlockSpec((tr, d), lambda i: (i, 0)),
            pl.BlockSpec((tr, PEER_SLOTS), lambda i: (i, 0)),
            pl.BlockSpec((tr, PEER_SLOTS), lambda i: (i, 0)),
        ],
        out_shape=[
            jax.ShapeDtypeStruct((t, d), _f32),
            jax.ShapeDtypeStruct((t, PEER_SLOTS), jnp.int32),
            jax.ShapeDtypeStruct((t, PEER_SLOTS), _f32),
        ],
        scratch_shapes=[
            pltpu.VMEM((PEER_SLOTS, tr), _f32),
            pltpu.VMEM((PEER_SLOTS, tr), _f32),
        ],
        compiler_params=pltpu.CompilerParams(
            dimension_semantics=("parallel",),
            vmem_limit_bytes=VMEM_LIMIT_BYTES),
        name="route",
    )(x, norm_g, w_query, keys1, keys2, ca, cc)


SC_CORES = 2
SC_SUBCORES = 16
SC_LANES = 16
SC_WORKERS = SC_CORES * SC_SUBCORES
SC_TOKENS = 8
SC_GATHER = 32
SC_BLOCK = 8
SC_UNROLL = 2
SC_COLS = 8
HI_MASK = -65536


def _sc_mesh():
    return plsc.VectorSubcoreMesh(core_axis_name="c", subcore_axis_name="s")


def _sc_worker_base(per_worker):
    return (lax.axis_index("s") * SC_CORES + lax.axis_index("c")) * per_worker


def _sc_gather_loop(tab_hbm, idx_v, bufs, consume):
    n_parts = PEER_SLOTS // SC_GATHER
    n_gathers = SC_TOKENS * n_parts

    def gather(g, parity):
        rows, sem = bufs[parity]
        i = g // n_parts
        col = pl.multiple_of((g % n_parts) * SC_GATHER, SC_GATHER)
        return pltpu.make_async_copy(tab_hbm.at[idx_v.at[i, pl.ds(col, SC_GATHER)]], rows, sem)

    gather(0, 0).start()

    @pl.loop(0, n_gathers // 2)
    def _(pair):
        g = 2 * pair
        gather(g + 1, 1).start()
        gather(g, 0).wait()
        consume(g // n_parts, g % n_parts, bufs[0][0])

        @pl.when(g + 2 < n_gathers)
        def _():
            gather(g + 2, 0).start()

        gather(g + 1, 1).wait()
        consume((g + 1) // n_parts, (g + 1) % n_parts, bufs[1][0])


def _pack_halves(tab):
    half = tab.shape[1] // 2
    bits = lax.bitcast_convert_type(tab.astype(_bf16), jnp.uint16).astype(jnp.uint32)
    return lax.bitcast_convert_type(bits[:, :half] | (bits[:, half:] << 16), jnp.int32)


def _sc_unpack(words):
    lo = lax.bitcast_convert_type(words << 16, _f32)
    hi = lax.bitcast_convert_type(words & HI_MASK, _f32)
    return lo, hi


def _sc_udot(idx, tab, h):
    t, d = h.shape
    per_worker = t // SC_WORKERS
    half = d // 2

    @functools.partial(
        pl.kernel, mesh=_sc_mesh(),
        out_type=jax.ShapeDtypeStruct((t, PEER_SLOTS), _f32),
        scratch_types=[
            pltpu.VMEM((SC_TOKENS, PEER_SLOTS), jnp.int32),
            pltpu.VMEM((SC_TOKENS, d), _f32),
            pltpu.VMEM((SC_TOKENS, PEER_SLOTS), _f32),
            pltpu.VMEM((SC_GATHER, half), jnp.int32),
            pltpu.VMEM((SC_GATHER, half), jnp.int32),
            pltpu.SemaphoreType.DMA,
            pltpu.SemaphoreType.DMA,
        ],
        compiler_params=pltpu.CompilerParams(needs_layout_passes=False),
        name="sc_udot",
    )
    def body(idx_hbm, tab_hbm, h_hbm, out_hbm, idx_v, h_v, acts_v, rows0, rows1, sem0, sem1):
        base = _sc_worker_base(per_worker)
        lane = lax.iota(jnp.int32, SC_LANES)

        def dots(i, part, rows):
            def block_body(blk, carry):
                row0 = blk * SC_LANES
                outv = jnp.zeros((SC_LANES,), _f32)
                for sub in range(SC_LANES // SC_BLOCK):
                    def chunk_body(jj, accs):
                        accs = list(accs)
                        for u in range(SC_UNROLL):
                            off = pl.multiple_of((jj * SC_UNROLL + u) * SC_LANES, SC_LANES)
                            h_lo = h_v[i, pl.ds(off, SC_LANES)]
                            h_hi = h_v[i, pl.ds(half + off, SC_LANES)]
                            for e in range(SC_BLOCK):
                                lo, hi = _sc_unpack(rows[row0 + sub * SC_BLOCK + e, pl.ds(off, SC_LANES)])
                                accs[e] = accs[e] + (lo * h_lo + hi * h_hi)
                        return tuple(accs)

                    accs = lax.fori_loop(0, half // SC_LANES // SC_UNROLL, chunk_body,
                                         tuple(jnp.zeros((SC_LANES,), _f32) for _ in range(SC_BLOCK)))
                    for e in range(SC_BLOCK):
                        outv = jnp.where(lane == sub * SC_BLOCK + e, jnp.sum(accs[e]), outv)
                col = pl.multiple_of(part * SC_GATHER + row0, SC_LANES)
                acts_v[i, pl.ds(col, SC_LANES)] = outv
                return carry

            lax.fori_loop(0, SC_GATHER // SC_LANES, block_body, 0)

        @pl.loop(0, per_worker // SC_TOKENS)
        def _(step):
            tok = pl.multiple_of(base + step * SC_TOKENS, SC_TOKENS)
            pltpu.sync_copy(idx_hbm.at[pl.ds(tok, SC_TOKENS)], idx_v)
            pltpu.sync_copy(h_hbm.at[pl.ds(tok, SC_TOKENS)], h_v)
            _sc_gather_loop(tab_hbm, idx_v, ((rows0, sem0), (rows1, sem1)), dots)
            pltpu.sync_copy(acts_v, out_hbm.at[pl.ds(tok, SC_TOKENS)])

    return body(idx, tab, h)


def _sc_vaxpy(idx, w, tab, *, first, count):
    half = tab.shape[1]
    d = 2 * half
    per_worker = count // SC_WORKERS
    span = SC_COLS * SC_LANES

    @functools.partial(
        pl.kernel, mesh=_sc_mesh(),
        out_type=jax.ShapeDtypeStruct((count, d), _f32),
        scratch_types=[
            pltpu.VMEM((SC_TOKENS, PEER_SLOTS), jnp.int32),
            pltpu.VMEM((SC_TOKENS, PEER_SLOTS), _f32),
            pltpu.VMEM((SC_TOKENS, d), _f32),
            pltpu.VMEM((SC_GATHER, half), jnp.int32),
            pltpu.VMEM((SC_GATHER, half), jnp.int32),
            pltpu.SemaphoreType.DMA,
            pltpu.SemaphoreType.DMA,
        ],
        compiler_params=pltpu.CompilerParams(needs_layout_passes=False),
        name="sc_vaxpy",
    )
    def body(idx_hbm, w_hbm, tab_hbm, out_hbm, idx_v, w_v, out_v, rows0, rows1, sem0, sem1):
        base = _sc_worker_base(per_worker)

        def accumulate(i, part, rows):
            i_vec = jnp.full((SC_LANES,), i, jnp.int32)

            def span_body(cq, carry):
                def cols(c, offset=0):
                    return pl.ds(pl.multiple_of(offset + cq * span + c * SC_LANES, SC_LANES), SC_LANES)

                def expert_body(e, accs):
                    k_vec = jnp.full((SC_LANES,), part * SC_GATHER + e, jnp.int32)
                    wv = plsc.load_gather(w_v, [i_vec, k_vec])
                    new = []
                    for c in range(SC_COLS):
                        lo, hi = _sc_unpack(rows[e, cols(c)])
                        new += [accs[2 * c] + lo * wv, accs[2 * c + 1] + hi * wv]
                    return tuple(new)

                init = []
                for c in range(SC_COLS):
                    init += [out_v[i, cols(c)], out_v[i, cols(c, half)]]
                accs = lax.fori_loop(0, SC_GATHER, expert_body, tuple(init))
                for c in range(SC_COLS):
                    out_v[i, cols(c)] = accs[2 * c]
                    out_v[i, cols(c, half)] = accs[2 * c + 1]
                return carry

            lax.fori_loop(0, half // span, span_body, 0)

        @pl.loop(0, per_worker // SC_TOKENS)
        def _(step):
            off = pl.multiple_of(base + step * SC_TOKENS, SC_TOKENS)
            pltpu.sync_copy(idx_hbm.at[pl.ds(first + off, SC_TOKENS)], idx_v)
            pltpu.sync_copy(w_hbm.at[pl.ds(first + off, SC_TOKENS)], w_v)

            @pl.loop(0, SC_TOKENS)
            def _(i):
                @pl.loop(0, d // SC_LANES)
                def _(j):
                    out_v[i, pl.ds(pl.multiple_of(j * SC_LANES, SC_LANES), SC_LANES)] = (
                        jnp.zeros((SC_LANES,), _f32))

            _sc_gather_loop(tab_hbm, idx_v, ((rows0, sem0), (rows1, sem1)), accumulate)
            pltpu.sync_copy(out_v, out_hbm.at[pl.ds(off, SC_TOKENS)])

    return body(idx, w, tab)


ROW_CHUNKS = 8
PACK_ROWS = ROW_CHUNKS // 2


def _pack_table(tab):
    n, d = tab.shape
    bits = lax.bitcast_convert_type(tab.astype(_bf16), jnp.uint16).astype(jnp.uint32)
    bits = bits.reshape(n, PACK_ROWS, 2, d // ROW_CHUNKS)
    word = bits[:, :, 0, :] | (bits[:, :, 1, :] << 16)
    return lax.bitcast_convert_type(word, jnp.int32)


def _gate_weights_kernel(acts_ref, gate_ref, w_ref):
    a = acts_ref[...]
    gelu = 0.5 * a * (1.0 + lax.erf(a * (2.0 ** -0.5)))
    w_ref[...] = gate_ref[...] * gelu


def _gate_weights(acts, gate, *, tw):
    t, n = acts.shape
    spec = pl.BlockSpec((tw, n), lambda i: (i, 0))
    return pl.pallas_call(
        _gate_weights_kernel,
        grid=(t // tw,),
        in_specs=[spec, spec],
        out_specs=spec,
        out_shape=jax.ShapeDtypeStruct((t, n), _f32),
        compiler_params=pltpu.CompilerParams(dimension_semantics=("parallel",)),
        name="gate_weights",
    )(acts, gate)


def _gather_rows(idx_ref, t, tab_ref, rows_ref):
    for k in range(PEER_SLOTS):
        rows_ref[k * PACK_ROWS:(k + 1) * PACK_ROWS, :] = tab_ref[idx_ref[t, k]]


def _rows_matrix(rows_ref):
    return pltpu.bitcast(rows_ref[...], _bf16)


def _token_loop(tb, idx_ref, tab_ref, rows_a, rows_b, compute):
    _gather_rows(idx_ref, 0, tab_ref, rows_a)

    def pair_body(i, carry):
        t0 = 2 * i
        _gather_rows(idx_ref, t0 + 1, tab_ref, rows_b)
        compute(t0, _rows_matrix(rows_a))
        _gather_rows(idx_ref, jnp.minimum(t0 + 2, tb - 1), tab_ref, rows_a)
        compute(t0 + 1, _rows_matrix(rows_b))
        return carry

    lax.fori_loop(0, tb // 2, pair_body, 0)


def _chunk_diag_mask():
    shape = (ROW_CHUNKS, PEER_SLOTS * ROW_CHUNKS)
    return (_iota(shape, 1) % ROW_CHUNKS) == _iota(shape, 0)


def _vaxpy_kernel(idx_ref, w_ref, tab_ref, out_ref, rows_a, rows_b, wrep_ref):
    tb = out_ref.shape[0]
    diag = _chunk_diag_mask()
    shape = (PEER_SLOTS, PEER_SLOTS * ROW_CHUNKS)
    spread = (_iota(shape, 0) == (_iota(shape, 1) // ROW_CHUNKS)).astype(_bf16)
    wrep_ref[...] = _dot_exact_rhs01(w_ref[...], spread)

    def compute(t, m):
        w_row = jnp.broadcast_to(wrep_ref[pl.ds(t, 1), :], diag.shape)
        w_hi, w_lo = _split2(jnp.where(diag, w_row, 0.0))
        out_ref[t] = _dot(w_hi, m) + _dot(w_lo, m)

    _token_loop(tb, idx_ref, tab_ref, rows_a, rows_b, compute)


def _vaxpy(idx, w, tab, *, tb, count):
    return pl.pallas_call(
        _vaxpy_kernel,
        grid=(count // tb,),
        in_specs=[
            pl.BlockSpec((tb, PEER_SLOTS), lambda i: (i, 0), memory_space=pltpu.SMEM),
            pl.BlockSpec((tb, PEER_SLOTS), lambda i: (i, 0)),
            _const_spec(tab.shape),
        ],
        out_specs=pl.BlockSpec((tb, ROW_CHUNKS, 128), lambda i: (i, 0, 0)),
        out_shape=jax.ShapeDtypeStruct((count, ROW_CHUNKS, 128), _f32),
        scratch_shapes=[
            pltpu.VMEM((PEER_SLOTS * PACK_ROWS, 128), jnp.int32),
            pltpu.VMEM((PEER_SLOTS * PACK_ROWS, 128), jnp.int32),
            pltpu.VMEM((tb, PEER_SLOTS * ROW_CHUNKS), _f32),
        ],
        compiler_params=pltpu.CompilerParams(
            dimension_semantics=("parallel",),
            vmem_limit_bytes=VMEM_LIMIT_BYTES),
        name="vaxpy",
    )(idx, w, tab)


def _final_kernel(x_ref, p_ref, g_ref, out_ref, *, normalize):
    x = x_ref[...] + p_ref[...]
    if normalize:
        x = x * lax.rsqrt(jnp.mean(x * x, axis=-1, keepdims=True) + EPS) * g_ref[...]
    out_ref[...] = x


def _final(x, peer, g, *, normalize, tf, first):
    t, d = peer.shape
    off = first // tf
    return pl.pallas_call(
        functools.partial(_final_kernel, normalize=normalize),
        grid=(t // tf,),
        in_specs=[
            pl.BlockSpec((tf, d), lambda i: (i + off, 0)),
            pl.BlockSpec((tf, d), lambda i: (i, 0)),
            _const_spec((1, d)),
        ],
        out_specs=pl.BlockSpec((tf, d), lambda i: (i, 0)),
        out_shape=jax.ShapeDtypeStruct((t, d), _f32),
        compiler_params=pltpu.CompilerParams(dimension_semantics=("parallel",)),
        name="final_norm",
    )(x, peer, g)


def kernel(x, norm_mix_g, w_in, hg_lb_logits, hg_out_norm_g, conv_w, w_branch_hg, w_branch_conv, w_out, norm_ffn_g, peer_w_query, peer_keys1, peer_keys2, peer_u, peer_v, norm_final_g):
    b_, s_, d = x.shape
    depth = w_in.shape[0]
    lb_all = jnp.cumsum(jax.nn.softmax(hg_lb_logits.astype(_f32), axis=0), axis=0)
    n_groups = BATCH_GROUPS if b_ % BATCH_GROUPS == 0 else 1
    bg = b_ // n_groups
    tg = bg * s_
    n_sc = SC_SHARE if SC_SHARE < tg else 0
    n_tc = tg - n_sc
    for l in range(depth):
        wq = peer_w_query[l].astype(_bf16).reshape(d, PEER_HEADS, 2, PEER_HALF).transpose(1, 2, 0, 3)
        v_tab = _pack_table(peer_v[l])
        u_sc = _pack_halves(peer_u[l])
        v_sc = _pack_halves(peer_v[l])
        last = l == depth - 1
        g = norm_final_g[None] if last else jnp.ones((1, d), _f32)
        def front(c, x_in):
            xc = _mix(x_in, norm_mix_g[l][None], w_in[l].astype(_bf16), lb_all[l][None],
                      hg_out_norm_g[l][None], conv_w[l], w_branch_hg[l].astype(_bf16),
                      w_branch_conv[l].astype(_bf16), w_out[l].astype(_bf16), ts=MIX_TILE)
            xf = xc.reshape(tg, d)
            h, idx, gate = _route(xf, norm_ffn_g[l][None], wq,
                                  peer_keys1[l].astype(_bf16), peer_keys2[l].astype(_bf16), tr=ROUTE_TILE)
            w = _gate_weights(_sc_udot(idx, u_sc, h), gate, tw=FINAL_TILE)
            return xf, idx, w

        def back(xf, idx, w):
            peer_tc = _vaxpy(idx, w, v_tab, tb=EXPERT_TILE, count=n_tc).reshape(n_tc, d)
            parts = [_final(xf, peer_tc, g, normalize=last, tf=FINAL_TILE, first=0)]
            if n_sc:
                peer_sc = _sc_vaxpy(idx, w, v_sc, first=n_tc, count=n_sc)
                parts.append(_final(xf, peer_sc, g, normalize=last, tf=FINAL_TILE, first=n_tc))
            return peer_tc, parts

        groups = [x[c * bg:(c + 1) * bg] for c in range(n_groups)]
        fronts = {c: front(c, groups[c]) for c in range(min(GROUPS_AHEAD + 1, n_groups))}
        outs = []
        for c in range(n_groups):
            xf, idx, w = fronts.pop(c)
            if c + GROUPS_AHEAD in fronts:
                w = lax.optimization_barrier((w, fronts[c + GROUPS_AHEAD][1]))[0]
            peer_tc, parts = back(xf, idx, w)
            outs.extend(parts)
            nxt = c + GROUPS_AHEAD + 1
            if nxt < n_groups:
                fronts[nxt] = front(nxt, lax.optimization_barrier((groups[nxt], peer_tc))[0])
        x = jnp.concatenate(outs, axis=0).reshape(b_, s_, d)
    return x
```

```python
import functools

import jax
import jax.numpy as jnp
from jax import lax
from jax.experimental import pallas as pl
from jax.experimental.pallas import tpu as pltpu
from jax.experimental.pallas import tpu_sc as plsc

EPS = 1e-6
CHUNK = 64
SUB = 16
HEADS = 8
HEAD_DIM = 64
HG_WIDTH = HEADS * HEAD_DIM
GROUP = 256
N_GROUPS = HG_WIDTH // GROUP
CONV_K = 3
PEER_HEADS = 8
PEER_NKEYS = 128
PEER_HALF = 128
PEER_TOPK = 16
PEER_SLOTS = PEER_HEADS * PEER_TOPK

VMEM_LIMIT_BYTES = 56 * 1024 * 1024

MIX_TILE = 256
ROUTE_TILE = 256
EXPERT_TILE = 128
FINAL_TILE = 512
BATCH_GROUPS = 4
SC_SHARE = 3072
GROUPS_AHEAD = 2

_f32 = jnp.float32
_bf16 = jnp.bfloat16


def _dot(a, b):
    return jnp.dot(a, b, preferred_element_type=_f32)


def _dot_nt(a, b):
    return lax.dot_general(a, b, (((1,), (1,)), ((), ())), preferred_element_type=_f32)


def _dot_tn(a, b):
    return lax.dot_general(a, b, (((0,), (0,)), ((), ())), preferred_element_type=_f32)


def _split3(x):
    hi = x.astype(_bf16)
    r1 = x - hi.astype(_f32)
    mid = r1.astype(_bf16)
    lo = (r1 - mid.astype(_f32)).astype(_bf16)
    return hi, mid, lo


def _split2(x):
    hi = x.astype(_bf16)
    lo = (x - hi.astype(_f32)).astype(_bf16)
    return hi, lo


def _dot_exact_rhs01(x, m01):
    hi, mid, lo = _split3(x)
    return _dot(hi, m01) + _dot(mid, m01) + _dot(lo, m01)


def _dot_exact_lhs01(m01, x):
    hi, mid, lo = _split3(x)
    return _dot(m01, hi) + _dot(m01, mid) + _dot(m01, lo)


def _iota(shape, dim):
    return lax.broadcasted_iota(jnp.int32, shape, dim)


def _hgrn2_chunk(q, k, lf, v, state_ref):
    n_sub = CHUNK // SUB
    row = _iota((CHUNK, CHUNK), 0)
    col = _iota((CHUNK, CHUNK), 1)
    tril = (col <= row).astype(_bf16)
    b = _dot_exact_lhs01(tril, lf)

    b_end = [b[(j + 1) * SUB - 1:(j + 1) * SUB, :] for j in range(n_sub)]
    b_end_rows = jnp.concatenate([jnp.broadcast_to(e, (SUB, HG_WIDTH)) for e in b_end], axis=0)
    b_last = b_end[-1]

    q_in = (q * jnp.exp(b)).astype(_bf16)
    k_sub = (k * jnp.exp(b_end_rows - b)).astype(_bf16)
    k_out = (k * jnp.exp(b_last - b)).astype(_bf16)
    q_from = [(q * jnp.exp(jnp.minimum(b - b_end[j], 0.0))).astype(_bf16) for j in range(n_sub - 1)]
    v_b = v.astype(_bf16)

    gr = _iota((GROUP, GROUP), 0) // HEAD_DIM
    gc = _iota((GROUP, GROUP), 1) // HEAD_DIM
    head_mask = gr == gc
    t_blk = _iota((CHUNK, GROUP), 0) // SUB
    s_blk = (_iota((CHUNK, GROUP), 1) % HEAD_DIM) // SUB

    outs = []
    for g in range(N_GROUPS):
        sl = slice(g * GROUP, (g + 1) * GROUP)
        st = state_ref[g]
        o_g = _dot_nt(q_in[:, sl], st.astype(_bf16))

        zero_b = jnp.zeros((), _bf16)
        k_bd = jnp.where(head_mask, jnp.concatenate([k_sub[:, sl]] * (GROUP // CHUNK), axis=0), zero_b)
        v_bd = jnp.where(head_mask, jnp.concatenate([v_b[:, sl]] * (GROUP // CHUNK), axis=0), zero_b)
        q_stack = jnp.concatenate([qf[:, sl] for qf in q_from], axis=0)
        r = _dot_nt(q_stack, k_bd)
        scores = jnp.zeros((CHUNK, GROUP), _f32)
        for j in range(n_sub - 1):
            sel = (s_blk == j) & (t_blk > j)
            scores = jnp.where(sel, r[j * CHUNK:(j + 1) * CHUNK, :], scores)
        o_g = o_g + _dot(scores.astype(_bf16), v_bd)
        outs.append(o_g)

        upd = _dot_tn(v_b[:, sl], k_out[:, sl])
        decay = jnp.exp(b_last[:, sl])
        state_ref[g] = st * decay + jnp.where(head_mask, upd, 0.0)
    o = jnp.concatenate(outs, axis=1)

    ones_bd = ((_iota((HG_WIDTH, HG_WIDTH), 0) // HEAD_DIM)
               == (_iota((HG_WIDTH, HG_WIDTH), 1) // HEAD_DIM)).astype(_bf16)
    t_in_sub = _iota((CHUNK, HG_WIDTH), 0) % SUB
    for lag in range(SUB):
        if lag == 0:
            p = q * k
            v_l = v
        else:
            valid = t_in_sub >= lag
            k_l = pltpu.roll(k, lag, 0)
            b_l = pltpu.roll(b, lag, 0)
            v_l = pltpu.roll(v, lag, 0)
            p = jnp.where(valid, q * k_l * jnp.exp(jnp.minimum(b - b_l, 0.0)), 0.0)
        s_l = _dot(p.astype(_bf16), ones_bd)
        o = o + s_l * v_l
    return o


def _mix_kernel(x_ref, g_ref, win_ref, lb_ref, hgn_ref, convw_ref, pa_ref, pb_ref, wo_ref,
                out_ref, state_ref, carry_ref, q_s, k_s, lf_s, v_s, o_s):
    ts = x_ref.shape[0]
    d_model = x_ref.shape[1]
    w = HG_WIDTH

    @pl.when(pl.program_id(1) == 0)
    def _():
        state_ref[...] = jnp.zeros_like(state_ref)
        carry_ref[...] = jnp.zeros_like(carry_ref)

    x = x_ref[...]
    h = x * lax.rsqrt(jnp.mean(x * x, axis=-1, keepdims=True) + EPS) * g_ref[...]
    hb = h.astype(_bf16)

    def proj(i, width=w):
        return _dot(hb, win_ref[:, i * w:i * w + width])

    lb = lb_ref[...]
    q_s[...] = jax.nn.silu(proj(0)) * (HEAD_DIM ** -0.5)
    forget = lb + (1.0 - lb) * jax.nn.sigmoid(proj(1))
    k_s[...] = 1.0 - forget
    lf_s[...] = jnp.log(forget)
    v_s[...] = proj(2)

    def chunk_body(c, carry):
        rows = pl.ds(pl.multiple_of(c * CHUNK, CHUNK), CHUNK)
        o_s[rows, :] = _hgrn2_chunk(q_s[rows, :], k_s[rows, :], lf_s[rows, :], v_s[rows, :], state_ref)
        return carry

    lax.fori_loop(0, ts // CHUNK, chunk_body, 0)

    o = o_s[...]
    ones_bd = ((_iota((w, w), 0) // HEAD_DIM) == (_iota((w, w), 1) // HEAD_DIM)).astype(_bf16)
    ms = _dot_exact_rhs01(o * o, ones_bd) * (1.0 / HEAD_DIM)
    o = o * lax.rsqrt(ms + EPS) * hgn_ref[...]
    y_a = (o * jax.nn.silu(proj(3))).astype(_bf16)

    u = proj(5) * proj(6)
    prev = carry_ref[...]
    rowi = _iota((ts, w), 0)
    u1 = jnp.where(rowi >= 1, pltpu.roll(u, 1, 0), jnp.broadcast_to(prev[7:8, :], (ts, w)))
    u2 = jnp.where(rowi >= 2, pltpu.roll(u, 2, 0),
                   jnp.where(rowi == 1, jnp.broadcast_to(prev[7:8, :], (ts, w)),
                             jnp.broadcast_to(prev[6:7, :], (ts, w))))
    carry_ref[...] = u[ts - 8:, :]
    cw = convw_ref[...]
    y_b = (proj(4) * (cw[0:1, :] * u2 + cw[1:2, :] * u1 + cw[2:3, :] * u)).astype(_bf16)

    g_a = jax.nn.sigmoid(proj(7, d_model))
    g_b = jax.nn.sigmoid(_dot(hb, win_ref[:, 7 * w + d_model:7 * w + 2 * d_model]))
    merged = g_a * _dot(y_a, pa_ref[...]) + g_b * _dot(y_b, pb_ref[...])
    out_ref[...] = x + _dot(merged.astype(_bf16), wo_ref[...])


def _const_spec(shape):
    nd = len(shape)
    return pl.BlockSpec(shape, lambda *_: (0,) * nd, pipeline_mode=pl.Buffered(1))


def _mix(x, norm_g, w_in, lb, hg_norm_g, conv_w, w_a, w_b, w_o, *, ts):
    b_, s_, d = x.shape
    in_cols = w_in.shape[1]
    w = HG_WIDTH
    grid = (b_, s_ // ts)
    return pl.pallas_call(
        _mix_kernel,
        grid=grid,
        in_specs=[
            pl.BlockSpec((None, ts, d), lambda b, s: (b, s, 0)),
            _const_spec((1, d)),
            _const_spec((d, in_cols)),
            _const_spec((1, w)),
            _const_spec((1, w)),
            _const_spec((CONV_K, w)),
            _const_spec((w, d)),
            _const_spec((w, d)),
            _const_spec((d, d)),
        ],
        out_specs=pl.BlockSpec((None, ts, d), lambda b, s: (b, s, 0)),
        out_shape=jax.ShapeDtypeStruct((b_, s_, d), _f32),
        scratch_shapes=[
            pltpu.VMEM((N_GROUPS, GROUP, GROUP), _f32),
            pltpu.VMEM((8, w), _f32),
            pltpu.VMEM((ts, w), _f32),
            pltpu.VMEM((ts, w), _f32),
            pltpu.VMEM((ts, w), _f32),
            pltpu.VMEM((ts, w), _f32),
            pltpu.VMEM((ts, w), _f32),
        ],
        compiler_params=pltpu.CompilerParams(
            dimension_semantics=("parallel", "arbitrary"),
            vmem_limit_bytes=VMEM_LIMIT_BYTES),
        name="mix",
    )(x, norm_g, w_in, lb, hg_norm_g, conv_w, w_a, w_b, w_o)


def _stair_pairs():
    pairs = [(a, c) for a in range(PEER_TOPK) for c in range(PEER_TOPK) if (a + 1) * (c + 1) <= PEER_TOPK]
    rows = -(-len(pairs) // 8) * 8
    pairs = pairs + [(-1, -1)] * (rows - len(pairs))
    arr = jnp.asarray(pairs, _f32)
    return arr[:, 0:1], arr[:, 1:2]


def _route_kernel(x_ref, g_ref, wq_ref, k1_ref, k2_ref, ca_ref, cc_ref, h_ref, idx_ref, gate_ref,
                  idx_t, e_t):
    tr = x_ref.shape[0]
    n_cand = ca_ref.shape[0]
    x = x_ref[...]
    h = x * lax.rsqrt(jnp.mean(x * x, axis=-1, keepdims=True) + EPS) * g_ref[...]
    h_ref[...] = h
    hb = h.astype(_bf16)

    key_row = _iota((PEER_NKEYS, tr), 0).astype(_f32)
    cand_row = _iota((n_cand, tr), 0).astype(_f32)
    ca = ca_ref[...]
    cc = cc_ref[...]
    neg_inf = jnp.float32(-jnp.inf)

    def extract_max(s):
        m = jnp.max(s, axis=0, keepdims=True)
        i = jnp.min(jnp.where(s == m, key_row, float(PEER_NKEYS)), axis=0, keepdims=True)
        return m, i, jnp.where(key_row == i, neg_inf, s)

    def head_body(hd, carry):
        q1 = _dot(hb, wq_ref[hd, 0]).astype(_bf16)
        q2 = _dot(hb, wq_ref[hd, 1]).astype(_bf16)
        s1 = _dot_nt(k1_ref[hd], q1)
        s2 = _dot_nt(k2_ref[hd], q2)

        def half_body(k, c):
            s1, s2, a_s, a_i, c_s, c_i = c
            kf = jnp.asarray(k, _f32)
            m1, i1, s1 = extract_max(s1)
            m2, i2, s2 = extract_max(s2)
            sel_a = ca == kf
            sel_c = cc == kf
            return (s1, s2, jnp.where(sel_a, m1, a_s), jnp.where(sel_a, i1, a_i),
                    jnp.where(sel_c, m2, c_s), jnp.where(sel_c, i2, c_i))

        zeros = jnp.zeros((n_cand, tr), _f32)
        _, _, a_s, a_i, c_s, c_i = lax.fori_loop(
            0, PEER_TOPK, half_body, (s1, s2, jnp.full((n_cand, tr), neg_inf), zeros, zeros, zeros))
        cand_i = a_i * float(PEER_NKEYS) + c_i

        def pick_body(k, c):
            cand_s, denom, m_first = c
            m = jnp.max(cand_s, axis=0, keepdims=True)
            pos = jnp.min(jnp.where(cand_s == m, cand_row, float(n_cand)), axis=0, keepdims=True)
            hit = cand_row == pos
            eid = jnp.max(jnp.where(hit, cand_i, -1.0), axis=0, keepdims=True)
            m_first = jnp.where(k == 0, m, m_first)
            e = jnp.exp(m - m_first)
            slot = hd * PEER_TOPK + k
            idx_t[pl.ds(slot, 1), :] = eid
            e_t[pl.ds(slot, 1), :] = e
            return jnp.where(hit, neg_inf, cand_s), denom + e, m_first

        zero_row = jnp.zeros((1, tr), _f32)
        _, denom, _ = lax.fori_loop(0, PEER_TOPK, pick_body, (a_s + c_s, zero_row, zero_row))
        rows = pl.ds(pl.multiple_of(hd * PEER_TOPK, PEER_TOPK), PEER_TOPK)
        e_t[rows, :] = e_t[rows, :] / denom
        return carry

    lax.fori_loop(0, PEER_HEADS, head_body, 0)
    idx_ref[...] = idx_t[...].T.astype(jnp.int32)
    gate_ref[...] = e_t[...].T


def _route(x, norm_g, w_query, keys1, keys2, *, tr):
    t, d = x.shape
    ca, cc = _stair_pairs()
    ca = jnp.broadcast_to(ca, (ca.shape[0], tr))
    cc = jnp.broadcast_to(cc, (cc.shape[0], tr))
    return pl.pallas_call(
        _route_kernel,
        grid=(t // tr,),
        in_specs=[
            pl.BlockSpec((tr, d), lambda i: (i, 0)),
            _const_spec((1, d)),
            _const_spec(w_query.shape),
            _const_spec(keys1.shape),
            _const_spec(keys2.shape),
            _const_spec(ca.shape),
            _const_spec(cc.shape),
        ],
        out_specs=[
            pl.BlockSpec((tr, d), lambda i: (i, 0)),
            pl.BlockSpec((tr, PEER_SLOTS), lambda i: (i, 0)),
            pl.BlockSpec((tr, PEER_SLOTS), lambda i: (i, 0)),
        ],
        out_shape=[
            jax.ShapeDtypeStruct((t, d), _f32),
            jax.ShapeDtypeStruct((t, PEER_SLOTS), jnp.int32),
            jax.ShapeDtypeStruct((t, PEER_SLOTS), _f32),
        ],
        scratch_shapes=[
            pltpu.VMEM((PEER_SLOTS, tr), _f32),
            pltpu.VMEM((PEER_SLOTS, tr), _f32),
        ],
        compiler_params=pltpu.CompilerParams(
            dimension_semantics=("parallel",),
            vmem_limit_bytes=VMEM_LIMIT_BYTES),
        name="route",
    )(x, norm_g, w_query, keys1, keys2, ca, cc)


SC_CORES = 2
SC_SUBCORES = 16
SC_LANES = 16
SC_WORKERS = SC_CORES * SC_SUBCORES
SC_TOKENS = 8
SC_GATHER = 32
SC_BLOCK = 8
SC_UNROLL = 2
SC_COLS = 8
HI_MASK = -65536


def _sc_mesh():
    return plsc.VectorSubcoreMesh(core_axis_name="c", subcore_axis_name="s")


def _sc_worker_base(per_worker):
    return (lax.axis_index("s") * SC_CORES + lax.axis_index("c")) * per_worker


def _sc_gather_loop(tab_hbm, idx_v, bufs, consume):
    n_parts = PEER_SLOTS // SC_GATHER
    n_gathers = SC_TOKENS * n_parts

    def gather(g, parity):
        rows, sem = bufs[parity]
        i = g // n_parts
        col = pl.multiple_of((g % n_parts) * SC_GATHER, SC_GATHER)
        return pltpu.make_async_copy(tab_hbm.at[idx_v.at[i, pl.ds(col, SC_GATHER)]], rows, sem)

    gather(0, 0).start()

    @pl.loop(0, n_gathers // 2)
    def _(pair):
        g = 2 * pair
        gather(g + 1, 1).start()
        gather(g, 0).wait()
        consume(g // n_parts, g % n_parts, bufs[0][0])

        @pl.when(g + 2 < n_gathers)
        def _():
            gather(g + 2, 0).start()

        gather(g + 1, 1).wait()
        consume((g + 1) // n_parts, (g + 1) % n_parts, bufs[1][0])


def _pack_halves(tab):
    half = tab.shape[1] // 2
    bits = lax.bitcast_convert_type(tab.astype(_bf16), jnp.uint16).astype(jnp.uint32)
    return lax.bitcast_convert_type(bits[:, :half] | (bits[:, half:] << 16), jnp.int32)


def _sc_unpack(words):
    lo = lax.bitcast_convert_type(words << 16, _f32)
    hi = lax.bitcast_convert_type(words & HI_MASK, _f32)
    return lo, hi


def _sc_udot(idx, tab, h):
    t, d = h.shape
    per_worker = t // SC_WORKERS
    half = d // 2

    @functools.partial(
        pl.kernel, mesh=_sc_mesh(),
        out_type=jax.ShapeDtypeStruct((t, PEER_SLOTS), _f32),
        scratch_types=[
            pltpu.VMEM((SC_TOKENS, PEER_SLOTS), jnp.int32),
            pltpu.VMEM((SC_TOKENS, d), _f32),
            pltpu.VMEM((SC_TOKENS, PEER_SLOTS), _f32),
            pltpu.VMEM((SC_GATHER, half), jnp.int32),
            pltpu.VMEM((SC_GATHER, half), jnp.int32),
            pltpu.SemaphoreType.DMA,
            pltpu.SemaphoreType.DMA,
        ],
        compiler_params=pltpu.CompilerParams(needs_layout_passes=False),
        name="sc_udot",
    )
    def body(idx_hbm, tab_hbm, h_hbm, out_hbm, idx_v, h_v, acts_v, rows0, rows1, sem0, sem1):
        base = _sc_worker_base(per_worker)
        lane = lax.iota(jnp.int32, SC_LANES)

        def dots(i, part, rows):
            def block_body(blk, carry):
                row0 = blk * SC_LANES
                outv = jnp.zeros((SC_LANES,), _f32)
                for sub in range(SC_LANES // SC_BLOCK):
                    def chunk_body(jj, accs):
                        accs = list(accs)
                        for u in range(SC_UNROLL):
                            off = pl.multiple_of((jj * SC_UNROLL + u) * SC_LANES, SC_LANES)
                            h_lo = h_v[i, pl.ds(off, SC_LANES)]
                            h_hi = h_v[i, pl.ds(half + off, SC_LANES)]
                            for e in range(SC_BLOCK):
                                lo, hi = _sc_unpack(rows[row0 + sub * SC_BLOCK + e, pl.ds(off, SC_LANES)])
                                accs[e] = accs[e] + (lo * h_lo + hi * h_hi)
                        return tuple(accs)

                    accs = lax.fori_loop(0, half // SC_LANES // SC_UNROLL, chunk_body,
                                         tuple(jnp.zeros((SC_LANES,), _f32) for _ in range(SC_BLOCK)))
                    for e in range(SC_BLOCK):
                        outv = jnp.where(lane == sub * SC_BLOCK + e, jnp.sum(accs[e]), outv)
                col = pl.multiple_of(part * SC_GATHER + row0, SC_LANES)
                acts_v[i, pl.ds(col, SC_LANES)] = outv
                return carry

            lax.fori_loop(0, SC_GATHER // SC_LANES, block_body, 0)

        @pl.loop(0, per_worker // SC_TOKENS)
        def _(step):
            tok = pl.multiple_of(base + step * SC_TOKENS, SC_TOKENS)
            pltpu.sync_copy(idx_hbm.at[pl.ds(tok, SC_TOKENS)], idx_v)
            pltpu.sync_copy(h_hbm.at[pl.ds(tok, SC_TOKENS)], h_v)
            _sc_gather_loop(tab_hbm, idx_v, ((rows0, sem0), (rows1, sem1)), dots)
            pltpu.sync_copy(acts_v, out_hbm.at[pl.ds(tok, SC_TOKENS)])

    return body(idx, tab, h)


def _sc_vaxpy(idx, w, tab, *, first, count):
    half = tab.shape[1]
    d = 2 * half
    per_worker = count // SC_WORKERS
    span = SC_COLS * SC_LANES

    @functools.partial(
        pl.kernel, mesh=_sc_mesh(),
        out_type=jax.ShapeDtypeStruct((count, d), _f32),
        scratch_types=[
            pltpu.VMEM((SC_TOKENS, PEER_SLOTS), jnp.int32),
            pltpu.VMEM((SC_TOKENS, PEER_SLOTS), _f32),
            pltpu.VMEM((SC_TOKENS, d), _f32),
            pltpu.VMEM((SC_GATHER, half), jnp.int32),
            pltpu.VMEM((SC_GATHER, half), jnp.int32),
            pltpu.SemaphoreType.DMA,
            pltpu.SemaphoreType.DMA,
        ],
        compiler_params=pltpu.CompilerParams(needs_layout_passes=False),
        name="sc_vaxpy",
    )
    def body(idx_hbm, w_hbm, tab_hbm, out_hbm, idx_v, w_v, out_v, rows0, rows1, sem0, sem1):
        base = _sc_worker_base(per_worker)

        def accumulate(i, part, rows):
            i_vec = jnp.full((SC_LANES,), i, jnp.int32)

            def span_body(cq, carry):
                def cols(c, offset=0):
                    return pl.ds(pl.multiple_of(offset + cq * span + c * SC_LANES, SC_LANES), SC_LANES)

                def expert_body(e, accs):
                    k_vec = jnp.full((SC_LANES,), part * SC_GATHER + e, jnp.int32)
                    wv = plsc.load_gather(w_v, [i_vec, k_vec])
                    new = []
                    for c in range(SC_COLS):
                        lo, hi = _sc_unpack(rows[e, cols(c)])
                        new += [accs[2 * c] + lo * wv, accs[2 * c + 1] + hi * wv]
                    return tuple(new)

                init = []
                for c in range(SC_COLS):
                    init += [out_v[i, cols(c)], out_v[i, cols(c, half)]]
                accs = lax.fori_loop(0, SC_GATHER, expert_body, tuple(init))
                for c in range(SC_COLS):
                    out_v[i, cols(c)] = accs[2 * c]
                    out_v[i, cols(c, half)] = accs[2 * c + 1]
                return carry

            lax.fori_loop(0, half // span, span_body, 0)

        @pl.loop(0, per_worker // SC_TOKENS)
        def _(step):
            off = pl.multiple_of(base + step * SC_TOKENS, SC_TOKENS)
            pltpu.sync_copy(idx_hbm.at[pl.ds(first + off, SC_TOKENS)], idx_v)
            pltpu.sync_copy(w_hbm.at[pl.ds(first + off, SC_TOKENS)], w_v)

            @pl.loop(0, SC_TOKENS)
            def _(i):
                @pl.loop(0, d // SC_LANES)
                def _(j):
                    out_v[i, pl.ds(pl.multiple_of(j * SC_LANES, SC_LANES), SC_LANES)] = (
                        jnp.zeros((SC_LANES,), _f32))

            _sc_gather_loop(tab_hbm, idx_v, ((rows0, sem0), (rows1, sem1)), accumulate)
            pltpu.sync_copy(out_v, out_hbm.at[pl.ds(off, SC_TOKENS)])

    return body(idx, w, tab)


ROW_CHUNKS = 8
PACK_ROWS = ROW_CHUNKS // 2


def _pack_table(tab):
    n, d = tab.shape
    bits = lax.bitcast_convert_type(tab.astype(_bf16), jnp.uint16).astype(jnp.uint32)
    bits = bits.reshape(n, PACK_ROWS, 2, d // ROW_CHUNKS)
    word = bits[:, :, 0, :] | (bits[:, :, 1, :] << 16)
    return lax.bitcast_convert_type(word, jnp.int32)


def _gate_weights_kernel(acts_ref, gate_ref, w_ref):
    a = acts_ref[...]
    gelu = 0.5 * a * (1.0 + lax.erf(a * (2.0 ** -0.5)))
    w_ref[...] = gate_ref[...] * gelu


def _gate_weights(acts, gate, *, tw):
    t, n = acts.shape
    spec = pl.BlockSpec((tw, n), lambda i: (i, 0))
    return pl.pallas_call(
        _gate_weights_kernel,
        grid=(t // tw,),
        in_specs=[spec, spec],
        out_specs=spec,
        out_shape=jax.ShapeDtypeStruct((t, n), _f32),
        compiler_params=pltpu.CompilerParams(dimension_semantics=("parallel",)),
        name="gate_weights",
    )(acts, gate)


def _gather_rows(idx_ref, t, tab_ref, rows_ref):
    for k in range(PEER_SLOTS):
        rows_ref[k * PACK_ROWS:(k + 1) * PACK_ROWS, :] = tab_ref[idx_ref[t, k]]


def _rows_matrix(rows_ref):
    return pltpu.bitcast(rows_ref[...], _bf16)


def _token_loop(tb, idx_ref, tab_ref, rows_a, rows_b, compute):
    _gather_rows(idx_ref, 0, tab_ref, rows_a)

    def pair_body(i, carry):
        t0 = 2 * i
        _gather_rows(idx_ref, t0 + 1, tab_ref, rows_b)
        compute(t0, _rows_matrix(rows_a))
        _gather_rows(idx_ref, jnp.minimum(t0 + 2, tb - 1), tab_ref, rows_a)
        compute(t0 + 1, _rows_matrix(rows_b))
        return carry

    lax.fori_loop(0, tb // 2, pair_body, 0)


def _chunk_diag_mask():
    shape = (ROW_CHUNKS, PEER_SLOTS * ROW_CHUNKS)
    return (_iota(shape, 1) % ROW_CHUNKS) == _iota(shape, 0)


def _vaxpy_kernel(idx_ref, w_ref, tab_ref, out_ref, rows_a, rows_b, wrep_ref):
    tb = out_ref.shape[0]
    diag = _chunk_diag_mask()
    shape = (PEER_SLOTS, PEER_SLOTS * ROW_CHUNKS)
    spread = (_iota(shape, 0) == (_iota(shape, 1) // ROW_CHUNKS)).astype(_bf16)
    wrep_ref[...] = _dot_exact_rhs01(w_ref[...], spread)

    def compute(t, m):
        w_row = jnp.broadcast_to(wrep_ref[pl.ds(t, 1), :], diag.shape)
        w_hi, w_lo = _split2(jnp.where(diag, w_row, 0.0))
        out_ref[t] = _dot(w_hi, m) + _dot(w_lo, m)

    _token_loop(tb, idx_ref, tab_ref, rows_a, rows_b, compute)


def _vaxpy(idx, w, tab, *, tb, count):
    return pl.pallas_call(
        _vaxpy_kernel,
        grid=(count // tb,),
        in_specs=[
            pl.BlockSpec((tb, PEER_SLOTS), lambda i: (i, 0), memory_space=pltpu.SMEM),
            pl.BlockSpec((tb, PEER_SLOTS), lambda i: (i, 0)),
            _const_spec(tab.shape),
        ],
        out_specs=pl.BlockSpec((tb, ROW_CHUNKS, 128), lambda i: (i, 0, 0)),
        out_shape=jax.ShapeDtypeStruct((count, ROW_CHUNKS, 128), _f32),
        scratch_shapes=[
            pltpu.VMEM((PEER_SLOTS * PACK_ROWS, 128), jnp.int32),
            pltpu.VMEM((PEER_SLOTS * PACK_ROWS, 128), jnp.int32),
            pltpu.VMEM((tb, PEER_SLOTS * ROW_CHUNKS), _f32),
        ],
        compiler_params=pltpu.CompilerParams(
            dimension_semantics=("parallel",),
            vmem_limit_bytes=VMEM_LIMIT_BYTES),
        name="vaxpy",
    )(idx, w, tab)


def _final_kernel(x_ref, p_ref, g_ref, out_ref, *, normalize):
    x = x_ref[...] + p_ref[...]
    if normalize:
        x = x * lax.rsqrt(jnp.mean(x * x, axis=-1, keepdims=True) + EPS) * g_ref[...]
    out_ref[...] = x


def _final(x, peer, g, *, normalize, tf, first):
    t, d = peer.shape
    off = first // tf
    return pl.pallas_call(
        functools.partial(_final_kernel, normalize=normalize),
        grid=(t // tf,),
        in_specs=[
            pl.BlockSpec((tf, d), lambda i: (i + off, 0)),
            pl.BlockSpec((tf, d), lambda i: (i, 0)),
            _const_spec((1, d)),
        ],
        out_specs=pl.BlockSpec((tf, d), lambda i: (i, 0)),
        out_shape=jax.ShapeDtypeStruct((t, d), _f32),
        compiler_params=pltpu.CompilerParams(dimension_semantics=("parallel",)),
        name="final_norm",
    )(x, peer, g)


def kernel(x, norm_mix_g, w_in, hg_lb_logits, hg_out_norm_g, conv_w, w_branch_hg, w_branch_conv, w_out, norm_ffn_g, peer_w_query, peer_keys1, peer_keys2, peer_u, peer_v, norm_final_g):
    b_, s_, d = x.shape
    depth = w_in.shape[0]
    lb_all = jnp.cumsum(jax.nn.softmax(hg_lb_logits.astype(_f32), axis=0), axis=0)
    n_groups = BATCH_GROUPS if b_ % BATCH_GROUPS == 0 else 1
    bg = b_ // n_groups
    tg = bg * s_
    n_sc = SC_SHARE if SC_SHARE < tg else 0
    n_tc = tg - n_sc
    for l in range(depth):
        wq = peer_w_query[l].astype(_bf16).reshape(d, PEER_HEADS, 2, PEER_HALF).transpose(1, 2, 0, 3)
        v_tab = _pack_table(peer_v[l])
        u_sc = _pack_halves(peer_u[l])
        v_sc = _pack_halves(peer_v[l])
        last = l == depth - 1
        g = norm_final_g[None] if last else jnp.ones((1, d), _f32)
        def front(c, x_in):
            xc = _mix(x_in, norm_mix_g[l][None], w_in[l].astype(_bf16), lb_all[l][None],
                      hg_out_norm_g[l][None], conv_w[l], w_branch_hg[l].astype(_bf16),
                      w_branch_conv[l].astype(_bf16), w_out[l].astype(_bf16), ts=MIX_TILE)
            xf = xc.reshape(tg, d)
            h, idx, gate = _route(xf, norm_ffn_g[l][None], wq,
                                  peer_keys1[l].astype(_bf16), peer_keys2[l].astype(_bf16), tr=ROUTE_TILE)
            return xf, idx, gate, _sc_udot(idx, u_sc, h)

        def back(xf, idx, gate, acts):
            w = _gate_weights(acts, gate, tw=FINAL_TILE)
            peer_tc = _vaxpy(idx, w, v_tab, tb=EXPERT_TILE, count=n_tc).reshape(n_tc, d)
            parts = [_final(xf, peer_tc, g, normalize=last, tf=FINAL_TILE, first=0)]
            if n_sc:
                peer_sc = _sc_vaxpy(idx, w, v_sc, first=n_tc, count=n_sc)
                parts.append(_final(xf, peer_sc, g, normalize=last, tf=FINAL_TILE, first=n_tc))
            return peer_tc, parts

        groups = [x[c * bg:(c + 1) * bg] for c in range(n_groups)]
        fronts = {c: front(c, groups[c]) for c in range(min(GROUPS_AHEAD + 1, n_groups))}
        outs, peer_tc = [], None
        for c in range(n_groups):
            xf, idx, gate, acts = fronts.pop(c)
            after = [fronts[c + GROUPS_AHEAD][1]] if c + GROUPS_AHEAD in fronts else []
            after += [peer_tc] if peer_tc is not None else []
            if after:
                acts = lax.optimization_barrier((acts, *after))[0]
            peer_tc, parts = back(xf, idx, gate, acts)
            outs.extend(parts)
            nxt = c + GROUPS_AHEAD + 1
            if nxt < n_groups:
                fronts[nxt] = front(nxt, lax.optimization_barrier((groups[nxt], peer_tc))[0])
        x = jnp.concatenate(outs, axis=0).reshape(b_, s_, d)
    return x
```

```python
import functools

import jax
import jax.numpy as jnp
from jax import lax
from jax.experimental import pallas as pl
from jax.experimental.pallas import tpu as pltpu
from jax.experimental.pallas import tpu_sc as plsc

EPS = 1e-6
CHUNK = 64
SUB = 16
HEADS = 8
HEAD_DIM = 64
HG_WIDTH = HEADS * HEAD_DIM
GROUP = 256
N_GROUPS = HG_WIDTH // GROUP
CONV_K = 3
PEER_HEADS = 8
PEER_NKEYS = 128
PEER_HALF = 128
PEER_TOPK = 16
PEER_SLOTS = PEER_HEADS * PEER_TOPK

VMEM_LIMIT_BYTES = 56 * 1024 * 1024

MIX_TILE = 256
ROUTE_TILE = 256
EXPERT_TILE = 128
FINAL_TILE = 512
BATCH_GROUPS = 4
SC_SHARE = 3072
GROUPS_AHEAD = 2

_f32 = jnp.float32
_bf16 = jnp.bfloat16


def _dot(a, b):
    return jnp.dot(a, b, preferred_element_type=_f32)


def _dot_nt(a, b):
    return lax.dot_general(a, b, (((1,), (1,)), ((), ())), preferred_element_type=_f32)


def _dot_tn(a, b):
    return lax.dot_general(a, b, (((0,), (0,)), ((), ())), preferred_element_type=_f32)


def _split3(x):
    hi = x.astype(_bf16)
    r1 = x - hi.astype(_f32)
    mid = r1.astype(_bf16)
    lo = (r1 - mid.astype(_f32)).astype(_bf16)
    return hi, mid, lo


def _split2(x):
    hi = x.astype(_bf16)
    lo = (x - hi.astype(_f32)).astype(_bf16)
    return hi, lo


def _dot_exact_rhs01(x, m01):
    hi, mid, lo = _split3(x)
    return _dot(hi, m01) + _dot(mid, m01) + _dot(lo, m01)


def _dot_exact_lhs01(m01, x):
    hi, mid, lo = _split3(x)
    return _dot(m01, hi) + _dot(m01, mid) + _dot(m01, lo)


def _iota(shape, dim):
    return lax.broadcasted_iota(jnp.int32, shape, dim)


def _hgrn2_chunk(q, k, lf, v, state_ref):
    n_sub = CHUNK // SUB
    row = _iota((CHUNK, CHUNK), 0)
    col = _iota((CHUNK, CHUNK), 1)
    tril = (col <= row).astype(_bf16)
    b = _dot_exact_lhs01(tril, lf)

    b_end = [b[(j + 1) * SUB - 1:(j + 1) * SUB, :] for j in range(n_sub)]
    b_end_rows = jnp.concatenate([jnp.broadcast_to(e, (SUB, HG_WIDTH)) for e in b_end], axis=0)
    b_last = b_end[-1]

    q_in = (q * jnp.exp(b)).astype(_bf16)
    k_sub = (k * jnp.exp(b_end_rows - b)).astype(_bf16)
    k_out = (k * jnp.exp(b_last - b)).astype(_bf16)
    q_from = [(q * jnp.exp(jnp.minimum(b - b_end[j], 0.0))).astype(_bf16) for j in range(n_sub - 1)]
    v_b = v.astype(_bf16)

    gr = _iota((GROUP, GROUP), 0) // HEAD_DIM
    gc = _iota((GROUP, GROUP), 1) // HEAD_DIM
    head_mask = gr == gc
    t_blk = _iota((CHUNK, GROUP), 0) // SUB
    s_blk = (_iota((CHUNK, GROUP), 1) % HEAD_DIM) // SUB

    outs = []
    for g in range(N_GROUPS):
        sl = slice(g * GROUP, (g + 1) * GROUP)
        st = state_ref[g]
        o_g = _dot_nt(q_in[:, sl], st.astype(_bf16))

        zero_b = jnp.zeros((), _bf16)
        k_bd = jnp.where(head_mask, jnp.concatenate([k_sub[:, sl]] * (GROUP // CHUNK), axis=0), zero_b)
        v_bd = jnp.where(head_mask, jnp.concatenate([v_b[:, sl]] * (GROUP // CHUNK), axis=0), zero_b)
        q_stack = jnp.concatenate([qf[:, sl] for qf in q_from], axis=0)
        r = _dot_nt(q_stack, k_bd)
        scores = jnp.zeros((CHUNK, GROUP), _f32)
        for j in range(n_sub - 1):
            sel = (s_blk == j) & (t_blk > j)
            scores = jnp.where(sel, r[j * CHUNK:(j + 1) * CHUNK, :], scores)
        o_g = o_g + _dot(scores.astype(_bf16), v_bd)
        outs.append(o_g)

        upd = _dot_tn(v_b[:, sl], k_out[:, sl])
        decay = jnp.exp(b_last[:, sl])
        state_ref[g] = st * decay + jnp.where(head_mask, upd, 0.0)
    o = jnp.concatenate(outs, axis=1)

    ones_bd = ((_iota((HG_WIDTH, HG_WIDTH), 0) // HEAD_DIM)
               == (_iota((HG_WIDTH, HG_WIDTH), 1) // HEAD_DIM)).astype(_bf16)
    t_in_sub = _iota((CHUNK, HG_WIDTH), 0) % SUB
    for lag in range(SUB):
        if lag == 0:
            p = q * k
            v_l = v
        else:
            valid = t_in_sub >= lag
            k_l = pltpu.roll(k, lag, 0)
            b_l = pltpu.roll(b, lag, 0)
            v_l = pltpu.roll(v, lag, 0)
            p = jnp.where(valid, q * k_l * jnp.exp(jnp.minimum(b - b_l, 0.0)), 0.0)
        s_l = _dot(p.astype(_bf16), ones_bd)
        o = o + s_l * v_l
    return o


def _mix_kernel(x_ref, g_ref, win_ref, lb_ref, hgn_ref, convw_ref, pa_ref, pb_ref, wo_ref,
                out_ref, state_ref, carry_ref, q_s, k_s, lf_s, v_s, o_s):
    ts = x_ref.shape[0]
    d_model = x_ref.shape[1]
    w = HG_WIDTH

    @pl.when(pl.program_id(1) == 0)
    def _():
        state_ref[...] = jnp.zeros_like(state_ref)
        carry_ref[...] = jnp.zeros_like(carry_ref)

    x = x_ref[...]
    h = x * lax.rsqrt(jnp.mean(x * x, axis=-1, keepdims=True) + EPS) * g_ref[...]
    hb = h.astype(_bf16)

    def proj(i, width=w):
        return _dot(hb, win_ref[:, i * w:i * w + width])

    lb = lb_ref[...]
    q_s[...] = jax.nn.silu(proj(0)) * (HEAD_DIM ** -0.5)
    forget = lb + (1.0 - lb) * jax.nn.sigmoid(proj(1))
    k_s[...] = 1.0 - forget
    lf_s[...] = jnp.log(forget)
    v_s[...] = proj(2)

    def chunk_body(c, carry):
        rows = pl.ds(pl.multiple_of(c * CHUNK, CHUNK), CHUNK)
        o_s[rows, :] = _hgrn2_chunk(q_s[rows, :], k_s[rows, :], lf_s[rows, :], v_s[rows, :], state_ref)
        return carry

    lax.fori_loop(0, ts // CHUNK, chunk_body, 0)

    o = o_s[...]
    ones_bd = ((_iota((w, w), 0) // HEAD_DIM) == (_iota((w, w), 1) // HEAD_DIM)).astype(_bf16)
    ms = _dot_exact_rhs01(o * o, ones_bd) * (1.0 / HEAD_DIM)
    o = o * lax.rsqrt(ms + EPS) * hgn_ref[...]
    y_a = (o * jax.nn.silu(proj(3))).astype(_bf16)

    u = proj(5) * proj(6)
    prev = carry_ref[...]
    rowi = _iota((ts, w), 0)
    u1 = jnp.where(rowi >= 1, pltpu.roll(u, 1, 0), jnp.broadcast_to(prev[7:8, :], (ts, w)))
    u2 = jnp.where(rowi >= 2, pltpu.roll(u, 2, 0),
                   jnp.where(rowi == 1, jnp.broadcast_to(prev[7:8, :], (ts, w)),
                             jnp.broadcast_to(prev[6:7, :], (ts, w))))
    carry_ref[...] = u[ts - 8:, :]
    cw = convw_ref[...]
    y_b = (proj(4) * (cw[0:1, :] * u2 + cw[1:2, :] * u1 + cw[2:3, :] * u)).astype(_bf16)

    g_a = jax.nn.sigmoid(proj(7, d_model))
    g_b = jax.nn.sigmoid(_dot(hb, win_ref[:, 7 * w + d_model:7 * w + 2 * d_model]))
    merged = g_a * _dot(y_a, pa_ref[...]) + g_b * _dot(y_b, pb_ref[...])
    out_ref[...] = x + _dot(merged.astype(_bf16), wo_ref[...])


def _const_spec(shape):
    nd = len(shape)
    return pl.BlockSpec(shape, lambda *_: (0,) * nd, pipeline_mode=pl.Buffered(1))


def _mix(x, norm_g, w_in, lb, hg_norm_g, conv_w, w_a, w_b, w_o, *, ts):
    b_, s_, d = x.shape
    in_cols = w_in.shape[1]
    w = HG_WIDTH
    grid = (b_, s_ // ts)
    return pl.pallas_call(
        _mix_kernel,
        grid=grid,
        in_specs=[
            pl.BlockSpec((None, ts, d), lambda b, s: (b, s, 0)),
            _const_spec((1, d)),
            _const_spec((d, in_cols)),
            _const_spec((1, w)),
            _const_spec((1, w)),
            _const_spec((CONV_K, w)),
            _const_spec((w, d)),
            _const_spec((w, d)),
            _const_spec((d, d)),
        ],
        out_specs=pl.BlockSpec((None, ts, d), lambda b, s: (b, s, 0)),
        out_shape=jax.ShapeDtypeStruct((b_, s_, d), _f32),
        scratch_shapes=[
            pltpu.VMEM((N_GROUPS, GROUP, GROUP), _f32),
            pltpu.VMEM((8, w), _f32),
            pltpu.VMEM((ts, w), _f32),
            pltpu.VMEM((ts, w), _f32),
            pltpu.VMEM((ts, w), _f32),
            pltpu.VMEM((ts, w), _f32),
            pltpu.VMEM((ts, w), _f32),
        ],
        compiler_params=pltpu.CompilerParams(
            dimension_semantics=("parallel", "arbitrary"),
            vmem_limit_bytes=VMEM_LIMIT_BYTES),
        name="mix",
    )(x, norm_g, w_in, lb, hg_norm_g, conv_w, w_a, w_b, w_o)


def _stair_pairs():
    pairs = [(a, c) for a in range(PEER_TOPK) for c in range(PEER_TOPK) if (a + 1) * (c + 1) <= PEER_TOPK]
    rows = -(-len(pairs) // 8) * 8
    pairs = pairs + [(-1, -1)] * (rows - len(pairs))
    arr = jnp.asarray(pairs, _f32)
    return arr[:, 0:1], arr[:, 1:2]


def _route_kernel(x_ref, g_ref, wq_ref, k1_ref, k2_ref, ca_ref, cc_ref, h_ref, idx_ref, gate_ref,
                  idx_t, e_t):
    tr = x_ref.shape[0]
    n_cand = ca_ref.shape[0]
    x = x_ref[...]
    h = x * lax.rsqrt(jnp.mean(x * x, axis=-1, keepdims=True) + EPS) * g_ref[...]
    h_ref[...] = h
    hb = h.astype(_bf16)

    key_row = _iota((PEER_NKEYS, tr), 0).astype(_f32)
    cand_row = _iota((n_cand, tr), 0).astype(_f32)
    ca = ca_ref[...]
    cc = cc_ref[...]
    neg_inf = jnp.float32(-jnp.inf)

    def extract_max(s):
        m = jnp.max(s, axis=0, keepdims=True)
        i = jnp.min(jnp.where(s == m, key_row, float(PEER_NKEYS)), axis=0, keepdims=True)
        return m, i, jnp.where(key_row == i, neg_inf, s)

    def head_body(hd, carry):
        q1 = _dot(hb, wq_ref[hd, 0]).astype(_bf16)
        q2 = _dot(hb, wq_ref[hd, 1]).astype(_bf16)
        s1 = _dot_nt(k1_ref[hd], q1)
        s2 = _dot_nt(k2_ref[hd], q2)

        def half_body(k, c):
            s1, s2, a_s, a_i, c_s, c_i = c
            kf = jnp.asarray(k, _f32)
            m1, i1, s1 = extract_max(s1)
            m2, i2, s2 = extract_max(s2)
            sel_a = ca == kf
            sel_c = cc == kf
            return (s1, s2, jnp.where(sel_a, m1, a_s), jnp.where(sel_a, i1, a_i),
                    jnp.where(sel_c, m2, c_s), jnp.where(sel_c, i2, c_i))

        zeros = jnp.zeros((n_cand, tr), _f32)
        _, _, a_s, a_i, c_s, c_i = lax.fori_loop(
            0, PEER_TOPK, half_body, (s1, s2, jnp.full((n_cand, tr), neg_inf), zeros, zeros, zeros))
        cand_i = a_i * float(PEER_NKEYS) + c_i

        def pick_body(k, c):
            cand_s, denom, m_first = c
            m = jnp.max(cand_s, axis=0, keepdims=True)
            pos = jnp.min(jnp.where(cand_s == m, cand_row, float(n_cand)), axis=0, keepdims=True)
            hit = cand_row == pos
            eid = jnp.max(jnp.where(hit, cand_i, -1.0), axis=0, keepdims=True)
            m_first = jnp.where(k == 0, m, m_first)
            e = jnp.exp(m - m_first)
            slot = hd * PEER_TOPK + k
            idx_t[pl.ds(slot, 1), :] = eid
            e_t[pl.ds(slot, 1), :] = e
            return jnp.where(hit, neg_inf, cand_s), denom + e, m_first

        zero_row = jnp.zeros((1, tr), _f32)
        _, denom, _ = lax.fori_loop(0, PEER_TOPK, pick_body, (a_s + c_s, zero_row, zero_row))
        rows = pl.ds(pl.multiple_of(hd * PEER_TOPK, PEER_TOPK), PEER_TOPK)
        e_t[rows, :] = e_t[rows, :] / denom
        return carry

    lax.fori_loop(0, PEER_HEADS, head_body, 0)
    idx_ref[...] = idx_t[...].T.astype(jnp.int32)
    gate_ref[...] = e_t[...].T


def _route(x, norm_g, w_query, keys1, keys2, *, tr):
    t, d = x.shape
    ca, cc = _stair_pairs()
    ca = jnp.broadcast_to(ca, (ca.shape[0], tr))
    cc = jnp.broadcast_to(cc, (cc.shape[0], tr))
    return pl.pallas_call(
        _route_kernel,
        grid=(t // tr,),
        in_specs=[
            pl.BlockSpec((tr, d), lambda i: (i, 0)),
            _const_spec((1, d)),
            _const_spec(w_query.shape),
            _const_spec(keys1.shape),
            _const_spec(keys2.shape),
            _const_spec(ca.shape),
            _const_spec(cc.shape),
        ],
        out_specs=[
            pl.BlockSpec((tr, d), lambda i: (i, 0)),
            pl.BlockSpec((tr, PEER_SLOTS), lambda i: (i, 0)),
            pl.BlockSpec((tr, PEER_SLOTS), lambda i: (i, 0)),
        ],
        out_shape=[
            jax.ShapeDtypeStruct((t, d), _f32),
            jax.ShapeDtypeStruct((t, PEER_SLOTS), jnp.int32),
            jax.ShapeDtypeStruct((t, PEER_SLOTS), _f32),
        ],
        scratch_shapes=[
            pltpu.VMEM((PEER_SLOTS, tr), _f32),
            pltpu.VMEM((PEER_SLOTS, tr), _f32),
        ],
        compiler_params=pltpu.CompilerParams(
            dimension_semantics=("parallel",),
            vmem_limit_bytes=VMEM_LIMIT_BYTES),
        name="route",
    )(x, norm_g, w_query, keys1, keys2, ca, cc)


SC_CORES = 2
SC_SUBCORES = 16
SC_LANES = 16
SC_WORKERS = SC_CORES * SC_SUBCORES
SC_TOKENS = 8
SC_GATHER = 32
SC_BLOCK = 8
SC_UNROLL = 2
SC_COLS = 8
HI_MASK = -65536


def _sc_mesh():
    return plsc.VectorSubcoreMesh(core_axis_name="c", subcore_axis_name="s")


def _sc_worker_base(per_worker):
    return (lax.axis_index("s") * SC_CORES + lax.axis_index("c")) * per_worker


def _sc_gather_loop(tab_hbm, idx_v, bufs, consume):
    n_parts = PEER_SLOTS // SC_GATHER
    n_gathers = SC_TOKENS * n_parts

    def gather(g, parity):
        rows, sem = bufs[parity]
        i = g // n_parts
        col = pl.multiple_of((g % n_parts) * SC_GATHER, SC_GATHER)
        return pltpu.make_async_copy(tab_hbm.at[idx_v.at[i, pl.ds(col, SC_GATHER)]], rows, sem)

    gather(0, 0).start()

    @pl.loop(0, n_gathers // 2)
    def _(pair):
        g = 2 * pair
        gather(g + 1, 1).start()
        gather(g, 0).wait()
        consume(g // n_parts, g % n_parts, bufs[0][0])

        @pl.when(g + 2 < n_gathers)
        def _():
            gather(g + 2, 0).start()

        gather(g + 1, 1).wait()
        consume((g + 1) // n_parts, (g + 1) % n_parts, bufs[1][0])


def _pack_halves(tab):
    half = tab.shape[1] // 2
    bits = lax.bitcast_convert_type(tab.astype(_bf16), jnp.uint16).astype(jnp.uint32)
    return lax.bitcast_convert_type(bits[:, :half] | (bits[:, half:] << 16), jnp.int32)


def _sc_unpack(words):
    lo = lax.bitcast_convert_type(words << 16, _f32)
    hi = lax.bitcast_convert_type(words & HI_MASK, _f32)
    return lo, hi


def _sc_udot(idx, tab, h):
    t, d = h.shape
    per_worker = t // SC_WORKERS
    half = d // 2

    @functools.partial(
        pl.kernel, mesh=_sc_mesh(),
        out_type=jax.ShapeDtypeStruct((t, PEER_SLOTS), _f32),
        scratch_types=[
            pltpu.VMEM((SC_TOKENS, PEER_SLOTS), jnp.int32),
            pltpu.VMEM((SC_TOKENS, d), _f32),
            pltpu.VMEM((SC_TOKENS, PEER_SLOTS), _f32),
            pltpu.VMEM((SC_GATHER, half), jnp.int32),
            pltpu.VMEM((SC_GATHER, half), jnp.int32),
            pltpu.SemaphoreType.DMA,
            pltpu.SemaphoreType.DMA,
        ],
        compiler_params=pltpu.CompilerParams(needs_layout_passes=False),
        name="sc_udot",
    )
    def body(idx_hbm, tab_hbm, h_hbm, out_hbm, idx_v, h_v, acts_v, rows0, rows1, sem0, sem1):
        base = _sc_worker_base(per_worker)
        lane = lax.iota(jnp.int32, SC_LANES)

        def dots(i, part, rows):
            def block_body(blk, carry):
                row0 = blk * SC_LANES
                outv = jnp.zeros((SC_LANES,), _f32)
                for sub in range(SC_LANES // SC_BLOCK):
                    def chunk_body(jj, accs):
                        accs = list(accs)
                        for u in range(SC_UNROLL):
                            off = pl.multiple_of((jj * SC_UNROLL + u) * SC_LANES, SC_LANES)
                            h_lo = h_v[i, pl.ds(off, SC_LANES)]
                            h_hi = h_v[i, pl.ds(half + off, SC_LANES)]
                            for e in range(SC_BLOCK):
                                lo, hi = _sc_unpack(rows[row0 + sub * SC_BLOCK + e, pl.ds(off, SC_LANES)])
                                accs[e] = accs[e] + (lo * h_lo + hi * h_hi)
                        return tuple(accs)

                    accs = lax.fori_loop(0, half // SC_LANES // SC_UNROLL, chunk_body,
                                         tuple(jnp.zeros((SC_LANES,), _f32) for _ in range(SC_BLOCK)))
                    for e in range(SC_BLOCK):
                        outv = jnp.where(lane == sub * SC_BLOCK + e, jnp.sum(accs[e]), outv)
                col = pl.multiple_of(part * SC_GATHER + row0, SC_LANES)
                acts_v[i, pl.ds(col, SC_LANES)] = outv
                return carry

            lax.fori_loop(0, SC_GATHER // SC_LANES, block_body, 0)

        @pl.loop(0, per_worker // SC_TOKENS)
        def _(step):
            tok = pl.multiple_of(base + step * SC_TOKENS, SC_TOKENS)
            pltpu.sync_copy(idx_hbm.at[pl.ds(tok, SC_TOKENS)], idx_v)
            pltpu.sync_copy(h_hbm.at[pl.ds(tok, SC_TOKENS)], h_v)
            _sc_gather_loop(tab_hbm, idx_v, ((rows0, sem0), (rows1, sem1)), dots)
            pltpu.sync_copy(acts_v, out_hbm.at[pl.ds(tok, SC_TOKENS)])

    return body(idx, tab, h)


def _sc_vaxpy(idx, w, tab, *, first, count):
    half = tab.shape[1]
    d = 2 * half
    per_worker = count // SC_WORKERS
    span = SC_COLS * SC_LANES

    @functools.partial(
        pl.kernel, mesh=_sc_mesh(),
        out_type=jax.ShapeDtypeStruct((count, d), _f32),
        scratch_types=[
            pltpu.VMEM((SC_TOKENS, PEER_SLOTS), jnp.int32),
            pltpu.VMEM((SC_TOKENS, PEER_SLOTS), _f32),
            pltpu.VMEM((SC_TOKENS, d), _f32),
            pltpu.VMEM((SC_GATHER, half), jnp.int32),
            pltpu.VMEM((SC_GATHER, half), jnp.int32),
            pltpu.SemaphoreType.DMA,
            pltpu.SemaphoreType.DMA,
        ],
        compiler_params=pltpu.CompilerParams(needs_layout_passes=False),
        name="sc_vaxpy",
    )
    def body(idx_hbm, w_hbm, tab_hbm, out_hbm, idx_v, w_v, out_v, rows0, rows1, sem0, sem1):
        base = _sc_worker_base(per_worker)

        def accumulate(i, part, rows):
            i_vec = jnp.full((SC_LANES,), i, jnp.int32)

            def span_body(cq, carry):
                def cols(c, offset=0):
                    return pl.ds(pl.multiple_of(offset + cq * span + c * SC_LANES, SC_LANES), SC_LANES)

                def expert_body(e, accs):
                    k_vec = jnp.full((SC_LANES,), part * SC_GATHER + e, jnp.int32)
                    wv = plsc.load_gather(w_v, [i_vec, k_vec])
                    new = []
                    for c in range(SC_COLS):
                        lo, hi = _sc_unpack(rows[e, cols(c)])
                        new += [accs[2 * c] + lo * wv, accs[2 * c + 1] + hi * wv]
                    return tuple(new)

                init = []
                for c in range(SC_COLS):
                    init += [out_v[i, cols(c)], out_v[i, cols(c, half)]]
                accs = lax.fori_loop(0, SC_GATHER, expert_body, tuple(init))
                for c in range(SC_COLS):
                    out_v[i, cols(c)] = accs[2 * c]
                    out_v[i, cols(c, half)] = accs[2 * c + 1]
                return carry

            lax.fori_loop(0, half // span, span_body, 0)

        @pl.loop(0, per_worker // SC_TOKENS)
        def _(step):
            off = pl.multiple_of(base + step * SC_TOKENS, SC_TOKENS)
            pltpu.sync_copy(idx_hbm.at[pl.ds(first + off, SC_TOKENS)], idx_v)
            pltpu.sync_copy(w_hbm.at[pl.ds(first + off, SC_TOKENS)], w_v)

            @pl.loop(0, SC_TOKENS)
            def _(i):
                @pl.loop(0, d // SC_LANES)
                def _(j):
                    out_v[i, pl.ds(pl.multiple_of(j * SC_LANES, SC_LANES), SC_LANES)] = (
                        jnp.zeros((SC_LANES,), _f32))

            _sc_gather_loop(tab_hbm, idx_v, ((rows0, sem0), (rows1, sem1)), accumulate)
            pltpu.sync_copy(out_v, out_hbm.at[pl.ds(off, SC_TOKENS)])

    return body(idx, w, tab)


ROW_CHUNKS = 8
PACK_ROWS = ROW_CHUNKS // 2


def _pack_table(tab):
    n, d = tab.shape
    bits = lax.bitcast_convert_type(tab.astype(_bf16), jnp.uint16).astype(jnp.uint32)
    bits = bits.reshape(n, PACK_ROWS, 2, d // ROW_CHUNKS)
    word = bits[:, :, 0, :] | (bits[:, :, 1, :] << 16)
    return lax.bitcast_convert_type(word, jnp.int32)


def _gate_weights_kernel(acts_ref, gate_ref, w_ref):
    a = acts_ref[...]
    gelu = 0.5 * a * (1.0 + lax.erf(a * (2.0 ** -0.5)))
    w_ref[...] = gate_ref[...] * gelu


def _gate_weights(acts, gate, *, tw):
    t, n = acts.shape
    spec = pl.BlockSpec((tw, n), lambda i: (i, 0))
    return pl.pallas_call(
        _gate_weights_kernel,
        grid=(t // tw,),
        in_specs=[spec, spec],
        out_specs=spec,
        out_shape=jax.ShapeDtypeStruct((t, n), _f32),
        compiler_params=pltpu.CompilerParams(dimension_semantics=("parallel",)),
        name="gate_weights",
    )(acts, gate)


def _gather_rows(idx_ref, t, tab_ref, rows_ref):
    for k in range(PEER_SLOTS):
        row = pl.multiple_of(idx_ref[t, k], PACK_ROWS)
        rows_ref[k * PACK_ROWS:(k + 1) * PACK_ROWS, :] = tab_ref[pl.ds(row, PACK_ROWS), :]


def _rows_matrix(rows_ref):
    return pltpu.bitcast(rows_ref[...], _bf16)


def _token_loop(tb, idx_ref, tab_ref, rows_a, rows_b, compute):
    _gather_rows(idx_ref, 0, tab_ref, rows_a)

    def pair_body(i, carry):
        t0 = 2 * i
        _gather_rows(idx_ref, t0 + 1, tab_ref, rows_b)
        compute(t0, _rows_matrix(rows_a))
        _gather_rows(idx_ref, jnp.minimum(t0 + 2, tb - 1), tab_ref, rows_a)
        compute(t0 + 1, _rows_matrix(rows_b))
        return carry

    lax.fori_loop(0, tb // 2, pair_body, 0)


def _chunk_diag_mask():
    shape = (ROW_CHUNKS, PEER_SLOTS * ROW_CHUNKS)
    return (_iota(shape, 1) % ROW_CHUNKS) == _iota(shape, 0)


def _vaxpy_kernel(idx_ref, w_ref, tab_ref, out_ref, rows_a, rows_b, wrep_ref):
    tb = out_ref.shape[0]
    diag = _chunk_diag_mask()
    shape = (PEER_SLOTS, PEER_SLOTS * ROW_CHUNKS)
    spread = (_iota(shape, 0) == (_iota(shape, 1) // ROW_CHUNKS)).astype(_bf16)
    wrep_ref[...] = _dot_exact_rhs01(w_ref[...], spread)

    def compute(t, m):
        w_row = jnp.broadcast_to(wrep_ref[pl.ds(t, 1), :], diag.shape)
        w_hi, w_lo = _split2(jnp.where(diag, w_row, 0.0))
        out_ref[t] = _dot(w_hi, m) + _dot(w_lo, m)

    _token_loop(tb, idx_ref, tab_ref, rows_a, rows_b, compute)


def _vaxpy(idx, w, tab, *, tb, count):
    return pl.pallas_call(
        _vaxpy_kernel,
        grid=(count // tb,),
        in_specs=[
            pl.BlockSpec((tb, PEER_SLOTS), lambda i: (i, 0), memory_space=pltpu.SMEM),
            pl.BlockSpec((tb, PEER_SLOTS), lambda i: (i, 0)),
            _const_spec(tab.shape),
        ],
        out_specs=pl.BlockSpec((tb, ROW_CHUNKS, 128), lambda i: (i, 0, 0)),
        out_shape=jax.ShapeDtypeStruct((count, ROW_CHUNKS, 128), _f32),
        scratch_shapes=[
            pltpu.VMEM((PEER_SLOTS * PACK_ROWS, 128), jnp.int32),
            pltpu.VMEM((PEER_SLOTS * PACK_ROWS, 128), jnp.int32),
            pltpu.VMEM((tb, PEER_SLOTS * ROW_CHUNKS), _f32),
        ],
        compiler_params=pltpu.CompilerParams(
            dimension_semantics=("parallel",),
            vmem_limit_bytes=VMEM_LIMIT_BYTES),
        name="vaxpy",
    )(idx, w, tab)


def _final_kernel(x_ref, p_ref, g_ref, out_ref, *, normalize):
    x = x_ref[...] + p_ref[...]
    if normalize:
        x = x * lax.rsqrt(jnp.mean(x * x, axis=-1, keepdims=True) + EPS) * g_ref[...]
    out_ref[...] = x


def _final(x, peer, g, *, normalize, tf, first):
    t, d = peer.shape
    off = first // tf
    return pl.pallas_call(
        functools.partial(_final_kernel, normalize=normalize),
        grid=(t // tf,),
        in_specs=[
            pl.BlockSpec((tf, d), lambda i: (i + off, 0)),
            pl.BlockSpec((tf, d), lambda i: (i, 0)),
            _const_spec((1, d)),
        ],
        out_specs=pl.BlockSpec((tf, d), lambda i: (i, 0)),
        out_shape=jax.ShapeDtypeStruct((t, d), _f32),
        compiler_params=pltpu.CompilerParams(dimension_semantics=("parallel",)),
        name="final_norm",
    )(x, peer, g)


def kernel(x, norm_mix_g, w_in, hg_lb_logits, hg_out_norm_g, conv_w, w_branch_hg, w_branch_conv, w_out, norm_ffn_g, peer_w_query, peer_keys1, peer_keys2, peer_u, peer_v, norm_final_g):
    b_, s_, d = x.shape
    depth = w_in.shape[0]
    lb_all = jnp.cumsum(jax.nn.softmax(hg_lb_logits.astype(_f32), axis=0), axis=0)
    n_groups = BATCH_GROUPS if b_ % BATCH_GROUPS == 0 else 1
    bg = b_ // n_groups
    tg = bg * s_
    n_sc = SC_SHARE if SC_SHARE < tg else 0
    n_tc = tg - n_sc
    for l in range(depth):
        wq = peer_w_query[l].astype(_bf16).reshape(d, PEER_HEADS, 2, PEER_HALF).transpose(1, 2, 0, 3)
        v_tab = _pack_table(peer_v[l])
        u_sc = _pack_halves(peer_u[l])
        v_sc = _pack_halves(peer_v[l])
        last = l == depth - 1
        g = norm_final_g[None] if last else jnp.ones((1, d), _f32)
        def front(c, x_in):
            xc = _mix(x_in, norm_mix_g[l][None], w_in[l].astype(_bf16), lb_all[l][None],
                      hg_out_norm_g[l][None], conv_w[l], w_branch_hg[l].astype(_bf16),
                      w_branch_conv[l].astype(_bf16), w_out[l].astype(_bf16), ts=MIX_TILE)
            xf = xc.reshape(tg, d)
            h, idx, gate = _route(xf, norm_ffn_g[l][None], wq,
                                  peer_keys1[l].astype(_bf16), peer_keys2[l].astype(_bf16), tr=ROUTE_TILE)
            return xf, idx, gate, _sc_udot(idx, u_sc, h)

        def back(xf, idx, gate, acts):
            w = _gate_weights(acts, gate, tw=FINAL_TILE)
            peer_tc = _vaxpy(idx * PACK_ROWS, w, v_tab.reshape(-1, 128), tb=EXPERT_TILE, count=n_tc).reshape(n_tc, d)
            parts = [_final(xf, peer_tc, g, normalize=last, tf=FINAL_TILE, first=0)]
            if n_sc:
                peer_sc = _sc_vaxpy(idx, w, v_sc, first=n_tc, count=n_sc)
                parts.append(_final(xf, peer_sc, g, normalize=last, tf=FINAL_TILE, first=n_tc))
            return peer_tc, parts

        groups = [x[c * bg:(c + 1) * bg] for c in range(n_groups)]
        fronts = {c: front(c, groups[c]) for c in range(min(GROUPS_AHEAD + 1, n_groups))}
        outs, peer_tc = [], None
        for c in range(n_groups):
            xf, idx, gate, acts = fronts.pop(c)
            after = [fronts[c + GROUPS_AHEAD][1]] if c + GROUPS_AHEAD in fronts else []
            after += [peer_tc] if peer_tc is not None else []
            if after:
                acts = lax.optimization_barrier((acts, *after))[0]
            peer_tc, parts = back(xf, idx, gate, acts)
            outs.extend(parts)
            nxt = c + GROUPS_AHEAD + 1
            if nxt < n_groups:
                fronts[nxt] = front(nxt, lax.optimization_barrier((groups[nxt], peer_tc))[0])
        x = jnp.concatenate(outs, axis=0).reshape(b_, s_, d)
    return x
```

```python
import functools

import jax
import jax.numpy as jnp
from jax import lax
from jax.experimental import pallas as pl
from jax.experimental.pallas import tpu as pltpu
from jax.experimental.pallas import tpu_sc as plsc

EPS = 1e-6
CHUNK = 64
SUB = 16
HEADS = 8
HEAD_DIM = 64
HG_WIDTH = HEADS * HEAD_DIM
GROUP = 256
N_GROUPS = HG_WIDTH // GROUP
CONV_K = 3
PEER_HEADS = 8
PEER_NKEYS = 128
PEER_HALF = 128
PEER_TOPK = 16
PEER_SLOTS = PEER_HEADS * PEER_TOPK

VMEM_LIMIT_BYTES = 56 * 1024 * 1024

MIX_TILE = 256
ROUTE_TILE = 256
EXPERT_TILE = 128
FINAL_TILE = 512
BATCH_GROUPS = 4
SC_SHARE = 3072
GROUPS_AHEAD = 2

_f32 = jnp.float32
_bf16 = jnp.bfloat16


def _dot(a, b):
    return jnp.dot(a, b, preferred_element_type=_f32)


def _dot_nt(a, b):
    return lax.dot_general(a, b, (((1,), (1,)), ((), ())), preferred_element_type=_f32)


def _dot_tn(a, b):
    return lax.dot_general(a, b, (((0,), (0,)), ((), ())), preferred_element_type=_f32)


def _split3(x):
    hi = x.astype(_bf16)
    r1 = x - hi.astype(_f32)
    mid = r1.astype(_bf16)
    lo = (r1 - mid.astype(_f32)).astype(_bf16)
    return hi, mid, lo


def _split2(x):
    hi = x.astype(_bf16)
    lo = (x - hi.astype(_f32)).astype(_bf16)
    return hi, lo


def _dot_exact_rhs01(x, m01):
    hi, mid, lo = _split3(x)
    return _dot(hi, m01) + _dot(mid, m01) + _dot(lo, m01)


def _dot_exact_lhs01(m01, x):
    hi, mid, lo = _split3(x)
    return _dot(m01, hi) + _dot(m01, mid) + _dot(m01, lo)


def _iota(shape, dim):
    return lax.broadcasted_iota(jnp.int32, shape, dim)


def _hgrn2_chunk(q, k, lf, v, state_ref):
    n_sub = CHUNK // SUB
    row = _iota((CHUNK, CHUNK), 0)
    col = _iota((CHUNK, CHUNK), 1)
    tril = (col <= row).astype(_bf16)
    b = _dot_exact_lhs01(tril, lf)

    b_end = [b[(j + 1) * SUB - 1:(j + 1) * SUB, :] for j in range(n_sub)]
    b_end_rows = jnp.concatenate([jnp.broadcast_to(e, (SUB, HG_WIDTH)) for e in b_end], axis=0)
    b_last = b_end[-1]

    q_in = (q * jnp.exp(b)).astype(_bf16)
    k_sub = (k * jnp.exp(b_end_rows - b)).astype(_bf16)
    k_out = (k * jnp.exp(b_last - b)).astype(_bf16)
    q_from = [(q * jnp.exp(jnp.minimum(b - b_end[j], 0.0))).astype(_bf16) for j in range(n_sub - 1)]
    v_b = v.astype(_bf16)

    gr = _iota((GROUP, GROUP), 0) // HEAD_DIM
    gc = _iota((GROUP, GROUP), 1) // HEAD_DIM
    head_mask = gr == gc
    t_blk = _iota((CHUNK, GROUP), 0) // SUB
    s_blk = (_iota((CHUNK, GROUP), 1) % HEAD_DIM) // SUB

    outs = []
    for g in range(N_GROUPS):
        sl = slice(g * GROUP, (g + 1) * GROUP)
        st = state_ref[g]
        o_g = _dot_nt(q_in[:, sl], st.astype(_bf16))

        zero_b = jnp.zeros((), _bf16)
        k_bd = jnp.where(head_mask, jnp.concatenate([k_sub[:, sl]] * (GROUP // CHUNK), axis=0), zero_b)
        v_bd = jnp.where(head_mask, jnp.concatenate([v_b[:, sl]] * (GROUP // CHUNK), axis=0), zero_b)
        q_stack = jnp.concatenate([qf[:, sl] for qf in q_from], axis=0)
        r = _dot_nt(q_stack, k_bd)
        scores = jnp.zeros((CHUNK, GROUP), _f32)
        for j in range(n_sub - 1):
            sel = (s_blk == j) & (t_blk > j)
            scores = jnp.where(sel, r[j * CHUNK:(j + 1) * CHUNK, :], scores)
        o_g = o_g + _dot(scores.astype(_bf16), v_bd)
        outs.append(o_g)

        upd = _dot_tn(v_b[:, sl], k_out[:, sl])
        decay = jnp.exp(b_last[:, sl])
        state_ref[g] = st * decay + jnp.where(head_mask, upd, 0.0)
    o = jnp.concatenate(outs, axis=1)

    ones_bd = ((_iota((HG_WIDTH, HG_WIDTH), 0) // HEAD_DIM)
               == (_iota((HG_WIDTH, HG_WIDTH), 1) // HEAD_DIM)).astype(_bf16)
    t_in_sub = _iota((CHUNK, HG_WIDTH), 0) % SUB
    for lag in range(SUB):
        if lag == 0:
            p = q * k
            v_l = v
        else:
            valid = t_in_sub >= lag
            k_l = pltpu.roll(k, lag, 0)
            b_l = pltpu.roll(b, lag, 0)
            v_l = pltpu.roll(v, lag, 0)
            p = jnp.where(valid, q * k_l * jnp.exp(jnp.minimum(b - b_l, 0.0)), 0.0)
        s_l = _dot(p.astype(_bf16), ones_bd)
        o = o + s_l * v_l
    return o


def _mix_kernel(x_ref, g_ref, win_ref, lb_ref, hgn_ref, convw_ref, pa_ref, pb_ref, wo_ref,
                out_ref, state_ref, carry_ref, q_s, k_s, lf_s, v_s, o_s):
    ts = x_ref.shape[0]
    d_model = x_ref.shape[1]
    w = HG_WIDTH

    @pl.when(pl.program_id(1) == 0)
    def _():
        state_ref[...] = jnp.zeros_like(state_ref)
        carry_ref[...] = jnp.zeros_like(carry_ref)

    x = x_ref[...]
    h = x * lax.rsqrt(jnp.mean(x * x, axis=-1, keepdims=True) + EPS) * g_ref[...]
    hb = h.astype(_bf16)

    def proj(i, width=w):
        return _dot(hb, win_ref[:, i * w:i * w + width])

    lb = lb_ref[...]
    q_s[...] = jax.nn.silu(proj(0)) * (HEAD_DIM ** -0.5)
    forget = lb + (1.0 - lb) * jax.nn.sigmoid(proj(1))
    k_s[...] = 1.0 - forget
    lf_s[...] = jnp.log(forget)
    v_s[...] = proj(2)

    def chunk_body(c, carry):
        rows = pl.ds(pl.multiple_of(c * CHUNK, CHUNK), CHUNK)
        o_s[rows, :] = _hgrn2_chunk(q_s[rows, :], k_s[rows, :], lf_s[rows, :], v_s[rows, :], state_ref)
        return carry

    lax.fori_loop(0, ts // CHUNK, chunk_body, 0)

    o = o_s[...]
    ones_bd = ((_iota((w, w), 0) // HEAD_DIM) == (_iota((w, w), 1) // HEAD_DIM)).astype(_bf16)
    ms = _dot_exact_rhs01(o * o, ones_bd) * (1.0 / HEAD_DIM)
    o = o * lax.rsqrt(ms + EPS) * hgn_ref[...]
    y_a = (o * jax.nn.silu(proj(3))).astype(_bf16)

    u = proj(5) * proj(6)
    prev = carry_ref[...]
    rowi = _iota((ts, w), 0)
    u1 = jnp.where(rowi >= 1, pltpu.roll(u, 1, 0), jnp.broadcast_to(prev[7:8, :], (ts, w)))
    u2 = jnp.where(rowi >= 2, pltpu.roll(u, 2, 0),
                   jnp.where(rowi == 1, jnp.broadcast_to(prev[7:8, :], (ts, w)),
                             jnp.broadcast_to(prev[6:7, :], (ts, w))))
    carry_ref[...] = u[ts - 8:, :]
    cw = convw_ref[...]
    y_b = (proj(4) * (cw[0:1, :] * u2 + cw[1:2, :] * u1 + cw[2:3, :] * u)).astype(_bf16)

    g_a = jax.nn.sigmoid(proj(7, d_model))
    g_b = jax.nn.sigmoid(_dot(hb, win_ref[:, 7 * w + d_model:7 * w + 2 * d_model]))
    merged = g_a * _dot(y_a, pa_ref[...]) + g_b * _dot(y_b, pb_ref[...])
    out_ref[...] = x + _dot(merged.astype(_bf16), wo_ref[...])


def _const_spec(shape):
    nd = len(shape)
    return pl.BlockSpec(shape, lambda *_: (0,) * nd, pipeline_mode=pl.Buffered(1))


def _mix(x, norm_g, w_in, lb, hg_norm_g, conv_w, w_a, w_b, w_o, *, ts):
    b_, s_, d = x.shape
    in_cols = w_in.shape[1]
    w = HG_WIDTH
    grid = (b_, s_ // ts)
    return pl.pallas_call(
        _mix_kernel,
        grid=grid,
        in_specs=[
            pl.BlockSpec((None, ts, d), lambda b, s: (b, s, 0)),
            _const_spec((1, d)),
            _const_spec((d, in_cols)),
            _const_spec((1, w)),
            _const_spec((1, w)),
            _const_spec((CONV_K, w)),
            _const_spec((w, d)),
            _const_spec((w, d)),
            _const_spec((d, d)),
        ],
        out_specs=pl.BlockSpec((None, ts, d), lambda b, s: (b, s, 0)),
        out_shape=jax.ShapeDtypeStruct((b_, s_, d), _f32),
        scratch_shapes=[
            pltpu.VMEM((N_GROUPS, GROUP, GROUP), _f32),
            pltpu.VMEM((8, w), _f32),
            pltpu.VMEM((ts, w), _f32),
            pltpu.VMEM((ts, w), _f32),
            pltpu.VMEM((ts, w), _f32),
            pltpu.VMEM((ts, w), _f32),
            pltpu.VMEM((ts, w), _f32),
        ],
        compiler_params=pltpu.CompilerParams(
            dimension_semantics=("parallel", "arbitrary"),
            vmem_limit_bytes=VMEM_LIMIT_BYTES),
        name="mix",
    )(x, norm_g, w_in, lb, hg_norm_g, conv_w, w_a, w_b, w_o)


def _stair_pairs():
    pairs = [(a, c) for a in range(PEER_TOPK) for c in range(PEER_TOPK) if (a + 1) * (c + 1) <= PEER_TOPK]
    rows = -(-len(pairs) // 8) * 8
    ranks = jnp.arange(PEER_TOPK)[None, :]
    a_col = jnp.asarray([a for a, _ in pairs] + [-1] * (rows - len(pairs)))[:, None]
    c_col = jnp.asarray([c for _, c in pairs] + [-1] * (rows - len(pairs)))[:, None]
    pad = jnp.where(a_col < 0, -jnp.inf, 0.0).astype(_f32)
    return (a_col == ranks).astype(_bf16), (c_col == ranks).astype(_bf16), pad


ROUTE_LANES = 128


def _route_kernel(x_ref, g_ref, wq_ref, k1_ref, k2_ref, sa_ref, sc_ref, pad_ref, h_ref, idx_ref, gate_ref,
                  idx_t, e_t, top_ref):
    tr = x_ref.shape[0]
    n_cand = sa_ref.shape[0]
    x = x_ref[...]
    h = x * lax.rsqrt(jnp.mean(x * x, axis=-1, keepdims=True) + EPS) * g_ref[...]
    h_ref[...] = h
    hb = h.astype(_bf16)

    key_row = _iota((PEER_NKEYS, ROUTE_LANES), 0).astype(_f32)
    cand_row = _iota((n_cand, tr), 0).astype(_f32)
    neg_inf = jnp.float32(-jnp.inf)

    def extract_max(s):
        m = jnp.max(s, axis=0, keepdims=True)
        i = jnp.min(jnp.where(s == m, key_row, float(PEER_NKEYS)), axis=0, keepdims=True)
        return m, i, jnp.where(key_row == i, neg_inf, s)

    def head_body(hd, carry):
        q1 = _dot(hb, wq_ref[hd, 0]).astype(_bf16)
        q2 = _dot(hb, wq_ref[hd, 1]).astype(_bf16)
        s1 = _dot_nt(k1_ref[hd], q1)
        s2 = _dot_nt(k2_ref[hd], q2)

        for lt in range(tr // ROUTE_LANES):
            lanes = slice(lt * ROUTE_LANES, (lt + 1) * ROUTE_LANES)

            def half_body(k, c):
                m1, i1, r1 = extract_max(c[0])
                m2, i2, r2 = extract_max(c[1])
                for j, row in enumerate((m1, i1, m2, i2)):
                    top_ref[lt, j, pl.ds(k, 1), :] = row
                return r1, r2

            lax.fori_loop(0, PEER_TOPK, half_body, (s1[:, lanes], s2[:, lanes]))

        def top(j):
            return jnp.concatenate([top_ref[lt, j] for lt in range(tr // ROUTE_LANES)], axis=1)

        sa = sa_ref[...]
        sc = sc_ref[...]
        cand_s = _dot_exact_lhs01(sa, top(0)) + _dot_exact_lhs01(sc, top(2)) + pad_ref[...]
        cand_i = (_dot(sa, top(1).astype(_bf16)) * float(PEER_NKEYS)
                  + _dot(sc, top(3).astype(_bf16)))

        def pick_body(k, c):
            cand_s, denom, m_first = c
            m = jnp.max(cand_s, axis=0, keepdims=True)
            pos = jnp.min(jnp.where(cand_s == m, cand_row, float(n_cand)), axis=0, keepdims=True)
            hit = cand_row == pos
            eid = jnp.max(jnp.where(hit, cand_i, -1.0), axis=0, keepdims=True)
            m_first = jnp.where(k == 0, m, m_first)
            e = jnp.exp(m - m_first)
            slot = hd * PEER_TOPK + k
            idx_t[pl.ds(slot, 1), :] = eid
            e_t[pl.ds(slot, 1), :] = e
            return jnp.where(hit, neg_inf, cand_s), denom + e, m_first

        zero_row = jnp.zeros((1, tr), _f32)
        _, denom, _ = lax.fori_loop(0, PEER_TOPK, pick_body, (cand_s, zero_row, zero_row))
        rows = pl.ds(pl.multiple_of(hd * PEER_TOPK, PEER_TOPK), PEER_TOPK)
        e_t[rows, :] = e_t[rows, :] / denom
        return carry

    lax.fori_loop(0, PEER_HEADS, head_body, 0)
    idx_ref[...] = idx_t[...].T.astype(jnp.int32)
    gate_ref[...] = e_t[...].T


def _route(x, norm_g, w_query, keys1, keys2, *, tr):
    t, d = x.shape
    sel_a, sel_c, pad = _stair_pairs()
    pad = jnp.broadcast_to(pad, (pad.shape[0], tr))
    return pl.pallas_call(
        _route_kernel,
        grid=(t // tr,),
        in_specs=[
            pl.BlockSpec((tr, d), lambda i: (i, 0)),
            _const_spec((1, d)),
            _const_spec(w_query.shape),
            _const_spec(keys1.shape),
            _const_spec(keys2.shape),
            _const_spec(sel_a.shape),
            _const_spec(sel_c.shape),
            _const_spec(pad.shape),
        ],
        out_specs=[
            pl.BlockSpec((tr, d), lambda i: (i, 0)),
            pl.BlockSpec((tr, PEER_SLOTS), lambda i: (i, 0)),
            pl.BlockSpec((tr, PEER_SLOTS), lambda i: (i, 0)),
        ],
        out_shape=[
            jax.ShapeDtypeStruct((t, d), _f32),
            jax.ShapeDtypeStruct((t, PEER_SLOTS), jnp.int32),
            jax.ShapeDtypeStruct((t, PEER_SLOTS), _f32),
        ],
        scratch_shapes=[
            pltpu.VMEM((PEER_SLOTS, tr), _f32),
            pltpu.VMEM((PEER_SLOTS, tr), _f32),
            pltpu.VMEM((tr // ROUTE_LANES, 4, PEER_TOPK, ROUTE_LANES), _f32),
        ],
        compiler_params=pltpu.CompilerParams(
            dimension_semantics=("parallel",),
            vmem_limit_bytes=VMEM_LIMIT_BYTES),
        name="route",
    )(x, norm_g, w_query, keys1, keys2, sel_a, sel_c, pad)


SC_CORES = 2
SC_SUBCORES = 16
SC_LANES = 16
SC_WORKERS = SC_CORES * SC_SUBCORES
SC_TOKENS = 8
SC_GATHER = 32
SC_BLOCK = 8
SC_UNROLL = 2
SC_COLS = 8
HI_MASK = -65536


def _sc_mesh():
    return plsc.VectorSubcoreMesh(core_axis_name="c", subcore_axis_name="s")


def _sc_worker_base(per_worker):
    return (lax.axis_index("s") * SC_CORES + lax.axis_index("c")) * per_worker


def _sc_gather_loop(tab_hbm, idx_v, bufs, consume):
    n_parts = PEER_SLOTS // SC_GATHER
    n_gathers = SC_TOKENS * n_parts

    def gather(g, parity):
        rows, sem = bufs[parity]
        i = g // n_parts
        col = pl.multiple_of((g % n_parts) * SC_GATHER, SC_GATHER)
        return pltpu.make_async_copy(tab_hbm.at[idx_v.at[i, pl.ds(col, SC_GATHER)]], rows, sem)

    gather(0, 0).start()

    @pl.loop(0, n_gathers // 2)
    def _(pair):
        g = 2 * pair
        gather(g + 1, 1).start()
        gather(g, 0).wait()
        consume(g // n_parts, g % n_parts, bufs[0][0])

        @pl.when(g + 2 < n_gathers)
        def _():
            gather(g + 2, 0).start()

        gather(g + 1, 1).wait()
        consume((g + 1) // n_parts, (g + 1) % n_parts, bufs[1][0])


def _pack_halves(tab):
    half = tab.shape[1] // 2
    bits = lax.bitcast_convert_type(tab.astype(_bf16), jnp.uint16).astype(jnp.uint32)
    return lax.bitcast_convert_type(bits[:, :half] | (bits[:, half:] << 16), jnp.int32)


def _sc_unpack(words):
    lo = lax.bitcast_convert_type(words << 16, _f32)
    hi = lax.bitcast_convert_type(words & HI_MASK, _f32)
    return lo, hi


def _sc_udot(idx, tab, h):
    t, d = h.shape
    per_worker = t // SC_WORKERS
    half = d // 2

    @functools.partial(
        pl.kernel, mesh=_sc_mesh(),
        out_type=jax.ShapeDtypeStruct((t, PEER_SLOTS), _f32),
        scratch_types=[
            pltpu.VMEM((SC_TOKENS, PEER_SLOTS), jnp.int32),
            pltpu.VMEM((SC_TOKENS, d), _f32),
            pltpu.VMEM((SC_TOKENS, PEER_SLOTS), _f32),
            pltpu.VMEM((SC_GATHER, half), jnp.int32),
            pltpu.VMEM((SC_GATHER, half), jnp.int32),
            pltpu.SemaphoreType.DMA,
            pltpu.SemaphoreType.DMA,
        ],
        compiler_params=pltpu.CompilerParams(needs_layout_passes=False),
        name="sc_udot",
    )
    def body(idx_hbm, tab_hbm, h_hbm, out_hbm, idx_v, h_v, acts_v, rows0, rows1, sem0, sem1):
        base = _sc_worker_base(per_worker)
        lane = lax.iota(jnp.int32, SC_LANES)

        def dots(i, part, rows):
            def block_body(blk, carry):
                row0 = blk * SC_LANES
                outv = jnp.zeros((SC_LANES,), _f32)
                for sub in range(SC_LANES // SC_BLOCK):
                    def chunk_body(jj, accs):
                        accs = list(accs)
                        for u in range(SC_UNROLL):
                            off = pl.multiple_of((jj * SC_UNROLL + u) * SC_LANES, SC_LANES)
                            h_lo = h_v[i, pl.ds(off, SC_LANES)]
                            h_hi = h_v[i, pl.ds(half + off, SC_LANES)]
                            for e in range(SC_BLOCK):
                                lo, hi = _sc_unpack(rows[row0 + sub * SC_BLOCK + e, pl.ds(off, SC_LANES)])
                                accs[e] = accs[e] + (lo * h_lo + hi * h_hi)
                        return tuple(accs)

                    accs = lax.fori_loop(0, half // SC_LANES // SC_UNROLL, chunk_body,
                                         tuple(jnp.zeros((SC_LANES,), _f32) for _ in range(SC_BLOCK)))
                    for e in range(SC_BLOCK):
                        outv = jnp.where(lane == sub * SC_BLOCK + e, jnp.sum(accs[e]), outv)
                col = pl.multiple_of(part * SC_GATHER + row0, SC_LANES)
                acts_v[i, pl.ds(col, SC_LANES)] = outv
                return carry

            lax.fori_loop(0, SC_GATHER // SC_LANES, block_body, 0)

        @pl.loop(0, per_worker // SC_TOKENS)
        def _(step):
            tok = pl.multiple_of(base + step * SC_TOKENS, SC_TOKENS)
            pltpu.sync_copy(idx_hbm.at[pl.ds(tok, SC_TOKENS)], idx_v)
            pltpu.sync_copy(h_hbm.at[pl.ds(tok, SC_TOKENS)], h_v)
            _sc_gather_loop(tab_hbm, idx_v, ((rows0, sem0), (rows1, sem1)), dots)
            pltpu.sync_copy(acts_v, out_hbm.at[pl.ds(tok, SC_TOKENS)])

    return body(idx, tab, h)


def _sc_vaxpy(idx, w, tab, *, first, count):
    half = tab.shape[1]
    d = 2 * half
    per_worker = count // SC_WORKERS
    span = SC_COLS * SC_LANES

    @functools.partial(
        pl.kernel, mesh=_sc_mesh(),
        out_type=jax.ShapeDtypeStruct((count, d), _f32),
        scratch_types=[
            pltpu.VMEM((SC_TOKENS, PEER_SLOTS), jnp.int32),
            pltpu.VMEM((SC_TOKENS, PEER_SLOTS), _f32),
            pltpu.VMEM((SC_TOKENS, d), _f32),
            pltpu.VMEM((SC_GATHER, half), jnp.int32),
            pltpu.VMEM((SC_GATHER, half), jnp.int32),
            pltpu.SemaphoreType.DMA,
            pltpu.SemaphoreType.DMA,
        ],
        compiler_params=pltpu.CompilerParams(needs_layout_passes=False),
        name="sc_vaxpy",
    )
    def body(idx_hbm, w_hbm, tab_hbm, out_hbm, idx_v, w_v, out_v, rows0, rows1, sem0, sem1):
        base = _sc_worker_base(per_worker)

        def accumulate(i, part, rows):
            i_vec = jnp.full((SC_LANES,), i, jnp.int32)

            def span_body(cq, carry):
                def cols(c, offset=0):
                    return pl.ds(pl.multiple_of(offset + cq * span + c * SC_LANES, SC_LANES), SC_LANES)

                def expert_body(e, accs):
                    k_vec = jnp.full((SC_LANES,), part * SC_GATHER + e, jnp.int32)
                    wv = plsc.load_gather(w_v, [i_vec, k_vec])
                    new = []
                    for c in range(SC_COLS):
                        lo, hi = _sc_unpack(rows[e, cols(c)])
                        new += [accs[2 * c] + lo * wv, accs[2 * c + 1] + hi * wv]
                    return tuple(new)

                init = []
                for c in range(SC_COLS):
                    init += [out_v[i, cols(c)], out_v[i, cols(c, half)]]
                accs = lax.fori_loop(0, SC_GATHER, expert_body, tuple(init))
                for c in range(SC_COLS):
                    out_v[i, cols(c)] = accs[2 * c]
                    out_v[i, cols(c, half)] = accs[2 * c + 1]
                return carry

            lax.fori_loop(0, half // span, span_body, 0)

        @pl.loop(0, per_worker // SC_TOKENS)
        def _(step):
            off = pl.multiple_of(base + step * SC_TOKENS, SC_TOKENS)
            pltpu.sync_copy(idx_hbm.at[pl.ds(first + off, SC_TOKENS)], idx_v)
            pltpu.sync_copy(w_hbm.at[pl.ds(first + off, SC_TOKENS)], w_v)

            @pl.loop(0, SC_TOKENS)
            def _(i):
                @pl.loop(0, d // SC_LANES)
                def _(j):
                    out_v[i, pl.ds(pl.multiple_of(j * SC_LANES, SC_LANES), SC_LANES)] = (
                        jnp.zeros((SC_LANES,), _f32))

            _sc_gather_loop(tab_hbm, idx_v, ((rows0, sem0), (rows1, sem1)), accumulate)
            pltpu.sync_copy(out_v, out_hbm.at[pl.ds(off, SC_TOKENS)])

    return body(idx, w, tab)


ROW_CHUNKS = 8
PACK_ROWS = ROW_CHUNKS // 2


def _pack_table(tab):
    n, d = tab.shape
    bits = lax.bitcast_convert_type(tab.astype(_bf16), jnp.uint16).astype(jnp.uint32)
    bits = bits.reshape(n, PACK_ROWS, 2, d // ROW_CHUNKS)
    word = bits[:, :, 0, :] | (bits[:, :, 1, :] << 16)
    return lax.bitcast_convert_type(word, jnp.int32)


def _gate_weights_kernel(acts_ref, gate_ref, w_ref):
    a = acts_ref[...]
    gelu = 0.5 * a * (1.0 + lax.erf(a * (2.0 ** -0.5)))
    w_ref[...] = gate_ref[...] * gelu


def _gate_weights(acts, gate, *, tw):
    t, n = acts.shape
    spec = pl.BlockSpec((tw, n), lambda i: (i, 0))
    return pl.pallas_call(
        _gate_weights_kernel,
        grid=(t // tw,),
        in_specs=[spec, spec],
        out_specs=spec,
        out_shape=jax.ShapeDtypeStruct((t, n), _f32),
        compiler_params=pltpu.CompilerParams(dimension_semantics=("parallel",)),
        name="gate_weights",
    )(acts, gate)


def _gather_rows(idx_ref, t, tab_ref, rows_ref):
    for k in range(PEER_SLOTS):
        row = pl.multiple_of(idx_ref[t, k], PACK_ROWS)
        rows_ref[k * PACK_ROWS:(k + 1) * PACK_ROWS, :] = tab_ref[pl.ds(row, PACK_ROWS), :]


def _rows_matrix(rows_ref):
    return pltpu.bitcast(rows_ref[...], _bf16)


def _token_loop(tb, idx_ref, tab_ref, rows_a, rows_b, compute):
    _gather_rows(idx_ref, 0, tab_ref, rows_a)

    def pair_body(i, carry):
        t0 = 2 * i
        _gather_rows(idx_ref, t0 + 1, tab_ref, rows_b)
        compute(t0, _rows_matrix(rows_a))
        _gather_rows(idx_ref, jnp.minimum(t0 + 2, tb - 1), tab_ref, rows_a)
        compute(t0 + 1, _rows_matrix(rows_b))
        return carry

    lax.fori_loop(0, tb // 2, pair_body, 0)


def _chunk_diag_mask():
    shape = (ROW_CHUNKS, PEER_SLOTS * ROW_CHUNKS)
    return (_iota(shape, 1) % ROW_CHUNKS) == _iota(shape, 0)


def _vaxpy_kernel(idx_ref, w_ref, tab_ref, out_ref, rows_a, rows_b, wrep_ref):
    tb = out_ref.shape[0]
    diag = _chunk_diag_mask()
    shape = (PEER_SLOTS, PEER_SLOTS * ROW_CHUNKS)
    spread = (_iota(shape, 0) == (_iota(shape, 1) // ROW_CHUNKS)).astype(_bf16)
    wrep_ref[...] = _dot_exact_rhs01(w_ref[...], spread)

    def compute(t, m):
        w_row = jnp.broadcast_to(wrep_ref[pl.ds(t, 1), :], diag.shape)
        w_hi, w_lo = _split2(jnp.where(diag, w_row, 0.0))
        out_ref[t] = _dot(w_hi, m) + _dot(w_lo, m)

    _token_loop(tb, idx_ref, tab_ref, rows_a, rows_b, compute)


def _vaxpy(idx, w, tab, *, tb, count):
    return pl.pallas_call(
        _vaxpy_kernel,
        grid=(count // tb,),
        in_specs=[
            pl.BlockSpec((tb, PEER_SLOTS), lambda i: (i, 0), memory_space=pltpu.SMEM),
            pl.BlockSpec((tb, PEER_SLOTS), lambda i: (i, 0)),
            _const_spec(tab.shape),
        ],
        out_specs=pl.BlockSpec((tb, ROW_CHUNKS, 128), lambda i: (i, 0, 0)),
        out_shape=jax.ShapeDtypeStruct((count, ROW_CHUNKS, 128), _f32),
        scratch_shapes=[
            pltpu.VMEM((PEER_SLOTS * PACK_ROWS, 128), jnp.int32),
            pltpu.VMEM((PEER_SLOTS * PACK_ROWS, 128), jnp.int32),
            pltpu.VMEM((tb, PEER_SLOTS * ROW_CHUNKS), _f32),
        ],
        compiler_params=pltpu.CompilerParams(
            dimension_semantics=("parallel",),
            vmem_limit_bytes=VMEM_LIMIT_BYTES),
        name="vaxpy",
    )(idx, w, tab)


def _final_kernel(x_ref, p_ref, g_ref, out_ref, *, normalize):
    x = x_ref[...] + p_ref[...]
    if normalize:
        x = x * lax.rsqrt(jnp.mean(x * x, axis=-1, keepdims=True) + EPS) * g_ref[...]
    out_ref[...] = x


def _final(x, peer, g, *, normalize, tf, first):
    t, d = peer.shape
    off = first // tf
    return pl.pallas_call(
        functools.partial(_final_kernel, normalize=normalize),
        grid=(t // tf,),
        in_specs=[
            pl.BlockSpec((tf, d), lambda i: (i + off, 0)),
            pl.BlockSpec((tf, d), lambda i: (i, 0)),
            _const_spec((1, d)),
        ],
        out_specs=pl.BlockSpec((tf, d), lambda i: (i, 0)),
        out_shape=jax.ShapeDtypeStruct((t, d), _f32),
        compiler_params=pltpu.CompilerParams(dimension_semantics=("parallel",)),
        name="final_norm",
    )(x, peer, g)


def kernel(x, norm_mix_g, w_in, hg_lb_logits, hg_out_norm_g, conv_w, w_branch_hg, w_branch_conv, w_out, norm_ffn_g, peer_w_query, peer_keys1, peer_keys2, peer_u, peer_v, norm_final_g):
    b_, s_, d = x.shape
    depth = w_in.shape[0]
    lb_all = jnp.cumsum(jax.nn.softmax(hg_lb_logits.astype(_f32), axis=0), axis=0)
    n_groups = BATCH_GROUPS if b_ % BATCH_GROUPS == 0 else 1
    bg = b_ // n_groups
    tg = bg * s_
    n_sc = SC_SHARE if SC_SHARE < tg else 0
    n_tc = tg - n_sc
    for l in range(depth):
        wq = peer_w_query[l].astype(_bf16).reshape(d, PEER_HEADS, 2, PEER_HALF).transpose(1, 2, 0, 3)
        v_tab = _pack_table(peer_v[l])
        u_sc = _pack_halves(peer_u[l])
        v_sc = _pack_halves(peer_v[l])
        last = l == depth - 1
        g = norm_final_g[None] if last else jnp.ones((1, d), _f32)
        def front(c, x_in, sc_before=None):
            xc = _mix(x_in, norm_mix_g[l][None], w_in[l].astype(_bf16), lb_all[l][None],
                      hg_out_norm_g[l][None], conv_w[l], w_branch_hg[l].astype(_bf16),
                      w_branch_conv[l].astype(_bf16), w_out[l].astype(_bf16), ts=MIX_TILE)
            xf = xc.reshape(tg, d)
            h, idx, gate = _route(xf, norm_ffn_g[l][None], wq,
                                  peer_keys1[l].astype(_bf16), peer_keys2[l].astype(_bf16), tr=ROUTE_TILE)
            idx_sc = idx if sc_before is None else lax.optimization_barrier((idx, sc_before))[0]
            return xf, idx, gate, _sc_udot(idx_sc, u_sc, h)

        def back(xf, idx, gate, acts):
            w = _gate_weights(acts, gate, tw=FINAL_TILE)
            peer_tc = _vaxpy(idx * PACK_ROWS, w, v_tab.reshape(-1, 128), tb=EXPERT_TILE, count=n_tc).reshape(n_tc, d)
            parts = [_final(xf, peer_tc, g, normalize=last, tf=FINAL_TILE, first=0)]
            peer_sc = None
            if n_sc:
                peer_sc = _sc_vaxpy(idx, w, v_sc, first=n_tc, count=n_sc)
                parts.append(_final(xf, peer_sc, g, normalize=last, tf=FINAL_TILE, first=n_tc))
            return peer_tc, peer_sc, parts

        groups = [x[c * bg:(c + 1) * bg] for c in range(n_groups)]
        fronts = {c: front(c, groups[c]) for c in range(min(GROUPS_AHEAD + 1, n_groups))}
        outs, peer_tc = [], None
        for c in range(n_groups):
            xf, idx, gate, acts = fronts.pop(c)
            after = [fronts[c + GROUPS_AHEAD][1]] if c + GROUPS_AHEAD in fronts else []
            after += [peer_tc] if peer_tc is not None else []
            if after:
                acts = lax.optimization_barrier((acts, *after))[0]
            peer_tc, peer_sc, parts = back(xf, idx, gate, acts)
            outs.extend(parts)
            nxt = c + GROUPS_AHEAD + 1
            if nxt < n_groups:
                fronts[nxt] = front(nxt, lax.optimization_barrier((groups[nxt], peer_tc))[0], peer_sc)
        x = jnp.concatenate(outs, axis=0).reshape(b_, s_, d)
    return x
```

```python
import functools

import jax
import jax.numpy as jnp
from jax import lax
from jax.experimental import pallas as pl
from jax.experimental.pallas import tpu as pltpu
from jax.experimental.pallas import tpu_sc as plsc

EPS = 1e-6
CHUNK = 64
SUB = 16
HEADS = 8
HEAD_DIM = 64
HG_WIDTH = HEADS * HEAD_DIM
GROUP = 256
N_GROUPS = HG_WIDTH // GROUP
CONV_K = 3
PEER_HEADS = 8
PEER_NKEYS = 128
PEER_HALF = 128
PEER_TOPK = 16
PEER_SLOTS = PEER_HEADS * PEER_TOPK

VMEM_LIMIT_BYTES = 56 * 1024 * 1024

MIX_TILE = 256
ROUTE_TILE = 256
EXPERT_TILE = 128
FINAL_TILE = 512
BATCH_GROUPS = 4
SC_SHARE = 3072
GROUPS_AHEAD = 2

_f32 = jnp.float32
_bf16 = jnp.bfloat16


def _dot(a, b):
    return jnp.dot(a, b, preferred_element_type=_f32)


def _dot_nt(a, b):
    return lax.dot_general(a, b, (((1,), (1,)), ((), ())), preferred_element_type=_f32)


def _dot_tn(a, b):
    return lax.dot_general(a, b, (((0,), (0,)), ((), ())), preferred_element_type=_f32)


def _split3(x):
    hi = x.astype(_bf16)
    r1 = x - hi.astype(_f32)
    mid = r1.astype(_bf16)
    lo = (r1 - mid.astype(_f32)).astype(_bf16)
    return hi, mid, lo


def _split2(x):
    hi = x.astype(_bf16)
    lo = (x - hi.astype(_f32)).astype(_bf16)
    return hi, lo


def _dot_exact_rhs01(x, m01):
    hi, mid, lo = _split3(x)
    return _dot(hi, m01) + _dot(mid, m01) + _dot(lo, m01)


def _dot_exact_lhs01(m01, x):
    hi, mid, lo = _split3(x)
    return _dot(m01, hi) + _dot(m01, mid) + _dot(m01, lo)


def _iota(shape, dim):
    return lax.broadcasted_iota(jnp.int32, shape, dim)


def _hgrn2_chunk(q, k, lf, v, state_ref):
    n_sub = CHUNK // SUB
    row = _iota((CHUNK, CHUNK), 0)
    col = _iota((CHUNK, CHUNK), 1)
    tril = (col <= row).astype(_bf16)
    b = _dot_exact_lhs01(tril, lf)

    b_end = [b[(j + 1) * SUB - 1:(j + 1) * SUB, :] for j in range(n_sub)]
    b_end_rows = jnp.concatenate([jnp.broadcast_to(e, (SUB, HG_WIDTH)) for e in b_end], axis=0)
    b_last = b_end[-1]

    q_in = (q * jnp.exp(b)).astype(_bf16)
    k_sub = (k * jnp.exp(b_end_rows - b)).astype(_bf16)
    k_out = (k * jnp.exp(b_last - b)).astype(_bf16)
    q_from = [(q * jnp.exp(jnp.minimum(b - b_end[j], 0.0))).astype(_bf16) for j in range(n_sub - 1)]
    v_b = v.astype(_bf16)

    gr = _iota((GROUP, GROUP), 0) // HEAD_DIM
    gc = _iota((GROUP, GROUP), 1) // HEAD_DIM
    head_mask = gr == gc
    t_blk = _iota((CHUNK, GROUP), 0) // SUB
    s_blk = (_iota((CHUNK, GROUP), 1) % HEAD_DIM) // SUB

    outs = []
    for g in range(N_GROUPS):
        sl = slice(g * GROUP, (g + 1) * GROUP)
        st = state_ref[g]
        o_g = _dot_nt(q_in[:, sl], st.astype(_bf16))

        zero_b = jnp.zeros((), _bf16)
        k_bd = jnp.where(head_mask, jnp.concatenate([k_sub[:, sl]] * (GROUP // CHUNK), axis=0), zero_b)
        v_bd = jnp.where(head_mask, jnp.concatenate([v_b[:, sl]] * (GROUP // CHUNK), axis=0), zero_b)
        q_stack = jnp.concatenate([qf[:, sl] for qf in q_from], axis=0)
        r = _dot_nt(q_stack, k_bd)
        scores = jnp.zeros((CHUNK, GROUP), _f32)
        for j in range(n_sub - 1):
            sel = (s_blk == j) & (t_blk > j)
            scores = jnp.where(sel, r[j * CHUNK:(j + 1) * CHUNK, :], scores)
        o_g = o_g + _dot(scores.astype(_bf16), v_bd)
        outs.append(o_g)

        upd = _dot_tn(v_b[:, sl], k_out[:, sl])
        decay = jnp.exp(b_last[:, sl])
        state_ref[g] = st * decay + jnp.where(head_mask, upd, 0.0)
    o = jnp.concatenate(outs, axis=1)

    ones_bd = ((_iota((HG_WIDTH, HG_WIDTH), 0) // HEAD_DIM)
               == (_iota((HG_WIDTH, HG_WIDTH), 1) // HEAD_DIM)).astype(_bf16)
    t_in_sub = _iota((CHUNK, HG_WIDTH), 0) % SUB
    for lag in range(SUB):
        if lag == 0:
            p = q * k
            v_l = v
        else:
            valid = t_in_sub >= lag
            k_l = pltpu.roll(k, lag, 0)
            b_l = pltpu.roll(b, lag, 0)
            v_l = pltpu.roll(v, lag, 0)
            p = jnp.where(valid, q * k_l * jnp.exp(jnp.minimum(b - b_l, 0.0)), 0.0)
        s_l = _dot(p.astype(_bf16), ones_bd)
        o = o + s_l * v_l
    return o


def _mix_kernel(x_ref, g_ref, win_ref, lb_ref, hgn_ref, convw_ref, pa_ref, pb_ref, wo_ref,
                out_ref, state_ref, carry_ref, q_s, k_s, lf_s, v_s, o_s):
    ts = x_ref.shape[0]
    d_model = x_ref.shape[1]
    w = HG_WIDTH

    @pl.when(pl.program_id(1) == 0)
    def _():
        state_ref[...] = jnp.zeros_like(state_ref)
        carry_ref[...] = jnp.zeros_like(carry_ref)

    x = x_ref[...]
    h = x * lax.rsqrt(jnp.mean(x * x, axis=-1, keepdims=True) + EPS) * g_ref[...]
    hb = h.astype(_bf16)

    def proj(i, width=w):
        return _dot(hb, win_ref[:, i * w:i * w + width])

    lb = lb_ref[...]
    q_s[...] = jax.nn.silu(proj(0)) * (HEAD_DIM ** -0.5)
    forget = lb + (1.0 - lb) * jax.nn.sigmoid(proj(1))
    k_s[...] = 1.0 - forget
    lf_s[...] = jnp.log(forget)
    v_s[...] = proj(2)

    def chunk_body(c, carry):
        rows = pl.ds(pl.multiple_of(c * CHUNK, CHUNK), CHUNK)
        o_s[rows, :] = _hgrn2_chunk(q_s[rows, :], k_s[rows, :], lf_s[rows, :], v_s[rows, :], state_ref)
        return carry

    lax.fori_loop(0, ts // CHUNK, chunk_body, 0)

    o = o_s[...]
    ones_bd = ((_iota((w, w), 0) // HEAD_DIM) == (_iota((w, w), 1) // HEAD_DIM)).astype(_bf16)
    ms = _dot_exact_rhs01(o * o, ones_bd) * (1.0 / HEAD_DIM)
    o = o * lax.rsqrt(ms + EPS) * hgn_ref[...]
    y_a = (o * jax.nn.silu(proj(3))).astype(_bf16)

    u = proj(5) * proj(6)
    prev = carry_ref[...]
    rowi = _iota((ts, w), 0)
    u1 = jnp.where(rowi >= 1, pltpu.roll(u, 1, 0), jnp.broadcast_to(prev[7:8, :], (ts, w)))
    u2 = jnp.where(rowi >= 2, pltpu.roll(u, 2, 0),
                   jnp.where(rowi == 1, jnp.broadcast_to(prev[7:8, :], (ts, w)),
                             jnp.broadcast_to(prev[6:7, :], (ts, w))))
    carry_ref[...] = u[ts - 8:, :]
    cw = convw_ref[...]
    y_b = (proj(4) * (cw[0:1, :] * u2 + cw[1:2, :] * u1 + cw[2:3, :] * u)).astype(_bf16)

    g_a = jax.nn.sigmoid(proj(7, d_model))
    g_b = jax.nn.sigmoid(_dot(hb, win_ref[:, 7 * w + d_model:7 * w + 2 * d_model]))
    merged = g_a * _dot(y_a, pa_ref[...]) + g_b * _dot(y_b, pb_ref[...])
    out_ref[...] = x + _dot(merged.astype(_bf16), wo_ref[...])


def _const_spec(shape):
    nd = len(shape)
    return pl.BlockSpec(shape, lambda *_: (0,) * nd, pipeline_mode=pl.Buffered(1))


def _mix(x, norm_g, w_in, lb, hg_norm_g, conv_w, w_a, w_b, w_o, *, ts):
    b_, s_, d = x.shape
    in_cols = w_in.shape[1]
    w = HG_WIDTH
    grid = (b_, s_ // ts)
    return pl.pallas_call(
        _mix_kernel,
        grid=grid,
        in_specs=[
            pl.BlockSpec((None, ts, d), lambda b, s: (b, s, 0)),
            _const_spec((1, d)),
            _const_spec((d, in_cols)),
            _const_spec((1, w)),
            _const_spec((1, w)),
            _const_spec((CONV_K, w)),
            _const_spec((w, d)),
            _const_spec((w, d)),
            _const_spec((d, d)),
        ],
        out_specs=pl.BlockSpec((None, ts, d), lambda b, s: (b, s, 0)),
        out_shape=jax.ShapeDtypeStruct((b_, s_, d), _f32),
        scratch_shapes=[
            pltpu.VMEM((N_GROUPS, GROUP, GROUP), _f32),
            pltpu.VMEM((8, w), _f32),
            pltpu.VMEM((ts, w), _f32),
            pltpu.VMEM((ts, w), _f32),
            pltpu.VMEM((ts, w), _f32),
            pltpu.VMEM((ts, w), _f32),
            pltpu.VMEM((ts, w), _f32),
        ],
        compiler_params=pltpu.CompilerParams(
            dimension_semantics=("parallel", "arbitrary"),
            vmem_limit_bytes=VMEM_LIMIT_BYTES),
        name="mix",
    )(x, norm_g, w_in, lb, hg_norm_g, conv_w, w_a, w_b, w_o)


def _stair_pairs():
    pairs = [(a, c) for a in range(PEER_TOPK) for c in range(PEER_TOPK) if (a + 1) * (c + 1) <= PEER_TOPK]
    rows = -(-len(pairs) // 8) * 8
    ranks = jnp.arange(PEER_TOPK)[None, :]
    a_col = jnp.asarray([a for a, _ in pairs] + [-1] * (rows - len(pairs)))[:, None]
    c_col = jnp.asarray([c for _, c in pairs] + [-1] * (rows - len(pairs)))[:, None]
    pad = jnp.where(a_col < 0, -jnp.inf, 0.0).astype(_f32)
    return (a_col == ranks).astype(_bf16), (c_col == ranks).astype(_bf16), pad


ROUTE_LANES = 128


def _route_kernel(x_ref, g_ref, wq_ref, k1_ref, k2_ref, sa_ref, sc_ref, pad_ref, h_ref, idx_ref, gate_ref,
                  idx_t, e_t, top_ref):
    tr = x_ref.shape[0]
    n_cand = sa_ref.shape[0]
    x = x_ref[...]
    h = x * lax.rsqrt(jnp.mean(x * x, axis=-1, keepdims=True) + EPS) * g_ref[...]
    h_ref[...] = h
    hb = h.astype(_bf16)

    key_row = _iota((PEER_NKEYS, ROUTE_LANES), 0).astype(_f32)
    cand_row = _iota((n_cand, tr), 0).astype(_f32)
    neg_inf = jnp.float32(-jnp.inf)

    def extract_max(s):
        m = jnp.max(s, axis=0, keepdims=True)
        i = jnp.min(jnp.where(s == m, key_row, float(PEER_NKEYS)), axis=0, keepdims=True)
        return m, i, jnp.where(key_row == i, neg_inf, s)

    def head_body(hd, carry):
        q1 = _dot(hb, wq_ref[hd, 0]).astype(_bf16)
        q2 = _dot(hb, wq_ref[hd, 1]).astype(_bf16)
        s1 = _dot_nt(k1_ref[hd], q1)
        s2 = _dot_nt(k2_ref[hd], q2)

        for lt in range(tr // ROUTE_LANES):
            lanes = slice(lt * ROUTE_LANES, (lt + 1) * ROUTE_LANES)

            def half_body(k, c):
                m1, i1, r1 = extract_max(c[0])
                m2, i2, r2 = extract_max(c[1])
                for j, row in enumerate((m1, i1, m2, i2)):
                    top_ref[lt, j, pl.ds(k, 1), :] = row
                return r1, r2

            lax.fori_loop(0, PEER_TOPK, half_body, (s1[:, lanes], s2[:, lanes]))

        def top(j):
            return jnp.concatenate([top_ref[lt, j] for lt in range(tr // ROUTE_LANES)], axis=1)

        sa = sa_ref[...]
        sc = sc_ref[...]
        cand_s = _dot_exact_lhs01(sa, top(0)) + _dot_exact_lhs01(sc, top(2)) + pad_ref[...]
        cand_i = (_dot(sa, top(1).astype(_bf16)) * float(PEER_NKEYS)
                  + _dot(sc, top(3).astype(_bf16)))

        def pick_body(k, c):
            cand_s, denom, m_first = c
            m = jnp.max(cand_s, axis=0, keepdims=True)
            pos = jnp.min(jnp.where(cand_s == m, cand_row, float(n_cand)), axis=0, keepdims=True)
            hit = cand_row == pos
            eid = jnp.max(jnp.where(hit, cand_i, -1.0), axis=0, keepdims=True)
            m_first = jnp.where(k == 0, m, m_first)
            e = jnp.exp(m - m_first)
            slot = hd * PEER_TOPK + k
            idx_t[pl.ds(slot, 1), :] = eid
            e_t[pl.ds(slot, 1), :] = e
            return jnp.where(hit, neg_inf, cand_s), denom + e, m_first

        zero_row = jnp.zeros((1, tr), _f32)
        _, denom, _ = lax.fori_loop(0, PEER_TOPK, pick_body, (cand_s, zero_row, zero_row))
        rows = pl.ds(pl.multiple_of(hd * PEER_TOPK, PEER_TOPK), PEER_TOPK)
        e_t[rows, :] = e_t[rows, :] / denom
        return carry

    lax.fori_loop(0, PEER_HEADS, head_body, 0)
    idx_ref[...] = idx_t[...].T.astype(jnp.int32)
    gate_ref[...] = e_t[...].T


def _route(x, norm_g, w_query, keys1, keys2, *, tr):
    t, d = x.shape
    sel_a, sel_c, pad = _stair_pairs()
    pad = jnp.broadcast_to(pad, (pad.shape[0], tr))
    return pl.pallas_call(
        _route_kernel,
        grid=(t // tr,),
        in_specs=[
            pl.BlockSpec((tr, d), lambda i: (i, 0)),
            _const_spec((1, d)),
            _const_spec(w_query.shape),
            _const_spec(keys1.shape),
            _const_spec(keys2.shape),
            _const_spec(sel_a.shape),
            _const_spec(sel_c.shape),
            _const_spec(pad.shape),
        ],
        out_specs=[
            pl.BlockSpec((tr, d), lambda i: (i, 0)),
            pl.BlockSpec((tr, PEER_SLOTS), lambda i: (i, 0)),
            pl.BlockSpec((tr, PEER_SLOTS), lambda i: (i, 0)),
        ],
        out_shape=[
            jax.ShapeDtypeStruct((t, d), _f32),
            jax.ShapeDtypeStruct((t, PEER_SLOTS), jnp.int32),
            jax.ShapeDtypeStruct((t, PEER_SLOTS), _f32),
        ],
        scratch_shapes=[
            pltpu.VMEM((PEER_SLOTS, tr), _f32),
            pltpu.VMEM((PEER_SLOTS, tr), _f32),
            pltpu.VMEM((tr // ROUTE_LANES, 4, PEER_TOPK, ROUTE_LANES), _f32),
        ],
        compiler_params=pltpu.CompilerParams(
            dimension_semantics=("parallel",),
            vmem_limit_bytes=VMEM_LIMIT_BYTES),
        name="route",
    )(x, norm_g, w_query, keys1, keys2, sel_a, sel_c, pad)


SC_CORES = 2
SC_SUBCORES = 16
SC_LANES = 16
SC_WORKERS = SC_CORES * SC_SUBCORES
SC_TOKENS = 8
SC_GATHER = 32
SC_BLOCK = 8
SC_UNROLL = 2
SC_COLS = 8
HI_MASK = -65536


def _sc_mesh():
    return plsc.VectorSubcoreMesh(core_axis_name="c", subcore_axis_name="s")


def _sc_worker_base(per_worker):
    return (lax.axis_index("s") * SC_CORES + lax.axis_index("c")) * per_worker


def _sc_gather_loop(tab_hbm, idx_v, bufs, consume):
    n_parts = PEER_SLOTS // SC_GATHER
    n_gathers = SC_TOKENS * n_parts

    def gather(g, parity):
        rows, sem = bufs[parity]
        i = g // n_parts
        col = pl.multiple_of((g % n_parts) * SC_GATHER, SC_GATHER)
        return pltpu.make_async_copy(tab_hbm.at[idx_v.at[i, pl.ds(col, SC_GATHER)]], rows, sem)

    gather(0, 0).start()

    @pl.loop(0, n_gathers // 2)
    def _(pair):
        g = 2 * pair
        gather(g + 1, 1).start()
        gather(g, 0).wait()
        consume(g // n_parts, g % n_parts, bufs[0][0])

        @pl.when(g + 2 < n_gathers)
        def _():
            gather(g + 2, 0).start()

        gather(g + 1, 1).wait()
        consume((g + 1) // n_parts, (g + 1) % n_parts, bufs[1][0])


def _pack_halves(tab):
    half = tab.shape[1] // 2
    bits = lax.bitcast_convert_type(tab.astype(_bf16), jnp.uint16).astype(jnp.uint32)
    return lax.bitcast_convert_type(bits[:, :half] | (bits[:, half:] << 16), jnp.int32)


def _sc_unpack(words):
    lo = lax.bitcast_convert_type(words << 16, _f32)
    hi = lax.bitcast_convert_type(words & HI_MASK, _f32)
    return lo, hi


def _sc_udot(idx, tab, h):
    t, d = h.shape
    per_worker = t // SC_WORKERS
    half = d // 2

    @functools.partial(
        pl.kernel, mesh=_sc_mesh(),
        out_type=jax.ShapeDtypeStruct((t, PEER_SLOTS), _f32),
        scratch_types=[
            pltpu.VMEM((SC_TOKENS, PEER_SLOTS), jnp.int32),
            pltpu.VMEM((SC_TOKENS, d), _f32),
            pltpu.VMEM((SC_TOKENS, PEER_SLOTS), _f32),
            pltpu.VMEM((SC_GATHER, half), jnp.int32),
            pltpu.VMEM((SC_GATHER, half), jnp.int32),
            pltpu.SemaphoreType.DMA,
            pltpu.SemaphoreType.DMA,
        ],
        compiler_params=pltpu.CompilerParams(needs_layout_passes=False),
        name="sc_udot",
    )
    def body(idx_hbm, tab_hbm, h_hbm, out_hbm, idx_v, h_v, acts_v, rows0, rows1, sem0, sem1):
        base = _sc_worker_base(per_worker)
        lane = lax.iota(jnp.int32, SC_LANES)

        def dots(i, part, rows):
            def block_body(blk, carry):
                row0 = blk * SC_LANES
                outv = jnp.zeros((SC_LANES,), _f32)
                for sub in range(SC_LANES // SC_BLOCK):
                    def chunk_body(jj, accs):
                        accs = list(accs)
                        for u in range(SC_UNROLL):
                            off = pl.multiple_of((jj * SC_UNROLL + u) * SC_LANES, SC_LANES)
                            h_lo = h_v[i, pl.ds(off, SC_LANES)]
                            h_hi = h_v[i, pl.ds(half + off, SC_LANES)]
                            for e in range(SC_BLOCK):
                                lo, hi = _sc_unpack(rows[row0 + sub * SC_BLOCK + e, pl.ds(off, SC_LANES)])
                                accs[e] = accs[e] + (lo * h_lo + hi * h_hi)
                        return tuple(accs)

                    accs = lax.fori_loop(0, half // SC_LANES // SC_UNROLL, chunk_body,
                                         tuple(jnp.zeros((SC_LANES,), _f32) for _ in range(SC_BLOCK)))
                    for e in range(SC_BLOCK):
                        outv = jnp.where(lane == sub * SC_BLOCK + e, jnp.sum(accs[e]), outv)
                col = pl.multiple_of(part * SC_GATHER + row0, SC_LANES)
                acts_v[i, pl.ds(col, SC_LANES)] = outv
                return carry

            lax.fori_loop(0, SC_GATHER // SC_LANES, block_body, 0)

        @pl.loop(0, per_worker // SC_TOKENS)
        def _(step):
            tok = pl.multiple_of(base + step * SC_TOKENS, SC_TOKENS)
            pltpu.sync_copy(idx_hbm.at[pl.ds(tok, SC_TOKENS)], idx_v)
            pltpu.sync_copy(h_hbm.at[pl.ds(tok, SC_TOKENS)], h_v)
            _sc_gather_loop(tab_hbm, idx_v, ((rows0, sem0), (rows1, sem1)), dots)
            pltpu.sync_copy(acts_v, out_hbm.at[pl.ds(tok, SC_TOKENS)])

    return body(idx, tab, h)


def _sc_vaxpy(idx, w, tab, *, first, count):
    half = tab.shape[1]
    d = 2 * half
    per_worker = count // SC_WORKERS
    span = SC_COLS * SC_LANES

    @functools.partial(
        pl.kernel, mesh=_sc_mesh(),
        out_type=jax.ShapeDtypeStruct((count, d), _f32),
        scratch_types=[
            pltpu.VMEM((SC_TOKENS, PEER_SLOTS), jnp.int32),
            pltpu.VMEM((SC_TOKENS, PEER_SLOTS), _f32),
            pltpu.VMEM((SC_TOKENS, d), _f32),
            pltpu.VMEM((SC_GATHER, half), jnp.int32),
            pltpu.VMEM((SC_GATHER, half), jnp.int32),
            pltpu.SemaphoreType.DMA,
            pltpu.SemaphoreType.DMA,
        ],
        compiler_params=pltpu.CompilerParams(needs_layout_passes=False),
        name="sc_vaxpy",
    )
    def body(idx_hbm, w_hbm, tab_hbm, out_hbm, idx_v, w_v, out_v, rows0, rows1, sem0, sem1):
        base = _sc_worker_base(per_worker)

        def accumulate(i, part, rows):
            i_vec = jnp.full((SC_LANES,), i, jnp.int32)

            def span_body(cq, carry):
                def cols(c, offset=0):
                    return pl.ds(pl.multiple_of(offset + cq * span + c * SC_LANES, SC_LANES), SC_LANES)

                def expert_body(e, accs):
                    k_vec = jnp.full((SC_LANES,), part * SC_GATHER + e, jnp.int32)
                    wv = plsc.load_gather(w_v, [i_vec, k_vec])
                    new = []
                    for c in range(SC_COLS):
                        lo, hi = _sc_unpack(rows[e, cols(c)])
                        new += [accs[2 * c] + lo * wv, accs[2 * c + 1] + hi * wv]
                    return tuple(new)

                init = []
                for c in range(SC_COLS):
                    init += [out_v[i, cols(c)], out_v[i, cols(c, half)]]
                accs = lax.fori_loop(0, SC_GATHER, expert_body, tuple(init))
                for c in range(SC_COLS):
                    out_v[i, cols(c)] = accs[2 * c]
                    out_v[i, cols(c, half)] = accs[2 * c + 1]
                return carry

            lax.fori_loop(0, half // span, span_body, 0)

        @pl.loop(0, per_worker // SC_TOKENS)
        def _(step):
            off = pl.multiple_of(base + step * SC_TOKENS, SC_TOKENS)
            pltpu.sync_copy(idx_hbm.at[pl.ds(first + off, SC_TOKENS)], idx_v)
            pltpu.sync_copy(w_hbm.at[pl.ds(first + off, SC_TOKENS)], w_v)

            @pl.loop(0, SC_TOKENS)
            def _(i):
                @pl.loop(0, d // SC_LANES)
                def _(j):
                    out_v[i, pl.ds(pl.multiple_of(j * SC_LANES, SC_LANES), SC_LANES)] = (
                        jnp.zeros((SC_LANES,), _f32))

            _sc_gather_loop(tab_hbm, idx_v, ((rows0, sem0), (rows1, sem1)), accumulate)
            pltpu.sync_copy(out_v, out_hbm.at[pl.ds(off, SC_TOKENS)])

    return body(idx, w, tab)


ROW_CHUNKS = 8
PACK_ROWS = ROW_CHUNKS // 2


def _pack_table(tab):
    n, d = tab.shape
    bits = lax.bitcast_convert_type(tab.astype(_bf16), jnp.uint16).astype(jnp.uint32)
    bits = bits.reshape(n, PACK_ROWS, 2, d // ROW_CHUNKS)
    word = bits[:, :, 0, :] | (bits[:, :, 1, :] << 16)
    return lax.bitcast_convert_type(word, jnp.int32)


def _gate_weights_kernel(acts_ref, gate_ref, w_ref):
    a = acts_ref[...]
    gelu = 0.5 * a * (1.0 + lax.erf(a * (2.0 ** -0.5)))
    w_ref[...] = gate_ref[...] * gelu


def _gate_weights(acts, gate, *, tw):
    t, n = acts.shape
    spec = pl.BlockSpec((tw, n), lambda i: (i, 0))
    return pl.pallas_call(
        _gate_weights_kernel,
        grid=(t // tw,),
        in_specs=[spec, spec],
        out_specs=spec,
        out_shape=jax.ShapeDtypeStruct((t, n), _f32),
        compiler_params=pltpu.CompilerParams(dimension_semantics=("parallel",)),
        name="gate_weights",
    )(acts, gate)


def _gather_rows(idx_ref, t, tab_ref, rows_ref):
    for k in range(PEER_SLOTS):
        row = pl.multiple_of(idx_ref[t, k], PACK_ROWS)
        rows_ref[k * PACK_ROWS:(k + 1) * PACK_ROWS, :] = tab_ref[pl.ds(row, PACK_ROWS), :]


def _rows_matrix(rows_ref):
    return pltpu.bitcast(rows_ref[...], _bf16)


def _token_loop(tb, idx_ref, tab_ref, rows_a, rows_b, compute):
    _gather_rows(idx_ref, 0, tab_ref, rows_a)

    def pair_body(i, carry):
        t0 = 2 * i
        _gather_rows(idx_ref, t0 + 1, tab_ref, rows_b)
        compute(t0, _rows_matrix(rows_a))
        _gather_rows(idx_ref, jnp.minimum(t0 + 2, tb - 1), tab_ref, rows_a)
        compute(t0 + 1, _rows_matrix(rows_b))
        return carry

    lax.fori_loop(0, tb // 2, pair_body, 0)


def _chunk_diag_mask():
    shape = (ROW_CHUNKS, PEER_SLOTS * ROW_CHUNKS)
    return (_iota(shape, 1) % ROW_CHUNKS) == _iota(shape, 0)


def _vaxpy_kernel(idx_ref, w_ref, tab_ref, out_ref, rows_a, rows_b, wrep_ref):
    tb = out_ref.shape[0]
    diag = _chunk_diag_mask()
    shape = (PEER_SLOTS, PEER_SLOTS * ROW_CHUNKS)
    spread = (_iota(shape, 0) == (_iota(shape, 1) // ROW_CHUNKS)).astype(_bf16)
    wrep_ref[...] = _dot_exact_rhs01(w_ref[...], spread)

    def compute(t, m):
        w_row = jnp.broadcast_to(wrep_ref[pl.ds(t, 1), :], diag.shape)
        w_hi, w_lo = _split2(jnp.where(diag, w_row, 0.0))
        out_ref[t] = _dot(w_hi, m) + _dot(w_lo, m)

    _token_loop(tb, idx_ref, tab_ref, rows_a, rows_b, compute)


def _vaxpy(idx, w, tab, *, tb, count):
    return pl.pallas_call(
        _vaxpy_kernel,
        grid=(count // tb,),
        in_specs=[
            pl.BlockSpec((tb, PEER_SLOTS), lambda i: (i, 0), memory_space=pltpu.SMEM),
            pl.BlockSpec((tb, PEER_SLOTS), lambda i: (i, 0)),
            _const_spec(tab.shape),
        ],
        out_specs=pl.BlockSpec((tb, ROW_CHUNKS, 128), lambda i: (i, 0, 0)),
        out_shape=jax.ShapeDtypeStruct((count, ROW_CHUNKS, 128), _f32),
        scratch_shapes=[
            pltpu.VMEM((PEER_SLOTS * PACK_ROWS, 128), jnp.int32),
            pltpu.VMEM((PEER_SLOTS * PACK_ROWS, 128), jnp.int32),
            pltpu.VMEM((tb, PEER_SLOTS * ROW_CHUNKS), _f32),
        ],
        compiler_params=pltpu.CompilerParams(
            dimension_semantics=("parallel",),
            vmem_limit_bytes=VMEM_LIMIT_BYTES),
        name="vaxpy",
    )(idx, w, tab)


def _final_kernel(x_ref, p_ref, g_ref, out_ref, *, normalize):
    x = x_ref[...] + p_ref[...]
    if normalize:
        x = x * lax.rsqrt(jnp.mean(x * x, axis=-1, keepdims=True) + EPS) * g_ref[...]
    out_ref[...] = x


def _final(x, peer, g, *, normalize, tf, first):
    t, d = peer.shape
    off = first // tf
    return pl.pallas_call(
        functools.partial(_final_kernel, normalize=normalize),
        grid=(t // tf,),
        in_specs=[
            pl.BlockSpec((tf, d), lambda i: (i + off, 0)),
            pl.BlockSpec((tf, d), lambda i: (i, 0)),
            _const_spec((1, d)),
        ],
        out_specs=pl.BlockSpec((tf, d), lambda i: (i, 0)),
        out_shape=jax.ShapeDtypeStruct((t, d), _f32),
        compiler_params=pltpu.CompilerParams(dimension_semantics=("parallel",)),
        name="final_norm",
    )(x, peer, g)


def kernel(x, norm_mix_g, w_in, hg_lb_logits, hg_out_norm_g, conv_w, w_branch_hg, w_branch_conv, w_out, norm_ffn_g, peer_w_query, peer_keys1, peer_keys2, peer_u, peer_v, norm_final_g):
    b_, s_, d = x.shape
    depth = w_in.shape[0]
    lb_all = jnp.cumsum(jax.nn.softmax(hg_lb_logits.astype(_f32), axis=0), axis=0)
    n_groups = BATCH_GROUPS if b_ % BATCH_GROUPS == 0 else 1
    bg = b_ // n_groups
    tg = bg * s_
    n_sc = SC_SHARE if SC_SHARE < tg else 0
    n_tc = tg - n_sc
    for l in range(depth):
        wq = peer_w_query[l].astype(_bf16).reshape(d, PEER_HEADS, 2, PEER_HALF).transpose(1, 2, 0, 3)
        v_tab = _pack_table(peer_v[l])
        u_sc = _pack_halves(peer_u[l])
        v_sc = _pack_halves(peer_v[l])
        last = l == depth - 1
        g = norm_final_g[None] if last else jnp.ones((1, d), _f32)
        def front(c, x_in):
            xc = _mix(x_in, norm_mix_g[l][None], w_in[l].astype(_bf16), lb_all[l][None],
                      hg_out_norm_g[l][None], conv_w[l], w_branch_hg[l].astype(_bf16),
                      w_branch_conv[l].astype(_bf16), w_out[l].astype(_bf16), ts=MIX_TILE)
            xf = xc.reshape(tg, d)
            h, idx, gate = _route(xf, norm_ffn_g[l][None], wq,
                                  peer_keys1[l].astype(_bf16), peer_keys2[l].astype(_bf16), tr=ROUTE_TILE)
            return xf, idx, gate, _sc_udot(idx, u_sc, h)

        def back(xf, idx, gate, acts):
            w = _gate_weights(acts, gate, tw=FINAL_TILE)
            peer_tc = _vaxpy(idx * PACK_ROWS, w, v_tab.reshape(-1, 128), tb=EXPERT_TILE, count=n_tc).reshape(n_tc, d)
            peer_sc = _sc_vaxpy(idx, w, v_sc, first=n_tc, count=n_sc) if n_sc else None
            return peer_tc, peer_sc

        groups = [x[c * bg:(c + 1) * bg] for c in range(n_groups)]
        fronts = {c: front(c, groups[c]) for c in range(min(GROUPS_AHEAD + 1, n_groups))}
        done, peer_tc = [], None
        for c in range(n_groups):
            xf, idx, gate, acts = fronts.pop(c)
            after = [fronts[c + GROUPS_AHEAD][1]] if c + GROUPS_AHEAD in fronts else []
            after += [peer_tc] if peer_tc is not None else []
            if after:
                acts = lax.optimization_barrier((acts, *after))[0]
            peer_tc, peer_sc = back(xf, idx, gate, acts)
            done.append((xf, peer_tc, peer_sc))
            nxt = c + GROUPS_AHEAD + 1
            if nxt < n_groups:
                fronts[nxt] = front(nxt, lax.optimization_barrier((groups[nxt], peer_tc))[0])
        outs = []
        for xf, p_tc, p_sc in done:
            outs.append(_final(xf, p_tc, g, normalize=last, tf=FINAL_TILE, first=0))
            if p_sc is not None:
                p_sc = lax.optimization_barrier((p_sc, peer_tc))[0]
                outs.append(_final(xf, p_sc, g, normalize=last, tf=FINAL_TILE, first=n_tc))
        x = jnp.concatenate(outs, axis=0).reshape(b_, s_, d)
    return x
```

```python
import functools

import jax
import jax.numpy as jnp
from jax import lax
from jax.experimental import pallas as pl
from jax.experimental.pallas import tpu as pltpu
from jax.experimental.pallas import tpu_sc as plsc

EPS = 1e-6
CHUNK = 64
SUB = 16
HEADS = 8
HEAD_DIM = 64
HG_WIDTH = HEADS * HEAD_DIM
GROUP = 256
N_GROUPS = HG_WIDTH // GROUP
CONV_K = 3
PEER_HEADS = 8
PEER_NKEYS = 128
PEER_HALF = 128
PEER_TOPK = 16
PEER_SLOTS = PEER_HEADS * PEER_TOPK

VMEM_LIMIT_BYTES = 56 * 1024 * 1024

MIX_TILE = 256
ROUTE_TILE = 256
EXPERT_TILE = 128
FINAL_TILE = 512
BATCH_GROUPS = 4
SC_SHARE = 3072
GROUPS_AHEAD = 2

_f32 = jnp.float32
_bf16 = jnp.bfloat16


def _dot(a, b):
    return jnp.dot(a, b, preferred_element_type=_f32)


def _dot_nt(a, b):
    return lax.dot_general(a, b, (((1,), (1,)), ((), ())), preferred_element_type=_f32)


def _dot_tn(a, b):
    return lax.dot_general(a, b, (((0,), (0,)), ((), ())), preferred_element_type=_f32)


def _split3(x):
    hi = x.astype(_bf16)
    r1 = x - hi.astype(_f32)
    mid = r1.astype(_bf16)
    lo = (r1 - mid.astype(_f32)).astype(_bf16)
    return hi, mid, lo


def _split2(x):
    hi = x.astype(_bf16)
    lo = (x - hi.astype(_f32)).astype(_bf16)
    return hi, lo


def _dot_exact_rhs01(x, m01):
    hi, mid, lo = _split3(x)
    return _dot(hi, m01) + _dot(mid, m01) + _dot(lo, m01)


def _dot_exact_lhs01(m01, x):
    hi, mid, lo = _split3(x)
    return _dot(m01, hi) + _dot(m01, mid) + _dot(m01, lo)


def _iota(shape, dim):
    return lax.broadcasted_iota(jnp.int32, shape, dim)


def _hgrn2_chunk(q, k, lf, v, state_ref):
    n_sub = CHUNK // SUB
    row = _iota((CHUNK, CHUNK), 0)
    col = _iota((CHUNK, CHUNK), 1)
    tril = (col <= row).astype(_bf16)
    b = _dot_exact_lhs01(tril, lf)

    b_end = [b[(j + 1) * SUB - 1:(j + 1) * SUB, :] for j in range(n_sub)]
    b_end_rows = jnp.concatenate([jnp.broadcast_to(e, (SUB, HG_WIDTH)) for e in b_end], axis=0)
    b_last = b_end[-1]

    q_in = (q * jnp.exp(b)).astype(_bf16)
    k_sub = (k * jnp.exp(b_end_rows - b)).astype(_bf16)
    k_out = (k * jnp.exp(b_last - b)).astype(_bf16)
    q_from = [(q * jnp.exp(jnp.minimum(b - b_end[j], 0.0))).astype(_bf16) for j in range(n_sub - 1)]
    v_b = v.astype(_bf16)

    gr = _iota((GROUP, GROUP), 0) // HEAD_DIM
    gc = _iota((GROUP, GROUP), 1) // HEAD_DIM
    head_mask = gr == gc
    t_blk = _iota((CHUNK, GROUP), 0) // SUB
    s_blk = (_iota((CHUNK, GROUP), 1) % HEAD_DIM) // SUB

    outs = []
    for g in range(N_GROUPS):
        sl = slice(g * GROUP, (g + 1) * GROUP)
        st = state_ref[g]
        o_g = _dot_nt(q_in[:, sl], st.astype(_bf16))

        zero_b = jnp.zeros((), _bf16)
        k_bd = jnp.where(head_mask, jnp.concatenate([k_sub[:, sl]] * (GROUP // CHUNK), axis=0), zero_b)
        v_bd = jnp.where(head_mask, jnp.concatenate([v_b[:, sl]] * (GROUP // CHUNK), axis=0), zero_b)
        q_stack = jnp.concatenate([qf[:, sl] for qf in q_from], axis=0)
        r = _dot_nt(q_stack, k_bd)
        scores = jnp.zeros((CHUNK, GROUP), _f32)
        for j in range(n_sub - 1):
            sel = (s_blk == j) & (t_blk > j)
            scores = jnp.where(sel, r[j * CHUNK:(j + 1) * CHUNK, :], scores)
        o_g = o_g + _dot(scores.astype(_bf16), v_bd)
        outs.append(o_g)

        upd = _dot_tn(v_b[:, sl], k_out[:, sl])
        decay = jnp.exp(b_last[:, sl])
        state_ref[g] = st * decay + jnp.where(head_mask, upd, 0.0)
    o = jnp.concatenate(outs, axis=1)

    ones_bd = ((_iota((HG_WIDTH, HG_WIDTH), 0) // HEAD_DIM)
               == (_iota((HG_WIDTH, HG_WIDTH), 1) // HEAD_DIM)).astype(_bf16)
    t_in_sub = _iota((CHUNK, HG_WIDTH), 0) % SUB
    for lag in range(SUB):
        if lag == 0:
            p = q * k
            v_l = v
        else:
            valid = t_in_sub >= lag
            k_l = pltpu.roll(k, lag, 0)
            b_l = pltpu.roll(b, lag, 0)
            v_l = pltpu.roll(v, lag, 0)
            p = jnp.where(valid, q * k_l * jnp.exp(jnp.minimum(b - b_l, 0.0)), 0.0)
        s_l = _dot(p.astype(_bf16), ones_bd)
        o = o + s_l * v_l
    return o


def _mix_kernel(x_ref, g_ref, win_ref, lb_ref, hgn_ref, convw_ref, pa_ref, pb_ref, wo_ref,
                out_ref, state_ref, carry_ref, q_s, k_s, lf_s, v_s, o_s):
    ts = x_ref.shape[0]
    d_model = x_ref.shape[1]
    w = HG_WIDTH

    @pl.when(pl.program_id(1) == 0)
    def _():
        state_ref[...] = jnp.zeros_like(state_ref)
        carry_ref[...] = jnp.zeros_like(carry_ref)

    x = x_ref[...]
    h = x * lax.rsqrt(jnp.mean(x * x, axis=-1, keepdims=True) + EPS) * g_ref[...]
    hb = h.astype(_bf16)

    def proj(i, width=w):
        return _dot(hb, win_ref[:, i * w:i * w + width])

    lb = lb_ref[...]
    q_s[...] = jax.nn.silu(proj(0)) * (HEAD_DIM ** -0.5)
    forget = lb + (1.0 - lb) * jax.nn.sigmoid(proj(1))
    k_s[...] = 1.0 - forget
    lf_s[...] = jnp.log(forget)
    v_s[...] = proj(2)

    def chunk_body(c, carry):
        rows = pl.ds(pl.multiple_of(c * CHUNK, CHUNK), CHUNK)
        o_s[rows, :] = _hgrn2_chunk(q_s[rows, :], k_s[rows, :], lf_s[rows, :], v_s[rows, :], state_ref)
        return carry

    lax.fori_loop(0, ts // CHUNK, chunk_body, 0)

    o = o_s[...]
    ones_bd = ((_iota((w, w), 0) // HEAD_DIM) == (_iota((w, w), 1) // HEAD_DIM)).astype(_bf16)
    ms = _dot_exact_rhs01(o * o, ones_bd) * (1.0 / HEAD_DIM)
    o = o * lax.rsqrt(ms + EPS) * hgn_ref[...]
    y_a = (o * jax.nn.silu(proj(3))).astype(_bf16)

    u = proj(5) * proj(6)
    prev = carry_ref[...]
    rowi = _iota((ts, w), 0)
    u1 = jnp.where(rowi >= 1, pltpu.roll(u, 1, 0), jnp.broadcast_to(prev[7:8, :], (ts, w)))
    u2 = jnp.where(rowi >= 2, pltpu.roll(u, 2, 0),
                   jnp.where(rowi == 1, jnp.broadcast_to(prev[7:8, :], (ts, w)),
                             jnp.broadcast_to(prev[6:7, :], (ts, w))))
    carry_ref[...] = u[ts - 8:, :]
    cw = convw_ref[...]
    y_b = (proj(4) * (cw[0:1, :] * u2 + cw[1:2, :] * u1 + cw[2:3, :] * u)).astype(_bf16)

    g_a = jax.nn.sigmoid(proj(7, d_model))
    g_b = jax.nn.sigmoid(_dot(hb, win_ref[:, 7 * w + d_model:7 * w + 2 * d_model]))
    merged = g_a * _dot(y_a, pa_ref[...]) + g_b * _dot(y_b, pb_ref[...])
    out_ref[...] = x + _dot(merged.astype(_bf16), wo_ref[...])


def _const_spec(shape):
    nd = len(shape)
    return pl.BlockSpec(shape, lambda *_: (0,) * nd, pipeline_mode=pl.Buffered(1))


def _mix(x, norm_g, w_in, lb, hg_norm_g, conv_w, w_a, w_b, w_o, *, ts):
    b_, s_, d = x.shape
    in_cols = w_in.shape[1]
    w = HG_WIDTH
    grid = (b_, s_ // ts)
    return pl.pallas_call(
        _mix_kernel,
        grid=grid,
        in_specs=[
            pl.BlockSpec((None, ts, d), lambda b, s: (b, s, 0)),
            _const_spec((1, d)),
            _const_spec((d, in_cols)),
            _const_spec((1, w)),
            _const_spec((1, w)),
            _const_spec((CONV_K, w)),
            _const_spec((w, d)),
            _const_spec((w, d)),
            _const_spec((d, d)),
        ],
        out_specs=pl.BlockSpec((None, ts, d), lambda b, s: (b, s, 0)),
        out_shape=jax.ShapeDtypeStruct((b_, s_, d), _f32),
        scratch_shapes=[
            pltpu.VMEM((N_GROUPS, GROUP, GROUP), _f32),
            pltpu.VMEM((8, w), _f32),
            pltpu.VMEM((ts, w), _f32),
            pltpu.VMEM((ts, w), _f32),
            pltpu.VMEM((ts, w), _f32),
            pltpu.VMEM((ts, w), _f32),
            pltpu.VMEM((ts, w), _f32),
        ],
        compiler_params=pltpu.CompilerParams(
            dimension_semantics=("parallel", "arbitrary"),
            vmem_limit_bytes=VMEM_LIMIT_BYTES),
        name="mix",
    )(x, norm_g, w_in, lb, hg_norm_g, conv_w, w_a, w_b, w_o)


def _stair_pairs():
    pairs = [(a, c) for a in range(PEER_TOPK) for c in range(PEER_TOPK) if (a + 1) * (c + 1) <= PEER_TOPK]
    rows = -(-len(pairs) // 8) * 8
    ranks = jnp.arange(PEER_TOPK)[None, :]
    a_col = jnp.asarray([a for a, _ in pairs] + [-1] * (rows - len(pairs)))[:, None]
    c_col = jnp.asarray([c for _, c in pairs] + [-1] * (rows - len(pairs)))[:, None]
    pad = jnp.where(a_col < 0, -jnp.inf, 0.0).astype(_f32)
    return (a_col == ranks).astype(_bf16), (c_col == ranks).astype(_bf16), pad


ROUTE_LANES = 128


def _route_kernel(x_ref, g_ref, wq_ref, k1_ref, k2_ref, sa_ref, sc_ref, pad_ref, h_ref, idx_ref, gate_ref,
                  idx_t, e_t, top_ref):
    tr = x_ref.shape[0]
    n_cand = sa_ref.shape[0]
    x = x_ref[...]
    h = x * lax.rsqrt(jnp.mean(x * x, axis=-1, keepdims=True) + EPS) * g_ref[...]
    h_ref[...] = h
    hb = h.astype(_bf16)

    key_row = _iota((PEER_NKEYS, ROUTE_LANES), 0).astype(_f32)
    cand_row = _iota((n_cand, tr), 0).astype(_f32)
    neg_inf = jnp.float32(-jnp.inf)

    def extract_max(s):
        m = jnp.max(s, axis=0, keepdims=True)
        i = jnp.min(jnp.where(s == m, key_row, float(PEER_NKEYS)), axis=0, keepdims=True)
        return m, i, jnp.where(key_row == i, neg_inf, s)

    def head_body(hd, carry):
        q1 = _dot(hb, wq_ref[hd, 0]).astype(_bf16)
        q2 = _dot(hb, wq_ref[hd, 1]).astype(_bf16)
        s1 = _dot_nt(k1_ref[hd], q1)
        s2 = _dot_nt(k2_ref[hd], q2)

        for lt in range(tr // ROUTE_LANES):
            lanes = slice(lt * ROUTE_LANES, (lt + 1) * ROUTE_LANES)

            def half_body(k, c):
                m1, i1, r1 = extract_max(c[0])
                m2, i2, r2 = extract_max(c[1])
                for j, row in enumerate((m1, i1, m2, i2)):
                    top_ref[lt, j, pl.ds(k, 1), :] = row
                return r1, r2

            lax.fori_loop(0, PEER_TOPK, half_body, (s1[:, lanes], s2[:, lanes]))

        def top(j):
            return jnp.concatenate([top_ref[lt, j] for lt in range(tr // ROUTE_LANES)], axis=1)

        sa = sa_ref[...]
        sc = sc_ref[...]
        cand_s = _dot_exact_lhs01(sa, top(0)) + _dot_exact_lhs01(sc, top(2)) + pad_ref[...]
        cand_i = (_dot(sa, top(1).astype(_bf16)) * float(PEER_NKEYS)
                  + _dot(sc, top(3).astype(_bf16)))

        def pick_body(k, c):
            cand_s, denom, m_first = c
            m = jnp.max(cand_s, axis=0, keepdims=True)
            pos = jnp.min(jnp.where(cand_s == m, cand_row, float(n_cand)), axis=0, keepdims=True)
            hit = cand_row == pos
            eid = jnp.max(jnp.where(hit, cand_i, -1.0), axis=0, keepdims=True)
            m_first = jnp.where(k == 0, m, m_first)
            e = jnp.exp(m - m_first)
            slot = hd * PEER_TOPK + k
            idx_t[pl.ds(slot, 1), :] = eid
            e_t[pl.ds(slot, 1), :] = e
            return jnp.where(hit, neg_inf, cand_s), denom + e, m_first

        zero_row = jnp.zeros((1, tr), _f32)
        _, denom, _ = lax.fori_loop(0, PEER_TOPK, pick_body, (cand_s, zero_row, zero_row))
        rows = pl.ds(pl.multiple_of(hd * PEER_TOPK, PEER_TOPK), PEER_TOPK)
        e_t[rows, :] = e_t[rows, :] / denom
        return carry

    lax.fori_loop(0, PEER_HEADS, head_body, 0)
    idx_ref[...] = idx_t[...].T.astype(jnp.int32)
    gate_ref[...] = e_t[...].T


def _route(x, norm_g, w_query, keys1, keys2, *, tr):
    t, d = x.shape
    sel_a, sel_c, pad = _stair_pairs()
    pad = jnp.broadcast_to(pad, (pad.shape[0], tr))
    return pl.pallas_call(
        _route_kernel,
        grid=(t // tr,),
        in_specs=[
            pl.BlockSpec((tr, d), lambda i: (i, 0)),
            _const_spec((1, d)),
            _const_spec(w_query.shape),
            _const_spec(keys1.shape),
            _const_spec(keys2.shape),
            _const_spec(sel_a.shape),
            _const_spec(sel_c.shape),
            _const_spec(pad.shape),
        ],
        out_specs=[
            pl.BlockSpec((tr, d), lambda i: (i, 0)),
            pl.BlockSpec((tr, PEER_SLOTS), lambda i: (i, 0)),
            pl.BlockSpec((tr, PEER_SLOTS), lambda i: (i, 0)),
        ],
        out_shape=[
            jax.ShapeDtypeStruct((t, d), _f32),
            jax.ShapeDtypeStruct((t, PEER_SLOTS), jnp.int32),
            jax.ShapeDtypeStruct((t, PEER_SLOTS), _f32),
        ],
        scratch_shapes=[
            pltpu.VMEM((PEER_SLOTS, tr), _f32),
            pltpu.VMEM((PEER_SLOTS, tr), _f32),
            pltpu.VMEM((tr // ROUTE_LANES, 4, PEER_TOPK, ROUTE_LANES), _f32),
        ],
        compiler_params=pltpu.CompilerParams(
            dimension_semantics=("parallel",),
            vmem_limit_bytes=VMEM_LIMIT_BYTES),
        name="route",
    )(x, norm_g, w_query, keys1, keys2, sel_a, sel_c, pad)


SC_CORES = 2
SC_SUBCORES = 16
SC_LANES = 16
SC_WORKERS = SC_CORES * SC_SUBCORES
SC_TOKENS = 8
SC_GATHER = 32
SC_BLOCK = 8
SC_UNROLL = 2
SC_COLS = 8
HI_MASK = -65536


def _sc_mesh():
    return plsc.VectorSubcoreMesh(core_axis_name="c", subcore_axis_name="s")


def _sc_worker_base(per_worker):
    return (lax.axis_index("s") * SC_CORES + lax.axis_index("c")) * per_worker


def _sc_gather_loop(tab_hbm, idx_v, bufs, consume):
    n_parts = PEER_SLOTS // SC_GATHER
    n_gathers = SC_TOKENS * n_parts

    def gather(g, parity):
        rows, sem = bufs[parity]
        i = g // n_parts
        col = pl.multiple_of((g % n_parts) * SC_GATHER, SC_GATHER)
        return pltpu.make_async_copy(tab_hbm.at[idx_v.at[i, pl.ds(col, SC_GATHER)]], rows, sem)

    gather(0, 0).start()

    @pl.loop(0, n_gathers // 2)
    def _(pair):
        g = 2 * pair
        gather(g + 1, 1).start()
        gather(g, 0).wait()
        consume(g // n_parts, g % n_parts, bufs[0][0])

        @pl.when(g + 2 < n_gathers)
        def _():
            gather(g + 2, 0).start()

        gather(g + 1, 1).wait()
        consume((g + 1) // n_parts, (g + 1) % n_parts, bufs[1][0])


def _pack_halves(tab):
    half = tab.shape[1] // 2
    bits = lax.bitcast_convert_type(tab.astype(_bf16), jnp.uint16).astype(jnp.uint32)
    return lax.bitcast_convert_type(bits[:, :half] | (bits[:, half:] << 16), jnp.int32)


def _sc_unpack(words):
    lo = lax.bitcast_convert_type(words << 16, _f32)
    hi = lax.bitcast_convert_type(words & HI_MASK, _f32)
    return lo, hi


def _sc_udot(idx, tab, h):
    t, d = h.shape
    per_worker = t // SC_WORKERS
    half = d // 2

    @functools.partial(
        pl.kernel, mesh=_sc_mesh(),
        out_type=jax.ShapeDtypeStruct((t, PEER_SLOTS), _f32),
        scratch_types=[
            pltpu.VMEM((SC_TOKENS, PEER_SLOTS), jnp.int32),
            pltpu.VMEM((SC_TOKENS, d), _f32),
            pltpu.VMEM((SC_TOKENS, PEER_SLOTS), _f32),
            pltpu.VMEM((SC_GATHER, half), jnp.int32),
            pltpu.VMEM((SC_GATHER, half), jnp.int32),
            pltpu.SemaphoreType.DMA,
            pltpu.SemaphoreType.DMA,
        ],
        compiler_params=pltpu.CompilerParams(needs_layout_passes=False),
        name="sc_udot",
    )
    def body(idx_hbm, tab_hbm, h_hbm, out_hbm, idx_v, h_v, acts_v, rows0, rows1, sem0, sem1):
        base = _sc_worker_base(per_worker)
        lane = lax.iota(jnp.int32, SC_LANES)

        def dots(i, part, rows):
            def block_body(blk, carry):
                row0 = blk * SC_LANES
                outv = jnp.zeros((SC_LANES,), _f32)
                for sub in range(SC_LANES // SC_BLOCK):
                    def chunk_body(jj, accs):
                        accs = list(accs)
                        for u in range(SC_UNROLL):
                            off = pl.multiple_of((jj * SC_UNROLL + u) * SC_LANES, SC_LANES)
                            h_lo = h_v[i, pl.ds(off, SC_LANES)]
                            h_hi = h_v[i, pl.ds(half + off, SC_LANES)]
                            for e in range(SC_BLOCK):
                                lo, hi = _sc_unpack(rows[row0 + sub * SC_BLOCK + e, pl.ds(off, SC_LANES)])
                                accs[e] = accs[e] + (lo * h_lo + hi * h_hi)
                        return tuple(accs)

                    accs = lax.fori_loop(0, half // SC_LANES // SC_UNROLL, chunk_body,
                                         tuple(jnp.zeros((SC_LANES,), _f32) for _ in range(SC_BLOCK)))
                    for e in range(SC_BLOCK):
                        outv = jnp.where(lane == sub * SC_BLOCK + e, jnp.sum(accs[e]), outv)
                col = pl.multiple_of(part * SC_GATHER + row0, SC_LANES)
                acts_v[i, pl.ds(col, SC_LANES)] = outv
                return carry

            lax.fori_loop(0, SC_GATHER // SC_LANES, block_body, 0)

        @pl.loop(0, per_worker // SC_TOKENS)
        def _(step):
            tok = pl.multiple_of(base + step * SC_TOKENS, SC_TOKENS)
            pltpu.sync_copy(idx_hbm.at[pl.ds(tok, SC_TOKENS)], idx_v)
            pltpu.sync_copy(h_hbm.at[pl.ds(tok, SC_TOKENS)], h_v)
            _sc_gather_loop(tab_hbm, idx_v, ((rows0, sem0), (rows1, sem1)), dots)
            pltpu.sync_copy(acts_v, out_hbm.at[pl.ds(tok, SC_TOKENS)])

    return body(idx, tab, h)


def _sc_vaxpy(idx, w, tab, *, first, count):
    half = tab.shape[1]
    d = 2 * half
    per_worker = count // SC_WORKERS
    span = SC_COLS * SC_LANES

    @functools.partial(
        pl.kernel, mesh=_sc_mesh(),
        out_type=jax.ShapeDtypeStruct((count, d), _f32),
        scratch_types=[
            pltpu.VMEM((SC_TOKENS, PEER_SLOTS), jnp.int32),
            pltpu.VMEM((SC_TOKENS, PEER_SLOTS), _f32),
            pltpu.VMEM((SC_TOKENS, d), _f32),
            pltpu.VMEM((SC_GATHER, half), jnp.int32),
            pltpu.VMEM((SC_GATHER, half), jnp.int32),
            pltpu.SemaphoreType.DMA,
            pltpu.SemaphoreType.DMA,
        ],
        compiler_params=pltpu.CompilerParams(needs_layout_passes=False),
        name="sc_vaxpy",
    )
    def body(idx_hbm, w_hbm, tab_hbm, out_hbm, idx_v, w_v, out_v, rows0, rows1, sem0, sem1):
        base = _sc_worker_base(per_worker)

        def accumulate(i, part, rows):
            i_vec = jnp.full((SC_LANES,), i, jnp.int32)

            def span_body(cq, carry):
                def cols(c, offset=0):
                    return pl.ds(pl.multiple_of(offset + cq * span + c * SC_LANES, SC_LANES), SC_LANES)

                def expert_body(e, accs):
                    k_vec = jnp.full((SC_LANES,), part * SC_GATHER + e, jnp.int32)
                    wv = plsc.load_gather(w_v, [i_vec, k_vec])
                    new = []
                    for c in range(SC_COLS):
                        lo, hi = _sc_unpack(rows[e, cols(c)])
                        new += [accs[2 * c] + lo * wv, accs[2 * c + 1] + hi * wv]
                    return tuple(new)

                init = []
                for c in range(SC_COLS):
                    init += [out_v[i, cols(c)], out_v[i, cols(c, half)]]
                accs = lax.fori_loop(0, SC_GATHER, expert_body, tuple(init))
                for c in range(SC_COLS):
                    out_v[i, cols(c)] = accs[2 * c]
                    out_v[i, cols(c, half)] = accs[2 * c + 1]
                return carry

            lax.fori_loop(0, half // span, span_body, 0)

        @pl.loop(0, per_worker // SC_TOKENS)
        def _(step):
            off = pl.multiple_of(base + step * SC_TOKENS, SC_TOKENS)
            pltpu.sync_copy(idx_hbm.at[pl.ds(first + off, SC_TOKENS)], idx_v)
            pltpu.sync_copy(w_hbm.at[pl.ds(first + off, SC_TOKENS)], w_v)

            @pl.loop(0, SC_TOKENS)
            def _(i):
                @pl.loop(0, d // SC_LANES)
                def _(j):
                    out_v[i, pl.ds(pl.multiple_of(j * SC_LANES, SC_LANES), SC_LANES)] = (
                        jnp.zeros((SC_LANES,), _f32))

            _sc_gather_loop(tab_hbm, idx_v, ((rows0, sem0), (rows1, sem1)), accumulate)
            pltpu.sync_copy(out_v, out_hbm.at[pl.ds(off, SC_TOKENS)])

    return body(idx, w, tab)


ROW_CHUNKS = 8
PACK_ROWS = ROW_CHUNKS // 2


def _pack_table(tab):
    n, d = tab.shape
    bits = lax.bitcast_convert_type(tab.astype(_bf16), jnp.uint16).astype(jnp.uint32)
    bits = bits.reshape(n, PACK_ROWS, 2, d // ROW_CHUNKS)
    word = bits[:, :, 0, :] | (bits[:, :, 1, :] << 16)
    return lax.bitcast_convert_type(word, jnp.int32)


def _gate_weights_kernel(acts_ref, gate_ref, w_ref):
    a = acts_ref[...]
    gelu = 0.5 * a * (1.0 + lax.erf(a * (2.0 ** -0.5)))
    w_ref[...] = gate_ref[...] * gelu


def _gate_weights(acts, gate, *, tw):
    t, n = acts.shape
    spec = pl.BlockSpec((tw, n), lambda i: (i, 0))
    return pl.pallas_call(
        _gate_weights_kernel,
        grid=(t // tw,),
        in_specs=[spec, spec],
        out_specs=spec,
        out_shape=jax.ShapeDtypeStruct((t, n), _f32),
        compiler_params=pltpu.CompilerParams(dimension_semantics=("parallel",)),
        name="gate_weights",
    )(acts, gate)


def _gather_rows(idx_ref, t, tab_ref, rows_ref):
    for k in range(PEER_SLOTS):
        row = pl.multiple_of(idx_ref[t, k], PACK_ROWS)
        rows_ref[k * PACK_ROWS:(k + 1) * PACK_ROWS, :] = tab_ref[pl.ds(row, PACK_ROWS), :]


def _rows_matrix(rows_ref):
    return pltpu.bitcast(rows_ref[...], _bf16)


def _token_loop(tb, idx_ref, tab_ref, rows_a, rows_b, compute):
    _gather_rows(idx_ref, 0, tab_ref, rows_a)

    def pair_body(i, carry):
        t0 = 2 * i
        _gather_rows(idx_ref, t0 + 1, tab_ref, rows_b)
        compute(t0, _rows_matrix(rows_a))
        _gather_rows(idx_ref, jnp.minimum(t0 + 2, tb - 1), tab_ref, rows_a)
        compute(t0 + 1, _rows_matrix(rows_b))
        return carry

    lax.fori_loop(0, tb // 2, pair_body, 0)


def _chunk_diag_mask():
    shape = (ROW_CHUNKS, PEER_SLOTS * ROW_CHUNKS)
    return (_iota(shape, 1) % ROW_CHUNKS) == _iota(shape, 0)


def _vaxpy_kernel(idx_ref, w_ref, tab_ref, out_ref, rows_a, rows_b, wrep_ref):
    tb = out_ref.shape[0]
    diag = _chunk_diag_mask()
    shape = (PEER_SLOTS, PEER_SLOTS * ROW_CHUNKS)
    spread = (_iota(shape, 0) == (_iota(shape, 1) // ROW_CHUNKS)).astype(_bf16)
    wrep_ref[...] = _dot_exact_rhs01(w_ref[...], spread)

    def compute(t, m):
        w_row = jnp.broadcast_to(wrep_ref[pl.ds(t, 1), :], diag.shape)
        w_hi, w_lo = _split2(jnp.where(diag, w_row, 0.0))
        out_ref[t] = _dot(w_hi, m) + _dot(w_lo, m)

    _token_loop(tb, idx_ref, tab_ref, rows_a, rows_b, compute)


def _vaxpy(idx, w, tab, *, tb, count):
    return pl.pallas_call(
        _vaxpy_kernel,
        grid=(count // tb,),
        in_specs=[
            pl.BlockSpec((tb, PEER_SLOTS), lambda i: (i, 0), memory_space=pltpu.SMEM),
            pl.BlockSpec((tb, PEER_SLOTS), lambda i: (i, 0)),
            _const_spec(tab.shape),
        ],
        out_specs=pl.BlockSpec((tb, ROW_CHUNKS, 128), lambda i: (i, 0, 0)),
        out_shape=jax.ShapeDtypeStruct((count, ROW_CHUNKS, 128), _f32),
        scratch_shapes=[
            pltpu.VMEM((PEER_SLOTS * PACK_ROWS, 128), jnp.int32),
            pltpu.VMEM((PEER_SLOTS * PACK_ROWS, 128), jnp.int32),
            pltpu.VMEM((tb, PEER_SLOTS * ROW_CHUNKS), _f32),
        ],
        compiler_params=pltpu.CompilerParams(
            dimension_semantics=("parallel",),
            vmem_limit_bytes=VMEM_LIMIT_BYTES),
        name="vaxpy",
    )(idx, w, tab)


def _final_kernel(x_ref, p_ref, g_ref, out_ref, *, normalize):
    x = x_ref[...] + p_ref[...]
    if normalize:
        x = x * lax.rsqrt(jnp.mean(x * x, axis=-1, keepdims=True) + EPS) * g_ref[...]
    out_ref[...] = x


def _final(x, peer, g, *, normalize, tf, first):
    t, d = peer.shape
    off = first // tf
    return pl.pallas_call(
        functools.partial(_final_kernel, normalize=normalize),
        grid=(t // tf,),
        in_specs=[
            pl.BlockSpec((tf, d), lambda i: (i + off, 0)),
            pl.BlockSpec((tf, d), lambda i: (i, 0)),
            _const_spec((1, d)),
        ],
        out_specs=pl.BlockSpec((tf, d), lambda i: (i, 0)),
        out_shape=jax.ShapeDtypeStruct((t, d), _f32),
        compiler_params=pltpu.CompilerParams(dimension_semantics=("parallel",)),
        name="final_norm",
    )(x, peer, g)


def kernel(x, norm_mix_g, w_in, hg_lb_logits, hg_out_norm_g, conv_w, w_branch_hg, w_branch_conv, w_out, norm_ffn_g, peer_w_query, peer_keys1, peer_keys2, peer_u, peer_v, norm_final_g):
    b_, s_, d = x.shape
    depth = w_in.shape[0]
    lb_all = jnp.cumsum(jax.nn.softmax(hg_lb_logits.astype(_f32), axis=0), axis=0)
    n_groups = BATCH_GROUPS if b_ % BATCH_GROUPS == 0 else 1
    bg = b_ // n_groups
    tg = bg * s_
    n_sc = SC_SHARE if SC_SHARE < tg else 0
    n_tc = tg - n_sc
    for l in range(depth):
        wq = peer_w_query[l].astype(_bf16).reshape(d, PEER_HEADS, 2, PEER_HALF).transpose(1, 2, 0, 3)
        v_tab = _pack_table(peer_v[l])
        u_sc = _pack_halves(peer_u[l])
        v_sc = _pack_halves(peer_v[l])
        last = l == depth - 1
        g = norm_final_g[None] if last else jnp.ones((1, d), _f32)
        def front(c, x_in, sc_before=None):
            xc = _mix(x_in, norm_mix_g[l][None], w_in[l].astype(_bf16), lb_all[l][None],
                      hg_out_norm_g[l][None], conv_w[l], w_branch_hg[l].astype(_bf16),
                      w_branch_conv[l].astype(_bf16), w_out[l].astype(_bf16), ts=MIX_TILE)
            xf = xc.reshape(tg, d)
            h, idx, gate = _route(xf, norm_ffn_g[l][None], wq,
                                  peer_keys1[l].astype(_bf16), peer_keys2[l].astype(_bf16), tr=ROUTE_TILE)
            idx_sc = idx if sc_before is None else lax.optimization_barrier((idx, sc_before))[0]
            return xf, idx, gate, _sc_udot(idx_sc, u_sc, h)

        def back(xf, idx, gate, acts):
            w = _gate_weights(acts, gate, tw=FINAL_TILE)
            peer_tc = _vaxpy(idx * PACK_ROWS, w, v_tab.reshape(-1, 128), tb=EXPERT_TILE, count=n_tc).reshape(n_tc, d)
            peer_sc = _sc_vaxpy(idx, w, v_sc, first=n_tc, count=n_sc) if n_sc else None
            return peer_tc, peer_sc

        groups = [x[c * bg:(c + 1) * bg] for c in range(n_groups)]
        fronts = {c: front(c, groups[c]) for c in range(min(GROUPS_AHEAD + 1, n_groups))}
        done, peer_tc = [], None
        for c in range(n_groups):
            xf, idx, gate, acts = fronts.pop(c)
            after = [fronts[c + GROUPS_AHEAD][1]] if c + GROUPS_AHEAD in fronts else []
            after += [peer_tc] if peer_tc is not None else []
            if after:
                acts = lax.optimization_barrier((acts, *after))[0]
            peer_tc, peer_sc = back(xf, idx, gate, acts)
            done.append((xf, peer_tc, peer_sc))
            nxt = c + GROUPS_AHEAD + 1
            if nxt < n_groups:
                fronts[nxt] = front(nxt, lax.optimization_barrier((groups[nxt], peer_tc))[0], peer_sc)
        outs = []
        for xf, p_tc, p_sc in done:
            outs.append(_final(xf, p_tc, g, normalize=last, tf=FINAL_TILE, first=0))
            if p_sc is not None:
                p_sc = lax.optimization_barrier((p_sc, peer_tc))[0]
                outs.append(_final(xf, p_sc, g, normalize=last, tf=FINAL_TILE, first=n_tc))
        x = jnp.concatenate(outs, axis=0).reshape(b_, s_, d)
    return x
```

```python
import functools

import jax
import jax.numpy as jnp
from jax import lax
from jax.experimental import pallas as pl
from jax.experimental.pallas import tpu as pltpu
from jax.experimental.pallas import tpu_sc as plsc

EPS = 1e-6
CHUNK = 64
SUB = 16
HEADS = 8
HEAD_DIM = 64
HG_WIDTH = HEADS * HEAD_DIM
GROUP = 256
N_GROUPS = HG_WIDTH // GROUP
CONV_K = 3
PEER_HEADS = 8
PEER_NKEYS = 128
PEER_HALF = 128
PEER_TOPK = 16
PEER_SLOTS = PEER_HEADS * PEER_TOPK

VMEM_LIMIT_BYTES = 56 * 1024 * 1024

MIX_TILE = 256
ROUTE_TILE = 256
EXPERT_TILE = 128
FINAL_TILE = 512
BATCH_GROUPS = 4
SC_SHARE = 3072
GROUPS_AHEAD = 2

_f32 = jnp.float32
_bf16 = jnp.bfloat16


def _dot(a, b):
    return jnp.dot(a, b, preferred_element_type=_f32)


def _dot_nt(a, b):
    return lax.dot_general(a, b, (((1,), (1,)), ((), ())), preferred_element_type=_f32)


def _dot_tn(a, b):
    return lax.dot_general(a, b, (((0,), (0,)), ((), ())), preferred_element_type=_f32)


def _split3(x):
    hi = x.astype(_bf16)
    r1 = x - hi.astype(_f32)
    mid = r1.astype(_bf16)
    lo = (r1 - mid.astype(_f32)).astype(_bf16)
    return hi, mid, lo


def _split2(x):
    hi = x.astype(_bf16)
    lo = (x - hi.astype(_f32)).astype(_bf16)
    return hi, lo


def _dot_exact_rhs01(x, m01):
    hi, mid, lo = _split3(x)
    return _dot(hi, m01) + _dot(mid, m01) + _dot(lo, m01)


def _dot_exact_lhs01(m01, x):
    hi, mid, lo = _split3(x)
    return _dot(m01, hi) + _dot(m01, mid) + _dot(m01, lo)


def _iota(shape, dim):
    return lax.broadcasted_iota(jnp.int32, shape, dim)


def _hgrn2_chunk(q, k, lf, v, state_ref):
    n_sub = CHUNK // SUB
    row = _iota((CHUNK, CHUNK), 0)
    col = _iota((CHUNK, CHUNK), 1)
    tril = (col <= row).astype(_bf16)
    b = _dot_exact_lhs01(tril, lf)

    b_end = [b[(j + 1) * SUB - 1:(j + 1) * SUB, :] for j in range(n_sub)]
    b_end_rows = jnp.concatenate([jnp.broadcast_to(e, (SUB, HG_WIDTH)) for e in b_end], axis=0)
    b_last = b_end[-1]

    q_in = (q * jnp.exp(b)).astype(_bf16)
    k_sub = (k * jnp.exp(b_end_rows - b)).astype(_bf16)
    k_out = (k * jnp.exp(b_last - b)).astype(_bf16)
    q_from = [(q * jnp.exp(jnp.minimum(b - b_end[j], 0.0))).astype(_bf16) for j in range(n_sub - 1)]
    v_b = v.astype(_bf16)

    gr = _iota((GROUP, GROUP), 0) // HEAD_DIM
    gc = _iota((GROUP, GROUP), 1) // HEAD_DIM
    head_mask = gr == gc
    t_blk = _iota((CHUNK, GROUP), 0) // SUB
    s_blk = (_iota((CHUNK, GROUP), 1) % HEAD_DIM) // SUB

    outs = []
    for g in range(N_GROUPS):
        sl = slice(g * GROUP, (g + 1) * GROUP)
        st = state_ref[g]
        o_g = _dot_nt(q_in[:, sl], st.astype(_bf16))

        zero_b = jnp.zeros((), _bf16)
        k_bd = jnp.where(head_mask, jnp.concatenate([k_sub[:, sl]] * (GROUP // CHUNK), axis=0), zero_b)
        v_bd = jnp.where(head_mask, jnp.concatenate([v_b[:, sl]] * (GROUP // CHUNK), axis=0), zero_b)
        q_stack = jnp.concatenate([qf[:, sl] for qf in q_from], axis=0)
        r = _dot_nt(q_stack, k_bd)
        scores = jnp.zeros((CHUNK, GROUP), _f32)
        for j in range(n_sub - 1):
            sel = (s_blk == j) & (t_blk > j)
            scores = jnp.where(sel, r[j * CHUNK:(j + 1) * CHUNK, :], scores)
        o_g = o_g + _dot(scores.astype(_bf16), v_bd)
        outs.append(o_g)

        upd = _dot_tn(v_b[:, sl], k_out[:, sl])
        decay = jnp.exp(b_last[:, sl])
        state_ref[g] = st * decay + jnp.where(head_mask, upd, 0.0)
    o = jnp.concatenate(outs, axis=1)

    ones_bd = ((_iota((HG_WIDTH, HG_WIDTH), 0) // HEAD_DIM)
               == (_iota((HG_WIDTH, HG_WIDTH), 1) // HEAD_DIM)).astype(_bf16)
    t_in_sub = _iota((CHUNK, HG_WIDTH), 0) % SUB
    for lag in range(SUB):
        if lag == 0:
            p = q * k
            v_l = v
        else:
            valid = t_in_sub >= lag
            k_l = pltpu.roll(k, lag, 0)
            b_l = pltpu.roll(b, lag, 0)
            v_l = pltpu.roll(v, lag, 0)
            p = jnp.where(valid, q * k_l * jnp.exp(jnp.minimum(b - b_l, 0.0)), 0.0)
        s_l = _dot(p.astype(_bf16), ones_bd)
        o = o + s_l * v_l
    return o


def _mix_kernel(x_ref, g_ref, win_ref, lb_ref, hgn_ref, convw_ref, pa_ref, pb_ref, wo_ref,
                out_ref, state_ref, carry_ref, q_s, k_s, lf_s, v_s, o_s):
    ts = x_ref.shape[0]
    d_model = x_ref.shape[1]
    w = HG_WIDTH

    @pl.when(pl.program_id(1) == 0)
    def _():
        state_ref[...] = jnp.zeros_like(state_ref)
        carry_ref[...] = jnp.zeros_like(carry_ref)

    x = x_ref[...]
    h = x * lax.rsqrt(jnp.mean(x * x, axis=-1, keepdims=True) + EPS) * g_ref[...]
    hb = h.astype(_bf16)

    def proj(i, width=w):
        return _dot(hb, win_ref[:, i * w:i * w + width])

    lb = lb_ref[...]
    q_s[...] = jax.nn.silu(proj(0)) * (HEAD_DIM ** -0.5)
    forget = lb + (1.0 - lb) * jax.nn.sigmoid(proj(1))
    k_s[...] = 1.0 - forget
    lf_s[...] = jnp.log(forget)
    v_s[...] = proj(2)

    def chunk_body(c, carry):
        rows = pl.ds(pl.multiple_of(c * CHUNK, CHUNK), CHUNK)
        o_s[rows, :] = _hgrn2_chunk(q_s[rows, :], k_s[rows, :], lf_s[rows, :], v_s[rows, :], state_ref)
        return carry

    lax.fori_loop(0, ts // CHUNK, chunk_body, 0)

    o = o_s[...]
    ones_bd = ((_iota((w, w), 0) // HEAD_DIM) == (_iota((w, w), 1) // HEAD_DIM)).astype(_bf16)
    ms = _dot_exact_rhs01(o * o, ones_bd) * (1.0 / HEAD_DIM)
    o = o * lax.rsqrt(ms + EPS) * hgn_ref[...]
    y_a = (o * jax.nn.silu(proj(3))).astype(_bf16)

    u = proj(5) * proj(6)
    prev = carry_ref[...]
    rowi = _iota((ts, w), 0)
    u1 = jnp.where(rowi >= 1, pltpu.roll(u, 1, 0), jnp.broadcast_to(prev[7:8, :], (ts, w)))
    u2 = jnp.where(rowi >= 2, pltpu.roll(u, 2, 0),
                   jnp.where(rowi == 1, jnp.broadcast_to(prev[7:8, :], (ts, w)),
                             jnp.broadcast_to(prev[6:7, :], (ts, w))))
    carry_ref[...] = u[ts - 8:, :]
    cw = convw_ref[...]
    y_b = (proj(4) * (cw[0:1, :] * u2 + cw[1:2, :] * u1 + cw[2:3, :] * u)).astype(_bf16)

    g_a = jax.nn.sigmoid(proj(7, d_model))
    g_b = jax.nn.sigmoid(_dot(hb, win_ref[:, 7 * w + d_model:7 * w + 2 * d_model]))
    merged = g_a * _dot(y_a, pa_ref[...]) + g_b * _dot(y_b, pb_ref[...])
    out_ref[...] = x + _dot(merged.astype(_bf16), wo_ref[...])


def _const_spec(shape):
    nd = len(shape)
    return pl.BlockSpec(shape, lambda *_: (0,) * nd, pipeline_mode=pl.Buffered(1))


def _mix(x, norm_g, w_in, lb, hg_norm_g, conv_w, w_a, w_b, w_o, *, ts):
    b_, s_, d = x.shape
    in_cols = w_in.shape[1]
    w = HG_WIDTH
    grid = (b_, s_ // ts)
    return pl.pallas_call(
        _mix_kernel,
        grid=grid,
        in_specs=[
            pl.BlockSpec((None, ts, d), lambda b, s: (b, s, 0)),
            _const_spec((1, d)),
            _const_spec((d, in_cols)),
            _const_spec((1, w)),
            _const_spec((1, w)),
            _const_spec((CONV_K, w)),
            _const_spec((w, d)),
            _const_spec((w, d)),
            _const_spec((d, d)),
        ],
        out_specs=pl.BlockSpec((None, ts, d), lambda b, s: (b, s, 0)),
        out_shape=jax.ShapeDtypeStruct((b_, s_, d), _f32),
        scratch_shapes=[
            pltpu.VMEM((N_GROUPS, GROUP, GROUP), _f32),
            pltpu.VMEM((8, w), _f32),
            pltpu.VMEM((ts, w), _f32),
            pltpu.VMEM((ts, w), _f32),
            pltpu.VMEM((ts, w), _f32),
            pltpu.VMEM((ts, w), _f32),
            pltpu.VMEM((ts, w), _f32),
        ],
        compiler_params=pltpu.CompilerParams(
            dimension_semantics=("parallel", "arbitrary"),
            vmem_limit_bytes=VMEM_LIMIT_BYTES),
        name="mix",
    )(x, norm_g, w_in, lb, hg_norm_g, conv_w, w_a, w_b, w_o)


def _stair_pairs():
    pairs = [(a, c) for a in range(PEER_TOPK) for c in range(PEER_TOPK) if (a + 1) * (c + 1) <= PEER_TOPK]
    rows = -(-len(pairs) // 8) * 8
    ranks = jnp.arange(PEER_TOPK)[None, :]
    a_col = jnp.asarray([a for a, _ in pairs] + [-1] * (rows - len(pairs)))[:, None]
    c_col = jnp.asarray([c for _, c in pairs] + [-1] * (rows - len(pairs)))[:, None]
    pad = jnp.where(a_col < 0, -jnp.inf, 0.0).astype(_f32)
    return (a_col == ranks).astype(_bf16), (c_col == ranks).astype(_bf16), pad


ROUTE_LANES = 128


def _route_kernel(x_ref, g_ref, wq_ref, k1_ref, k2_ref, sa_ref, sc_ref, pad_ref, h_ref, idx_ref, gate_ref,
                  idx_t, e_t, top_ref):
    tr = x_ref.shape[0]
    n_cand = sa_ref.shape[0]
    x = x_ref[...]
    h = x * lax.rsqrt(jnp.mean(x * x, axis=-1, keepdims=True) + EPS) * g_ref[...]
    h_ref[...] = h
    hb = h.astype(_bf16)

    key_row = _iota((PEER_NKEYS, ROUTE_LANES), 0).astype(_f32)
    cand_row = _iota((n_cand, tr), 0).astype(_f32)
    neg_inf = jnp.float32(-jnp.inf)

    def extract_max(s):
        m = jnp.max(s, axis=0, keepdims=True)
        i = jnp.min(jnp.where(s == m, key_row, float(PEER_NKEYS)), axis=0, keepdims=True)
        return m, i, jnp.where(key_row == i, neg_inf, s)

    def head_body(hd, carry):
        q1 = _dot(hb, wq_ref[hd, 0]).astype(_bf16)
        q2 = _dot(hb, wq_ref[hd, 1]).astype(_bf16)
        s1 = _dot_nt(k1_ref[hd], q1)
        s2 = _dot_nt(k2_ref[hd], q2)

        for lt in range(tr // ROUTE_LANES):
            lanes = slice(lt * ROUTE_LANES, (lt + 1) * ROUTE_LANES)

            def half_body(k, c):
                m1, i1, r1 = extract_max(c[0])
                m2, i2, r2 = extract_max(c[1])
                for j, row in enumerate((m1, i1, m2, i2)):
                    top_ref[lt, j, pl.ds(k, 1), :] = row
                return r1, r2

            lax.fori_loop(0, PEER_TOPK, half_body, (s1[:, lanes], s2[:, lanes]))

        def top(j):
            return jnp.concatenate([top_ref[lt, j] for lt in range(tr // ROUTE_LANES)], axis=1)

        sa = sa_ref[...]
        sc = sc_ref[...]
        cand_s = _dot_exact_lhs01(sa, top(0)) + _dot_exact_lhs01(sc, top(2)) + pad_ref[...]
        cand_i = (_dot(sa, top(1).astype(_bf16)) * float(PEER_NKEYS)
                  + _dot(sc, top(3).astype(_bf16)))

        def pick_body(k, c):
            cand_s, denom, m_first = c
            m = jnp.max(cand_s, axis=0, keepdims=True)
            pos = jnp.min(jnp.where(cand_s == m, cand_row, float(n_cand)), axis=0, keepdims=True)
            hit = cand_row == pos
            eid = jnp.max(jnp.where(hit, cand_i, -1.0), axis=0, keepdims=True)
            m_first = jnp.where(k == 0, m, m_first)
            e = jnp.exp(m - m_first)
            slot = hd * PEER_TOPK + k
            idx_t[pl.ds(slot, 1), :] = eid
            e_t[pl.ds(slot, 1), :] = e
            return jnp.where(hit, neg_inf, cand_s), denom + e, m_first

        zero_row = jnp.zeros((1, tr), _f32)
        _, denom, _ = lax.fori_loop(0, PEER_TOPK, pick_body, (cand_s, zero_row, zero_row))
        rows = pl.ds(pl.multiple_of(hd * PEER_TOPK, PEER_TOPK), PEER_TOPK)
        e_t[rows, :] = e_t[rows, :] / denom
        return carry

    lax.fori_loop(0, PEER_HEADS, head_body, 0)
    idx_ref[...] = idx_t[...].T.astype(jnp.int32)
    gate_ref[...] = e_t[...].T


def _route(x, norm_g, w_query, keys1, keys2, *, tr):
    t, d = x.shape
    sel_a, sel_c, pad = _stair_pairs()
    pad = jnp.broadcast_to(pad, (pad.shape[0], tr))
    return pl.pallas_call(
        _route_kernel,
        grid=(t // tr,),
        in_specs=[
            pl.BlockSpec((tr, d), lambda i: (i, 0)),
            _const_spec((1, d)),
            _const_spec(w_query.shape),
            _const_spec(keys1.shape),
            _const_spec(keys2.shape),
            _const_spec(sel_a.shape),
            _const_spec(sel_c.shape),
            _const_spec(pad.shape),
        ],
        out_specs=[
            pl.BlockSpec((tr, d), lambda i: (i, 0)),
            pl.BlockSpec((tr, PEER_SLOTS), lambda i: (i, 0)),
            pl.BlockSpec((tr, PEER_SLOTS), lambda i: (i, 0)),
        ],
        out_shape=[
            jax.ShapeDtypeStruct((t, d), _f32),
            jax.ShapeDtypeStruct((t, PEER_SLOTS), jnp.int32),
            jax.ShapeDtypeStruct((t, PEER_SLOTS), _f32),
        ],
        scratch_shapes=[
            pltpu.VMEM((PEER_SLOTS, tr), _f32),
            pltpu.VMEM((PEER_SLOTS, tr), _f32),
            pltpu.VMEM((tr // ROUTE_LANES, 4, PEER_TOPK, ROUTE_LANES), _f32),
        ],
        compiler_params=pltpu.CompilerParams(
            dimension_semantics=("parallel",),
            vmem_limit_bytes=VMEM_LIMIT_BYTES),
        name="route",
    )(x, norm_g, w_query, keys1, keys2, sel_a, sel_c, pad)


def _scores_kernel(x_ref, g_ref, wq_ref, k1_ref, k2_ref, h_ref, s_ref):
    x = x_ref[...]
    h = x * lax.rsqrt(jnp.mean(x * x, axis=-1, keepdims=True) + EPS) * g_ref[...]
    h_ref[...] = h
    hb = h.astype(_bf16)
    for hd in range(PEER_HEADS):
        for half, keys_ref in enumerate((k1_ref, k2_ref)):
            q = _dot(hb, wq_ref[hd, half]).astype(_bf16)
            col = (2 * hd + half) * PEER_NKEYS
            s_ref[:, col:col + PEER_NKEYS] = _dot_nt(q, keys_ref[hd])


def _scores(x, norm_g, w_query, keys1, keys2, *, tr):
    t, d = x.shape
    width = 2 * PEER_HEADS * PEER_NKEYS
    return pl.pallas_call(
        _scores_kernel,
        grid=(t // tr,),
        in_specs=[
            pl.BlockSpec((tr, d), lambda i: (i, 0)),
            _const_spec((1, d)),
            _const_spec(w_query.shape),
            _const_spec(keys1.shape),
            _const_spec(keys2.shape),
        ],
        out_specs=[
            pl.BlockSpec((tr, d), lambda i: (i, 0)),
            pl.BlockSpec((tr, width), lambda i: (i, 0)),
        ],
        out_shape=[
            jax.ShapeDtypeStruct((t, d), _f32),
            jax.ShapeDtypeStruct((t, width), _f32),
        ],
        compiler_params=pltpu.CompilerParams(
            dimension_semantics=("parallel",),
            vmem_limit_bytes=VMEM_LIMIT_BYTES),
        name="scores",
    )(x, norm_g, w_query, keys1, keys2)


SC_CORES = 2
SC_SUBCORES = 16
SC_LANES = 16
SC_WORKERS = SC_CORES * SC_SUBCORES
SC_TOKENS = 8
SC_GATHER = 32
SC_BLOCK = 8
SC_UNROLL = 2
SC_COLS = 8
HI_MASK = -65536


def _sc_mesh():
    return plsc.VectorSubcoreMesh(core_axis_name="c", subcore_axis_name="s")


def _sc_worker_base(per_worker):
    return (lax.axis_index("s") * SC_CORES + lax.axis_index("c")) * per_worker


def _sc_gather_loop(tab_hbm, idx_v, bufs, consume):
    n_parts = PEER_SLOTS // SC_GATHER
    n_gathers = SC_TOKENS * n_parts

    def gather(g, parity):
        rows, sem = bufs[parity]
        i = g // n_parts
        col = pl.multiple_of((g % n_parts) * SC_GATHER, SC_GATHER)
        return pltpu.make_async_copy(tab_hbm.at[idx_v.at[i, pl.ds(col, SC_GATHER)]], rows, sem)

    gather(0, 0).start()

    @pl.loop(0, n_gathers // 2)
    def _(pair):
        g = 2 * pair
        gather(g + 1, 1).start()
        gather(g, 0).wait()
        consume(g // n_parts, g % n_parts, bufs[0][0])

        @pl.when(g + 2 < n_gathers)
        def _():
            gather(g + 2, 0).start()

        gather(g + 1, 1).wait()
        consume((g + 1) // n_parts, (g + 1) % n_parts, bufs[1][0])


def _pack_halves(tab):
    half = tab.shape[1] // 2
    bits = lax.bitcast_convert_type(tab.astype(_bf16), jnp.uint16).astype(jnp.uint32)
    return lax.bitcast_convert_type(bits[:, :half] | (bits[:, half:] << 16), jnp.int32)


def _sc_unpack(words):
    lo = lax.bitcast_convert_type(words << 16, _f32)
    hi = lax.bitcast_convert_type(words & HI_MASK, _f32)
    return lo, hi


def _sc_udot(idx, tab, h):
    t, d = h.shape
    per_worker = t // SC_WORKERS
    half = d // 2

    @functools.partial(
        pl.kernel, mesh=_sc_mesh(),
        out_type=jax.ShapeDtypeStruct((t, PEER_SLOTS), _f32),
        scratch_types=[
            pltpu.VMEM((SC_TOKENS, PEER_SLOTS), jnp.int32),
            pltpu.VMEM((SC_TOKENS, d), _f32),
            pltpu.VMEM((SC_TOKENS, PEER_SLOTS), _f32),
            pltpu.VMEM((SC_GATHER, half), jnp.int32),
            pltpu.VMEM((SC_GATHER, half), jnp.int32),
            pltpu.SemaphoreType.DMA,
            pltpu.SemaphoreType.DMA,
        ],
        compiler_params=pltpu.CompilerParams(needs_layout_passes=False),
        name="sc_udot",
    )
    def body(idx_hbm, tab_hbm, h_hbm, out_hbm, idx_v, h_v, acts_v, rows0, rows1, sem0, sem1):
        base = _sc_worker_base(per_worker)
        lane = lax.iota(jnp.int32, SC_LANES)

        def dots(i, part, rows):
            def block_body(blk, carry):
                row0 = blk * SC_LANES
                outv = jnp.zeros((SC_LANES,), _f32)
                for sub in range(SC_LANES // SC_BLOCK):
                    def chunk_body(jj, accs):
                        accs = list(accs)
                        for u in range(SC_UNROLL):
                            off = pl.multiple_of((jj * SC_UNROLL + u) * SC_LANES, SC_LANES)
                            h_lo = h_v[i, pl.ds(off, SC_LANES)]
                            h_hi = h_v[i, pl.ds(half + off, SC_LANES)]
                            for e in range(SC_BLOCK):
                                lo, hi = _sc_unpack(rows[row0 + sub * SC_BLOCK + e, pl.ds(off, SC_LANES)])
                                accs[e] = accs[e] + (lo * h_lo + hi * h_hi)
                        return tuple(accs)

                    accs = lax.fori_loop(0, half // SC_LANES // SC_UNROLL, chunk_body,
                                         tuple(jnp.zeros((SC_LANES,), _f32) for _ in range(SC_BLOCK)))
                    for e in range(SC_BLOCK):
                        outv = jnp.where(lane == sub * SC_BLOCK + e, jnp.sum(accs[e]), outv)
                col = pl.multiple_of(part * SC_GATHER + row0, SC_LANES)
                acts_v[i, pl.ds(col, SC_LANES)] = outv
                return carry

            lax.fori_loop(0, SC_GATHER // SC_LANES, block_body, 0)

        @pl.loop(0, per_worker // SC_TOKENS)
        def _(step):
            tok = pl.multiple_of(base + step * SC_TOKENS, SC_TOKENS)
            pltpu.sync_copy(idx_hbm.at[pl.ds(tok, SC_TOKENS)], idx_v)
            pltpu.sync_copy(h_hbm.at[pl.ds(tok, SC_TOKENS)], h_v)
            _sc_gather_loop(tab_hbm, idx_v, ((rows0, sem0), (rows1, sem1)), dots)
            pltpu.sync_copy(acts_v, out_hbm.at[pl.ds(tok, SC_TOKENS)])

    return body(idx, tab, h)


def _stair_vectors():
    pairs = [(a, c) for a in range(PEER_TOPK) for c in range(PEER_TOPK) if (a + 1) * (c + 1) <= PEER_TOPK]
    n = -(-len(pairs) // SC_LANES)
    fill = n * SC_LANES - len(pairs)
    a = jnp.asarray([p[0] for p in pairs] + [0] * fill, jnp.int32).reshape(n, SC_LANES)
    c = jnp.asarray([p[1] for p in pairs] + [0] * fill, jnp.int32).reshape(n, SC_LANES)
    pad = jnp.asarray([0.0] * len(pairs) + [-jnp.inf] * fill, _f32).reshape(n, SC_LANES)
    return a, c, pad


def _sc_topk(scores):
    t, width = scores.shape
    per_worker = t // SC_WORKERS
    pair_a, pair_c, pair_pad = _stair_vectors()
    n_cand = pair_a.shape[0]
    n_vec = PEER_NKEYS // SC_LANES

    @functools.partial(
        pl.kernel, mesh=_sc_mesh(),
        out_type=(jax.ShapeDtypeStruct((t, PEER_SLOTS), jnp.int32),
                  jax.ShapeDtypeStruct((t, PEER_SLOTS), _f32)),
        scratch_types=[
            pltpu.VMEM((SC_TOKENS, width), _f32),
            pltpu.VMEM((SC_TOKENS, PEER_SLOTS), jnp.int32),
            pltpu.VMEM((SC_TOKENS, PEER_SLOTS), _f32),
            pltpu.VMEM((n_cand, SC_LANES), jnp.int32),
            pltpu.VMEM((n_cand, SC_LANES), jnp.int32),
            pltpu.VMEM((n_cand, SC_LANES), _f32),
            pltpu.VMEM((4, SC_LANES), _f32),
        ],
        compiler_params=pltpu.CompilerParams(needs_layout_passes=False),
        name="sc_topk",
    )
    def body(s_hbm, a_hbm, c_hbm, pad_hbm, idx_hbm, gate_hbm, s_v, idx_v, gate_v, a_v, c_v, pad_v, top_v):
        base = _sc_worker_base(per_worker)
        lane = lax.iota(jnp.int32, SC_LANES)
        pltpu.sync_copy(a_hbm, a_v)
        pltpu.sync_copy(c_hbm, c_v)
        pltpu.sync_copy(pad_hbm, pad_v)

        def sort_desc(k, v):
            return plsc.sort_key_val(k, v, descending=True)

        def merge(x, y):
            yk, yv = lax.rev(y[0], (0,)), lax.rev(y[1], (0,))
            take = x[0] >= yk
            return sort_desc(jnp.where(take, x[0], yk), jnp.where(take, x[1], yv))

        def top_of(vectors):
            while len(vectors) > 1:
                vectors = [merge(vectors[j], vectors[j + 1]) for j in range(0, len(vectors), 2)]
            return vectors[0]

        def head_body(i, hd):
            halves = []
            for half in range(2):
                col = (2 * hd + half) * PEER_NKEYS
                vecs = [sort_desc(s_v[i, pl.ds(pl.multiple_of(col + j * SC_LANES, SC_LANES), SC_LANES)],
                                  lane + j * SC_LANES) for j in range(n_vec)]
                halves.append(top_of(vecs))
            (v1, i1), (v2, i2) = halves
            top_v[0, :] = v1
            top_v[1, :] = i1.astype(_f32)
            top_v[2, :] = v2
            top_v[3, :] = i2.astype(_f32)

            def pick(row, pos):
                return plsc.load_gather(top_v, [jnp.full((SC_LANES,), row, jnp.int32), pos])

            cands = []
            for q in range(n_cand):
                a, c = a_v[q, :], c_v[q, :]
                cs = pick(0, a) + pick(2, c) + pad_v[q, :]
                ci = pick(1, a) * float(PEER_NKEYS) + pick(3, c)
                cands.append(sort_desc(cs, ci))
            top_s, top_i = top_of(cands)
            e = jnp.exp(top_s - jnp.max(top_s))
            slots = pl.ds(pl.multiple_of(hd * PEER_TOPK, PEER_TOPK), PEER_TOPK)
            idx_v[i, slots] = top_i.astype(jnp.int32)
            gate_v[i, slots] = e / jnp.sum(e)

        @pl.loop(0, per_worker // SC_TOKENS)
        def _(step):
            tok = pl.multiple_of(base + step * SC_TOKENS, SC_TOKENS)
            pltpu.sync_copy(s_hbm.at[pl.ds(tok, SC_TOKENS)], s_v)

            @pl.loop(0, SC_TOKENS * PEER_HEADS)
            def _(n):
                head_body(n // PEER_HEADS, n % PEER_HEADS)

            pltpu.sync_copy(idx_v, idx_hbm.at[pl.ds(tok, SC_TOKENS)])
            pltpu.sync_copy(gate_v, gate_hbm.at[pl.ds(tok, SC_TOKENS)])

    return body(scores, pair_a, pair_c, pair_pad)


def _sc_vaxpy(idx, w, tab, *, first, count):
    half = tab.shape[1]
    d = 2 * half
    per_worker = count // SC_WORKERS
    span = SC_COLS * SC_LANES

    @functools.partial(
        pl.kernel, mesh=_sc_mesh(),
        out_type=jax.ShapeDtypeStruct((count, d), _f32),
        scratch_types=[
            pltpu.VMEM((SC_TOKENS, PEER_SLOTS), jnp.int32),
            pltpu.VMEM((SC_TOKENS, PEER_SLOTS), _f32),
            pltpu.VMEM((SC_TOKENS, d), _f32),
            pltpu.VMEM((SC_GATHER, half), jnp.int32),
            pltpu.VMEM((SC_GATHER, half), jnp.int32),
            pltpu.SemaphoreType.DMA,
            pltpu.SemaphoreType.DMA,
        ],
        compiler_params=pltpu.CompilerParams(needs_layout_passes=False),
        name="sc_vaxpy",
    )
    def body(idx_hbm, w_hbm, tab_hbm, out_hbm, idx_v, w_v, out_v, rows0, rows1, sem0, sem1):
        base = _sc_worker_base(per_worker)

        def accumulate(i, part, rows):
            i_vec = jnp.full((SC_LANES,), i, jnp.int32)

            def span_body(cq, carry):
                def cols(c, offset=0):
                    return pl.ds(pl.multiple_of(offset + cq * span + c * SC_LANES, SC_LANES), SC_LANES)

                def expert_body(e, accs):
                    k_vec = jnp.full((SC_LANES,), part * SC_GATHER + e, jnp.int32)
                    wv = plsc.load_gather(w_v, [i_vec, k_vec])
                    new = []
                    for c in range(SC_COLS):
                        lo, hi = _sc_unpack(rows[e, cols(c)])
                        new += [accs[2 * c] + lo * wv, accs[2 * c + 1] + hi * wv]
                    return tuple(new)

                init = []
                for c in range(SC_COLS):
                    init += [out_v[i, cols(c)], out_v[i, cols(c, half)]]
                accs = lax.fori_loop(0, SC_GATHER, expert_body, tuple(init))
                for c in range(SC_COLS):
                    out_v[i, cols(c)] = accs[2 * c]
                    out_v[i, cols(c, half)] = accs[2 * c + 1]
                return carry

            lax.fori_loop(0, half // span, span_body, 0)

        @pl.loop(0, per_worker // SC_TOKENS)
        def _(step):
            off = pl.multiple_of(base + step * SC_TOKENS, SC_TOKENS)
            pltpu.sync_copy(idx_hbm.at[pl.ds(first + off, SC_TOKENS)], idx_v)
            pltpu.sync_copy(w_hbm.at[pl.ds(first + off, SC_TOKENS)], w_v)

            @pl.loop(0, SC_TOKENS)
            def _(i):
                @pl.loop(0, d // SC_LANES)
                def _(j):
                    out_v[i, pl.ds(pl.multiple_of(j * SC_LANES, SC_LANES), SC_LANES)] = (
                        jnp.zeros((SC_LANES,), _f32))

            _sc_gather_loop(tab_hbm, idx_v, ((rows0, sem0), (rows1, sem1)), accumulate)
            pltpu.sync_copy(out_v, out_hbm.at[pl.ds(off, SC_TOKENS)])

    return body(idx, w, tab)


ROW_CHUNKS = 8
PACK_ROWS = ROW_CHUNKS // 2


def _pack_table(tab):
    n, d = tab.shape
    bits = lax.bitcast_convert_type(tab.astype(_bf16), jnp.uint16).astype(jnp.uint32)
    bits = bits.reshape(n, PACK_ROWS, 2, d // ROW_CHUNKS)
    word = bits[:, :, 0, :] | (bits[:, :, 1, :] << 16)
    return lax.bitcast_convert_type(word, jnp.int32)


def _gate_weights_kernel(acts_ref, gate_ref, w_ref):
    a = acts_ref[...]
    gelu = 0.5 * a * (1.0 + lax.erf(a * (2.0 ** -0.5)))
    w_ref[...] = gate_ref[...] * gelu


def _gate_weights(acts, gate, *, tw):
    t, n = acts.shape
    spec = pl.BlockSpec((tw, n), lambda i: (i, 0))
    return pl.pallas_call(
        _gate_weights_kernel,
        grid=(t // tw,),
        in_specs=[spec, spec],
        out_specs=spec,
        out_shape=jax.ShapeDtypeStruct((t, n), _f32),
        compiler_params=pltpu.CompilerParams(dimension_semantics=("parallel",)),
        name="gate_weights",
    )(acts, gate)


def _gather_rows(idx_ref, t, tab_ref, rows_ref):
    for k in range(PEER_SLOTS):
        row = pl.multiple_of(idx_ref[t, k], PACK_ROWS)
        rows_ref[k * PACK_ROWS:(k + 1) * PACK_ROWS, :] = tab_ref[pl.ds(row, PACK_ROWS), :]


def _rows_matrix(rows_ref):
    return pltpu.bitcast(rows_ref[...], _bf16)


def _token_loop(tb, idx_ref, tab_ref, rows_a, rows_b, compute):
    _gather_rows(idx_ref, 0, tab_ref, rows_a)

    def pair_body(i, carry):
        t0 = 2 * i
        _gather_rows(idx_ref, t0 + 1, tab_ref, rows_b)
        compute(t0, _rows_matrix(rows_a))
        _gather_rows(idx_ref, jnp.minimum(t0 + 2, tb - 1), tab_ref, rows_a)
        compute(t0 + 1, _rows_matrix(rows_b))
        return carry

    lax.fori_loop(0, tb // 2, pair_body, 0)


def _chunk_diag_mask():
    shape = (ROW_CHUNKS, PEER_SLOTS * ROW_CHUNKS)
    return (_iota(shape, 1) % ROW_CHUNKS) == _iota(shape, 0)


def _vaxpy_kernel(idx_ref, w_ref, tab_ref, out_ref, rows_a, rows_b, wrep_ref):
    tb = out_ref.shape[0]
    diag = _chunk_diag_mask()
    shape = (PEER_SLOTS, PEER_SLOTS * ROW_CHUNKS)
    spread = (_iota(shape, 0) == (_iota(shape, 1) // ROW_CHUNKS)).astype(_bf16)
    wrep_ref[...] = _dot_exact_rhs01(w_ref[...], spread)

    def compute(t, m):
        w_row = jnp.broadcast_to(wrep_ref[pl.ds(t, 1), :], diag.shape)
        w_hi, w_lo = _split2(jnp.where(diag, w_row, 0.0))
        out_ref[t] = _dot(w_hi, m) + _dot(w_lo, m)

    _token_loop(tb, idx_ref, tab_ref, rows_a, rows_b, compute)


def _vaxpy(idx, w, tab, *, tb, count):
    return pl.pallas_call(
        _vaxpy_kernel,
        grid=(count // tb,),
        in_specs=[
            pl.BlockSpec((tb, PEER_SLOTS), lambda i: (i, 0), memory_space=pltpu.SMEM),
            pl.BlockSpec((tb, PEER_SLOTS), lambda i: (i, 0)),
            _const_spec(tab.shape),
        ],
        out_specs=pl.BlockSpec((tb, ROW_CHUNKS, 128), lambda i: (i, 0, 0)),
        out_shape=jax.ShapeDtypeStruct((count, ROW_CHUNKS, 128), _f32),
        scratch_shapes=[
            pltpu.VMEM((PEER_SLOTS * PACK_ROWS, 128), jnp.int32),
            pltpu.VMEM((PEER_SLOTS * PACK_ROWS, 128), jnp.int32),
            pltpu.VMEM((tb, PEER_SLOTS * ROW_CHUNKS), _f32),
        ],
        compiler_params=pltpu.CompilerParams(
            dimension_semantics=("parallel",),
            vmem_limit_bytes=VMEM_LIMIT_BYTES),
        name="vaxpy",
    )(idx, w, tab)


def _final_kernel(x_ref, p_ref, g_ref, out_ref, *, normalize):
    x = x_ref[...] + p_ref[...]
    if normalize:
        x = x * lax.rsqrt(jnp.mean(x * x, axis=-1, keepdims=True) + EPS) * g_ref[...]
    out_ref[...] = x


def _final(x, peer, g, *, normalize, tf, first):
    t, d = peer.shape
    off = first // tf
    return pl.pallas_call(
        functools.partial(_final_kernel, normalize=normalize),
        grid=(t // tf,),
        in_specs=[
            pl.BlockSpec((tf, d), lambda i: (i + off, 0)),
            pl.BlockSpec((tf, d), lambda i: (i, 0)),
            _const_spec((1, d)),
        ],
        out_specs=pl.BlockSpec((tf, d), lambda i: (i, 0)),
        out_shape=jax.ShapeDtypeStruct((t, d), _f32),
        compiler_params=pltpu.CompilerParams(dimension_semantics=("parallel",)),
        name="final_norm",
    )(x, peer, g)


def kernel(x, norm_mix_g, w_in, hg_lb_logits, hg_out_norm_g, conv_w, w_branch_hg, w_branch_conv, w_out, norm_ffn_g, peer_w_query, peer_keys1, peer_keys2, peer_u, peer_v, norm_final_g):
    b_, s_, d = x.shape
    depth = w_in.shape[0]
    lb_all = jnp.cumsum(jax.nn.softmax(hg_lb_logits.astype(_f32), axis=0), axis=0)
    n_groups = BATCH_GROUPS if b_ % BATCH_GROUPS == 0 else 1
    bg = b_ // n_groups
    tg = bg * s_
    n_sc = SC_SHARE if SC_SHARE < tg else 0
    n_tc = tg - n_sc
    for l in range(depth):
        wq = peer_w_query[l].astype(_bf16).reshape(d, PEER_HEADS, 2, PEER_HALF).transpose(1, 2, 0, 3)
        v_tab = _pack_table(peer_v[l])
        u_sc = _pack_halves(peer_u[l])
        v_sc = _pack_halves(peer_v[l])
        last = l == depth - 1
        g = norm_final_g[None] if last else jnp.ones((1, d), _f32)
        def front(c, x_in, sc_before=None):
            xc = _mix(x_in, norm_mix_g[l][None], w_in[l].astype(_bf16), lb_all[l][None],
                      hg_out_norm_g[l][None], conv_w[l], w_branch_hg[l].astype(_bf16),
                      w_branch_conv[l].astype(_bf16), w_out[l].astype(_bf16), ts=MIX_TILE)
            xf = xc.reshape(tg, d)
            h, scores = _scores(xf, norm_ffn_g[l][None], wq,
                                peer_keys1[l].astype(_bf16), peer_keys2[l].astype(_bf16), tr=ROUTE_TILE)
            idx, gate = _sc_topk(scores)
            idx_sc = idx if sc_before is None else lax.optimization_barrier((idx, sc_before))[0]
            return xf, idx, gate, _sc_udot(idx_sc, u_sc, h)

        def back(xf, idx, gate, acts):
            w = _gate_weights(acts, gate, tw=FINAL_TILE)
            peer_tc = _vaxpy(idx * PACK_ROWS, w, v_tab.reshape(-1, 128), tb=EXPERT_TILE, count=n_tc).reshape(n_tc, d)
            peer_sc = _sc_vaxpy(idx, w, v_sc, first=n_tc, count=n_sc) if n_sc else None
            return peer_tc, peer_sc

        groups = [x[c * bg:(c + 1) * bg] for c in range(n_groups)]
        fronts = {c: front(c, groups[c]) for c in range(min(GROUPS_AHEAD + 1, n_groups))}
        done, peer_tc = [], None
        for c in range(n_groups):
            xf, idx, gate, acts = fronts.pop(c)
            after = [fronts[c + GROUPS_AHEAD][1]] if c + GROUPS_AHEAD in fronts else []
            after += [peer_tc] if peer_tc is not None else []
            if after:
                acts = lax.optimization_barrier((acts, *after))[0]
            peer_tc, peer_sc = back(xf, idx, gate, acts)
            done.append((xf, peer_tc, peer_sc))
            nxt = c + GROUPS_AHEAD + 1
            if nxt < n_groups:
                fronts[nxt] = front(nxt, lax.optimization_barrier((groups[nxt], peer_tc))[0], peer_sc)
        outs = []
        for xf, p_tc, p_sc in done:
            outs.append(_final(xf, p_tc, g, normalize=last, tf=FINAL_TILE, first=0))
            if p_sc is not None:
                p_sc = lax.optimization_barrier((p_sc, peer_tc))[0]
                outs.append(_final(xf, p_sc, g, normalize=last, tf=FINAL_TILE, first=n_tc))
        x = jnp.concatenate(outs, axis=0).reshape(b_, s_, d)
    return x
```

```python
import functools

import jax
import jax.numpy as jnp
from jax import lax
from jax.experimental import pallas as pl
from jax.experimental.pallas import tpu as pltpu
from jax.experimental.pallas import tpu_sc as plsc

EPS = 1e-6
CHUNK = 64
SUB = 16
HEADS = 8
HEAD_DIM = 64
HG_WIDTH = HEADS * HEAD_DIM
GROUP = 256
N_GROUPS = HG_WIDTH // GROUP
CONV_K = 3
PEER_HEADS = 8
PEER_NKEYS = 128
PEER_HALF = 128
PEER_TOPK = 16
PEER_SLOTS = PEER_HEADS * PEER_TOPK

VMEM_LIMIT_BYTES = 56 * 1024 * 1024

MIX_TILE = 256
ROUTE_TILE = 256
EXPERT_TILE = 128
FINAL_TILE = 512
BATCH_GROUPS = 4
SC_SHARE = 1536
GROUPS_AHEAD = 2

_f32 = jnp.float32
_bf16 = jnp.bfloat16


def _dot(a, b):
    return jnp.dot(a, b, preferred_element_type=_f32)


def _dot_nt(a, b):
    return lax.dot_general(a, b, (((1,), (1,)), ((), ())), preferred_element_type=_f32)


def _dot_tn(a, b):
    return lax.dot_general(a, b, (((0,), (0,)), ((), ())), preferred_element_type=_f32)


def _split3(x):
    hi = x.astype(_bf16)
    r1 = x - hi.astype(_f32)
    mid = r1.astype(_bf16)
    lo = (r1 - mid.astype(_f32)).astype(_bf16)
    return hi, mid, lo


def _split2(x):
    hi = x.astype(_bf16)
    lo = (x - hi.astype(_f32)).astype(_bf16)
    return hi, lo


def _dot_exact_rhs01(x, m01):
    hi, mid, lo = _split3(x)
    return _dot(hi, m01) + _dot(mid, m01) + _dot(lo, m01)


def _dot_exact_lhs01(m01, x):
    hi, mid, lo = _split3(x)
    return _dot(m01, hi) + _dot(m01, mid) + _dot(m01, lo)


def _iota(shape, dim):
    return lax.broadcasted_iota(jnp.int32, shape, dim)


def _hgrn2_chunk(q, k, lf, v, state_ref):
    n_sub = CHUNK // SUB
    row = _iota((CHUNK, CHUNK), 0)
    col = _iota((CHUNK, CHUNK), 1)
    tril = (col <= row).astype(_bf16)
    b = _dot_exact_lhs01(tril, lf)

    b_end = [b[(j + 1) * SUB - 1:(j + 1) * SUB, :] for j in range(n_sub)]
    b_end_rows = jnp.concatenate([jnp.broadcast_to(e, (SUB, HG_WIDTH)) for e in b_end], axis=0)
    b_last = b_end[-1]

    q_in = (q * jnp.exp(b)).astype(_bf16)
    k_sub = (k * jnp.exp(b_end_rows - b)).astype(_bf16)
    k_out = (k * jnp.exp(b_last - b)).astype(_bf16)
    q_from = [(q * jnp.exp(jnp.minimum(b - b_end[j], 0.0))).astype(_bf16) for j in range(n_sub - 1)]
    v_b = v.astype(_bf16)

    gr = _iota((GROUP, GROUP), 0) // HEAD_DIM
    gc = _iota((GROUP, GROUP), 1) // HEAD_DIM
    head_mask = gr == gc
    t_blk = _iota((CHUNK, GROUP), 0) // SUB
    s_blk = (_iota((CHUNK, GROUP), 1) % HEAD_DIM) // SUB

    outs = []
    for g in range(N_GROUPS):
        sl = slice(g * GROUP, (g + 1) * GROUP)
        st = state_ref[g]
        o_g = _dot_nt(q_in[:, sl], st.astype(_bf16))

        zero_b = jnp.zeros((), _bf16)
        k_bd = jnp.where(head_mask, jnp.concatenate([k_sub[:, sl]] * (GROUP // CHUNK), axis=0), zero_b)
        v_bd = jnp.where(head_mask, jnp.concatenate([v_b[:, sl]] * (GROUP // CHUNK), axis=0), zero_b)
        q_stack = jnp.concatenate([qf[:, sl] for qf in q_from], axis=0)
        r = _dot_nt(q_stack, k_bd)
        scores = jnp.zeros((CHUNK, GROUP), _f32)
        for j in range(n_sub - 1):
            sel = (s_blk == j) & (t_blk > j)
            scores = jnp.where(sel, r[j * CHUNK:(j + 1) * CHUNK, :], scores)
        o_g = o_g + _dot(scores.astype(_bf16), v_bd)
        outs.append(o_g)

        upd = _dot_tn(v_b[:, sl], k_out[:, sl])
        decay = jnp.exp(b_last[:, sl])
        state_ref[g] = st * decay + jnp.where(head_mask, upd, 0.0)
    o = jnp.concatenate(outs, axis=1)

    ones_bd = ((_iota((HG_WIDTH, HG_WIDTH), 0) // HEAD_DIM)
               == (_iota((HG_WIDTH, HG_WIDTH), 1) // HEAD_DIM)).astype(_bf16)
    t_in_sub = _iota((CHUNK, HG_WIDTH), 0) % SUB
    for lag in range(SUB):
        if lag == 0:
            p = q * k
            v_l = v
        else:
            valid = t_in_sub >= lag
            k_l = pltpu.roll(k, lag, 0)
            b_l = pltpu.roll(b, lag, 0)
            v_l = pltpu.roll(v, lag, 0)
            p = jnp.where(valid, q * k_l * jnp.exp(jnp.minimum(b - b_l, 0.0)), 0.0)
        s_l = _dot(p.astype(_bf16), ones_bd)
        o = o + s_l * v_l
    return o


def _mix_kernel(x_ref, g_ref, win_ref, lb_ref, hgn_ref, convw_ref, pa_ref, pb_ref, wo_ref,
                out_ref, state_ref, carry_ref, q_s, k_s, lf_s, v_s, o_s):
    ts = x_ref.shape[0]
    d_model = x_ref.shape[1]
    w = HG_WIDTH

    @pl.when(pl.program_id(1) == 0)
    def _():
        state_ref[...] = jnp.zeros_like(state_ref)
        carry_ref[...] = jnp.zeros_like(carry_ref)

    x = x_ref[...]
    h = x * lax.rsqrt(jnp.mean(x * x, axis=-1, keepdims=True) + EPS) * g_ref[...]
    hb = h.astype(_bf16)

    def proj(i, width=w):
        return _dot(hb, win_ref[:, i * w:i * w + width])

    lb = lb_ref[...]
    q_s[...] = jax.nn.silu(proj(0)) * (HEAD_DIM ** -0.5)
    forget = lb + (1.0 - lb) * jax.nn.sigmoid(proj(1))
    k_s[...] = 1.0 - forget
    lf_s[...] = jnp.log(forget)
    v_s[...] = proj(2)

    def chunk_body(c, carry):
        rows = pl.ds(pl.multiple_of(c * CHUNK, CHUNK), CHUNK)
        o_s[rows, :] = _hgrn2_chunk(q_s[rows, :], k_s[rows, :], lf_s[rows, :], v_s[rows, :], state_ref)
        return carry

    lax.fori_loop(0, ts // CHUNK, chunk_body, 0)

    o = o_s[...]
    ones_bd = ((_iota((w, w), 0) // HEAD_DIM) == (_iota((w, w), 1) // HEAD_DIM)).astype(_bf16)
    ms = _dot_exact_rhs01(o * o, ones_bd) * (1.0 / HEAD_DIM)
    o = o * lax.rsqrt(ms + EPS) * hgn_ref[...]
    y_a = (o * jax.nn.silu(proj(3))).astype(_bf16)

    u = proj(5) * proj(6)
    prev = carry_ref[...]
    rowi = _iota((ts, w), 0)
    u1 = jnp.where(rowi >= 1, pltpu.roll(u, 1, 0), jnp.broadcast_to(prev[7:8, :], (ts, w)))
    u2 = jnp.where(rowi >= 2, pltpu.roll(u, 2, 0),
                   jnp.where(rowi == 1, jnp.broadcast_to(prev[7:8, :], (ts, w)),
                             jnp.broadcast_to(prev[6:7, :], (ts, w))))
    carry_ref[...] = u[ts - 8:, :]
    cw = convw_ref[...]
    y_b = (proj(4) * (cw[0:1, :] * u2 + cw[1:2, :] * u1 + cw[2:3, :] * u)).astype(_bf16)

    g_a = jax.nn.sigmoid(proj(7, d_model))
    g_b = jax.nn.sigmoid(_dot(hb, win_ref[:, 7 * w + d_model:7 * w + 2 * d_model]))
    merged = g_a * _dot(y_a, pa_ref[...]) + g_b * _dot(y_b, pb_ref[...])
    out_ref[...] = x + _dot(merged.astype(_bf16), wo_ref[...])


def _const_spec(shape):
    nd = len(shape)
    return pl.BlockSpec(shape, lambda *_: (0,) * nd, pipeline_mode=pl.Buffered(1))


def _mix(x, norm_g, w_in, lb, hg_norm_g, conv_w, w_a, w_b, w_o, *, ts):
    b_, s_, d = x.shape
    in_cols = w_in.shape[1]
    w = HG_WIDTH
    grid = (b_, s_ // ts)
    return pl.pallas_call(
        _mix_kernel,
        grid=grid,
        in_specs=[
            pl.BlockSpec((None, ts, d), lambda b, s: (b, s, 0)),
            _const_spec((1, d)),
            _const_spec((d, in_cols)),
            _const_spec((1, w)),
            _const_spec((1, w)),
            _const_spec((CONV_K, w)),
            _const_spec((w, d)),
            _const_spec((w, d)),
            _const_spec((d, d)),
        ],
        out_specs=pl.BlockSpec((None, ts, d), lambda b, s: (b, s, 0)),
        out_shape=jax.ShapeDtypeStruct((b_, s_, d), _f32),
        scratch_shapes=[
            pltpu.VMEM((N_GROUPS, GROUP, GROUP), _f32),
            pltpu.VMEM((8, w), _f32),
            pltpu.VMEM((ts, w), _f32),
            pltpu.VMEM((ts, w), _f32),
            pltpu.VMEM((ts, w), _f32),
            pltpu.VMEM((ts, w), _f32),
            pltpu.VMEM((ts, w), _f32),
        ],
        compiler_params=pltpu.CompilerParams(
            dimension_semantics=("parallel", "arbitrary"),
            vmem_limit_bytes=VMEM_LIMIT_BYTES),
        name="mix",
    )(x, norm_g, w_in, lb, hg_norm_g, conv_w, w_a, w_b, w_o)


def _stair_pairs():
    pairs = [(a, c) for a in range(PEER_TOPK) for c in range(PEER_TOPK) if (a + 1) * (c + 1) <= PEER_TOPK]
    rows = -(-len(pairs) // 8) * 8
    ranks = jnp.arange(PEER_TOPK)[None, :]
    a_col = jnp.asarray([a for a, _ in pairs] + [-1] * (rows - len(pairs)))[:, None]
    c_col = jnp.asarray([c for _, c in pairs] + [-1] * (rows - len(pairs)))[:, None]
    pad = jnp.where(a_col < 0, -jnp.inf, 0.0).astype(_f32)
    return (a_col == ranks).astype(_bf16), (c_col == ranks).astype(_bf16), pad


ROUTE_LANES = 128


def _route_kernel(x_ref, g_ref, wq_ref, k1_ref, k2_ref, sa_ref, sc_ref, pad_ref, h_ref, idx_ref, gate_ref,
                  idx_t, e_t, top_ref):
    tr = x_ref.shape[0]
    n_cand = sa_ref.shape[0]
    x = x_ref[...]
    h = x * lax.rsqrt(jnp.mean(x * x, axis=-1, keepdims=True) + EPS) * g_ref[...]
    h_ref[...] = h
    hb = h.astype(_bf16)

    key_row = _iota((PEER_NKEYS, ROUTE_LANES), 0).astype(_f32)
    cand_row = _iota((n_cand, tr), 0).astype(_f32)
    neg_inf = jnp.float32(-jnp.inf)

    def extract_max(s):
        m = jnp.max(s, axis=0, keepdims=True)
        i = jnp.min(jnp.where(s == m, key_row, float(PEER_NKEYS)), axis=0, keepdims=True)
        return m, i, jnp.where(key_row == i, neg_inf, s)

    def head_body(hd, carry):
        q1 = _dot(hb, wq_ref[hd, 0]).astype(_bf16)
        q2 = _dot(hb, wq_ref[hd, 1]).astype(_bf16)
        s1 = _dot_nt(k1_ref[hd], q1)
        s2 = _dot_nt(k2_ref[hd], q2)

        for lt in range(tr // ROUTE_LANES):
            lanes = slice(lt * ROUTE_LANES, (lt + 1) * ROUTE_LANES)

            def half_body(k, c):
                m1, i1, r1 = extract_max(c[0])
                m2, i2, r2 = extract_max(c[1])
                for j, row in enumerate((m1, i1, m2, i2)):
                    top_ref[lt, j, pl.ds(k, 1), :] = row
                return r1, r2

            lax.fori_loop(0, PEER_TOPK, half_body, (s1[:, lanes], s2[:, lanes]))

        def top(j):
            return jnp.concatenate([top_ref[lt, j] for lt in range(tr // ROUTE_LANES)], axis=1)

        sa = sa_ref[...]
        sc = sc_ref[...]
        cand_s = _dot_exact_lhs01(sa, top(0)) + _dot_exact_lhs01(sc, top(2)) + pad_ref[...]
        cand_i = (_dot(sa, top(1).astype(_bf16)) * float(PEER_NKEYS)
                  + _dot(sc, top(3).astype(_bf16)))

        def pick_body(k, c):
            cand_s, denom, m_first = c
            m = jnp.max(cand_s, axis=0, keepdims=True)
            pos = jnp.min(jnp.where(cand_s == m, cand_row, float(n_cand)), axis=0, keepdims=True)
            hit = cand_row == pos
            eid = jnp.max(jnp.where(hit, cand_i, -1.0), axis=0, keepdims=True)
            m_first = jnp.where(k == 0, m, m_first)
            e = jnp.exp(m - m_first)
            slot = hd * PEER_TOPK + k
            idx_t[pl.ds(slot, 1), :] = eid
            e_t[pl.ds(slot, 1), :] = e
            return jnp.where(hit, neg_inf, cand_s), denom + e, m_first

        zero_row = jnp.zeros((1, tr), _f32)
        _, denom, _ = lax.fori_loop(0, PEER_TOPK, pick_body, (cand_s, zero_row, zero_row))
        rows = pl.ds(pl.multiple_of(hd * PEER_TOPK, PEER_TOPK), PEER_TOPK)
        e_t[rows, :] = e_t[rows, :] / denom
        return carry

    lax.fori_loop(0, PEER_HEADS, head_body, 0)
    idx_ref[...] = idx_t[...].T.astype(jnp.int32)
    gate_ref[...] = e_t[...].T


def _route(x, norm_g, w_query, keys1, keys2, *, tr):
    t, d = x.shape
    sel_a, sel_c, pad = _stair_pairs()
    pad = jnp.broadcast_to(pad, (pad.shape[0], tr))
    return pl.pallas_call(
        _route_kernel,
        grid=(t // tr,),
        in_specs=[
            pl.BlockSpec((tr, d), lambda i: (i, 0)),
            _const_spec((1, d)),
            _const_spec(w_query.shape),
            _const_spec(keys1.shape),
            _const_spec(keys2.shape),
            _const_spec(sel_a.shape),
            _const_spec(sel_c.shape),
            _const_spec(pad.shape),
        ],
        out_specs=[
            pl.BlockSpec((tr, d), lambda i: (i, 0)),
            pl.BlockSpec((tr, PEER_SLOTS), lambda i: (i, 0)),
            pl.BlockSpec((tr, PEER_SLOTS), lambda i: (i, 0)),
        ],
        out_shape=[
            jax.ShapeDtypeStruct((t, d), _f32),
            jax.ShapeDtypeStruct((t, PEER_SLOTS), jnp.int32),
            jax.ShapeDtypeStruct((t, PEER_SLOTS), _f32),
        ],
        scratch_shapes=[
            pltpu.VMEM((PEER_SLOTS, tr), _f32),
            pltpu.VMEM((PEER_SLOTS, tr), _f32),
            pltpu.VMEM((tr // ROUTE_LANES, 4, PEER_TOPK, ROUTE_LANES), _f32),
        ],
        compiler_params=pltpu.CompilerParams(
            dimension_semantics=("parallel",),
            vmem_limit_bytes=VMEM_LIMIT_BYTES),
        name="route",
    )(x, norm_g, w_query, keys1, keys2, sel_a, sel_c, pad)


def _scores_kernel(x_ref, g_ref, wq_ref, k1_ref, k2_ref, h_ref, s_ref):
    x = x_ref[...]
    h = x * lax.rsqrt(jnp.mean(x * x, axis=-1, keepdims=True) + EPS) * g_ref[...]
    h_ref[...] = h
    hb = h.astype(_bf16)
    for hd in range(PEER_HEADS):
        for half, keys_ref in enumerate((k1_ref, k2_ref)):
            q = _dot(hb, wq_ref[hd, half]).astype(_bf16)
            col = (2 * hd + half) * PEER_NKEYS
            s_ref[:, col:col + PEER_NKEYS] = _dot_nt(q, keys_ref[hd])


def _scores(x, norm_g, w_query, keys1, keys2, *, tr):
    t, d = x.shape
    width = 2 * PEER_HEADS * PEER_NKEYS
    return pl.pallas_call(
        _scores_kernel,
        grid=(t // tr,),
        in_specs=[
            pl.BlockSpec((tr, d), lambda i: (i, 0)),
            _const_spec((1, d)),
            _const_spec(w_query.shape),
            _const_spec(keys1.shape),
            _const_spec(keys2.shape),
        ],
        out_specs=[
            pl.BlockSpec((tr, d), lambda i: (i, 0)),
            pl.BlockSpec((tr, width), lambda i: (i, 0)),
        ],
        out_shape=[
            jax.ShapeDtypeStruct((t, d), _f32),
            jax.ShapeDtypeStruct((t, width), _f32),
        ],
        compiler_params=pltpu.CompilerParams(
            dimension_semantics=("parallel",),
            vmem_limit_bytes=VMEM_LIMIT_BYTES),
        name="scores",
    )(x, norm_g, w_query, keys1, keys2)


SC_CORES = 2
SC_SUBCORES = 16
SC_LANES = 16
SC_WORKERS = SC_CORES * SC_SUBCORES
SC_TOKENS = 8
SC_GATHER = 32
SC_BLOCK = 8
SC_UNROLL = 2
SC_COLS = 8
HI_MASK = -65536


def _sc_mesh():
    return plsc.VectorSubcoreMesh(core_axis_name="c", subcore_axis_name="s")


def _sc_worker_base(per_worker):
    return (lax.axis_index("s") * SC_CORES + lax.axis_index("c")) * per_worker


def _sc_gather_loop(tab_hbm, idx_v, bufs, consume):
    n_parts = PEER_SLOTS // SC_GATHER
    n_gathers = SC_TOKENS * n_parts

    def gather(g, parity):
        rows, sem = bufs[parity]
        i = g // n_parts
        col = pl.multiple_of((g % n_parts) * SC_GATHER, SC_GATHER)
        return pltpu.make_async_copy(tab_hbm.at[idx_v.at[i, pl.ds(col, SC_GATHER)]], rows, sem)

    gather(0, 0).start()

    @pl.loop(0, n_gathers // 2)
    def _(pair):
        g = 2 * pair
        gather(g + 1, 1).start()
        gather(g, 0).wait()
        consume(g // n_parts, g % n_parts, bufs[0][0])

        @pl.when(g + 2 < n_gathers)
        def _():
            gather(g + 2, 0).start()

        gather(g + 1, 1).wait()
        consume((g + 1) // n_parts, (g + 1) % n_parts, bufs[1][0])


def _pack_halves(tab):
    half = tab.shape[1] // 2
    bits = lax.bitcast_convert_type(tab.astype(_bf16), jnp.uint16).astype(jnp.uint32)
    return lax.bitcast_convert_type(bits[:, :half] | (bits[:, half:] << 16), jnp.int32)


def _sc_unpack(words):
    lo = lax.bitcast_convert_type(words << 16, _f32)
    hi = lax.bitcast_convert_type(words & HI_MASK, _f32)
    return lo, hi


def _sc_udot(idx, tab, h):
    t, d = h.shape
    per_worker = t // SC_WORKERS
    half = d // 2

    @functools.partial(
        pl.kernel, mesh=_sc_mesh(),
        out_type=jax.ShapeDtypeStruct((t, PEER_SLOTS), _f32),
        scratch_types=[
            pltpu.VMEM((SC_TOKENS, PEER_SLOTS), jnp.int32),
            pltpu.VMEM((SC_TOKENS, d), _f32),
            pltpu.VMEM((SC_TOKENS, PEER_SLOTS), _f32),
            pltpu.VMEM((SC_GATHER, half), jnp.int32),
            pltpu.VMEM((SC_GATHER, half), jnp.int32),
            pltpu.SemaphoreType.DMA,
            pltpu.SemaphoreType.DMA,
        ],
        compiler_params=pltpu.CompilerParams(needs_layout_passes=False),
        name="sc_udot",
    )
    def body(idx_hbm, tab_hbm, h_hbm, out_hbm, idx_v, h_v, acts_v, rows0, rows1, sem0, sem1):
        base = _sc_worker_base(per_worker)
        lane = lax.iota(jnp.int32, SC_LANES)

        def dots(i, part, rows):
            def block_body(blk, carry):
                row0 = blk * SC_LANES
                outv = jnp.zeros((SC_LANES,), _f32)
                for sub in range(SC_LANES // SC_BLOCK):
                    def chunk_body(jj, accs):
                        accs = list(accs)
                        for u in range(SC_UNROLL):
                            off = pl.multiple_of((jj * SC_UNROLL + u) * SC_LANES, SC_LANES)
                            h_lo = h_v[i, pl.ds(off, SC_LANES)]
                            h_hi = h_v[i, pl.ds(half + off, SC_LANES)]
                            for e in range(SC_BLOCK):
                                lo, hi = _sc_unpack(rows[row0 + sub * SC_BLOCK + e, pl.ds(off, SC_LANES)])
                                accs[e] = accs[e] + (lo * h_lo + hi * h_hi)
                        return tuple(accs)

                    accs = lax.fori_loop(0, half // SC_LANES // SC_UNROLL, chunk_body,
                                         tuple(jnp.zeros((SC_LANES,), _f32) for _ in range(SC_BLOCK)))
                    for e in range(SC_BLOCK):
                        outv = jnp.where(lane == sub * SC_BLOCK + e, jnp.sum(accs[e]), outv)
                col = pl.multiple_of(part * SC_GATHER + row0, SC_LANES)
                acts_v[i, pl.ds(col, SC_LANES)] = outv
                return carry

            lax.fori_loop(0, SC_GATHER // SC_LANES, block_body, 0)

        @pl.loop(0, per_worker // SC_TOKENS)
        def _(step):
            tok = pl.multiple_of(base + step * SC_TOKENS, SC_TOKENS)
            pltpu.sync_copy(idx_hbm.at[pl.ds(tok, SC_TOKENS)], idx_v)
            pltpu.sync_copy(h_hbm.at[pl.ds(tok, SC_TOKENS)], h_v)
            _sc_gather_loop(tab_hbm, idx_v, ((rows0, sem0), (rows1, sem1)), dots)
            pltpu.sync_copy(acts_v, out_hbm.at[pl.ds(tok, SC_TOKENS)])

    return body(idx, tab, h)


def _stair_vectors():
    pairs = [(a, c) for a in range(PEER_TOPK) for c in range(PEER_TOPK) if (a + 1) * (c + 1) <= PEER_TOPK]
    n = -(-len(pairs) // SC_LANES)
    fill = n * SC_LANES - len(pairs)
    a = jnp.asarray([p[0] for p in pairs] + [0] * fill, jnp.int32).reshape(n, SC_LANES)
    c = jnp.asarray([p[1] for p in pairs] + [0] * fill, jnp.int32).reshape(n, SC_LANES)
    pad = jnp.asarray([0.0] * len(pairs) + [-jnp.inf] * fill, _f32).reshape(n, SC_LANES)
    return a, c, pad


def _sc_topk(scores):
    t, width = scores.shape
    per_worker = t // SC_WORKERS
    pair_a, pair_c, pair_pad = _stair_vectors()
    n_cand = pair_a.shape[0]
    n_vec = PEER_NKEYS // SC_LANES

    @functools.partial(
        pl.kernel, mesh=_sc_mesh(),
        out_type=(jax.ShapeDtypeStruct((t, PEER_SLOTS), jnp.int32),
                  jax.ShapeDtypeStruct((t, PEER_SLOTS), _f32)),
        scratch_types=[
            pltpu.VMEM((SC_TOKENS, width), _f32),
            pltpu.VMEM((SC_TOKENS, PEER_SLOTS), jnp.int32),
            pltpu.VMEM((SC_TOKENS, PEER_SLOTS), _f32),
            pltpu.VMEM((n_cand, SC_LANES), jnp.int32),
            pltpu.VMEM((n_cand, SC_LANES), jnp.int32),
            pltpu.VMEM((n_cand, SC_LANES), _f32),
            pltpu.VMEM((4, SC_LANES), _f32),
        ],
        compiler_params=pltpu.CompilerParams(needs_layout_passes=False),
        name="sc_topk",
    )
    def body(s_hbm, a_hbm, c_hbm, pad_hbm, idx_hbm, gate_hbm, s_v, idx_v, gate_v, a_v, c_v, pad_v, top_v):
        base = _sc_worker_base(per_worker)
        lane = lax.iota(jnp.int32, SC_LANES)
        pltpu.sync_copy(a_hbm, a_v)
        pltpu.sync_copy(c_hbm, c_v)
        pltpu.sync_copy(pad_hbm, pad_v)

        def sort_desc(k, v):
            return plsc.sort_key_val(k, v, descending=True)

        def merge(x, y):
            yk, yv = lax.rev(y[0], (0,)), lax.rev(y[1], (0,))
            take = x[0] >= yk
            return sort_desc(jnp.where(take, x[0], yk), jnp.where(take, x[1], yv))

        def top_of(vectors):
            while len(vectors) > 1:
                vectors = [merge(vectors[j], vectors[j + 1]) for j in range(0, len(vectors), 2)]
            return vectors[0]

        def head_body(i, hd):
            halves = []
            for half in range(2):
                col = (2 * hd + half) * PEER_NKEYS
                vecs = [sort_desc(s_v[i, pl.ds(pl.multiple_of(col + j * SC_LANES, SC_LANES), SC_LANES)],
                                  lane + j * SC_LANES) for j in range(n_vec)]
                halves.append(top_of(vecs))
            (v1, i1), (v2, i2) = halves
            top_v[0, :] = v1
            top_v[1, :] = i1.astype(_f32)
            top_v[2, :] = v2
            top_v[3, :] = i2.astype(_f32)

            def pick(row, pos):
                return plsc.load_gather(top_v, [jnp.full((SC_LANES,), row, jnp.int32), pos])

            cands = []
            for q in range(n_cand):
                a, c = a_v[q, :], c_v[q, :]
                cs = pick(0, a) + pick(2, c) + pad_v[q, :]
                ci = pick(1, a) * float(PEER_NKEYS) + pick(3, c)
                cands.append(sort_desc(cs, ci))
            top_s, top_i = top_of(cands)
            e = jnp.exp(top_s - jnp.max(top_s))
            slots = pl.ds(pl.multiple_of(hd * PEER_TOPK, PEER_TOPK), PEER_TOPK)
            idx_v[i, slots] = top_i.astype(jnp.int32)
            gate_v[i, slots] = e / jnp.sum(e)

        @pl.loop(0, per_worker // SC_TOKENS)
        def _(step):
            tok = pl.multiple_of(base + step * SC_TOKENS, SC_TOKENS)
            pltpu.sync_copy(s_hbm.at[pl.ds(tok, SC_TOKENS)], s_v)

            @pl.loop(0, SC_TOKENS * PEER_HEADS)
            def _(n):
                head_body(n // PEER_HEADS, n % PEER_HEADS)

            pltpu.sync_copy(idx_v, idx_hbm.at[pl.ds(tok, SC_TOKENS)])
            pltpu.sync_copy(gate_v, gate_hbm.at[pl.ds(tok, SC_TOKENS)])

    return body(scores, pair_a, pair_c, pair_pad)


def _sc_vaxpy(idx, w, tab, *, first, count):
    half = tab.shape[1]
    d = 2 * half
    per_worker = count // SC_WORKERS
    span = SC_COLS * SC_LANES

    @functools.partial(
        pl.kernel, mesh=_sc_mesh(),
        out_type=jax.ShapeDtypeStruct((count, d), _f32),
        scratch_types=[
            pltpu.VMEM((SC_TOKENS, PEER_SLOTS), jnp.int32),
            pltpu.VMEM((SC_TOKENS, PEER_SLOTS), _f32),
            pltpu.VMEM((SC_TOKENS, d), _f32),
            pltpu.VMEM((SC_GATHER, half), jnp.int32),
            pltpu.VMEM((SC_GATHER, half), jnp.int32),
            pltpu.SemaphoreType.DMA,
            pltpu.SemaphoreType.DMA,
        ],
        compiler_params=pltpu.CompilerParams(needs_layout_passes=False),
        name="sc_vaxpy",
    )
    def body(idx_hbm, w_hbm, tab_hbm, out_hbm, idx_v, w_v, out_v, rows0, rows1, sem0, sem1):
        base = _sc_worker_base(per_worker)

        def accumulate(i, part, rows):
            i_vec = jnp.full((SC_LANES,), i, jnp.int32)

            def span_body(cq, carry):
                def cols(c, offset=0):
                    return pl.ds(pl.multiple_of(offset + cq * span + c * SC_LANES, SC_LANES), SC_LANES)

                def expert_body(e, accs):
                    k_vec = jnp.full((SC_LANES,), part * SC_GATHER + e, jnp.int32)
                    wv = plsc.load_gather(w_v, [i_vec, k_vec])
                    new = []
                    for c in range(SC_COLS):
                        lo, hi = _sc_unpack(rows[e, cols(c)])
                        new += [accs[2 * c] + lo * wv, accs[2 * c + 1] + hi * wv]
                    return tuple(new)

                init = []
                for c in range(SC_COLS):
                    init += [out_v[i, cols(c)], out_v[i, cols(c, half)]]
                accs = lax.fori_loop(0, SC_GATHER, expert_body, tuple(init))
                for c in range(SC_COLS):
                    out_v[i, cols(c)] = accs[2 * c]
                    out_v[i, cols(c, half)] = accs[2 * c + 1]
                return carry

            lax.fori_loop(0, half // span, span_body, 0)

        @pl.loop(0, per_worker // SC_TOKENS)
        def _(step):
            off = pl.multiple_of(base + step * SC_TOKENS, SC_TOKENS)
            pltpu.sync_copy(idx_hbm.at[pl.ds(first + off, SC_TOKENS)], idx_v)
            pltpu.sync_copy(w_hbm.at[pl.ds(first + off, SC_TOKENS)], w_v)

            @pl.loop(0, SC_TOKENS)
            def _(i):
                @pl.loop(0, d // SC_LANES)
                def _(j):
                    out_v[i, pl.ds(pl.multiple_of(j * SC_LANES, SC_LANES), SC_LANES)] = (
                        jnp.zeros((SC_LANES,), _f32))

            _sc_gather_loop(tab_hbm, idx_v, ((rows0, sem0), (rows1, sem1)), accumulate)
            pltpu.sync_copy(out_v, out_hbm.at[pl.ds(off, SC_TOKENS)])

    return body(idx, w, tab)


ROW_CHUNKS = 8
PACK_ROWS = ROW_CHUNKS // 2


def _pack_table(tab):
    n, d = tab.shape
    bits = lax.bitcast_convert_type(tab.astype(_bf16), jnp.uint16).astype(jnp.uint32)
    bits = bits.reshape(n, PACK_ROWS, 2, d // ROW_CHUNKS)
    word = bits[:, :, 0, :] | (bits[:, :, 1, :] << 16)
    return lax.bitcast_convert_type(word, jnp.int32)


def _gate_weights_kernel(acts_ref, gate_ref, w_ref):
    a = acts_ref[...]
    gelu = 0.5 * a * (1.0 + lax.erf(a * (2.0 ** -0.5)))
    w_ref[...] = gate_ref[...] * gelu


def _gate_weights(acts, gate, *, tw):
    t, n = acts.shape
    spec = pl.BlockSpec((tw, n), lambda i: (i, 0))
    return pl.pallas_call(
        _gate_weights_kernel,
        grid=(t // tw,),
        in_specs=[spec, spec],
        out_specs=spec,
        out_shape=jax.ShapeDtypeStruct((t, n), _f32),
        compiler_params=pltpu.CompilerParams(dimension_semantics=("parallel",)),
        name="gate_weights",
    )(acts, gate)


def _gather_rows(idx_ref, t, tab_ref, rows_ref):
    for k in range(PEER_SLOTS):
        row = pl.multiple_of(idx_ref[t, k], PACK_ROWS)
        rows_ref[k * PACK_ROWS:(k + 1) * PACK_ROWS, :] = tab_ref[pl.ds(row, PACK_ROWS), :]


def _rows_matrix(rows_ref):
    return pltpu.bitcast(rows_ref[...], _bf16)


def _token_loop(tb, idx_ref, tab_ref, rows_a, rows_b, compute):
    _gather_rows(idx_ref, 0, tab_ref, rows_a)

    def pair_body(i, carry):
        t0 = 2 * i
        _gather_rows(idx_ref, t0 + 1, tab_ref, rows_b)
        compute(t0, _rows_matrix(rows_a))
        _gather_rows(idx_ref, jnp.minimum(t0 + 2, tb - 1), tab_ref, rows_a)
        compute(t0 + 1, _rows_matrix(rows_b))
        return carry

    lax.fori_loop(0, tb // 2, pair_body, 0)


def _chunk_diag_mask():
    shape = (ROW_CHUNKS, PEER_SLOTS * ROW_CHUNKS)
    return (_iota(shape, 1) % ROW_CHUNKS) == _iota(shape, 0)


def _vaxpy_kernel(idx_ref, w_ref, tab_ref, out_ref, rows_a, rows_b, wrep_ref):
    tb = out_ref.shape[0]
    diag = _chunk_diag_mask()
    shape = (PEER_SLOTS, PEER_SLOTS * ROW_CHUNKS)
    spread = (_iota(shape, 0) == (_iota(shape, 1) // ROW_CHUNKS)).astype(_bf16)
    wrep_ref[...] = _dot_exact_rhs01(w_ref[...], spread)

    def compute(t, m):
        w_row = jnp.broadcast_to(wrep_ref[pl.ds(t, 1), :], diag.shape)
        w_hi, w_lo = _split2(jnp.where(diag, w_row, 0.0))
        out_ref[t] = _dot(w_hi, m) + _dot(w_lo, m)

    _token_loop(tb, idx_ref, tab_ref, rows_a, rows_b, compute)


def _vaxpy(idx, w, tab, *, tb, count):
    return pl.pallas_call(
        _vaxpy_kernel,
        grid=(count // tb,),
        in_specs=[
            pl.BlockSpec((tb, PEER_SLOTS), lambda i: (i, 0), memory_space=pltpu.SMEM),
            pl.BlockSpec((tb, PEER_SLOTS), lambda i: (i, 0)),
            _const_spec(tab.shape),
        ],
        out_specs=pl.BlockSpec((tb, ROW_CHUNKS, 128), lambda i: (i, 0, 0)),
        out_shape=jax.ShapeDtypeStruct((count, ROW_CHUNKS, 128), _f32),
        scratch_shapes=[
            pltpu.VMEM((PEER_SLOTS * PACK_ROWS, 128), jnp.int32),
            pltpu.VMEM((PEER_SLOTS * PACK_ROWS, 128), jnp.int32),
            pltpu.VMEM((tb, PEER_SLOTS * ROW_CHUNKS), _f32),
        ],
        compiler_params=pltpu.CompilerParams(
            dimension_semantics=("parallel",),
            vmem_limit_bytes=VMEM_LIMIT_BYTES),
        name="vaxpy",
    )(idx, w, tab)


def _final_kernel(x_ref, p_ref, g_ref, out_ref, *, normalize):
    x = x_ref[...] + p_ref[...]
    if normalize:
        x = x * lax.rsqrt(jnp.mean(x * x, axis=-1, keepdims=True) + EPS) * g_ref[...]
    out_ref[...] = x


def _final(x, peer, g, *, normalize, tf, first):
    t, d = peer.shape
    off = first // tf
    return pl.pallas_call(
        functools.partial(_final_kernel, normalize=normalize),
        grid=(t // tf,),
        in_specs=[
            pl.BlockSpec((tf, d), lambda i: (i + off, 0)),
            pl.BlockSpec((tf, d), lambda i: (i, 0)),
            _const_spec((1, d)),
        ],
        out_specs=pl.BlockSpec((tf, d), lambda i: (i, 0)),
        out_shape=jax.ShapeDtypeStruct((t, d), _f32),
        compiler_params=pltpu.CompilerParams(dimension_semantics=("parallel",)),
        name="final_norm",
    )(x, peer, g)


def kernel(x, norm_mix_g, w_in, hg_lb_logits, hg_out_norm_g, conv_w, w_branch_hg, w_branch_conv, w_out, norm_ffn_g, peer_w_query, peer_keys1, peer_keys2, peer_u, peer_v, norm_final_g):
    b_, s_, d = x.shape
    depth = w_in.shape[0]
    lb_all = jnp.cumsum(jax.nn.softmax(hg_lb_logits.astype(_f32), axis=0), axis=0)
    n_groups = BATCH_GROUPS if b_ % BATCH_GROUPS == 0 else 1
    bg = b_ // n_groups
    tg = bg * s_
    n_sc = SC_SHARE if SC_SHARE < tg else 0
    n_tc = tg - n_sc
    for l in range(depth):
        wq = peer_w_query[l].astype(_bf16).reshape(d, PEER_HEADS, 2, PEER_HALF).transpose(1, 2, 0, 3)
        v_tab = _pack_table(peer_v[l])
        u_sc = _pack_halves(peer_u[l])
        v_sc = _pack_halves(peer_v[l])
        last = l == depth - 1
        g = norm_final_g[None] if last else jnp.ones((1, d), _f32)
        def front(c, x_in, sc_before=None):
            xc = _mix(x_in, norm_mix_g[l][None], w_in[l].astype(_bf16), lb_all[l][None],
                      hg_out_norm_g[l][None], conv_w[l], w_branch_hg[l].astype(_bf16),
                      w_branch_conv[l].astype(_bf16), w_out[l].astype(_bf16), ts=MIX_TILE)
            xf = xc.reshape(tg, d)
            h, scores = _scores(xf, norm_ffn_g[l][None], wq,
                                peer_keys1[l].astype(_bf16), peer_keys2[l].astype(_bf16), tr=ROUTE_TILE)
            idx, gate = _sc_topk(scores)
            idx_sc = idx if sc_before is None else lax.optimization_barrier((idx, sc_before))[0]
            return xf, idx, gate, _sc_udot(idx_sc, u_sc, h)

        def back(xf, idx, gate, acts):
            w = _gate_weights(acts, gate, tw=FINAL_TILE)
            peer_tc = _vaxpy(idx * PACK_ROWS, w, v_tab.reshape(-1, 128), tb=EXPERT_TILE, count=n_tc).reshape(n_tc, d)
            peer_sc = _sc_vaxpy(idx, w, v_sc, first=n_tc, count=n_sc) if n_sc else None
            return peer_tc, peer_sc

        groups = [x[c * bg:(c + 1) * bg] for c in range(n_groups)]
        fronts = {c: front(c, groups[c]) for c in range(min(GROUPS_AHEAD + 1, n_groups))}
        done, peer_tc = [], None
        for c in range(n_groups):
            xf, idx, gate, acts = fronts.pop(c)
            after = [fronts[c + GROUPS_AHEAD][1]] if c + GROUPS_AHEAD in fronts else []
            after += [peer_tc] if peer_tc is not None else []
            if after:
                acts = lax.optimization_barrier((acts, *after))[0]
            peer_tc, peer_sc = back(xf, idx, gate, acts)
            done.append((xf, peer_tc, peer_sc))
            nxt = c + GROUPS_AHEAD + 1
            if nxt < n_groups:
                fronts[nxt] = front(nxt, lax.optimization_barrier((groups[nxt], peer_tc))[0], peer_sc)
        outs = []
        for xf, p_tc, p_sc in done:
            outs.append(_final(xf, p_tc, g, normalize=last, tf=FINAL_TILE, first=0))
            if p_sc is not None:
                p_sc = lax.optimization_barrier((p_sc, peer_tc))[0]
                outs.append(_final(xf, p_sc, g, normalize=last, tf=FINAL_TILE, first=n_tc))
        x = jnp.concatenate(outs, axis=0).reshape(b_, s_, d)
    return x
```

```python
import functools

import jax
import jax.numpy as jnp
from jax import lax
from jax.experimental import pallas as pl
from jax.experimental.pallas import tpu as pltpu
from jax.experimental.pallas import tpu_sc as plsc

EPS = 1e-6
CHUNK = 64
SUB = 16
HEADS = 8
HEAD_DIM = 64
HG_WIDTH = HEADS * HEAD_DIM
GROUP = 256
N_GROUPS = HG_WIDTH // GROUP
CONV_K = 3
PEER_HEADS = 8
PEER_NKEYS = 128
PEER_HALF = 128
PEER_TOPK = 16
PEER_SLOTS = PEER_HEADS * PEER_TOPK

VMEM_LIMIT_BYTES = 56 * 1024 * 1024

MIX_TILE = 256
ROUTE_TILE = 256
EXPERT_TILE = 128
FINAL_TILE = 512
BATCH_GROUPS = 4
SC_SHARE = 1024
GROUPS_AHEAD = 3

_f32 = jnp.float32
_bf16 = jnp.bfloat16


def _dot(a, b):
    return jnp.dot(a, b, preferred_element_type=_f32)


def _dot_nt(a, b):
    return lax.dot_general(a, b, (((1,), (1,)), ((), ())), preferred_element_type=_f32)


def _dot_tn(a, b):
    return lax.dot_general(a, b, (((0,), (0,)), ((), ())), preferred_element_type=_f32)


def _split3(x):
    hi = x.astype(_bf16)
    r1 = x - hi.astype(_f32)
    mid = r1.astype(_bf16)
    lo = (r1 - mid.astype(_f32)).astype(_bf16)
    return hi, mid, lo


def _split2(x):
    hi = x.astype(_bf16)
    lo = (x - hi.astype(_f32)).astype(_bf16)
    return hi, lo


def _dot_exact_rhs01(x, m01):
    hi, mid, lo = _split3(x)
    return _dot(hi, m01) + _dot(mid, m01) + _dot(lo, m01)


def _dot_exact_lhs01(m01, x):
    hi, mid, lo = _split3(x)
    return _dot(m01, hi) + _dot(m01, mid) + _dot(m01, lo)


def _iota(shape, dim):
    return lax.broadcasted_iota(jnp.int32, shape, dim)


def _hgrn2_chunk(q, k, lf, v, state_ref):
    n_sub = CHUNK // SUB
    row = _iota((CHUNK, CHUNK), 0)
    col = _iota((CHUNK, CHUNK), 1)
    tril = (col <= row).astype(_bf16)
    b = _dot_exact_lhs01(tril, lf)

    b_end = [b[(j + 1) * SUB - 1:(j + 1) * SUB, :] for j in range(n_sub)]
    b_end_rows = jnp.concatenate([jnp.broadcast_to(e, (SUB, HG_WIDTH)) for e in b_end], axis=0)
    b_last = b_end[-1]

    q_in = (q * jnp.exp(b)).astype(_bf16)
    k_sub = (k * jnp.exp(b_end_rows - b)).astype(_bf16)
    k_out = (k * jnp.exp(b_last - b)).astype(_bf16)
    q_from = [(q * jnp.exp(jnp.minimum(b - b_end[j], 0.0))).astype(_bf16) for j in range(n_sub - 1)]
    v_b = v.astype(_bf16)

    gr = _iota((GROUP, GROUP), 0) // HEAD_DIM
    gc = _iota((GROUP, GROUP), 1) // HEAD_DIM
    head_mask = gr == gc
    t_blk = _iota((CHUNK, GROUP), 0) // SUB
    s_blk = (_iota((CHUNK, GROUP), 1) % HEAD_DIM) // SUB

    outs = []
    for g in range(N_GROUPS):
        sl = slice(g * GROUP, (g + 1) * GROUP)
        st = state_ref[g]
        o_g = _dot_nt(q_in[:, sl], st.astype(_bf16))

        zero_b = jnp.zeros((), _bf16)
        k_bd = jnp.where(head_mask, jnp.concatenate([k_sub[:, sl]] * (GROUP // CHUNK), axis=0), zero_b)
        v_bd = jnp.where(head_mask, jnp.concatenate([v_b[:, sl]] * (GROUP // CHUNK), axis=0), zero_b)
        q_stack = jnp.concatenate([qf[:, sl] for qf in q_from], axis=0)
        r = _dot_nt(q_stack, k_bd)
        scores = jnp.zeros((CHUNK, GROUP), _f32)
        for j in range(n_sub - 1):
            sel = (s_blk == j) & (t_blk > j)
            scores = jnp.where(sel, r[j * CHUNK:(j + 1) * CHUNK, :], scores)
        o_g = o_g + _dot(scores.astype(_bf16), v_bd)
        outs.append(o_g)

        upd = _dot_tn(v_b[:, sl], k_out[:, sl])
        decay = jnp.exp(b_last[:, sl])
        state_ref[g] = st * decay + jnp.where(head_mask, upd, 0.0)
    o = jnp.concatenate(outs, axis=1)

    ones_bd = ((_iota((HG_WIDTH, HG_WIDTH), 0) // HEAD_DIM)
               == (_iota((HG_WIDTH, HG_WIDTH), 1) // HEAD_DIM)).astype(_bf16)
    t_in_sub = _iota((CHUNK, HG_WIDTH), 0) % SUB
    for lag in range(SUB):
        if lag == 0:
            p = q * k
            v_l = v
        else:
            valid = t_in_sub >= lag
            k_l = pltpu.roll(k, lag, 0)
            b_l = pltpu.roll(b, lag, 0)
            v_l = pltpu.roll(v, lag, 0)
            p = jnp.where(valid, q * k_l * jnp.exp(jnp.minimum(b - b_l, 0.0)), 0.0)
        s_l = _dot(p.astype(_bf16), ones_bd)
        o = o + s_l * v_l
    return o


def _mix_kernel(x_ref, g_ref, win_ref, lb_ref, hgn_ref, convw_ref, pa_ref, pb_ref, wo_ref,
                out_ref, state_ref, carry_ref, q_s, k_s, lf_s, v_s, o_s):
    ts = x_ref.shape[0]
    d_model = x_ref.shape[1]
    w = HG_WIDTH

    @pl.when(pl.program_id(1) == 0)
    def _():
        state_ref[...] = jnp.zeros_like(state_ref)
        carry_ref[...] = jnp.zeros_like(carry_ref)

    x = x_ref[...]
    h = x * lax.rsqrt(jnp.mean(x * x, axis=-1, keepdims=True) + EPS) * g_ref[...]
    hb = h.astype(_bf16)

    def proj(i, width=w):
        return _dot(hb, win_ref[:, i * w:i * w + width])

    lb = lb_ref[...]
    q_s[...] = jax.nn.silu(proj(0)) * (HEAD_DIM ** -0.5)
    forget = lb + (1.0 - lb) * jax.nn.sigmoid(proj(1))
    k_s[...] = 1.0 - forget
    lf_s[...] = jnp.log(forget)
    v_s[...] = proj(2)

    def chunk_body(c, carry):
        rows = pl.ds(pl.multiple_of(c * CHUNK, CHUNK), CHUNK)
        o_s[rows, :] = _hgrn2_chunk(q_s[rows, :], k_s[rows, :], lf_s[rows, :], v_s[rows, :], state_ref)
        return carry

    lax.fori_loop(0, ts // CHUNK, chunk_body, 0)

    o = o_s[...]
    ones_bd = ((_iota((w, w), 0) // HEAD_DIM) == (_iota((w, w), 1) // HEAD_DIM)).astype(_bf16)
    ms = _dot_exact_rhs01(o * o, ones_bd) * (1.0 / HEAD_DIM)
    o = o * lax.rsqrt(ms + EPS) * hgn_ref[...]
    y_a = (o * jax.nn.silu(proj(3))).astype(_bf16)

    u = proj(5) * proj(6)
    prev = carry_ref[...]
    rowi = _iota((ts, w), 0)
    u1 = jnp.where(rowi >= 1, pltpu.roll(u, 1, 0), jnp.broadcast_to(prev[7:8, :], (ts, w)))
    u2 = jnp.where(rowi >= 2, pltpu.roll(u, 2, 0),
                   jnp.where(rowi == 1, jnp.broadcast_to(prev[7:8, :], (ts, w)),
                             jnp.broadcast_to(prev[6:7, :], (ts, w))))
    carry_ref[...] = u[ts - 8:, :]
    cw = convw_ref[...]
    y_b = (proj(4) * (cw[0:1, :] * u2 + cw[1:2, :] * u1 + cw[2:3, :] * u)).astype(_bf16)

    g_a = jax.nn.sigmoid(proj(7, d_model))
    g_b = jax.nn.sigmoid(_dot(hb, win_ref[:, 7 * w + d_model:7 * w + 2 * d_model]))
    merged = g_a * _dot(y_a, pa_ref[...]) + g_b * _dot(y_b, pb_ref[...])
    out_ref[...] = x + _dot(merged.astype(_bf16), wo_ref[...])


def _const_spec(shape):
    nd = len(shape)
    return pl.BlockSpec(shape, lambda *_: (0,) * nd, pipeline_mode=pl.Buffered(1))


def _mix(x, norm_g, w_in, lb, hg_norm_g, conv_w, w_a, w_b, w_o, *, ts):
    b_, s_, d = x.shape
    in_cols = w_in.shape[1]
    w = HG_WIDTH
    grid = (b_, s_ // ts)
    return pl.pallas_call(
        _mix_kernel,
        grid=grid,
        in_specs=[
            pl.BlockSpec((None, ts, d), lambda b, s: (b, s, 0)),
            _const_spec((1, d)),
            _const_spec((d, in_cols)),
            _const_spec((1, w)),
            _const_spec((1, w)),
            _const_spec((CONV_K, w)),
            _const_spec((w, d)),
            _const_spec((w, d)),
            _const_spec((d, d)),
        ],
        out_specs=pl.BlockSpec((None, ts, d), lambda b, s: (b, s, 0)),
        out_shape=jax.ShapeDtypeStruct((b_, s_, d), _f32),
        scratch_shapes=[
            pltpu.VMEM((N_GROUPS, GROUP, GROUP), _f32),
            pltpu.VMEM((8, w), _f32),
            pltpu.VMEM((ts, w), _f32),
            pltpu.VMEM((ts, w), _f32),
            pltpu.VMEM((ts, w), _f32),
            pltpu.VMEM((ts, w), _f32),
            pltpu.VMEM((ts, w), _f32),
        ],
        compiler_params=pltpu.CompilerParams(
            dimension_semantics=("parallel", "arbitrary"),
            vmem_limit_bytes=VMEM_LIMIT_BYTES),
        name="mix",
    )(x, norm_g, w_in, lb, hg_norm_g, conv_w, w_a, w_b, w_o)


def _stair_pairs():
    pairs = [(a, c) for a in range(PEER_TOPK) for c in range(PEER_TOPK) if (a + 1) * (c + 1) <= PEER_TOPK]
    rows = -(-len(pairs) // 8) * 8
    ranks = jnp.arange(PEER_TOPK)[None, :]
    a_col = jnp.asarray([a for a, _ in pairs] + [-1] * (rows - len(pairs)))[:, None]
    c_col = jnp.asarray([c for _, c in pairs] + [-1] * (rows - len(pairs)))[:, None]
    pad = jnp.where(a_col < 0, -jnp.inf, 0.0).astype(_f32)
    return (a_col == ranks).astype(_bf16), (c_col == ranks).astype(_bf16), pad


ROUTE_LANES = 128


def _route_kernel(x_ref, g_ref, wq_ref, k1_ref, k2_ref, sa_ref, sc_ref, pad_ref, h_ref, idx_ref, gate_ref,
                  idx_t, e_t, top_ref):
    tr = x_ref.shape[0]
    n_cand = sa_ref.shape[0]
    x = x_ref[...]
    h = x * lax.rsqrt(jnp.mean(x * x, axis=-1, keepdims=True) + EPS) * g_ref[...]
    h_ref[...] = h
    hb = h.astype(_bf16)

    key_row = _iota((PEER_NKEYS, ROUTE_LANES), 0).astype(_f32)
    cand_row = _iota((n_cand, tr), 0).astype(_f32)
    neg_inf = jnp.float32(-jnp.inf)

    def extract_max(s):
        m = jnp.max(s, axis=0, keepdims=True)
        i = jnp.min(jnp.where(s == m, key_row, float(PEER_NKEYS)), axis=0, keepdims=True)
        return m, i, jnp.where(key_row == i, neg_inf, s)

    def head_body(hd, carry):
        q1 = _dot(hb, wq_ref[hd, 0]).astype(_bf16)
        q2 = _dot(hb, wq_ref[hd, 1]).astype(_bf16)
        s1 = _dot_nt(k1_ref[hd], q1)
        s2 = _dot_nt(k2_ref[hd], q2)

        for lt in range(tr // ROUTE_LANES):
            lanes = slice(lt * ROUTE_LANES, (lt + 1) * ROUTE_LANES)

            def half_body(k, c):
                m1, i1, r1 = extract_max(c[0])
                m2, i2, r2 = extract_max(c[1])
                for j, row in enumerate((m1, i1, m2, i2)):
                    top_ref[lt, j, pl.ds(k, 1), :] = row
                return r1, r2

            lax.fori_loop(0, PEER_TOPK, half_body, (s1[:, lanes], s2[:, lanes]))

        def top(j):
            return jnp.concatenate([top_ref[lt, j] for lt in range(tr // ROUTE_LANES)], axis=1)

        sa = sa_ref[...]
        sc = sc_ref[...]
        cand_s = _dot_exact_lhs01(sa, top(0)) + _dot_exact_lhs01(sc, top(2)) + pad_ref[...]
        cand_i = (_dot(sa, top(1).astype(_bf16)) * float(PEER_NKEYS)
                  + _dot(sc, top(3).astype(_bf16)))

        def pick_body(k, c):
            cand_s, denom, m_first = c
            m = jnp.max(cand_s, axis=0, keepdims=True)
            pos = jnp.min(jnp.where(cand_s == m, cand_row, float(n_cand)), axis=0, keepdims=True)
            hit = cand_row == pos
            eid = jnp.max(jnp.where(hit, cand_i, -1.0), axis=0, keepdims=True)
            m_first = jnp.where(k == 0, m, m_first)
            e = jnp.exp(m - m_first)
            slot = hd * PEER_TOPK + k
            idx_t[pl.ds(slot, 1), :] = eid
            e_t[pl.ds(slot, 1), :] = e
            return jnp.where(hit, neg_inf, cand_s), denom + e, m_first

        zero_row = jnp.zeros((1, tr), _f32)
        _, denom, _ = lax.fori_loop(0, PEER_TOPK, pick_body, (cand_s, zero_row, zero_row))
        rows = pl.ds(pl.multiple_of(hd * PEER_TOPK, PEER_TOPK), PEER_TOPK)
        e_t[rows, :] = e_t[rows, :] / denom
        return carry

    lax.fori_loop(0, PEER_HEADS, head_body, 0)
    idx_ref[...] = idx_t[...].T.astype(jnp.int32)
    gate_ref[...] = e_t[...].T


def _route(x, norm_g, w_query, keys1, keys2, *, tr):
    t, d = x.shape
    sel_a, sel_c, pad = _stair_pairs()
    pad = jnp.broadcast_to(pad, (pad.shape[0], tr))
    return pl.pallas_call(
        _route_kernel,
        grid=(t // tr,),
        in_specs=[
            pl.BlockSpec((tr, d), lambda i: (i, 0)),
            _const_spec((1, d)),
            _const_spec(w_query.shape),
            _const_spec(keys1.shape),
            _const_spec(keys2.shape),
            _const_spec(sel_a.shape),
            _const_spec(sel_c.shape),
            _const_spec(pad.shape),
        ],
        out_specs=[
            pl.BlockSpec((tr, d), lambda i: (i, 0)),
            pl.BlockSpec((tr, PEER_SLOTS), lambda i: (i, 0)),
            pl.BlockSpec((tr, PEER_SLOTS), lambda i: (i, 0)),
        ],
        out_shape=[
            jax.ShapeDtypeStruct((t, d), _f32),
            jax.ShapeDtypeStruct((t, PEER_SLOTS), jnp.int32),
            jax.ShapeDtypeStruct((t, PEER_SLOTS), _f32),
        ],
        scratch_shapes=[
            pltpu.VMEM((PEER_SLOTS, tr), _f32),
            pltpu.VMEM((PEER_SLOTS, tr), _f32),
            pltpu.VMEM((tr // ROUTE_LANES, 4, PEER_TOPK, ROUTE_LANES), _f32),
        ],
        compiler_params=pltpu.CompilerParams(
            dimension_semantics=("parallel",),
            vmem_limit_bytes=VMEM_LIMIT_BYTES),
        name="route",
    )(x, norm_g, w_query, keys1, keys2, sel_a, sel_c, pad)


def _scores_kernel(x_ref, g_ref, wq_ref, k1_ref, k2_ref, h_ref, s_ref):
    x = x_ref[...]
    h = x * lax.rsqrt(jnp.mean(x * x, axis=-1, keepdims=True) + EPS) * g_ref[...]
    h_ref[...] = h
    hb = h.astype(_bf16)
    for hd in range(PEER_HEADS):
        for half, keys_ref in enumerate((k1_ref, k2_ref)):
            q = _dot(hb, wq_ref[hd, half]).astype(_bf16)
            col = (2 * hd + half) * PEER_NKEYS
            s_ref[:, col:col + PEER_NKEYS] = _dot_nt(q, keys_ref[hd])


def _scores(x, norm_g, w_query, keys1, keys2, *, tr):
    t, d = x.shape
    width = 2 * PEER_HEADS * PEER_NKEYS
    return pl.pallas_call(
        _scores_kernel,
        grid=(t // tr,),
        in_specs=[
            pl.BlockSpec((tr, d), lambda i: (i, 0)),
            _const_spec((1, d)),
            _const_spec(w_query.shape),
            _const_spec(keys1.shape),
            _const_spec(keys2.shape),
        ],
        out_specs=[
            pl.BlockSpec((tr, d), lambda i: (i, 0)),
            pl.BlockSpec((tr, width), lambda i: (i, 0)),
        ],
        out_shape=[
            jax.ShapeDtypeStruct((t, d), _f32),
            jax.ShapeDtypeStruct((t, width), _f32),
        ],
        compiler_params=pltpu.CompilerParams(
            dimension_semantics=("parallel",),
            vmem_limit_bytes=VMEM_LIMIT_BYTES),
        name="scores",
    )(x, norm_g, w_query, keys1, keys2)


SC_CORES = 2
SC_SUBCORES = 16
SC_LANES = 16
SC_WORKERS = SC_CORES * SC_SUBCORES
SC_TOKENS = 8
SC_GATHER = 32
SC_BLOCK = 8
SC_UNROLL = 2
SC_COLS = 8
HI_MASK = -65536


def _sc_mesh():
    return plsc.VectorSubcoreMesh(core_axis_name="c", subcore_axis_name="s")


def _sc_worker_base(per_worker):
    return (lax.axis_index("s") * SC_CORES + lax.axis_index("c")) * per_worker


def _sc_gather_loop(tab_hbm, idx_v, bufs, consume):
    n_parts = PEER_SLOTS // SC_GATHER
    n_gathers = SC_TOKENS * n_parts

    def gather(g, parity):
        rows, sem = bufs[parity]
        i = g // n_parts
        col = pl.multiple_of((g % n_parts) * SC_GATHER, SC_GATHER)
        return pltpu.make_async_copy(tab_hbm.at[idx_v.at[i, pl.ds(col, SC_GATHER)]], rows, sem)

    gather(0, 0).start()

    @pl.loop(0, n_gathers // 2)
    def _(pair):
        g = 2 * pair
        gather(g + 1, 1).start()
        gather(g, 0).wait()
        consume(g // n_parts, g % n_parts, bufs[0][0])

        @pl.when(g + 2 < n_gathers)
        def _():
            gather(g + 2, 0).start()

        gather(g + 1, 1).wait()
        consume((g + 1) // n_parts, (g + 1) % n_parts, bufs[1][0])


def _pack_halves(tab):
    half = tab.shape[1] // 2
    bits = lax.bitcast_convert_type(tab.astype(_bf16), jnp.uint16).astype(jnp.uint32)
    return lax.bitcast_convert_type(bits[:, :half] | (bits[:, half:] << 16), jnp.int32)


def _sc_unpack(words):
    lo = lax.bitcast_convert_type(words << 16, _f32)
    hi = lax.bitcast_convert_type(words & HI_MASK, _f32)
    return lo, hi


def _stair_vectors():
    pairs = [(a, c) for a in range(PEER_TOPK) for c in range(PEER_TOPK) if (a + 1) * (c + 1) <= PEER_TOPK]
    n = -(-len(pairs) // SC_LANES)
    fill = n * SC_LANES - len(pairs)
    a = jnp.asarray([p[0] for p in pairs] + [0] * fill, jnp.int32).reshape(n, SC_LANES)
    c = jnp.asarray([p[1] for p in pairs] + [0] * fill, jnp.int32).reshape(n, SC_LANES)
    pad = jnp.asarray([0.0] * len(pairs) + [-jnp.inf] * fill, _f32).reshape(n, SC_LANES)
    return a, c, pad


def _sc_route_dot(scores, tab, h):
    t, width = scores.shape
    d = h.shape[1]
    half_d = d // 2
    per_worker = t // SC_WORKERS
    pair_a, pair_c, pair_pad = _stair_vectors()
    n_cand = pair_a.shape[0]
    n_vec = PEER_NKEYS // SC_LANES

    @functools.partial(
        pl.kernel, mesh=_sc_mesh(),
        out_type=(jax.ShapeDtypeStruct((t, PEER_SLOTS), jnp.int32),
                  jax.ShapeDtypeStruct((t, PEER_SLOTS), _f32),
                  jax.ShapeDtypeStruct((t, PEER_SLOTS), _f32)),
        scratch_types=[
            pltpu.VMEM((SC_TOKENS, width), _f32),
            pltpu.VMEM((SC_TOKENS, PEER_SLOTS), jnp.int32),
            pltpu.VMEM((SC_TOKENS, PEER_SLOTS), _f32),
            pltpu.VMEM((n_cand, SC_LANES), jnp.int32),
            pltpu.VMEM((n_cand, SC_LANES), jnp.int32),
            pltpu.VMEM((n_cand, SC_LANES), _f32),
            pltpu.VMEM((4, SC_LANES), _f32),
            pltpu.VMEM((SC_TOKENS, d), _f32),
            pltpu.VMEM((SC_TOKENS, PEER_SLOTS), _f32),
            pltpu.VMEM((SC_GATHER, half_d), jnp.int32),
            pltpu.VMEM((SC_GATHER, half_d), jnp.int32),
            pltpu.SemaphoreType.DMA,
            pltpu.SemaphoreType.DMA,
        ],
        compiler_params=pltpu.CompilerParams(needs_layout_passes=False),
        name="sc_route_dot",
    )
    def body(s_hbm, a_hbm, c_hbm, pad_hbm, tab_hbm, h_hbm, idx_hbm, gate_hbm, acts_hbm,
             s_v, idx_v, gate_v, a_v, c_v, pad_v, top_v, h_v, acts_v, rows0, rows1, sem0, sem1):
        base = _sc_worker_base(per_worker)
        lane = lax.iota(jnp.int32, SC_LANES)
        pltpu.sync_copy(a_hbm, a_v)
        pltpu.sync_copy(c_hbm, c_v)
        pltpu.sync_copy(pad_hbm, pad_v)

        def dots(i, part, rows):
            def block_body(blk, carry):
                row0 = blk * SC_LANES
                outv = jnp.zeros((SC_LANES,), _f32)
                for sub in range(SC_LANES // SC_BLOCK):
                    def chunk_body(jj, accs):
                        accs = list(accs)
                        for u in range(SC_UNROLL):
                            off = pl.multiple_of((jj * SC_UNROLL + u) * SC_LANES, SC_LANES)
                            h_lo = h_v[i, pl.ds(off, SC_LANES)]
                            h_hi = h_v[i, pl.ds(half_d + off, SC_LANES)]
                            for e in range(SC_BLOCK):
                                lo, hi = _sc_unpack(rows[row0 + sub * SC_BLOCK + e, pl.ds(off, SC_LANES)])
                                accs[e] = accs[e] + (lo * h_lo + hi * h_hi)
                        return tuple(accs)

                    accs = lax.fori_loop(0, half_d // SC_LANES // SC_UNROLL, chunk_body,
                                         tuple(jnp.zeros((SC_LANES,), _f32) for _ in range(SC_BLOCK)))
                    for e in range(SC_BLOCK):
                        outv = jnp.where(lane == sub * SC_BLOCK + e, jnp.sum(accs[e]), outv)
                col = pl.multiple_of(part * SC_GATHER + row0, SC_LANES)
                acts_v[i, pl.ds(col, SC_LANES)] = outv
                return carry

            lax.fori_loop(0, SC_GATHER // SC_LANES, block_body, 0)

        def sort_desc(k, v):
            return plsc.sort_key_val(k, v, descending=True)

        def merge(x, y):
            yk, yv = lax.rev(y[0], (0,)), lax.rev(y[1], (0,))
            take = x[0] >= yk
            return sort_desc(jnp.where(take, x[0], yk), jnp.where(take, x[1], yv))

        def top_of(vectors):
            while len(vectors) > 1:
                vectors = [merge(vectors[j], vectors[j + 1]) for j in range(0, len(vectors), 2)]
            return vectors[0]

        def head_body(i, hd):
            halves = []
            for half in range(2):
                col = (2 * hd + half) * PEER_NKEYS
                vecs = [sort_desc(s_v[i, pl.ds(pl.multiple_of(col + j * SC_LANES, SC_LANES), SC_LANES)],
                                  lane + j * SC_LANES) for j in range(n_vec)]
                halves.append(top_of(vecs))
            (v1, i1), (v2, i2) = halves
            top_v[0, :] = v1
            top_v[1, :] = i1.astype(_f32)
            top_v[2, :] = v2
            top_v[3, :] = i2.astype(_f32)

            def pick(row, pos):
                return plsc.load_gather(top_v, [jnp.full((SC_LANES,), row, jnp.int32), pos])

            cands = []
            for q in range(n_cand):
                a, c = a_v[q, :], c_v[q, :]
                cs = pick(0, a) + pick(2, c) + pad_v[q, :]
                ci = pick(1, a) * float(PEER_NKEYS) + pick(3, c)
                cands.append(sort_desc(cs, ci))
            top_s, top_i = top_of(cands)
            e = jnp.exp(top_s - jnp.max(top_s))
            slots = pl.ds(pl.multiple_of(hd * PEER_TOPK, PEER_TOPK), PEER_TOPK)
            idx_v[i, slots] = top_i.astype(jnp.int32)
            gate_v[i, slots] = e / jnp.sum(e)

        @pl.loop(0, per_worker // SC_TOKENS)
        def _(step):
            tok = pl.multiple_of(base + step * SC_TOKENS, SC_TOKENS)
            pltpu.sync_copy(s_hbm.at[pl.ds(tok, SC_TOKENS)], s_v)
            pltpu.sync_copy(h_hbm.at[pl.ds(tok, SC_TOKENS)], h_v)

            @pl.loop(0, SC_TOKENS * PEER_HEADS)
            def _(n):
                head_body(n // PEER_HEADS, n % PEER_HEADS)

            pltpu.sync_copy(idx_v, idx_hbm.at[pl.ds(tok, SC_TOKENS)])
            pltpu.sync_copy(gate_v, gate_hbm.at[pl.ds(tok, SC_TOKENS)])
            _sc_gather_loop(tab_hbm, idx_v, ((rows0, sem0), (rows1, sem1)), dots)
            pltpu.sync_copy(acts_v, acts_hbm.at[pl.ds(tok, SC_TOKENS)])

    return body(scores, pair_a, pair_c, pair_pad, tab, h)


def _sc_vaxpy(idx, w, tab, *, first, count):
    half = tab.shape[1]
    d = 2 * half
    per_worker = count // SC_WORKERS
    span = SC_COLS * SC_LANES

    @functools.partial(
        pl.kernel, mesh=_sc_mesh(),
        out_type=jax.ShapeDtypeStruct((count, d), _f32),
        scratch_types=[
            pltpu.VMEM((SC_TOKENS, PEER_SLOTS), jnp.int32),
            pltpu.VMEM((SC_TOKENS, PEER_SLOTS), _f32),
            pltpu.VMEM((SC_TOKENS, d), _f32),
            pltpu.VMEM((SC_GATHER, half), jnp.int32),
            pltpu.VMEM((SC_GATHER, half), jnp.int32),
            pltpu.SemaphoreType.DMA,
            pltpu.SemaphoreType.DMA,
        ],
        compiler_params=pltpu.CompilerParams(needs_layout_passes=False),
        name="sc_vaxpy",
    )
    def body(idx_hbm, w_hbm, tab_hbm, out_hbm, idx_v, w_v, out_v, rows0, rows1, sem0, sem1):
        base = _sc_worker_base(per_worker)

        def accumulate(i, part, rows):
            i_vec = jnp.full((SC_LANES,), i, jnp.int32)

            def span_body(cq, carry):
                def cols(c, offset=0):
                    return pl.ds(pl.multiple_of(offset + cq * span + c * SC_LANES, SC_LANES), SC_LANES)

                def expert_body(e, accs):
                    k_vec = jnp.full((SC_LANES,), part * SC_GATHER + e, jnp.int32)
                    wv = plsc.load_gather(w_v, [i_vec, k_vec])
                    new = []
                    for c in range(SC_COLS):
                        lo, hi = _sc_unpack(rows[e, cols(c)])
                        new += [accs[2 * c] + lo * wv, accs[2 * c + 1] + hi * wv]
                    return tuple(new)

                init = []
                for c in range(SC_COLS):
                    init += [out_v[i, cols(c)], out_v[i, cols(c, half)]]
                accs = lax.fori_loop(0, SC_GATHER, expert_body, tuple(init))
                for c in range(SC_COLS):
                    out_v[i, cols(c)] = accs[2 * c]
                    out_v[i, cols(c, half)] = accs[2 * c + 1]
                return carry

            lax.fori_loop(0, half // span, span_body, 0)

        @pl.loop(0, per_worker // SC_TOKENS)
        def _(step):
            off = pl.multiple_of(base + step * SC_TOKENS, SC_TOKENS)
            pltpu.sync_copy(idx_hbm.at[pl.ds(first + off, SC_TOKENS)], idx_v)
            pltpu.sync_copy(w_hbm.at[pl.ds(first + off, SC_TOKENS)], w_v)

            @pl.loop(0, SC_TOKENS)
            def _(i):
                @pl.loop(0, d // SC_LANES)
                def _(j):
                    out_v[i, pl.ds(pl.multiple_of(j * SC_LANES, SC_LANES), SC_LANES)] = (
                        jnp.zeros((SC_LANES,), _f32))

            _sc_gather_loop(tab_hbm, idx_v, ((rows0, sem0), (rows1, sem1)), accumulate)
            pltpu.sync_copy(out_v, out_hbm.at[pl.ds(off, SC_TOKENS)])

    return body(idx, w, tab)


ROW_CHUNKS = 8
PACK_ROWS = ROW_CHUNKS // 2


def _pack_table(tab):
    n, d = tab.shape
    bits = lax.bitcast_convert_type(tab.astype(_bf16), jnp.uint16).astype(jnp.uint32)
    bits = bits.reshape(n, PACK_ROWS, 2, d // ROW_CHUNKS)
    word = bits[:, :, 0, :] | (bits[:, :, 1, :] << 16)
    return lax.bitcast_convert_type(word, jnp.int32)


def _gate_weights_kernel(acts_ref, gate_ref, w_ref):
    a = acts_ref[...]
    gelu = 0.5 * a * (1.0 + lax.erf(a * (2.0 ** -0.5)))
    w_ref[...] = gate_ref[...] * gelu


def _gate_weights(acts, gate, *, tw):
    t, n = acts.shape
    spec = pl.BlockSpec((tw, n), lambda i: (i, 0))
    return pl.pallas_call(
        _gate_weights_kernel,
        grid=(t // tw,),
        in_specs=[spec, spec],
        out_specs=spec,
        out_shape=jax.ShapeDtypeStruct((t, n), _f32),
        compiler_params=pltpu.CompilerParams(dimension_semantics=("parallel",)),
        name="gate_weights",
    )(acts, gate)


def _gather_rows(idx_ref, t, tab_ref, rows_ref):
    for k in range(PEER_SLOTS):
        row = pl.multiple_of(idx_ref[t, k], PACK_ROWS)
        rows_ref[k * PACK_ROWS:(k + 1) * PACK_ROWS, :] = tab_ref[pl.ds(row, PACK_ROWS), :]


def _rows_matrix(rows_ref):
    return pltpu.bitcast(rows_ref[...], _bf16)


def _token_loop(tb, idx_ref, tab_ref, rows_a, rows_b, compute):
    _gather_rows(idx_ref, 0, tab_ref, rows_a)

    def pair_body(i, carry):
        t0 = 2 * i
        _gather_rows(idx_ref, t0 + 1, tab_ref, rows_b)
        compute(t0, _rows_matrix(rows_a))
        _gather_rows(idx_ref, jnp.minimum(t0 + 2, tb - 1), tab_ref, rows_a)
        compute(t0 + 1, _rows_matrix(rows_b))
        return carry

    lax.fori_loop(0, tb // 2, pair_body, 0)


def _chunk_diag_mask():
    shape = (ROW_CHUNKS, PEER_SLOTS * ROW_CHUNKS)
    return (_iota(shape, 1) % ROW_CHUNKS) == _iota(shape, 0)


def _vaxpy_kernel(idx_ref, w_ref, tab_ref, out_ref, rows_a, rows_b, wrep_ref):
    tb = out_ref.shape[0]
    diag = _chunk_diag_mask()
    shape = (PEER_SLOTS, PEER_SLOTS * ROW_CHUNKS)
    spread = (_iota(shape, 0) == (_iota(shape, 1) // ROW_CHUNKS)).astype(_bf16)
    wrep_ref[...] = _dot_exact_rhs01(w_ref[...], spread)

    def compute(t, m):
        w_row = jnp.broadcast_to(wrep_ref[pl.ds(t, 1), :], diag.shape)
        w_hi, w_lo = _split2(jnp.where(diag, w_row, 0.0))
        out_ref[t] = _dot(w_hi, m) + _dot(w_lo, m)

    _token_loop(tb, idx_ref, tab_ref, rows_a, rows_b, compute)


def _vaxpy(idx, w, tab, *, tb, count):
    return pl.pallas_call(
        _vaxpy_kernel,
        grid=(count // tb,),
        in_specs=[
            pl.BlockSpec((tb, PEER_SLOTS), lambda i: (i, 0), memory_space=pltpu.SMEM),
            pl.BlockSpec((tb, PEER_SLOTS), lambda i: (i, 0)),
            _const_spec(tab.shape),
        ],
        out_specs=pl.BlockSpec((tb, ROW_CHUNKS, 128), lambda i: (i, 0, 0)),
        out_shape=jax.ShapeDtypeStruct((count, ROW_CHUNKS, 128), _f32),
        scratch_shapes=[
            pltpu.VMEM((PEER_SLOTS * PACK_ROWS, 128), jnp.int32),
            pltpu.VMEM((PEER_SLOTS * PACK_ROWS, 128), jnp.int32),
            pltpu.VMEM((tb, PEER_SLOTS * ROW_CHUNKS), _f32),
        ],
        compiler_params=pltpu.CompilerParams(
            dimension_semantics=("parallel",),
            vmem_limit_bytes=VMEM_LIMIT_BYTES),
        name="vaxpy",
    )(idx, w, tab)


def _final_kernel(x_ref, p_ref, g_ref, out_ref, *, normalize):
    x = x_ref[...] + p_ref[...]
    if normalize:
        x = x * lax.rsqrt(jnp.mean(x * x, axis=-1, keepdims=True) + EPS) * g_ref[...]
    out_ref[...] = x


def _final(x, peer, g, *, normalize, tf, first):
    t, d = peer.shape
    off = first // tf
    return pl.pallas_call(
        functools.partial(_final_kernel, normalize=normalize),
        grid=(t // tf,),
        in_specs=[
            pl.BlockSpec((tf, d), lambda i: (i + off, 0)),
            pl.BlockSpec((tf, d), lambda i: (i, 0)),
            _const_spec((1, d)),
        ],
        out_specs=pl.BlockSpec((tf, d), lambda i: (i, 0)),
        out_shape=jax.ShapeDtypeStruct((t, d), _f32),
        compiler_params=pltpu.CompilerParams(dimension_semantics=("parallel",)),
        name="final_norm",
    )(x, peer, g)


def kernel(x, norm_mix_g, w_in, hg_lb_logits, hg_out_norm_g, conv_w, w_branch_hg, w_branch_conv, w_out, norm_ffn_g, peer_w_query, peer_keys1, peer_keys2, peer_u, peer_v, norm_final_g):
    b_, s_, d = x.shape
    depth = w_in.shape[0]
    lb_all = jnp.cumsum(jax.nn.softmax(hg_lb_logits.astype(_f32), axis=0), axis=0)
    n_groups = BATCH_GROUPS if b_ % BATCH_GROUPS == 0 else 1
    bg = b_ // n_groups
    tg = bg * s_
    n_sc = SC_SHARE if SC_SHARE < tg else 0
    n_tc = tg - n_sc
    for l in range(depth):
        wq = peer_w_query[l].astype(_bf16).reshape(d, PEER_HEADS, 2, PEER_HALF).transpose(1, 2, 0, 3)
        v_tab = _pack_table(peer_v[l])
        u_sc = _pack_halves(peer_u[l])
        v_sc = _pack_halves(peer_v[l])
        last = l == depth - 1
        g = norm_final_g[None] if last else jnp.ones((1, d), _f32)
        def front(c, x_in, sc_before=None):
            xc = _mix(x_in, norm_mix_g[l][None], w_in[l].astype(_bf16), lb_all[l][None],
                      hg_out_norm_g[l][None], conv_w[l], w_branch_hg[l].astype(_bf16),
                      w_branch_conv[l].astype(_bf16), w_out[l].astype(_bf16), ts=MIX_TILE)
            xf = xc.reshape(tg, d)
            h, scores = _scores(xf, norm_ffn_g[l][None], wq,
                                peer_keys1[l].astype(_bf16), peer_keys2[l].astype(_bf16), tr=ROUTE_TILE)
            if sc_before is not None:
                scores = lax.optimization_barrier((scores, sc_before))[0]
            idx, gate, acts = _sc_route_dot(scores, u_sc, h)
            return (xf, idx, gate, acts), scores

        def back(xf, idx, gate, acts):
            w = _gate_weights(acts, gate, tw=FINAL_TILE)
            peer_tc = _vaxpy(idx * PACK_ROWS, w, v_tab.reshape(-1, 128), tb=EXPERT_TILE, count=n_tc).reshape(n_tc, d)
            peer_sc = _sc_vaxpy(idx, w, v_sc, first=n_tc, count=n_sc) if n_sc else None
            return peer_tc, peer_sc

        groups = [x[c * bg:(c + 1) * bg] for c in range(n_groups)]
        fronts = {c: front(c, groups[c]) for c in range(min(GROUPS_AHEAD + 1, n_groups))}
        done, peer_tc = [], None
        for c in range(n_groups):
            (xf, idx, gate, acts), _ = fronts.pop(c)
            after = [fronts[c + GROUPS_AHEAD][1]] if c + GROUPS_AHEAD in fronts else []
            after += [peer_tc] if peer_tc is not None else []
            if after:
                acts = lax.optimization_barrier((acts, *after))[0]
            peer_tc, peer_sc = back(xf, idx, gate, acts)
            done.append((xf, peer_tc, peer_sc))
            nxt = c + GROUPS_AHEAD + 1
            if nxt < n_groups:
                fronts[nxt] = front(nxt, lax.optimization_barrier((groups[nxt], peer_tc))[0], peer_sc)
        outs = []
        for xf, p_tc, p_sc in done:
            outs.append(_final(xf, p_tc, g, normalize=last, tf=FINAL_TILE, first=0))
            if p_sc is not None:
                p_sc = lax.optimization_barrier((p_sc, peer_tc))[0]
                outs.append(_final(xf, p_sc, g, normalize=last, tf=FINAL_TILE, first=n_tc))
        x = jnp.concatenate(outs, axis=0).reshape(b_, s_, d)
    return x
```

```python
import functools

import jax
import jax.numpy as jnp
from jax import lax
from jax.experimental import pallas as pl
from jax.experimental.pallas import tpu as pltpu
from jax.experimental.pallas import tpu_sc as plsc

EPS = 1e-6
CHUNK = 64
SUB = 16
HEADS = 8
HEAD_DIM = 64
HG_WIDTH = HEADS * HEAD_DIM
GROUP = 256
N_GROUPS = HG_WIDTH // GROUP
CONV_K = 3
PEER_HEADS = 8
PEER_NKEYS = 128
PEER_HALF = 128
PEER_TOPK = 16
PEER_SLOTS = PEER_HEADS * PEER_TOPK

VMEM_LIMIT_BYTES = 56 * 1024 * 1024

MIX_TILE = 256
ROUTE_TILE = 256
EXPERT_TILE = 128
FINAL_TILE = 512
BATCH_GROUPS = 8
SC_SHARE = 512
GROUPS_AHEAD = 3

_f32 = jnp.float32
_bf16 = jnp.bfloat16


def _dot(a, b):
    return jnp.dot(a, b, preferred_element_type=_f32)


def _dot_nt(a, b):
    return lax.dot_general(a, b, (((1,), (1,)), ((), ())), preferred_element_type=_f32)


def _dot_tn(a, b):
    return lax.dot_general(a, b, (((0,), (0,)), ((), ())), preferred_element_type=_f32)


def _split3(x):
    hi = x.astype(_bf16)
    r1 = x - hi.astype(_f32)
    mid = r1.astype(_bf16)
    lo = (r1 - mid.astype(_f32)).astype(_bf16)
    return hi, mid, lo


def _split2(x):
    hi = x.astype(_bf16)
    lo = (x - hi.astype(_f32)).astype(_bf16)
    return hi, lo


def _dot_exact_rhs01(x, m01):
    hi, mid, lo = _split3(x)
    return _dot(hi, m01) + _dot(mid, m01) + _dot(lo, m01)


def _dot_exact_lhs01(m01, x):
    hi, mid, lo = _split3(x)
    return _dot(m01, hi) + _dot(m01, mid) + _dot(m01, lo)


def _iota(shape, dim):
    return lax.broadcasted_iota(jnp.int32, shape, dim)


def _hgrn2_chunk(q, k, lf, v, state_ref):
    n_sub = CHUNK // SUB
    row = _iota((CHUNK, CHUNK), 0)
    col = _iota((CHUNK, CHUNK), 1)
    tril = (col <= row).astype(_bf16)
    b = _dot_exact_lhs01(tril, lf)

    b_end = [b[(j + 1) * SUB - 1:(j + 1) * SUB, :] for j in range(n_sub)]
    b_end_rows = jnp.concatenate([jnp.broadcast_to(e, (SUB, HG_WIDTH)) for e in b_end], axis=0)
    b_last = b_end[-1]

    q_in = (q * jnp.exp(b)).astype(_bf16)
    k_sub = (k * jnp.exp(b_end_rows - b)).astype(_bf16)
    k_out = (k * jnp.exp(b_last - b)).astype(_bf16)
    q_from = [(q * jnp.exp(jnp.minimum(b - b_end[j], 0.0))).astype(_bf16) for j in range(n_sub - 1)]
    v_b = v.astype(_bf16)

    gr = _iota((GROUP, GROUP), 0) // HEAD_DIM
    gc = _iota((GROUP, GROUP), 1) // HEAD_DIM
    head_mask = gr == gc
    t_blk = _iota((CHUNK, GROUP), 0) // SUB
    s_blk = (_iota((CHUNK, GROUP), 1) % HEAD_DIM) // SUB

    outs = []
    for g in range(N_GROUPS):
        sl = slice(g * GROUP, (g + 1) * GROUP)
        st = state_ref[g]
        o_g = _dot_nt(q_in[:, sl], st.astype(_bf16))

        zero_b = jnp.zeros((), _bf16)
        k_bd = jnp.where(head_mask, jnp.concatenate([k_sub[:, sl]] * (GROUP // CHUNK), axis=0), zero_b)
        v_bd = jnp.where(head_mask, jnp.concatenate([v_b[:, sl]] * (GROUP // CHUNK), axis=0), zero_b)
        q_stack = jnp.concatenate([qf[:, sl] for qf in q_from], axis=0)
        r = _dot_nt(q_stack, k_bd)
        scores = jnp.zeros((CHUNK, GROUP), _f32)
        for j in range(n_sub - 1):
            sel = (s_blk == j) & (t_blk > j)
            scores = jnp.where(sel, r[j * CHUNK:(j + 1) * CHUNK, :], scores)
        o_g = o_g + _dot(scores.astype(_bf16), v_bd)
        outs.append(o_g)

        upd = _dot_tn(v_b[:, sl], k_out[:, sl])
        decay = jnp.exp(b_last[:, sl])
        state_ref[g] = st * decay + jnp.where(head_mask, upd, 0.0)
    o = jnp.concatenate(outs, axis=1)

    ones_bd = ((_iota((HG_WIDTH, HG_WIDTH), 0) // HEAD_DIM)
               == (_iota((HG_WIDTH, HG_WIDTH), 1) // HEAD_DIM)).astype(_bf16)
    t_in_sub = _iota((CHUNK, HG_WIDTH), 0) % SUB
    for lag in range(SUB):
        if lag == 0:
            p = q * k
            v_l = v
        else:
            valid = t_in_sub >= lag
            k_l = pltpu.roll(k, lag, 0)
            b_l = pltpu.roll(b, lag, 0)
            v_l = pltpu.roll(v, lag, 0)
            p = jnp.where(valid, q * k_l * jnp.exp(jnp.minimum(b - b_l, 0.0)), 0.0)
        s_l = _dot(p.astype(_bf16), ones_bd)
        o = o + s_l * v_l
    return o


def _mix_kernel(x_ref, g_ref, win_ref, lb_ref, hgn_ref, convw_ref, pa_ref, pb_ref, wo_ref,
                out_ref, state_ref, carry_ref, q_s, k_s, lf_s, v_s, o_s):
    ts = x_ref.shape[0]
    d_model = x_ref.shape[1]
    w = HG_WIDTH

    @pl.when(pl.program_id(1) == 0)
    def _():
        state_ref[...] = jnp.zeros_like(state_ref)
        carry_ref[...] = jnp.zeros_like(carry_ref)

    x = x_ref[...]
    h = x * lax.rsqrt(jnp.mean(x * x, axis=-1, keepdims=True) + EPS) * g_ref[...]
    hb = h.astype(_bf16)

    def proj(i, width=w):
        return _dot(hb, win_ref[:, i * w:i * w + width])

    lb = lb_ref[...]
    q_s[...] = jax.nn.silu(proj(0)) * (HEAD_DIM ** -0.5)
    forget = lb + (1.0 - lb) * jax.nn.sigmoid(proj(1))
    k_s[...] = 1.0 - forget
    lf_s[...] = jnp.log(forget)
    v_s[...] = proj(2)

    def chunk_body(c, carry):
        rows = pl.ds(pl.multiple_of(c * CHUNK, CHUNK), CHUNK)
        o_s[rows, :] = _hgrn2_chunk(q_s[rows, :], k_s[rows, :], lf_s[rows, :], v_s[rows, :], state_ref)
        return carry

    lax.fori_loop(0, ts // CHUNK, chunk_body, 0)

    o = o_s[...]
    ones_bd = ((_iota((w, w), 0) // HEAD_DIM) == (_iota((w, w), 1) // HEAD_DIM)).astype(_bf16)
    ms = _dot_exact_rhs01(o * o, ones_bd) * (1.0 / HEAD_DIM)
    o = o * lax.rsqrt(ms + EPS) * hgn_ref[...]
    y_a = (o * jax.nn.silu(proj(3))).astype(_bf16)

    u = proj(5) * proj(6)
    prev = carry_ref[...]
    rowi = _iota((ts, w), 0)
    u1 = jnp.where(rowi >= 1, pltpu.roll(u, 1, 0), jnp.broadcast_to(prev[7:8, :], (ts, w)))
    u2 = jnp.where(rowi >= 2, pltpu.roll(u, 2, 0),
                   jnp.where(rowi == 1, jnp.broadcast_to(prev[7:8, :], (ts, w)),
                             jnp.broadcast_to(prev[6:7, :], (ts, w))))
    carry_ref[...] = u[ts - 8:, :]
    cw = convw_ref[...]
    y_b = (proj(4) * (cw[0:1, :] * u2 + cw[1:2, :] * u1 + cw[2:3, :] * u)).astype(_bf16)

    g_a = jax.nn.sigmoid(proj(7, d_model))
    g_b = jax.nn.sigmoid(_dot(hb, win_ref[:, 7 * w + d_model:7 * w + 2 * d_model]))
    merged = g_a * _dot(y_a, pa_ref[...]) + g_b * _dot(y_b, pb_ref[...])
    out_ref[...] = x + _dot(merged.astype(_bf16), wo_ref[...])


def _const_spec(shape):
    nd = len(shape)
    return pl.BlockSpec(shape, lambda *_: (0,) * nd, pipeline_mode=pl.Buffered(1))


def _mix(x, norm_g, w_in, lb, hg_norm_g, conv_w, w_a, w_b, w_o, *, ts):
    b_, s_, d = x.shape
    in_cols = w_in.shape[1]
    w = HG_WIDTH
    grid = (b_, s_ // ts)
    return pl.pallas_call(
        _mix_kernel,
        grid=grid,
        in_specs=[
            pl.BlockSpec((None, ts, d), lambda b, s: (b, s, 0)),
            _const_spec((1, d)),
            _const_spec((d, in_cols)),
            _const_spec((1, w)),
            _const_spec((1, w)),
            _const_spec((CONV_K, w)),
            _const_spec((w, d)),
            _const_spec((w, d)),
            _const_spec((d, d)),
        ],
        out_specs=pl.BlockSpec((None, ts, d), lambda b, s: (b, s, 0)),
        out_shape=jax.ShapeDtypeStruct((b_, s_, d), _f32),
        scratch_shapes=[
            pltpu.VMEM((N_GROUPS, GROUP, GROUP), _f32),
            pltpu.VMEM((8, w), _f32),
            pltpu.VMEM((ts, w), _f32),
            pltpu.VMEM((ts, w), _f32),
            pltpu.VMEM((ts, w), _f32),
            pltpu.VMEM((ts, w), _f32),
            pltpu.VMEM((ts, w), _f32),
        ],
        compiler_params=pltpu.CompilerParams(
            dimension_semantics=("parallel", "arbitrary"),
            vmem_limit_bytes=VMEM_LIMIT_BYTES),
        name="mix",
    )(x, norm_g, w_in, lb, hg_norm_g, conv_w, w_a, w_b, w_o)


def _stair_pairs():
    pairs = [(a, c) for a in range(PEER_TOPK) for c in range(PEER_TOPK) if (a + 1) * (c + 1) <= PEER_TOPK]
    rows = -(-len(pairs) // 8) * 8
    ranks = jnp.arange(PEER_TOPK)[None, :]
    a_col = jnp.asarray([a for a, _ in pairs] + [-1] * (rows - len(pairs)))[:, None]
    c_col = jnp.asarray([c for _, c in pairs] + [-1] * (rows - len(pairs)))[:, None]
    pad = jnp.where(a_col < 0, -jnp.inf, 0.0).astype(_f32)
    return (a_col == ranks).astype(_bf16), (c_col == ranks).astype(_bf16), pad


ROUTE_LANES = 128


def _route_kernel(x_ref, g_ref, wq_ref, k1_ref, k2_ref, sa_ref, sc_ref, pad_ref, h_ref, idx_ref, gate_ref,
                  idx_t, e_t, top_ref):
    tr = x_ref.shape[0]
    n_cand = sa_ref.shape[0]
    x = x_ref[...]
    h = x * lax.rsqrt(jnp.mean(x * x, axis=-1, keepdims=True) + EPS) * g_ref[...]
    h_ref[...] = h
    hb = h.astype(_bf16)

    key_row = _iota((PEER_NKEYS, ROUTE_LANES), 0).astype(_f32)
    cand_row = _iota((n_cand, tr), 0).astype(_f32)
    neg_inf = jnp.float32(-jnp.inf)

    def extract_max(s):
        m = jnp.max(s, axis=0, keepdims=True)
        i = jnp.min(jnp.where(s == m, key_row, float(PEER_NKEYS)), axis=0, keepdims=True)
        return m, i, jnp.where(key_row == i, neg_inf, s)

    def head_body(hd, carry):
        q1 = _dot(hb, wq_ref[hd, 0]).astype(_bf16)
        q2 = _dot(hb, wq_ref[hd, 1]).astype(_bf16)
        s1 = _dot_nt(k1_ref[hd], q1)
        s2 = _dot_nt(k2_ref[hd], q2)

        for lt in range(tr // ROUTE_LANES):
            lanes = slice(lt * ROUTE_LANES, (lt + 1) * ROUTE_LANES)

            def half_body(k, c):
                m1, i1, r1 = extract_max(c[0])
                m2, i2, r2 = extract_max(c[1])
                for j, row in enumerate((m1, i1, m2, i2)):
                    top_ref[lt, j, pl.ds(k, 1), :] = row
                return r1, r2

            lax.fori_loop(0, PEER_TOPK, half_body, (s1[:, lanes], s2[:, lanes]))

        def top(j):
            return jnp.concatenate([top_ref[lt, j] for lt in range(tr // ROUTE_LANES)], axis=1)

        sa = sa_ref[...]
        sc = sc_ref[...]
        cand_s = _dot_exact_lhs01(sa, top(0)) + _dot_exact_lhs01(sc, top(2)) + pad_ref[...]
        cand_i = (_dot(sa, top(1).astype(_bf16)) * float(PEER_NKEYS)
                  + _dot(sc, top(3).astype(_bf16)))

        def pick_body(k, c):
            cand_s, denom, m_first = c
            m = jnp.max(cand_s, axis=0, keepdims=True)
            pos = jnp.min(jnp.where(cand_s == m, cand_row, float(n_cand)), axis=0, keepdims=True)
            hit = cand_row == pos
            eid = jnp.max(jnp.where(hit, cand_i, -1.0), axis=0, keepdims=True)
            m_first = jnp.where(k == 0, m, m_first)
            e = jnp.exp(m - m_first)
            slot = hd * PEER_TOPK + k
            idx_t[pl.ds(slot, 1), :] = eid
            e_t[pl.ds(slot, 1), :] = e
            return jnp.where(hit, neg_inf, cand_s), denom + e, m_first

        zero_row = jnp.zeros((1, tr), _f32)
        _, denom, _ = lax.fori_loop(0, PEER_TOPK, pick_body, (cand_s, zero_row, zero_row))
        rows = pl.ds(pl.multiple_of(hd * PEER_TOPK, PEER_TOPK), PEER_TOPK)
        e_t[rows, :] = e_t[rows, :] / denom
        return carry

    lax.fori_loop(0, PEER_HEADS, head_body, 0)
    idx_ref[...] = idx_t[...].T.astype(jnp.int32)
    gate_ref[...] = e_t[...].T


def _route(x, norm_g, w_query, keys1, keys2, *, tr):
    t, d = x.shape
    sel_a, sel_c, pad = _stair_pairs()
    pad = jnp.broadcast_to(pad, (pad.shape[0], tr))
    return pl.pallas_call(
        _route_kernel,
        grid=(t // tr,),
        in_specs=[
            pl.BlockSpec((tr, d), lambda i: (i, 0)),
            _const_spec((1, d)),
            _const_spec(w_query.shape),
            _const_spec(keys1.shape),
            _const_spec(keys2.shape),
            _const_spec(sel_a.shape),
            _const_spec(sel_c.shape),
            _const_spec(pad.shape),
        ],
        out_specs=[
            pl.BlockSpec((tr, d), lambda i: (i, 0)),
            pl.BlockSpec((tr, PEER_SLOTS), lambda i: (i, 0)),
            pl.BlockSpec((tr, PEER_SLOTS), lambda i: (i, 0)),
        ],
        out_shape=[
            jax.ShapeDtypeStruct((t, d), _f32),
            jax.ShapeDtypeStruct((t, PEER_SLOTS), jnp.int32),
            jax.ShapeDtypeStruct((t, PEER_SLOTS), _f32),
        ],
        scratch_shapes=[
            pltpu.VMEM((PEER_SLOTS, tr), _f32),
            pltpu.VMEM((PEER_SLOTS, tr), _f32),
            pltpu.VMEM((tr // ROUTE_LANES, 4, PEER_TOPK, ROUTE_LANES), _f32),
        ],
        compiler_params=pltpu.CompilerParams(
            dimension_semantics=("parallel",),
            vmem_limit_bytes=VMEM_LIMIT_BYTES),
        name="route",
    )(x, norm_g, w_query, keys1, keys2, sel_a, sel_c, pad)


def _scores_kernel(x_ref, g_ref, wq_ref, k1_ref, k2_ref, h_ref, s_ref):
    x = x_ref[...]
    h = x * lax.rsqrt(jnp.mean(x * x, axis=-1, keepdims=True) + EPS) * g_ref[...]
    h_ref[...] = h
    hb = h.astype(_bf16)
    for hd in range(PEER_HEADS):
        for half, keys_ref in enumerate((k1_ref, k2_ref)):
            q = _dot(hb, wq_ref[hd, half]).astype(_bf16)
            col = (2 * hd + half) * PEER_NKEYS
            s_ref[:, col:col + PEER_NKEYS] = _dot_nt(q, keys_ref[hd])


def _scores(x, norm_g, w_query, keys1, keys2, *, tr):
    t, d = x.shape
    width = 2 * PEER_HEADS * PEER_NKEYS
    return pl.pallas_call(
        _scores_kernel,
        grid=(t // tr,),
        in_specs=[
            pl.BlockSpec((tr, d), lambda i: (i, 0)),
            _const_spec((1, d)),
            _const_spec(w_query.shape),
            _const_spec(keys1.shape),
            _const_spec(keys2.shape),
        ],
        out_specs=[
            pl.BlockSpec((tr, d), lambda i: (i, 0)),
            pl.BlockSpec((tr, width), lambda i: (i, 0)),
        ],
        out_shape=[
            jax.ShapeDtypeStruct((t, d), _f32),
            jax.ShapeDtypeStruct((t, width), _f32),
        ],
        compiler_params=pltpu.CompilerParams(
            dimension_semantics=("parallel",),
            vmem_limit_bytes=VMEM_LIMIT_BYTES),
        name="scores",
    )(x, norm_g, w_query, keys1, keys2)


SC_CORES = 2
SC_SUBCORES = 16
SC_LANES = 16
SC_WORKERS = SC_CORES * SC_SUBCORES
SC_TOKENS = 8
SC_GATHER = 32
SC_BLOCK = 8
SC_UNROLL = 2
SC_COLS = 8
HI_MASK = -65536


def _sc_mesh():
    return plsc.VectorSubcoreMesh(core_axis_name="c", subcore_axis_name="s")


def _sc_worker_base(per_worker):
    return (lax.axis_index("s") * SC_CORES + lax.axis_index("c")) * per_worker


def _sc_gather_loop(tab_hbm, idx_v, bufs, consume):
    n_parts = PEER_SLOTS // SC_GATHER
    n_gathers = SC_TOKENS * n_parts

    def gather(g, parity):
        rows, sem = bufs[parity]
        i = g // n_parts
        col = pl.multiple_of((g % n_parts) * SC_GATHER, SC_GATHER)
        return pltpu.make_async_copy(tab_hbm.at[idx_v.at[i, pl.ds(col, SC_GATHER)]], rows, sem)

    gather(0, 0).start()

    @pl.loop(0, n_gathers // 2)
    def _(pair):
        g = 2 * pair
        gather(g + 1, 1).start()
        gather(g, 0).wait()
        consume(g // n_parts, g % n_parts, bufs[0][0])

        @pl.when(g + 2 < n_gathers)
        def _():
            gather(g + 2, 0).start()

        gather(g + 1, 1).wait()
        consume((g + 1) // n_parts, (g + 1) % n_parts, bufs[1][0])


def _pack_halves(tab):
    half = tab.shape[1] // 2
    bits = lax.bitcast_convert_type(tab.astype(_bf16), jnp.uint16).astype(jnp.uint32)
    return lax.bitcast_convert_type(bits[:, :half] | (bits[:, half:] << 16), jnp.int32)


def _sc_unpack(words):
    lo = lax.bitcast_convert_type(words << 16, _f32)
    hi = lax.bitcast_convert_type(words & HI_MASK, _f32)
    return lo, hi


def _stair_vectors():
    pairs = [(a, c) for a in range(PEER_TOPK) for c in range(PEER_TOPK) if (a + 1) * (c + 1) <= PEER_TOPK]
    n = -(-len(pairs) // SC_LANES)
    fill = n * SC_LANES - len(pairs)
    a = jnp.asarray([p[0] for p in pairs] + [0] * fill, jnp.int32).reshape(n, SC_LANES)
    c = jnp.asarray([p[1] for p in pairs] + [0] * fill, jnp.int32).reshape(n, SC_LANES)
    pad = jnp.asarray([0.0] * len(pairs) + [-jnp.inf] * fill, _f32).reshape(n, SC_LANES)
    return a, c, pad


def _sc_route_dot(scores, tab, h):
    t, width = scores.shape
    d = h.shape[1]
    half_d = d // 2
    per_worker = t // SC_WORKERS
    pair_a, pair_c, pair_pad = _stair_vectors()
    n_cand = pair_a.shape[0]
    n_vec = PEER_NKEYS // SC_LANES

    @functools.partial(
        pl.kernel, mesh=_sc_mesh(),
        out_type=(jax.ShapeDtypeStruct((t, PEER_SLOTS), jnp.int32),
                  jax.ShapeDtypeStruct((t, PEER_SLOTS), _f32),
                  jax.ShapeDtypeStruct((t, PEER_SLOTS), _f32)),
        scratch_types=[
            pltpu.VMEM((SC_TOKENS, width), _f32),
            pltpu.VMEM((SC_TOKENS, PEER_SLOTS), jnp.int32),
            pltpu.VMEM((SC_TOKENS, PEER_SLOTS), _f32),
            pltpu.VMEM((n_cand, SC_LANES), jnp.int32),
            pltpu.VMEM((n_cand, SC_LANES), jnp.int32),
            pltpu.VMEM((n_cand, SC_LANES), _f32),
            pltpu.VMEM((4, SC_LANES), _f32),
            pltpu.VMEM((SC_TOKENS, d), _f32),
            pltpu.VMEM((SC_TOKENS, PEER_SLOTS), _f32),
            pltpu.VMEM((SC_GATHER, half_d), jnp.int32),
            pltpu.VMEM((SC_GATHER, half_d), jnp.int32),
            pltpu.SemaphoreType.DMA,
            pltpu.SemaphoreType.DMA,
        ],
        compiler_params=pltpu.CompilerParams(needs_layout_passes=False),
        name="sc_route_dot",
    )
    def body(s_hbm, a_hbm, c_hbm, pad_hbm, tab_hbm, h_hbm, idx_hbm, gate_hbm, acts_hbm,
             s_v, idx_v, gate_v, a_v, c_v, pad_v, top_v, h_v, acts_v, rows0, rows1, sem0, sem1):
        base = _sc_worker_base(per_worker)
        lane = lax.iota(jnp.int32, SC_LANES)
        pltpu.sync_copy(a_hbm, a_v)
        pltpu.sync_copy(c_hbm, c_v)
        pltpu.sync_copy(pad_hbm, pad_v)

        def dots(i, part, rows):
            def block_body(blk, carry):
                row0 = blk * SC_LANES
                outv = jnp.zeros((SC_LANES,), _f32)
                for sub in range(SC_LANES // SC_BLOCK):
                    def chunk_body(jj, accs):
                        accs = list(accs)
                        for u in range(SC_UNROLL):
                            off = pl.multiple_of((jj * SC_UNROLL + u) * SC_LANES, SC_LANES)
                            h_lo = h_v[i, pl.ds(off, SC_LANES)]
                            h_hi = h_v[i, pl.ds(half_d + off, SC_LANES)]
                            for e in range(SC_BLOCK):
                                lo, hi = _sc_unpack(rows[row0 + sub * SC_BLOCK + e, pl.ds(off, SC_LANES)])
                                accs[e] = accs[e] + (lo * h_lo + hi * h_hi)
                        return tuple(accs)

                    accs = lax.fori_loop(0, half_d // SC_LANES // SC_UNROLL, chunk_body,
                                         tuple(jnp.zeros((SC_LANES,), _f32) for _ in range(SC_BLOCK)))
                    for e in range(SC_BLOCK):
                        outv = jnp.where(lane == sub * SC_BLOCK + e, jnp.sum(accs[e]), outv)
                col = pl.multiple_of(part * SC_GATHER + row0, SC_LANES)
                acts_v[i, pl.ds(col, SC_LANES)] = outv
                return carry

            lax.fori_loop(0, SC_GATHER // SC_LANES, block_body, 0)

        def sort_desc(k, v):
            return plsc.sort_key_val(k, v, descending=True)

        def merge(x, y):
            yk, yv = lax.rev(y[0], (0,)), lax.rev(y[1], (0,))
            take = x[0] >= yk
            return sort_desc(jnp.where(take, x[0], yk), jnp.where(take, x[1], yv))

        def top_of(vectors):
            while len(vectors) > 1:
                vectors = [merge(vectors[j], vectors[j + 1]) for j in range(0, len(vectors), 2)]
            return vectors[0]

        def head_body(i, hd):
            halves = []
            for half in range(2):
                col = (2 * hd + half) * PEER_NKEYS
                vecs = [sort_desc(s_v[i, pl.ds(pl.multiple_of(col + j * SC_LANES, SC_LANES), SC_LANES)],
                                  lane + j * SC_LANES) for j in range(n_vec)]
                halves.append(top_of(vecs))
            (v1, i1), (v2, i2) = halves
            top_v[0, :] = v1
            top_v[1, :] = i1.astype(_f32)
            top_v[2, :] = v2
            top_v[3, :] = i2.astype(_f32)

            def pick(row, pos):
                return plsc.load_gather(top_v, [jnp.full((SC_LANES,), row, jnp.int32), pos])

            cands = []
            for q in range(n_cand):
                a, c = a_v[q, :], c_v[q, :]
                cs = pick(0, a) + pick(2, c) + pad_v[q, :]
                ci = pick(1, a) * float(PEER_NKEYS) + pick(3, c)
                cands.append(sort_desc(cs, ci))
            top_s, top_i = top_of(cands)
            e = jnp.exp(top_s - jnp.max(top_s))
            slots = pl.ds(pl.multiple_of(hd * PEER_TOPK, PEER_TOPK), PEER_TOPK)
            idx_v[i, slots] = top_i.astype(jnp.int32)
            gate_v[i, slots] = e / jnp.sum(e)

        @pl.loop(0, per_worker // SC_TOKENS)
        def _(step):
            tok = pl.multiple_of(base + step * SC_TOKENS, SC_TOKENS)
            pltpu.sync_copy(s_hbm.at[pl.ds(tok, SC_TOKENS)], s_v)
            pltpu.sync_copy(h_hbm.at[pl.ds(tok, SC_TOKENS)], h_v)

            @pl.loop(0, SC_TOKENS * PEER_HEADS)
            def _(n):
                head_body(n // PEER_HEADS, n % PEER_HEADS)

            pltpu.sync_copy(idx_v, idx_hbm.at[pl.ds(tok, SC_TOKENS)])
            pltpu.sync_copy(gate_v, gate_hbm.at[pl.ds(tok, SC_TOKENS)])
            _sc_gather_loop(tab_hbm, idx_v, ((rows0, sem0), (rows1, sem1)), dots)
            pltpu.sync_copy(acts_v, acts_hbm.at[pl.ds(tok, SC_TOKENS)])

    return body(scores, pair_a, pair_c, pair_pad, tab, h)


def _sc_vaxpy(idx, w, tab, *, first, count):
    half = tab.shape[1]
    d = 2 * half
    per_worker = count // SC_WORKERS
    span = SC_COLS * SC_LANES

    @functools.partial(
        pl.kernel, mesh=_sc_mesh(),
        out_type=jax.ShapeDtypeStruct((count, d), _f32),
        scratch_types=[
            pltpu.VMEM((SC_TOKENS, PEER_SLOTS), jnp.int32),
            pltpu.VMEM((SC_TOKENS, PEER_SLOTS), _f32),
            pltpu.VMEM((SC_TOKENS, d), _f32),
            pltpu.VMEM((SC_GATHER, half), jnp.int32),
            pltpu.VMEM((SC_GATHER, half), jnp.int32),
            pltpu.SemaphoreType.DMA,
            pltpu.SemaphoreType.DMA,
        ],
        compiler_params=pltpu.CompilerParams(needs_layout_passes=False),
        name="sc_vaxpy",
    )
    def body(idx_hbm, w_hbm, tab_hbm, out_hbm, idx_v, w_v, out_v, rows0, rows1, sem0, sem1):
        base = _sc_worker_base(per_worker)

        def accumulate(i, part, rows):
            i_vec = jnp.full((SC_LANES,), i, jnp.int32)

            def span_body(cq, carry):
                def cols(c, offset=0):
                    return pl.ds(pl.multiple_of(offset + cq * span + c * SC_LANES, SC_LANES), SC_LANES)

                def expert_body(e, accs):
                    k_vec = jnp.full((SC_LANES,), part * SC_GATHER + e, jnp.int32)
                    wv = plsc.load_gather(w_v, [i_vec, k_vec])
                    new = []
                    for c in range(SC_COLS):
                        lo, hi = _sc_unpack(rows[e, cols(c)])
                        new += [accs[2 * c] + lo * wv, accs[2 * c + 1] + hi * wv]
                    return tuple(new)

                init = []
                for c in range(SC_COLS):
                    init += [out_v[i, cols(c)], out_v[i, cols(c, half)]]
                accs = lax.fori_loop(0, SC_GATHER, expert_body, tuple(init))
                for c in range(SC_COLS):
                    out_v[i, cols(c)] = accs[2 * c]
                    out_v[i, cols(c, half)] = accs[2 * c + 1]
                return carry

            lax.fori_loop(0, half // span, span_body, 0)

        @pl.loop(0, per_worker // SC_TOKENS)
        def _(step):
            off = pl.multiple_of(base + step * SC_TOKENS, SC_TOKENS)
            pltpu.sync_copy(idx_hbm.at[pl.ds(first + off, SC_TOKENS)], idx_v)
            pltpu.sync_copy(w_hbm.at[pl.ds(first + off, SC_TOKENS)], w_v)

            @pl.loop(0, SC_TOKENS)
            def _(i):
                @pl.loop(0, d // SC_LANES)
                def _(j):
                    out_v[i, pl.ds(pl.multiple_of(j * SC_LANES, SC_LANES), SC_LANES)] = (
                        jnp.zeros((SC_LANES,), _f32))

            _sc_gather_loop(tab_hbm, idx_v, ((rows0, sem0), (rows1, sem1)), accumulate)
            pltpu.sync_copy(out_v, out_hbm.at[pl.ds(off, SC_TOKENS)])

    return body(idx, w, tab)


ROW_CHUNKS = 8
PACK_ROWS = ROW_CHUNKS // 2


def _pack_table(tab):
    n, d = tab.shape
    bits = lax.bitcast_convert_type(tab.astype(_bf16), jnp.uint16).astype(jnp.uint32)
    bits = bits.reshape(n, PACK_ROWS, 2, d // ROW_CHUNKS)
    word = bits[:, :, 0, :] | (bits[:, :, 1, :] << 16)
    return lax.bitcast_convert_type(word, jnp.int32)


def _gate_weights_kernel(acts_ref, gate_ref, w_ref):
    a = acts_ref[...]
    gelu = 0.5 * a * (1.0 + lax.erf(a * (2.0 ** -0.5)))
    w_ref[...] = gate_ref[...] * gelu


def _gate_weights(acts, gate, *, tw):
    t, n = acts.shape
    spec = pl.BlockSpec((tw, n), lambda i: (i, 0))
    return pl.pallas_call(
        _gate_weights_kernel,
        grid=(t // tw,),
        in_specs=[spec, spec],
        out_specs=spec,
        out_shape=jax.ShapeDtypeStruct((t, n), _f32),
        compiler_params=pltpu.CompilerParams(dimension_semantics=("parallel",)),
        name="gate_weights",
    )(acts, gate)


def _gather_rows(idx_ref, t, tab_ref, rows_ref):
    for k in range(PEER_SLOTS):
        row = pl.multiple_of(idx_ref[t, k], PACK_ROWS)
        rows_ref[k * PACK_ROWS:(k + 1) * PACK_ROWS, :] = tab_ref[pl.ds(row, PACK_ROWS), :]


def _rows_matrix(rows_ref):
    return pltpu.bitcast(rows_ref[...], _bf16)


def _token_loop(tb, idx_ref, tab_ref, rows_a, rows_b, compute):
    _gather_rows(idx_ref, 0, tab_ref, rows_a)

    def pair_body(i, carry):
        t0 = 2 * i
        _gather_rows(idx_ref, t0 + 1, tab_ref, rows_b)
        compute(t0, _rows_matrix(rows_a))
        _gather_rows(idx_ref, jnp.minimum(t0 + 2, tb - 1), tab_ref, rows_a)
        compute(t0 + 1, _rows_matrix(rows_b))
        return carry

    lax.fori_loop(0, tb // 2, pair_body, 0)


def _chunk_diag_mask():
    shape = (ROW_CHUNKS, PEER_SLOTS * ROW_CHUNKS)
    return (_iota(shape, 1) % ROW_CHUNKS) == _iota(shape, 0)


def _vaxpy_kernel(idx_ref, w_ref, tab_ref, out_ref, rows_a, rows_b, wrep_ref):
    tb = out_ref.shape[0]
    diag = _chunk_diag_mask()
    shape = (PEER_SLOTS, PEER_SLOTS * ROW_CHUNKS)
    spread = (_iota(shape, 0) == (_iota(shape, 1) // ROW_CHUNKS)).astype(_bf16)
    wrep_ref[...] = _dot_exact_rhs01(w_ref[...], spread)

    def compute(t, m):
        w_row = jnp.broadcast_to(wrep_ref[pl.ds(t, 1), :], diag.shape)
        w_hi, w_lo = _split2(jnp.where(diag, w_row, 0.0))
        out_ref[t] = _dot(w_hi, m) + _dot(w_lo, m)

    _token_loop(tb, idx_ref, tab_ref, rows_a, rows_b, compute)


def _vaxpy(idx, w, tab, *, tb, count):
    return pl.pallas_call(
        _vaxpy_kernel,
        grid=(count // tb,),
        in_specs=[
            pl.BlockSpec((tb, PEER_SLOTS), lambda i: (i, 0), memory_space=pltpu.SMEM),
            pl.BlockSpec((tb, PEER_SLOTS), lambda i: (i, 0)),
            _const_spec(tab.shape),
        ],
        out_specs=pl.BlockSpec((tb, ROW_CHUNKS, 128), lambda i: (i, 0, 0)),
        out_shape=jax.ShapeDtypeStruct((count, ROW_CHUNKS, 128), _f32),
        scratch_shapes=[
            pltpu.VMEM((PEER_SLOTS * PACK_ROWS, 128), jnp.int32),
            pltpu.VMEM((PEER_SLOTS * PACK_ROWS, 128), jnp.int32),
            pltpu.VMEM((tb, PEER_SLOTS * ROW_CHUNKS), _f32),
        ],
        compiler_params=pltpu.CompilerParams(
            dimension_semantics=("parallel",),
            vmem_limit_bytes=VMEM_LIMIT_BYTES),
        name="vaxpy",
    )(idx, w, tab)


def _final_kernel(x_ref, p_ref, g_ref, out_ref, *, normalize):
    x = x_ref[...] + p_ref[...]
    if normalize:
        x = x * lax.rsqrt(jnp.mean(x * x, axis=-1, keepdims=True) + EPS) * g_ref[...]
    out_ref[...] = x


def _final(x, peer, g, *, normalize, tf, first):
    t, d = peer.shape
    off = first // tf
    return pl.pallas_call(
        functools.partial(_final_kernel, normalize=normalize),
        grid=(t // tf,),
        in_specs=[
            pl.BlockSpec((tf, d), lambda i: (i + off, 0)),
            pl.BlockSpec((tf, d), lambda i: (i, 0)),
            _const_spec((1, d)),
        ],
        out_specs=pl.BlockSpec((tf, d), lambda i: (i, 0)),
        out_shape=jax.ShapeDtypeStruct((t, d), _f32),
        compiler_params=pltpu.CompilerParams(dimension_semantics=("parallel",)),
        name="final_norm",
    )(x, peer, g)


def kernel(x, norm_mix_g, w_in, hg_lb_logits, hg_out_norm_g, conv_w, w_branch_hg, w_branch_conv, w_out, norm_ffn_g, peer_w_query, peer_keys1, peer_keys2, peer_u, peer_v, norm_final_g):
    b_, s_, d = x.shape
    depth = w_in.shape[0]
    lb_all = jnp.cumsum(jax.nn.softmax(hg_lb_logits.astype(_f32), axis=0), axis=0)
    n_groups = BATCH_GROUPS if b_ % BATCH_GROUPS == 0 else 1
    bg = b_ // n_groups
    tg = bg * s_
    n_sc = SC_SHARE if SC_SHARE < tg else 0
    n_tc = tg - n_sc
    for l in range(depth):
        wq = peer_w_query[l].astype(_bf16).reshape(d, PEER_HEADS, 2, PEER_HALF).transpose(1, 2, 0, 3)
        v_tab = _pack_table(peer_v[l])
        u_sc = _pack_halves(peer_u[l])
        v_sc = _pack_halves(peer_v[l])
        last = l == depth - 1
        g = norm_final_g[None] if last else jnp.ones((1, d), _f32)
        def front(c, x_in, sc_before=None):
            xc = _mix(x_in, norm_mix_g[l][None], w_in[l].astype(_bf16), lb_all[l][None],
                      hg_out_norm_g[l][None], conv_w[l], w_branch_hg[l].astype(_bf16),
                      w_branch_conv[l].astype(_bf16), w_out[l].astype(_bf16), ts=MIX_TILE)
            xf = xc.reshape(tg, d)
            h, scores = _scores(xf, norm_ffn_g[l][None], wq,
                                peer_keys1[l].astype(_bf16), peer_keys2[l].astype(_bf16), tr=ROUTE_TILE)
            if sc_before is not None:
                scores = lax.optimization_barrier((scores, sc_before))[0]
            idx, gate, acts = _sc_route_dot(scores, u_sc, h)
            return (xf, idx, gate, acts), scores

        def back(xf, idx, gate, acts):
            w = _gate_weights(acts, gate, tw=FINAL_TILE)
            peer_tc = _vaxpy(idx * PACK_ROWS, w, v_tab.reshape(-1, 128), tb=EXPERT_TILE, count=n_tc).reshape(n_tc, d)
            peer_sc = _sc_vaxpy(idx, w, v_sc, first=n_tc, count=n_sc) if n_sc else None
            return peer_tc, peer_sc

        groups = [x[c * bg:(c + 1) * bg] for c in range(n_groups)]
        fronts = {c: front(c, groups[c]) for c in range(min(GROUPS_AHEAD + 1, n_groups))}
        done, peer_tc = [], None
        for c in range(n_groups):
            (xf, idx, gate, acts), _ = fronts.pop(c)
            after = [fronts[c + GROUPS_AHEAD][1]] if c + GROUPS_AHEAD in fronts else []
            after += [peer_tc] if peer_tc is not None else []
            if after:
                acts = lax.optimization_barrier((acts, *after))[0]
            peer_tc, peer_sc = back(xf, idx, gate, acts)
            done.append((xf, peer_tc, peer_sc))
            nxt = c + GROUPS_AHEAD + 1
            if nxt < n_groups:
                fronts[nxt] = front(nxt, lax.optimization_barrier((groups[nxt], peer_tc))[0], peer_sc)
        outs = []
        for xf, p_tc, p_sc in done:
            outs.append(_final(xf, p_tc, g, normalize=last, tf=FINAL_TILE, first=0))
            if p_sc is not None:
                p_sc = lax.optimization_barrier((p_sc, peer_tc))[0]
                outs.append(_final(xf, p_sc, g, normalize=last, tf=FINAL_TILE, first=n_tc))
        x = jnp.concatenate(outs, axis=0).reshape(b_, s_, d)
    return x
```

```python
import functools

import jax
import jax.numpy as jnp
from jax import lax
from jax.experimental import pallas as pl
from jax.experimental.pallas import tpu as pltpu
from jax.experimental.pallas import tpu_sc as plsc

EPS = 1e-6
CHUNK = 64
SUB = 16
HEADS = 8
HEAD_DIM = 64
HG_WIDTH = HEADS * HEAD_DIM
GROUP = 256
N_GROUPS = HG_WIDTH // GROUP
CONV_K = 3
PEER_HEADS = 8
PEER_NKEYS = 128
PEER_HALF = 128
PEER_TOPK = 16
PEER_SLOTS = PEER_HEADS * PEER_TOPK

VMEM_LIMIT_BYTES = 56 * 1024 * 1024

MIX_TILE = 256
ROUTE_TILE = 256
EXPERT_TILE = 128
FINAL_TILE = 512
BATCH_GROUPS = 8
SC_SHARE = 512
GROUPS_AHEAD = 7

_f32 = jnp.float32
_bf16 = jnp.bfloat16


def _dot(a, b):
    return jnp.dot(a, b, preferred_element_type=_f32)


def _dot_nt(a, b):
    return lax.dot_general(a, b, (((1,), (1,)), ((), ())), preferred_element_type=_f32)


def _dot_tn(a, b):
    return lax.dot_general(a, b, (((0,), (0,)), ((), ())), preferred_element_type=_f32)


def _split3(x):
    hi = x.astype(_bf16)
    r1 = x - hi.astype(_f32)
    mid = r1.astype(_bf16)
    lo = (r1 - mid.astype(_f32)).astype(_bf16)
    return hi, mid, lo


def _split2(x):
    hi = x.astype(_bf16)
    lo = (x - hi.astype(_f32)).astype(_bf16)
    return hi, lo


def _dot_exact_rhs01(x, m01):
    hi, mid, lo = _split3(x)
    return _dot(hi, m01) + _dot(mid, m01) + _dot(lo, m01)


def _dot_exact_lhs01(m01, x):
    hi, mid, lo = _split3(x)
    return _dot(m01, hi) + _dot(m01, mid) + _dot(m01, lo)


def _iota(shape, dim):
    return lax.broadcasted_iota(jnp.int32, shape, dim)


def _hgrn2_chunk(q, k, lf, v, state_ref):
    n_sub = CHUNK // SUB
    row = _iota((CHUNK, CHUNK), 0)
    col = _iota((CHUNK, CHUNK), 1)
    tril = (col <= row).astype(_bf16)
    b = _dot_exact_lhs01(tril, lf)

    b_end = [b[(j + 1) * SUB - 1:(j + 1) * SUB, :] for j in range(n_sub)]
    b_end_rows = jnp.concatenate([jnp.broadcast_to(e, (SUB, HG_WIDTH)) for e in b_end], axis=0)
    b_last = b_end[-1]

    q_in = (q * jnp.exp(b)).astype(_bf16)
    k_sub = (k * jnp.exp(b_end_rows - b)).astype(_bf16)
    k_out = (k * jnp.exp(b_last - b)).astype(_bf16)
    q_from = [(q * jnp.exp(jnp.minimum(b - b_end[j], 0.0))).astype(_bf16) for j in range(n_sub - 1)]
    v_b = v.astype(_bf16)

    gr = _iota((GROUP, GROUP), 0) // HEAD_DIM
    gc = _iota((GROUP, GROUP), 1) // HEAD_DIM
    head_mask = gr == gc
    t_blk = _iota((CHUNK, GROUP), 0) // SUB
    s_blk = (_iota((CHUNK, GROUP), 1) % HEAD_DIM) // SUB

    outs = []
    for g in range(N_GROUPS):
        sl = slice(g * GROUP, (g + 1) * GROUP)
        st = state_ref[g]
        o_g = _dot_nt(q_in[:, sl], st.astype(_bf16))

        zero_b = jnp.zeros((), _bf16)
        k_bd = jnp.where(head_mask, jnp.concatenate([k_sub[:, sl]] * (GROUP // CHUNK), axis=0), zero_b)
        v_bd = jnp.where(head_mask, jnp.concatenate([v_b[:, sl]] * (GROUP // CHUNK), axis=0), zero_b)
        q_stack = jnp.concatenate([qf[:, sl] for qf in q_from], axis=0)
        r = _dot_nt(q_stack, k_bd)
        scores = jnp.zeros((CHUNK, GROUP), _f32)
        for j in range(n_sub - 1):
            sel = (s_blk == j) & (t_blk > j)
            scores = jnp.where(sel, r[j * CHUNK:(j + 1) * CHUNK, :], scores)
        o_g = o_g + _dot(scores.astype(_bf16), v_bd)
        outs.append(o_g)

        upd = _dot_tn(v_b[:, sl], k_out[:, sl])
        decay = jnp.exp(b_last[:, sl])
        state_ref[g] = st * decay + jnp.where(head_mask, upd, 0.0)
    o = jnp.concatenate(outs, axis=1)

    ones_bd = ((_iota((HG_WIDTH, HG_WIDTH), 0) // HEAD_DIM)
               == (_iota((HG_WIDTH, HG_WIDTH), 1) // HEAD_DIM)).astype(_bf16)
    t_in_sub = _iota((CHUNK, HG_WIDTH), 0) % SUB
    for lag in range(SUB):
        if lag == 0:
            p = q * k
            v_l = v
        else:
            valid = t_in_sub >= lag
            k_l = pltpu.roll(k, lag, 0)
            b_l = pltpu.roll(b, lag, 0)
            v_l = pltpu.roll(v, lag, 0)
            p = jnp.where(valid, q * k_l * jnp.exp(jnp.minimum(b - b_l, 0.0)), 0.0)
        s_l = _dot(p.astype(_bf16), ones_bd)
        o = o + s_l * v_l
    return o


def _mix_kernel(x_ref, g_ref, win_ref, lb_ref, hgn_ref, convw_ref, pa_ref, pb_ref, wo_ref,
                out_ref, state_ref, carry_ref, q_s, k_s, lf_s, v_s, o_s):
    ts = x_ref.shape[0]
    d_model = x_ref.shape[1]
    w = HG_WIDTH

    @pl.when(pl.program_id(1) == 0)
    def _():
        state_ref[...] = jnp.zeros_like(state_ref)
        carry_ref[...] = jnp.zeros_like(carry_ref)

    x = x_ref[...]
    h = x * lax.rsqrt(jnp.mean(x * x, axis=-1, keepdims=True) + EPS) * g_ref[...]
    hb = h.astype(_bf16)

    def proj(i, width=w):
        return _dot(hb, win_ref[:, i * w:i * w + width])

    lb = lb_ref[...]
    q_s[...] = jax.nn.silu(proj(0)) * (HEAD_DIM ** -0.5)
    forget = lb + (1.0 - lb) * jax.nn.sigmoid(proj(1))
    k_s[...] = 1.0 - forget
    lf_s[...] = jnp.log(forget)
    v_s[...] = proj(2)

    def chunk_body(c, carry):
        rows = pl.ds(pl.multiple_of(c * CHUNK, CHUNK), CHUNK)
        o_s[rows, :] = _hgrn2_chunk(q_s[rows, :], k_s[rows, :], lf_s[rows, :], v_s[rows, :], state_ref)
        return carry

    lax.fori_loop(0, ts // CHUNK, chunk_body, 0)

    o = o_s[...]
    ones_bd = ((_iota((w, w), 0) // HEAD_DIM) == (_iota((w, w), 1) // HEAD_DIM)).astype(_bf16)
    ms = _dot_exact_rhs01(o * o, ones_bd) * (1.0 / HEAD_DIM)
    o = o * lax.rsqrt(ms + EPS) * hgn_ref[...]
    y_a = (o * jax.nn.silu(proj(3))).astype(_bf16)

    u = proj(5) * proj(6)
    prev = carry_ref[...]
    rowi = _iota((ts, w), 0)
    u1 = jnp.where(rowi >= 1, pltpu.roll(u, 1, 0), jnp.broadcast_to(prev[7:8, :], (ts, w)))
    u2 = jnp.where(rowi >= 2, pltpu.roll(u, 2, 0),
                   jnp.where(rowi == 1, jnp.broadcast_to(prev[7:8, :], (ts, w)),
                             jnp.broadcast_to(prev[6:7, :], (ts, w))))
    carry_ref[...] = u[ts - 8:, :]
    cw = convw_ref[...]
    y_b = (proj(4) * (cw[0:1, :] * u2 + cw[1:2, :] * u1 + cw[2:3, :] * u)).astype(_bf16)

    g_a = jax.nn.sigmoid(proj(7, d_model))
    g_b = jax.nn.sigmoid(_dot(hb, win_ref[:, 7 * w + d_model:7 * w + 2 * d_model]))
    merged = g_a * _dot(y_a, pa_ref[...]) + g_b * _dot(y_b, pb_ref[...])
    out_ref[...] = x + _dot(merged.astype(_bf16), wo_ref[...])


def _const_spec(shape):
    nd = len(shape)
    return pl.BlockSpec(shape, lambda *_: (0,) * nd, pipeline_mode=pl.Buffered(1))


def _mix(x, norm_g, w_in, lb, hg_norm_g, conv_w, w_a, w_b, w_o, *, ts):
    b_, s_, d = x.shape
    in_cols = w_in.shape[1]
    w = HG_WIDTH
    grid = (b_, s_ // ts)
    return pl.pallas_call(
        _mix_kernel,
        grid=grid,
        in_specs=[
            pl.BlockSpec((None, ts, d), lambda b, s: (b, s, 0)),
            _const_spec((1, d)),
            _const_spec((d, in_cols)),
            _const_spec((1, w)),
            _const_spec((1, w)),
            _const_spec((CONV_K, w)),
            _const_spec((w, d)),
            _const_spec((w, d)),
            _const_spec((d, d)),
        ],
        out_specs=pl.BlockSpec((None, ts, d), lambda b, s: (b, s, 0)),
        out_shape=jax.ShapeDtypeStruct((b_, s_, d), _f32),
        scratch_shapes=[
            pltpu.VMEM((N_GROUPS, GROUP, GROUP), _f32),
            pltpu.VMEM((8, w), _f32),
            pltpu.VMEM((ts, w), _f32),
            pltpu.VMEM((ts, w), _f32),
            pltpu.VMEM((ts, w), _f32),
            pltpu.VMEM((ts, w), _f32),
            pltpu.VMEM((ts, w), _f32),
        ],
        compiler_params=pltpu.CompilerParams(
            dimension_semantics=("parallel", "arbitrary"),
            vmem_limit_bytes=VMEM_LIMIT_BYTES),
        name="mix",
    )(x, norm_g, w_in, lb, hg_norm_g, conv_w, w_a, w_b, w_o)


def _stair_pairs():
    pairs = [(a, c) for a in range(PEER_TOPK) for c in range(PEER_TOPK) if (a + 1) * (c + 1) <= PEER_TOPK]
    rows = -(-len(pairs) // 8) * 8
    ranks = jnp.arange(PEER_TOPK)[None, :]
    a_col = jnp.asarray([a for a, _ in pairs] + [-1] * (rows - len(pairs)))[:, None]
    c_col = jnp.asarray([c for _, c in pairs] + [-1] * (rows - len(pairs)))[:, None]
    pad = jnp.where(a_col < 0, -jnp.inf, 0.0).astype(_f32)
    return (a_col == ranks).astype(_bf16), (c_col == ranks).astype(_bf16), pad


ROUTE_LANES = 128


def _route_kernel(x_ref, g_ref, wq_ref, k1_ref, k2_ref, sa_ref, sc_ref, pad_ref, h_ref, idx_ref, gate_ref,
                  idx_t, e_t, top_ref):
    tr = x_ref.shape[0]
    n_cand = sa_ref.shape[0]
    x = x_ref[...]
    h = x * lax.rsqrt(jnp.mean(x * x, axis=-1, keepdims=True) + EPS) * g_ref[...]
    h_ref[...] = h
    hb = h.astype(_bf16)

    key_row = _iota((PEER_NKEYS, ROUTE_LANES), 0).astype(_f32)
    cand_row = _iota((n_cand, tr), 0).astype(_f32)
    neg_inf = jnp.float32(-jnp.inf)

    def extract_max(s):
        m = jnp.max(s, axis=0, keepdims=True)
        i = jnp.min(jnp.where(s == m, key_row, float(PEER_NKEYS)), axis=0, keepdims=True)
        return m, i, jnp.where(key_row == i, neg_inf, s)

    def head_body(hd, carry):
        q1 = _dot(hb, wq_ref[hd, 0]).astype(_bf16)
        q2 = _dot(hb, wq_ref[hd, 1]).astype(_bf16)
        s1 = _dot_nt(k1_ref[hd], q1)
        s2 = _dot_nt(k2_ref[hd], q2)

        for lt in range(tr // ROUTE_LANES):
            lanes = slice(lt * ROUTE_LANES, (lt + 1) * ROUTE_LANES)

            def half_body(k, c):
                m1, i1, r1 = extract_max(c[0])
                m2, i2, r2 = extract_max(c[1])
                for j, row in enumerate((m1, i1, m2, i2)):
                    top_ref[lt, j, pl.ds(k, 1), :] = row
                return r1, r2

            lax.fori_loop(0, PEER_TOPK, half_body, (s1[:, lanes], s2[:, lanes]))

        def top(j):
            return jnp.concatenate([top_ref[lt, j] for lt in range(tr // ROUTE_LANES)], axis=1)

        sa = sa_ref[...]
        sc = sc_ref[...]
        cand_s = _dot_exact_lhs01(sa, top(0)) + _dot_exact_lhs01(sc, top(2)) + pad_ref[...]
        cand_i = (_dot(sa, top(1).astype(_bf16)) * float(PEER_NKEYS)
                  + _dot(sc, top(3).astype(_bf16)))

        def pick_body(k, c):
            cand_s, denom, m_first = c
            m = jnp.max(cand_s, axis=0, keepdims=True)
            pos = jnp.min(jnp.where(cand_s == m, cand_row, float(n_cand)), axis=0, keepdims=True)
            hit = cand_row == pos
            eid = jnp.max(jnp.where(hit, cand_i, -1.0), axis=0, keepdims=True)
            m_first = jnp.where(k == 0, m, m_first)
            e = jnp.exp(m - m_first)
            slot = hd * PEER_TOPK + k
            idx_t[pl.ds(slot, 1), :] = eid
            e_t[pl.ds(slot, 1), :] = e
            return jnp.where(hit, neg_inf, cand_s), denom + e, m_first

        zero_row = jnp.zeros((1, tr), _f32)
        _, denom, _ = lax.fori_loop(0, PEER_TOPK, pick_body, (cand_s, zero_row, zero_row))
        rows = pl.ds(pl.multiple_of(hd * PEER_TOPK, PEER_TOPK), PEER_TOPK)
        e_t[rows, :] = e_t[rows, :] / denom
        return carry

    lax.fori_loop(0, PEER_HEADS, head_body, 0)
    idx_ref[...] = idx_t[...].T.astype(jnp.int32)
    gate_ref[...] = e_t[...].T


def _route(x, norm_g, w_query, keys1, keys2, *, tr):
    t, d = x.shape
    sel_a, sel_c, pad = _stair_pairs()
    pad = jnp.broadcast_to(pad, (pad.shape[0], tr))
    return pl.pallas_call(
        _route_kernel,
        grid=(t // tr,),
        in_specs=[
            pl.BlockSpec((tr, d), lambda i: (i, 0)),
            _const_spec((1, d)),
            _const_spec(w_query.shape),
            _const_spec(keys1.shape),
            _const_spec(keys2.shape),
            _const_spec(sel_a.shape),
            _const_spec(sel_c.shape),
            _const_spec(pad.shape),
        ],
        out_specs=[
            pl.BlockSpec((tr, d), lambda i: (i, 0)),
            pl.BlockSpec((tr, PEER_SLOTS), lambda i: (i, 0)),
            pl.BlockSpec((tr, PEER_SLOTS), lambda i: (i, 0)),
        ],
        out_shape=[
            jax.ShapeDtypeStruct((t, d), _f32),
            jax.ShapeDtypeStruct((t, PEER_SLOTS), jnp.int32),
            jax.ShapeDtypeStruct((t, PEER_SLOTS), _f32),
        ],
        scratch_shapes=[
            pltpu.VMEM((PEER_SLOTS, tr), _f32),
            pltpu.VMEM((PEER_SLOTS, tr), _f32),
            pltpu.VMEM((tr // ROUTE_LANES, 4, PEER_TOPK, ROUTE_LANES), _f32),
        ],
        compiler_params=pltpu.CompilerParams(
            dimension_semantics=("parallel",),
            vmem_limit_bytes=VMEM_LIMIT_BYTES),
        name="route",
    )(x, norm_g, w_query, keys1, keys2, sel_a, sel_c, pad)


def _scores_kernel(x_ref, g_ref, wq_ref, k1_ref, k2_ref, h_ref, s_ref):
    x = x_ref[...]
    h = x * lax.rsqrt(jnp.mean(x * x, axis=-1, keepdims=True) + EPS) * g_ref[...]
    h_ref[...] = h
    hb = h.astype(_bf16)
    for hd in range(PEER_HEADS):
        for half, keys_ref in enumerate((k1_ref, k2_ref)):
            q = _dot(hb, wq_ref[hd, half]).astype(_bf16)
            col = (2 * hd + half) * PEER_NKEYS
            s_ref[:, col:col + PEER_NKEYS] = _dot_nt(q, keys_ref[hd])


def _scores(x, norm_g, w_query, keys1, keys2, *, tr):
    t, d = x.shape
    width = 2 * PEER_HEADS * PEER_NKEYS
    return pl.pallas_call(
        _scores_kernel,
        grid=(t // tr,),
        in_specs=[
            pl.BlockSpec((tr, d), lambda i: (i, 0)),
            _const_spec((1, d)),
            _const_spec(w_query.shape),
            _const_spec(keys1.shape),
            _const_spec(keys2.shape),
        ],
        out_specs=[
            pl.BlockSpec((tr, d), lambda i: (i, 0)),
            pl.BlockSpec((tr, width), lambda i: (i, 0)),
        ],
        out_shape=[
            jax.ShapeDtypeStruct((t, d), _f32),
            jax.ShapeDtypeStruct((t, width), _f32),
        ],
        compiler_params=pltpu.CompilerParams(
            dimension_semantics=("parallel",),
            vmem_limit_bytes=VMEM_LIMIT_BYTES),
        name="scores",
    )(x, norm_g, w_query, keys1, keys2)


SC_CORES = 2
SC_SUBCORES = 16
SC_LANES = 16
SC_WORKERS = SC_CORES * SC_SUBCORES
SC_TOKENS = 8
SC_GATHER = 32
SC_BLOCK = 8
SC_UNROLL = 2
SC_COLS = 8
HI_MASK = -65536


def _sc_mesh():
    return plsc.VectorSubcoreMesh(core_axis_name="c", subcore_axis_name="s")


def _sc_worker_base(per_worker):
    return (lax.axis_index("s") * SC_CORES + lax.axis_index("c")) * per_worker


def _sc_gather_loop(tab_hbm, idx_v, bufs, consume):
    n_parts = PEER_SLOTS // SC_GATHER
    n_gathers = SC_TOKENS * n_parts

    def gather(g, parity):
        rows, sem = bufs[parity]
        i = g // n_parts
        col = pl.multiple_of((g % n_parts) * SC_GATHER, SC_GATHER)
        return pltpu.make_async_copy(tab_hbm.at[idx_v.at[i, pl.ds(col, SC_GATHER)]], rows, sem)

    gather(0, 0).start()

    @pl.loop(0, n_gathers // 2)
    def _(pair):
        g = 2 * pair
        gather(g + 1, 1).start()
        gather(g, 0).wait()
        consume(g // n_parts, g % n_parts, bufs[0][0])

        @pl.when(g + 2 < n_gathers)
        def _():
            gather(g + 2, 0).start()

        gather(g + 1, 1).wait()
        consume((g + 1) // n_parts, (g + 1) % n_parts, bufs[1][0])


def _pack_halves(tab):
    half = tab.shape[1] // 2
    bits = lax.bitcast_convert_type(tab.astype(_bf16), jnp.uint16).astype(jnp.uint32)
    return lax.bitcast_convert_type(bits[:, :half] | (bits[:, half:] << 16), jnp.int32)


def _sc_unpack(words):
    lo = lax.bitcast_convert_type(words << 16, _f32)
    hi = lax.bitcast_convert_type(words & HI_MASK, _f32)
    return lo, hi


def _stair_vectors():
    pairs = [(a, c) for a in range(PEER_TOPK) for c in range(PEER_TOPK) if (a + 1) * (c + 1) <= PEER_TOPK]
    n = -(-len(pairs) // SC_LANES)
    fill = n * SC_LANES - len(pairs)
    a = jnp.asarray([p[0] for p in pairs] + [0] * fill, jnp.int32).reshape(n, SC_LANES)
    c = jnp.asarray([p[1] for p in pairs] + [0] * fill, jnp.int32).reshape(n, SC_LANES)
    pad = jnp.asarray([0.0] * len(pairs) + [-jnp.inf] * fill, _f32).reshape(n, SC_LANES)
    return a, c, pad


def _sc_route_dot(scores, tab, h):
    t, width = scores.shape
    d = h.shape[1]
    half_d = d // 2
    per_worker = t // SC_WORKERS
    pair_a, pair_c, pair_pad = _stair_vectors()
    n_cand = pair_a.shape[0]
    n_vec = PEER_NKEYS // SC_LANES

    @functools.partial(
        pl.kernel, mesh=_sc_mesh(),
        out_type=(jax.ShapeDtypeStruct((t, PEER_SLOTS), jnp.int32),
                  jax.ShapeDtypeStruct((t, PEER_SLOTS), _f32),
                  jax.ShapeDtypeStruct((t, PEER_SLOTS), _f32)),
        scratch_types=[
            pltpu.VMEM((SC_TOKENS, width), _f32),
            pltpu.VMEM((SC_TOKENS, PEER_SLOTS), jnp.int32),
            pltpu.VMEM((SC_TOKENS, PEER_SLOTS), _f32),
            pltpu.VMEM((n_cand, SC_LANES), jnp.int32),
            pltpu.VMEM((n_cand, SC_LANES), jnp.int32),
            pltpu.VMEM((n_cand, SC_LANES), _f32),
            pltpu.VMEM((4, SC_LANES), _f32),
            pltpu.VMEM((SC_TOKENS, d), _f32),
            pltpu.VMEM((SC_TOKENS, PEER_SLOTS), _f32),
            pltpu.VMEM((SC_GATHER, half_d), jnp.int32),
            pltpu.VMEM((SC_GATHER, half_d), jnp.int32),
            pltpu.SemaphoreType.DMA,
            pltpu.SemaphoreType.DMA,
        ],
        compiler_params=pltpu.CompilerParams(needs_layout_passes=False),
        name="sc_route_dot",
    )
    def body(s_hbm, a_hbm, c_hbm, pad_hbm, tab_hbm, h_hbm, idx_hbm, gate_hbm, acts_hbm,
             s_v, idx_v, gate_v, a_v, c_v, pad_v, top_v, h_v, acts_v, rows0, rows1, sem0, sem1):
        base = _sc_worker_base(per_worker)
        lane = lax.iota(jnp.int32, SC_LANES)
        pltpu.sync_copy(a_hbm, a_v)
        pltpu.sync_copy(c_hbm, c_v)
        pltpu.sync_copy(pad_hbm, pad_v)

        def dots(i, part, rows):
            def block_body(blk, carry):
                row0 = blk * SC_LANES
                outv = jnp.zeros((SC_LANES,), _f32)
                for sub in range(SC_LANES // SC_BLOCK):
                    def chunk_body(jj, accs):
                        accs = list(accs)
                        for u in range(SC_UNROLL):
                            off = pl.multiple_of((jj * SC_UNROLL + u) * SC_LANES, SC_LANES)
                            h_lo = h_v[i, pl.ds(off, SC_LANES)]
                            h_hi = h_v[i, pl.ds(half_d + off, SC_LANES)]
                            for e in range(SC_BLOCK):
                                lo, hi = _sc_unpack(rows[row0 + sub * SC_BLOCK + e, pl.ds(off, SC_LANES)])
                                accs[e] = accs[e] + (lo * h_lo + hi * h_hi)
                        return tuple(accs)

                    accs = lax.fori_loop(0, half_d // SC_LANES // SC_UNROLL, chunk_body,
                                         tuple(jnp.zeros((SC_LANES,), _f32) for _ in range(SC_BLOCK)))
                    for e in range(SC_BLOCK):
                        outv = jnp.where(lane == sub * SC_BLOCK + e, jnp.sum(accs[e]), outv)
                col = pl.multiple_of(part * SC_GATHER + row0, SC_LANES)
                acts_v[i, pl.ds(col, SC_LANES)] = outv
                return carry

            lax.fori_loop(0, SC_GATHER // SC_LANES, block_body, 0)

        def sort_desc(k, v):
            return plsc.sort_key_val(k, v, descending=True)

        def merge(x, y):
            yk, yv = lax.rev(y[0], (0,)), lax.rev(y[1], (0,))
            take = x[0] >= yk
            return sort_desc(jnp.where(take, x[0], yk), jnp.where(take, x[1], yv))

        def top_of(vectors):
            while len(vectors) > 1:
                vectors = [merge(vectors[j], vectors[j + 1]) for j in range(0, len(vectors), 2)]
            return vectors[0]

        def head_body(i, hd):
            halves = []
            for half in range(2):
                col = (2 * hd + half) * PEER_NKEYS
                vecs = [sort_desc(s_v[i, pl.ds(pl.multiple_of(col + j * SC_LANES, SC_LANES), SC_LANES)],
                                  lane + j * SC_LANES) for j in range(n_vec)]
                halves.append(top_of(vecs))
            (v1, i1), (v2, i2) = halves
            top_v[0, :] = v1
            top_v[1, :] = i1.astype(_f32)
            top_v[2, :] = v2
            top_v[3, :] = i2.astype(_f32)

            def pick(row, pos):
                return plsc.load_gather(top_v, [jnp.full((SC_LANES,), row, jnp.int32), pos])

            cands = []
            for q in range(n_cand):
                a, c = a_v[q, :], c_v[q, :]
                cs = pick(0, a) + pick(2, c) + pad_v[q, :]
                ci = pick(1, a) * float(PEER_NKEYS) + pick(3, c)
                cands.append(sort_desc(cs, ci))
            top_s, top_i = top_of(cands)
            e = jnp.exp(top_s - jnp.max(top_s))
            slots = pl.ds(pl.multiple_of(hd * PEER_TOPK, PEER_TOPK), PEER_TOPK)
            idx_v[i, slots] = top_i.astype(jnp.int32)
            gate_v[i, slots] = e / jnp.sum(e)

        @pl.loop(0, per_worker // SC_TOKENS)
        def _(step):
            tok = pl.multiple_of(base + step * SC_TOKENS, SC_TOKENS)
            pltpu.sync_copy(s_hbm.at[pl.ds(tok, SC_TOKENS)], s_v)
            pltpu.sync_copy(h_hbm.at[pl.ds(tok, SC_TOKENS)], h_v)

            @pl.loop(0, SC_TOKENS * PEER_HEADS)
            def _(n):
                head_body(n // PEER_HEADS, n % PEER_HEADS)

            pltpu.sync_copy(idx_v, idx_hbm.at[pl.ds(tok, SC_TOKENS)])
            pltpu.sync_copy(gate_v, gate_hbm.at[pl.ds(tok, SC_TOKENS)])
            _sc_gather_loop(tab_hbm, idx_v, ((rows0, sem0), (rows1, sem1)), dots)
            pltpu.sync_copy(acts_v, acts_hbm.at[pl.ds(tok, SC_TOKENS)])

    return body(scores, pair_a, pair_c, pair_pad, tab, h)


def _sc_vaxpy(idx, w, tab, *, first, count):
    half = tab.shape[1]
    d = 2 * half
    per_worker = count // SC_WORKERS
    span = SC_COLS * SC_LANES

    @functools.partial(
        pl.kernel, mesh=_sc_mesh(),
        out_type=jax.ShapeDtypeStruct((count, d), _f32),
        scratch_types=[
            pltpu.VMEM((SC_TOKENS, PEER_SLOTS), jnp.int32),
            pltpu.VMEM((SC_TOKENS, PEER_SLOTS), _f32),
            pltpu.VMEM((SC_TOKENS, d), _f32),
            pltpu.VMEM((SC_GATHER, half), jnp.int32),
            pltpu.VMEM((SC_GATHER, half), jnp.int32),
            pltpu.SemaphoreType.DMA,
            pltpu.SemaphoreType.DMA,
        ],
        compiler_params=pltpu.CompilerParams(needs_layout_passes=False),
        name="sc_vaxpy",
    )
    def body(idx_hbm, w_hbm, tab_hbm, out_hbm, idx_v, w_v, out_v, rows0, rows1, sem0, sem1):
        base = _sc_worker_base(per_worker)

        def accumulate(i, part, rows):
            i_vec = jnp.full((SC_LANES,), i, jnp.int32)

            def span_body(cq, carry):
                def cols(c, offset=0):
                    return pl.ds(pl.multiple_of(offset + cq * span + c * SC_LANES, SC_LANES), SC_LANES)

                def expert_body(e, accs):
                    k_vec = jnp.full((SC_LANES,), part * SC_GATHER + e, jnp.int32)
                    wv = plsc.load_gather(w_v, [i_vec, k_vec])
                    new = []
                    for c in range(SC_COLS):
                        lo, hi = _sc_unpack(rows[e, cols(c)])
                        new += [accs[2 * c] + lo * wv, accs[2 * c + 1] + hi * wv]
                    return tuple(new)

                init = []
                for c in range(SC_COLS):
                    init += [out_v[i, cols(c)], out_v[i, cols(c, half)]]
                accs = lax.fori_loop(0, SC_GATHER, expert_body, tuple(init))
                for c in range(SC_COLS):
                    out_v[i, cols(c)] = accs[2 * c]
                    out_v[i, cols(c, half)] = accs[2 * c + 1]
                return carry

            lax.fori_loop(0, half // span, span_body, 0)

        @pl.loop(0, per_worker // SC_TOKENS)
        def _(step):
            off = pl.multiple_of(base + step * SC_TOKENS, SC_TOKENS)
            pltpu.sync_copy(idx_hbm.at[pl.ds(first + off, SC_TOKENS)], idx_v)
            pltpu.sync_copy(w_hbm.at[pl.ds(first + off, SC_TOKENS)], w_v)

            @pl.loop(0, SC_TOKENS)
            def _(i):
                @pl.loop(0, d // SC_LANES)
                def _(j):
                    out_v[i, pl.ds(pl.multiple_of(j * SC_LANES, SC_LANES), SC_LANES)] = (
                        jnp.zeros((SC_LANES,), _f32))

            _sc_gather_loop(tab_hbm, idx_v, ((rows0, sem0), (rows1, sem1)), accumulate)
            pltpu.sync_copy(out_v, out_hbm.at[pl.ds(off, SC_TOKENS)])

    return body(idx, w, tab)


ROW_CHUNKS = 8
PACK_ROWS = ROW_CHUNKS // 2


def _pack_table(tab):
    n, d = tab.shape
    bits = lax.bitcast_convert_type(tab.astype(_bf16), jnp.uint16).astype(jnp.uint32)
    bits = bits.reshape(n, PACK_ROWS, 2, d // ROW_CHUNKS)
    word = bits[:, :, 0, :] | (bits[:, :, 1, :] << 16)
    return lax.bitcast_convert_type(word, jnp.int32)


def _gate_weights_kernel(acts_ref, gate_ref, w_ref):
    a = acts_ref[...]
    gelu = 0.5 * a * (1.0 + lax.erf(a * (2.0 ** -0.5)))
    w_ref[...] = gate_ref[...] * gelu


def _gate_weights(acts, gate, *, tw):
    t, n = acts.shape
    spec = pl.BlockSpec((tw, n), lambda i: (i, 0))
    return pl.pallas_call(
        _gate_weights_kernel,
        grid=(t // tw,),
        in_specs=[spec, spec],
        out_specs=spec,
        out_shape=jax.ShapeDtypeStruct((t, n), _f32),
        compiler_params=pltpu.CompilerParams(dimension_semantics=("parallel",)),
        name="gate_weights",
    )(acts, gate)


def _gather_rows(idx_ref, t, tab_ref, rows_ref):
    for k in range(PEER_SLOTS):
        row = pl.multiple_of(idx_ref[t, k], PACK_ROWS)
        rows_ref[k * PACK_ROWS:(k + 1) * PACK_ROWS, :] = tab_ref[pl.ds(row, PACK_ROWS), :]


def _rows_matrix(rows_ref):
    return pltpu.bitcast(rows_ref[...], _bf16)


def _token_loop(tb, idx_ref, tab_ref, rows_a, rows_b, compute):
    _gather_rows(idx_ref, 0, tab_ref, rows_a)

    def pair_body(i, carry):
        t0 = 2 * i
        _gather_rows(idx_ref, t0 + 1, tab_ref, rows_b)
        compute(t0, _rows_matrix(rows_a))
        _gather_rows(idx_ref, jnp.minimum(t0 + 2, tb - 1), tab_ref, rows_a)
        compute(t0 + 1, _rows_matrix(rows_b))
        return carry

    lax.fori_loop(0, tb // 2, pair_body, 0)


def _chunk_diag_mask():
    shape = (ROW_CHUNKS, PEER_SLOTS * ROW_CHUNKS)
    return (_iota(shape, 1) % ROW_CHUNKS) == _iota(shape, 0)


def _vaxpy_kernel(idx_ref, w_ref, tab_ref, out_ref, rows_a, rows_b, wrep_ref):
    tb = out_ref.shape[0]
    diag = _chunk_diag_mask()
    shape = (PEER_SLOTS, PEER_SLOTS * ROW_CHUNKS)
    spread = (_iota(shape, 0) == (_iota(shape, 1) // ROW_CHUNKS)).astype(_bf16)
    wrep_ref[...] = _dot_exact_rhs01(w_ref[...], spread)

    def compute(t, m):
        w_row = jnp.broadcast_to(wrep_ref[pl.ds(t, 1), :], diag.shape)
        w_hi, w_lo = _split2(jnp.where(diag, w_row, 0.0))
        out_ref[t] = _dot(w_hi, m) + _dot(w_lo, m)

    _token_loop(tb, idx_ref, tab_ref, rows_a, rows_b, compute)


def _vaxpy(idx, w, tab, *, tb, count):
    return pl.pallas_call(
        _vaxpy_kernel,
        grid=(count // tb,),
        in_specs=[
            pl.BlockSpec((tb, PEER_SLOTS), lambda i: (i, 0), memory_space=pltpu.SMEM),
            pl.BlockSpec((tb, PEER_SLOTS), lambda i: (i, 0)),
            _const_spec(tab.shape),
        ],
        out_specs=pl.BlockSpec((tb, ROW_CHUNKS, 128), lambda i: (i, 0, 0)),
        out_shape=jax.ShapeDtypeStruct((count, ROW_CHUNKS, 128), _f32),
        scratch_shapes=[
            pltpu.VMEM((PEER_SLOTS * PACK_ROWS, 128), jnp.int32),
            pltpu.VMEM((PEER_SLOTS * PACK_ROWS, 128), jnp.int32),
            pltpu.VMEM((tb, PEER_SLOTS * ROW_CHUNKS), _f32),
        ],
        compiler_params=pltpu.CompilerParams(
            dimension_semantics=("parallel",),
            vmem_limit_bytes=VMEM_LIMIT_BYTES),
        name="vaxpy",
    )(idx, w, tab)


def _final_kernel(x_ref, p_ref, g_ref, out_ref, *, normalize):
    x = x_ref[...] + p_ref[...]
    if normalize:
        x = x * lax.rsqrt(jnp.mean(x * x, axis=-1, keepdims=True) + EPS) * g_ref[...]
    out_ref[...] = x


def _final(x, peer, g, *, normalize, tf, first):
    t, d = peer.shape
    off = first // tf
    return pl.pallas_call(
        functools.partial(_final_kernel, normalize=normalize),
        grid=(t // tf,),
        in_specs=[
            pl.BlockSpec((tf, d), lambda i: (i + off, 0)),
            pl.BlockSpec((tf, d), lambda i: (i, 0)),
            _const_spec((1, d)),
        ],
        out_specs=pl.BlockSpec((tf, d), lambda i: (i, 0)),
        out_shape=jax.ShapeDtypeStruct((t, d), _f32),
        compiler_params=pltpu.CompilerParams(dimension_semantics=("parallel",)),
        name="final_norm",
    )(x, peer, g)


def kernel(x, norm_mix_g, w_in, hg_lb_logits, hg_out_norm_g, conv_w, w_branch_hg, w_branch_conv, w_out, norm_ffn_g, peer_w_query, peer_keys1, peer_keys2, peer_u, peer_v, norm_final_g):
    b_, s_, d = x.shape
    depth = w_in.shape[0]
    lb_all = jnp.cumsum(jax.nn.softmax(hg_lb_logits.astype(_f32), axis=0), axis=0)
    n_groups = BATCH_GROUPS if b_ % BATCH_GROUPS == 0 else 1
    bg = b_ // n_groups
    tg = bg * s_
    n_sc = SC_SHARE if SC_SHARE < tg else 0
    n_tc = tg - n_sc
    for l in range(depth):
        wq = peer_w_query[l].astype(_bf16).reshape(d, PEER_HEADS, 2, PEER_HALF).transpose(1, 2, 0, 3)
        v_tab = _pack_table(peer_v[l])
        u_sc = _pack_halves(peer_u[l])
        v_sc = _pack_halves(peer_v[l])
        last = l == depth - 1
        g = norm_final_g[None] if last else jnp.ones((1, d), _f32)
        def front(c, x_in, sc_before=None):
            xc = _mix(x_in, norm_mix_g[l][None], w_in[l].astype(_bf16), lb_all[l][None],
                      hg_out_norm_g[l][None], conv_w[l], w_branch_hg[l].astype(_bf16),
                      w_branch_conv[l].astype(_bf16), w_out[l].astype(_bf16), ts=MIX_TILE)
            xf = xc.reshape(tg, d)
            h, scores = _scores(xf, norm_ffn_g[l][None], wq,
                                peer_keys1[l].astype(_bf16), peer_keys2[l].astype(_bf16), tr=ROUTE_TILE)
            if sc_before is not None:
                scores = lax.optimization_barrier((scores, sc_before))[0]
            idx, gate, acts = _sc_route_dot(scores, u_sc, h)
            return (xf, idx, gate, acts), scores

        def back(xf, idx, gate, acts):
            w = _gate_weights(acts, gate, tw=FINAL_TILE)
            peer_tc = _vaxpy(idx * PACK_ROWS, w, v_tab.reshape(-1, 128), tb=EXPERT_TILE, count=n_tc).reshape(n_tc, d)
            peer_sc = _sc_vaxpy(idx, w, v_sc, first=n_tc, count=n_sc) if n_sc else None
            return peer_tc, peer_sc

        groups = [x[c * bg:(c + 1) * bg] for c in range(n_groups)]
        fronts = {c: front(c, groups[c]) for c in range(min(GROUPS_AHEAD + 1, n_groups))}
        done, peer_tc = [], None
        for c in range(n_groups):
            (xf, idx, gate, acts), _ = fronts.pop(c)
            after = [fronts[c + GROUPS_AHEAD][1]] if c + GROUPS_AHEAD in fronts else []
            after += [peer_tc] if peer_tc is not None else []
            if after:
                acts = lax.optimization_barrier((acts, *after))[0]
            peer_tc, peer_sc = back(xf, idx, gate, acts)
            done.append((xf, peer_tc, peer_sc))
            nxt = c + GROUPS_AHEAD + 1
            if nxt < n_groups:
                fronts[nxt] = front(nxt, lax.optimization_barrier((groups[nxt], peer_tc))[0], peer_sc)
        outs = []
        for xf, p_tc, p_sc in done:
            outs.append(_final(xf, p_tc, g, normalize=last, tf=FINAL_TILE, first=0))
            if p_sc is not None:
                p_sc = lax.optimization_barrier((p_sc, peer_tc))[0]
                outs.append(_final(xf, p_sc, g, normalize=last, tf=FINAL_TILE, first=n_tc))
        x = jnp.concatenate(outs, axis=0).reshape(b_, s_, d)
    return x
```

```python
import functools

import jax
import jax.numpy as jnp
from jax import lax
from jax.experimental import pallas as pl
from jax.experimental.pallas import tpu as pltpu
from jax.experimental.pallas import tpu_sc as plsc

EPS = 1e-6
CHUNK = 64
SUB = 16
HEADS = 8
HEAD_DIM = 64
HG_WIDTH = HEADS * HEAD_DIM
GROUP = 256
N_GROUPS = HG_WIDTH // GROUP
CONV_K = 3
PEER_HEADS = 8
PEER_NKEYS = 128
PEER_HALF = 128
PEER_TOPK = 16
PEER_SLOTS = PEER_HEADS * PEER_TOPK

VMEM_LIMIT_BYTES = 56 * 1024 * 1024

MIX_TILE = 256
ROUTE_TILE = 256
EXPERT_TILE = 128
FINAL_TILE = 512
BATCH_GROUPS = 8
SC_SHARE = 1024
GROUPS_AHEAD = 7

_f32 = jnp.float32
_bf16 = jnp.bfloat16


def _dot(a, b):
    return jnp.dot(a, b, preferred_element_type=_f32)


def _dot_nt(a, b):
    return lax.dot_general(a, b, (((1,), (1,)), ((), ())), preferred_element_type=_f32)


def _dot_tn(a, b):
    return lax.dot_general(a, b, (((0,), (0,)), ((), ())), preferred_element_type=_f32)


def _split3(x):
    hi = x.astype(_bf16)
    r1 = x - hi.astype(_f32)
    mid = r1.astype(_bf16)
    lo = (r1 - mid.astype(_f32)).astype(_bf16)
    return hi, mid, lo


def _split2(x):
    hi = x.astype(_bf16)
    lo = (x - hi.astype(_f32)).astype(_bf16)
    return hi, lo


def _dot_exact_rhs01(x, m01):
    hi, mid, lo = _split3(x)
    return _dot(hi, m01) + _dot(mid, m01) + _dot(lo, m01)


def _dot_exact_lhs01(m01, x):
    hi, mid, lo = _split3(x)
    return _dot(m01, hi) + _dot(m01, mid) + _dot(m01, lo)


def _iota(shape, dim):
    return lax.broadcasted_iota(jnp.int32, shape, dim)


def _hgrn2_chunk(q, k, lf, v, state_ref):
    n_sub = CHUNK // SUB
    row = _iota((CHUNK, CHUNK), 0)
    col = _iota((CHUNK, CHUNK), 1)
    tril = (col <= row).astype(_bf16)
    b = _dot_exact_lhs01(tril, lf)

    b_end = [b[(j + 1) * SUB - 1:(j + 1) * SUB, :] for j in range(n_sub)]
    b_end_rows = jnp.concatenate([jnp.broadcast_to(e, (SUB, HG_WIDTH)) for e in b_end], axis=0)
    b_last = b_end[-1]

    q_in = (q * jnp.exp(b)).astype(_bf16)
    k_sub = (k * jnp.exp(b_end_rows - b)).astype(_bf16)
    k_out = (k * jnp.exp(b_last - b)).astype(_bf16)
    q_from = [(q * jnp.exp(jnp.minimum(b - b_end[j], 0.0))).astype(_bf16) for j in range(n_sub - 1)]
    v_b = v.astype(_bf16)

    gr = _iota((GROUP, GROUP), 0) // HEAD_DIM
    gc = _iota((GROUP, GROUP), 1) // HEAD_DIM
    head_mask = gr == gc
    t_blk = _iota((CHUNK, GROUP), 0) // SUB
    s_blk = (_iota((CHUNK, GROUP), 1) % HEAD_DIM) // SUB

    outs = []
    for g in range(N_GROUPS):
        sl = slice(g * GROUP, (g + 1) * GROUP)
        st = state_ref[g]
        o_g = _dot_nt(q_in[:, sl], st.astype(_bf16))

        zero_b = jnp.zeros((), _bf16)
        k_bd = jnp.where(head_mask, jnp.concatenate([k_sub[:, sl]] * (GROUP // CHUNK), axis=0), zero_b)
        v_bd = jnp.where(head_mask, jnp.concatenate([v_b[:, sl]] * (GROUP // CHUNK), axis=0), zero_b)
        q_stack = jnp.concatenate([qf[:, sl] for qf in q_from], axis=0)
        r = _dot_nt(q_stack, k_bd)
        scores = jnp.zeros((CHUNK, GROUP), _f32)
        for j in range(n_sub - 1):
            sel = (s_blk == j) & (t_blk > j)
            scores = jnp.where(sel, r[j * CHUNK:(j + 1) * CHUNK, :], scores)
        o_g = o_g + _dot(scores.astype(_bf16), v_bd)
        outs.append(o_g)

        upd = _dot_tn(v_b[:, sl], k_out[:, sl])
        decay = jnp.exp(b_last[:, sl])
        state_ref[g] = st * decay + jnp.where(head_mask, upd, 0.0)
    o = jnp.concatenate(outs, axis=1)

    ones_bd = ((_iota((HG_WIDTH, HG_WIDTH), 0) // HEAD_DIM)
               == (_iota((HG_WIDTH, HG_WIDTH), 1) // HEAD_DIM)).astype(_bf16)
    t_in_sub = _iota((CHUNK, HG_WIDTH), 0) % SUB
    for lag in range(SUB):
        if lag == 0:
            p = q * k
            v_l = v
        else:
            valid = t_in_sub >= lag
            k_l = pltpu.roll(k, lag, 0)
            b_l = pltpu.roll(b, lag, 0)
            v_l = pltpu.roll(v, lag, 0)
            p = jnp.where(valid, q * k_l * jnp.exp(jnp.minimum(b - b_l, 0.0)), 0.0)
        s_l = _dot(p.astype(_bf16), ones_bd)
        o = o + s_l * v_l
    return o


def _mix_kernel(x_ref, g_ref, win_ref, lb_ref, hgn_ref, convw_ref, pa_ref, pb_ref, wo_ref,
                out_ref, state_ref, carry_ref, q_s, k_s, lf_s, v_s, o_s):
    ts = x_ref.shape[0]
    d_model = x_ref.shape[1]
    w = HG_WIDTH

    @pl.when(pl.program_id(1) == 0)
    def _():
        state_ref[...] = jnp.zeros_like(state_ref)
        carry_ref[...] = jnp.zeros_like(carry_ref)

    x = x_ref[...]
    h = x * lax.rsqrt(jnp.mean(x * x, axis=-1, keepdims=True) + EPS) * g_ref[...]
    hb = h.astype(_bf16)

    def proj(i, width=w):
        return _dot(hb, win_ref[:, i * w:i * w + width])

    lb = lb_ref[...]
    q_s[...] = jax.nn.silu(proj(0)) * (HEAD_DIM ** -0.5)
    forget = lb + (1.0 - lb) * jax.nn.sigmoid(proj(1))
    k_s[...] = 1.0 - forget
    lf_s[...] = jnp.log(forget)
    v_s[...] = proj(2)

    def chunk_body(c, carry):
        rows = pl.ds(pl.multiple_of(c * CHUNK, CHUNK), CHUNK)
        o_s[rows, :] = _hgrn2_chunk(q_s[rows, :], k_s[rows, :], lf_s[rows, :], v_s[rows, :], state_ref)
        return carry

    lax.fori_loop(0, ts // CHUNK, chunk_body, 0)

    o = o_s[...]
    ones_bd = ((_iota((w, w), 0) // HEAD_DIM) == (_iota((w, w), 1) // HEAD_DIM)).astype(_bf16)
    ms = _dot_exact_rhs01(o * o, ones_bd) * (1.0 / HEAD_DIM)
    o = o * lax.rsqrt(ms + EPS) * hgn_ref[...]
    y_a = (o * jax.nn.silu(proj(3))).astype(_bf16)

    u = proj(5) * proj(6)
    prev = carry_ref[...]
    rowi = _iota((ts, w), 0)
    u1 = jnp.where(rowi >= 1, pltpu.roll(u, 1, 0), jnp.broadcast_to(prev[7:8, :], (ts, w)))
    u2 = jnp.where(rowi >= 2, pltpu.roll(u, 2, 0),
                   jnp.where(rowi == 1, jnp.broadcast_to(prev[7:8, :], (ts, w)),
                             jnp.broadcast_to(prev[6:7, :], (ts, w))))
    carry_ref[...] = u[ts - 8:, :]
    cw = convw_ref[...]
    y_b = (proj(4) * (cw[0:1, :] * u2 + cw[1:2, :] * u1 + cw[2:3, :] * u)).astype(_bf16)

    g_a = jax.nn.sigmoid(proj(7, d_model))
    g_b = jax.nn.sigmoid(_dot(hb, win_ref[:, 7 * w + d_model:7 * w + 2 * d_model]))
    merged = g_a * _dot(y_a, pa_ref[...]) + g_b * _dot(y_b, pb_ref[...])
    out_ref[...] = x + _dot(merged.astype(_bf16), wo_ref[...])


def _const_spec(shape):
    nd = len(shape)
    return pl.BlockSpec(shape, lambda *_: (0,) * nd, pipeline_mode=pl.Buffered(1))


def _mix(x, norm_g, w_in, lb, hg_norm_g, conv_w, w_a, w_b, w_o, *, ts):
    b_, s_, d = x.shape
    in_cols = w_in.shape[1]
    w = HG_WIDTH
    grid = (b_, s_ // ts)
    return pl.pallas_call(
        _mix_kernel,
        grid=grid,
        in_specs=[
            pl.BlockSpec((None, ts, d), lambda b, s: (b, s, 0)),
            _const_spec((1, d)),
            _const_spec((d, in_cols)),
            _const_spec((1, w)),
            _const_spec((1, w)),
            _const_spec((CONV_K, w)),
            _const_spec((w, d)),
            _const_spec((w, d)),
            _const_spec((d, d)),
        ],
        out_specs=pl.BlockSpec((None, ts, d), lambda b, s: (b, s, 0)),
        out_shape=jax.ShapeDtypeStruct((b_, s_, d), _f32),
        scratch_shapes=[
            pltpu.VMEM((N_GROUPS, GROUP, GROUP), _f32),
            pltpu.VMEM((8, w), _f32),
            pltpu.VMEM((ts, w), _f32),
            pltpu.VMEM((ts, w), _f32),
            pltpu.VMEM((ts, w), _f32),
            pltpu.VMEM((ts, w), _f32),
            pltpu.VMEM((ts, w), _f32),
        ],
        compiler_params=pltpu.CompilerParams(
            dimension_semantics=("parallel", "arbitrary"),
            vmem_limit_bytes=VMEM_LIMIT_BYTES),
        name="mix",
    )(x, norm_g, w_in, lb, hg_norm_g, conv_w, w_a, w_b, w_o)


def _stair_pairs():
    pairs = [(a, c) for a in range(PEER_TOPK) for c in range(PEER_TOPK) if (a + 1) * (c + 1) <= PEER_TOPK]
    rows = -(-len(pairs) // 8) * 8
    ranks = jnp.arange(PEER_TOPK)[None, :]
    a_col = jnp.asarray([a for a, _ in pairs] + [-1] * (rows - len(pairs)))[:, None]
    c_col = jnp.asarray([c for _, c in pairs] + [-1] * (rows - len(pairs)))[:, None]
    pad = jnp.where(a_col < 0, -jnp.inf, 0.0).astype(_f32)
    return (a_col == ranks).astype(_bf16), (c_col == ranks).astype(_bf16), pad


ROUTE_LANES = 128


def _route_kernel(x_ref, g_ref, wq_ref, k1_ref, k2_ref, sa_ref, sc_ref, pad_ref, h_ref, idx_ref, gate_ref,
                  idx_t, e_t, top_ref):
    tr = x_ref.shape[0]
    n_cand = sa_ref.shape[0]
    x = x_ref[...]
    h = x * lax.rsqrt(jnp.mean(x * x, axis=-1, keepdims=True) + EPS) * g_ref[...]
    h_ref[...] = h
    hb = h.astype(_bf16)

    key_row = _iota((PEER_NKEYS, ROUTE_LANES), 0).astype(_f32)
    cand_row = _iota((n_cand, tr), 0).astype(_f32)
    neg_inf = jnp.float32(-jnp.inf)

    def extract_max(s):
        m = jnp.max(s, axis=0, keepdims=True)
        i = jnp.min(jnp.where(s == m, key_row, float(PEER_NKEYS)), axis=0, keepdims=True)
        return m, i, jnp.where(key_row == i, neg_inf, s)

    def head_body(hd, carry):
        q1 = _dot(hb, wq_ref[hd, 0]).astype(_bf16)
        q2 = _dot(hb, wq_ref[hd, 1]).astype(_bf16)
        s1 = _dot_nt(k1_ref[hd], q1)
        s2 = _dot_nt(k2_ref[hd], q2)

        for lt in range(tr // ROUTE_LANES):
            lanes = slice(lt * ROUTE_LANES, (lt + 1) * ROUTE_LANES)

            def half_body(k, c):
                m1, i1, r1 = extract_max(c[0])
                m2, i2, r2 = extract_max(c[1])
                for j, row in enumerate((m1, i1, m2, i2)):
                    top_ref[lt, j, pl.ds(k, 1), :] = row
                return r1, r2

            lax.fori_loop(0, PEER_TOPK, half_body, (s1[:, lanes], s2[:, lanes]))

        def top(j):
            return jnp.concatenate([top_ref[lt, j] for lt in range(tr // ROUTE_LANES)], axis=1)

        sa = sa_ref[...]
        sc = sc_ref[...]
        cand_s = _dot_exact_lhs01(sa, top(0)) + _dot_exact_lhs01(sc, top(2)) + pad_ref[...]
        cand_i = (_dot(sa, top(1).astype(_bf16)) * float(PEER_NKEYS)
                  + _dot(sc, top(3).astype(_bf16)))

        def pick_body(k, c):
            cand_s, denom, m_first = c
            m = jnp.max(cand_s, axis=0, keepdims=True)
            pos = jnp.min(jnp.where(cand_s == m, cand_row, float(n_cand)), axis=0, keepdims=True)
            hit = cand_row == pos
            eid = jnp.max(jnp.where(hit, cand_i, -1.0), axis=0, keepdims=True)
            m_first = jnp.where(k == 0, m, m_first)
            e = jnp.exp(m - m_first)
            slot = hd * PEER_TOPK + k
            idx_t[pl.ds(slot, 1), :] = eid
            e_t[pl.ds(slot, 1), :] = e
            return jnp.where(hit, neg_inf, cand_s), denom + e, m_first

        zero_row = jnp.zeros((1, tr), _f32)
        _, denom, _ = lax.fori_loop(0, PEER_TOPK, pick_body, (cand_s, zero_row, zero_row))
        rows = pl.ds(pl.multiple_of(hd * PEER_TOPK, PEER_TOPK), PEER_TOPK)
        e_t[rows, :] = e_t[rows, :] / denom
        return carry

    lax.fori_loop(0, PEER_HEADS, head_body, 0)
    idx_ref[...] = idx_t[...].T.astype(jnp.int32)
    gate_ref[...] = e_t[...].T


def _route(x, norm_g, w_query, keys1, keys2, *, tr):
    t, d = x.shape
    sel_a, sel_c, pad = _stair_pairs()
    pad = jnp.broadcast_to(pad, (pad.shape[0], tr))
    return pl.pallas_call(
        _route_kernel,
        grid=(t // tr,),
        in_specs=[
            pl.BlockSpec((tr, d), lambda i: (i, 0)),
            _const_spec((1, d)),
            _const_spec(w_query.shape),
            _const_spec(keys1.shape),
            _const_spec(keys2.shape),
            _const_spec(sel_a.shape),
            _const_spec(sel_c.shape),
            _const_spec(pad.shape),
        ],
        out_specs=[
            pl.BlockSpec((tr, d), lambda i: (i, 0)),
            pl.BlockSpec((tr, PEER_SLOTS), lambda i: (i, 0)),
            pl.BlockSpec((tr, PEER_SLOTS), lambda i: (i, 0)),
        ],
        out_shape=[
            jax.ShapeDtypeStruct((t, d), _f32),
            jax.ShapeDtypeStruct((t, PEER_SLOTS), jnp.int32),
            jax.ShapeDtypeStruct((t, PEER_SLOTS), _f32),
        ],
        scratch_shapes=[
            pltpu.VMEM((PEER_SLOTS, tr), _f32),
            pltpu.VMEM((PEER_SLOTS, tr), _f32),
            pltpu.VMEM((tr // ROUTE_LANES, 4, PEER_TOPK, ROUTE_LANES), _f32),
        ],
        compiler_params=pltpu.CompilerParams(
            dimension_semantics=("parallel",),
            vmem_limit_bytes=VMEM_LIMIT_BYTES),
        name="route",
    )(x, norm_g, w_query, keys1, keys2, sel_a, sel_c, pad)


def _scores_kernel(x_ref, g_ref, wq_ref, k1_ref, k2_ref, h_ref, s_ref):
    x = x_ref[...]
    h = x * lax.rsqrt(jnp.mean(x * x, axis=-1, keepdims=True) + EPS) * g_ref[...]
    h_ref[...] = h
    hb = h.astype(_bf16)
    for hd in range(PEER_HEADS):
        for half, keys_ref in enumerate((k1_ref, k2_ref)):
            q = _dot(hb, wq_ref[hd, half]).astype(_bf16)
            col = (2 * hd + half) * PEER_NKEYS
            s_ref[:, col:col + PEER_NKEYS] = _dot_nt(q, keys_ref[hd])


def _scores(x, norm_g, w_query, keys1, keys2, *, tr):
    t, d = x.shape
    width = 2 * PEER_HEADS * PEER_NKEYS
    return pl.pallas_call(
        _scores_kernel,
        grid=(t // tr,),
        in_specs=[
            pl.BlockSpec((tr, d), lambda i: (i, 0)),
            _const_spec((1, d)),
            _const_spec(w_query.shape),
            _const_spec(keys1.shape),
            _const_spec(keys2.shape),
        ],
        out_specs=[
            pl.BlockSpec((tr, d), lambda i: (i, 0)),
            pl.BlockSpec((tr, width), lambda i: (i, 0)),
        ],
        out_shape=[
            jax.ShapeDtypeStruct((t, d), _f32),
            jax.ShapeDtypeStruct((t, width), _f32),
        ],
        compiler_params=pltpu.CompilerParams(
            dimension_semantics=("parallel",),
            vmem_limit_bytes=VMEM_LIMIT_BYTES),
        name="scores",
    )(x, norm_g, w_query, keys1, keys2)


SC_CORES = 2
SC_SUBCORES = 16
SC_LANES = 16
SC_WORKERS = SC_CORES * SC_SUBCORES
SC_TOKENS = 8
SC_GATHER = 32
SC_BLOCK = 8
SC_UNROLL = 2
SC_COLS = 8
HI_MASK = -65536


def _sc_mesh():
    return plsc.VectorSubcoreMesh(core_axis_name="c", subcore_axis_name="s")


def _sc_worker_base(per_worker):
    return (lax.axis_index("s") * SC_CORES + lax.axis_index("c")) * per_worker


def _sc_gather_loop(tab_hbm, idx_v, bufs, consume):
    n_parts = PEER_SLOTS // SC_GATHER
    n_gathers = SC_TOKENS * n_parts

    def gather(g, parity):
        rows, sem = bufs[parity]
        i = g // n_parts
        col = pl.multiple_of((g % n_parts) * SC_GATHER, SC_GATHER)
        return pltpu.make_async_copy(tab_hbm.at[idx_v.at[i, pl.ds(col, SC_GATHER)]], rows, sem)

    gather(0, 0).start()

    @pl.loop(0, n_gathers // 2)
    def _(pair):
        g = 2 * pair
        gather(g + 1, 1).start()
        gather(g, 0).wait()
        consume(g // n_parts, g % n_parts, bufs[0][0])

        @pl.when(g + 2 < n_gathers)
        def _():
            gather(g + 2, 0).start()

        gather(g + 1, 1).wait()
        consume((g + 1) // n_parts, (g + 1) % n_parts, bufs[1][0])


def _pack_halves(tab):
    half = tab.shape[1] // 2
    bits = lax.bitcast_convert_type(tab.astype(_bf16), jnp.uint16).astype(jnp.uint32)
    return lax.bitcast_convert_type(bits[:, :half] | (bits[:, half:] << 16), jnp.int32)


def _sc_unpack(words):
    lo = lax.bitcast_convert_type(words << 16, _f32)
    hi = lax.bitcast_convert_type(words & HI_MASK, _f32)
    return lo, hi


def _stair_vectors():
    pairs = [(a, c) for a in range(PEER_TOPK) for c in range(PEER_TOPK) if (a + 1) * (c + 1) <= PEER_TOPK]
    n = -(-len(pairs) // SC_LANES)
    fill = n * SC_LANES - len(pairs)
    a = jnp.asarray([p[0] for p in pairs] + [0] * fill, jnp.int32).reshape(n, SC_LANES)
    c = jnp.asarray([p[1] for p in pairs] + [0] * fill, jnp.int32).reshape(n, SC_LANES)
    pad = jnp.asarray([0.0] * len(pairs) + [-jnp.inf] * fill, _f32).reshape(n, SC_LANES)
    return a, c, pad


def _sc_route_dot(scores, tab, h):
    t, width = scores.shape
    d = h.shape[1]
    half_d = d // 2
    per_worker = t // SC_WORKERS
    pair_a, pair_c, pair_pad = _stair_vectors()
    n_cand = pair_a.shape[0]
    n_vec = PEER_NKEYS // SC_LANES

    @functools.partial(
        pl.kernel, mesh=_sc_mesh(),
        out_type=(jax.ShapeDtypeStruct((t, PEER_SLOTS), jnp.int32),
                  jax.ShapeDtypeStruct((t, PEER_SLOTS), _f32),
                  jax.ShapeDtypeStruct((t, PEER_SLOTS), _f32)),
        scratch_types=[
            pltpu.VMEM((SC_TOKENS, width), _f32),
            pltpu.VMEM((SC_TOKENS, PEER_SLOTS), jnp.int32),
            pltpu.VMEM((SC_TOKENS, PEER_SLOTS), _f32),
            pltpu.VMEM((n_cand, SC_LANES), jnp.int32),
            pltpu.VMEM((n_cand, SC_LANES), jnp.int32),
            pltpu.VMEM((n_cand, SC_LANES), _f32),
            pltpu.VMEM((4, SC_LANES), _f32),
            pltpu.VMEM((SC_TOKENS, d), _f32),
            pltpu.VMEM((SC_TOKENS, PEER_SLOTS), _f32),
            pltpu.VMEM((SC_GATHER, half_d), jnp.int32),
            pltpu.VMEM((SC_GATHER, half_d), jnp.int32),
            pltpu.SemaphoreType.DMA,
            pltpu.SemaphoreType.DMA,
        ],
        compiler_params=pltpu.CompilerParams(needs_layout_passes=False),
        name="sc_route_dot",
    )
    def body(s_hbm, a_hbm, c_hbm, pad_hbm, tab_hbm, h_hbm, idx_hbm, gate_hbm, acts_hbm,
             s_v, idx_v, gate_v, a_v, c_v, pad_v, top_v, h_v, acts_v, rows0, rows1, sem0, sem1):
        base = _sc_worker_base(per_worker)
        lane = lax.iota(jnp.int32, SC_LANES)
        pltpu.sync_copy(a_hbm, a_v)
        pltpu.sync_copy(c_hbm, c_v)
        pltpu.sync_copy(pad_hbm, pad_v)

        def dots(i, part, rows):
            def block_body(blk, carry):
                row0 = blk * SC_LANES
                outv = jnp.zeros((SC_LANES,), _f32)
                for sub in range(SC_LANES // SC_BLOCK):
                    def chunk_body(jj, accs):
                        accs = list(accs)
                        for u in range(SC_UNROLL):
                            off = pl.multiple_of((jj * SC_UNROLL + u) * SC_LANES, SC_LANES)
                            h_lo = h_v[i, pl.ds(off, SC_LANES)]
                            h_hi = h_v[i, pl.ds(half_d + off, SC_LANES)]
                            for e in range(SC_BLOCK):
                                lo, hi = _sc_unpack(rows[row0 + sub * SC_BLOCK + e, pl.ds(off, SC_LANES)])
                                accs[e] = accs[e] + (lo * h_lo + hi * h_hi)
                        return tuple(accs)

                    accs = lax.fori_loop(0, half_d // SC_LANES // SC_UNROLL, chunk_body,
                                         tuple(jnp.zeros((SC_LANES,), _f32) for _ in range(SC_BLOCK)))
                    for e in range(SC_BLOCK):
                        outv = jnp.where(lane == sub * SC_BLOCK + e, jnp.sum(accs[e]), outv)
                col = pl.multiple_of(part * SC_GATHER + row0, SC_LANES)
                acts_v[i, pl.ds(col, SC_LANES)] = outv
                return carry

            lax.fori_loop(0, SC_GATHER // SC_LANES, block_body, 0)

        def sort_desc(k, v):
            return plsc.sort_key_val(k, v, descending=True)

        def merge(x, y):
            yk, yv = lax.rev(y[0], (0,)), lax.rev(y[1], (0,))
            take = x[0] >= yk
            return sort_desc(jnp.where(take, x[0], yk), jnp.where(take, x[1], yv))

        def top_of(vectors):
            while len(vectors) > 1:
                vectors = [merge(vectors[j], vectors[j + 1]) for j in range(0, len(vectors), 2)]
            return vectors[0]

        def head_body(i, hd):
            halves = []
            for half in range(2):
                col = (2 * hd + half) * PEER_NKEYS
                vecs = [sort_desc(s_v[i, pl.ds(pl.multiple_of(col + j * SC_LANES, SC_LANES), SC_LANES)],
                                  lane + j * SC_LANES) for j in range(n_vec)]
                halves.append(top_of(vecs))
            (v1, i1), (v2, i2) = halves
            top_v[0, :] = v1
            top_v[1, :] = i1.astype(_f32)
            top_v[2, :] = v2
            top_v[3, :] = i2.astype(_f32)

            def pick(row, pos):
                return plsc.load_gather(top_v, [jnp.full((SC_LANES,), row, jnp.int32), pos])

            cands = []
            for q in range(n_cand):
                a, c = a_v[q, :], c_v[q, :]
                cs = pick(0, a) + pick(2, c) + pad_v[q, :]
                ci = pick(1, a) * float(PEER_NKEYS) + pick(3, c)
                cands.append(sort_desc(cs, ci))
            top_s, top_i = top_of(cands)
            e = jnp.exp(top_s - jnp.max(top_s))
            slots = pl.ds(pl.multiple_of(hd * PEER_TOPK, PEER_TOPK), PEER_TOPK)
            idx_v[i, slots] = top_i.astype(jnp.int32)
            gate_v[i, slots] = e / jnp.sum(e)

        @pl.loop(0, per_worker // SC_TOKENS)
        def _(step):
            tok = pl.multiple_of(base + step * SC_TOKENS, SC_TOKENS)
            pltpu.sync_copy(s_hbm.at[pl.ds(tok, SC_TOKENS)], s_v)
            pltpu.sync_copy(h_hbm.at[pl.ds(tok, SC_TOKENS)], h_v)

            @pl.loop(0, SC_TOKENS * PEER_HEADS)
            def _(n):
                head_body(n // PEER_HEADS, n % PEER_HEADS)

            pltpu.sync_copy(idx_v, idx_hbm.at[pl.ds(tok, SC_TOKENS)])
            pltpu.sync_copy(gate_v, gate_hbm.at[pl.ds(tok, SC_TOKENS)])
            _sc_gather_loop(tab_hbm, idx_v, ((rows0, sem0), (rows1, sem1)), dots)
            pltpu.sync_copy(acts_v, acts_hbm.at[pl.ds(tok, SC_TOKENS)])

    return body(scores, pair_a, pair_c, pair_pad, tab, h)


def _sc_vaxpy(idx, w, tab, *, first, count):
    half = tab.shape[1]
    d = 2 * half
    per_worker = count // SC_WORKERS
    span = SC_COLS * SC_LANES

    @functools.partial(
        pl.kernel, mesh=_sc_mesh(),
        out_type=jax.ShapeDtypeStruct((count, d), _f32),
        scratch_types=[
            pltpu.VMEM((SC_TOKENS, PEER_SLOTS), jnp.int32),
            pltpu.VMEM((SC_TOKENS, PEER_SLOTS), _f32),
            pltpu.VMEM((SC_TOKENS, d), _f32),
            pltpu.VMEM((SC_GATHER, half), jnp.int32),
            pltpu.VMEM((SC_GATHER, half), jnp.int32),
            pltpu.SemaphoreType.DMA,
            pltpu.SemaphoreType.DMA,
        ],
        compiler_params=pltpu.CompilerParams(needs_layout_passes=False),
        name="sc_vaxpy",
    )
    def body(idx_hbm, w_hbm, tab_hbm, out_hbm, idx_v, w_v, out_v, rows0, rows1, sem0, sem1):
        base = _sc_worker_base(per_worker)

        def accumulate(i, part, rows):
            i_vec = jnp.full((SC_LANES,), i, jnp.int32)

            def span_body(cq, carry):
                def cols(c, offset=0):
                    return pl.ds(pl.multiple_of(offset + cq * span + c * SC_LANES, SC_LANES), SC_LANES)

                def expert_body(e, accs):
                    k_vec = jnp.full((SC_LANES,), part * SC_GATHER + e, jnp.int32)
                    wv = plsc.load_gather(w_v, [i_vec, k_vec])
                    new = []
                    for c in range(SC_COLS):
                        lo, hi = _sc_unpack(rows[e, cols(c)])
                        new += [accs[2 * c] + lo * wv, accs[2 * c + 1] + hi * wv]
                    return tuple(new)

                init = []
                for c in range(SC_COLS):
                    init += [out_v[i, cols(c)], out_v[i, cols(c, half)]]
                accs = lax.fori_loop(0, SC_GATHER, expert_body, tuple(init))
                for c in range(SC_COLS):
                    out_v[i, cols(c)] = accs[2 * c]
                    out_v[i, cols(c, half)] = accs[2 * c + 1]
                return carry

            lax.fori_loop(0, half // span, span_body, 0)

        @pl.loop(0, per_worker // SC_TOKENS)
        def _(step):
            off = pl.multiple_of(base + step * SC_TOKENS, SC_TOKENS)
            pltpu.sync_copy(idx_hbm.at[pl.ds(first + off, SC_TOKENS)], idx_v)
            pltpu.sync_copy(w_hbm.at[pl.ds(first + off, SC_TOKENS)], w_v)

            @pl.loop(0, SC_TOKENS)
            def _(i):
                @pl.loop(0, d // SC_LANES)
                def _(j):
                    out_v[i, pl.ds(pl.multiple_of(j * SC_LANES, SC_LANES), SC_LANES)] = (
                        jnp.zeros((SC_LANES,), _f32))

            _sc_gather_loop(tab_hbm, idx_v, ((rows0, sem0), (rows1, sem1)), accumulate)
            pltpu.sync_copy(out_v, out_hbm.at[pl.ds(off, SC_TOKENS)])

    return body(idx, w, tab)


ROW_CHUNKS = 8
PACK_ROWS = ROW_CHUNKS // 2


def _pack_table(tab):
    n, d = tab.shape
    bits = lax.bitcast_convert_type(tab.astype(_bf16), jnp.uint16).astype(jnp.uint32)
    bits = bits.reshape(n, PACK_ROWS, 2, d // ROW_CHUNKS)
    word = bits[:, :, 0, :] | (bits[:, :, 1, :] << 16)
    return lax.bitcast_convert_type(word, jnp.int32)


def _gate_weights_kernel(acts_ref, gate_ref, w_ref):
    a = acts_ref[...]
    gelu = 0.5 * a * (1.0 + lax.erf(a * (2.0 ** -0.5)))
    w_ref[...] = gate_ref[...] * gelu


def _gate_weights(acts, gate, *, tw):
    t, n = acts.shape
    spec = pl.BlockSpec((tw, n), lambda i: (i, 0))
    return pl.pallas_call(
        _gate_weights_kernel,
        grid=(t // tw,),
        in_specs=[spec, spec],
        out_specs=spec,
        out_shape=jax.ShapeDtypeStruct((t, n), _f32),
        compiler_params=pltpu.CompilerParams(dimension_semantics=("parallel",)),
        name="gate_weights",
    )(acts, gate)


def _gather_rows(idx_ref, t, tab_ref, rows_ref):
    for k in range(PEER_SLOTS):
        row = pl.multiple_of(idx_ref[t, k], PACK_ROWS)
        rows_ref[k * PACK_ROWS:(k + 1) * PACK_ROWS, :] = tab_ref[pl.ds(row, PACK_ROWS), :]


def _rows_matrix(rows_ref):
    return pltpu.bitcast(rows_ref[...], _bf16)


def _token_loop(tb, idx_ref, tab_ref, rows_a, rows_b, compute):
    _gather_rows(idx_ref, 0, tab_ref, rows_a)

    def pair_body(i, carry):
        t0 = 2 * i
        _gather_rows(idx_ref, t0 + 1, tab_ref, rows_b)
        compute(t0, _rows_matrix(rows_a))
        _gather_rows(idx_ref, jnp.minimum(t0 + 2, tb - 1), tab_ref, rows_a)
        compute(t0 + 1, _rows_matrix(rows_b))
        return carry

    lax.fori_loop(0, tb // 2, pair_body, 0)


def _chunk_diag_mask():
    shape = (ROW_CHUNKS, PEER_SLOTS * ROW_CHUNKS)
    return (_iota(shape, 1) % ROW_CHUNKS) == _iota(shape, 0)


def _vaxpy_kernel(idx_ref, w_ref, tab_ref, out_ref, rows_a, rows_b, wrep_ref):
    tb = out_ref.shape[0]
    diag = _chunk_diag_mask()
    shape = (PEER_SLOTS, PEER_SLOTS * ROW_CHUNKS)
    spread = (_iota(shape, 0) == (_iota(shape, 1) // ROW_CHUNKS)).astype(_bf16)
    wrep_ref[...] = _dot_exact_rhs01(w_ref[...], spread)

    def compute(t, m):
        w_row = jnp.broadcast_to(wrep_ref[pl.ds(t, 1), :], diag.shape)
        w_hi, w_lo = _split2(jnp.where(diag, w_row, 0.0))
        out_ref[t] = _dot(w_hi, m) + _dot(w_lo, m)

    _token_loop(tb, idx_ref, tab_ref, rows_a, rows_b, compute)


def _vaxpy(idx, w, tab, *, tb, count):
    return pl.pallas_call(
        _vaxpy_kernel,
        grid=(count // tb,),
        in_specs=[
            pl.BlockSpec((tb, PEER_SLOTS), lambda i: (i, 0), memory_space=pltpu.SMEM),
            pl.BlockSpec((tb, PEER_SLOTS), lambda i: (i, 0)),
            _const_spec(tab.shape),
        ],
        out_specs=pl.BlockSpec((tb, ROW_CHUNKS, 128), lambda i: (i, 0, 0)),
        out_shape=jax.ShapeDtypeStruct((count, ROW_CHUNKS, 128), _f32),
        scratch_shapes=[
            pltpu.VMEM((PEER_SLOTS * PACK_ROWS, 128), jnp.int32),
            pltpu.VMEM((PEER_SLOTS * PACK_ROWS, 128), jnp.int32),
            pltpu.VMEM((tb, PEER_SLOTS * ROW_CHUNKS), _f32),
        ],
        compiler_params=pltpu.CompilerParams(
            dimension_semantics=("parallel",),
            vmem_limit_bytes=VMEM_LIMIT_BYTES),
        name="vaxpy",
    )(idx, w, tab)


def _final_kernel(x_ref, p_ref, g_ref, out_ref, *, normalize):
    x = x_ref[...] + p_ref[...]
    if normalize:
        x = x * lax.rsqrt(jnp.mean(x * x, axis=-1, keepdims=True) + EPS) * g_ref[...]
    out_ref[...] = x


def _final(x, peer, g, *, normalize, tf, first):
    t, d = peer.shape
    off = first // tf
    return pl.pallas_call(
        functools.partial(_final_kernel, normalize=normalize),
        grid=(t // tf,),
        in_specs=[
            pl.BlockSpec((tf, d), lambda i: (i + off, 0)),
            pl.BlockSpec((tf, d), lambda i: (i, 0)),
            _const_spec((1, d)),
        ],
        out_specs=pl.BlockSpec((tf, d), lambda i: (i, 0)),
        out_shape=jax.ShapeDtypeStruct((t, d), _f32),
        compiler_params=pltpu.CompilerParams(dimension_semantics=("parallel",)),
        name="final_norm",
    )(x, peer, g)


def kernel(x, norm_mix_g, w_in, hg_lb_logits, hg_out_norm_g, conv_w, w_branch_hg, w_branch_conv, w_out, norm_ffn_g, peer_w_query, peer_keys1, peer_keys2, peer_u, peer_v, norm_final_g):
    b_, s_, d = x.shape
    depth = w_in.shape[0]
    lb_all = jnp.cumsum(jax.nn.softmax(hg_lb_logits.astype(_f32), axis=0), axis=0)
    n_groups = BATCH_GROUPS if b_ % BATCH_GROUPS == 0 else 1
    bg = b_ // n_groups
    tg = bg * s_
    n_sc = SC_SHARE if SC_SHARE < tg else 0
    n_tc = tg - n_sc
    for l in range(depth):
        wq = peer_w_query[l].astype(_bf16).reshape(d, PEER_HEADS, 2, PEER_HALF).transpose(1, 2, 0, 3)
        v_tab = _pack_table(peer_v[l])
        u_sc = _pack_halves(peer_u[l])
        v_sc = _pack_halves(peer_v[l])
        last = l == depth - 1
        g = norm_final_g[None] if last else jnp.ones((1, d), _f32)
        def front(c, x_in, sc_before=None):
            xc = _mix(x_in, norm_mix_g[l][None], w_in[l].astype(_bf16), lb_all[l][None],
                      hg_out_norm_g[l][None], conv_w[l], w_branch_hg[l].astype(_bf16),
                      w_branch_conv[l].astype(_bf16), w_out[l].astype(_bf16), ts=MIX_TILE)
            xf = xc.reshape(tg, d)
            h, scores = _scores(xf, norm_ffn_g[l][None], wq,
                                peer_keys1[l].astype(_bf16), peer_keys2[l].astype(_bf16), tr=ROUTE_TILE)
            if sc_before is not None:
                scores = lax.optimization_barrier((scores, sc_before))[0]
            idx, gate, acts = _sc_route_dot(scores, u_sc, h)
            return (xf, idx, gate, acts), scores

        def back(xf, idx, gate, acts):
            w = _gate_weights(acts, gate, tw=FINAL_TILE)
            peer_tc = _vaxpy(idx * PACK_ROWS, w, v_tab.reshape(-1, 128), tb=EXPERT_TILE, count=n_tc).reshape(n_tc, d)
            peer_sc = _sc_vaxpy(idx, w, v_sc, first=n_tc, count=n_sc) if n_sc else None
            return peer_tc, peer_sc

        groups = [x[c * bg:(c + 1) * bg] for c in range(n_groups)]
        fronts = {c: front(c, groups[c]) for c in range(min(GROUPS_AHEAD + 1, n_groups))}
        done, peer_tc = [], None
        for c in range(n_groups):
            (xf, idx, gate, acts), _ = fronts.pop(c)
            after = [fronts[c + GROUPS_AHEAD][1]] if c + GROUPS_AHEAD in fronts else []
            after += [peer_tc] if peer_tc is not None else []
            if after:
                acts = lax.optimization_barrier((acts, *after))[0]
            peer_tc, peer_sc = back(xf, idx, gate, acts)
            done.append((xf, peer_tc, peer_sc))
            nxt = c + GROUPS_AHEAD + 1
            if nxt < n_groups:
                fronts[nxt] = front(nxt, lax.optimization_barrier((groups[nxt], peer_tc))[0], peer_sc)
        outs = []
        for xf, p_tc, p_sc in done:
            outs.append(_final(xf, p_tc, g, normalize=last, tf=FINAL_TILE, first=0))
            if p_sc is not None:
                p_sc = lax.optimization_barrier((p_sc, peer_tc))[0]
                outs.append(_final(xf, p_sc, g, normalize=last, tf=FINAL_TILE, first=n_tc))
        x = jnp.concatenate(outs, axis=0).reshape(b_, s_, d)
    return x
```

```python
import functools

import jax
import jax.numpy as jnp
from jax import lax
from jax.experimental import pallas as pl
from jax.experimental.pallas import tpu as pltpu
from jax.experimental.pallas import tpu_sc as plsc

EPS = 1e-6
CHUNK = 64
SUB = 16
HEADS = 8
HEAD_DIM = 64
HG_WIDTH = HEADS * HEAD_DIM
GROUP = 256
N_GROUPS = HG_WIDTH // GROUP
CONV_K = 3
PEER_HEADS = 8
PEER_NKEYS = 128
PEER_HALF = 128
PEER_TOPK = 16
PEER_SLOTS = PEER_HEADS * PEER_TOPK

VMEM_LIMIT_BYTES = 56 * 1024 * 1024

MIX_TILE = 256
ROUTE_TILE = 256
EXPERT_TILE = 128
FINAL_TILE = 256
BATCH_GROUPS = 8
SC_SHARE = 768
GROUPS_AHEAD = 7

_f32 = jnp.float32
_bf16 = jnp.bfloat16


def _dot(a, b):
    return jnp.dot(a, b, preferred_element_type=_f32)


def _dot_nt(a, b):
    return lax.dot_general(a, b, (((1,), (1,)), ((), ())), preferred_element_type=_f32)


def _dot_tn(a, b):
    return lax.dot_general(a, b, (((0,), (0,)), ((), ())), preferred_element_type=_f32)


def _split3(x):
    hi = x.astype(_bf16)
    r1 = x - hi.astype(_f32)
    mid = r1.astype(_bf16)
    lo = (r1 - mid.astype(_f32)).astype(_bf16)
    return hi, mid, lo


def _split2(x):
    hi = x.astype(_bf16)
    lo = (x - hi.astype(_f32)).astype(_bf16)
    return hi, lo


def _dot_exact_rhs01(x, m01):
    hi, mid, lo = _split3(x)
    return _dot(hi, m01) + _dot(mid, m01) + _dot(lo, m01)


def _dot_exact_lhs01(m01, x):
    hi, mid, lo = _split3(x)
    return _dot(m01, hi) + _dot(m01, mid) + _dot(m01, lo)


def _iota(shape, dim):
    return lax.broadcasted_iota(jnp.int32, shape, dim)


def _hgrn2_chunk(q, k, lf, v, state_ref):
    n_sub = CHUNK // SUB
    row = _iota((CHUNK, CHUNK), 0)
    col = _iota((CHUNK, CHUNK), 1)
    tril = (col <= row).astype(_bf16)
    b = _dot_exact_lhs01(tril, lf)

    b_end = [b[(j + 1) * SUB - 1:(j + 1) * SUB, :] for j in range(n_sub)]
    b_end_rows = jnp.concatenate([jnp.broadcast_to(e, (SUB, HG_WIDTH)) for e in b_end], axis=0)
    b_last = b_end[-1]

    q_in = (q * jnp.exp(b)).astype(_bf16)
    k_sub = (k * jnp.exp(b_end_rows - b)).astype(_bf16)
    k_out = (k * jnp.exp(b_last - b)).astype(_bf16)
    q_from = [(q * jnp.exp(jnp.minimum(b - b_end[j], 0.0))).astype(_bf16) for j in range(n_sub - 1)]
    v_b = v.astype(_bf16)

    gr = _iota((GROUP, GROUP), 0) // HEAD_DIM
    gc = _iota((GROUP, GROUP), 1) // HEAD_DIM
    head_mask = gr == gc
    t_blk = _iota((CHUNK, GROUP), 0) // SUB
    s_blk = (_iota((CHUNK, GROUP), 1) % HEAD_DIM) // SUB

    outs = []
    for g in range(N_GROUPS):
        sl = slice(g * GROUP, (g + 1) * GROUP)
        st = state_ref[g]
        o_g = _dot_nt(q_in[:, sl], st.astype(_bf16))

        zero_b = jnp.zeros((), _bf16)
        k_bd = jnp.where(head_mask, jnp.concatenate([k_sub[:, sl]] * (GROUP // CHUNK), axis=0), zero_b)
        v_bd = jnp.where(head_mask, jnp.concatenate([v_b[:, sl]] * (GROUP // CHUNK), axis=0), zero_b)
        q_stack = jnp.concatenate([qf[:, sl] for qf in q_from], axis=0)
        r = _dot_nt(q_stack, k_bd)
        scores = jnp.zeros((CHUNK, GROUP), _f32)
        for j in range(n_sub - 1):
            sel = (s_blk == j) & (t_blk > j)
            scores = jnp.where(sel, r[j * CHUNK:(j + 1) * CHUNK, :], scores)
        o_g = o_g + _dot(scores.astype(_bf16), v_bd)
        outs.append(o_g)

        upd = _dot_tn(v_b[:, sl], k_out[:, sl])
        decay = jnp.exp(b_last[:, sl])
        state_ref[g] = st * decay + jnp.where(head_mask, upd, 0.0)
    o = jnp.concatenate(outs, axis=1)

    ones_bd = ((_iota((HG_WIDTH, HG_WIDTH), 0) // HEAD_DIM)
               == (_iota((HG_WIDTH, HG_WIDTH), 1) // HEAD_DIM)).astype(_bf16)
    t_in_sub = _iota((CHUNK, HG_WIDTH), 0) % SUB
    for lag in range(SUB):
        if lag == 0:
            p = q * k
            v_l = v
        else:
            valid = t_in_sub >= lag
            k_l = pltpu.roll(k, lag, 0)
            b_l = pltpu.roll(b, lag, 0)
            v_l = pltpu.roll(v, lag, 0)
            p = jnp.where(valid, q * k_l * jnp.exp(jnp.minimum(b - b_l, 0.0)), 0.0)
        s_l = _dot(p.astype(_bf16), ones_bd)
        o = o + s_l * v_l
    return o


def _mix_kernel(x_ref, g_ref, win_ref, lb_ref, hgn_ref, convw_ref, pa_ref, pb_ref, wo_ref,
                out_ref, state_ref, carry_ref, q_s, k_s, lf_s, v_s, o_s):
    ts = x_ref.shape[0]
    d_model = x_ref.shape[1]
    w = HG_WIDTH

    @pl.when(pl.program_id(1) == 0)
    def _():
        state_ref[...] = jnp.zeros_like(state_ref)
        carry_ref[...] = jnp.zeros_like(carry_ref)

    x = x_ref[...]
    h = x * lax.rsqrt(jnp.mean(x * x, axis=-1, keepdims=True) + EPS) * g_ref[...]
    hb = h.astype(_bf16)

    def proj(i, width=w):
        return _dot(hb, win_ref[:, i * w:i * w + width])

    lb = lb_ref[...]
    q_s[...] = jax.nn.silu(proj(0)) * (HEAD_DIM ** -0.5)
    forget = lb + (1.0 - lb) * jax.nn.sigmoid(proj(1))
    k_s[...] = 1.0 - forget
    lf_s[...] = jnp.log(forget)
    v_s[...] = proj(2)

    def chunk_body(c, carry):
        rows = pl.ds(pl.multiple_of(c * CHUNK, CHUNK), CHUNK)
        o_s[rows, :] = _hgrn2_chunk(q_s[rows, :], k_s[rows, :], lf_s[rows, :], v_s[rows, :], state_ref)
        return carry

    lax.fori_loop(0, ts // CHUNK, chunk_body, 0)

    o = o_s[...]
    ones_bd = ((_iota((w, w), 0) // HEAD_DIM) == (_iota((w, w), 1) // HEAD_DIM)).astype(_bf16)
    ms = _dot_exact_rhs01(o * o, ones_bd) * (1.0 / HEAD_DIM)
    o = o * lax.rsqrt(ms + EPS) * hgn_ref[...]
    y_a = (o * jax.nn.silu(proj(3))).astype(_bf16)

    u = proj(5) * proj(6)
    prev = carry_ref[...]
    rowi = _iota((ts, w), 0)
    u1 = jnp.where(rowi >= 1, pltpu.roll(u, 1, 0), jnp.broadcast_to(prev[7:8, :], (ts, w)))
    u2 = jnp.where(rowi >= 2, pltpu.roll(u, 2, 0),
                   jnp.where(rowi == 1, jnp.broadcast_to(prev[7:8, :], (ts, w)),
                             jnp.broadcast_to(prev[6:7, :], (ts, w))))
    carry_ref[...] = u[ts - 8:, :]
    cw = convw_ref[...]
    y_b = (proj(4) * (cw[0:1, :] * u2 + cw[1:2, :] * u1 + cw[2:3, :] * u)).astype(_bf16)

    g_a = jax.nn.sigmoid(proj(7, d_model))
    g_b = jax.nn.sigmoid(_dot(hb, win_ref[:, 7 * w + d_model:7 * w + 2 * d_model]))
    merged = g_a * _dot(y_a, pa_ref[...]) + g_b * _dot(y_b, pb_ref[...])
    out_ref[...] = x + _dot(merged.astype(_bf16), wo_ref[...])


def _const_spec(shape):
    nd = len(shape)
    return pl.BlockSpec(shape, lambda *_: (0,) * nd, pipeline_mode=pl.Buffered(1))


def _mix(x, norm_g, w_in, lb, hg_norm_g, conv_w, w_a, w_b, w_o, *, ts):
    b_, s_, d = x.shape
    in_cols = w_in.shape[1]
    w = HG_WIDTH
    grid = (b_, s_ // ts)
    return pl.pallas_call(
        _mix_kernel,
        grid=grid,
        in_specs=[
            pl.BlockSpec((None, ts, d), lambda b, s: (b, s, 0)),
            _const_spec((1, d)),
            _const_spec((d, in_cols)),
            _const_spec((1, w)),
            _const_spec((1, w)),
            _const_spec((CONV_K, w)),
            _const_spec((w, d)),
            _const_spec((w, d)),
            _const_spec((d, d)),
        ],
        out_specs=pl.BlockSpec((None, ts, d), lambda b, s: (b, s, 0)),
        out_shape=jax.ShapeDtypeStruct((b_, s_, d), _f32),
        scratch_shapes=[
            pltpu.VMEM((N_GROUPS, GROUP, GROUP), _f32),
            pltpu.VMEM((8, w), _f32),
            pltpu.VMEM((ts, w), _f32),
            pltpu.VMEM((ts, w), _f32),
            pltpu.VMEM((ts, w), _f32),
            pltpu.VMEM((ts, w), _f32),
            pltpu.VMEM((ts, w), _f32),
        ],
        compiler_params=pltpu.CompilerParams(
            dimension_semantics=("parallel", "arbitrary"),
            vmem_limit_bytes=VMEM_LIMIT_BYTES),
        name="mix",
    )(x, norm_g, w_in, lb, hg_norm_g, conv_w, w_a, w_b, w_o)


def _stair_pairs():
    pairs = [(a, c) for a in range(PEER_TOPK) for c in range(PEER_TOPK) if (a + 1) * (c + 1) <= PEER_TOPK]
    rows = -(-len(pairs) // 8) * 8
    ranks = jnp.arange(PEER_TOPK)[None, :]
    a_col = jnp.asarray([a for a, _ in pairs] + [-1] * (rows - len(pairs)))[:, None]
    c_col = jnp.asarray([c for _, c in pairs] + [-1] * (rows - len(pairs)))[:, None]
    pad = jnp.where(a_col < 0, -jnp.inf, 0.0).astype(_f32)
    return (a_col == ranks).astype(_bf16), (c_col == ranks).astype(_bf16), pad


ROUTE_LANES = 128


def _route_kernel(x_ref, g_ref, wq_ref, k1_ref, k2_ref, sa_ref, sc_ref, pad_ref, h_ref, idx_ref, gate_ref,
                  idx_t, e_t, top_ref):
    tr = x_ref.shape[0]
    n_cand = sa_ref.shape[0]
    x = x_ref[...]
    h = x * lax.rsqrt(jnp.mean(x * x, axis=-1, keepdims=True) + EPS) * g_ref[...]
    h_ref[...] = h
    hb = h.astype(_bf16)

    key_row = _iota((PEER_NKEYS, ROUTE_LANES), 0).astype(_f32)
    cand_row = _iota((n_cand, tr), 0).astype(_f32)
    neg_inf = jnp.float32(-jnp.inf)

    def extract_max(s):
        m = jnp.max(s, axis=0, keepdims=True)
        i = jnp.min(jnp.where(s == m, key_row, float(PEER_NKEYS)), axis=0, keepdims=True)
        return m, i, jnp.where(key_row == i, neg_inf, s)

    def head_body(hd, carry):
        q1 = _dot(hb, wq_ref[hd, 0]).astype(_bf16)
        q2 = _dot(hb, wq_ref[hd, 1]).astype(_bf16)
        s1 = _dot_nt(k1_ref[hd], q1)
        s2 = _dot_nt(k2_ref[hd], q2)

        for lt in range(tr // ROUTE_LANES):
            lanes = slice(lt * ROUTE_LANES, (lt + 1) * ROUTE_LANES)

            def half_body(k, c):
                m1, i1, r1 = extract_max(c[0])
                m2, i2, r2 = extract_max(c[1])
                for j, row in enumerate((m1, i1, m2, i2)):
                    top_ref[lt, j, pl.ds(k, 1), :] = row
                return r1, r2

            lax.fori_loop(0, PEER_TOPK, half_body, (s1[:, lanes], s2[:, lanes]))

        def top(j):
            return jnp.concatenate([top_ref[lt, j] for lt in range(tr // ROUTE_LANES)], axis=1)

        sa = sa_ref[...]
        sc = sc_ref[...]
        cand_s = _dot_exact_lhs01(sa, top(0)) + _dot_exact_lhs01(sc, top(2)) + pad_ref[...]
        cand_i = (_dot(sa, top(1).astype(_bf16)) * float(PEER_NKEYS)
                  + _dot(sc, top(3).astype(_bf16)))

        def pick_body(k, c):
            cand_s, denom, m_first = c
            m = jnp.max(cand_s, axis=0, keepdims=True)
            pos = jnp.min(jnp.where(cand_s == m, cand_row, float(n_cand)), axis=0, keepdims=True)
            hit = cand_row == pos
            eid = jnp.max(jnp.where(hit, cand_i, -1.0), axis=0, keepdims=True)
            m_first = jnp.where(k == 0, m, m_first)
            e = jnp.exp(m - m_first)
            slot = hd * PEER_TOPK + k
            idx_t[pl.ds(slot, 1), :] = eid
            e_t[pl.ds(slot, 1), :] = e
            return jnp.where(hit, neg_inf, cand_s), denom + e, m_first

        zero_row = jnp.zeros((1, tr), _f32)
        _, denom, _ = lax.fori_loop(0, PEER_TOPK, pick_body, (cand_s, zero_row, zero_row))
        rows = pl.ds(pl.multiple_of(hd * PEER_TOPK, PEER_TOPK), PEER_TOPK)
        e_t[rows, :] = e_t[rows, :] / denom
        return carry

    lax.fori_loop(0, PEER_HEADS, head_body, 0)
    idx_ref[...] = idx_t[...].T.astype(jnp.int32)
    gate_ref[...] = e_t[...].T


def _route(x, norm_g, w_query, keys1, keys2, *, tr):
    t, d = x.shape
    sel_a, sel_c, pad = _stair_pairs()
    pad = jnp.broadcast_to(pad, (pad.shape[0], tr))
    return pl.pallas_call(
        _route_kernel,
        grid=(t // tr,),
        in_specs=[
            pl.BlockSpec((tr, d), lambda i: (i, 0)),
            _const_spec((1, d)),
            _const_spec(w_query.shape),
            _const_spec(keys1.shape),
            _const_spec(keys2.shape),
            _const_spec(sel_a.shape),
            _const_spec(sel_c.shape),
            _const_spec(pad.shape),
        ],
        out_specs=[
            pl.BlockSpec((tr, d), lambda i: (i, 0)),
            pl.BlockSpec((tr, PEER_SLOTS), lambda i: (i, 0)),
            pl.BlockSpec((tr, PEER_SLOTS), lambda i: (i, 0)),
        ],
        out_shape=[
            jax.ShapeDtypeStruct((t, d), _f32),
            jax.ShapeDtypeStruct((t, PEER_SLOTS), jnp.int32),
            jax.ShapeDtypeStruct((t, PEER_SLOTS), _f32),
        ],
        scratch_shapes=[
            pltpu.VMEM((PEER_SLOTS, tr), _f32),
            pltpu.VMEM((PEER_SLOTS, tr), _f32),
            pltpu.VMEM((tr // ROUTE_LANES, 4, PEER_TOPK, ROUTE_LANES), _f32),
        ],
        compiler_params=pltpu.CompilerParams(
            dimension_semantics=("parallel",),
            vmem_limit_bytes=VMEM_LIMIT_BYTES),
        name="route",
    )(x, norm_g, w_query, keys1, keys2, sel_a, sel_c, pad)


def _scores_kernel(x_ref, g_ref, wq_ref, k1_ref, k2_ref, h_ref, s_ref):
    x = x_ref[...]
    h = x * lax.rsqrt(jnp.mean(x * x, axis=-1, keepdims=True) + EPS) * g_ref[...]
    h_ref[...] = h
    hb = h.astype(_bf16)
    for hd in range(PEER_HEADS):
        for half, keys_ref in enumerate((k1_ref, k2_ref)):
            q = _dot(hb, wq_ref[hd, half]).astype(_bf16)
            col = (2 * hd + half) * PEER_NKEYS
            s_ref[:, col:col + PEER_NKEYS] = _dot_nt(q, keys_ref[hd])


def _scores(x, norm_g, w_query, keys1, keys2, *, tr):
    t, d = x.shape
    width = 2 * PEER_HEADS * PEER_NKEYS
    return pl.pallas_call(
        _scores_kernel,
        grid=(t // tr,),
        in_specs=[
            pl.BlockSpec((tr, d), lambda i: (i, 0)),
            _const_spec((1, d)),
            _const_spec(w_query.shape),
            _const_spec(keys1.shape),
            _const_spec(keys2.shape),
        ],
        out_specs=[
            pl.BlockSpec((tr, d), lambda i: (i, 0)),
            pl.BlockSpec((tr, width), lambda i: (i, 0)),
        ],
        out_shape=[
            jax.ShapeDtypeStruct((t, d), _f32),
            jax.ShapeDtypeStruct((t, width), _f32),
        ],
        compiler_params=pltpu.CompilerParams(
            dimension_semantics=("parallel",),
            vmem_limit_bytes=VMEM_LIMIT_BYTES),
        name="scores",
    )(x, norm_g, w_query, keys1, keys2)


SC_CORES = 2
SC_SUBCORES = 16
SC_LANES = 16
SC_WORKERS = SC_CORES * SC_SUBCORES
SC_TOKENS = 8
SC_GATHER = 32
SC_BLOCK = 8
SC_UNROLL = 2
SC_COLS = 8
HI_MASK = -65536


def _sc_mesh():
    return plsc.VectorSubcoreMesh(core_axis_name="c", subcore_axis_name="s")


def _sc_worker_base(per_worker):
    return (lax.axis_index("s") * SC_CORES + lax.axis_index("c")) * per_worker


def _sc_gather_loop(tab_hbm, idx_v, bufs, consume):
    n_parts = PEER_SLOTS // SC_GATHER
    n_gathers = SC_TOKENS * n_parts

    def gather(g, parity):
        rows, sem = bufs[parity]
        i = g // n_parts
        col = pl.multiple_of((g % n_parts) * SC_GATHER, SC_GATHER)
        return pltpu.make_async_copy(tab_hbm.at[idx_v.at[i, pl.ds(col, SC_GATHER)]], rows, sem)

    gather(0, 0).start()

    @pl.loop(0, n_gathers // 2)
    def _(pair):
        g = 2 * pair
        gather(g + 1, 1).start()
        gather(g, 0).wait()
        consume(g // n_parts, g % n_parts, bufs[0][0])

        @pl.when(g + 2 < n_gathers)
        def _():
            gather(g + 2, 0).start()

        gather(g + 1, 1).wait()
        consume((g + 1) // n_parts, (g + 1) % n_parts, bufs[1][0])


def _pack_halves(tab):
    half = tab.shape[1] // 2
    bits = lax.bitcast_convert_type(tab.astype(_bf16), jnp.uint16).astype(jnp.uint32)
    return lax.bitcast_convert_type(bits[:, :half] | (bits[:, half:] << 16), jnp.int32)


def _sc_unpack(words):
    lo = lax.bitcast_convert_type(words << 16, _f32)
    hi = lax.bitcast_convert_type(words & HI_MASK, _f32)
    return lo, hi


def _stair_vectors():
    pairs = [(a, c) for a in range(PEER_TOPK) for c in range(PEER_TOPK) if (a + 1) * (c + 1) <= PEER_TOPK]
    n = -(-len(pairs) // SC_LANES)
    fill = n * SC_LANES - len(pairs)
    a = jnp.asarray([p[0] for p in pairs] + [0] * fill, jnp.int32).reshape(n, SC_LANES)
    c = jnp.asarray([p[1] for p in pairs] + [0] * fill, jnp.int32).reshape(n, SC_LANES)
    pad = jnp.asarray([0.0] * len(pairs) + [-jnp.inf] * fill, _f32).reshape(n, SC_LANES)
    return a, c, pad


def _sc_route_dot(scores, tab, h):
    t, width = scores.shape
    d = h.shape[1]
    half_d = d // 2
    per_worker = t // SC_WORKERS
    pair_a, pair_c, pair_pad = _stair_vectors()
    n_cand = pair_a.shape[0]
    n_vec = PEER_NKEYS // SC_LANES

    @functools.partial(
        pl.kernel, mesh=_sc_mesh(),
        out_type=(jax.ShapeDtypeStruct((t, PEER_SLOTS), jnp.int32),
                  jax.ShapeDtypeStruct((t, PEER_SLOTS), _f32),
                  jax.ShapeDtypeStruct((t, PEER_SLOTS), _f32)),
        scratch_types=[
            pltpu.VMEM((SC_TOKENS, width), _f32),
            pltpu.VMEM((SC_TOKENS, PEER_SLOTS), jnp.int32),
            pltpu.VMEM((SC_TOKENS, PEER_SLOTS), _f32),
            pltpu.VMEM((n_cand, SC_LANES), jnp.int32),
            pltpu.VMEM((n_cand, SC_LANES), jnp.int32),
            pltpu.VMEM((n_cand, SC_LANES), _f32),
            pltpu.VMEM((4, SC_LANES), _f32),
            pltpu.VMEM((SC_TOKENS, d), _f32),
            pltpu.VMEM((SC_TOKENS, PEER_SLOTS), _f32),
            pltpu.VMEM((SC_GATHER, half_d), jnp.int32),
            pltpu.VMEM((SC_GATHER, half_d), jnp.int32),
            pltpu.SemaphoreType.DMA,
            pltpu.SemaphoreType.DMA,
        ],
        compiler_params=pltpu.CompilerParams(needs_layout_passes=False),
        name="sc_route_dot",
    )
    def body(s_hbm, a_hbm, c_hbm, pad_hbm, tab_hbm, h_hbm, idx_hbm, gate_hbm, acts_hbm,
             s_v, idx_v, gate_v, a_v, c_v, pad_v, top_v, h_v, acts_v, rows0, rows1, sem0, sem1):
        base = _sc_worker_base(per_worker)
        lane = lax.iota(jnp.int32, SC_LANES)
        pltpu.sync_copy(a_hbm, a_v)
        pltpu.sync_copy(c_hbm, c_v)
        pltpu.sync_copy(pad_hbm, pad_v)

        def dots(i, part, rows):
            def block_body(blk, carry):
                row0 = blk * SC_LANES
                outv = jnp.zeros((SC_LANES,), _f32)
                for sub in range(SC_LANES // SC_BLOCK):
                    def chunk_body(jj, accs):
                        accs = list(accs)
                        for u in range(SC_UNROLL):
                            off = pl.multiple_of((jj * SC_UNROLL + u) * SC_LANES, SC_LANES)
                            h_lo = h_v[i, pl.ds(off, SC_LANES)]
                            h_hi = h_v[i, pl.ds(half_d + off, SC_LANES)]
                            for e in range(SC_BLOCK):
                                lo, hi = _sc_unpack(rows[row0 + sub * SC_BLOCK + e, pl.ds(off, SC_LANES)])
                                accs[e] = accs[e] + (lo * h_lo + hi * h_hi)
                        return tuple(accs)

                    accs = lax.fori_loop(0, half_d // SC_LANES // SC_UNROLL, chunk_body,
                                         tuple(jnp.zeros((SC_LANES,), _f32) for _ in range(SC_BLOCK)))
                    for e in range(SC_BLOCK):
                        outv = jnp.where(lane == sub * SC_BLOCK + e, jnp.sum(accs[e]), outv)
                col = pl.multiple_of(part * SC_GATHER + row0, SC_LANES)
                acts_v[i, pl.ds(col, SC_LANES)] = outv
                return carry

            lax.fori_loop(0, SC_GATHER // SC_LANES, block_body, 0)

        def sort_desc(k, v):
            return plsc.sort_key_val(k, v, descending=True)

        def merge(x, y):
            yk, yv = lax.rev(y[0], (0,)), lax.rev(y[1], (0,))
            take = x[0] >= yk
            return sort_desc(jnp.where(take, x[0], yk), jnp.where(take, x[1], yv))

        def top_of(vectors):
            while len(vectors) > 1:
                vectors = [merge(vectors[j], vectors[j + 1]) for j in range(0, len(vectors), 2)]
            return vectors[0]

        def head_body(i, hd):
            halves = []
            for half in range(2):
                col = (2 * hd + half) * PEER_NKEYS
                vecs = [sort_desc(s_v[i, pl.ds(pl.multiple_of(col + j * SC_LANES, SC_LANES), SC_LANES)],
                                  lane + j * SC_LANES) for j in range(n_vec)]
                halves.append(top_of(vecs))
            (v1, i1), (v2, i2) = halves
            top_v[0, :] = v1
            top_v[1, :] = i1.astype(_f32)
            top_v[2, :] = v2
            top_v[3, :] = i2.astype(_f32)

            def pick(row, pos):
                return plsc.load_gather(top_v, [jnp.full((SC_LANES,), row, jnp.int32), pos])

            cands = []
            for q in range(n_cand):
                a, c = a_v[q, :], c_v[q, :]
                cs = pick(0, a) + pick(2, c) + pad_v[q, :]
                ci = pick(1, a) * float(PEER_NKEYS) + pick(3, c)
                cands.append(sort_desc(cs, ci))
            top_s, top_i = top_of(cands)
            e = jnp.exp(top_s - jnp.max(top_s))
            slots = pl.ds(pl.multiple_of(hd * PEER_TOPK, PEER_TOPK), PEER_TOPK)
            idx_v[i, slots] = top_i.astype(jnp.int32)
            gate_v[i, slots] = e / jnp.sum(e)

        @pl.loop(0, per_worker // SC_TOKENS)
        def _(step):
            tok = pl.multiple_of(base + step * SC_TOKENS, SC_TOKENS)
            pltpu.sync_copy(s_hbm.at[pl.ds(tok, SC_TOKENS)], s_v)
            pltpu.sync_copy(h_hbm.at[pl.ds(tok, SC_TOKENS)], h_v)

            @pl.loop(0, SC_TOKENS * PEER_HEADS)
            def _(n):
                head_body(n // PEER_HEADS, n % PEER_HEADS)

            pltpu.sync_copy(idx_v, idx_hbm.at[pl.ds(tok, SC_TOKENS)])
            pltpu.sync_copy(gate_v, gate_hbm.at[pl.ds(tok, SC_TOKENS)])
            _sc_gather_loop(tab_hbm, idx_v, ((rows0, sem0), (rows1, sem1)), dots)
            pltpu.sync_copy(acts_v, acts_hbm.at[pl.ds(tok, SC_TOKENS)])

    return body(scores, pair_a, pair_c, pair_pad, tab, h)


def _sc_vaxpy(idx, w, tab, *, first, count):
    half = tab.shape[1]
    d = 2 * half
    per_worker = count // SC_WORKERS
    span = SC_COLS * SC_LANES

    @functools.partial(
        pl.kernel, mesh=_sc_mesh(),
        out_type=jax.ShapeDtypeStruct((count, d), _f32),
        scratch_types=[
            pltpu.VMEM((SC_TOKENS, PEER_SLOTS), jnp.int32),
            pltpu.VMEM((SC_TOKENS, PEER_SLOTS), _f32),
            pltpu.VMEM((SC_TOKENS, d), _f32),
            pltpu.VMEM((SC_GATHER, half), jnp.int32),
            pltpu.VMEM((SC_GATHER, half), jnp.int32),
            pltpu.SemaphoreType.DMA,
            pltpu.SemaphoreType.DMA,
        ],
        compiler_params=pltpu.CompilerParams(needs_layout_passes=False),
        name="sc_vaxpy",
    )
    def body(idx_hbm, w_hbm, tab_hbm, out_hbm, idx_v, w_v, out_v, rows0, rows1, sem0, sem1):
        base = _sc_worker_base(per_worker)

        def accumulate(i, part, rows):
            i_vec = jnp.full((SC_LANES,), i, jnp.int32)

            def span_body(cq, carry):
                def cols(c, offset=0):
                    return pl.ds(pl.multiple_of(offset + cq * span + c * SC_LANES, SC_LANES), SC_LANES)

                def expert_body(e, accs):
                    k_vec = jnp.full((SC_LANES,), part * SC_GATHER + e, jnp.int32)
                    wv = plsc.load_gather(w_v, [i_vec, k_vec])
                    new = []
                    for c in range(SC_COLS):
                        lo, hi = _sc_unpack(rows[e, cols(c)])
                        new += [accs[2 * c] + lo * wv, accs[2 * c + 1] + hi * wv]
                    return tuple(new)

                init = []
                for c in range(SC_COLS):
                    init += [out_v[i, cols(c)], out_v[i, cols(c, half)]]
                accs = lax.fori_loop(0, SC_GATHER, expert_body, tuple(init))
                for c in range(SC_COLS):
                    out_v[i, cols(c)] = accs[2 * c]
                    out_v[i, cols(c, half)] = accs[2 * c + 1]
                return carry

            lax.fori_loop(0, half // span, span_body, 0)

        @pl.loop(0, per_worker // SC_TOKENS)
        def _(step):
            off = pl.multiple_of(base + step * SC_TOKENS, SC_TOKENS)
            pltpu.sync_copy(idx_hbm.at[pl.ds(first + off, SC_TOKENS)], idx_v)
            pltpu.sync_copy(w_hbm.at[pl.ds(first + off, SC_TOKENS)], w_v)

            @pl.loop(0, SC_TOKENS)
            def _(i):
                @pl.loop(0, d // SC_LANES)
                def _(j):
                    out_v[i, pl.ds(pl.multiple_of(j * SC_LANES, SC_LANES), SC_LANES)] = (
                        jnp.zeros((SC_LANES,), _f32))

            _sc_gather_loop(tab_hbm, idx_v, ((rows0, sem0), (rows1, sem1)), accumulate)
            pltpu.sync_copy(out_v, out_hbm.at[pl.ds(off, SC_TOKENS)])

    return body(idx, w, tab)


ROW_CHUNKS = 8
PACK_ROWS = ROW_CHUNKS // 2


def _pack_table(tab):
    n, d = tab.shape
    bits = lax.bitcast_convert_type(tab.astype(_bf16), jnp.uint16).astype(jnp.uint32)
    bits = bits.reshape(n, PACK_ROWS, 2, d // ROW_CHUNKS)
    word = bits[:, :, 0, :] | (bits[:, :, 1, :] << 16)
    return lax.bitcast_convert_type(word, jnp.int32)


def _gate_weights_kernel(acts_ref, gate_ref, w_ref):
    a = acts_ref[...]
    gelu = 0.5 * a * (1.0 + lax.erf(a * (2.0 ** -0.5)))
    w_ref[...] = gate_ref[...] * gelu


def _gate_weights(acts, gate, *, tw):
    t, n = acts.shape
    spec = pl.BlockSpec((tw, n), lambda i: (i, 0))
    return pl.pallas_call(
        _gate_weights_kernel,
        grid=(t // tw,),
        in_specs=[spec, spec],
        out_specs=spec,
        out_shape=jax.ShapeDtypeStruct((t, n), _f32),
        compiler_params=pltpu.CompilerParams(dimension_semantics=("parallel",)),
        name="gate_weights",
    )(acts, gate)


def _gather_rows(idx_ref, t, tab_ref, rows_ref):
    for k in range(PEER_SLOTS):
        row = pl.multiple_of(idx_ref[t, k], PACK_ROWS)
        rows_ref[k * PACK_ROWS:(k + 1) * PACK_ROWS, :] = tab_ref[pl.ds(row, PACK_ROWS), :]


def _rows_matrix(rows_ref):
    return pltpu.bitcast(rows_ref[...], _bf16)


def _token_loop(tb, idx_ref, tab_ref, rows_a, rows_b, compute):
    _gather_rows(idx_ref, 0, tab_ref, rows_a)

    def pair_body(i, carry):
        t0 = 2 * i
        _gather_rows(idx_ref, t0 + 1, tab_ref, rows_b)
        compute(t0, _rows_matrix(rows_a))
        _gather_rows(idx_ref, jnp.minimum(t0 + 2, tb - 1), tab_ref, rows_a)
        compute(t0 + 1, _rows_matrix(rows_b))
        return carry

    lax.fori_loop(0, tb // 2, pair_body, 0)


def _chunk_diag_mask():
    shape = (ROW_CHUNKS, PEER_SLOTS * ROW_CHUNKS)
    return (_iota(shape, 1) % ROW_CHUNKS) == _iota(shape, 0)


def _vaxpy_kernel(idx_ref, w_ref, tab_ref, out_ref, rows_a, rows_b, wrep_ref):
    tb = out_ref.shape[0]
    diag = _chunk_diag_mask()
    shape = (PEER_SLOTS, PEER_SLOTS * ROW_CHUNKS)
    spread = (_iota(shape, 0) == (_iota(shape, 1) // ROW_CHUNKS)).astype(_bf16)
    wrep_ref[...] = _dot_exact_rhs01(w_ref[...], spread)

    def compute(t, m):
        w_row = jnp.broadcast_to(wrep_ref[pl.ds(t, 1), :], diag.shape)
        w_hi, w_lo = _split2(jnp.where(diag, w_row, 0.0))
        out_ref[t] = _dot(w_hi, m) + _dot(w_lo, m)

    _token_loop(tb, idx_ref, tab_ref, rows_a, rows_b, compute)


def _vaxpy(idx, w, tab, *, tb, count):
    return pl.pallas_call(
        _vaxpy_kernel,
        grid=(count // tb,),
        in_specs=[
            pl.BlockSpec((tb, PEER_SLOTS), lambda i: (i, 0), memory_space=pltpu.SMEM),
            pl.BlockSpec((tb, PEER_SLOTS), lambda i: (i, 0)),
            _const_spec(tab.shape),
        ],
        out_specs=pl.BlockSpec((tb, ROW_CHUNKS, 128), lambda i: (i, 0, 0)),
        out_shape=jax.ShapeDtypeStruct((count, ROW_CHUNKS, 128), _f32),
        scratch_shapes=[
            pltpu.VMEM((PEER_SLOTS * PACK_ROWS, 128), jnp.int32),
            pltpu.VMEM((PEER_SLOTS * PACK_ROWS, 128), jnp.int32),
            pltpu.VMEM((tb, PEER_SLOTS * ROW_CHUNKS), _f32),
        ],
        compiler_params=pltpu.CompilerParams(
            dimension_semantics=("parallel",),
            vmem_limit_bytes=VMEM_LIMIT_BYTES),
        name="vaxpy",
    )(idx, w, tab)


def _final_kernel(x_ref, p_ref, g_ref, out_ref, *, normalize):
    x = x_ref[...] + p_ref[...]
    if normalize:
        x = x * lax.rsqrt(jnp.mean(x * x, axis=-1, keepdims=True) + EPS) * g_ref[...]
    out_ref[...] = x


def _final(x, peer, g, *, normalize, tf, first):
    t, d = peer.shape
    off = first // tf
    return pl.pallas_call(
        functools.partial(_final_kernel, normalize=normalize),
        grid=(t // tf,),
        in_specs=[
            pl.BlockSpec((tf, d), lambda i: (i + off, 0)),
            pl.BlockSpec((tf, d), lambda i: (i, 0)),
            _const_spec((1, d)),
        ],
        out_specs=pl.BlockSpec((tf, d), lambda i: (i, 0)),
        out_shape=jax.ShapeDtypeStruct((t, d), _f32),
        compiler_params=pltpu.CompilerParams(dimension_semantics=("parallel",)),
        name="final_norm",
    )(x, peer, g)


def kernel(x, norm_mix_g, w_in, hg_lb_logits, hg_out_norm_g, conv_w, w_branch_hg, w_branch_conv, w_out, norm_ffn_g, peer_w_query, peer_keys1, peer_keys2, peer_u, peer_v, norm_final_g):
    b_, s_, d = x.shape
    depth = w_in.shape[0]
    lb_all = jnp.cumsum(jax.nn.softmax(hg_lb_logits.astype(_f32), axis=0), axis=0)
    n_groups = BATCH_GROUPS if b_ % BATCH_GROUPS == 0 else 1
    bg = b_ // n_groups
    tg = bg * s_
    n_sc = SC_SHARE if SC_SHARE < tg else 0
    n_tc = tg - n_sc
    for l in range(depth):
        wq = peer_w_query[l].astype(_bf16).reshape(d, PEER_HEADS, 2, PEER_HALF).transpose(1, 2, 0, 3)
        v_tab = _pack_table(peer_v[l])
        u_sc = _pack_halves(peer_u[l])
        v_sc = _pack_halves(peer_v[l])
        last = l == depth - 1
        g = norm_final_g[None] if last else jnp.ones((1, d), _f32)
        def front(c, x_in, sc_before=None):
            xc = _mix(x_in, norm_mix_g[l][None], w_in[l].astype(_bf16), lb_all[l][None],
                      hg_out_norm_g[l][None], conv_w[l], w_branch_hg[l].astype(_bf16),
                      w_branch_conv[l].astype(_bf16), w_out[l].astype(_bf16), ts=MIX_TILE)
            xf = xc.reshape(tg, d)
            h, scores = _scores(xf, norm_ffn_g[l][None], wq,
                                peer_keys1[l].astype(_bf16), peer_keys2[l].astype(_bf16), tr=ROUTE_TILE)
            if sc_before is not None:
                scores = lax.optimization_barrier((scores, sc_before))[0]
            idx, gate, acts = _sc_route_dot(scores, u_sc, h)
            return (xf, idx, gate, acts), scores

        def back(xf, idx, gate, acts):
            w = _gate_weights(acts, gate, tw=FINAL_TILE)
            peer_tc = _vaxpy(idx * PACK_ROWS, w, v_tab.reshape(-1, 128), tb=EXPERT_TILE, count=n_tc).reshape(n_tc, d)
            peer_sc = _sc_vaxpy(idx, w, v_sc, first=n_tc, count=n_sc) if n_sc else None
            return peer_tc, peer_sc

        groups = [x[c * bg:(c + 1) * bg] for c in range(n_groups)]
        fronts = {c: front(c, groups[c]) for c in range(min(GROUPS_AHEAD + 1, n_groups))}
        done, peer_tc = [], None
        for c in range(n_groups):
            (xf, idx, gate, acts), _ = fronts.pop(c)
            after = [fronts[c + GROUPS_AHEAD][1]] if c + GROUPS_AHEAD in fronts else []
            after += [peer_tc] if peer_tc is not None else []
            if after:
                acts = lax.optimization_barrier((acts, *after))[0]
            peer_tc, peer_sc = back(xf, idx, gate, acts)
            done.append((xf, peer_tc, peer_sc))
            nxt = c + GROUPS_AHEAD + 1
            if nxt < n_groups:
                fronts[nxt] = front(nxt, lax.optimization_barrier((groups[nxt], peer_tc))[0], peer_sc)
        outs = []
        for xf, p_tc, p_sc in done:
            outs.append(_final(xf, p_tc, g, normalize=last, tf=FINAL_TILE, first=0))
            if p_sc is not None:
                p_sc = lax.optimization_barrier((p_sc, peer_tc))[0]
                outs.append(_final(xf, p_sc, g, normalize=last, tf=FINAL_TILE, first=n_tc))
        x = jnp.concatenate(outs, axis=0).reshape(b_, s_, d)
    return x
```

```python
import functools

import jax
import jax.numpy as jnp
from jax import lax
from jax.experimental import pallas as pl
from jax.experimental.pallas import tpu as pltpu
from jax.experimental.pallas import tpu_sc as plsc

EPS = 1e-6
CHUNK = 64
SUB = 16
HEADS = 8
HEAD_DIM = 64
HG_WIDTH = HEADS * HEAD_DIM
GROUP = 256
N_GROUPS = HG_WIDTH // GROUP
CONV_K = 3
PEER_HEADS = 8
PEER_NKEYS = 128
PEER_HALF = 128
PEER_TOPK = 16
PEER_SLOTS = PEER_HEADS * PEER_TOPK

VMEM_LIMIT_BYTES = 56 * 1024 * 1024

MIX_TILE = 256
ROUTE_TILE = 256
EXPERT_TILE = 128
FINAL_TILE = 256
BATCH_GROUPS = 8
SC_SHARE = 512
GROUPS_AHEAD = 7

_f32 = jnp.float32
_bf16 = jnp.bfloat16


def _dot(a, b):
    return jnp.dot(a, b, preferred_element_type=_f32)


def _dot_nt(a, b):
    return lax.dot_general(a, b, (((1,), (1,)), ((), ())), preferred_element_type=_f32)


def _dot_tn(a, b):
    return lax.dot_general(a, b, (((0,), (0,)), ((), ())), preferred_element_type=_f32)


def _split3(x):
    hi = x.astype(_bf16)
    r1 = x - hi.astype(_f32)
    mid = r1.astype(_bf16)
    lo = (r1 - mid.astype(_f32)).astype(_bf16)
    return hi, mid, lo


def _split2(x):
    hi = x.astype(_bf16)
    lo = (x - hi.astype(_f32)).astype(_bf16)
    return hi, lo


def _dot_exact_rhs01(x, m01):
    hi, mid, lo = _split3(x)
    return _dot(hi, m01) + _dot(mid, m01) + _dot(lo, m01)


def _dot_exact_lhs01(m01, x):
    hi, mid, lo = _split3(x)
    return _dot(m01, hi) + _dot(m01, mid) + _dot(m01, lo)


def _iota(shape, dim):
    return lax.broadcasted_iota(jnp.int32, shape, dim)


def _hgrn2_chunk(q, k, lf, v, state_ref):
    n_sub = CHUNK // SUB
    row = _iota((CHUNK, CHUNK), 0)
    col = _iota((CHUNK, CHUNK), 1)
    tril = (col <= row).astype(_bf16)
    b = _dot_exact_lhs01(tril, lf)

    b_end = [b[(j + 1) * SUB - 1:(j + 1) * SUB, :] for j in range(n_sub)]
    b_end_rows = jnp.concatenate([jnp.broadcast_to(e, (SUB, HG_WIDTH)) for e in b_end], axis=0)
    b_last = b_end[-1]

    q_in = (q * jnp.exp(b)).astype(_bf16)
    k_sub = (k * jnp.exp(b_end_rows - b)).astype(_bf16)
    k_out = (k * jnp.exp(b_last - b)).astype(_bf16)
    q_from = [(q * jnp.exp(jnp.minimum(b - b_end[j], 0.0))).astype(_bf16) for j in range(n_sub - 1)]
    v_b = v.astype(_bf16)

    gr = _iota((GROUP, GROUP), 0) // HEAD_DIM
    gc = _iota((GROUP, GROUP), 1) // HEAD_DIM
    head_mask = gr == gc
    t_blk = _iota((CHUNK, GROUP), 0) // SUB
    s_blk = (_iota((CHUNK, GROUP), 1) % HEAD_DIM) // SUB

    outs = []
    for g in range(N_GROUPS):
        sl = slice(g * GROUP, (g + 1) * GROUP)
        st = state_ref[g]
        o_g = _dot_nt(q_in[:, sl], st.astype(_bf16))

        zero_b = jnp.zeros((), _bf16)
        k_bd = jnp.where(head_mask, jnp.concatenate([k_sub[:, sl]] * (GROUP // CHUNK), axis=0), zero_b)
        v_bd = jnp.where(head_mask, jnp.concatenate([v_b[:, sl]] * (GROUP // CHUNK), axis=0), zero_b)
        q_stack = jnp.concatenate([qf[:, sl] for qf in q_from], axis=0)
        r = _dot_nt(q_stack, k_bd)
        scores = jnp.zeros((CHUNK, GROUP), _f32)
        for j in range(n_sub - 1):
            sel = (s_blk == j) & (t_blk > j)
            scores = jnp.where(sel, r[j * CHUNK:(j + 1) * CHUNK, :], scores)
        o_g = o_g + _dot(scores.astype(_bf16), v_bd)
        outs.append(o_g)

        upd = _dot_tn(v_b[:, sl], k_out[:, sl])
        decay = jnp.exp(b_last[:, sl])
        state_ref[g] = st * decay + jnp.where(head_mask, upd, 0.0)
    o = jnp.concatenate(outs, axis=1)

    ones_bd = ((_iota((HG_WIDTH, HG_WIDTH), 0) // HEAD_DIM)
               == (_iota((HG_WIDTH, HG_WIDTH), 1) // HEAD_DIM)).astype(_bf16)
    t_in_sub = _iota((CHUNK, HG_WIDTH), 0) % SUB
    for lag in range(SUB):
        if lag == 0:
            p = q * k
            v_l = v
        else:
            valid = t_in_sub >= lag
            k_l = pltpu.roll(k, lag, 0)
            b_l = pltpu.roll(b, lag, 0)
            v_l = pltpu.roll(v, lag, 0)
            p = jnp.where(valid, q * k_l * jnp.exp(jnp.minimum(b - b_l, 0.0)), 0.0)
        s_l = _dot(p.astype(_bf16), ones_bd)
        o = o + s_l * v_l
    return o


def _mix_kernel(x_ref, g_ref, win_ref, lb_ref, hgn_ref, convw_ref, pa_ref, pb_ref, wo_ref,
                out_ref, state_ref, carry_ref, q_s, k_s, lf_s, v_s, o_s):
    ts = x_ref.shape[0]
    d_model = x_ref.shape[1]
    w = HG_WIDTH

    @pl.when(pl.program_id(1) == 0)
    def _():
        state_ref[...] = jnp.zeros_like(state_ref)
        carry_ref[...] = jnp.zeros_like(carry_ref)

    x = x_ref[...]
    h = x * lax.rsqrt(jnp.mean(x * x, axis=-1, keepdims=True) + EPS) * g_ref[...]
    hb = h.astype(_bf16)

    def proj(i, width=w):
        return _dot(hb, win_ref[:, i * w:i * w + width])

    lb = lb_ref[...]
    q_s[...] = jax.nn.silu(proj(0)) * (HEAD_DIM ** -0.5)
    forget = lb + (1.0 - lb) * jax.nn.sigmoid(proj(1))
    k_s[...] = 1.0 - forget
    lf_s[...] = jnp.log(forget)
    v_s[...] = proj(2)

    def chunk_body(c, carry):
        rows = pl.ds(pl.multiple_of(c * CHUNK, CHUNK), CHUNK)
        o_s[rows, :] = _hgrn2_chunk(q_s[rows, :], k_s[rows, :], lf_s[rows, :], v_s[rows, :], state_ref)
        return carry

    lax.fori_loop(0, ts // CHUNK, chunk_body, 0)

    o = o_s[...]
    ones_bd = ((_iota((w, w), 0) // HEAD_DIM) == (_iota((w, w), 1) // HEAD_DIM)).astype(_bf16)
    ms = _dot_exact_rhs01(o * o, ones_bd) * (1.0 / HEAD_DIM)
    o = o * lax.rsqrt(ms + EPS) * hgn_ref[...]
    y_a = (o * jax.nn.silu(proj(3))).astype(_bf16)

    u = proj(5) * proj(6)
    prev = carry_ref[...]
    rowi = _iota((ts, w), 0)
    u1 = jnp.where(rowi >= 1, pltpu.roll(u, 1, 0), jnp.broadcast_to(prev[7:8, :], (ts, w)))
    u2 = jnp.where(rowi >= 2, pltpu.roll(u, 2, 0),
                   jnp.where(rowi == 1, jnp.broadcast_to(prev[7:8, :], (ts, w)),
                             jnp.broadcast_to(prev[6:7, :], (ts, w))))
    carry_ref[...] = u[ts - 8:, :]
    cw = convw_ref[...]
    y_b = (proj(4) * (cw[0:1, :] * u2 + cw[1:2, :] * u1 + cw[2:3, :] * u)).astype(_bf16)

    g_a = jax.nn.sigmoid(proj(7, d_model))
    g_b = jax.nn.sigmoid(_dot(hb, win_ref[:, 7 * w + d_model:7 * w + 2 * d_model]))
    merged = g_a * _dot(y_a, pa_ref[...]) + g_b * _dot(y_b, pb_ref[...])
    out_ref[...] = x + _dot(merged.astype(_bf16), wo_ref[...])


def _const_spec(shape):
    nd = len(shape)
    return pl.BlockSpec(shape, lambda *_: (0,) * nd, pipeline_mode=pl.Buffered(1))


def _mix(x, norm_g, w_in, lb, hg_norm_g, conv_w, w_a, w_b, w_o, *, ts):
    b_, s_, d = x.shape
    in_cols = w_in.shape[1]
    w = HG_WIDTH
    grid = (b_, s_ // ts)
    return pl.pallas_call(
        _mix_kernel,
        grid=grid,
        in_specs=[
            pl.BlockSpec((None, ts, d), lambda b, s: (b, s, 0)),
            _const_spec((1, d)),
            _const_spec((d, in_cols)),
            _const_spec((1, w)),
            _const_spec((1, w)),
            _const_spec((CONV_K, w)),
            _const_spec((w, d)),
            _const_spec((w, d)),
            _const_spec((d, d)),
        ],
        out_specs=pl.BlockSpec((None, ts, d), lambda b, s: (b, s, 0)),
        out_shape=jax.ShapeDtypeStruct((b_, s_, d), _f32),
        scratch_shapes=[
            pltpu.VMEM((N_GROUPS, GROUP, GROUP), _f32),
            pltpu.VMEM((8, w), _f32),
            pltpu.VMEM((ts, w), _f32),
            pltpu.VMEM((ts, w), _f32),
            pltpu.VMEM((ts, w), _f32),
            pltpu.VMEM((ts, w), _f32),
            pltpu.VMEM((ts, w), _f32),
        ],
        compiler_params=pltpu.CompilerParams(
            dimension_semantics=("parallel", "arbitrary"),
            vmem_limit_bytes=VMEM_LIMIT_BYTES),
        name="mix",
    )(x, norm_g, w_in, lb, hg_norm_g, conv_w, w_a, w_b, w_o)


def _stair_pairs():
    pairs = [(a, c) for a in range(PEER_TOPK) for c in range(PEER_TOPK) if (a + 1) * (c + 1) <= PEER_TOPK]
    rows = -(-len(pairs) // 8) * 8
    ranks = jnp.arange(PEER_TOPK)[None, :]
    a_col = jnp.asarray([a for a, _ in pairs] + [-1] * (rows - len(pairs)))[:, None]
    c_col = jnp.asarray([c for _, c in pairs] + [-1] * (rows - len(pairs)))[:, None]
    pad = jnp.where(a_col < 0, -jnp.inf, 0.0).astype(_f32)
    return (a_col == ranks).astype(_bf16), (c_col == ranks).astype(_bf16), pad


ROUTE_LANES = 128


def _route_kernel(x_ref, g_ref, wq_ref, k1_ref, k2_ref, sa_ref, sc_ref, pad_ref, h_ref, idx_ref, gate_ref,
                  idx_t, e_t, top_ref):
    tr = x_ref.shape[0]
    n_cand = sa_ref.shape[0]
    x = x_ref[...]
    h = x * lax.rsqrt(jnp.mean(x * x, axis=-1, keepdims=True) + EPS) * g_ref[...]
    h_ref[...] = h
    hb = h.astype(_bf16)

    key_row = _iota((PEER_NKEYS, ROUTE_LANES), 0).astype(_f32)
    cand_row = _iota((n_cand, tr), 0).astype(_f32)
    neg_inf = jnp.float32(-jnp.inf)

    def extract_max(s):
        m = jnp.max(s, axis=0, keepdims=True)
        i = jnp.min(jnp.where(s == m, key_row, float(PEER_NKEYS)), axis=0, keepdims=True)
        return m, i, jnp.where(key_row == i, neg_inf, s)

    def head_body(hd, carry):
        q1 = _dot(hb, wq_ref[hd, 0]).astype(_bf16)
        q2 = _dot(hb, wq_ref[hd, 1]).astype(_bf16)
        s1 = _dot_nt(k1_ref[hd], q1)
        s2 = _dot_nt(k2_ref[hd], q2)

        for lt in range(tr // ROUTE_LANES):
            lanes = slice(lt * ROUTE_LANES, (lt + 1) * ROUTE_LANES)

            def half_body(k, c):
                m1, i1, r1 = extract_max(c[0])
                m2, i2, r2 = extract_max(c[1])
                for j, row in enumerate((m1, i1, m2, i2)):
                    top_ref[lt, j, pl.ds(k, 1), :] = row
                return r1, r2

            lax.fori_loop(0, PEER_TOPK, half_body, (s1[:, lanes], s2[:, lanes]))

        def top(j):
            return jnp.concatenate([top_ref[lt, j] for lt in range(tr // ROUTE_LANES)], axis=1)

        sa = sa_ref[...]
        sc = sc_ref[...]
        cand_s = _dot_exact_lhs01(sa, top(0)) + _dot_exact_lhs01(sc, top(2)) + pad_ref[...]
        cand_i = (_dot(sa, top(1).astype(_bf16)) * float(PEER_NKEYS)
                  + _dot(sc, top(3).astype(_bf16)))

        def pick_body(k, c):
            cand_s, denom, m_first = c
            m = jnp.max(cand_s, axis=0, keepdims=True)
            pos = jnp.min(jnp.where(cand_s == m, cand_row, float(n_cand)), axis=0, keepdims=True)
            hit = cand_row == pos
            eid = jnp.max(jnp.where(hit, cand_i, -1.0), axis=0, keepdims=True)
            m_first = jnp.where(k == 0, m, m_first)
            e = jnp.exp(m - m_first)
            slot = hd * PEER_TOPK + k
            idx_t[pl.ds(slot, 1), :] = eid
            e_t[pl.ds(slot, 1), :] = e
            return jnp.where(hit, neg_inf, cand_s), denom + e, m_first

        zero_row = jnp.zeros((1, tr), _f32)
        _, denom, _ = lax.fori_loop(0, PEER_TOPK, pick_body, (cand_s, zero_row, zero_row))
        rows = pl.ds(pl.multiple_of(hd * PEER_TOPK, PEER_TOPK), PEER_TOPK)
        e_t[rows, :] = e_t[rows, :] / denom
        return carry

    lax.fori_loop(0, PEER_HEADS, head_body, 0)
    idx_ref[...] = idx_t[...].T.astype(jnp.int32)
    gate_ref[...] = e_t[...].T


def _route(x, norm_g, w_query, keys1, keys2, *, tr):
    t, d = x.shape
    sel_a, sel_c, pad = _stair_pairs()
    pad = jnp.broadcast_to(pad, (pad.shape[0], tr))
    return pl.pallas_call(
        _route_kernel,
        grid=(t // tr,),
        in_specs=[
            pl.BlockSpec((tr, d), lambda i: (i, 0)),
            _const_spec((1, d)),
            _const_spec(w_query.shape),
            _const_spec(keys1.shape),
            _const_spec(keys2.shape),
            _const_spec(sel_a.shape),
            _const_spec(sel_c.shape),
            _const_spec(pad.shape),
        ],
        out_specs=[
            pl.BlockSpec((tr, d), lambda i: (i, 0)),
            pl.BlockSpec((tr, PEER_SLOTS), lambda i: (i, 0)),
            pl.BlockSpec((tr, PEER_SLOTS), lambda i: (i, 0)),
        ],
        out_shape=[
            jax.ShapeDtypeStruct((t, d), _f32),
            jax.ShapeDtypeStruct((t, PEER_SLOTS), jnp.int32),
            jax.ShapeDtypeStruct((t, PEER_SLOTS), _f32),
        ],
        scratch_shapes=[
            pltpu.VMEM((PEER_SLOTS, tr), _f32),
            pltpu.VMEM((PEER_SLOTS, tr), _f32),
            pltpu.VMEM((tr // ROUTE_LANES, 4, PEER_TOPK, ROUTE_LANES), _f32),
        ],
        compiler_params=pltpu.CompilerParams(
            dimension_semantics=("parallel",),
            vmem_limit_bytes=VMEM_LIMIT_BYTES),
        name="route",
    )(x, norm_g, w_query, keys1, keys2, sel_a, sel_c, pad)


def _scores_kernel(x_ref, g_ref, wq_ref, k1_ref, k2_ref, h_ref, s_ref):
    x = x_ref[...]
    h = x * lax.rsqrt(jnp.mean(x * x, axis=-1, keepdims=True) + EPS) * g_ref[...]
    h_ref[...] = h
    hb = h.astype(_bf16)
    for hd in range(PEER_HEADS):
        for half, keys_ref in enumerate((k1_ref, k2_ref)):
            q = _dot(hb, wq_ref[hd, half]).astype(_bf16)
            col = (2 * hd + half) * PEER_NKEYS
            s_ref[:, col:col + PEER_NKEYS] = _dot_nt(q, keys_ref[hd])


def _scores(x, norm_g, w_query, keys1, keys2, *, tr):
    t, d = x.shape
    width = 2 * PEER_HEADS * PEER_NKEYS
    return pl.pallas_call(
        _scores_kernel,
        grid=(t // tr,),
        in_specs=[
            pl.BlockSpec((tr, d), lambda i: (i, 0)),
            _const_spec((1, d)),
            _const_spec(w_query.shape),
            _const_spec(keys1.shape),
            _const_spec(keys2.shape),
        ],
        out_specs=[
            pl.BlockSpec((tr, d), lambda i: (i, 0)),
            pl.BlockSpec((tr, width), lambda i: (i, 0)),
        ],
        out_shape=[
            jax.ShapeDtypeStruct((t, d), _f32),
            jax.ShapeDtypeStruct((t, width), _f32),
        ],
        compiler_params=pltpu.CompilerParams(
            dimension_semantics=("parallel",),
            vmem_limit_bytes=VMEM_LIMIT_BYTES),
        name="scores",
    )(x, norm_g, w_query, keys1, keys2)


SC_CORES = 2
SC_SUBCORES = 16
SC_LANES = 16
SC_WORKERS = SC_CORES * SC_SUBCORES
SC_TOKENS = 8
SC_GATHER = 32
SC_BLOCK = 8
SC_UNROLL = 2
SC_COLS = 8
HI_MASK = -65536


def _sc_mesh():
    return plsc.VectorSubcoreMesh(core_axis_name="c", subcore_axis_name="s")


def _sc_worker_base(per_worker):
    return (lax.axis_index("s") * SC_CORES + lax.axis_index("c")) * per_worker


def _sc_gather_loop(tab_hbm, idx_v, bufs, consume):
    n_parts = PEER_SLOTS // SC_GATHER
    n_gathers = SC_TOKENS * n_parts

    def gather(g, parity):
        rows, sem = bufs[parity]
        i = g // n_parts
        col = pl.multiple_of((g % n_parts) * SC_GATHER, SC_GATHER)
        return pltpu.make_async_copy(tab_hbm.at[idx_v.at[i, pl.ds(col, SC_GATHER)]], rows, sem)

    gather(0, 0).start()

    @pl.loop(0, n_gathers // 2)
    def _(pair):
        g = 2 * pair
        gather(g + 1, 1).start()
        gather(g, 0).wait()
        consume(g // n_parts, g % n_parts, bufs[0][0])

        @pl.when(g + 2 < n_gathers)
        def _():
            gather(g + 2, 0).start()

        gather(g + 1, 1).wait()
        consume((g + 1) // n_parts, (g + 1) % n_parts, bufs[1][0])


def _pack_halves(tab):
    half = tab.shape[1] // 2
    bits = lax.bitcast_convert_type(tab.astype(_bf16), jnp.uint16).astype(jnp.uint32)
    return lax.bitcast_convert_type(bits[:, :half] | (bits[:, half:] << 16), jnp.int32)


def _sc_unpack(words):
    lo = lax.bitcast_convert_type(words << 16, _f32)
    hi = lax.bitcast_convert_type(words & HI_MASK, _f32)
    return lo, hi


def _stair_vectors():
    pairs = [(a, c) for a in range(PEER_TOPK) for c in range(PEER_TOPK) if (a + 1) * (c + 1) <= PEER_TOPK]
    n = -(-len(pairs) // SC_LANES)
    fill = n * SC_LANES - len(pairs)
    a = jnp.asarray([p[0] for p in pairs] + [0] * fill, jnp.int32).reshape(n, SC_LANES)
    c = jnp.asarray([p[1] for p in pairs] + [0] * fill, jnp.int32).reshape(n, SC_LANES)
    pad = jnp.asarray([0.0] * len(pairs) + [-jnp.inf] * fill, _f32).reshape(n, SC_LANES)
    return a, c, pad


def _sc_route_dot(scores, tab, h):
    t, width = scores.shape
    d = h.shape[1]
    half_d = d // 2
    per_worker = t // SC_WORKERS
    pair_a, pair_c, pair_pad = _stair_vectors()
    n_cand = pair_a.shape[0]
    n_vec = PEER_NKEYS // SC_LANES

    @functools.partial(
        pl.kernel, mesh=_sc_mesh(),
        out_type=(jax.ShapeDtypeStruct((t, PEER_SLOTS), jnp.int32),
                  jax.ShapeDtypeStruct((t, PEER_SLOTS), _f32),
                  jax.ShapeDtypeStruct((t, PEER_SLOTS), _f32)),
        scratch_types=[
            pltpu.VMEM((SC_TOKENS, width), _f32),
            pltpu.VMEM((SC_TOKENS, PEER_SLOTS), jnp.int32),
            pltpu.VMEM((SC_TOKENS, PEER_SLOTS), _f32),
            pltpu.VMEM((n_cand, SC_LANES), jnp.int32),
            pltpu.VMEM((n_cand, SC_LANES), jnp.int32),
            pltpu.VMEM((n_cand, SC_LANES), _f32),
            pltpu.VMEM((4, SC_LANES), _f32),
            pltpu.VMEM((SC_TOKENS, d), _f32),
            pltpu.VMEM((SC_TOKENS, PEER_SLOTS), _f32),
            pltpu.VMEM((SC_GATHER, half_d), jnp.int32),
            pltpu.VMEM((SC_GATHER, half_d), jnp.int32),
            pltpu.SemaphoreType.DMA,
            pltpu.SemaphoreType.DMA,
        ],
        compiler_params=pltpu.CompilerParams(needs_layout_passes=False),
        name="sc_route_dot",
    )
    def body(s_hbm, a_hbm, c_hbm, pad_hbm, tab_hbm, h_hbm, idx_hbm, gate_hbm, acts_hbm,
             s_v, idx_v, gate_v, a_v, c_v, pad_v, top_v, h_v, acts_v, rows0, rows1, sem0, sem1):
        base = _sc_worker_base(per_worker)
        lane = lax.iota(jnp.int32, SC_LANES)
        pltpu.sync_copy(a_hbm, a_v)
        pltpu.sync_copy(c_hbm, c_v)
        pltpu.sync_copy(pad_hbm, pad_v)

        def dots(i, part, rows):
            def block_body(blk, carry):
                row0 = blk * SC_LANES
                outv = jnp.zeros((SC_LANES,), _f32)
                for sub in range(SC_LANES // SC_BLOCK):
                    def chunk_body(jj, accs):
                        accs = list(accs)
                        for u in range(SC_UNROLL):
                            off = pl.multiple_of((jj * SC_UNROLL + u) * SC_LANES, SC_LANES)
                            h_lo = h_v[i, pl.ds(off, SC_LANES)]
                            h_hi = h_v[i, pl.ds(half_d + off, SC_LANES)]
                            for e in range(SC_BLOCK):
                                lo, hi = _sc_unpack(rows[row0 + sub * SC_BLOCK + e, pl.ds(off, SC_LANES)])
                                accs[e] = accs[e] + (lo * h_lo + hi * h_hi)
                        return tuple(accs)

                    accs = lax.fori_loop(0, half_d // SC_LANES // SC_UNROLL, chunk_body,
                                         tuple(jnp.zeros((SC_LANES,), _f32) for _ in range(SC_BLOCK)))
                    for e in range(SC_BLOCK):
                        outv = jnp.where(lane == sub * SC_BLOCK + e, jnp.sum(accs[e]), outv)
                col = pl.multiple_of(part * SC_GATHER + row0, SC_LANES)
                acts_v[i, pl.ds(col, SC_LANES)] = outv
                return carry

            lax.fori_loop(0, SC_GATHER // SC_LANES, block_body, 0)

        def sort_desc(k, v):
            return plsc.sort_key_val(k, v, descending=True)

        def merge(x, y):
            yk, yv = lax.rev(y[0], (0,)), lax.rev(y[1], (0,))
            take = x[0] >= yk
            return sort_desc(jnp.where(take, x[0], yk), jnp.where(take, x[1], yv))

        def top_of(vectors):
            while len(vectors) > 1:
                vectors = [merge(vectors[j], vectors[j + 1]) for j in range(0, len(vectors), 2)]
            return vectors[0]

        def head_body(i, hd):
            halves = []
            for half in range(2):
                col = (2 * hd + half) * PEER_NKEYS
                vecs = [sort_desc(s_v[i, pl.ds(pl.multiple_of(col + j * SC_LANES, SC_LANES), SC_LANES)],
                                  lane + j * SC_LANES) for j in range(n_vec)]
                halves.append(top_of(vecs))
            (v1, i1), (v2, i2) = halves
            top_v[0, :] = v1
            top_v[1, :] = i1.astype(_f32)
            top_v[2, :] = v2
            top_v[3, :] = i2.astype(_f32)

            def pick(row, pos):
                return plsc.load_gather(top_v, [jnp.full((SC_LANES,), row, jnp.int32), pos])

            cands = []
            for q in range(n_cand):
                a, c = a_v[q, :], c_v[q, :]
                cs = pick(0, a) + pick(2, c) + pad_v[q, :]
                ci = pick(1, a) * float(PEER_NKEYS) + pick(3, c)
                cands.append(sort_desc(cs, ci))
            top_s, top_i = top_of(cands)
            e = jnp.exp(top_s - jnp.max(top_s))
            slots = pl.ds(pl.multiple_of(hd * PEER_TOPK, PEER_TOPK), PEER_TOPK)
            idx_v[i, slots] = top_i.astype(jnp.int32)
            gate_v[i, slots] = e / jnp.sum(e)

        @pl.loop(0, per_worker // SC_TOKENS)
        def _(step):
            tok = pl.multiple_of(base + step * SC_TOKENS, SC_TOKENS)
            pltpu.sync_copy(s_hbm.at[pl.ds(tok, SC_TOKENS)], s_v)
            pltpu.sync_copy(h_hbm.at[pl.ds(tok, SC_TOKENS)], h_v)

            @pl.loop(0, SC_TOKENS * PEER_HEADS)
            def _(n):
                head_body(n // PEER_HEADS, n % PEER_HEADS)

            pltpu.sync_copy(idx_v, idx_hbm.at[pl.ds(tok, SC_TOKENS)])
            pltpu.sync_copy(gate_v, gate_hbm.at[pl.ds(tok, SC_TOKENS)])
            _sc_gather_loop(tab_hbm, idx_v, ((rows0, sem0), (rows1, sem1)), dots)
            pltpu.sync_copy(acts_v, acts_hbm.at[pl.ds(tok, SC_TOKENS)])

    return body(scores, pair_a, pair_c, pair_pad, tab, h)


def _sc_vaxpy(idx, w, tab, *, first, count):
    half = tab.shape[1]
    d = 2 * half
    per_worker = count // SC_WORKERS
    span = SC_COLS * SC_LANES

    @functools.partial(
        pl.kernel, mesh=_sc_mesh(),
        out_type=jax.ShapeDtypeStruct((count, d), _f32),
        scratch_types=[
            pltpu.VMEM((SC_TOKENS, PEER_SLOTS), jnp.int32),
            pltpu.VMEM((SC_TOKENS, PEER_SLOTS), _f32),
            pltpu.VMEM((SC_TOKENS, d), _f32),
            pltpu.VMEM((SC_GATHER, half), jnp.int32),
            pltpu.VMEM((SC_GATHER, half), jnp.int32),
            pltpu.SemaphoreType.DMA,
            pltpu.SemaphoreType.DMA,
        ],
        compiler_params=pltpu.CompilerParams(needs_layout_passes=False),
        name="sc_vaxpy",
    )
    def body(idx_hbm, w_hbm, tab_hbm, out_hbm, idx_v, w_v, out_v, rows0, rows1, sem0, sem1):
        base = _sc_worker_base(per_worker)

        def accumulate(i, part, rows):
            i_vec = jnp.full((SC_LANES,), i, jnp.int32)

            def span_body(cq, carry):
                def cols(c, offset=0):
                    return pl.ds(pl.multiple_of(offset + cq * span + c * SC_LANES, SC_LANES), SC_LANES)

                def expert_body(e, accs):
                    k_vec = jnp.full((SC_LANES,), part * SC_GATHER + e, jnp.int32)
                    wv = plsc.load_gather(w_v, [i_vec, k_vec])
                    new = []
                    for c in range(SC_COLS):
                        lo, hi = _sc_unpack(rows[e, cols(c)])
                        new += [accs[2 * c] + lo * wv, accs[2 * c + 1] + hi * wv]
                    return tuple(new)

                init = []
                for c in range(SC_COLS):
                    init += [out_v[i, cols(c)], out_v[i, cols(c, half)]]
                accs = lax.fori_loop(0, SC_GATHER, expert_body, tuple(init))
                for c in range(SC_COLS):
                    out_v[i, cols(c)] = accs[2 * c]
                    out_v[i, cols(c, half)] = accs[2 * c + 1]
                return carry

            lax.fori_loop(0, half // span, span_body, 0)

        @pl.loop(0, per_worker // SC_TOKENS)
        def _(step):
            off = pl.multiple_of(base + step * SC_TOKENS, SC_TOKENS)
            pltpu.sync_copy(idx_hbm.at[pl.ds(first + off, SC_TOKENS)], idx_v)
            pltpu.sync_copy(w_hbm.at[pl.ds(first + off, SC_TOKENS)], w_v)

            @pl.loop(0, SC_TOKENS)
            def _(i):
                @pl.loop(0, d // SC_LANES)
                def _(j):
                    out_v[i, pl.ds(pl.multiple_of(j * SC_LANES, SC_LANES), SC_LANES)] = (
                        jnp.zeros((SC_LANES,), _f32))

            _sc_gather_loop(tab_hbm, idx_v, ((rows0, sem0), (rows1, sem1)), accumulate)
            pltpu.sync_copy(out_v, out_hbm.at[pl.ds(off, SC_TOKENS)])

    return body(idx, w, tab)


ROW_CHUNKS = 8
PACK_ROWS = ROW_CHUNKS // 2
ROW_BUFFERS = 4


def _pack_table(tab):
    n, d = tab.shape
    bits = lax.bitcast_convert_type(tab.astype(_bf16), jnp.uint16).astype(jnp.uint32)
    bits = bits.reshape(n, PACK_ROWS, 2, d // ROW_CHUNKS)
    word = bits[:, :, 0, :] | (bits[:, :, 1, :] << 16)
    return lax.bitcast_convert_type(word, jnp.int32)


def _gate_weights_kernel(acts_ref, gate_ref, w_ref):
    a = acts_ref[...]
    gelu = 0.5 * a * (1.0 + lax.erf(a * (2.0 ** -0.5)))
    w_ref[...] = gate_ref[...] * gelu


def _gate_weights(acts, gate, *, tw):
    t, n = acts.shape
    spec = pl.BlockSpec((tw, n), lambda i: (i, 0))
    return pl.pallas_call(
        _gate_weights_kernel,
        grid=(t // tw,),
        in_specs=[spec, spec],
        out_specs=spec,
        out_shape=jax.ShapeDtypeStruct((t, n), _f32),
        compiler_params=pltpu.CompilerParams(dimension_semantics=("parallel",)),
        name="gate_weights",
    )(acts, gate)


def _gather_rows(idx_ref, t, tab_ref, rows_ref):
    for k in range(PEER_SLOTS):
        row = pl.multiple_of(idx_ref[t, k], PACK_ROWS)
        rows_ref[k * PACK_ROWS:(k + 1) * PACK_ROWS, :] = tab_ref[pl.ds(row, PACK_ROWS), :]


def _rows_matrix(rows_ref):
    return pltpu.bitcast(rows_ref[...], _bf16)


def _token_loop(tb, idx_ref, tab_ref, row_bufs, compute):
    n = len(row_bufs)
    ahead = 2
    for j in range(ahead):
        _gather_rows(idx_ref, j, tab_ref, row_bufs[j])

    def body(i, carry):
        t0 = n * i
        for j in range(n):
            nxt = jnp.minimum(t0 + j + ahead, tb - 1)
            _gather_rows(idx_ref, nxt, tab_ref, row_bufs[(j + ahead) % n])
            compute(t0 + j, _rows_matrix(row_bufs[j]))
        return carry

    lax.fori_loop(0, tb // n, body, 0)


def _chunk_diag_mask():
    shape = (ROW_CHUNKS, PEER_SLOTS * ROW_CHUNKS)
    return (_iota(shape, 1) % ROW_CHUNKS) == _iota(shape, 0)


def _vaxpy_kernel(idx_ref, w_ref, tab_ref, out_ref, wrep_ref, *row_bufs):
    tb = out_ref.shape[0]
    diag = _chunk_diag_mask()
    shape = (PEER_SLOTS, PEER_SLOTS * ROW_CHUNKS)
    spread = (_iota(shape, 0) == (_iota(shape, 1) // ROW_CHUNKS)).astype(_bf16)
    wrep_ref[...] = _dot_exact_rhs01(w_ref[...], spread)

    def compute(t, m):
        w_row = jnp.broadcast_to(wrep_ref[pl.ds(t, 1), :], diag.shape)
        w_hi, w_lo = _split2(jnp.where(diag, w_row, 0.0))
        out_ref[t] = _dot(w_hi, m) + _dot(w_lo, m)

    _token_loop(tb, idx_ref, tab_ref, row_bufs, compute)


def _vaxpy(idx, w, tab, *, tb, count):
    return pl.pallas_call(
        _vaxpy_kernel,
        grid=(count // tb,),
        in_specs=[
            pl.BlockSpec((tb, PEER_SLOTS), lambda i: (i, 0), memory_space=pltpu.SMEM),
            pl.BlockSpec((tb, PEER_SLOTS), lambda i: (i, 0)),
            _const_spec(tab.shape),
        ],
        out_specs=pl.BlockSpec((tb, ROW_CHUNKS, 128), lambda i: (i, 0, 0)),
        out_shape=jax.ShapeDtypeStruct((count, ROW_CHUNKS, 128), _f32),
        scratch_shapes=[
            pltpu.VMEM((tb, PEER_SLOTS * ROW_CHUNKS), _f32),
        ] + [pltpu.VMEM((PEER_SLOTS * PACK_ROWS, 128), jnp.int32)
             for _ in range(ROW_BUFFERS)],
        compiler_params=pltpu.CompilerParams(
            dimension_semantics=("parallel",),
            vmem_limit_bytes=VMEM_LIMIT_BYTES),
        name="vaxpy",
    )(idx, w, tab)


def _final_kernel(x_ref, p_ref, g_ref, out_ref, *, normalize):
    x = x_ref[...] + p_ref[...]
    if normalize:
        x = x * lax.rsqrt(jnp.mean(x * x, axis=-1, keepdims=True) + EPS) * g_ref[...]
    out_ref[...] = x


def _final(x, peer, g, *, normalize, tf, first):
    t, d = peer.shape
    off = first // tf
    return pl.pallas_call(
        functools.partial(_final_kernel, normalize=normalize),
        grid=(t // tf,),
        in_specs=[
            pl.BlockSpec((tf, d), lambda i: (i + off, 0)),
            pl.BlockSpec((tf, d), lambda i: (i, 0)),
            _const_spec((1, d)),
        ],
        out_specs=pl.BlockSpec((tf, d), lambda i: (i, 0)),
        out_shape=jax.ShapeDtypeStruct((t, d), _f32),
        compiler_params=pltpu.CompilerParams(dimension_semantics=("parallel",)),
        name="final_norm",
    )(x, peer, g)


def kernel(x, norm_mix_g, w_in, hg_lb_logits, hg_out_norm_g, conv_w, w_branch_hg, w_branch_conv, w_out, norm_ffn_g, peer_w_query, peer_keys1, peer_keys2, peer_u, peer_v, norm_final_g):
    b_, s_, d = x.shape
    depth = w_in.shape[0]
    lb_all = jnp.cumsum(jax.nn.softmax(hg_lb_logits.astype(_f32), axis=0), axis=0)
    n_groups = BATCH_GROUPS if b_ % BATCH_GROUPS == 0 else 1
    bg = b_ // n_groups
    tg = bg * s_
    n_sc = SC_SHARE if SC_SHARE < tg else 0
    n_tc = tg - n_sc
    for l in range(depth):
        wq = peer_w_query[l].astype(_bf16).reshape(d, PEER_HEADS, 2, PEER_HALF).transpose(1, 2, 0, 3)
        v_tab = _pack_table(peer_v[l])
        u_sc = _pack_halves(peer_u[l])
        v_sc = _pack_halves(peer_v[l])
        last = l == depth - 1
        g = norm_final_g[None] if last else jnp.ones((1, d), _f32)
        def front(c, x_in, sc_before=None):
            xc = _mix(x_in, norm_mix_g[l][None], w_in[l].astype(_bf16), lb_all[l][None],
                      hg_out_norm_g[l][None], conv_w[l], w_branch_hg[l].astype(_bf16),
                      w_branch_conv[l].astype(_bf16), w_out[l].astype(_bf16), ts=MIX_TILE)
            xf = xc.reshape(tg, d)
            h, scores = _scores(xf, norm_ffn_g[l][None], wq,
                                peer_keys1[l].astype(_bf16), peer_keys2[l].astype(_bf16), tr=ROUTE_TILE)
            if sc_before is not None:
                scores = lax.optimization_barrier((scores, sc_before))[0]
            idx, gate, acts = _sc_route_dot(scores, u_sc, h)
            return (xf, idx, gate, acts), scores

        def back(xf, idx, gate, acts):
            w = _gate_weights(acts, gate, tw=FINAL_TILE)
            peer_tc = _vaxpy(idx * PACK_ROWS, w, v_tab.reshape(-1, 128), tb=EXPERT_TILE, count=n_tc).reshape(n_tc, d)
            peer_sc = _sc_vaxpy(idx, w, v_sc, first=n_tc, count=n_sc) if n_sc else None
            return peer_tc, peer_sc

        groups = [x[c * bg:(c + 1) * bg] for c in range(n_groups)]
        fronts = {c: front(c, groups[c]) for c in range(min(GROUPS_AHEAD + 1, n_groups))}
        done, peer_tc = [], None
        for c in range(n_groups):
            (xf, idx, gate, acts), _ = fronts.pop(c)
            after = [fronts[c + GROUPS_AHEAD][1]] if c + GROUPS_AHEAD in fronts else []
            after += [peer_tc] if peer_tc is not None else []
            if after:
                acts = lax.optimization_barrier((acts, *after))[0]
            peer_tc, peer_sc = back(xf, idx, gate, acts)
            done.append((xf, peer_tc, peer_sc))
            nxt = c + GROUPS_AHEAD + 1
            if nxt < n_groups:
                fronts[nxt] = front(nxt, lax.optimization_barrier((groups[nxt], peer_tc))[0], peer_sc)
        outs = []
        for xf, p_tc, p_sc in done:
            outs.append(_final(xf, p_tc, g, normalize=last, tf=FINAL_TILE, first=0))
            if p_sc is not None:
                p_sc = lax.optimization_barrier((p_sc, peer_tc))[0]
                outs.append(_final(xf, p_sc, g, normalize=last, tf=FINAL_TILE, first=n_tc))
        x = jnp.concatenate(outs, axis=0).reshape(b_, s_, d)
    return x
```

```python
import functools

import jax
import jax.numpy as jnp
from jax import lax
from jax.experimental import pallas as pl
from jax.experimental.pallas import tpu as pltpu
from jax.experimental.pallas import tpu_sc as plsc

EPS = 1e-6
CHUNK = 64
SUB = 16
HEADS = 8
HEAD_DIM = 64
HG_WIDTH = HEADS * HEAD_DIM
GROUP = 256
N_GROUPS = HG_WIDTH // GROUP
CONV_K = 3
PEER_HEADS = 8
PEER_NKEYS = 128
PEER_HALF = 128
PEER_TOPK = 16
PEER_SLOTS = PEER_HEADS * PEER_TOPK

VMEM_LIMIT_BYTES = 56 * 1024 * 1024

MIX_TILE = 256
ROUTE_TILE = 256
EXPERT_TILE = 128
FINAL_TILE = 256
BATCH_GROUPS = 8
SC_SHARE = 512
GROUPS_AHEAD = 7

_f32 = jnp.float32
_bf16 = jnp.bfloat16


def _dot(a, b):
    return jnp.dot(a, b, preferred_element_type=_f32)


def _dot_nt(a, b):
    return lax.dot_general(a, b, (((1,), (1,)), ((), ())), preferred_element_type=_f32)


def _dot_tn(a, b):
    return lax.dot_general(a, b, (((0,), (0,)), ((), ())), preferred_element_type=_f32)


def _split3(x):
    hi = x.astype(_bf16)
    r1 = x - hi.astype(_f32)
    mid = r1.astype(_bf16)
    lo = (r1 - mid.astype(_f32)).astype(_bf16)
    return hi, mid, lo


def _split2(x):
    hi = x.astype(_bf16)
    lo = (x - hi.astype(_f32)).astype(_bf16)
    return hi, lo


def _dot_exact_rhs01(x, m01):
    hi, mid, lo = _split3(x)
    return _dot(hi, m01) + _dot(mid, m01) + _dot(lo, m01)


def _dot_exact_lhs01(m01, x):
    hi, mid, lo = _split3(x)
    return _dot(m01, hi) + _dot(m01, mid) + _dot(m01, lo)


def _iota(shape, dim):
    return lax.broadcasted_iota(jnp.int32, shape, dim)


def _hgrn2_chunk(q, k, lf, v, state_ref):
    n_sub = CHUNK // SUB
    row = _iota((CHUNK, CHUNK), 0)
    col = _iota((CHUNK, CHUNK), 1)
    tril = (col <= row).astype(_bf16)
    b = _dot_exact_lhs01(tril, lf)

    b_end = [b[(j + 1) * SUB - 1:(j + 1) * SUB, :] for j in range(n_sub)]
    b_end_rows = jnp.concatenate([jnp.broadcast_to(e, (SUB, HG_WIDTH)) for e in b_end], axis=0)
    b_last = b_end[-1]

    q_in = (q * jnp.exp(b)).astype(_bf16)
    k_sub = (k * jnp.exp(b_end_rows - b)).astype(_bf16)
    k_out = (k * jnp.exp(b_last - b)).astype(_bf16)
    q_from = [(q * jnp.exp(jnp.minimum(b - b_end[j], 0.0))).astype(_bf16) for j in range(n_sub - 1)]
    v_b = v.astype(_bf16)

    gr = _iota((GROUP, GROUP), 0) // HEAD_DIM
    gc = _iota((GROUP, GROUP), 1) // HEAD_DIM
    head_mask = gr == gc
    t_blk = _iota((CHUNK, GROUP), 0) // SUB
    s_blk = (_iota((CHUNK, GROUP), 1) % HEAD_DIM) // SUB

    outs = []
    for g in range(N_GROUPS):
        sl = slice(g * GROUP, (g + 1) * GROUP)
        st = state_ref[g]
        o_g = _dot_nt(q_in[:, sl], st.astype(_bf16))

        zero_b = jnp.zeros((), _bf16)
        k_bd = jnp.where(head_mask, jnp.concatenate([k_sub[:, sl]] * (GROUP // CHUNK), axis=0), zero_b)
        v_bd = jnp.where(head_mask, jnp.concatenate([v_b[:, sl]] * (GROUP // CHUNK), axis=0), zero_b)
        q_stack = jnp.concatenate([qf[:, sl] for qf in q_from], axis=0)
        r = _dot_nt(q_stack, k_bd)
        scores = jnp.zeros((CHUNK, GROUP), _f32)
        for j in range(n_sub - 1):
            sel = (s_blk == j) & (t_blk > j)
            scores = jnp.where(sel, r[j * CHUNK:(j + 1) * CHUNK, :], scores)
        o_g = o_g + _dot(scores.astype(_bf16), v_bd)
        outs.append(o_g)

        upd = _dot_tn(v_b[:, sl], k_out[:, sl])
        decay = jnp.exp(b_last[:, sl])
        state_ref[g] = st * decay + jnp.where(head_mask, upd, 0.0)
    o = jnp.concatenate(outs, axis=1)

    ones_bd = ((_iota((HG_WIDTH, HG_WIDTH), 0) // HEAD_DIM)
               == (_iota((HG_WIDTH, HG_WIDTH), 1) // HEAD_DIM)).astype(_bf16)
    t_in_sub = _iota((CHUNK, HG_WIDTH), 0) % SUB
    for lag in range(SUB):
        if lag == 0:
            p = q * k
            v_l = v
        else:
            valid = t_in_sub >= lag
            k_l = pltpu.roll(k, lag, 0)
            b_l = pltpu.roll(b, lag, 0)
            v_l = pltpu.roll(v, lag, 0)
            p = jnp.where(valid, q * k_l * jnp.exp(jnp.minimum(b - b_l, 0.0)), 0.0)
        s_l = _dot(p.astype(_bf16), ones_bd)
        o = o + s_l * v_l
    return o


def _mix_kernel(x_ref, g_ref, win_ref, lb_ref, hgn_ref, convw_ref, pa_ref, pb_ref, wo_ref,
                out_ref, state_ref, carry_ref, q_s, k_s, lf_s, v_s, o_s):
    ts = x_ref.shape[0]
    d_model = x_ref.shape[1]
    w = HG_WIDTH

    @pl.when(pl.program_id(1) == 0)
    def _():
        state_ref[...] = jnp.zeros_like(state_ref)
        carry_ref[...] = jnp.zeros_like(carry_ref)

    x = x_ref[...]
    h = x * lax.rsqrt(jnp.mean(x * x, axis=-1, keepdims=True) + EPS) * g_ref[...]
    hb = h.astype(_bf16)

    def proj(i, width=w):
        return _dot(hb, win_ref[:, i * w:i * w + width])

    lb = lb_ref[...]
    q_s[...] = jax.nn.silu(proj(0)) * (HEAD_DIM ** -0.5)
    forget = lb + (1.0 - lb) * jax.nn.sigmoid(proj(1))
    k_s[...] = 1.0 - forget
    lf_s[...] = jnp.log(forget)
    v_s[...] = proj(2)

    def chunk_body(c, carry):
        rows = pl.ds(pl.multiple_of(c * CHUNK, CHUNK), CHUNK)
        o_s[rows, :] = _hgrn2_chunk(q_s[rows, :], k_s[rows, :], lf_s[rows, :], v_s[rows, :], state_ref)
        return carry

    lax.fori_loop(0, ts // CHUNK, chunk_body, 0)

    o = o_s[...]
    ones_bd = ((_iota((w, w), 0) // HEAD_DIM) == (_iota((w, w), 1) // HEAD_DIM)).astype(_bf16)
    ms = _dot_exact_rhs01(o * o, ones_bd) * (1.0 / HEAD_DIM)
    o = o * lax.rsqrt(ms + EPS) * hgn_ref[...]
    y_a = (o * jax.nn.silu(proj(3))).astype(_bf16)

    u = proj(5) * proj(6)
    prev = carry_ref[...]
    rowi = _iota((ts, w), 0)
    u1 = jnp.where(rowi >= 1, pltpu.roll(u, 1, 0), jnp.broadcast_to(prev[7:8, :], (ts, w)))
    u2 = jnp.where(rowi >= 2, pltpu.roll(u, 2, 0),
                   jnp.where(rowi == 1, jnp.broadcast_to(prev[7:8, :], (ts, w)),
                             jnp.broadcast_to(prev[6:7, :], (ts, w))))
    carry_ref[...] = u[ts - 8:, :]
    cw = convw_ref[...]
    y_b = (proj(4) * (cw[0:1, :] * u2 + cw[1:2, :] * u1 + cw[2:3, :] * u)).astype(_bf16)

    g_a = jax.nn.sigmoid(proj(7, d_model))
    g_b = jax.nn.sigmoid(_dot(hb, win_ref[:, 7 * w + d_model:7 * w + 2 * d_model]))
    merged = g_a * _dot(y_a, pa_ref[...]) + g_b * _dot(y_b, pb_ref[...])
    out_ref[...] = x + _dot(merged.astype(_bf16), wo_ref[...])


def _const_spec(shape):
    nd = len(shape)
    return pl.BlockSpec(shape, lambda *_: (0,) * nd, pipeline_mode=pl.Buffered(1))


def _mix(x, norm_g, w_in, lb, hg_norm_g, conv_w, w_a, w_b, w_o, *, ts):
    b_, s_, d = x.shape
    in_cols = w_in.shape[1]
    w = HG_WIDTH
    grid = (b_, s_ // ts)
    return pl.pallas_call(
        _mix_kernel,
        grid=grid,
        in_specs=[
            pl.BlockSpec((None, ts, d), lambda b, s: (b, s, 0)),
            _const_spec((1, d)),
            _const_spec((d, in_cols)),
            _const_spec((1, w)),
            _const_spec((1, w)),
            _const_spec((CONV_K, w)),
            _const_spec((w, d)),
            _const_spec((w, d)),
            _const_spec((d, d)),
        ],
        out_specs=pl.BlockSpec((None, ts, d), lambda b, s: (b, s, 0)),
        out_shape=jax.ShapeDtypeStruct((b_, s_, d), _f32),
        scratch_shapes=[
            pltpu.VMEM((N_GROUPS, GROUP, GROUP), _f32),
            pltpu.VMEM((8, w), _f32),
            pltpu.VMEM((ts, w), _f32),
            pltpu.VMEM((ts, w), _f32),
            pltpu.VMEM((ts, w), _f32),
            pltpu.VMEM((ts, w), _f32),
            pltpu.VMEM((ts, w), _f32),
        ],
        compiler_params=pltpu.CompilerParams(
            dimension_semantics=("parallel", "arbitrary"),
            vmem_limit_bytes=VMEM_LIMIT_BYTES),
        name="mix",
    )(x, norm_g, w_in, lb, hg_norm_g, conv_w, w_a, w_b, w_o)


def _stair_pairs():
    pairs = [(a, c) for a in range(PEER_TOPK) for c in range(PEER_TOPK) if (a + 1) * (c + 1) <= PEER_TOPK]
    rows = -(-len(pairs) // 8) * 8
    ranks = jnp.arange(PEER_TOPK)[None, :]
    a_col = jnp.asarray([a for a, _ in pairs] + [-1] * (rows - len(pairs)))[:, None]
    c_col = jnp.asarray([c for _, c in pairs] + [-1] * (rows - len(pairs)))[:, None]
    pad = jnp.where(a_col < 0, -jnp.inf, 0.0).astype(_f32)
    return (a_col == ranks).astype(_bf16), (c_col == ranks).astype(_bf16), pad


ROUTE_LANES = 128


def _route_kernel(x_ref, g_ref, wq_ref, k1_ref, k2_ref, sa_ref, sc_ref, pad_ref, h_ref, idx_ref, gate_ref,
                  idx_t, e_t, top_ref):
    tr = x_ref.shape[0]
    n_cand = sa_ref.shape[0]
    x = x_ref[...]
    h = x * lax.rsqrt(jnp.mean(x * x, axis=-1, keepdims=True) + EPS) * g_ref[...]
    h_ref[...] = h
    hb = h.astype(_bf16)

    key_row = _iota((PEER_NKEYS, ROUTE_LANES), 0).astype(_f32)
    cand_row = _iota((n_cand, tr), 0).astype(_f32)
    neg_inf = jnp.float32(-jnp.inf)

    def extract_max(s):
        m = jnp.max(s, axis=0, keepdims=True)
        i = jnp.min(jnp.where(s == m, key_row, float(PEER_NKEYS)), axis=0, keepdims=True)
        return m, i, jnp.where(key_row == i, neg_inf, s)

    def head_body(hd, carry):
        q1 = _dot(hb, wq_ref[hd, 0]).astype(_bf16)
        q2 = _dot(hb, wq_ref[hd, 1]).astype(_bf16)
        s1 = _dot_nt(k1_ref[hd], q1)
        s2 = _dot_nt(k2_ref[hd], q2)

        for lt in range(tr // ROUTE_LANES):
            lanes = slice(lt * ROUTE_LANES, (lt + 1) * ROUTE_LANES)

            def half_body(k, c):
                m1, i1, r1 = extract_max(c[0])
                m2, i2, r2 = extract_max(c[1])
                for j, row in enumerate((m1, i1, m2, i2)):
                    top_ref[lt, j, pl.ds(k, 1), :] = row
                return r1, r2

            lax.fori_loop(0, PEER_TOPK, half_body, (s1[:, lanes], s2[:, lanes]))

        def top(j):
            return jnp.concatenate([top_ref[lt, j] for lt in range(tr // ROUTE_LANES)], axis=1)

        sa = sa_ref[...]
        sc = sc_ref[...]
        cand_s = _dot_exact_lhs01(sa, top(0)) + _dot_exact_lhs01(sc, top(2)) + pad_ref[...]
        cand_i = (_dot(sa, top(1).astype(_bf16)) * float(PEER_NKEYS)
                  + _dot(sc, top(3).astype(_bf16)))

        def pick_body(k, c):
            cand_s, denom, m_first = c
            m = jnp.max(cand_s, axis=0, keepdims=True)
            pos = jnp.min(jnp.where(cand_s == m, cand_row, float(n_cand)), axis=0, keepdims=True)
            hit = cand_row == pos
            eid = jnp.max(jnp.where(hit, cand_i, -1.0), axis=0, keepdims=True)
            m_first = jnp.where(k == 0, m, m_first)
            e = jnp.exp(m - m_first)
            slot = hd * PEER_TOPK + k
            idx_t[pl.ds(slot, 1), :] = eid
            e_t[pl.ds(slot, 1), :] = e
            return jnp.where(hit, neg_inf, cand_s), denom + e, m_first

        zero_row = jnp.zeros((1, tr), _f32)
        _, denom, _ = lax.fori_loop(0, PEER_TOPK, pick_body, (cand_s, zero_row, zero_row))
        rows = pl.ds(pl.multiple_of(hd * PEER_TOPK, PEER_TOPK), PEER_TOPK)
        e_t[rows, :] = e_t[rows, :] / denom
        return carry

    lax.fori_loop(0, PEER_HEADS, head_body, 0)
    idx_ref[...] = idx_t[...].T.astype(jnp.int32)
    gate_ref[...] = e_t[...].T


def _route(x, norm_g, w_query, keys1, keys2, *, tr):
    t, d = x.shape
    sel_a, sel_c, pad = _stair_pairs()
    pad = jnp.broadcast_to(pad, (pad.shape[0], tr))
    return pl.pallas_call(
        _route_kernel,
        grid=(t // tr,),
        in_specs=[
            pl.BlockSpec((tr, d), lambda i: (i, 0)),
            _const_spec((1, d)),
            _const_spec(w_query.shape),
            _const_spec(keys1.shape),
            _const_spec(keys2.shape),
            _const_spec(sel_a.shape),
            _const_spec(sel_c.shape),
            _const_spec(pad.shape),
        ],
        out_specs=[
            pl.BlockSpec((tr, d), lambda i: (i, 0)),
            pl.BlockSpec((tr, PEER_SLOTS), lambda i: (i, 0)),
            pl.BlockSpec((tr, PEER_SLOTS), lambda i: (i, 0)),
        ],
        out_shape=[
            jax.ShapeDtypeStruct((t, d), _f32),
            jax.ShapeDtypeStruct((t, PEER_SLOTS), jnp.int32),
            jax.ShapeDtypeStruct((t, PEER_SLOTS), _f32),
        ],
        scratch_shapes=[
            pltpu.VMEM((PEER_SLOTS, tr), _f32),
            pltpu.VMEM((PEER_SLOTS, tr), _f32),
            pltpu.VMEM((tr // ROUTE_LANES, 4, PEER_TOPK, ROUTE_LANES), _f32),
        ],
        compiler_params=pltpu.CompilerParams(
            dimension_semantics=("parallel",),
            vmem_limit_bytes=VMEM_LIMIT_BYTES),
        name="route",
    )(x, norm_g, w_query, keys1, keys2, sel_a, sel_c, pad)


def _scores_kernel(x_ref, g_ref, wq_ref, k1_ref, k2_ref, h_ref, s_ref):
    x = x_ref[...]
    h = x * lax.rsqrt(jnp.mean(x * x, axis=-1, keepdims=True) + EPS) * g_ref[...]
    h_ref[...] = h
    hb = h.astype(_bf16)
    for hd in range(PEER_HEADS):
        for half, keys_ref in enumerate((k1_ref, k2_ref)):
            q = _dot(hb, wq_ref[hd, half]).astype(_bf16)
            col = (2 * hd + half) * PEER_NKEYS
            s_ref[:, col:col + PEER_NKEYS] = _dot_nt(q, keys_ref[hd])


def _scores(x, norm_g, w_query, keys1, keys2, *, tr):
    t, d = x.shape
    width = 2 * PEER_HEADS * PEER_NKEYS
    return pl.pallas_call(
        _scores_kernel,
        grid=(t // tr,),
        in_specs=[
            pl.BlockSpec((tr, d), lambda i: (i, 0)),
            _const_spec((1, d)),
            _const_spec(w_query.shape),
            _const_spec(keys1.shape),
            _const_spec(keys2.shape),
        ],
        out_specs=[
            pl.BlockSpec((tr, d), lambda i: (i, 0)),
            pl.BlockSpec((tr, width), lambda i: (i, 0)),
        ],
        out_shape=[
            jax.ShapeDtypeStruct((t, d), _f32),
            jax.ShapeDtypeStruct((t, width), _f32),
        ],
        compiler_params=pltpu.CompilerParams(
            dimension_semantics=("parallel",),
            vmem_limit_bytes=VMEM_LIMIT_BYTES),
        name="scores",
    )(x, norm_g, w_query, keys1, keys2)


SC_CORES = 2
SC_SUBCORES = 16
SC_LANES = 16
SC_WORKERS = SC_CORES * SC_SUBCORES
SC_TOKENS = 8
SC_GATHER = 32
SC_BLOCK = 8
SC_UNROLL = 2
SC_COLS = 8
HI_MASK = -65536


def _sc_mesh():
    return plsc.VectorSubcoreMesh(core_axis_name="c", subcore_axis_name="s")


def _sc_worker_base(per_worker):
    return (lax.axis_index("s") * SC_CORES + lax.axis_index("c")) * per_worker


def _sc_gather_loop(tab_hbm, idx_v, bufs, consume):
    n_parts = PEER_SLOTS // SC_GATHER
    n_gathers = SC_TOKENS * n_parts

    def gather(g, parity):
        rows, sem = bufs[parity]
        i = g // n_parts
        col = pl.multiple_of((g % n_parts) * SC_GATHER, SC_GATHER)
        return pltpu.make_async_copy(tab_hbm.at[idx_v.at[i, pl.ds(col, SC_GATHER)]], rows, sem)

    gather(0, 0).start()

    @pl.loop(0, n_gathers // 2)
    def _(pair):
        g = 2 * pair
        gather(g + 1, 1).start()
        gather(g, 0).wait()
        consume(g // n_parts, g % n_parts, bufs[0][0])

        @pl.when(g + 2 < n_gathers)
        def _():
            gather(g + 2, 0).start()

        gather(g + 1, 1).wait()
        consume((g + 1) // n_parts, (g + 1) % n_parts, bufs[1][0])


def _pack_halves(tab):
    half = tab.shape[1] // 2
    bits = lax.bitcast_convert_type(tab.astype(_bf16), jnp.uint16).astype(jnp.uint32)
    return lax.bitcast_convert_type(bits[:, :half] | (bits[:, half:] << 16), jnp.int32)


def _sc_unpack(words):
    lo = lax.bitcast_convert_type(words << 16, _f32)
    hi = lax.bitcast_convert_type(words & HI_MASK, _f32)
    return lo, hi


def _stair_vectors():
    pairs = [(a, c) for a in range(PEER_TOPK) for c in range(PEER_TOPK) if (a + 1) * (c + 1) <= PEER_TOPK]
    n = -(-len(pairs) // SC_LANES)
    fill = n * SC_LANES - len(pairs)
    a = jnp.asarray([p[0] for p in pairs] + [0] * fill, jnp.int32).reshape(n, SC_LANES)
    c = jnp.asarray([p[1] for p in pairs] + [0] * fill, jnp.int32).reshape(n, SC_LANES)
    pad = jnp.asarray([0.0] * len(pairs) + [-jnp.inf] * fill, _f32).reshape(n, SC_LANES)
    return a, c, pad


def _sc_route_dot(scores, tab, h):
    t, width = scores.shape
    d = h.shape[1]
    half_d = d // 2
    per_worker = t // SC_WORKERS
    pair_a, pair_c, pair_pad = _stair_vectors()
    n_cand = pair_a.shape[0]
    n_vec = PEER_NKEYS // SC_LANES

    @functools.partial(
        pl.kernel, mesh=_sc_mesh(),
        out_type=(jax.ShapeDtypeStruct((t, PEER_SLOTS), jnp.int32),
                  jax.ShapeDtypeStruct((t, PEER_SLOTS), _f32),
                  jax.ShapeDtypeStruct((t, PEER_SLOTS), _f32)),
        scratch_types=[
            pltpu.VMEM((SC_TOKENS, width), _f32),
            pltpu.VMEM((SC_TOKENS, PEER_SLOTS), jnp.int32),
            pltpu.VMEM((SC_TOKENS, PEER_SLOTS), _f32),
            pltpu.VMEM((n_cand, SC_LANES), jnp.int32),
            pltpu.VMEM((n_cand, SC_LANES), jnp.int32),
            pltpu.VMEM((n_cand, SC_LANES), _f32),
            pltpu.VMEM((4, SC_LANES), _f32),
            pltpu.VMEM((SC_TOKENS, d), _f32),
            pltpu.VMEM((SC_TOKENS, PEER_SLOTS), _f32),
            pltpu.VMEM((SC_GATHER, half_d), jnp.int32),
            pltpu.VMEM((SC_GATHER, half_d), jnp.int32),
            pltpu.SemaphoreType.DMA,
            pltpu.SemaphoreType.DMA,
        ],
        compiler_params=pltpu.CompilerParams(needs_layout_passes=False),
        name="sc_route_dot",
    )
    def body(s_hbm, a_hbm, c_hbm, pad_hbm, tab_hbm, h_hbm, idx_hbm, gate_hbm, acts_hbm,
             s_v, idx_v, gate_v, a_v, c_v, pad_v, top_v, h_v, acts_v, rows0, rows1, sem0, sem1):
        base = _sc_worker_base(per_worker)
        lane = lax.iota(jnp.int32, SC_LANES)
        pltpu.sync_copy(a_hbm, a_v)
        pltpu.sync_copy(c_hbm, c_v)
        pltpu.sync_copy(pad_hbm, pad_v)

        def dots(i, part, rows):
            def block_body(blk, carry):
                row0 = blk * SC_LANES
                outv = jnp.zeros((SC_LANES,), _f32)
                for sub in range(SC_LANES // SC_BLOCK):
                    def chunk_body(jj, accs):
                        accs = list(accs)
                        for u in range(SC_UNROLL):
                            off = pl.multiple_of((jj * SC_UNROLL + u) * SC_LANES, SC_LANES)
                            h_lo = h_v[i, pl.ds(off, SC_LANES)]
                            h_hi = h_v[i, pl.ds(half_d + off, SC_LANES)]
                            for e in range(SC_BLOCK):
                                lo, hi = _sc_unpack(rows[row0 + sub * SC_BLOCK + e, pl.ds(off, SC_LANES)])
                                accs[e] = accs[e] + (lo * h_lo + hi * h_hi)
                        return tuple(accs)

                    accs = lax.fori_loop(0, half_d // SC_LANES // SC_UNROLL, chunk_body,
                                         tuple(jnp.zeros((SC_LANES,), _f32) for _ in range(SC_BLOCK)))
                    for e in range(SC_BLOCK):
                        outv = jnp.where(lane == sub * SC_BLOCK + e, jnp.sum(accs[e]), outv)
                col = pl.multiple_of(part * SC_GATHER + row0, SC_LANES)
                acts_v[i, pl.ds(col, SC_LANES)] = outv
                return carry

            lax.fori_loop(0, SC_GATHER // SC_LANES, block_body, 0)

        def sort_desc(k, v):
            return plsc.sort_key_val(k, v, descending=True)

        def merge(x, y):
            yk, yv = lax.rev(y[0], (0,)), lax.rev(y[1], (0,))
            take = x[0] >= yk
            return sort_desc(jnp.where(take, x[0], yk), jnp.where(take, x[1], yv))

        def top_of(vectors):
            while len(vectors) > 1:
                vectors = [merge(vectors[j], vectors[j + 1]) for j in range(0, len(vectors), 2)]
            return vectors[0]

        def head_body(i, hd):
            halves = []
            for half in range(2):
                col = (2 * hd + half) * PEER_NKEYS
                vecs = [sort_desc(s_v[i, pl.ds(pl.multiple_of(col + j * SC_LANES, SC_LANES), SC_LANES)],
                                  lane + j * SC_LANES) for j in range(n_vec)]
                halves.append(top_of(vecs))
            (v1, i1), (v2, i2) = halves
            top_v[0, :] = v1
            top_v[1, :] = i1.astype(_f32)
            top_v[2, :] = v2
            top_v[3, :] = i2.astype(_f32)

            def pick(row, pos):
                return plsc.load_gather(top_v, [jnp.full((SC_LANES,), row, jnp.int32), pos])

            cands = []
            for q in range(n_cand):
                a, c = a_v[q, :], c_v[q, :]
                cs = pick(0, a) + pick(2, c) + pad_v[q, :]
                ci = pick(1, a) * float(PEER_NKEYS) + pick(3, c)
                cands.append(sort_desc(cs, ci))
            top_s, top_i = top_of(cands)
            e = jnp.exp(top_s - jnp.max(top_s))
            slots = pl.ds(pl.multiple_of(hd * PEER_TOPK, PEER_TOPK), PEER_TOPK)
            idx_v[i, slots] = top_i.astype(jnp.int32)
            gate_v[i, slots] = e / jnp.sum(e)

        @pl.loop(0, per_worker // SC_TOKENS)
        def _(step):
            tok = pl.multiple_of(base + step * SC_TOKENS, SC_TOKENS)
            pltpu.sync_copy(s_hbm.at[pl.ds(tok, SC_TOKENS)], s_v)
            pltpu.sync_copy(h_hbm.at[pl.ds(tok, SC_TOKENS)], h_v)

            @pl.loop(0, SC_TOKENS * PEER_HEADS)
            def _(n):
                head_body(n // PEER_HEADS, n % PEER_HEADS)

            pltpu.sync_copy(idx_v, idx_hbm.at[pl.ds(tok, SC_TOKENS)])
            pltpu.sync_copy(gate_v, gate_hbm.at[pl.ds(tok, SC_TOKENS)])
            _sc_gather_loop(tab_hbm, idx_v, ((rows0, sem0), (rows1, sem1)), dots)
            pltpu.sync_copy(acts_v, acts_hbm.at[pl.ds(tok, SC_TOKENS)])

    return body(scores, pair_a, pair_c, pair_pad, tab, h)


def _sc_vaxpy(idx, w, tab, *, first, count):
    half = tab.shape[1]
    d = 2 * half
    per_worker = count // SC_WORKERS
    span = SC_COLS * SC_LANES

    @functools.partial(
        pl.kernel, mesh=_sc_mesh(),
        out_type=jax.ShapeDtypeStruct((count, d), _f32),
        scratch_types=[
            pltpu.VMEM((SC_TOKENS, PEER_SLOTS), jnp.int32),
            pltpu.VMEM((SC_TOKENS, PEER_SLOTS), _f32),
            pltpu.VMEM((SC_TOKENS, d), _f32),
            pltpu.VMEM((SC_GATHER, half), jnp.int32),
            pltpu.VMEM((SC_GATHER, half), jnp.int32),
            pltpu.SemaphoreType.DMA,
            pltpu.SemaphoreType.DMA,
        ],
        compiler_params=pltpu.CompilerParams(needs_layout_passes=False),
        name="sc_vaxpy",
    )
    def body(idx_hbm, w_hbm, tab_hbm, out_hbm, idx_v, w_v, out_v, rows0, rows1, sem0, sem1):
        base = _sc_worker_base(per_worker)

        def accumulate(i, part, rows):
            i_vec = jnp.full((SC_LANES,), i, jnp.int32)

            def span_body(cq, carry):
                def cols(c, offset=0):
                    return pl.ds(pl.multiple_of(offset + cq * span + c * SC_LANES, SC_LANES), SC_LANES)

                def expert_body(e, accs):
                    k_vec = jnp.full((SC_LANES,), part * SC_GATHER + e, jnp.int32)
                    wv = plsc.load_gather(w_v, [i_vec, k_vec])
                    new = []
                    for c in range(SC_COLS):
                        lo, hi = _sc_unpack(rows[e, cols(c)])
                        new += [accs[2 * c] + lo * wv, accs[2 * c + 1] + hi * wv]
                    return tuple(new)

                init = []
                for c in range(SC_COLS):
                    init += [out_v[i, cols(c)], out_v[i, cols(c, half)]]
                accs = lax.fori_loop(0, SC_GATHER, expert_body, tuple(init))
                for c in range(SC_COLS):
                    out_v[i, cols(c)] = accs[2 * c]
                    out_v[i, cols(c, half)] = accs[2 * c + 1]
                return carry

            lax.fori_loop(0, half // span, span_body, 0)

        @pl.loop(0, per_worker // SC_TOKENS)
        def _(step):
            off = pl.multiple_of(base + step * SC_TOKENS, SC_TOKENS)
            pltpu.sync_copy(idx_hbm.at[pl.ds(first + off, SC_TOKENS)], idx_v)
            pltpu.sync_copy(w_hbm.at[pl.ds(first + off, SC_TOKENS)], w_v)

            @pl.loop(0, SC_TOKENS)
            def _(i):
                @pl.loop(0, d // SC_LANES)
                def _(j):
                    out_v[i, pl.ds(pl.multiple_of(j * SC_LANES, SC_LANES), SC_LANES)] = (
                        jnp.zeros((SC_LANES,), _f32))

            _sc_gather_loop(tab_hbm, idx_v, ((rows0, sem0), (rows1, sem1)), accumulate)
            pltpu.sync_copy(out_v, out_hbm.at[pl.ds(off, SC_TOKENS)])

    return body(idx, w, tab)


ROW_CHUNKS = 8
PACK_ROWS = ROW_CHUNKS // 2
ROW_BUFFERS = 4


def _pack_table(tab):
    n, d = tab.shape
    bits = lax.bitcast_convert_type(tab.astype(_bf16), jnp.uint16).astype(jnp.uint32)
    bits = bits.reshape(n, PACK_ROWS, 2, d // ROW_CHUNKS)
    word = bits[:, :, 0, :] | (bits[:, :, 1, :] << 16)
    return lax.bitcast_convert_type(word, jnp.int32)


def _gate_weights_kernel(acts_ref, gate_ref, w_ref):
    a = acts_ref[...]
    gelu = 0.5 * a * (1.0 + lax.erf(a * (2.0 ** -0.5)))
    w_ref[...] = gate_ref[...] * gelu


def _gate_weights(acts, gate, *, tw):
    t, n = acts.shape
    spec = pl.BlockSpec((tw, n), lambda i: (i, 0))
    return pl.pallas_call(
        _gate_weights_kernel,
        grid=(t // tw,),
        in_specs=[spec, spec],
        out_specs=spec,
        out_shape=jax.ShapeDtypeStruct((t, n), _f32),
        compiler_params=pltpu.CompilerParams(dimension_semantics=("parallel",)),
        name="gate_weights",
    )(acts, gate)


def _gather_rows(idx_ref, t, tab_ref, rows_ref):
    for k in range(PEER_SLOTS):
        row = pl.multiple_of(idx_ref[t, k], PACK_ROWS)
        rows_ref[k * PACK_ROWS:(k + 1) * PACK_ROWS, :] = tab_ref[pl.ds(row, PACK_ROWS), :]


def _rows_matrix(rows_ref):
    return pltpu.bitcast(rows_ref[...], _bf16)


def _token_loop(tb, idx_ref, tab_ref, row_bufs, compute):
    n = len(row_bufs)
    ahead = 2
    for j in range(ahead):
        _gather_rows(idx_ref, j, tab_ref, row_bufs[j])

    def body(i, carry):
        t0 = n * i
        for j in range(n):
            nxt = jnp.minimum(t0 + j + ahead, tb - 1)
            _gather_rows(idx_ref, nxt, tab_ref, row_bufs[(j + ahead) % n])
            compute(t0 + j, _rows_matrix(row_bufs[j]))
        return carry

    lax.fori_loop(0, tb // n, body, 0)


def _chunk_diag_mask():
    shape = (ROW_CHUNKS, PEER_SLOTS * ROW_CHUNKS)
    return (_iota(shape, 1) % ROW_CHUNKS) == _iota(shape, 0)


def _vaxpy_kernel(idx_ref, w_ref, tab_ref, out_ref, wrep_ref, *row_bufs):
    tb = out_ref.shape[0]
    diag = _chunk_diag_mask()
    shape = (PEER_SLOTS, PEER_SLOTS * ROW_CHUNKS)
    spread = (_iota(shape, 0) == (_iota(shape, 1) // ROW_CHUNKS)).astype(_bf16)
    wrep_ref[...] = _dot_exact_rhs01(w_ref[...], spread)

    def compute(t, m):
        w_row = jnp.broadcast_to(wrep_ref[pl.ds(t, 1), :], diag.shape)
        w_hi, w_lo = _split2(jnp.where(diag, w_row, 0.0))
        out_ref[t] = _dot(w_hi, m) + _dot(w_lo, m)

    _token_loop(tb, idx_ref, tab_ref, row_bufs, compute)


def _vaxpy(idx, w, tab, *, tb, count):
    return pl.pallas_call(
        _vaxpy_kernel,
        grid=(count // tb,),
        in_specs=[
            pl.BlockSpec((tb, PEER_SLOTS), lambda i: (i, 0), memory_space=pltpu.SMEM),
            pl.BlockSpec((tb, PEER_SLOTS), lambda i: (i, 0)),
            _const_spec(tab.shape),
        ],
        out_specs=pl.BlockSpec((tb, ROW_CHUNKS, 128), lambda i: (i, 0, 0)),
        out_shape=jax.ShapeDtypeStruct((count, ROW_CHUNKS, 128), _f32),
        scratch_shapes=[
            pltpu.VMEM((tb, PEER_SLOTS * ROW_CHUNKS), _f32),
        ] + [pltpu.VMEM((PEER_SLOTS * PACK_ROWS, 128), jnp.int32)
             for _ in range(ROW_BUFFERS)],
        compiler_params=pltpu.CompilerParams(
            dimension_semantics=("parallel",),
            vmem_limit_bytes=VMEM_LIMIT_BYTES),
        name="vaxpy",
    )(idx, w, tab)


def _final_kernel(x_ref, p_ref, g_ref, acc_ref, out_ref, *, normalize):
    del acc_ref
    x = x_ref[...] + p_ref[...]
    if normalize:
        x = x * lax.rsqrt(jnp.mean(x * x, axis=-1, keepdims=True) + EPS) * g_ref[...]
    out_ref[...] = x


def _final(acc, x, peer, g, *, normalize, tf, first, row):
    t, d = peer.shape
    x_off = first // tf
    out_off = row // tf
    return pl.pallas_call(
        functools.partial(_final_kernel, normalize=normalize),
        grid=(t // tf,),
        in_specs=[
            pl.BlockSpec((tf, d), lambda i: (i + x_off, 0)),
            pl.BlockSpec((tf, d), lambda i: (i, 0)),
            _const_spec((1, d)),
            pl.BlockSpec(memory_space=pl.ANY),
        ],
        out_specs=pl.BlockSpec((tf, d), lambda i: (i + out_off, 0)),
        out_shape=jax.ShapeDtypeStruct(acc.shape, _f32),
        input_output_aliases={3: 0},
        compiler_params=pltpu.CompilerParams(dimension_semantics=("parallel",)),
        name="final_norm",
    )(x, peer, g, acc)


def kernel(x, norm_mix_g, w_in, hg_lb_logits, hg_out_norm_g, conv_w, w_branch_hg, w_branch_conv, w_out, norm_ffn_g, peer_w_query, peer_keys1, peer_keys2, peer_u, peer_v, norm_final_g):
    b_, s_, d = x.shape
    depth = w_in.shape[0]
    lb_all = jnp.cumsum(jax.nn.softmax(hg_lb_logits.astype(_f32), axis=0), axis=0)
    n_groups = BATCH_GROUPS if b_ % BATCH_GROUPS == 0 else 1
    bg = b_ // n_groups
    tg = bg * s_
    n_sc = SC_SHARE if SC_SHARE < tg else 0
    n_tc = tg - n_sc
    for l in range(depth):
        wq = peer_w_query[l].astype(_bf16).reshape(d, PEER_HEADS, 2, PEER_HALF).transpose(1, 2, 0, 3)
        u_sc = _pack_halves(peer_u[l])
        last = l == depth - 1
        g = norm_final_g[None] if last else jnp.ones((1, d), _f32)
        def front(c, x_in, sc_before=None):
            xc = _mix(x_in, norm_mix_g[l][None], w_in[l].astype(_bf16), lb_all[l][None],
                      hg_out_norm_g[l][None], conv_w[l], w_branch_hg[l].astype(_bf16),
                      w_branch_conv[l].astype(_bf16), w_out[l].astype(_bf16), ts=MIX_TILE)
            xf = xc.reshape(tg, d)
            h, scores = _scores(xf, norm_ffn_g[l][None], wq,
                                peer_keys1[l].astype(_bf16), peer_keys2[l].astype(_bf16), tr=ROUTE_TILE)
            if sc_before is not None:
                scores = lax.optimization_barrier((scores, sc_before))[0]
            idx, gate, acts = _sc_route_dot(scores, u_sc, h)
            return (xf, idx, gate, acts), scores

        def back(xf, idx, gate, acts):
            w = _gate_weights(acts, gate, tw=FINAL_TILE)
            peer_tc = _vaxpy(idx * PACK_ROWS, w, v_tab.reshape(-1, 128), tb=EXPERT_TILE, count=n_tc).reshape(n_tc, d)
            peer_sc = _sc_vaxpy(idx, w, v_sc, first=n_tc, count=n_sc) if n_sc else None
            return peer_tc, peer_sc

        groups = [x[c * bg:(c + 1) * bg] for c in range(n_groups)]
        fronts = {c: front(c, groups[c]) for c in range(min(GROUPS_AHEAD + 1, n_groups))}
        v_rows = lax.optimization_barrier((peer_v[l], fronts[0][1]))[0]
        v_tab = _pack_table(v_rows)
        v_sc = _pack_halves(v_rows)
        done, peer_tc = [], None
        for c in range(n_groups):
            (xf, idx, gate, acts), _ = fronts.pop(c)
            after = [fronts[c + GROUPS_AHEAD][1]] if c + GROUPS_AHEAD in fronts else []
            after += [peer_tc] if peer_tc is not None else []
            if after:
                acts = lax.optimization_barrier((acts, *after))[0]
            peer_tc, peer_sc = back(xf, idx, gate, acts)
            done.append((xf, peer_tc, peer_sc))
            nxt = c + GROUPS_AHEAD + 1
            if nxt < n_groups:
                fronts[nxt] = front(nxt, lax.optimization_barrier((groups[nxt], peer_tc))[0], peer_sc)
        acc = jnp.zeros((b_ * s_, d), _f32)
        for c, (xf, p_tc, _) in enumerate(done):
            acc = _final(acc, xf, p_tc, g, normalize=last, tf=FINAL_TILE, first=0, row=c * tg)
        for c, (xf, _, p_sc) in enumerate(done):
            if p_sc is not None:
                acc = _final(acc, xf, p_sc, g, normalize=last, tf=FINAL_TILE, first=n_tc, row=c * tg + n_tc)
        x = acc.reshape(b_, s_, d)
    return x
```

```python
import functools

import jax
import jax.numpy as jnp
from jax import lax
from jax.experimental import pallas as pl
from jax.experimental.pallas import tpu as pltpu
from jax.experimental.pallas import tpu_sc as plsc

EPS = 1e-6
CHUNK = 64
SUB = 16
HEADS = 8
HEAD_DIM = 64
HG_WIDTH = HEADS * HEAD_DIM
GROUP = 256
N_GROUPS = HG_WIDTH // GROUP
CONV_K = 3
PEER_HEADS = 8
PEER_NKEYS = 128
PEER_HALF = 128
PEER_TOPK = 16
PEER_SLOTS = PEER_HEADS * PEER_TOPK

VMEM_LIMIT_BYTES = 56 * 1024 * 1024

MIX_TILE = 256
ROUTE_TILE = 256
EXPERT_TILE = 128
FINAL_TILE = 256
BATCH_GROUPS = 8
SC_SHARE = 512
GROUPS_AHEAD = 7

_f32 = jnp.float32
_bf16 = jnp.bfloat16


def _dot(a, b):
    return jnp.dot(a, b, preferred_element_type=_f32)


def _dot_nt(a, b):
    return lax.dot_general(a, b, (((1,), (1,)), ((), ())), preferred_element_type=_f32)


def _dot_tn(a, b):
    return lax.dot_general(a, b, (((0,), (0,)), ((), ())), preferred_element_type=_f32)


def _split3(x):
    hi = x.astype(_bf16)
    r1 = x - hi.astype(_f32)
    mid = r1.astype(_bf16)
    lo = (r1 - mid.astype(_f32)).astype(_bf16)
    return hi, mid, lo


def _split2(x):
    hi = x.astype(_bf16)
    lo = (x - hi.astype(_f32)).astype(_bf16)
    return hi, lo


def _dot_exact_rhs01(x, m01):
    hi, mid, lo = _split3(x)
    return _dot(hi, m01) + _dot(mid, m01) + _dot(lo, m01)


def _dot_exact_lhs01(m01, x):
    hi, mid, lo = _split3(x)
    return _dot(m01, hi) + _dot(m01, mid) + _dot(m01, lo)


def _iota(shape, dim):
    return lax.broadcasted_iota(jnp.int32, shape, dim)


def _hgrn2_chunk(q, k, lf, v, state_ref):
    n_sub = CHUNK // SUB
    row = _iota((CHUNK, CHUNK), 0)
    col = _iota((CHUNK, CHUNK), 1)
    tril = (col <= row).astype(_bf16)
    b = _dot_exact_lhs01(tril, lf)

    b_end = [b[(j + 1) * SUB - 1:(j + 1) * SUB, :] for j in range(n_sub)]
    b_end_rows = jnp.concatenate([jnp.broadcast_to(e, (SUB, HG_WIDTH)) for e in b_end], axis=0)
    b_last = b_end[-1]

    q_in = (q * jnp.exp(b)).astype(_bf16)
    k_sub = (k * jnp.exp(b_end_rows - b)).astype(_bf16)
    k_out = (k * jnp.exp(b_last - b)).astype(_bf16)
    q_from = [(q * jnp.exp(jnp.minimum(b - b_end[j], 0.0))).astype(_bf16) for j in range(n_sub - 1)]
    v_b = v.astype(_bf16)

    gr = _iota((GROUP, GROUP), 0) // HEAD_DIM
    gc = _iota((GROUP, GROUP), 1) // HEAD_DIM
    head_mask = gr == gc
    t_blk = _iota((CHUNK, GROUP), 0) // SUB
    s_blk = (_iota((CHUNK, GROUP), 1) % HEAD_DIM) // SUB

    outs = []
    for g in range(N_GROUPS):
        sl = slice(g * GROUP, (g + 1) * GROUP)
        st = state_ref[g]
        o_g = _dot_nt(q_in[:, sl], st.astype(_bf16))

        zero_b = jnp.zeros((), _bf16)
        k_bd = jnp.where(head_mask, jnp.concatenate([k_sub[:, sl]] * (GROUP // CHUNK), axis=0), zero_b)
        v_bd = jnp.where(head_mask, jnp.concatenate([v_b[:, sl]] * (GROUP // CHUNK), axis=0), zero_b)
        q_stack = jnp.concatenate([qf[:, sl] for qf in q_from], axis=0)
        r = _dot_nt(q_stack, k_bd)
        scores = jnp.zeros((CHUNK, GROUP), _f32)
        for j in range(n_sub - 1):
            sel = (s_blk == j) & (t_blk > j)
            scores = jnp.where(sel, r[j * CHUNK:(j + 1) * CHUNK, :], scores)
        o_g = o_g + _dot(scores.astype(_bf16), v_bd)
        outs.append(o_g)

        upd = _dot_tn(v_b[:, sl], k_out[:, sl])
        decay = jnp.exp(b_last[:, sl])
        state_ref[g] = st * decay + jnp.where(head_mask, upd, 0.0)
    o = jnp.concatenate(outs, axis=1)

    ones_bd = ((_iota((HG_WIDTH, HG_WIDTH), 0) // HEAD_DIM)
               == (_iota((HG_WIDTH, HG_WIDTH), 1) // HEAD_DIM)).astype(_bf16)
    t_in_sub = _iota((CHUNK, HG_WIDTH), 0) % SUB
    for lag in range(SUB):
        if lag == 0:
            p = q * k
            v_l = v
        else:
            valid = t_in_sub >= lag
            k_l = pltpu.roll(k, lag, 0)
            b_l = pltpu.roll(b, lag, 0)
            v_l = pltpu.roll(v, lag, 0)
            p = jnp.where(valid, q * k_l * jnp.exp(jnp.minimum(b - b_l, 0.0)), 0.0)
        s_l = _dot(p.astype(_bf16), ones_bd)
        o = o + s_l * v_l
    return o


def _mix_kernel(x_ref, g_ref, win_ref, lb_ref, hgn_ref, convw_ref, pa_ref, pb_ref, wo_ref,
                out_ref, state_ref, carry_ref, q_s, k_s, lf_s, v_s, o_s):
    ts = x_ref.shape[0]
    d_model = x_ref.shape[1]
    w = HG_WIDTH

    @pl.when(pl.program_id(1) == 0)
    def _():
        state_ref[...] = jnp.zeros_like(state_ref)
        carry_ref[...] = jnp.zeros_like(carry_ref)

    x = x_ref[...]
    h = x * lax.rsqrt(jnp.mean(x * x, axis=-1, keepdims=True) + EPS) * g_ref[...]
    hb = h.astype(_bf16)

    def proj(i, width=w):
        return _dot(hb, win_ref[:, i * w:i * w + width])

    lb = lb_ref[...]
    q_s[...] = jax.nn.silu(proj(0)) * (HEAD_DIM ** -0.5)
    forget = lb + (1.0 - lb) * jax.nn.sigmoid(proj(1))
    k_s[...] = 1.0 - forget
    lf_s[...] = jnp.log(forget)
    v_s[...] = proj(2)

    def chunk_body(c, carry):
        rows = pl.ds(pl.multiple_of(c * CHUNK, CHUNK), CHUNK)
        o_s[rows, :] = _hgrn2_chunk(q_s[rows, :], k_s[rows, :], lf_s[rows, :], v_s[rows, :], state_ref)
        return carry

    lax.fori_loop(0, ts // CHUNK, chunk_body, 0)

    o = o_s[...]
    ones_bd = ((_iota((w, w), 0) // HEAD_DIM) == (_iota((w, w), 1) // HEAD_DIM)).astype(_bf16)
    ms = _dot_exact_rhs01(o * o, ones_bd) * (1.0 / HEAD_DIM)
    o = o * lax.rsqrt(ms + EPS) * hgn_ref[...]
    y_a = (o * jax.nn.silu(proj(3))).astype(_bf16)

    u = proj(5) * proj(6)
    prev = carry_ref[...]
    rowi = _iota((ts, w), 0)
    u1 = jnp.where(rowi >= 1, pltpu.roll(u, 1, 0), jnp.broadcast_to(prev[7:8, :], (ts, w)))
    u2 = jnp.where(rowi >= 2, pltpu.roll(u, 2, 0),
                   jnp.where(rowi == 1, jnp.broadcast_to(prev[7:8, :], (ts, w)),
                             jnp.broadcast_to(prev[6:7, :], (ts, w))))
    carry_ref[...] = u[ts - 8:, :]
    cw = convw_ref[...]
    y_b = (proj(4) * (cw[0:1, :] * u2 + cw[1:2, :] * u1 + cw[2:3, :] * u)).astype(_bf16)

    g_a = jax.nn.sigmoid(proj(7, d_model))
    g_b = jax.nn.sigmoid(_dot(hb, win_ref[:, 7 * w + d_model:7 * w + 2 * d_model]))
    merged = g_a * _dot(y_a, pa_ref[...]) + g_b * _dot(y_b, pb_ref[...])
    out_ref[...] = x + _dot(merged.astype(_bf16), wo_ref[...])


def _const_spec(shape):
    nd = len(shape)
    return pl.BlockSpec(shape, lambda *_: (0,) * nd, pipeline_mode=pl.Buffered(1))


def _mix(x, norm_g, w_in, lb, hg_norm_g, conv_w, w_a, w_b, w_o, *, ts):
    b_, s_, d = x.shape
    in_cols = w_in.shape[1]
    w = HG_WIDTH
    grid = (b_, s_ // ts)
    return pl.pallas_call(
        _mix_kernel,
        grid=grid,
        in_specs=[
            pl.BlockSpec((None, ts, d), lambda b, s: (b, s, 0)),
            _const_spec((1, d)),
            _const_spec((d, in_cols)),
            _const_spec((1, w)),
            _const_spec((1, w)),
            _const_spec((CONV_K, w)),
            _const_spec((w, d)),
            _const_spec((w, d)),
            _const_spec((d, d)),
        ],
        out_specs=pl.BlockSpec((None, ts, d), lambda b, s: (b, s, 0)),
        out_shape=jax.ShapeDtypeStruct((b_, s_, d), _f32),
        scratch_shapes=[
            pltpu.VMEM((N_GROUPS, GROUP, GROUP), _f32),
            pltpu.VMEM((8, w), _f32),
            pltpu.VMEM((ts, w), _f32),
            pltpu.VMEM((ts, w), _f32),
            pltpu.VMEM((ts, w), _f32),
            pltpu.VMEM((ts, w), _f32),
            pltpu.VMEM((ts, w), _f32),
        ],
        compiler_params=pltpu.CompilerParams(
            dimension_semantics=("parallel", "arbitrary"),
            vmem_limit_bytes=VMEM_LIMIT_BYTES),
        name="mix",
    )(x, norm_g, w_in, lb, hg_norm_g, conv_w, w_a, w_b, w_o)


def _stair_pairs():
    pairs = [(a, c) for a in range(PEER_TOPK) for c in range(PEER_TOPK) if (a + 1) * (c + 1) <= PEER_TOPK]
    rows = -(-len(pairs) // 8) * 8
    ranks = jnp.arange(PEER_TOPK)[None, :]
    a_col = jnp.asarray([a for a, _ in pairs] + [-1] * (rows - len(pairs)))[:, None]
    c_col = jnp.asarray([c for _, c in pairs] + [-1] * (rows - len(pairs)))[:, None]
    pad = jnp.where(a_col < 0, -jnp.inf, 0.0).astype(_f32)
    return (a_col == ranks).astype(_bf16), (c_col == ranks).astype(_bf16), pad


ROUTE_LANES = 128


def _route_kernel(x_ref, g_ref, wq_ref, k1_ref, k2_ref, sa_ref, sc_ref, pad_ref, h_ref, idx_ref, gate_ref,
                  idx_t, e_t, top_ref):
    tr = x_ref.shape[0]
    n_cand = sa_ref.shape[0]
    x = x_ref[...]
    h = x * lax.rsqrt(jnp.mean(x * x, axis=-1, keepdims=True) + EPS) * g_ref[...]
    h_ref[...] = h
    hb = h.astype(_bf16)

    key_row = _iota((PEER_NKEYS, ROUTE_LANES), 0).astype(_f32)
    cand_row = _iota((n_cand, tr), 0).astype(_f32)
    neg_inf = jnp.float32(-jnp.inf)

    def extract_max(s):
        m = jnp.max(s, axis=0, keepdims=True)
        i = jnp.min(jnp.where(s == m, key_row, float(PEER_NKEYS)), axis=0, keepdims=True)
        return m, i, jnp.where(key_row == i, neg_inf, s)

    def head_body(hd, carry):
        q1 = _dot(hb, wq_ref[hd, 0]).astype(_bf16)
        q2 = _dot(hb, wq_ref[hd, 1]).astype(_bf16)
        s1 = _dot_nt(k1_ref[hd], q1)
        s2 = _dot_nt(k2_ref[hd], q2)

        for lt in range(tr // ROUTE_LANES):
            lanes = slice(lt * ROUTE_LANES, (lt + 1) * ROUTE_LANES)

            def half_body(k, c):
                m1, i1, r1 = extract_max(c[0])
                m2, i2, r2 = extract_max(c[1])
                for j, row in enumerate((m1, i1, m2, i2)):
                    top_ref[lt, j, pl.ds(k, 1), :] = row
                return r1, r2

            lax.fori_loop(0, PEER_TOPK, half_body, (s1[:, lanes], s2[:, lanes]))

        def top(j):
            return jnp.concatenate([top_ref[lt, j] for lt in range(tr // ROUTE_LANES)], axis=1)

        sa = sa_ref[...]
        sc = sc_ref[...]
        cand_s = _dot_exact_lhs01(sa, top(0)) + _dot_exact_lhs01(sc, top(2)) + pad_ref[...]
        cand_i = (_dot(sa, top(1).astype(_bf16)) * float(PEER_NKEYS)
                  + _dot(sc, top(3).astype(_bf16)))

        def pick_body(k, c):
            cand_s, denom, m_first = c
            m = jnp.max(cand_s, axis=0, keepdims=True)
            pos = jnp.min(jnp.where(cand_s == m, cand_row, float(n_cand)), axis=0, keepdims=True)
            hit = cand_row == pos
            eid = jnp.max(jnp.where(hit, cand_i, -1.0), axis=0, keepdims=True)
            m_first = jnp.where(k == 0, m, m_first)
            e = jnp.exp(m - m_first)
            slot = hd * PEER_TOPK + k
            idx_t[pl.ds(slot, 1), :] = eid
            e_t[pl.ds(slot, 1), :] = e
            return jnp.where(hit, neg_inf, cand_s), denom + e, m_first

        zero_row = jnp.zeros((1, tr), _f32)
        _, denom, _ = lax.fori_loop(0, PEER_TOPK, pick_body, (cand_s, zero_row, zero_row))
        rows = pl.ds(pl.multiple_of(hd * PEER_TOPK, PEER_TOPK), PEER_TOPK)
        e_t[rows, :] = e_t[rows, :] / denom
        return carry

    lax.fori_loop(0, PEER_HEADS, head_body, 0)
    idx_ref[...] = idx_t[...].T.astype(jnp.int32)
    gate_ref[...] = e_t[...].T


def _route(x, norm_g, w_query, keys1, keys2, *, tr):
    t, d = x.shape
    sel_a, sel_c, pad = _stair_pairs()
    pad = jnp.broadcast_to(pad, (pad.shape[0], tr))
    return pl.pallas_call(
        _route_kernel,
        grid=(t // tr,),
        in_specs=[
            pl.BlockSpec((tr, d), lambda i: (i, 0)),
            _const_spec((1, d)),
            _const_spec(w_query.shape),
            _const_spec(keys1.shape),
            _const_spec(keys2.shape),
            _const_spec(sel_a.shape),
            _const_spec(sel_c.shape),
            _const_spec(pad.shape),
        ],
        out_specs=[
            pl.BlockSpec((tr, d), lambda i: (i, 0)),
            pl.BlockSpec((tr, PEER_SLOTS), lambda i: (i, 0)),
            pl.BlockSpec((tr, PEER_SLOTS), lambda i: (i, 0)),
        ],
        out_shape=[
            jax.ShapeDtypeStruct((t, d), _f32),
            jax.ShapeDtypeStruct((t, PEER_SLOTS), jnp.int32),
            jax.ShapeDtypeStruct((t, PEER_SLOTS), _f32),
        ],
        scratch_shapes=[
            pltpu.VMEM((PEER_SLOTS, tr), _f32),
            pltpu.VMEM((PEER_SLOTS, tr), _f32),
            pltpu.VMEM((tr // ROUTE_LANES, 4, PEER_TOPK, ROUTE_LANES), _f32),
        ],
        compiler_params=pltpu.CompilerParams(
            dimension_semantics=("parallel",),
            vmem_limit_bytes=VMEM_LIMIT_BYTES),
        name="route",
    )(x, norm_g, w_query, keys1, keys2, sel_a, sel_c, pad)


def _scores_kernel(x_ref, g_ref, wq_ref, k1_ref, k2_ref, h_ref, s_ref):
    x = x_ref[...]
    h = x * lax.rsqrt(jnp.mean(x * x, axis=-1, keepdims=True) + EPS) * g_ref[...]
    h_ref[...] = h
    hb = h.astype(_bf16)
    for hd in range(PEER_HEADS):
        for half, keys_ref in enumerate((k1_ref, k2_ref)):
            q = _dot(hb, wq_ref[hd, half]).astype(_bf16)
            col = (2 * hd + half) * PEER_NKEYS
            s_ref[:, col:col + PEER_NKEYS] = _dot_nt(q, keys_ref[hd])


def _scores(x, norm_g, w_query, keys1, keys2, *, tr):
    t, d = x.shape
    width = 2 * PEER_HEADS * PEER_NKEYS
    return pl.pallas_call(
        _scores_kernel,
        grid=(t // tr,),
        in_specs=[
            pl.BlockSpec((tr, d), lambda i: (i, 0)),
            _const_spec((1, d)),
            _const_spec(w_query.shape),
            _const_spec(keys1.shape),
            _const_spec(keys2.shape),
        ],
        out_specs=[
            pl.BlockSpec((tr, d), lambda i: (i, 0)),
            pl.BlockSpec((tr, width), lambda i: (i, 0)),
        ],
        out_shape=[
            jax.ShapeDtypeStruct((t, d), _f32),
            jax.ShapeDtypeStruct((t, width), _f32),
        ],
        compiler_params=pltpu.CompilerParams(
            dimension_semantics=("parallel",),
            vmem_limit_bytes=VMEM_LIMIT_BYTES),
        name="scores",
    )(x, norm_g, w_query, keys1, keys2)


SC_CORES = 2
SC_SUBCORES = 16
SC_LANES = 16
SC_WORKERS = SC_CORES * SC_SUBCORES
SC_TOKENS = 8
SC_GATHER = 32
SC_BLOCK = 8
SC_UNROLL = 2
SC_COLS = 8
HI_MASK = -65536


def _sc_mesh():
    return plsc.VectorSubcoreMesh(core_axis_name="c", subcore_axis_name="s")


def _sc_worker_base(per_worker):
    return (lax.axis_index("s") * SC_CORES + lax.axis_index("c")) * per_worker


def _sc_gather_loop(tab_hbm, idx_v, bufs, consume):
    n_parts = PEER_SLOTS // SC_GATHER
    n_gathers = SC_TOKENS * n_parts

    def gather(g, parity):
        rows, sem = bufs[parity]
        i = g // n_parts
        col = pl.multiple_of((g % n_parts) * SC_GATHER, SC_GATHER)
        return pltpu.make_async_copy(tab_hbm.at[idx_v.at[i, pl.ds(col, SC_GATHER)]], rows, sem)

    gather(0, 0).start()

    @pl.loop(0, n_gathers // 2)
    def _(pair):
        g = 2 * pair
        gather(g + 1, 1).start()
        gather(g, 0).wait()
        consume(g // n_parts, g % n_parts, bufs[0][0])

        @pl.when(g + 2 < n_gathers)
        def _():
            gather(g + 2, 0).start()

        gather(g + 1, 1).wait()
        consume((g + 1) // n_parts, (g + 1) % n_parts, bufs[1][0])


def _pack_halves(tab):
    half = tab.shape[1] // 2
    bits = lax.bitcast_convert_type(tab.astype(_bf16), jnp.uint16).astype(jnp.uint32)
    return lax.bitcast_convert_type(bits[:, :half] | (bits[:, half:] << 16), jnp.int32)


def _sc_unpack(words):
    lo = lax.bitcast_convert_type(words << 16, _f32)
    hi = lax.bitcast_convert_type(words & HI_MASK, _f32)
    return lo, hi


def _stair_vectors():
    pairs = [(a, c) for a in range(PEER_TOPK) for c in range(PEER_TOPK) if (a + 1) * (c + 1) <= PEER_TOPK]
    n = -(-len(pairs) // SC_LANES)
    fill = n * SC_LANES - len(pairs)
    a = jnp.asarray([p[0] for p in pairs] + [0] * fill, jnp.int32).reshape(n, SC_LANES)
    c = jnp.asarray([p[1] for p in pairs] + [0] * fill, jnp.int32).reshape(n, SC_LANES)
    pad = jnp.asarray([0.0] * len(pairs) + [-jnp.inf] * fill, _f32).reshape(n, SC_LANES)
    return a, c, pad


def _sc_route_dot(scores, tab, h):
    t, width = scores.shape
    d = h.shape[1]
    half_d = d // 2
    per_worker = t // SC_WORKERS
    pair_a, pair_c, pair_pad = _stair_vectors()
    n_cand = pair_a.shape[0]
    n_vec = PEER_NKEYS // SC_LANES

    @functools.partial(
        pl.kernel, mesh=_sc_mesh(),
        out_type=(jax.ShapeDtypeStruct((t, PEER_SLOTS), jnp.int32),
                  jax.ShapeDtypeStruct((t, PEER_SLOTS), _f32),
                  jax.ShapeDtypeStruct((t, PEER_SLOTS), _f32)),
        scratch_types=[
            pltpu.VMEM((SC_TOKENS, width), _f32),
            pltpu.VMEM((SC_TOKENS, PEER_SLOTS), jnp.int32),
            pltpu.VMEM((SC_TOKENS, PEER_SLOTS), _f32),
            pltpu.VMEM((n_cand, SC_LANES), jnp.int32),
            pltpu.VMEM((n_cand, SC_LANES), jnp.int32),
            pltpu.VMEM((n_cand, SC_LANES), _f32),
            pltpu.VMEM((4, SC_LANES), _f32),
            pltpu.VMEM((SC_TOKENS, d), _f32),
            pltpu.VMEM((SC_TOKENS, PEER_SLOTS), _f32),
            pltpu.VMEM((SC_GATHER, half_d), jnp.int32),
            pltpu.VMEM((SC_GATHER, half_d), jnp.int32),
            pltpu.SemaphoreType.DMA,
            pltpu.SemaphoreType.DMA,
        ],
        compiler_params=pltpu.CompilerParams(needs_layout_passes=False),
        name="sc_route_dot",
    )
    def body(s_hbm, a_hbm, c_hbm, pad_hbm, tab_hbm, h_hbm, idx_hbm, gate_hbm, acts_hbm,
             s_v, idx_v, gate_v, a_v, c_v, pad_v, top_v, h_v, acts_v, rows0, rows1, sem0, sem1):
        base = _sc_worker_base(per_worker)
        lane = lax.iota(jnp.int32, SC_LANES)
        pltpu.sync_copy(a_hbm, a_v)
        pltpu.sync_copy(c_hbm, c_v)
        pltpu.sync_copy(pad_hbm, pad_v)

        def dots(i, part, rows):
            def block_body(blk, carry):
                row0 = blk * SC_LANES
                outv = jnp.zeros((SC_LANES,), _f32)
                for sub in range(SC_LANES // SC_BLOCK):
                    def chunk_body(jj, accs):
                        accs = list(accs)
                        for u in range(SC_UNROLL):
                            off = pl.multiple_of((jj * SC_UNROLL + u) * SC_LANES, SC_LANES)
                            h_lo = h_v[i, pl.ds(off, SC_LANES)]
                            h_hi = h_v[i, pl.ds(half_d + off, SC_LANES)]
                            for e in range(SC_BLOCK):
                                lo, hi = _sc_unpack(rows[row0 + sub * SC_BLOCK + e, pl.ds(off, SC_LANES)])
                                accs[e] = accs[e] + (lo * h_lo + hi * h_hi)
                        return tuple(accs)

                    accs = lax.fori_loop(0, half_d // SC_LANES // SC_UNROLL, chunk_body,
                                         tuple(jnp.zeros((SC_LANES,), _f32) for _ in range(SC_BLOCK)))
                    for e in range(SC_BLOCK):
                        outv = jnp.where(lane == sub * SC_BLOCK + e, jnp.sum(accs[e]), outv)
                col = pl.multiple_of(part * SC_GATHER + row0, SC_LANES)
                acts_v[i, pl.ds(col, SC_LANES)] = outv
                return carry

            lax.fori_loop(0, SC_GATHER // SC_LANES, block_body, 0)

        def sort_desc(k, v):
            return plsc.sort_key_val(k, v, descending=True)

        def merge(x, y):
            yk, yv = lax.rev(y[0], (0,)), lax.rev(y[1], (0,))
            take = x[0] >= yk
            return sort_desc(jnp.where(take, x[0], yk), jnp.where(take, x[1], yv))

        def top_of(vectors):
            while len(vectors) > 1:
                vectors = [merge(vectors[j], vectors[j + 1]) for j in range(0, len(vectors), 2)]
            return vectors[0]

        def head_body(i, hd):
            halves = []
            for half in range(2):
                col = (2 * hd + half) * PEER_NKEYS
                vecs = [sort_desc(s_v[i, pl.ds(pl.multiple_of(col + j * SC_LANES, SC_LANES), SC_LANES)],
                                  lane + j * SC_LANES) for j in range(n_vec)]
                halves.append(top_of(vecs))
            (v1, i1), (v2, i2) = halves
            top_v[0, :] = v1
            top_v[1, :] = i1.astype(_f32)
            top_v[2, :] = v2
            top_v[3, :] = i2.astype(_f32)

            def pick(row, pos):
                return plsc.load_gather(top_v, [jnp.full((SC_LANES,), row, jnp.int32), pos])

            cands = []
            for q in range(n_cand):
                a, c = a_v[q, :], c_v[q, :]
                cs = pick(0, a) + pick(2, c) + pad_v[q, :]
                ci = pick(1, a) * float(PEER_NKEYS) + pick(3, c)
                cands.append(sort_desc(cs, ci))
            top_s, top_i = top_of(cands)
            e = jnp.exp(top_s - jnp.max(top_s))
            slots = pl.ds(pl.multiple_of(hd * PEER_TOPK, PEER_TOPK), PEER_TOPK)
            idx_v[i, slots] = top_i.astype(jnp.int32)
            gate_v[i, slots] = e / jnp.sum(e)

        @pl.loop(0, per_worker // SC_TOKENS)
        def _(step):
            tok = pl.multiple_of(base + step * SC_TOKENS, SC_TOKENS)
            pltpu.sync_copy(s_hbm.at[pl.ds(tok, SC_TOKENS)], s_v)
            pltpu.sync_copy(h_hbm.at[pl.ds(tok, SC_TOKENS)], h_v)

            @pl.loop(0, SC_TOKENS * PEER_HEADS)
            def _(n):
                head_body(n // PEER_HEADS, n % PEER_HEADS)

            pltpu.sync_copy(idx_v, idx_hbm.at[pl.ds(tok, SC_TOKENS)])
            pltpu.sync_copy(gate_v, gate_hbm.at[pl.ds(tok, SC_TOKENS)])
            _sc_gather_loop(tab_hbm, idx_v, ((rows0, sem0), (rows1, sem1)), dots)
            pltpu.sync_copy(acts_v, acts_hbm.at[pl.ds(tok, SC_TOKENS)])

    return body(scores, pair_a, pair_c, pair_pad, tab, h)


def _sc_vaxpy(idx, w, tab, *, first, count):
    half = tab.shape[1]
    d = 2 * half
    per_worker = count // SC_WORKERS
    span = SC_COLS * SC_LANES

    @functools.partial(
        pl.kernel, mesh=_sc_mesh(),
        out_type=jax.ShapeDtypeStruct((count, d), _f32),
        scratch_types=[
            pltpu.VMEM((SC_TOKENS, PEER_SLOTS), jnp.int32),
            pltpu.VMEM((SC_TOKENS, PEER_SLOTS), _f32),
            pltpu.VMEM((SC_TOKENS, d), _f32),
            pltpu.VMEM((SC_GATHER, half), jnp.int32),
            pltpu.VMEM((SC_GATHER, half), jnp.int32),
            pltpu.SemaphoreType.DMA,
            pltpu.SemaphoreType.DMA,
        ],
        compiler_params=pltpu.CompilerParams(needs_layout_passes=False),
        name="sc_vaxpy",
    )
    def body(idx_hbm, w_hbm, tab_hbm, out_hbm, idx_v, w_v, out_v, rows0, rows1, sem0, sem1):
        base = _sc_worker_base(per_worker)

        def accumulate(i, part, rows):
            i_vec = jnp.full((SC_LANES,), i, jnp.int32)

            def span_body(cq, carry):
                def cols(c, offset=0):
                    return pl.ds(pl.multiple_of(offset + cq * span + c * SC_LANES, SC_LANES), SC_LANES)

                def expert_body(e, accs):
                    k_vec = jnp.full((SC_LANES,), part * SC_GATHER + e, jnp.int32)
                    wv = plsc.load_gather(w_v, [i_vec, k_vec])
                    new = []
                    for c in range(SC_COLS):
                        lo, hi = _sc_unpack(rows[e, cols(c)])
                        new += [accs[2 * c] + lo * wv, accs[2 * c + 1] + hi * wv]
                    return tuple(new)

                init = []
                for c in range(SC_COLS):
                    init += [out_v[i, cols(c)], out_v[i, cols(c, half)]]
                accs = lax.fori_loop(0, SC_GATHER, expert_body, tuple(init))
                for c in range(SC_COLS):
                    out_v[i, cols(c)] = accs[2 * c]
                    out_v[i, cols(c, half)] = accs[2 * c + 1]
                return carry

            lax.fori_loop(0, half // span, span_body, 0)

        @pl.loop(0, per_worker // SC_TOKENS)
        def _(step):
            off = pl.multiple_of(base + step * SC_TOKENS, SC_TOKENS)
            pltpu.sync_copy(idx_hbm.at[pl.ds(first + off, SC_TOKENS)], idx_v)
            pltpu.sync_copy(w_hbm.at[pl.ds(first + off, SC_TOKENS)], w_v)

            @pl.loop(0, SC_TOKENS)
            def _(i):
                @pl.loop(0, d // SC_LANES)
                def _(j):
                    out_v[i, pl.ds(pl.multiple_of(j * SC_LANES, SC_LANES), SC_LANES)] = (
                        jnp.zeros((SC_LANES,), _f32))

            _sc_gather_loop(tab_hbm, idx_v, ((rows0, sem0), (rows1, sem1)), accumulate)
            pltpu.sync_copy(out_v, out_hbm.at[pl.ds(off, SC_TOKENS)])

    return body(idx, w, tab)


ROW_CHUNKS = 8
PACK_ROWS = ROW_CHUNKS // 2
ROW_BUFFERS = 4


def _pack_table(tab):
    n, d = tab.shape
    bits = lax.bitcast_convert_type(tab.astype(_bf16), jnp.uint16).astype(jnp.uint32)
    bits = bits.reshape(n, PACK_ROWS, 2, d // ROW_CHUNKS)
    word = bits[:, :, 0, :] | (bits[:, :, 1, :] << 16)
    return lax.bitcast_convert_type(word, jnp.int32)


def _gate_weights_kernel(acts_ref, gate_ref, w_ref):
    a = acts_ref[...]
    gelu = 0.5 * a * (1.0 + lax.erf(a * (2.0 ** -0.5)))
    w_ref[...] = gate_ref[...] * gelu


def _gate_weights(acts, gate, *, tw):
    t, n = acts.shape
    spec = pl.BlockSpec((tw, n), lambda i: (i, 0))
    return pl.pallas_call(
        _gate_weights_kernel,
        grid=(t // tw,),
        in_specs=[spec, spec],
        out_specs=spec,
        out_shape=jax.ShapeDtypeStruct((t, n), _f32),
        compiler_params=pltpu.CompilerParams(dimension_semantics=("parallel",)),
        name="gate_weights",
    )(acts, gate)


def _gather_rows(idx_ref, t, tab_ref, rows_ref):
    for k in range(PEER_SLOTS):
        row = pl.multiple_of(idx_ref[t, k], PACK_ROWS)
        rows_ref[k * PACK_ROWS:(k + 1) * PACK_ROWS, :] = tab_ref[pl.ds(row, PACK_ROWS), :]


def _rows_matrix(rows_ref):
    return pltpu.bitcast(rows_ref[...], _bf16)


def _token_loop(tb, idx_ref, tab_ref, row_bufs, compute):
    n = len(row_bufs)
    ahead = 2
    for j in range(ahead):
        _gather_rows(idx_ref, j, tab_ref, row_bufs[j])

    def body(i, carry):
        t0 = n * i
        for j in range(n):
            nxt = jnp.minimum(t0 + j + ahead, tb - 1)
            _gather_rows(idx_ref, nxt, tab_ref, row_bufs[(j + ahead) % n])
            compute(t0 + j, _rows_matrix(row_bufs[j]))
        return carry

    lax.fori_loop(0, tb // n, body, 0)


def _chunk_diag_mask():
    shape = (ROW_CHUNKS, PEER_SLOTS * ROW_CHUNKS)
    return (_iota(shape, 1) % ROW_CHUNKS) == _iota(shape, 0)


def _vaxpy_kernel(idx_ref, w_ref, tab_ref, out_ref, wrep_ref, *row_bufs):
    tb = out_ref.shape[0]
    diag = _chunk_diag_mask()
    shape = (PEER_SLOTS, PEER_SLOTS * ROW_CHUNKS)
    spread = (_iota(shape, 0) == (_iota(shape, 1) // ROW_CHUNKS)).astype(_bf16)
    wrep_ref[...] = _dot_exact_rhs01(w_ref[...], spread)

    def compute(t, m):
        w_row = jnp.broadcast_to(wrep_ref[pl.ds(t, 1), :], diag.shape)
        w_hi, w_lo = _split2(jnp.where(diag, w_row, 0.0))
        out_ref[t] = _dot(w_hi, m) + _dot(w_lo, m)

    _token_loop(tb, idx_ref, tab_ref, row_bufs, compute)


def _vaxpy(idx, w, tab, *, tb, count):
    return pl.pallas_call(
        _vaxpy_kernel,
        grid=(count // tb,),
        in_specs=[
            pl.BlockSpec((tb, PEER_SLOTS), lambda i: (i, 0), memory_space=pltpu.SMEM),
            pl.BlockSpec((tb, PEER_SLOTS), lambda i: (i, 0)),
            _const_spec(tab.shape),
        ],
        out_specs=pl.BlockSpec((tb, ROW_CHUNKS, 128), lambda i: (i, 0, 0)),
        out_shape=jax.ShapeDtypeStruct((count, ROW_CHUNKS, 128), _f32),
        scratch_shapes=[
            pltpu.VMEM((tb, PEER_SLOTS * ROW_CHUNKS), _f32),
        ] + [pltpu.VMEM((PEER_SLOTS * PACK_ROWS, 128), jnp.int32)
             for _ in range(ROW_BUFFERS)],
        compiler_params=pltpu.CompilerParams(
            dimension_semantics=("parallel",),
            vmem_limit_bytes=VMEM_LIMIT_BYTES),
        name="vaxpy",
    )(idx, w, tab)


def _final_kernel(x_ref, p_ref, g_ref, acc_ref, out_ref, *, normalize):
    del acc_ref
    x = x_ref[...] + p_ref[...]
    if normalize:
        x = x * lax.rsqrt(jnp.mean(x * x, axis=-1, keepdims=True) + EPS) * g_ref[...]
    out_ref[...] = x


def _final(acc, x, peer, g, *, normalize, tf, first, row):
    t, d = peer.shape
    x_off = first // tf
    out_off = row // tf
    return pl.pallas_call(
        functools.partial(_final_kernel, normalize=normalize),
        grid=(t // tf,),
        in_specs=[
            pl.BlockSpec((tf, d), lambda i: (i + x_off, 0)),
            pl.BlockSpec((tf, d), lambda i: (i, 0)),
            _const_spec((1, d)),
            pl.BlockSpec(memory_space=pl.ANY),
        ],
        out_specs=pl.BlockSpec((tf, d), lambda i: (i + out_off, 0)),
        out_shape=jax.ShapeDtypeStruct(acc.shape, _f32),
        input_output_aliases={3: 0},
        compiler_params=pltpu.CompilerParams(dimension_semantics=("parallel",)),
        name="final_norm",
    )(x, peer, g, acc)


def kernel(x, norm_mix_g, w_in, hg_lb_logits, hg_out_norm_g, conv_w, w_branch_hg, w_branch_conv, w_out, norm_ffn_g, peer_w_query, peer_keys1, peer_keys2, peer_u, peer_v, norm_final_g):
    b_, s_, d = x.shape
    depth = w_in.shape[0]
    lb_all = jnp.cumsum(jax.nn.softmax(hg_lb_logits.astype(_f32), axis=0), axis=0)
    n_groups = BATCH_GROUPS if b_ % BATCH_GROUPS == 0 else 1
    bg = b_ // n_groups
    tg = bg * s_
    n_sc = SC_SHARE if SC_SHARE < tg else 0
    n_tc = tg - n_sc
    for l in range(depth):
        wq = peer_w_query[l].astype(_bf16).reshape(d, PEER_HEADS, 2, PEER_HALF).transpose(1, 2, 0, 3)
        u_sc = _pack_halves(peer_u[l])
        last = l == depth - 1
        g = norm_final_g[None] if last else jnp.ones((1, d), _f32)
        def front(c, x_in, sc_before=None):
            xc = _mix(x_in, norm_mix_g[l][None], w_in[l].astype(_bf16), lb_all[l][None],
                      hg_out_norm_g[l][None], conv_w[l], w_branch_hg[l].astype(_bf16),
                      w_branch_conv[l].astype(_bf16), w_out[l].astype(_bf16), ts=MIX_TILE)
            xf = xc.reshape(tg, d)
            h, scores = _scores(xf, norm_ffn_g[l][None], wq,
                                peer_keys1[l].astype(_bf16), peer_keys2[l].astype(_bf16), tr=ROUTE_TILE)
            if sc_before is not None:
                scores = lax.optimization_barrier((scores, sc_before))[0]
            idx, gate, acts = _sc_route_dot(scores, u_sc, h)
            return (xf, idx, gate, acts), scores

        def back(xf, idx, gate, acts):
            w = _gate_weights(acts, gate, tw=FINAL_TILE)
            peer_tc = _vaxpy(idx * PACK_ROWS, w, v_tab.reshape(-1, 128), tb=EXPERT_TILE, count=n_tc).reshape(n_tc, d)
            peer_sc = _sc_vaxpy(idx, w, v_sc, first=n_tc, count=n_sc) if n_sc else None
            return peer_tc, peer_sc

        groups = [x[c * bg:(c + 1) * bg] for c in range(n_groups)]
        fronts = {c: front(c, groups[c]) for c in range(min(GROUPS_AHEAD + 1, n_groups))}
        v_rows = lax.optimization_barrier((peer_v[l], fronts[0][1]))[0]
        v_tab = _pack_table(v_rows)
        v_sc = _pack_halves(v_rows)
        acc = jnp.zeros((b_ * s_, d), _f32)
        sc_parts, peer_tc = [], None
        for c in range(n_groups):
            (xf, idx, gate, acts), _ = fronts.pop(c)
            after = [fronts[c + GROUPS_AHEAD][1]] if c + GROUPS_AHEAD in fronts else []
            after += [acc] if c else []
            if after:
                acts = lax.optimization_barrier((acts, *after))[0]
            peer_tc, peer_sc = back(xf, idx, gate, acts)
            acc = _final(acc, xf, peer_tc, g, normalize=last, tf=FINAL_TILE, first=0, row=c * tg)
            sc_parts.append((c, xf, peer_sc))
            nxt = c + GROUPS_AHEAD + 1
            if nxt < n_groups:
                fronts[nxt] = front(nxt, lax.optimization_barrier((groups[nxt], peer_tc))[0], peer_sc)
        for c, xf, p_sc in sc_parts:
            if p_sc is not None:
                acc = _final(acc, xf, p_sc, g, normalize=last, tf=FINAL_TILE, first=n_tc, row=c * tg + n_tc)
        x = acc.reshape(b_, s_, d)
    return x
```

```python
import functools

import jax
import jax.numpy as jnp
from jax import lax
from jax.experimental import pallas as pl
from jax.experimental.pallas import tpu as pltpu
from jax.experimental.pallas import tpu_sc as plsc

EPS = 1e-6
CHUNK = 64
SUB = 16
HEADS = 8
HEAD_DIM = 64
HG_WIDTH = HEADS * HEAD_DIM
GROUP = 256
N_GROUPS = HG_WIDTH // GROUP
CONV_K = 3
PEER_HEADS = 8
PEER_NKEYS = 128
PEER_HALF = 128
PEER_TOPK = 16
PEER_SLOTS = PEER_HEADS * PEER_TOPK

VMEM_LIMIT_BYTES = 56 * 1024 * 1024

MIX_TILE = 256
ROUTE_TILE = 256
EXPERT_TILE = 128
FINAL_TILE = 256
BATCH_GROUPS = 8
SC_SHARE = 256
SC_SHARE_LAST = 2048

_f32 = jnp.float32
_bf16 = jnp.bfloat16


def _dot(a, b):
    return jnp.dot(a, b, preferred_element_type=_f32)


def _dot_nt(a, b):
    return lax.dot_general(a, b, (((1,), (1,)), ((), ())), preferred_element_type=_f32)


def _dot_tn(a, b):
    return lax.dot_general(a, b, (((0,), (0,)), ((), ())), preferred_element_type=_f32)


def _split3(x):
    hi = x.astype(_bf16)
    r1 = x - hi.astype(_f32)
    mid = r1.astype(_bf16)
    lo = (r1 - mid.astype(_f32)).astype(_bf16)
    return hi, mid, lo


def _split2(x):
    hi = x.astype(_bf16)
    lo = (x - hi.astype(_f32)).astype(_bf16)
    return hi, lo


def _dot_exact_rhs01(x, m01):
    hi, mid, lo = _split3(x)
    return _dot(hi, m01) + _dot(mid, m01) + _dot(lo, m01)


def _dot_exact_lhs01(m01, x):
    hi, mid, lo = _split3(x)
    return _dot(m01, hi) + _dot(m01, mid) + _dot(m01, lo)


def _iota(shape, dim):
    return lax.broadcasted_iota(jnp.int32, shape, dim)


def _hgrn2_chunk(q, k, lf, v, state_ref):
    n_sub = CHUNK // SUB
    row = _iota((CHUNK, CHUNK), 0)
    col = _iota((CHUNK, CHUNK), 1)
    tril = (col <= row).astype(_bf16)
    b = _dot_exact_lhs01(tril, lf)

    b_end = [b[(j + 1) * SUB - 1:(j + 1) * SUB, :] for j in range(n_sub)]
    b_end_rows = jnp.concatenate([jnp.broadcast_to(e, (SUB, HG_WIDTH)) for e in b_end], axis=0)
    b_last = b_end[-1]

    q_in = (q * jnp.exp(b)).astype(_bf16)
    k_sub = (k * jnp.exp(b_end_rows - b)).astype(_bf16)
    k_out = (k * jnp.exp(b_last - b)).astype(_bf16)
    q_from = [(q * jnp.exp(jnp.minimum(b - b_end[j], 0.0))).astype(_bf16) for j in range(n_sub - 1)]
    v_b = v.astype(_bf16)

    gr = _iota((GROUP, GROUP), 0) // HEAD_DIM
    gc = _iota((GROUP, GROUP), 1) // HEAD_DIM
    head_mask = gr == gc
    t_blk = _iota((CHUNK, GROUP), 0) // SUB
    s_blk = (_iota((CHUNK, GROUP), 1) % HEAD_DIM) // SUB

    outs = []
    for g in range(N_GROUPS):
        sl = slice(g * GROUP, (g + 1) * GROUP)
        st = state_ref[g]
        o_g = _dot_nt(q_in[:, sl], st.astype(_bf16))

        zero_b = jnp.zeros((), _bf16)
        k_bd = jnp.where(head_mask, jnp.concatenate([k_sub[:, sl]] * (GROUP // CHUNK), axis=0), zero_b)
        v_bd = jnp.where(head_mask, jnp.concatenate([v_b[:, sl]] * (GROUP // CHUNK), axis=0), zero_b)
        q_stack = jnp.concatenate([qf[:, sl] for qf in q_from], axis=0)
        r = _dot_nt(q_stack, k_bd)
        scores = jnp.zeros((CHUNK, GROUP), _f32)
        for j in range(n_sub - 1):
            sel = (s_blk == j) & (t_blk > j)
            scores = jnp.where(sel, r[j * CHUNK:(j + 1) * CHUNK, :], scores)
        o_g = o_g + _dot(scores.astype(_bf16), v_bd)
        outs.append(o_g)

        upd = _dot_tn(v_b[:, sl], k_out[:, sl])
        decay = jnp.exp(b_last[:, sl])
        state_ref[g] = st * decay + jnp.where(head_mask, upd, 0.0)
    o = jnp.concatenate(outs, axis=1)

    ones_bd = ((_iota((HG_WIDTH, HG_WIDTH), 0) // HEAD_DIM)
               == (_iota((HG_WIDTH, HG_WIDTH), 1) // HEAD_DIM)).astype(_bf16)
    t_in_sub = _iota((CHUNK, HG_WIDTH), 0) % SUB
    for lag in range(SUB):
        if lag == 0:
            p = q * k
            v_l = v
        else:
            valid = t_in_sub >= lag
            k_l = pltpu.roll(k, lag, 0)
            b_l = pltpu.roll(b, lag, 0)
            v_l = pltpu.roll(v, lag, 0)
            p = jnp.where(valid, q * k_l * jnp.exp(jnp.minimum(b - b_l, 0.0)), 0.0)
        s_l = _dot(p.astype(_bf16), ones_bd)
        o = o + s_l * v_l
    return o


def _mix_kernel(x_ref, g_ref, win_ref, lb_ref, hgn_ref, convw_ref, pa_ref, pb_ref, wo_ref,
                out_ref, state_ref, carry_ref, q_s, k_s, lf_s, v_s, o_s):
    ts = x_ref.shape[0]
    d_model = x_ref.shape[1]
    w = HG_WIDTH

    @pl.when(pl.program_id(1) == 0)
    def _():
        state_ref[...] = jnp.zeros_like(state_ref)
        carry_ref[...] = jnp.zeros_like(carry_ref)

    x = x_ref[...]
    h = x * lax.rsqrt(jnp.mean(x * x, axis=-1, keepdims=True) + EPS) * g_ref[...]
    hb = h.astype(_bf16)

    def proj(i, width=w):
        return _dot(hb, win_ref[:, i * w:i * w + width])

    lb = lb_ref[...]
    q_s[...] = jax.nn.silu(proj(0)) * (HEAD_DIM ** -0.5)
    forget = lb + (1.0 - lb) * jax.nn.sigmoid(proj(1))
    k_s[...] = 1.0 - forget
    lf_s[...] = jnp.log(forget)
    v_s[...] = proj(2)

    def chunk_body(c, carry):
        rows = pl.ds(pl.multiple_of(c * CHUNK, CHUNK), CHUNK)
        o_s[rows, :] = _hgrn2_chunk(q_s[rows, :], k_s[rows, :], lf_s[rows, :], v_s[rows, :], state_ref)
        return carry

    lax.fori_loop(0, ts // CHUNK, chunk_body, 0)

    o = o_s[...]
    ones_bd = ((_iota((w, w), 0) // HEAD_DIM) == (_iota((w, w), 1) // HEAD_DIM)).astype(_bf16)
    ms = _dot_exact_rhs01(o * o, ones_bd) * (1.0 / HEAD_DIM)
    o = o * lax.rsqrt(ms + EPS) * hgn_ref[...]
    y_a = (o * jax.nn.silu(proj(3))).astype(_bf16)

    u = proj(5) * proj(6)
    prev = carry_ref[...]
    rowi = _iota((ts, w), 0)
    u1 = jnp.where(rowi >= 1, pltpu.roll(u, 1, 0), jnp.broadcast_to(prev[7:8, :], (ts, w)))
    u2 = jnp.where(rowi >= 2, pltpu.roll(u, 2, 0),
                   jnp.where(rowi == 1, jnp.broadcast_to(prev[7:8, :], (ts, w)),
                             jnp.broadcast_to(prev[6:7, :], (ts, w))))
    carry_ref[...] = u[ts - 8:, :]
    cw = convw_ref[...]
    y_b = (proj(4) * (cw[0:1, :] * u2 + cw[1:2, :] * u1 + cw[2:3, :] * u)).astype(_bf16)

    g_a = jax.nn.sigmoid(proj(7, d_model))
    g_b = jax.nn.sigmoid(_dot(hb, win_ref[:, 7 * w + d_model:7 * w + 2 * d_model]))
    merged = g_a * _dot(y_a, pa_ref[...]) + g_b * _dot(y_b, pb_ref[...])
    out_ref[...] = x + _dot(merged.astype(_bf16), wo_ref[...])


def _const_spec(shape):
    nd = len(shape)
    return pl.BlockSpec(shape, lambda *_: (0,) * nd, pipeline_mode=pl.Buffered(1))


def _mix(x, norm_g, w_in, lb, hg_norm_g, conv_w, w_a, w_b, w_o, *, ts):
    b_, s_, d = x.shape
    in_cols = w_in.shape[1]
    w = HG_WIDTH
    grid = (b_, s_ // ts)
    return pl.pallas_call(
        _mix_kernel,
        grid=grid,
        in_specs=[
            pl.BlockSpec((None, ts, d), lambda b, s: (b, s, 0)),
            _const_spec((1, d)),
            _const_spec((d, in_cols)),
            _const_spec((1, w)),
            _const_spec((1, w)),
            _const_spec((CONV_K, w)),
            _const_spec((w, d)),
            _const_spec((w, d)),
            _const_spec((d, d)),
        ],
        out_specs=pl.BlockSpec((None, ts, d), lambda b, s: (b, s, 0)),
        out_shape=jax.ShapeDtypeStruct((b_, s_, d), _f32),
        scratch_shapes=[
            pltpu.VMEM((N_GROUPS, GROUP, GROUP), _f32),
            pltpu.VMEM((8, w), _f32),
            pltpu.VMEM((ts, w), _f32),
            pltpu.VMEM((ts, w), _f32),
            pltpu.VMEM((ts, w), _f32),
            pltpu.VMEM((ts, w), _f32),
            pltpu.VMEM((ts, w), _f32),
        ],
        compiler_params=pltpu.CompilerParams(
            dimension_semantics=("parallel", "arbitrary"),
            vmem_limit_bytes=VMEM_LIMIT_BYTES),
        name="mix",
    )(x, norm_g, w_in, lb, hg_norm_g, conv_w, w_a, w_b, w_o)


def _stair_pairs():
    pairs = [(a, c) for a in range(PEER_TOPK) for c in range(PEER_TOPK) if (a + 1) * (c + 1) <= PEER_TOPK]
    rows = -(-len(pairs) // 8) * 8
    ranks = jnp.arange(PEER_TOPK)[None, :]
    a_col = jnp.asarray([a for a, _ in pairs] + [-1] * (rows - len(pairs)))[:, None]
    c_col = jnp.asarray([c for _, c in pairs] + [-1] * (rows - len(pairs)))[:, None]
    pad = jnp.where(a_col < 0, -jnp.inf, 0.0).astype(_f32)
    return (a_col == ranks).astype(_bf16), (c_col == ranks).astype(_bf16), pad


ROUTE_LANES = 128


def _route_kernel(x_ref, g_ref, wq_ref, k1_ref, k2_ref, sa_ref, sc_ref, pad_ref, h_ref, idx_ref, gate_ref,
                  idx_t, e_t, top_ref):
    tr = x_ref.shape[0]
    n_cand = sa_ref.shape[0]
    x = x_ref[...]
    h = x * lax.rsqrt(jnp.mean(x * x, axis=-1, keepdims=True) + EPS) * g_ref[...]
    h_ref[...] = h
    hb = h.astype(_bf16)

    key_row = _iota((PEER_NKEYS, ROUTE_LANES), 0).astype(_f32)
    cand_row = _iota((n_cand, tr), 0).astype(_f32)
    neg_inf = jnp.float32(-jnp.inf)

    def extract_max(s):
        m = jnp.max(s, axis=0, keepdims=True)
        i = jnp.min(jnp.where(s == m, key_row, float(PEER_NKEYS)), axis=0, keepdims=True)
        return m, i, jnp.where(key_row == i, neg_inf, s)

    def head_body(hd, carry):
        q1 = _dot(hb, wq_ref[hd, 0]).astype(_bf16)
        q2 = _dot(hb, wq_ref[hd, 1]).astype(_bf16)
        s1 = _dot_nt(k1_ref[hd], q1)
        s2 = _dot_nt(k2_ref[hd], q2)

        for lt in range(tr // ROUTE_LANES):
            lanes = slice(lt * ROUTE_LANES, (lt + 1) * ROUTE_LANES)

            def half_body(k, c):
                m1, i1, r1 = extract_max(c[0])
                m2, i2, r2 = extract_max(c[1])
                for j, row in enumerate((m1, i1, m2, i2)):
                    top_ref[lt, j, pl.ds(k, 1), :] = row
                return r1, r2

            lax.fori_loop(0, PEER_TOPK, half_body, (s1[:, lanes], s2[:, lanes]))

        def top(j):
            return jnp.concatenate([top_ref[lt, j] for lt in range(tr // ROUTE_LANES)], axis=1)

        sa = sa_ref[...]
        sc = sc_ref[...]
        cand_s = _dot_exact_lhs01(sa, top(0)) + _dot_exact_lhs01(sc, top(2)) + pad_ref[...]
        cand_i = (_dot(sa, top(1).astype(_bf16)) * float(PEER_NKEYS)
                  + _dot(sc, top(3).astype(_bf16)))

        def pick_body(k, c):
            cand_s, denom, m_first = c
            m = jnp.max(cand_s, axis=0, keepdims=True)
            pos = jnp.min(jnp.where(cand_s == m, cand_row, float(n_cand)), axis=0, keepdims=True)
            hit = cand_row == pos
            eid = jnp.max(jnp.where(hit, cand_i, -1.0), axis=0, keepdims=True)
            m_first = jnp.where(k == 0, m, m_first)
            e = jnp.exp(m - m_first)
            slot = hd * PEER_TOPK + k
            idx_t[pl.ds(slot, 1), :] = eid
            e_t[pl.ds(slot, 1), :] = e
            return jnp.where(hit, neg_inf, cand_s), denom + e, m_first

        zero_row = jnp.zeros((1, tr), _f32)
        _, denom, _ = lax.fori_loop(0, PEER_TOPK, pick_body, (cand_s, zero_row, zero_row))
        rows = pl.ds(pl.multiple_of(hd * PEER_TOPK, PEER_TOPK), PEER_TOPK)
        e_t[rows, :] = e_t[rows, :] / denom
        return carry

    lax.fori_loop(0, PEER_HEADS, head_body, 0)
    idx_ref[...] = idx_t[...].T.astype(jnp.int32)
    gate_ref[...] = e_t[...].T


def _route(x, norm_g, w_query, keys1, keys2, *, tr):
    t, d = x.shape
    sel_a, sel_c, pad = _stair_pairs()
    pad = jnp.broadcast_to(pad, (pad.shape[0], tr))
    return pl.pallas_call(
        _route_kernel,
        grid=(t // tr,),
        in_specs=[
            pl.BlockSpec((tr, d), lambda i: (i, 0)),
            _const_spec((1, d)),
            _const_spec(w_query.shape),
            _const_spec(keys1.shape),
            _const_spec(keys2.shape),
            _const_spec(sel_a.shape),
            _const_spec(sel_c.shape),
            _const_spec(pad.shape),
        ],
        out_specs=[
            pl.BlockSpec((tr, d), lambda i: (i, 0)),
            pl.BlockSpec((tr, PEER_SLOTS), lambda i: (i, 0)),
            pl.BlockSpec((tr, PEER_SLOTS), lambda i: (i, 0)),
        ],
        out_shape=[
            jax.ShapeDtypeStruct((t, d), _f32),
            jax.ShapeDtypeStruct((t, PEER_SLOTS), jnp.int32),
            jax.ShapeDtypeStruct((t, PEER_SLOTS), _f32),
        ],
        scratch_shapes=[
            pltpu.VMEM((PEER_SLOTS, tr), _f32),
            pltpu.VMEM((PEER_SLOTS, tr), _f32),
            pltpu.VMEM((tr // ROUTE_LANES, 4, PEER_TOPK, ROUTE_LANES), _f32),
        ],
        compiler_params=pltpu.CompilerParams(
            dimension_semantics=("parallel",),
            vmem_limit_bytes=VMEM_LIMIT_BYTES),
        name="route",
    )(x, norm_g, w_query, keys1, keys2, sel_a, sel_c, pad)


def _scores_kernel(x_ref, g_ref, wq_ref, k1_ref, k2_ref, h_ref, s_ref):
    x = x_ref[...]
    h = x * lax.rsqrt(jnp.mean(x * x, axis=-1, keepdims=True) + EPS) * g_ref[...]
    h_ref[...] = h
    hb = h.astype(_bf16)
    for hd in range(PEER_HEADS):
        for half, keys_ref in enumerate((k1_ref, k2_ref)):
            q = _dot(hb, wq_ref[hd, half]).astype(_bf16)
            col = (2 * hd + half) * PEER_NKEYS
            s_ref[:, col:col + PEER_NKEYS] = _dot_nt(q, keys_ref[hd])


def _scores(x, norm_g, w_query, keys1, keys2, *, tr):
    t, d = x.shape
    width = 2 * PEER_HEADS * PEER_NKEYS
    return pl.pallas_call(
        _scores_kernel,
        grid=(t // tr,),
        in_specs=[
            pl.BlockSpec((tr, d), lambda i: (i, 0)),
            _const_spec((1, d)),
            _const_spec(w_query.shape),
            _const_spec(keys1.shape),
            _const_spec(keys2.shape),
        ],
        out_specs=[
            pl.BlockSpec((tr, d), lambda i: (i, 0)),
            pl.BlockSpec((tr, width), lambda i: (i, 0)),
        ],
        out_shape=[
            jax.ShapeDtypeStruct((t, d), _f32),
            jax.ShapeDtypeStruct((t, width), _f32),
        ],
        compiler_params=pltpu.CompilerParams(
            dimension_semantics=("parallel",),
            vmem_limit_bytes=VMEM_LIMIT_BYTES),
        name="scores",
    )(x, norm_g, w_query, keys1, keys2)


SC_CORES = 2
SC_SUBCORES = 16
SC_LANES = 16
SC_WORKERS = SC_CORES * SC_SUBCORES
SC_TOKENS = 8
SC_GATHER = 32
SC_BLOCK = 8
SC_UNROLL = 2
SC_COLS = 8
HI_MASK = -65536


def _sc_mesh():
    return plsc.VectorSubcoreMesh(core_axis_name="c", subcore_axis_name="s")


def _sc_worker_base(per_worker):
    return (lax.axis_index("s") * SC_CORES + lax.axis_index("c")) * per_worker


def _sc_gather_loop(tab_hbm, idx_v, bufs, consume):
    n_parts = PEER_SLOTS // SC_GATHER
    n_gathers = SC_TOKENS * n_parts

    def gather(g, parity):
        rows, sem = bufs[parity]
        i = g // n_parts
        col = pl.multiple_of((g % n_parts) * SC_GATHER, SC_GATHER)
        return pltpu.make_async_copy(tab_hbm.at[idx_v.at[i, pl.ds(col, SC_GATHER)]], rows, sem)

    gather(0, 0).start()

    @pl.loop(0, n_gathers // 2)
    def _(pair):
        g = 2 * pair
        gather(g + 1, 1).start()
        gather(g, 0).wait()
        consume(g // n_parts, g % n_parts, bufs[0][0])

        @pl.when(g + 2 < n_gathers)
        def _():
            gather(g + 2, 0).start()

        gather(g + 1, 1).wait()
        consume((g + 1) // n_parts, (g + 1) % n_parts, bufs[1][0])


def _pack_halves(tab):
    half = tab.shape[1] // 2
    bits = lax.bitcast_convert_type(tab.astype(_bf16), jnp.uint16).astype(jnp.uint32)
    return lax.bitcast_convert_type(bits[:, :half] | (bits[:, half:] << 16), jnp.int32)


def _sc_unpack(words):
    lo = lax.bitcast_convert_type(words << 16, _f32)
    hi = lax.bitcast_convert_type(words & HI_MASK, _f32)
    return lo, hi


def _stair_vectors():
    pairs = [(a, c) for a in range(PEER_TOPK) for c in range(PEER_TOPK) if (a + 1) * (c + 1) <= PEER_TOPK]
    n = -(-len(pairs) // SC_LANES)
    fill = n * SC_LANES - len(pairs)
    a = jnp.asarray([p[0] for p in pairs] + [0] * fill, jnp.int32).reshape(n, SC_LANES)
    c = jnp.asarray([p[1] for p in pairs] + [0] * fill, jnp.int32).reshape(n, SC_LANES)
    pad = jnp.asarray([0.0] * len(pairs) + [-jnp.inf] * fill, _f32).reshape(n, SC_LANES)
    return a, c, pad


def _sc_route_dot(scores, tab, h):
    t, width = scores.shape
    d = h.shape[1]
    half_d = d // 2
    per_worker = t // SC_WORKERS
    pair_a, pair_c, pair_pad = _stair_vectors()
    n_cand = pair_a.shape[0]
    n_vec = PEER_NKEYS // SC_LANES

    @functools.partial(
        pl.kernel, mesh=_sc_mesh(),
        out_type=(jax.ShapeDtypeStruct((t, PEER_SLOTS), jnp.int32),
                  jax.ShapeDtypeStruct((t, PEER_SLOTS), _f32),
                  jax.ShapeDtypeStruct((t, PEER_SLOTS), _f32)),
        scratch_types=[
            pltpu.VMEM((SC_TOKENS, width), _f32),
            pltpu.VMEM((SC_TOKENS, PEER_SLOTS), jnp.int32),
            pltpu.VMEM((SC_TOKENS, PEER_SLOTS), _f32),
            pltpu.VMEM((n_cand, SC_LANES), jnp.int32),
            pltpu.VMEM((n_cand, SC_LANES), jnp.int32),
            pltpu.VMEM((n_cand, SC_LANES), _f32),
            pltpu.VMEM((4, SC_LANES), _f32),
            pltpu.VMEM((SC_TOKENS, d), _f32),
            pltpu.VMEM((SC_TOKENS, PEER_SLOTS), _f32),
            pltpu.VMEM((SC_GATHER, half_d), jnp.int32),
            pltpu.VMEM((SC_GATHER, half_d), jnp.int32),
            pltpu.SemaphoreType.DMA,
            pltpu.SemaphoreType.DMA,
        ],
        compiler_params=pltpu.CompilerParams(needs_layout_passes=False),
        name="sc_route_dot",
    )
    def body(s_hbm, a_hbm, c_hbm, pad_hbm, tab_hbm, h_hbm, idx_hbm, gate_hbm, acts_hbm,
             s_v, idx_v, gate_v, a_v, c_v, pad_v, top_v, h_v, acts_v, rows0, rows1, sem0, sem1):
        base = _sc_worker_base(per_worker)
        lane = lax.iota(jnp.int32, SC_LANES)
        pltpu.sync_copy(a_hbm, a_v)
        pltpu.sync_copy(c_hbm, c_v)
        pltpu.sync_copy(pad_hbm, pad_v)

        def dots(i, part, rows):
            def block_body(blk, carry):
                row0 = blk * SC_LANES
                outv = jnp.zeros((SC_LANES,), _f32)
                for sub in range(SC_LANES // SC_BLOCK):
                    def chunk_body(jj, accs):
                        accs = list(accs)
                        for u in range(SC_UNROLL):
                            off = pl.multiple_of((jj * SC_UNROLL + u) * SC_LANES, SC_LANES)
                            h_lo = h_v[i, pl.ds(off, SC_LANES)]
                            h_hi = h_v[i, pl.ds(half_d + off, SC_LANES)]
                            for e in range(SC_BLOCK):
                                lo, hi = _sc_unpack(rows[row0 + sub * SC_BLOCK + e, pl.ds(off, SC_LANES)])
                                accs[e] = accs[e] + (lo * h_lo + hi * h_hi)
                        return tuple(accs)

                    accs = lax.fori_loop(0, half_d // SC_LANES // SC_UNROLL, chunk_body,
                                         tuple(jnp.zeros((SC_LANES,), _f32) for _ in range(SC_BLOCK)))
                    for e in range(SC_BLOCK):
                        outv = jnp.where(lane == sub * SC_BLOCK + e, jnp.sum(accs[e]), outv)
                col = pl.multiple_of(part * SC_GATHER + row0, SC_LANES)
                acts_v[i, pl.ds(col, SC_LANES)] = outv
                return carry

            lax.fori_loop(0, SC_GATHER // SC_LANES, block_body, 0)

        def sort_desc(k, v):
            return plsc.sort_key_val(k, v, descending=True)

        def merge(x, y):
            yk, yv = lax.rev(y[0], (0,)), lax.rev(y[1], (0,))
            take = x[0] >= yk
            return sort_desc(jnp.where(take, x[0], yk), jnp.where(take, x[1], yv))

        def top_of(vectors):
            while len(vectors) > 1:
                vectors = [merge(vectors[j], vectors[j + 1]) for j in range(0, len(vectors), 2)]
            return vectors[0]

        def head_body(i, hd):
            halves = []
            for half in range(2):
                col = (2 * hd + half) * PEER_NKEYS
                vecs = [sort_desc(s_v[i, pl.ds(pl.multiple_of(col + j * SC_LANES, SC_LANES), SC_LANES)],
                                  lane + j * SC_LANES) for j in range(n_vec)]
                halves.append(top_of(vecs))
            (v1, i1), (v2, i2) = halves
            top_v[0, :] = v1
            top_v[1, :] = i1.astype(_f32)
            top_v[2, :] = v2
            top_v[3, :] = i2.astype(_f32)

            def pick(row, pos):
                return plsc.load_gather(top_v, [jnp.full((SC_LANES,), row, jnp.int32), pos])

            cands = []
            for q in range(n_cand):
                a, c = a_v[q, :], c_v[q, :]
                cs = pick(0, a) + pick(2, c) + pad_v[q, :]
                ci = pick(1, a) * float(PEER_NKEYS) + pick(3, c)
                cands.append(sort_desc(cs, ci))
            top_s, top_i = top_of(cands)
            e = jnp.exp(top_s - jnp.max(top_s))
            slots = pl.ds(pl.multiple_of(hd * PEER_TOPK, PEER_TOPK), PEER_TOPK)
            idx_v[i, slots] = top_i.astype(jnp.int32)
            gate_v[i, slots] = e / jnp.sum(e)

        @pl.loop(0, per_worker // SC_TOKENS)
        def _(step):
            tok = pl.multiple_of(base + step * SC_TOKENS, SC_TOKENS)
            pltpu.sync_copy(s_hbm.at[pl.ds(tok, SC_TOKENS)], s_v)
            pltpu.sync_copy(h_hbm.at[pl.ds(tok, SC_TOKENS)], h_v)

            @pl.loop(0, SC_TOKENS * PEER_HEADS)
            def _(n):
                head_body(n // PEER_HEADS, n % PEER_HEADS)

            pltpu.sync_copy(idx_v, idx_hbm.at[pl.ds(tok, SC_TOKENS)])
            pltpu.sync_copy(gate_v, gate_hbm.at[pl.ds(tok, SC_TOKENS)])
            _sc_gather_loop(tab_hbm, idx_v, ((rows0, sem0), (rows1, sem1)), dots)
            pltpu.sync_copy(acts_v, acts_hbm.at[pl.ds(tok, SC_TOKENS)])

    return body(scores, pair_a, pair_c, pair_pad, tab, h)


def _sc_vaxpy(idx, w, tab, *, first, count):
    half = tab.shape[1]
    d = 2 * half
    per_worker = count // SC_WORKERS
    span = SC_COLS * SC_LANES

    @functools.partial(
        pl.kernel, mesh=_sc_mesh(),
        out_type=jax.ShapeDtypeStruct((count, d), _f32),
        scratch_types=[
            pltpu.VMEM((SC_TOKENS, PEER_SLOTS), jnp.int32),
            pltpu.VMEM((SC_TOKENS, PEER_SLOTS), _f32),
            pltpu.VMEM((SC_TOKENS, d), _f32),
            pltpu.VMEM((SC_GATHER, half), jnp.int32),
            pltpu.VMEM((SC_GATHER, half), jnp.int32),
            pltpu.SemaphoreType.DMA,
            pltpu.SemaphoreType.DMA,
        ],
        compiler_params=pltpu.CompilerParams(needs_layout_passes=False),
        name="sc_vaxpy",
    )
    def body(idx_hbm, w_hbm, tab_hbm, out_hbm, idx_v, w_v, out_v, rows0, rows1, sem0, sem1):
        base = _sc_worker_base(per_worker)

        def accumulate(i, part, rows):
            i_vec = jnp.full((SC_LANES,), i, jnp.int32)

            def span_body(cq, carry):
                def cols(c, offset=0):
                    return pl.ds(pl.multiple_of(offset + cq * span + c * SC_LANES, SC_LANES), SC_LANES)

                def expert_body(e, accs):
                    k_vec = jnp.full((SC_LANES,), part * SC_GATHER + e, jnp.int32)
                    wv = plsc.load_gather(w_v, [i_vec, k_vec])
                    new = []
                    for c in range(SC_COLS):
                        lo, hi = _sc_unpack(rows[e, cols(c)])
                        new += [accs[2 * c] + lo * wv, accs[2 * c + 1] + hi * wv]
                    return tuple(new)

                init = []
                for c in range(SC_COLS):
                    init += [out_v[i, cols(c)], out_v[i, cols(c, half)]]
                accs = lax.fori_loop(0, SC_GATHER, expert_body, tuple(init))
                for c in range(SC_COLS):
                    out_v[i, cols(c)] = accs[2 * c]
                    out_v[i, cols(c, half)] = accs[2 * c + 1]
                return carry

            lax.fori_loop(0, half // span, span_body, 0)

        @pl.loop(0, per_worker // SC_TOKENS)
        def _(step):
            off = pl.multiple_of(base + step * SC_TOKENS, SC_TOKENS)
            pltpu.sync_copy(idx_hbm.at[pl.ds(first + off, SC_TOKENS)], idx_v)
            pltpu.sync_copy(w_hbm.at[pl.ds(first + off, SC_TOKENS)], w_v)

            @pl.loop(0, SC_TOKENS)
            def _(i):
                @pl.loop(0, d // SC_LANES)
                def _(j):
                    out_v[i, pl.ds(pl.multiple_of(j * SC_LANES, SC_LANES), SC_LANES)] = (
                        jnp.zeros((SC_LANES,), _f32))

            _sc_gather_loop(tab_hbm, idx_v, ((rows0, sem0), (rows1, sem1)), accumulate)
            pltpu.sync_copy(out_v, out_hbm.at[pl.ds(off, SC_TOKENS)])

    return body(idx, w, tab)


ROW_CHUNKS = 8
PACK_ROWS = ROW_CHUNKS // 2
ROW_BUFFERS = 4


def _pack_table(tab):
    n, d = tab.shape
    bits = lax.bitcast_convert_type(tab.astype(_bf16), jnp.uint16).astype(jnp.uint32)
    bits = bits.reshape(n, PACK_ROWS, 2, d // ROW_CHUNKS)
    word = bits[:, :, 0, :] | (bits[:, :, 1, :] << 16)
    return lax.bitcast_convert_type(word, jnp.int32)


def _gate_weights_kernel(acts_ref, gate_ref, w_ref):
    a = acts_ref[...]
    gelu = 0.5 * a * (1.0 + lax.erf(a * (2.0 ** -0.5)))
    w_ref[...] = gate_ref[...] * gelu


def _gate_weights(acts, gate, *, tw):
    t, n = acts.shape
    spec = pl.BlockSpec((tw, n), lambda i: (i, 0))
    return pl.pallas_call(
        _gate_weights_kernel,
        grid=(t // tw,),
        in_specs=[spec, spec],
        out_specs=spec,
        out_shape=jax.ShapeDtypeStruct((t, n), _f32),
        compiler_params=pltpu.CompilerParams(dimension_semantics=("parallel",)),
        name="gate_weights",
    )(acts, gate)


def _gather_rows(idx_ref, t, tab_ref, rows_ref):
    for k in range(PEER_SLOTS):
        row = pl.multiple_of(idx_ref[t, k], PACK_ROWS)
        rows_ref[k * PACK_ROWS:(k + 1) * PACK_ROWS, :] = tab_ref[pl.ds(row, PACK_ROWS), :]


def _rows_matrix(rows_ref):
    return pltpu.bitcast(rows_ref[...], _bf16)


def _token_loop(tb, idx_ref, tab_ref, row_bufs, compute):
    n = len(row_bufs)
    ahead = 2
    for j in range(ahead):
        _gather_rows(idx_ref, j, tab_ref, row_bufs[j])

    def body(i, carry):
        t0 = n * i
        for j in range(n):
            nxt = jnp.minimum(t0 + j + ahead, tb - 1)
            _gather_rows(idx_ref, nxt, tab_ref, row_bufs[(j + ahead) % n])
            compute(t0 + j, _rows_matrix(row_bufs[j]))
        return carry

    lax.fori_loop(0, tb // n, body, 0)


def _chunk_diag_mask():
    shape = (ROW_CHUNKS, PEER_SLOTS * ROW_CHUNKS)
    return (_iota(shape, 1) % ROW_CHUNKS) == _iota(shape, 0)


def _vaxpy_kernel(idx_ref, w_ref, tab_ref, out_ref, wrep_ref, *row_bufs):
    tb = out_ref.shape[0]
    diag = _chunk_diag_mask()
    shape = (PEER_SLOTS, PEER_SLOTS * ROW_CHUNKS)
    spread = (_iota(shape, 0) == (_iota(shape, 1) // ROW_CHUNKS)).astype(_bf16)
    wrep_ref[...] = _dot_exact_rhs01(w_ref[...], spread)

    def compute(t, m):
        w_row = jnp.broadcast_to(wrep_ref[pl.ds(t, 1), :], diag.shape)
        w_hi, w_lo = _split2(jnp.where(diag, w_row, 0.0))
        out_ref[t] = _dot(w_hi, m) + _dot(w_lo, m)

    _token_loop(tb, idx_ref, tab_ref, row_bufs, compute)


def _vaxpy(idx, w, tab, *, tb, count):
    return pl.pallas_call(
        _vaxpy_kernel,
        grid=(count // tb,),
        in_specs=[
            pl.BlockSpec((tb, PEER_SLOTS), lambda i: (i, 0), memory_space=pltpu.SMEM),
            pl.BlockSpec((tb, PEER_SLOTS), lambda i: (i, 0)),
            _const_spec(tab.shape),
        ],
        out_specs=pl.BlockSpec((tb, ROW_CHUNKS, 128), lambda i: (i, 0, 0)),
        out_shape=jax.ShapeDtypeStruct((count, ROW_CHUNKS, 128), _f32),
        scratch_shapes=[
            pltpu.VMEM((tb, PEER_SLOTS * ROW_CHUNKS), _f32),
        ] + [pltpu.VMEM((PEER_SLOTS * PACK_ROWS, 128), jnp.int32)
             for _ in range(ROW_BUFFERS)],
        compiler_params=pltpu.CompilerParams(
            dimension_semantics=("parallel",),
            vmem_limit_bytes=VMEM_LIMIT_BYTES),
        name="vaxpy",
    )(idx, w, tab)


def _final_kernel(x_ref, p_ref, g_ref, acc_ref, out_ref, *, normalize):
    del acc_ref
    x = x_ref[...] + p_ref[...]
    if normalize:
        x = x * lax.rsqrt(jnp.mean(x * x, axis=-1, keepdims=True) + EPS) * g_ref[...]
    out_ref[...] = x


def _final(acc, x, peer, g, *, normalize, tf, first, row):
    t, d = peer.shape
    x_off = first // tf
    out_off = row // tf
    return pl.pallas_call(
        functools.partial(_final_kernel, normalize=normalize),
        grid=(t // tf,),
        in_specs=[
            pl.BlockSpec((tf, d), lambda i: (i + x_off, 0)),
            pl.BlockSpec((tf, d), lambda i: (i, 0)),
            _const_spec((1, d)),
            pl.BlockSpec(memory_space=pl.ANY),
        ],
        out_specs=pl.BlockSpec((tf, d), lambda i: (i + out_off, 0)),
        out_shape=jax.ShapeDtypeStruct(acc.shape, _f32),
        input_output_aliases={3: 0},
        compiler_params=pltpu.CompilerParams(dimension_semantics=("parallel",)),
        name="final_norm",
    )(x, peer, g, acc)


def kernel(x, norm_mix_g, w_in, hg_lb_logits, hg_out_norm_g, conv_w, w_branch_hg, w_branch_conv, w_out, norm_ffn_g, peer_w_query, peer_keys1, peer_keys2, peer_u, peer_v, norm_final_g):
    b_, s_, d = x.shape
    depth = w_in.shape[0]
    lb_all = jnp.cumsum(jax.nn.softmax(hg_lb_logits.astype(_f32), axis=0), axis=0)
    n_groups = BATCH_GROUPS if b_ % BATCH_GROUPS == 0 else 1
    bg = b_ // n_groups
    tg = bg * s_
    for l in range(depth):
        wq = peer_w_query[l].astype(_bf16).reshape(d, PEER_HEADS, 2, PEER_HALF).transpose(1, 2, 0, 3)
        u_sc = _pack_halves(peer_u[l])
        last = l == depth - 1
        g = norm_final_g[None] if last else jnp.ones((1, d), _f32)

        def front(x_in):
            xc = _mix(x_in, norm_mix_g[l][None], w_in[l].astype(_bf16), lb_all[l][None],
                      hg_out_norm_g[l][None], conv_w[l], w_branch_hg[l].astype(_bf16),
                      w_branch_conv[l].astype(_bf16), w_out[l].astype(_bf16), ts=MIX_TILE)
            xf = xc.reshape(tg, d)
            h, scores = _scores(xf, norm_ffn_g[l][None], wq,
                                peer_keys1[l].astype(_bf16), peer_keys2[l].astype(_bf16), tr=ROUTE_TILE)
            return (xf, *_sc_route_dot(scores, u_sc, h)), scores

        def back(xf, idx, gate, acts, n_sc):
            n_tc = tg - n_sc
            w = _gate_weights(acts, gate, tw=FINAL_TILE)
            peer_tc = _vaxpy(idx * PACK_ROWS, w, v_tab.reshape(-1, 128), tb=EXPERT_TILE, count=n_tc).reshape(n_tc, d)
            peer_sc = _sc_vaxpy(idx, w, v_sc, first=n_tc, count=n_sc) if n_sc else None
            return peer_tc, peer_sc

        fronts = [front(x[c * bg:(c + 1) * bg]) for c in range(n_groups)]
        v_rows = lax.optimization_barrier((peer_v[l], fronts[0][1]))[0]
        v_tab = _pack_table(v_rows)
        v_sc = _pack_halves(v_rows)
        acc = jnp.zeros((b_ * s_, d), _f32)
        sc_parts = []
        for c, ((xf, idx, gate, acts), _) in enumerate(fronts):
            share = SC_SHARE_LAST if c == n_groups - 1 else SC_SHARE
            n_sc = share if share < tg else 0
            acts = lax.optimization_barrier((acts, acc if c else fronts[-1][1]))[0]
            peer_tc, peer_sc = back(xf, idx, gate, acts, n_sc)
            acc = _final(acc, xf, peer_tc, g, normalize=last, tf=FINAL_TILE, first=0, row=c * tg)
            sc_parts.append((c * tg + tg - n_sc, tg - n_sc, xf, peer_sc))
        for row, first, xf, p_sc in sc_parts:
            if p_sc is not None:
                acc = _final(acc, xf, p_sc, g, normalize=last, tf=FINAL_TILE, first=first, row=row)
        x = acc.reshape(b_, s_, d)
    return x
```

```python
import functools

import jax
import jax.numpy as jnp
from jax import lax
from jax.experimental import pallas as pl
from jax.experimental.pallas import tpu as pltpu
from jax.experimental.pallas import tpu_sc as plsc

EPS = 1e-6
CHUNK = 64
SUB = 8
HEADS = 8
HEAD_DIM = 64
HG_WIDTH = HEADS * HEAD_DIM
GROUP = 256
N_GROUPS = HG_WIDTH // GROUP
CONV_K = 3
PEER_HEADS = 8
PEER_NKEYS = 128
PEER_HALF = 128
PEER_TOPK = 16
PEER_SLOTS = PEER_HEADS * PEER_TOPK

VMEM_LIMIT_BYTES = 56 * 1024 * 1024

MIX_TILE = 256
ROUTE_TILE = 256
EXPERT_TILE = 128
FINAL_TILE = 256
BATCH_GROUPS = 8
SC_SHARE = 256
SC_SHARE_LAST = 2048

_f32 = jnp.float32
_bf16 = jnp.bfloat16


def _dot(a, b):
    return jnp.dot(a, b, preferred_element_type=_f32)


def _dot_nt(a, b):
    return lax.dot_general(a, b, (((1,), (1,)), ((), ())), preferred_element_type=_f32)


def _dot_tn(a, b):
    return lax.dot_general(a, b, (((0,), (0,)), ((), ())), preferred_element_type=_f32)


def _split3(x):
    hi = x.astype(_bf16)
    r1 = x - hi.astype(_f32)
    mid = r1.astype(_bf16)
    lo = (r1 - mid.astype(_f32)).astype(_bf16)
    return hi, mid, lo


def _split2(x):
    hi = x.astype(_bf16)
    lo = (x - hi.astype(_f32)).astype(_bf16)
    return hi, lo


def _dot_exact_rhs01(x, m01):
    hi, mid, lo = _split3(x)
    return _dot(hi, m01) + _dot(mid, m01) + _dot(lo, m01)


def _dot_exact_lhs01(m01, x):
    hi, mid, lo = _split3(x)
    return _dot(m01, hi) + _dot(m01, mid) + _dot(m01, lo)


def _iota(shape, dim):
    return lax.broadcasted_iota(jnp.int32, shape, dim)


def _hgrn2_chunk(q, k, lf, v, state_ref):
    n_sub = CHUNK // SUB
    row = _iota((CHUNK, CHUNK), 0)
    col = _iota((CHUNK, CHUNK), 1)
    tril = (col <= row).astype(_bf16)
    b = _dot_exact_lhs01(tril, lf)

    b_end = [b[(j + 1) * SUB - 1:(j + 1) * SUB, :] for j in range(n_sub)]
    b_end_rows = jnp.concatenate([jnp.broadcast_to(e, (SUB, HG_WIDTH)) for e in b_end], axis=0)
    b_last = b_end[-1]

    q_in = (q * jnp.exp(b)).astype(_bf16)
    k_sub = (k * jnp.exp(b_end_rows - b)).astype(_bf16)
    k_out = (k * jnp.exp(b_last - b)).astype(_bf16)
    q_from = [(q * jnp.exp(jnp.minimum(b - b_end[j], 0.0))).astype(_bf16) for j in range(n_sub - 1)]
    v_b = v.astype(_bf16)

    gr = _iota((GROUP, GROUP), 0) // HEAD_DIM
    gc = _iota((GROUP, GROUP), 1) // HEAD_DIM
    head_mask = gr == gc
    t_blk = _iota((CHUNK, GROUP), 0) // SUB
    s_blk = (_iota((CHUNK, GROUP), 1) % HEAD_DIM) // SUB

    outs = []
    for g in range(N_GROUPS):
        sl = slice(g * GROUP, (g + 1) * GROUP)
        st = state_ref[g]
        o_g = _dot_nt(q_in[:, sl], st.astype(_bf16))

        zero_b = jnp.zeros((), _bf16)
        k_bd = jnp.where(head_mask, jnp.concatenate([k_sub[:, sl]] * (GROUP // CHUNK), axis=0), zero_b)
        v_bd = jnp.where(head_mask, jnp.concatenate([v_b[:, sl]] * (GROUP // CHUNK), axis=0), zero_b)
        q_stack = jnp.concatenate([qf[:, sl] for qf in q_from], axis=0)
        r = _dot_nt(q_stack, k_bd)
        scores = jnp.zeros((CHUNK, GROUP), _f32)
        for j in range(n_sub - 1):
            sel = (s_blk == j) & (t_blk > j)
            scores = jnp.where(sel, r[j * CHUNK:(j + 1) * CHUNK, :], scores)
        o_g = o_g + _dot(scores.astype(_bf16), v_bd)
        outs.append(o_g)

        upd = _dot_tn(v_b[:, sl], k_out[:, sl])
        decay = jnp.exp(b_last[:, sl])
        state_ref[g] = st * decay + jnp.where(head_mask, upd, 0.0)
    o = jnp.concatenate(outs, axis=1)

    ones_bd = ((_iota((HG_WIDTH, HG_WIDTH), 0) // HEAD_DIM)
               == (_iota((HG_WIDTH, HG_WIDTH), 1) // HEAD_DIM)).astype(_bf16)
    t_in_sub = _iota((CHUNK, HG_WIDTH), 0) % SUB
    for lag in range(SUB):
        if lag == 0:
            p = q * k
            v_l = v
        else:
            valid = t_in_sub >= lag
            k_l = pltpu.roll(k, lag, 0)
            b_l = pltpu.roll(b, lag, 0)
            v_l = pltpu.roll(v, lag, 0)
            p = jnp.where(valid, q * k_l * jnp.exp(jnp.minimum(b - b_l, 0.0)), 0.0)
        s_l = _dot(p.astype(_bf16), ones_bd)
        o = o + s_l * v_l
    return o


def _mix_kernel(x_ref, g_ref, win_ref, lb_ref, hgn_ref, convw_ref, pa_ref, pb_ref, wo_ref,
                out_ref, state_ref, carry_ref, q_s, k_s, lf_s, v_s, o_s):
    ts = x_ref.shape[0]
    d_model = x_ref.shape[1]
    w = HG_WIDTH

    @pl.when(pl.program_id(1) == 0)
    def _():
        state_ref[...] = jnp.zeros_like(state_ref)
        carry_ref[...] = jnp.zeros_like(carry_ref)

    x = x_ref[...]
    h = x * lax.rsqrt(jnp.mean(x * x, axis=-1, keepdims=True) + EPS) * g_ref[...]
    hb = h.astype(_bf16)

    def proj(i, width=w):
        return _dot(hb, win_ref[:, i * w:i * w + width])

    lb = lb_ref[...]
    q_s[...] = jax.nn.silu(proj(0)) * (HEAD_DIM ** -0.5)
    forget = lb + (1.0 - lb) * jax.nn.sigmoid(proj(1))
    k_s[...] = 1.0 - forget
    lf_s[...] = jnp.log(forget)
    v_s[...] = proj(2)

    def chunk_body(c, carry):
        rows = pl.ds(pl.multiple_of(c * CHUNK, CHUNK), CHUNK)
        o_s[rows, :] = _hgrn2_chunk(q_s[rows, :], k_s[rows, :], lf_s[rows, :], v_s[rows, :], state_ref)
        return carry

    lax.fori_loop(0, ts // CHUNK, chunk_body, 0)

    o = o_s[...]
    ones_bd = ((_iota((w, w), 0) // HEAD_DIM) == (_iota((w, w), 1) // HEAD_DIM)).astype(_bf16)
    ms = _dot_exact_rhs01(o * o, ones_bd) * (1.0 / HEAD_DIM)
    o = o * lax.rsqrt(ms + EPS) * hgn_ref[...]
    y_a = (o * jax.nn.silu(proj(3))).astype(_bf16)

    u = proj(5) * proj(6)
    prev = carry_ref[...]
    rowi = _iota((ts, w), 0)
    u1 = jnp.where(rowi >= 1, pltpu.roll(u, 1, 0), jnp.broadcast_to(prev[7:8, :], (ts, w)))
    u2 = jnp.where(rowi >= 2, pltpu.roll(u, 2, 0),
                   jnp.where(rowi == 1, jnp.broadcast_to(prev[7:8, :], (ts, w)),
                             jnp.broadcast_to(prev[6:7, :], (ts, w))))
    carry_ref[...] = u[ts - 8:, :]
    cw = convw_ref[...]
    y_b = (proj(4) * (cw[0:1, :] * u2 + cw[1:2, :] * u1 + cw[2:3, :] * u)).astype(_bf16)

    g_a = jax.nn.sigmoid(proj(7, d_model))
    g_b = jax.nn.sigmoid(_dot(hb, win_ref[:, 7 * w + d_model:7 * w + 2 * d_model]))
    merged = g_a * _dot(y_a, pa_ref[...]) + g_b * _dot(y_b, pb_ref[...])
    out_ref[...] = x + _dot(merged.astype(_bf16), wo_ref[...])


def _const_spec(shape):
    nd = len(shape)
    return pl.BlockSpec(shape, lambda *_: (0,) * nd, pipeline_mode=pl.Buffered(1))


def _mix(x, norm_g, w_in, lb, hg_norm_g, conv_w, w_a, w_b, w_o, *, ts):
    b_, s_, d = x.shape
    in_cols = w_in.shape[1]
    w = HG_WIDTH
    grid = (b_, s_ // ts)
    return pl.pallas_call(
        _mix_kernel,
        grid=grid,
        in_specs=[
            pl.BlockSpec((None, ts, d), lambda b, s: (b, s, 0)),
            _const_spec((1, d)),
            _const_spec((d, in_cols)),
            _const_spec((1, w)),
            _const_spec((1, w)),
            _const_spec((CONV_K, w)),
            _const_spec((w, d)),
            _const_spec((w, d)),
            _const_spec((d, d)),
        ],
        out_specs=pl.BlockSpec((None, ts, d), lambda b, s: (b, s, 0)),
        out_shape=jax.ShapeDtypeStruct((b_, s_, d), _f32),
        scratch_shapes=[
            pltpu.VMEM((N_GROUPS, GROUP, GROUP), _f32),
            pltpu.VMEM((8, w), _f32),
            pltpu.VMEM((ts, w), _f32),
            pltpu.VMEM((ts, w), _f32),
            pltpu.VMEM((ts, w), _f32),
            pltpu.VMEM((ts, w), _f32),
            pltpu.VMEM((ts, w), _f32),
        ],
        compiler_params=pltpu.CompilerParams(
            dimension_semantics=("parallel", "arbitrary"),
            vmem_limit_bytes=VMEM_LIMIT_BYTES),
        name="mix",
    )(x, norm_g, w_in, lb, hg_norm_g, conv_w, w_a, w_b, w_o)


def _stair_pairs():
    pairs = [(a, c) for a in range(PEER_TOPK) for c in range(PEER_TOPK) if (a + 1) * (c + 1) <= PEER_TOPK]
    rows = -(-len(pairs) // 8) * 8
    ranks = jnp.arange(PEER_TOPK)[None, :]
    a_col = jnp.asarray([a for a, _ in pairs] + [-1] * (rows - len(pairs)))[:, None]
    c_col = jnp.asarray([c for _, c in pairs] + [-1] * (rows - len(pairs)))[:, None]
    pad = jnp.where(a_col < 0, -jnp.inf, 0.0).astype(_f32)
    return (a_col == ranks).astype(_bf16), (c_col == ranks).astype(_bf16), pad


ROUTE_LANES = 128


def _route_kernel(x_ref, g_ref, wq_ref, k1_ref, k2_ref, sa_ref, sc_ref, pad_ref, h_ref, idx_ref, gate_ref,
                  idx_t, e_t, top_ref):
    tr = x_ref.shape[0]
    n_cand = sa_ref.shape[0]
    x = x_ref[...]
    h = x * lax.rsqrt(jnp.mean(x * x, axis=-1, keepdims=True) + EPS) * g_ref[...]
    h_ref[...] = h
    hb = h.astype(_bf16)

    key_row = _iota((PEER_NKEYS, ROUTE_LANES), 0).astype(_f32)
    cand_row = _iota((n_cand, tr), 0).astype(_f32)
    neg_inf = jnp.float32(-jnp.inf)

    def extract_max(s):
        m = jnp.max(s, axis=0, keepdims=True)
        i = jnp.min(jnp.where(s == m, key_row, float(PEER_NKEYS)), axis=0, keepdims=True)
        return m, i, jnp.where(key_row == i, neg_inf, s)

    def head_body(hd, carry):
        q1 = _dot(hb, wq_ref[hd, 0]).astype(_bf16)
        q2 = _dot(hb, wq_ref[hd, 1]).astype(_bf16)
        s1 = _dot_nt(k1_ref[hd], q1)
        s2 = _dot_nt(k2_ref[hd], q2)

        for lt in range(tr // ROUTE_LANES):
            lanes = slice(lt * ROUTE_LANES, (lt + 1) * ROUTE_LANES)

            def half_body(k, c):
                m1, i1, r1 = extract_max(c[0])
                m2, i2, r2 = extract_max(c[1])
                for j, row in enumerate((m1, i1, m2, i2)):
                    top_ref[lt, j, pl.ds(k, 1), :] = row
                return r1, r2

            lax.fori_loop(0, PEER_TOPK, half_body, (s1[:, lanes], s2[:, lanes]))

        def top(j):
            return jnp.concatenate([top_ref[lt, j] for lt in range(tr // ROUTE_LANES)], axis=1)

        sa = sa_ref[...]
        sc = sc_ref[...]
        cand_s = _dot_exact_lhs01(sa, top(0)) + _dot_exact_lhs01(sc, top(2)) + pad_ref[...]
        cand_i = (_dot(sa, top(1).astype(_bf16)) * float(PEER_NKEYS)
                  + _dot(sc, top(3).astype(_bf16)))

        def pick_body(k, c):
            cand_s, denom, m_first = c
            m = jnp.max(cand_s, axis=0, keepdims=True)
            pos = jnp.min(jnp.where(cand_s == m, cand_row, float(n_cand)), axis=0, keepdims=True)
            hit = cand_row == pos
            eid = jnp.max(jnp.where(hit, cand_i, -1.0), axis=0, keepdims=True)
            m_first = jnp.where(k == 0, m, m_first)
            e = jnp.exp(m - m_first)
            slot = hd * PEER_TOPK + k
            idx_t[pl.ds(slot, 1), :] = eid
            e_t[pl.ds(slot, 1), :] = e
            return jnp.where(hit, neg_inf, cand_s), denom + e, m_first

        zero_row = jnp.zeros((1, tr), _f32)
        _, denom, _ = lax.fori_loop(0, PEER_TOPK, pick_body, (cand_s, zero_row, zero_row))
        rows = pl.ds(pl.multiple_of(hd * PEER_TOPK, PEER_TOPK), PEER_TOPK)
        e_t[rows, :] = e_t[rows, :] / denom
        return carry

    lax.fori_loop(0, PEER_HEADS, head_body, 0)
    idx_ref[...] = idx_t[...].T.astype(jnp.int32)
    gate_ref[...] = e_t[...].T


def _route(x, norm_g, w_query, keys1, keys2, *, tr):
    t, d = x.shape
    sel_a, sel_c, pad = _stair_pairs()
    pad = jnp.broadcast_to(pad, (pad.shape[0], tr))
    return pl.pallas_call(
        _route_kernel,
        grid=(t // tr,),
        in_specs=[
            pl.BlockSpec((tr, d), lambda i: (i, 0)),
            _const_spec((1, d)),
            _const_spec(w_query.shape),
            _const_spec(keys1.shape),
            _const_spec(keys2.shape),
            _const_spec(sel_a.shape),
            _const_spec(sel_c.shape),
            _const_spec(pad.shape),
        ],
        out_specs=[
            pl.BlockSpec((tr, d), lambda i: (i, 0)),
            pl.BlockSpec((tr, PEER_SLOTS), lambda i: (i, 0)),
            pl.BlockSpec((tr, PEER_SLOTS), lambda i: (i, 0)),
        ],
        out_shape=[
            jax.ShapeDtypeStruct((t, d), _f32),
            jax.ShapeDtypeStruct((t, PEER_SLOTS), jnp.int32),
            jax.ShapeDtypeStruct((t, PEER_SLOTS), _f32),
        ],
        scratch_shapes=[
            pltpu.VMEM((PEER_SLOTS, tr), _f32),
            pltpu.VMEM((PEER_SLOTS, tr), _f32),
            pltpu.VMEM((tr // ROUTE_LANES, 4, PEER_TOPK, ROUTE_LANES), _f32),
        ],
        compiler_params=pltpu.CompilerParams(
            dimension_semantics=("parallel",),
            vmem_limit_bytes=VMEM_LIMIT_BYTES),
        name="route",
    )(x, norm_g, w_query, keys1, keys2, sel_a, sel_c, pad)


def _scores_kernel(x_ref, g_ref, wq_ref, k1_ref, k2_ref, h_ref, s_ref):
    x = x_ref[...]
    h = x * lax.rsqrt(jnp.mean(x * x, axis=-1, keepdims=True) + EPS) * g_ref[...]
    h_ref[...] = h
    hb = h.astype(_bf16)
    for hd in range(PEER_HEADS):
        for half, keys_ref in enumerate((k1_ref, k2_ref)):
            q = _dot(hb, wq_ref[hd, half]).astype(_bf16)
            col = (2 * hd + half) * PEER_NKEYS
            s_ref[:, col:col + PEER_NKEYS] = _dot_nt(q, keys_ref[hd])


def _scores(x, norm_g, w_query, keys1, keys2, *, tr):
    t, d = x.shape
    width = 2 * PEER_HEADS * PEER_NKEYS
    return pl.pallas_call(
        _scores_kernel,
        grid=(t // tr,),
        in_specs=[
            pl.BlockSpec((tr, d), lambda i: (i, 0)),
            _const_spec((1, d)),
            _const_spec(w_query.shape),
            _const_spec(keys1.shape),
            _const_spec(keys2.shape),
        ],
        out_specs=[
            pl.BlockSpec((tr, d), lambda i: (i, 0)),
            pl.BlockSpec((tr, width), lambda i: (i, 0)),
        ],
        out_shape=[
            jax.ShapeDtypeStruct((t, d), _f32),
            jax.ShapeDtypeStruct((t, width), _f32),
        ],
        compiler_params=pltpu.CompilerParams(
            dimension_semantics=("parallel",),
            vmem_limit_bytes=VMEM_LIMIT_BYTES),
        name="scores",
    )(x, norm_g, w_query, keys1, keys2)


SC_CORES = 2
SC_SUBCORES = 16
SC_LANES = 16
SC_WORKERS = SC_CORES * SC_SUBCORES
SC_TOKENS = 8
SC_GATHER = 32
SC_BLOCK = 8
SC_UNROLL = 2
SC_COLS = 8
HI_MASK = -65536


def _sc_mesh():
    return plsc.VectorSubcoreMesh(core_axis_name="c", subcore_axis_name="s")


def _sc_worker_base(per_worker):
    return (lax.axis_index("s") * SC_CORES + lax.axis_index("c")) * per_worker


def _sc_gather_loop(tab_hbm, idx_v, bufs, consume):
    n_parts = PEER_SLOTS // SC_GATHER
    n_gathers = SC_TOKENS * n_parts

    def gather(g, parity):
        rows, sem = bufs[parity]
        i = g // n_parts
        col = pl.multiple_of((g % n_parts) * SC_GATHER, SC_GATHER)
        return pltpu.make_async_copy(tab_hbm.at[idx_v.at[i, pl.ds(col, SC_GATHER)]], rows, sem)

    gather(0, 0).start()

    @pl.loop(0, n_gathers // 2)
    def _(pair):
        g = 2 * pair
        gather(g + 1, 1).start()
        gather(g, 0).wait()
        consume(g // n_parts, g % n_parts, bufs[0][0])

        @pl.when(g + 2 < n_gathers)
        def _():
            gather(g + 2, 0).start()

        gather(g + 1, 1).wait()
        consume((g + 1) // n_parts, (g + 1) % n_parts, bufs[1][0])


def _pack_halves(tab):
    half = tab.shape[1] // 2
    bits = lax.bitcast_convert_type(tab.astype(_bf16), jnp.uint16).astype(jnp.uint32)
    return lax.bitcast_convert_type(bits[:, :half] | (bits[:, half:] << 16), jnp.int32)


def _sc_unpack(words):
    lo = lax.bitcast_convert_type(words << 16, _f32)
    hi = lax.bitcast_convert_type(words & HI_MASK, _f32)
    return lo, hi


def _stair_vectors():
    pairs = [(a, c) for a in range(PEER_TOPK) for c in range(PEER_TOPK) if (a + 1) * (c + 1) <= PEER_TOPK]
    n = -(-len(pairs) // SC_LANES)
    fill = n * SC_LANES - len(pairs)
    a = jnp.asarray([p[0] for p in pairs] + [0] * fill, jnp.int32).reshape(n, SC_LANES)
    c = jnp.asarray([p[1] for p in pairs] + [0] * fill, jnp.int32).reshape(n, SC_LANES)
    pad = jnp.asarray([0.0] * len(pairs) + [-jnp.inf] * fill, _f32).reshape(n, SC_LANES)
    return a, c, pad


def _sc_route_dot(scores, tab, h):
    t, width = scores.shape
    d = h.shape[1]
    half_d = d // 2
    per_worker = t // SC_WORKERS
    pair_a, pair_c, pair_pad = _stair_vectors()
    n_cand = pair_a.shape[0]
    n_vec = PEER_NKEYS // SC_LANES

    @functools.partial(
        pl.kernel, mesh=_sc_mesh(),
        out_type=(jax.ShapeDtypeStruct((t, PEER_SLOTS), jnp.int32),
                  jax.ShapeDtypeStruct((t, PEER_SLOTS), _f32),
                  jax.ShapeDtypeStruct((t, PEER_SLOTS), _f32)),
        scratch_types=[
            pltpu.VMEM((SC_TOKENS, width), _f32),
            pltpu.VMEM((SC_TOKENS, PEER_SLOTS), jnp.int32),
            pltpu.VMEM((SC_TOKENS, PEER_SLOTS), _f32),
            pltpu.VMEM((n_cand, SC_LANES), jnp.int32),
            pltpu.VMEM((n_cand, SC_LANES), jnp.int32),
            pltpu.VMEM((n_cand, SC_LANES), _f32),
            pltpu.VMEM((4, SC_LANES), _f32),
            pltpu.VMEM((SC_TOKENS, d), _f32),
            pltpu.VMEM((SC_TOKENS, PEER_SLOTS), _f32),
            pltpu.VMEM((SC_GATHER, half_d), jnp.int32),
            pltpu.VMEM((SC_GATHER, half_d), jnp.int32),
            pltpu.SemaphoreType.DMA,
            pltpu.SemaphoreType.DMA,
        ],
        compiler_params=pltpu.CompilerParams(needs_layout_passes=False),
        name="sc_route_dot",
    )
    def body(s_hbm, a_hbm, c_hbm, pad_hbm, tab_hbm, h_hbm, idx_hbm, gate_hbm, acts_hbm,
             s_v, idx_v, gate_v, a_v, c_v, pad_v, top_v, h_v, acts_v, rows0, rows1, sem0, sem1):
        base = _sc_worker_base(per_worker)
        lane = lax.iota(jnp.int32, SC_LANES)
        pltpu.sync_copy(a_hbm, a_v)
        pltpu.sync_copy(c_hbm, c_v)
        pltpu.sync_copy(pad_hbm, pad_v)

        def dots(i, part, rows):
            def block_body(blk, carry):
                row0 = blk * SC_LANES
                outv = jnp.zeros((SC_LANES,), _f32)
                for sub in range(SC_LANES // SC_BLOCK):
                    def chunk_body(jj, accs):
                        accs = list(accs)
                        for u in range(SC_UNROLL):
                            off = pl.multiple_of((jj * SC_UNROLL + u) * SC_LANES, SC_LANES)
                            h_lo = h_v[i, pl.ds(off, SC_LANES)]
                            h_hi = h_v[i, pl.ds(half_d + off, SC_LANES)]
                            for e in range(SC_BLOCK):
                                lo, hi = _sc_unpack(rows[row0 + sub * SC_BLOCK + e, pl.ds(off, SC_LANES)])
                                accs[e] = accs[e] + (lo * h_lo + hi * h_hi)
                        return tuple(accs)

                    accs = lax.fori_loop(0, half_d // SC_LANES // SC_UNROLL, chunk_body,
                                         tuple(jnp.zeros((SC_LANES,), _f32) for _ in range(SC_BLOCK)))
                    for e in range(SC_BLOCK):
                        outv = jnp.where(lane == sub * SC_BLOCK + e, jnp.sum(accs[e]), outv)
                col = pl.multiple_of(part * SC_GATHER + row0, SC_LANES)
                acts_v[i, pl.ds(col, SC_LANES)] = outv
                return carry

            lax.fori_loop(0, SC_GATHER // SC_LANES, block_body, 0)

        def sort_desc(k, v):
            return plsc.sort_key_val(k, v, descending=True)

        def merge(x, y):
            yk, yv = lax.rev(y[0], (0,)), lax.rev(y[1], (0,))
            take = x[0] >= yk
            return sort_desc(jnp.where(take, x[0], yk), jnp.where(take, x[1], yv))

        def top_of(vectors):
            while len(vectors) > 1:
                vectors = [merge(vectors[j], vectors[j + 1]) for j in range(0, len(vectors), 2)]
            return vectors[0]

        def head_body(i, hd):
            halves = []
            for half in range(2):
                col = (2 * hd + half) * PEER_NKEYS
                vecs = [sort_desc(s_v[i, pl.ds(pl.multiple_of(col + j * SC_LANES, SC_LANES), SC_LANES)],
                                  lane + j * SC_LANES) for j in range(n_vec)]
                halves.append(top_of(vecs))
            (v1, i1), (v2, i2) = halves
            top_v[0, :] = v1
            top_v[1, :] = i1.astype(_f32)
            top_v[2, :] = v2
            top_v[3, :] = i2.astype(_f32)

            def pick(row, pos):
                return plsc.load_gather(top_v, [jnp.full((SC_LANES,), row, jnp.int32), pos])

            cands = []
            for q in range(n_cand):
                a, c = a_v[q, :], c_v[q, :]
                cs = pick(0, a) + pick(2, c) + pad_v[q, :]
                ci = pick(1, a) * float(PEER_NKEYS) + pick(3, c)
                cands.append(sort_desc(cs, ci))
            top_s, top_i = top_of(cands)
            e = jnp.exp(top_s - jnp.max(top_s))
            slots = pl.ds(pl.multiple_of(hd * PEER_TOPK, PEER_TOPK), PEER_TOPK)
            idx_v[i, slots] = top_i.astype(jnp.int32)
            gate_v[i, slots] = e / jnp.sum(e)

        @pl.loop(0, per_worker // SC_TOKENS)
        def _(step):
            tok = pl.multiple_of(base + step * SC_TOKENS, SC_TOKENS)
            pltpu.sync_copy(s_hbm.at[pl.ds(tok, SC_TOKENS)], s_v)
            pltpu.sync_copy(h_hbm.at[pl.ds(tok, SC_TOKENS)], h_v)

            @pl.loop(0, SC_TOKENS * PEER_HEADS)
            def _(n):
                head_body(n // PEER_HEADS, n % PEER_HEADS)

            pltpu.sync_copy(idx_v, idx_hbm.at[pl.ds(tok, SC_TOKENS)])
            pltpu.sync_copy(gate_v, gate_hbm.at[pl.ds(tok, SC_TOKENS)])
            _sc_gather_loop(tab_hbm, idx_v, ((rows0, sem0), (rows1, sem1)), dots)
            pltpu.sync_copy(acts_v, acts_hbm.at[pl.ds(tok, SC_TOKENS)])

    return body(scores, pair_a, pair_c, pair_pad, tab, h)


def _sc_vaxpy(idx, w, tab, *, first, count):
    half = tab.shape[1]
    d = 2 * half
    per_worker = count // SC_WORKERS
    span = SC_COLS * SC_LANES

    @functools.partial(
        pl.kernel, mesh=_sc_mesh(),
        out_type=jax.ShapeDtypeStruct((count, d), _f32),
        scratch_types=[
            pltpu.VMEM((SC_TOKENS, PEER_SLOTS), jnp.int32),
            pltpu.VMEM((SC_TOKENS, PEER_SLOTS), _f32),
            pltpu.VMEM((SC_TOKENS, d), _f32),
            pltpu.VMEM((SC_GATHER, half), jnp.int32),
            pltpu.VMEM((SC_GATHER, half), jnp.int32),
            pltpu.SemaphoreType.DMA,
            pltpu.SemaphoreType.DMA,
        ],
        compiler_params=pltpu.CompilerParams(needs_layout_passes=False),
        name="sc_vaxpy",
    )
    def body(idx_hbm, w_hbm, tab_hbm, out_hbm, idx_v, w_v, out_v, rows0, rows1, sem0, sem1):
        base = _sc_worker_base(per_worker)

        def accumulate(i, part, rows):
            i_vec = jnp.full((SC_LANES,), i, jnp.int32)

            def span_body(cq, carry):
                def cols(c, offset=0):
                    return pl.ds(pl.multiple_of(offset + cq * span + c * SC_LANES, SC_LANES), SC_LANES)

                def expert_body(e, accs):
                    k_vec = jnp.full((SC_LANES,), part * SC_GATHER + e, jnp.int32)
                    wv = plsc.load_gather(w_v, [i_vec, k_vec])
                    new = []
                    for c in range(SC_COLS):
                        lo, hi = _sc_unpack(rows[e, cols(c)])
                        new += [accs[2 * c] + lo * wv, accs[2 * c + 1] + hi * wv]
                    return tuple(new)

                init = []
                for c in range(SC_COLS):
                    init += [out_v[i, cols(c)], out_v[i, cols(c, half)]]
                accs = lax.fori_loop(0, SC_GATHER, expert_body, tuple(init))
                for c in range(SC_COLS):
                    out_v[i, cols(c)] = accs[2 * c]
                    out_v[i, cols(c, half)] = accs[2 * c + 1]
                return carry

            lax.fori_loop(0, half // span, span_body, 0)

        @pl.loop(0, per_worker // SC_TOKENS)
        def _(step):
            off = pl.multiple_of(base + step * SC_TOKENS, SC_TOKENS)
            pltpu.sync_copy(idx_hbm.at[pl.ds(first + off, SC_TOKENS)], idx_v)
            pltpu.sync_copy(w_hbm.at[pl.ds(first + off, SC_TOKENS)], w_v)

            @pl.loop(0, SC_TOKENS)
            def _(i):
                @pl.loop(0, d // SC_LANES)
                def _(j):
                    out_v[i, pl.ds(pl.multiple_of(j * SC_LANES, SC_LANES), SC_LANES)] = (
                        jnp.zeros((SC_LANES,), _f32))

            _sc_gather_loop(tab_hbm, idx_v, ((rows0, sem0), (rows1, sem1)), accumulate)
            pltpu.sync_copy(out_v, out_hbm.at[pl.ds(off, SC_TOKENS)])

    return body(idx, w, tab)


ROW_CHUNKS = 8
PACK_ROWS = ROW_CHUNKS // 2
ROW_BUFFERS = 4


def _pack_table(tab):
    n, d = tab.shape
    bits = lax.bitcast_convert_type(tab.astype(_bf16), jnp.uint16).astype(jnp.uint32)
    bits = bits.reshape(n, PACK_ROWS, 2, d // ROW_CHUNKS)
    word = bits[:, :, 0, :] | (bits[:, :, 1, :] << 16)
    return lax.bitcast_convert_type(word, jnp.int32)


def _gate_weights_kernel(acts_ref, gate_ref, w_ref):
    a = acts_ref[...]
    gelu = 0.5 * a * (1.0 + lax.erf(a * (2.0 ** -0.5)))
    w_ref[...] = gate_ref[...] * gelu


def _gate_weights(acts, gate, *, tw):
    t, n = acts.shape
    spec = pl.BlockSpec((tw, n), lambda i: (i, 0))
    return pl.pallas_call(
        _gate_weights_kernel,
        grid=(t // tw,),
        in_specs=[spec, spec],
        out_specs=spec,
        out_shape=jax.ShapeDtypeStruct((t, n), _f32),
        compiler_params=pltpu.CompilerParams(dimension_semantics=("parallel",)),
        name="gate_weights",
    )(acts, gate)


def _gather_rows(idx_ref, t, tab_ref, rows_ref):
    for k in range(PEER_SLOTS):
        row = pl.multiple_of(idx_ref[t, k], PACK_ROWS)
        rows_ref[k * PACK_ROWS:(k + 1) * PACK_ROWS, :] = tab_ref[pl.ds(row, PACK_ROWS), :]


def _rows_matrix(rows_ref):
    return pltpu.bitcast(rows_ref[...], _bf16)


def _token_loop(tb, idx_ref, tab_ref, row_bufs, compute):
    n = len(row_bufs)
    ahead = 2
    for j in range(ahead):
        _gather_rows(idx_ref, j, tab_ref, row_bufs[j])

    def body(i, carry):
        t0 = n * i
        for j in range(n):
            nxt = jnp.minimum(t0 + j + ahead, tb - 1)
            _gather_rows(idx_ref, nxt, tab_ref, row_bufs[(j + ahead) % n])
            compute(t0 + j, _rows_matrix(row_bufs[j]))
        return carry

    lax.fori_loop(0, tb // n, body, 0)


def _chunk_diag_mask():
    shape = (ROW_CHUNKS, PEER_SLOTS * ROW_CHUNKS)
    return (_iota(shape, 1) % ROW_CHUNKS) == _iota(shape, 0)


def _vaxpy_kernel(idx_ref, w_ref, tab_ref, out_ref, wrep_ref, *row_bufs):
    tb = out_ref.shape[0]
    diag = _chunk_diag_mask()
    shape = (PEER_SLOTS, PEER_SLOTS * ROW_CHUNKS)
    spread = (_iota(shape, 0) == (_iota(shape, 1) // ROW_CHUNKS)).astype(_bf16)
    wrep_ref[...] = _dot_exact_rhs01(w_ref[...], spread)

    def compute(t, m):
        w_row = jnp.broadcast_to(wrep_ref[pl.ds(t, 1), :], diag.shape)
        w_hi, w_lo = _split2(jnp.where(diag, w_row, 0.0))
        out_ref[t] = _dot(w_hi, m) + _dot(w_lo, m)

    _token_loop(tb, idx_ref, tab_ref, row_bufs, compute)


def _vaxpy(idx, w, tab, *, tb, count):
    return pl.pallas_call(
        _vaxpy_kernel,
        grid=(count // tb,),
        in_specs=[
            pl.BlockSpec((tb, PEER_SLOTS), lambda i: (i, 0), memory_space=pltpu.SMEM),
            pl.BlockSpec((tb, PEER_SLOTS), lambda i: (i, 0)),
            _const_spec(tab.shape),
        ],
        out_specs=pl.BlockSpec((tb, ROW_CHUNKS, 128), lambda i: (i, 0, 0)),
        out_shape=jax.ShapeDtypeStruct((count, ROW_CHUNKS, 128), _f32),
        scratch_shapes=[
            pltpu.VMEM((tb, PEER_SLOTS * ROW_CHUNKS), _f32),
        ] + [pltpu.VMEM((PEER_SLOTS * PACK_ROWS, 128), jnp.int32)
             for _ in range(ROW_BUFFERS)],
        compiler_params=pltpu.CompilerParams(
            dimension_semantics=("parallel",),
            vmem_limit_bytes=VMEM_LIMIT_BYTES),
        name="vaxpy",
    )(idx, w, tab)


def _final_kernel(x_ref, p_ref, g_ref, acc_ref, out_ref, *, normalize):
    del acc_ref
    x = x_ref[...] + p_ref[...]
    if normalize:
        x = x * lax.rsqrt(jnp.mean(x * x, axis=-1, keepdims=True) + EPS) * g_ref[...]
    out_ref[...] = x


def _final(acc, x, peer, g, *, normalize, tf, first, row):
    t, d = peer.shape
    x_off = first // tf
    out_off = row // tf
    return pl.pallas_call(
        functools.partial(_final_kernel, normalize=normalize),
        grid=(t // tf,),
        in_specs=[
            pl.BlockSpec((tf, d), lambda i: (i + x_off, 0)),
            pl.BlockSpec((tf, d), lambda i: (i, 0)),
            _const_spec((1, d)),
            pl.BlockSpec(memory_space=pl.ANY),
        ],
        out_specs=pl.BlockSpec((tf, d), lambda i: (i + out_off, 0)),
        out_shape=jax.ShapeDtypeStruct(acc.shape, _f32),
        input_output_aliases={3: 0},
        compiler_params=pltpu.CompilerParams(dimension_semantics=("parallel",)),
        name="final_norm",
    )(x, peer, g, acc)


def kernel(x, norm_mix_g, w_in, hg_lb_logits, hg_out_norm_g, conv_w, w_branch_hg, w_branch_conv, w_out, norm_ffn_g, peer_w_query, peer_keys1, peer_keys2, peer_u, peer_v, norm_final_g):
    b_, s_, d = x.shape
    depth = w_in.shape[0]
    lb_all = jnp.cumsum(jax.nn.softmax(hg_lb_logits.astype(_f32), axis=0), axis=0)
    n_groups = BATCH_GROUPS if b_ % BATCH_GROUPS == 0 else 1
    bg = b_ // n_groups
    tg = bg * s_
    for l in range(depth):
        wq = peer_w_query[l].astype(_bf16).reshape(d, PEER_HEADS, 2, PEER_HALF).transpose(1, 2, 0, 3)
        u_sc = _pack_halves(peer_u[l])
        last = l == depth - 1
        g = norm_final_g[None] if last else jnp.ones((1, d), _f32)

        def front(x_in):
            xc = _mix(x_in, norm_mix_g[l][None], w_in[l].astype(_bf16), lb_all[l][None],
                      hg_out_norm_g[l][None], conv_w[l], w_branch_hg[l].astype(_bf16),
                      w_branch_conv[l].astype(_bf16), w_out[l].astype(_bf16), ts=MIX_TILE)
            xf = xc.reshape(tg, d)
            h, scores = _scores(xf, norm_ffn_g[l][None], wq,
                                peer_keys1[l].astype(_bf16), peer_keys2[l].astype(_bf16), tr=ROUTE_TILE)
            return (xf, *_sc_route_dot(scores, u_sc, h)), scores

        def back(xf, idx, gate, acts, n_sc):
            n_tc = tg - n_sc
            w = _gate_weights(acts, gate, tw=FINAL_TILE)
            peer_tc = _vaxpy(idx * PACK_ROWS, w, v_tab.reshape(-1, 128), tb=EXPERT_TILE, count=n_tc).reshape(n_tc, d)
            peer_sc = _sc_vaxpy(idx, w, v_sc, first=n_tc, count=n_sc) if n_sc else None
            return peer_tc, peer_sc

        fronts = [front(x[c * bg:(c + 1) * bg]) for c in range(n_groups)]
        v_rows = lax.optimization_barrier((peer_v[l], fronts[0][1]))[0]
        v_tab = _pack_table(v_rows)
        v_sc = _pack_halves(v_rows)
        acc = jnp.zeros((b_ * s_, d), _f32)
        sc_parts = []
        for c, ((xf, idx, gate, acts), _) in enumerate(fronts):
            share = SC_SHARE_LAST if c == n_groups - 1 else SC_SHARE
            n_sc = share if share < tg else 0
            acts = lax.optimization_barrier((acts, acc if c else fronts[-1][1]))[0]
            peer_tc, peer_sc = back(xf, idx, gate, acts, n_sc)
            acc = _final(acc, xf, peer_tc, g, normalize=last, tf=FINAL_TILE, first=0, row=c * tg)
            sc_parts.append((c * tg + tg - n_sc, tg - n_sc, xf, peer_sc))
        for row, first, xf, p_sc in sc_parts:
            if p_sc is not None:
                acc = _final(acc, xf, p_sc, g, normalize=last, tf=FINAL_TILE, first=first, row=row)
        x = acc.reshape(b_, s_, d)
    return x
```

```python
import functools

import jax
import jax.numpy as jnp
from jax import lax
from jax.experimental import pallas as pl
from jax.experimental.pallas import tpu as pltpu
from jax.experimental.pallas import tpu_sc as plsc

EPS = 1e-6
CHUNK = 64
SUB = 8
HEADS = 8
HEAD_DIM = 64
HG_WIDTH = HEADS * HEAD_DIM
GROUP = 256
N_GROUPS = HG_WIDTH // GROUP
CONV_K = 3
PEER_HEADS = 8
PEER_NKEYS = 128
PEER_HALF = 128
PEER_TOPK = 16
PEER_SLOTS = PEER_HEADS * PEER_TOPK

VMEM_LIMIT_BYTES = 56 * 1024 * 1024

MIX_TILE = 256
ROUTE_TILE = 256
EXPERT_TILE = 128
FINAL_TILE = 256
BATCH_GROUPS = 8
SC_SHARE = 0
SC_SHARE_LAST = 2048

_f32 = jnp.float32
_bf16 = jnp.bfloat16


def _dot(a, b):
    return jnp.dot(a, b, preferred_element_type=_f32)


def _dot_nt(a, b):
    return lax.dot_general(a, b, (((1,), (1,)), ((), ())), preferred_element_type=_f32)


def _dot_tn(a, b):
    return lax.dot_general(a, b, (((0,), (0,)), ((), ())), preferred_element_type=_f32)


def _split3(x):
    hi = x.astype(_bf16)
    r1 = x - hi.astype(_f32)
    mid = r1.astype(_bf16)
    lo = (r1 - mid.astype(_f32)).astype(_bf16)
    return hi, mid, lo


def _split2(x):
    hi = x.astype(_bf16)
    lo = (x - hi.astype(_f32)).astype(_bf16)
    return hi, lo


def _dot_exact_rhs01(x, m01):
    hi, mid, lo = _split3(x)
    return _dot(hi, m01) + _dot(mid, m01) + _dot(lo, m01)


def _dot_exact_lhs01(m01, x):
    hi, mid, lo = _split3(x)
    return _dot(m01, hi) + _dot(m01, mid) + _dot(m01, lo)


def _iota(shape, dim):
    return lax.broadcasted_iota(jnp.int32, shape, dim)


def _hgrn2_chunk(q, k, lf, v, state_ref):
    n_sub = CHUNK // SUB
    row = _iota((CHUNK, CHUNK), 0)
    col = _iota((CHUNK, CHUNK), 1)
    tril = (col <= row).astype(_bf16)
    b = _dot_exact_lhs01(tril, lf)

    b_end = [b[(j + 1) * SUB - 1:(j + 1) * SUB, :] for j in range(n_sub)]
    b_end_rows = jnp.concatenate([jnp.broadcast_to(e, (SUB, HG_WIDTH)) for e in b_end], axis=0)
    b_last = b_end[-1]

    q_in = (q * jnp.exp(b)).astype(_bf16)
    k_sub = (k * jnp.exp(b_end_rows - b)).astype(_bf16)
    k_out = (k * jnp.exp(b_last - b)).astype(_bf16)
    q_from = [(q * jnp.exp(jnp.minimum(b - b_end[j], 0.0))).astype(_bf16) for j in range(n_sub - 1)]
    v_b = v.astype(_bf16)

    gr = _iota((GROUP, GROUP), 0) // HEAD_DIM
    gc = _iota((GROUP, GROUP), 1) // HEAD_DIM
    head_mask = gr == gc
    t_blk = _iota((CHUNK, GROUP), 0) // SUB
    s_blk = (_iota((CHUNK, GROUP), 1) % HEAD_DIM) // SUB

    outs = []
    for g in range(N_GROUPS):
        sl = slice(g * GROUP, (g + 1) * GROUP)
        st = state_ref[g]
        o_g = _dot_nt(q_in[:, sl], st.astype(_bf16))

        zero_b = jnp.zeros((), _bf16)
        k_bd = jnp.where(head_mask, jnp.concatenate([k_sub[:, sl]] * (GROUP // CHUNK), axis=0), zero_b)
        v_bd = jnp.where(head_mask, jnp.concatenate([v_b[:, sl]] * (GROUP // CHUNK), axis=0), zero_b)
        q_stack = jnp.concatenate([qf[:, sl] for qf in q_from], axis=0)
        r = _dot_nt(q_stack, k_bd)
        scores = jnp.zeros((CHUNK, GROUP), _f32)
        for j in range(n_sub - 1):
            sel = (s_blk == j) & (t_blk > j)
            scores = jnp.where(sel, r[j * CHUNK:(j + 1) * CHUNK, :], scores)
        o_g = o_g + _dot(scores.astype(_bf16), v_bd)
        outs.append(o_g)

        upd = _dot_tn(v_b[:, sl], k_out[:, sl])
        decay = jnp.exp(b_last[:, sl])
        state_ref[g] = st * decay + jnp.where(head_mask, upd, 0.0)
    o = jnp.concatenate(outs, axis=1)

    ones_bd = ((_iota((HG_WIDTH, HG_WIDTH), 0) // HEAD_DIM)
               == (_iota((HG_WIDTH, HG_WIDTH), 1) // HEAD_DIM)).astype(_bf16)
    t_in_sub = _iota((CHUNK, HG_WIDTH), 0) % SUB
    for lag in range(SUB):
        if lag == 0:
            p = q * k
            v_l = v
        else:
            valid = t_in_sub >= lag
            k_l = pltpu.roll(k, lag, 0)
            b_l = pltpu.roll(b, lag, 0)
            v_l = pltpu.roll(v, lag, 0)
            p = jnp.where(valid, q * k_l * jnp.exp(jnp.minimum(b - b_l, 0.0)), 0.0)
        s_l = _dot(p.astype(_bf16), ones_bd)
        o = o + s_l * v_l
    return o


def _mix_kernel(x_ref, g_ref, win_ref, lb_ref, hgn_ref, convw_ref, pa_ref, pb_ref, wo_ref,
                out_ref, state_ref, carry_ref, q_s, k_s, lf_s, v_s, o_s):
    ts = x_ref.shape[0]
    d_model = x_ref.shape[1]
    w = HG_WIDTH

    @pl.when(pl.program_id(1) == 0)
    def _():
        state_ref[...] = jnp.zeros_like(state_ref)
        carry_ref[...] = jnp.zeros_like(carry_ref)

    x = x_ref[...]
    h = x * lax.rsqrt(jnp.mean(x * x, axis=-1, keepdims=True) + EPS) * g_ref[...]
    hb = h.astype(_bf16)

    def proj(i, width=w):
        return _dot(hb, win_ref[:, i * w:i * w + width])

    lb = lb_ref[...]
    q_s[...] = jax.nn.silu(proj(0)) * (HEAD_DIM ** -0.5)
    forget = lb + (1.0 - lb) * jax.nn.sigmoid(proj(1))
    k_s[...] = 1.0 - forget
    lf_s[...] = jnp.log(forget)
    v_s[...] = proj(2)

    def chunk_body(c, carry):
        rows = pl.ds(pl.multiple_of(c * CHUNK, CHUNK), CHUNK)
        o_s[rows, :] = _hgrn2_chunk(q_s[rows, :], k_s[rows, :], lf_s[rows, :], v_s[rows, :], state_ref)
        return carry

    lax.fori_loop(0, ts // CHUNK, chunk_body, 0)

    o = o_s[...]
    ones_bd = ((_iota((w, w), 0) // HEAD_DIM) == (_iota((w, w), 1) // HEAD_DIM)).astype(_bf16)
    ms = _dot_exact_rhs01(o * o, ones_bd) * (1.0 / HEAD_DIM)
    o = o * lax.rsqrt(ms + EPS) * hgn_ref[...]
    y_a = (o * jax.nn.silu(proj(3))).astype(_bf16)

    u = proj(5) * proj(6)
    prev = carry_ref[...]
    rowi = _iota((ts, w), 0)
    u1 = jnp.where(rowi >= 1, pltpu.roll(u, 1, 0), jnp.broadcast_to(prev[7:8, :], (ts, w)))
    u2 = jnp.where(rowi >= 2, pltpu.roll(u, 2, 0),
                   jnp.where(rowi == 1, jnp.broadcast_to(prev[7:8, :], (ts, w)),
                             jnp.broadcast_to(prev[6:7, :], (ts, w))))
    carry_ref[...] = u[ts - 8:, :]
    cw = convw_ref[...]
    y_b = (proj(4) * (cw[0:1, :] * u2 + cw[1:2, :] * u1 + cw[2:3, :] * u)).astype(_bf16)

    g_a = jax.nn.sigmoid(proj(7, d_model))
    g_b = jax.nn.sigmoid(_dot(hb, win_ref[:, 7 * w + d_model:7 * w + 2 * d_model]))
    merged = g_a * _dot(y_a, pa_ref[...]) + g_b * _dot(y_b, pb_ref[...])
    out_ref[...] = x + _dot(merged.astype(_bf16), wo_ref[...])


def _const_spec(shape):
    nd = len(shape)
    return pl.BlockSpec(shape, lambda *_: (0,) * nd, pipeline_mode=pl.Buffered(1))


def _mix(x, norm_g, w_in, lb, hg_norm_g, conv_w, w_a, w_b, w_o, *, ts):
    b_, s_, d = x.shape
    in_cols = w_in.shape[1]
    w = HG_WIDTH
    grid = (b_, s_ // ts)
    return pl.pallas_call(
        _mix_kernel,
        grid=grid,
        in_specs=[
            pl.BlockSpec((None, ts, d), lambda b, s: (b, s, 0)),
            _const_spec((1, d)),
            _const_spec((d, in_cols)),
            _const_spec((1, w)),
            _const_spec((1, w)),
            _const_spec((CONV_K, w)),
            _const_spec((w, d)),
            _const_spec((w, d)),
            _const_spec((d, d)),
        ],
        out_specs=pl.BlockSpec((None, ts, d), lambda b, s: (b, s, 0)),
        out_shape=jax.ShapeDtypeStruct((b_, s_, d), _f32),
        scratch_shapes=[
            pltpu.VMEM((N_GROUPS, GROUP, GROUP), _f32),
            pltpu.VMEM((8, w), _f32),
            pltpu.VMEM((ts, w), _f32),
            pltpu.VMEM((ts, w), _f32),
            pltpu.VMEM((ts, w), _f32),
            pltpu.VMEM((ts, w), _f32),
            pltpu.VMEM((ts, w), _f32),
        ],
        compiler_params=pltpu.CompilerParams(
            dimension_semantics=("parallel", "arbitrary"),
            vmem_limit_bytes=VMEM_LIMIT_BYTES),
        name="mix",
    )(x, norm_g, w_in, lb, hg_norm_g, conv_w, w_a, w_b, w_o)


def _stair_pairs():
    pairs = [(a, c) for a in range(PEER_TOPK) for c in range(PEER_TOPK) if (a + 1) * (c + 1) <= PEER_TOPK]
    rows = -(-len(pairs) // 8) * 8
    ranks = jnp.arange(PEER_TOPK)[None, :]
    a_col = jnp.asarray([a for a, _ in pairs] + [-1] * (rows - len(pairs)))[:, None]
    c_col = jnp.asarray([c for _, c in pairs] + [-1] * (rows - len(pairs)))[:, None]
    pad = jnp.where(a_col < 0, -jnp.inf, 0.0).astype(_f32)
    return (a_col == ranks).astype(_bf16), (c_col == ranks).astype(_bf16), pad


ROUTE_LANES = 128


def _route_kernel(x_ref, g_ref, wq_ref, k1_ref, k2_ref, sa_ref, sc_ref, pad_ref, h_ref, idx_ref, gate_ref,
                  idx_t, e_t, top_ref):
    tr = x_ref.shape[0]
    n_cand = sa_ref.shape[0]
    x = x_ref[...]
    h = x * lax.rsqrt(jnp.mean(x * x, axis=-1, keepdims=True) + EPS) * g_ref[...]
    h_ref[...] = h
    hb = h.astype(_bf16)

    key_row = _iota((PEER_NKEYS, ROUTE_LANES), 0).astype(_f32)
    cand_row = _iota((n_cand, tr), 0).astype(_f32)
    neg_inf = jnp.float32(-jnp.inf)

    def extract_max(s):
        m = jnp.max(s, axis=0, keepdims=True)
        i = jnp.min(jnp.where(s == m, key_row, float(PEER_NKEYS)), axis=0, keepdims=True)
        return m, i, jnp.where(key_row == i, neg_inf, s)

    def head_body(hd, carry):
        q1 = _dot(hb, wq_ref[hd, 0]).astype(_bf16)
        q2 = _dot(hb, wq_ref[hd, 1]).astype(_bf16)
        s1 = _dot_nt(k1_ref[hd], q1)
        s2 = _dot_nt(k2_ref[hd], q2)

        for lt in range(tr // ROUTE_LANES):
            lanes = slice(lt * ROUTE_LANES, (lt + 1) * ROUTE_LANES)

            def half_body(k, c):
                m1, i1, r1 = extract_max(c[0])
                m2, i2, r2 = extract_max(c[1])
                for j, row in enumerate((m1, i1, m2, i2)):
                    top_ref[lt, j, pl.ds(k, 1), :] = row
                return r1, r2

            lax.fori_loop(0, PEER_TOPK, half_body, (s1[:, lanes], s2[:, lanes]))

        def top(j):
            return jnp.concatenate([top_ref[lt, j] for lt in range(tr // ROUTE_LANES)], axis=1)

        sa = sa_ref[...]
        sc = sc_ref[...]
        cand_s = _dot_exact_lhs01(sa, top(0)) + _dot_exact_lhs01(sc, top(2)) + pad_ref[...]
        cand_i = (_dot(sa, top(1).astype(_bf16)) * float(PEER_NKEYS)
                  + _dot(sc, top(3).astype(_bf16)))

        def pick_body(k, c):
            cand_s, denom, m_first = c
            m = jnp.max(cand_s, axis=0, keepdims=True)
            pos = jnp.min(jnp.where(cand_s == m, cand_row, float(n_cand)), axis=0, keepdims=True)
            hit = cand_row == pos
            eid = jnp.max(jnp.where(hit, cand_i, -1.0), axis=0, keepdims=True)
            m_first = jnp.where(k == 0, m, m_first)
            e = jnp.exp(m - m_first)
            slot = hd * PEER_TOPK + k
            idx_t[pl.ds(slot, 1), :] = eid
            e_t[pl.ds(slot, 1), :] = e
            return jnp.where(hit, neg_inf, cand_s), denom + e, m_first

        zero_row = jnp.zeros((1, tr), _f32)
        _, denom, _ = lax.fori_loop(0, PEER_TOPK, pick_body, (cand_s, zero_row, zero_row))
        rows = pl.ds(pl.multiple_of(hd * PEER_TOPK, PEER_TOPK), PEER_TOPK)
        e_t[rows, :] = e_t[rows, :] / denom
        return carry

    lax.fori_loop(0, PEER_HEADS, head_body, 0)
    idx_ref[...] = idx_t[...].T.astype(jnp.int32)
    gate_ref[...] = e_t[...].T


def _route(x, norm_g, w_query, keys1, keys2, *, tr):
    t, d = x.shape
    sel_a, sel_c, pad = _stair_pairs()
    pad = jnp.broadcast_to(pad, (pad.shape[0], tr))
    return pl.pallas_call(
        _route_kernel,
        grid=(t // tr,),
        in_specs=[
            pl.BlockSpec((tr, d), lambda i: (i, 0)),
            _const_spec((1, d)),
            _const_spec(w_query.shape),
            _const_spec(keys1.shape),
            _const_spec(keys2.shape),
            _const_spec(sel_a.shape),
            _const_spec(sel_c.shape),
            _const_spec(pad.shape),
        ],
        out_specs=[
            pl.BlockSpec((tr, d), lambda i: (i, 0)),
            pl.BlockSpec((tr, PEER_SLOTS), lambda i: (i, 0)),
            pl.BlockSpec((tr, PEER_SLOTS), lambda i: (i, 0)),
        ],
        out_shape=[
            jax.ShapeDtypeStruct((t, d), _f32),
            jax.ShapeDtypeStruct((t, PEER_SLOTS), jnp.int32),
            jax.ShapeDtypeStruct((t, PEER_SLOTS), _f32),
        ],
        scratch_shapes=[
            pltpu.VMEM((PEER_SLOTS, tr), _f32),
            pltpu.VMEM((PEER_SLOTS, tr), _f32),
            pltpu.VMEM((tr // ROUTE_LANES, 4, PEER_TOPK, ROUTE_LANES), _f32),
        ],
        compiler_params=pltpu.CompilerParams(
            dimension_semantics=("parallel",),
            vmem_limit_bytes=VMEM_LIMIT_BYTES),
        name="route",
    )(x, norm_g, w_query, keys1, keys2, sel_a, sel_c, pad)


def _scores_kernel(x_ref, g_ref, wq_ref, k1_ref, k2_ref, h_ref, s_ref):
    x = x_ref[...]
    h = x * lax.rsqrt(jnp.mean(x * x, axis=-1, keepdims=True) + EPS) * g_ref[...]
    h_ref[...] = h
    hb = h.astype(_bf16)
    for hd in range(PEER_HEADS):
        for half, keys_ref in enumerate((k1_ref, k2_ref)):
            q = _dot(hb, wq_ref[hd, half]).astype(_bf16)
            col = (2 * hd + half) * PEER_NKEYS
            s_ref[:, col:col + PEER_NKEYS] = _dot_nt(q, keys_ref[hd])


def _scores(x, norm_g, w_query, keys1, keys2, *, tr):
    t, d = x.shape
    width = 2 * PEER_HEADS * PEER_NKEYS
    return pl.pallas_call(
        _scores_kernel,
        grid=(t // tr,),
        in_specs=[
            pl.BlockSpec((tr, d), lambda i: (i, 0)),
            _const_spec((1, d)),
            _const_spec(w_query.shape),
            _const_spec(keys1.shape),
            _const_spec(keys2.shape),
        ],
        out_specs=[
            pl.BlockSpec((tr, d), lambda i: (i, 0)),
            pl.BlockSpec((tr, width), lambda i: (i, 0)),
        ],
        out_shape=[
            jax.ShapeDtypeStruct((t, d), _f32),
            jax.ShapeDtypeStruct((t, width), _f32),
        ],
        compiler_params=pltpu.CompilerParams(
            dimension_semantics=("parallel",),
            vmem_limit_bytes=VMEM_LIMIT_BYTES),
        name="scores",
    )(x, norm_g, w_query, keys1, keys2)


SC_CORES = 2
SC_SUBCORES = 16
SC_LANES = 16
SC_WORKERS = SC_CORES * SC_SUBCORES
SC_TOKENS = 8
SC_GATHER = 32
SC_BLOCK = 8
SC_UNROLL = 2
SC_COLS = 8
HI_MASK = -65536


def _sc_mesh():
    return plsc.VectorSubcoreMesh(core_axis_name="c", subcore_axis_name="s")


def _sc_worker_base(per_worker):
    return (lax.axis_index("s") * SC_CORES + lax.axis_index("c")) * per_worker


def _sc_gather_loop(tab_hbm, idx_v, bufs, consume):
    n_parts = PEER_SLOTS // SC_GATHER
    n_gathers = SC_TOKENS * n_parts

    def gather(g, parity):
        rows, sem = bufs[parity]
        i = g // n_parts
        col = pl.multiple_of((g % n_parts) * SC_GATHER, SC_GATHER)
        return pltpu.make_async_copy(tab_hbm.at[idx_v.at[i, pl.ds(col, SC_GATHER)]], rows, sem)

    gather(0, 0).start()

    @pl.loop(0, n_gathers // 2)
    def _(pair):
        g = 2 * pair
        gather(g + 1, 1).start()
        gather(g, 0).wait()
        consume(g // n_parts, g % n_parts, bufs[0][0])

        @pl.when(g + 2 < n_gathers)
        def _():
            gather(g + 2, 0).start()

        gather(g + 1, 1).wait()
        consume((g + 1) // n_parts, (g + 1) % n_parts, bufs[1][0])


def _pack_halves(tab):
    half = tab.shape[1] // 2
    bits = lax.bitcast_convert_type(tab.astype(_bf16), jnp.uint16).astype(jnp.uint32)
    return lax.bitcast_convert_type(bits[:, :half] | (bits[:, half:] << 16), jnp.int32)


def _sc_unpack(words):
    lo = lax.bitcast_convert_type(words << 16, _f32)
    hi = lax.bitcast_convert_type(words & HI_MASK, _f32)
    return lo, hi


def _stair_vectors():
    pairs = [(a, c) for a in range(PEER_TOPK) for c in range(PEER_TOPK) if (a + 1) * (c + 1) <= PEER_TOPK]
    n = -(-len(pairs) // SC_LANES)
    fill = n * SC_LANES - len(pairs)
    a = jnp.asarray([p[0] for p in pairs] + [0] * fill, jnp.int32).reshape(n, SC_LANES)
    c = jnp.asarray([p[1] for p in pairs] + [0] * fill, jnp.int32).reshape(n, SC_LANES)
    pad = jnp.asarray([0.0] * len(pairs) + [-jnp.inf] * fill, _f32).reshape(n, SC_LANES)
    return a, c, pad


def _sc_route_dot(scores, tab, h):
    t, width = scores.shape
    d = h.shape[1]
    half_d = d // 2
    per_worker = t // SC_WORKERS
    pair_a, pair_c, pair_pad = _stair_vectors()
    n_cand = pair_a.shape[0]
    n_vec = PEER_NKEYS // SC_LANES

    @functools.partial(
        pl.kernel, mesh=_sc_mesh(),
        out_type=(jax.ShapeDtypeStruct((t, PEER_SLOTS), jnp.int32),
                  jax.ShapeDtypeStruct((t, PEER_SLOTS), _f32),
                  jax.ShapeDtypeStruct((t, PEER_SLOTS), _f32)),
        scratch_types=[
            pltpu.VMEM((SC_TOKENS, width), _f32),
            pltpu.VMEM((SC_TOKENS, PEER_SLOTS), jnp.int32),
            pltpu.VMEM((SC_TOKENS, PEER_SLOTS), _f32),
            pltpu.VMEM((n_cand, SC_LANES), jnp.int32),
            pltpu.VMEM((n_cand, SC_LANES), jnp.int32),
            pltpu.VMEM((n_cand, SC_LANES), _f32),
            pltpu.VMEM((4, SC_LANES), _f32),
            pltpu.VMEM((SC_TOKENS, d), _f32),
            pltpu.VMEM((SC_TOKENS, PEER_SLOTS), _f32),
            pltpu.VMEM((SC_GATHER, half_d), jnp.int32),
            pltpu.VMEM((SC_GATHER, half_d), jnp.int32),
            pltpu.SemaphoreType.DMA,
            pltpu.SemaphoreType.DMA,
        ],
        compiler_params=pltpu.CompilerParams(needs_layout_passes=False),
        name="sc_route_dot",
    )
    def body(s_hbm, a_hbm, c_hbm, pad_hbm, tab_hbm, h_hbm, idx_hbm, gate_hbm, acts_hbm,
             s_v, idx_v, gate_v, a_v, c_v, pad_v, top_v, h_v, acts_v, rows0, rows1, sem0, sem1):
        base = _sc_worker_base(per_worker)
        lane = lax.iota(jnp.int32, SC_LANES)
        pltpu.sync_copy(a_hbm, a_v)
        pltpu.sync_copy(c_hbm, c_v)
        pltpu.sync_copy(pad_hbm, pad_v)

        def dots(i, part, rows):
            def block_body(blk, carry):
                row0 = blk * SC_LANES
                outv = jnp.zeros((SC_LANES,), _f32)
                for sub in range(SC_LANES // SC_BLOCK):
                    def chunk_body(jj, accs):
                        accs = list(accs)
                        for u in range(SC_UNROLL):
                            off = pl.multiple_of((jj * SC_UNROLL + u) * SC_LANES, SC_LANES)
                            h_lo = h_v[i, pl.ds(off, SC_LANES)]
                            h_hi = h_v[i, pl.ds(half_d + off, SC_LANES)]
                            for e in range(SC_BLOCK):
                                lo, hi = _sc_unpack(rows[row0 + sub * SC_BLOCK + e, pl.ds(off, SC_LANES)])
                                accs[e] = accs[e] + (lo * h_lo + hi * h_hi)
                        return tuple(accs)

                    accs = lax.fori_loop(0, half_d // SC_LANES // SC_UNROLL, chunk_body,
                                         tuple(jnp.zeros((SC_LANES,), _f32) for _ in range(SC_BLOCK)))
                    for e in range(SC_BLOCK):
                        outv = jnp.where(lane == sub * SC_BLOCK + e, jnp.sum(accs[e]), outv)
                col = pl.multiple_of(part * SC_GATHER + row0, SC_LANES)
                acts_v[i, pl.ds(col, SC_LANES)] = outv
                return carry

            lax.fori_loop(0, SC_GATHER // SC_LANES, block_body, 0)

        def sort_desc(k, v):
            return plsc.sort_key_val(k, v, descending=True)

        def merge(x, y):
            yk, yv = lax.rev(y[0], (0,)), lax.rev(y[1], (0,))
            take = x[0] >= yk
            return sort_desc(jnp.where(take, x[0], yk), jnp.where(take, x[1], yv))

        def top_of(vectors):
            while len(vectors) > 1:
                vectors = [merge(vectors[j], vectors[j + 1]) for j in range(0, len(vectors), 2)]
            return vectors[0]

        def head_body(i, hd):
            halves = []
            for half in range(2):
                col = (2 * hd + half) * PEER_NKEYS
                vecs = [sort_desc(s_v[i, pl.ds(pl.multiple_of(col + j * SC_LANES, SC_LANES), SC_LANES)],
                                  lane + j * SC_LANES) for j in range(n_vec)]
                halves.append(top_of(vecs))
            (v1, i1), (v2, i2) = halves
            top_v[0, :] = v1
            top_v[1, :] = i1.astype(_f32)
            top_v[2, :] = v2
            top_v[3, :] = i2.astype(_f32)

            def pick(row, pos):
                return plsc.load_gather(top_v, [jnp.full((SC_LANES,), row, jnp.int32), pos])

            cands = []
            for q in range(n_cand):
                a, c = a_v[q, :], c_v[q, :]
                cs = pick(0, a) + pick(2, c) + pad_v[q, :]
                ci = pick(1, a) * float(PEER_NKEYS) + pick(3, c)
                cands.append(sort_desc(cs, ci))
            top_s, top_i = top_of(cands)
            e = jnp.exp(top_s - jnp.max(top_s))
            slots = pl.ds(pl.multiple_of(hd * PEER_TOPK, PEER_TOPK), PEER_TOPK)
            idx_v[i, slots] = top_i.astype(jnp.int32)
            gate_v[i, slots] = e / jnp.sum(e)

        @pl.loop(0, per_worker // SC_TOKENS)
        def _(step):
            tok = pl.multiple_of(base + step * SC_TOKENS, SC_TOKENS)
            pltpu.sync_copy(s_hbm.at[pl.ds(tok, SC_TOKENS)], s_v)
            pltpu.sync_copy(h_hbm.at[pl.ds(tok, SC_TOKENS)], h_v)

            @pl.loop(0, SC_TOKENS * PEER_HEADS)
            def _(n):
                head_body(n // PEER_HEADS, n % PEER_HEADS)

            pltpu.sync_copy(idx_v, idx_hbm.at[pl.ds(tok, SC_TOKENS)])
            pltpu.sync_copy(gate_v, gate_hbm.at[pl.ds(tok, SC_TOKENS)])
            _sc_gather_loop(tab_hbm, idx_v, ((rows0, sem0), (rows1, sem1)), dots)
            pltpu.sync_copy(acts_v, acts_hbm.at[pl.ds(tok, SC_TOKENS)])

    return body(scores, pair_a, pair_c, pair_pad, tab, h)


def _sc_vaxpy(idx, w, tab, *, first, count):
    half = tab.shape[1]
    d = 2 * half
    per_worker = count // SC_WORKERS
    span = SC_COLS * SC_LANES

    @functools.partial(
        pl.kernel, mesh=_sc_mesh(),
        out_type=jax.ShapeDtypeStruct((count, d), _f32),
        scratch_types=[
            pltpu.VMEM((SC_TOKENS, PEER_SLOTS), jnp.int32),
            pltpu.VMEM((SC_TOKENS, PEER_SLOTS), _f32),
            pltpu.VMEM((SC_TOKENS, d), _f32),
            pltpu.VMEM((SC_GATHER, half), jnp.int32),
            pltpu.VMEM((SC_GATHER, half), jnp.int32),
            pltpu.SemaphoreType.DMA,
            pltpu.SemaphoreType.DMA,
        ],
        compiler_params=pltpu.CompilerParams(needs_layout_passes=False),
        name="sc_vaxpy",
    )
    def body(idx_hbm, w_hbm, tab_hbm, out_hbm, idx_v, w_v, out_v, rows0, rows1, sem0, sem1):
        base = _sc_worker_base(per_worker)

        def accumulate(i, part, rows):
            i_vec = jnp.full((SC_LANES,), i, jnp.int32)

            def span_body(cq, carry):
                def cols(c, offset=0):
                    return pl.ds(pl.multiple_of(offset + cq * span + c * SC_LANES, SC_LANES), SC_LANES)

                def expert_body(e, accs):
                    k_vec = jnp.full((SC_LANES,), part * SC_GATHER + e, jnp.int32)
                    wv = plsc.load_gather(w_v, [i_vec, k_vec])
                    new = []
                    for c in range(SC_COLS):
                        lo, hi = _sc_unpack(rows[e, cols(c)])
                        new += [accs[2 * c] + lo * wv, accs[2 * c + 1] + hi * wv]
                    return tuple(new)

                init = []
                for c in range(SC_COLS):
                    init += [out_v[i, cols(c)], out_v[i, cols(c, half)]]
                accs = lax.fori_loop(0, SC_GATHER, expert_body, tuple(init))
                for c in range(SC_COLS):
                    out_v[i, cols(c)] = accs[2 * c]
                    out_v[i, cols(c, half)] = accs[2 * c + 1]
                return carry

            lax.fori_loop(0, half // span, span_body, 0)

        @pl.loop(0, per_worker // SC_TOKENS)
        def _(step):
            off = pl.multiple_of(base + step * SC_TOKENS, SC_TOKENS)
            pltpu.sync_copy(idx_hbm.at[pl.ds(first + off, SC_TOKENS)], idx_v)
            pltpu.sync_copy(w_hbm.at[pl.ds(first + off, SC_TOKENS)], w_v)

            @pl.loop(0, SC_TOKENS)
            def _(i):
                @pl.loop(0, d // SC_LANES)
                def _(j):
                    out_v[i, pl.ds(pl.multiple_of(j * SC_LANES, SC_LANES), SC_LANES)] = (
                        jnp.zeros((SC_LANES,), _f32))

            _sc_gather_loop(tab_hbm, idx_v, ((rows0, sem0), (rows1, sem1)), accumulate)
            pltpu.sync_copy(out_v, out_hbm.at[pl.ds(off, SC_TOKENS)])

    return body(idx, w, tab)


ROW_CHUNKS = 8
PACK_ROWS = ROW_CHUNKS // 2
ROW_BUFFERS = 4


def _pack_table(tab):
    n, d = tab.shape
    bits = lax.bitcast_convert_type(tab.astype(_bf16), jnp.uint16).astype(jnp.uint32)
    bits = bits.reshape(n, PACK_ROWS, 2, d // ROW_CHUNKS)
    word = bits[:, :, 0, :] | (bits[:, :, 1, :] << 16)
    return lax.bitcast_convert_type(word, jnp.int32)


def _gate_weights_kernel(acts_ref, gate_ref, w_ref):
    a = acts_ref[...]
    gelu = 0.5 * a * (1.0 + lax.erf(a * (2.0 ** -0.5)))
    w_ref[...] = gate_ref[...] * gelu


def _gate_weights(acts, gate, *, tw):
    t, n = acts.shape
    spec = pl.BlockSpec((tw, n), lambda i: (i, 0))
    return pl.pallas_call(
        _gate_weights_kernel,
        grid=(t // tw,),
        in_specs=[spec, spec],
        out_specs=spec,
        out_shape=jax.ShapeDtypeStruct((t, n), _f32),
        compiler_params=pltpu.CompilerParams(dimension_semantics=("parallel",)),
        name="gate_weights",
    )(acts, gate)


def _gather_rows(idx_ref, t, tab_ref, rows_ref):
    for k in range(PEER_SLOTS):
        row = pl.multiple_of(idx_ref[t, k], PACK_ROWS)
        rows_ref[k * PACK_ROWS:(k + 1) * PACK_ROWS, :] = tab_ref[pl.ds(row, PACK_ROWS), :]


def _rows_matrix(rows_ref):
    return pltpu.bitcast(rows_ref[...], _bf16)


def _token_loop(tb, idx_ref, tab_ref, row_bufs, compute):
    n = len(row_bufs)
    ahead = 2
    for j in range(ahead):
        _gather_rows(idx_ref, j, tab_ref, row_bufs[j])

    def body(i, carry):
        t0 = n * i
        for j in range(n):
            nxt = jnp.minimum(t0 + j + ahead, tb - 1)
            _gather_rows(idx_ref, nxt, tab_ref, row_bufs[(j + ahead) % n])
            compute(t0 + j, _rows_matrix(row_bufs[j]))
        return carry

    lax.fori_loop(0, tb // n, body, 0)


def _chunk_diag_mask():
    shape = (ROW_CHUNKS, PEER_SLOTS * ROW_CHUNKS)
    return (_iota(shape, 1) % ROW_CHUNKS) == _iota(shape, 0)


def _vaxpy_kernel(idx_ref, w_ref, tab_ref, out_ref, wrep_ref, *row_bufs):
    tb = out_ref.shape[0]
    diag = _chunk_diag_mask()
    shape = (PEER_SLOTS, PEER_SLOTS * ROW_CHUNKS)
    spread = (_iota(shape, 0) == (_iota(shape, 1) // ROW_CHUNKS)).astype(_bf16)
    wrep_ref[...] = _dot_exact_rhs01(w_ref[...], spread)

    def compute(t, m):
        w_row = jnp.broadcast_to(wrep_ref[pl.ds(t, 1), :], diag.shape)
        w_hi, w_lo = _split2(jnp.where(diag, w_row, 0.0))
        out_ref[t] = _dot(w_hi, m) + _dot(w_lo, m)

    _token_loop(tb, idx_ref, tab_ref, row_bufs, compute)


def _vaxpy(idx, w, tab, *, tb, count):
    return pl.pallas_call(
        _vaxpy_kernel,
        grid=(count // tb,),
        in_specs=[
            pl.BlockSpec((tb, PEER_SLOTS), lambda i: (i, 0), memory_space=pltpu.SMEM),
            pl.BlockSpec((tb, PEER_SLOTS), lambda i: (i, 0)),
            _const_spec(tab.shape),
        ],
        out_specs=pl.BlockSpec((tb, ROW_CHUNKS, 128), lambda i: (i, 0, 0)),
        out_shape=jax.ShapeDtypeStruct((count, ROW_CHUNKS, 128), _f32),
        scratch_shapes=[
            pltpu.VMEM((tb, PEER_SLOTS * ROW_CHUNKS), _f32),
        ] + [pltpu.VMEM((PEER_SLOTS * PACK_ROWS, 128), jnp.int32)
             for _ in range(ROW_BUFFERS)],
        compiler_params=pltpu.CompilerParams(
            dimension_semantics=("parallel",),
            vmem_limit_bytes=VMEM_LIMIT_BYTES),
        name="vaxpy",
    )(idx, w, tab)


def _final_kernel(x_ref, p_ref, g_ref, acc_ref, out_ref, *, normalize):
    del acc_ref
    x = x_ref[...] + p_ref[...]
    if normalize:
        x = x * lax.rsqrt(jnp.mean(x * x, axis=-1, keepdims=True) + EPS) * g_ref[...]
    out_ref[...] = x


def _final(acc, x, peer, g, *, normalize, tf, first, row):
    t, d = peer.shape
    x_off = first // tf
    out_off = row // tf
    return pl.pallas_call(
        functools.partial(_final_kernel, normalize=normalize),
        grid=(t // tf,),
        in_specs=[
            pl.BlockSpec((tf, d), lambda i: (i + x_off, 0)),
            pl.BlockSpec((tf, d), lambda i: (i, 0)),
            _const_spec((1, d)),
            pl.BlockSpec(memory_space=pl.ANY),
        ],
        out_specs=pl.BlockSpec((tf, d), lambda i: (i + out_off, 0)),
        out_shape=jax.ShapeDtypeStruct(acc.shape, _f32),
        input_output_aliases={3: 0},
        compiler_params=pltpu.CompilerParams(dimension_semantics=("parallel",)),
        name="final_norm",
    )(x, peer, g, acc)


def kernel(x, norm_mix_g, w_in, hg_lb_logits, hg_out_norm_g, conv_w, w_branch_hg, w_branch_conv, w_out, norm_ffn_g, peer_w_query, peer_keys1, peer_keys2, peer_u, peer_v, norm_final_g):
    b_, s_, d = x.shape
    depth = w_in.shape[0]
    lb_all = jnp.cumsum(jax.nn.softmax(hg_lb_logits.astype(_f32), axis=0), axis=0)
    n_groups = BATCH_GROUPS if b_ % BATCH_GROUPS == 0 else 1
    bg = b_ // n_groups
    tg = bg * s_
    for l in range(depth):
        wq = peer_w_query[l].astype(_bf16).reshape(d, PEER_HEADS, 2, PEER_HALF).transpose(1, 2, 0, 3)
        u_sc = _pack_halves(peer_u[l])
        last = l == depth - 1
        g = norm_final_g[None] if last else jnp.ones((1, d), _f32)

        def front(x_in):
            xc = _mix(x_in, norm_mix_g[l][None], w_in[l].astype(_bf16), lb_all[l][None],
                      hg_out_norm_g[l][None], conv_w[l], w_branch_hg[l].astype(_bf16),
                      w_branch_conv[l].astype(_bf16), w_out[l].astype(_bf16), ts=MIX_TILE)
            xf = xc.reshape(tg, d)
            h, scores = _scores(xf, norm_ffn_g[l][None], wq,
                                peer_keys1[l].astype(_bf16), peer_keys2[l].astype(_bf16), tr=ROUTE_TILE)
            return (xf, *_sc_route_dot(scores, u_sc, h)), scores

        def back(xf, idx, gate, acts, n_sc):
            n_tc = tg - n_sc
            w = _gate_weights(acts, gate, tw=FINAL_TILE)
            peer_tc = _vaxpy(idx * PACK_ROWS, w, v_tab.reshape(-1, 128), tb=EXPERT_TILE, count=n_tc).reshape(n_tc, d)
            peer_sc = _sc_vaxpy(idx, w, v_sc, first=n_tc, count=n_sc) if n_sc else None
            return peer_tc, peer_sc

        fronts = [front(x[c * bg:(c + 1) * bg]) for c in range(n_groups)]
        v_rows = lax.optimization_barrier((peer_v[l], fronts[0][1]))[0]
        v_tab = _pack_table(v_rows)
        v_sc = _pack_halves(v_rows)
        acc = jnp.zeros((b_ * s_, d), _f32)
        sc_parts = []
        for c, ((xf, idx, gate, acts), _) in enumerate(fronts):
            share = SC_SHARE_LAST if c == n_groups - 1 else SC_SHARE
            n_sc = share if share < tg else 0
            acts = lax.optimization_barrier((acts, acc if c else fronts[-1][1]))[0]
            peer_tc, peer_sc = back(xf, idx, gate, acts, n_sc)
            acc = _final(acc, xf, peer_tc, g, normalize=last, tf=FINAL_TILE, first=0, row=c * tg)
            sc_parts.append((c * tg + tg - n_sc, tg - n_sc, xf, peer_sc))
        for row, first, xf, p_sc in sc_parts:
            if p_sc is not None:
                acc = _final(acc, xf, p_sc, g, normalize=last, tf=FINAL_TILE, first=first, row=row)
        x = acc.reshape(b_, s_, d)
    return x
```

```python
import functools

import jax
import jax.numpy as jnp
from jax import lax
from jax.experimental import pallas as pl
from jax.experimental.pallas import tpu as pltpu
from jax.experimental.pallas import tpu_sc as plsc

EPS = 1e-6
CHUNK = 64
SUB = 8
HEADS = 8
HEAD_DIM = 64
HG_WIDTH = HEADS * HEAD_DIM
GROUP = 256
N_GROUPS = HG_WIDTH // GROUP
CONV_K = 3
PEER_HEADS = 8
PEER_NKEYS = 128
PEER_HALF = 128
PEER_TOPK = 16
PEER_SLOTS = PEER_HEADS * PEER_TOPK

VMEM_LIMIT_BYTES = 56 * 1024 * 1024

MIX_TILE = 256
ROUTE_TILE = 256
EXPERT_TILE = 128
FINAL_TILE = 256
BATCH_GROUPS = 8
SC_SHARE = 0
SC_SHARE_LAST = 2048

_f32 = jnp.float32
_bf16 = jnp.bfloat16


def _dot(a, b):
    return jnp.dot(a, b, preferred_element_type=_f32)


def _dot_nt(a, b):
    return lax.dot_general(a, b, (((1,), (1,)), ((), ())), preferred_element_type=_f32)


def _dot_tn(a, b):
    return lax.dot_general(a, b, (((0,), (0,)), ((), ())), preferred_element_type=_f32)


def _split3(x):
    hi = x.astype(_bf16)
    r1 = x - hi.astype(_f32)
    mid = r1.astype(_bf16)
    lo = (r1 - mid.astype(_f32)).astype(_bf16)
    return hi, mid, lo


def _split2(x):
    hi = x.astype(_bf16)
    lo = (x - hi.astype(_f32)).astype(_bf16)
    return hi, lo


def _dot_exact_rhs01(x, m01):
    hi, mid, lo = _split3(x)
    return _dot(hi, m01) + _dot(mid, m01) + _dot(lo, m01)


def _dot_exact_lhs01(m01, x):
    hi, mid, lo = _split3(x)
    return _dot(m01, hi) + _dot(m01, mid) + _dot(m01, lo)


def _iota(shape, dim):
    return lax.broadcasted_iota(jnp.int32, shape, dim)


def _hgrn2_chunk(q, k, lf, v, state_ref):
    n_sub = CHUNK // SUB
    row = _iota((CHUNK, CHUNK), 0)
    col = _iota((CHUNK, CHUNK), 1)
    tril = (col <= row).astype(_bf16)
    b = _dot_exact_lhs01(tril, lf)

    b_end = [b[(j + 1) * SUB - 1:(j + 1) * SUB, :] for j in range(n_sub)]
    b_end_rows = jnp.concatenate([jnp.broadcast_to(e, (SUB, HG_WIDTH)) for e in b_end], axis=0)
    b_last = b_end[-1]

    q_in = (q * jnp.exp(b)).astype(_bf16)
    k_sub = (k * jnp.exp(b_end_rows - b)).astype(_bf16)
    k_out = (k * jnp.exp(b_last - b)).astype(_bf16)
    q_from = [(q * jnp.exp(jnp.minimum(b - b_end[j], 0.0))).astype(_bf16) for j in range(n_sub - 1)]
    v_b = v.astype(_bf16)

    gr = _iota((GROUP, GROUP), 0) // HEAD_DIM
    gc = _iota((GROUP, GROUP), 1) // HEAD_DIM
    head_mask = gr == gc
    t_blk = _iota((CHUNK, GROUP), 0) // SUB
    s_blk = (_iota((CHUNK, GROUP), 1) % HEAD_DIM) // SUB

    outs = []
    for g in range(N_GROUPS):
        sl = slice(g * GROUP, (g + 1) * GROUP)
        st = state_ref[g]
        o_g = _dot_nt(q_in[:, sl], st.astype(_bf16))

        zero_b = jnp.zeros((), _bf16)
        k_bd = jnp.where(head_mask, jnp.concatenate([k_sub[:, sl]] * (GROUP // CHUNK), axis=0), zero_b)
        v_bd = jnp.where(head_mask, jnp.concatenate([v_b[:, sl]] * (GROUP // CHUNK), axis=0), zero_b)
        q_stack = jnp.concatenate([qf[:, sl] for qf in q_from], axis=0)
        r = _dot_nt(q_stack, k_bd)
        scores = jnp.zeros((CHUNK, GROUP), _f32)
        for j in range(n_sub - 1):
            sel = (s_blk == j) & (t_blk > j)
            scores = jnp.where(sel, r[j * CHUNK:(j + 1) * CHUNK, :], scores)
        o_g = o_g + _dot(scores.astype(_bf16), v_bd)
        outs.append(o_g)

        upd = _dot_tn(v_b[:, sl], k_out[:, sl])
        decay = jnp.exp(b_last[:, sl])
        state_ref[g] = st * decay + jnp.where(head_mask, upd, 0.0)
    o = jnp.concatenate(outs, axis=1)

    ones_bd = ((_iota((HG_WIDTH, HG_WIDTH), 0) // HEAD_DIM)
               == (_iota((HG_WIDTH, HG_WIDTH), 1) // HEAD_DIM)).astype(_bf16)
    t_in_sub = _iota((CHUNK, HG_WIDTH), 0) % SUB
    for lag in range(SUB):
        if lag == 0:
            p = q * k
            v_l = v
        else:
            valid = t_in_sub >= lag
            k_l = pltpu.roll(k, lag, 0)
            b_l = pltpu.roll(b, lag, 0)
            v_l = pltpu.roll(v, lag, 0)
            p = jnp.where(valid, q * k_l * jnp.exp(jnp.minimum(b - b_l, 0.0)), 0.0)
        s_l = _dot(p.astype(_bf16), ones_bd)
        o = o + s_l * v_l
    return o


def _mix_kernel(x_ref, g_ref, win_ref, lb_ref, hgn_ref, convw_ref, pa_ref, pb_ref, wo_ref,
                out_ref, state_ref, carry_ref, q_s, k_s, lf_s, v_s, o_s):
    ts = x_ref.shape[0]
    d_model = x_ref.shape[1]
    w = HG_WIDTH

    @pl.when(pl.program_id(1) == 0)
    def _():
        state_ref[...] = jnp.zeros_like(state_ref)
        carry_ref[...] = jnp.zeros_like(carry_ref)

    x = x_ref[...]
    h = x * lax.rsqrt(jnp.mean(x * x, axis=-1, keepdims=True) + EPS) * g_ref[...]
    hb = h.astype(_bf16)

    def proj(i, width=w):
        return _dot(hb, win_ref[:, i * w:i * w + width])

    lb = lb_ref[...]
    q_s[...] = jax.nn.silu(proj(0)) * (HEAD_DIM ** -0.5)
    forget = lb + (1.0 - lb) * jax.nn.sigmoid(proj(1))
    k_s[...] = 1.0 - forget
    lf_s[...] = jnp.log(forget)
    v_s[...] = proj(2)

    def chunk_body(c, carry):
        rows = pl.ds(pl.multiple_of(c * CHUNK, CHUNK), CHUNK)
        o_s[rows, :] = _hgrn2_chunk(q_s[rows, :], k_s[rows, :], lf_s[rows, :], v_s[rows, :], state_ref)
        return carry

    lax.fori_loop(0, ts // CHUNK, chunk_body, 0)

    o = o_s[...]
    ones_bd = ((_iota((w, w), 0) // HEAD_DIM) == (_iota((w, w), 1) // HEAD_DIM)).astype(_bf16)
    ms = _dot_exact_rhs01(o * o, ones_bd) * (1.0 / HEAD_DIM)
    o = o * lax.rsqrt(ms + EPS) * hgn_ref[...]
    y_a = (o * jax.nn.silu(proj(3))).astype(_bf16)

    u = proj(5) * proj(6)
    prev = carry_ref[...]
    rowi = _iota((ts, w), 0)
    u1 = jnp.where(rowi >= 1, pltpu.roll(u, 1, 0), jnp.broadcast_to(prev[7:8, :], (ts, w)))
    u2 = jnp.where(rowi >= 2, pltpu.roll(u, 2, 0),
                   jnp.where(rowi == 1, jnp.broadcast_to(prev[7:8, :], (ts, w)),
                             jnp.broadcast_to(prev[6:7, :], (ts, w))))
    carry_ref[...] = u[ts - 8:, :]
    cw = convw_ref[...]
    y_b = (proj(4) * (cw[0:1, :] * u2 + cw[1:2, :] * u1 + cw[2:3, :] * u)).astype(_bf16)

    g_a = jax.nn.sigmoid(proj(7, d_model))
    g_b = jax.nn.sigmoid(_dot(hb, win_ref[:, 7 * w + d_model:7 * w + 2 * d_model]))
    merged = g_a * _dot(y_a, pa_ref[...]) + g_b * _dot(y_b, pb_ref[...])
    out_ref[...] = x + _dot(merged.astype(_bf16), wo_ref[...])


def _const_spec(shape):
    nd = len(shape)
    return pl.BlockSpec(shape, lambda *_: (0,) * nd, pipeline_mode=pl.Buffered(1))


def _mix(x, norm_g, w_in, lb, hg_norm_g, conv_w, w_a, w_b, w_o, *, ts):
    b_, s_, d = x.shape
    in_cols = w_in.shape[1]
    w = HG_WIDTH
    grid = (b_, s_ // ts)
    return pl.pallas_call(
        _mix_kernel,
        grid=grid,
        in_specs=[
            pl.BlockSpec((None, ts, d), lambda b, s: (b, s, 0)),
            _const_spec((1, d)),
            _const_spec((d, in_cols)),
            _const_spec((1, w)),
            _const_spec((1, w)),
            _const_spec((CONV_K, w)),
            _const_spec((w, d)),
            _const_spec((w, d)),
            _const_spec((d, d)),
        ],
        out_specs=pl.BlockSpec((None, ts, d), lambda b, s: (b, s, 0)),
        out_shape=jax.ShapeDtypeStruct((b_, s_, d), _f32),
        scratch_shapes=[
            pltpu.VMEM((N_GROUPS, GROUP, GROUP), _f32),
            pltpu.VMEM((8, w), _f32),
            pltpu.VMEM((ts, w), _f32),
            pltpu.VMEM((ts, w), _f32),
            pltpu.VMEM((ts, w), _f32),
            pltpu.VMEM((ts, w), _f32),
            pltpu.VMEM((ts, w), _f32),
        ],
        compiler_params=pltpu.CompilerParams(
            dimension_semantics=("parallel", "arbitrary"),
            vmem_limit_bytes=VMEM_LIMIT_BYTES),
        name="mix",
    )(x, norm_g, w_in, lb, hg_norm_g, conv_w, w_a, w_b, w_o)


def _scores_kernel(x_ref, g_ref, wq_ref, k1_ref, k2_ref, h_ref, s_ref):
    x = x_ref[...]
    h = x * lax.rsqrt(jnp.mean(x * x, axis=-1, keepdims=True) + EPS) * g_ref[...]
    h_ref[...] = h
    hb = h.astype(_bf16)
    for hd in range(PEER_HEADS):
        for half, keys_ref in enumerate((k1_ref, k2_ref)):
            q = _dot(hb, wq_ref[hd, half]).astype(_bf16)
            col = (2 * hd + half) * PEER_NKEYS
            s_ref[:, col:col + PEER_NKEYS] = _dot_nt(q, keys_ref[hd])


def _scores(x, norm_g, w_query, keys1, keys2, *, tr):
    t, d = x.shape
    width = 2 * PEER_HEADS * PEER_NKEYS
    return pl.pallas_call(
        _scores_kernel,
        grid=(t // tr,),
        in_specs=[
            pl.BlockSpec((tr, d), lambda i: (i, 0)),
            _const_spec((1, d)),
            _const_spec(w_query.shape),
            _const_spec(keys1.shape),
            _const_spec(keys2.shape),
        ],
        out_specs=[
            pl.BlockSpec((tr, d), lambda i: (i, 0)),
            pl.BlockSpec((tr, width), lambda i: (i, 0)),
        ],
        out_shape=[
            jax.ShapeDtypeStruct((t, d), _f32),
            jax.ShapeDtypeStruct((t, width), _f32),
        ],
        compiler_params=pltpu.CompilerParams(
            dimension_semantics=("parallel",),
            vmem_limit_bytes=VMEM_LIMIT_BYTES),
        name="scores",
    )(x, norm_g, w_query, keys1, keys2)


SC_CORES = 2
SC_SUBCORES = 16
SC_LANES = 16
SC_WORKERS = SC_CORES * SC_SUBCORES
SC_TOKENS = 8
SC_GATHER = 32
SC_BLOCK = 8
SC_UNROLL = 2
SC_COLS = 8
HI_MASK = -65536
SC_HEADS_PER_ITER = 2


def _sc_mesh():
    return plsc.VectorSubcoreMesh(core_axis_name="c", subcore_axis_name="s")


def _sc_worker_base(per_worker):
    return (lax.axis_index("s") * SC_CORES + lax.axis_index("c")) * per_worker


def _sc_gather_loop(tab_hbm, idx_v, bufs, consume):
    n_parts = PEER_SLOTS // SC_GATHER
    n_gathers = SC_TOKENS * n_parts

    def gather(g, parity):
        rows, sem = bufs[parity]
        i = g // n_parts
        col = pl.multiple_of((g % n_parts) * SC_GATHER, SC_GATHER)
        return pltpu.make_async_copy(tab_hbm.at[idx_v.at[i, pl.ds(col, SC_GATHER)]], rows, sem)

    gather(0, 0).start()

    @pl.loop(0, n_gathers // 2)
    def _(pair):
        g = 2 * pair
        gather(g + 1, 1).start()
        gather(g, 0).wait()
        consume(g // n_parts, g % n_parts, bufs[0][0])

        @pl.when(g + 2 < n_gathers)
        def _():
            gather(g + 2, 0).start()

        gather(g + 1, 1).wait()
        consume((g + 1) // n_parts, (g + 1) % n_parts, bufs[1][0])


def _pack_halves(tab):
    half = tab.shape[1] // 2
    bits = lax.bitcast_convert_type(tab.astype(_bf16), jnp.uint16).astype(jnp.uint32)
    return lax.bitcast_convert_type(bits[:, :half] | (bits[:, half:] << 16), jnp.int32)


def _sc_unpack(words):
    lo = lax.bitcast_convert_type(words << 16, _f32)
    hi = lax.bitcast_convert_type(words & HI_MASK, _f32)
    return lo, hi


def _stair_vectors():
    pairs = [(a, c) for a in range(PEER_TOPK) for c in range(PEER_TOPK) if (a + 1) * (c + 1) <= PEER_TOPK]
    n = -(-len(pairs) // SC_LANES)
    fill = n * SC_LANES - len(pairs)
    a = jnp.asarray([p[0] for p in pairs] + [0] * fill, jnp.int32).reshape(n, SC_LANES)
    c = jnp.asarray([p[1] for p in pairs] + [0] * fill, jnp.int32).reshape(n, SC_LANES)
    pad = jnp.asarray([0.0] * len(pairs) + [-jnp.inf] * fill, _f32).reshape(n, SC_LANES)
    return a, c, pad


def _sc_route_dot(scores, tab, h):
    t, width = scores.shape
    d = h.shape[1]
    half_d = d // 2
    per_worker = t // SC_WORKERS
    pair_a, pair_c, pair_pad = _stair_vectors()
    n_cand = pair_a.shape[0]
    n_vec = PEER_NKEYS // SC_LANES

    @functools.partial(
        pl.kernel, mesh=_sc_mesh(),
        out_type=(jax.ShapeDtypeStruct((t, PEER_SLOTS), jnp.int32),
                  jax.ShapeDtypeStruct((t, PEER_SLOTS), _f32),
                  jax.ShapeDtypeStruct((t, PEER_SLOTS), _f32)),
        scratch_types=[
            pltpu.VMEM((SC_TOKENS, width), _f32),
            pltpu.VMEM((SC_TOKENS, PEER_SLOTS), jnp.int32),
            pltpu.VMEM((SC_TOKENS, PEER_SLOTS), _f32),
            pltpu.VMEM((n_cand, SC_LANES), jnp.int32),
            pltpu.VMEM((n_cand, SC_LANES), jnp.int32),
            pltpu.VMEM((n_cand, SC_LANES), _f32),
            pltpu.VMEM((4 * SC_HEADS_PER_ITER, SC_LANES), _f32),
            pltpu.VMEM((SC_TOKENS, d), _f32),
            pltpu.VMEM((SC_TOKENS, PEER_SLOTS), _f32),
            pltpu.VMEM((SC_GATHER, half_d), jnp.int32),
            pltpu.VMEM((SC_GATHER, half_d), jnp.int32),
            pltpu.SemaphoreType.DMA,
            pltpu.SemaphoreType.DMA,
        ],
        compiler_params=pltpu.CompilerParams(needs_layout_passes=False),
        name="sc_route_dot",
    )
    def body(s_hbm, a_hbm, c_hbm, pad_hbm, tab_hbm, h_hbm, idx_hbm, gate_hbm, acts_hbm,
             s_v, idx_v, gate_v, a_v, c_v, pad_v, top_v, h_v, acts_v, rows0, rows1, sem0, sem1):
        base = _sc_worker_base(per_worker)
        lane = lax.iota(jnp.int32, SC_LANES)
        pltpu.sync_copy(a_hbm, a_v)
        pltpu.sync_copy(c_hbm, c_v)
        pltpu.sync_copy(pad_hbm, pad_v)

        def dots(i, part, rows):
            def block_body(blk, carry):
                row0 = blk * SC_LANES
                outv = jnp.zeros((SC_LANES,), _f32)
                for sub in range(SC_LANES // SC_BLOCK):
                    def chunk_body(jj, accs):
                        accs = list(accs)
                        for u in range(SC_UNROLL):
                            off = pl.multiple_of((jj * SC_UNROLL + u) * SC_LANES, SC_LANES)
                            h_lo = h_v[i, pl.ds(off, SC_LANES)]
                            h_hi = h_v[i, pl.ds(half_d + off, SC_LANES)]
                            for e in range(SC_BLOCK):
                                lo, hi = _sc_unpack(rows[row0 + sub * SC_BLOCK + e, pl.ds(off, SC_LANES)])
                                accs[e] = accs[e] + (lo * h_lo + hi * h_hi)
                        return tuple(accs)

                    accs = lax.fori_loop(0, half_d // SC_LANES // SC_UNROLL, chunk_body,
                                         tuple(jnp.zeros((SC_LANES,), _f32) for _ in range(SC_BLOCK)))
                    for e in range(SC_BLOCK):
                        outv = jnp.where(lane == sub * SC_BLOCK + e, jnp.sum(accs[e]), outv)
                col = pl.multiple_of(part * SC_GATHER + row0, SC_LANES)
                acts_v[i, pl.ds(col, SC_LANES)] = outv
                return carry

            lax.fori_loop(0, SC_GATHER // SC_LANES, block_body, 0)

        def sort_desc(k, v):
            return plsc.sort_key_val(k, v, descending=True)

        def merge(x, y):
            yk, yv = lax.rev(y[0], (0,)), lax.rev(y[1], (0,))
            take = x[0] >= yk
            return sort_desc(jnp.where(take, x[0], yk), jnp.where(take, x[1], yv))

        def top_of(vectors):
            while len(vectors) > 1:
                vectors = [merge(vectors[j], vectors[j + 1]) for j in range(0, len(vectors), 2)]
            return vectors[0]

        def head_body(i, hd, slot):
            halves = []
            for half in range(2):
                col = (2 * hd + half) * PEER_NKEYS
                vecs = [sort_desc(s_v[i, pl.ds(pl.multiple_of(col + j * SC_LANES, SC_LANES), SC_LANES)],
                                  lane + j * SC_LANES) for j in range(n_vec)]
                halves.append(top_of(vecs))
            (v1, i1), (v2, i2) = halves
            del slot
            f1, f2 = i1.astype(_f32), i2.astype(_f32)

            def pick(vec, pos):
                return vec.at[pos].get(mode="promise_in_bounds")

            cands = []
            for q in range(n_cand):
                a, c = a_v[q, :], c_v[q, :]
                cs = pick(v1, a) + pick(v2, c) + pad_v[q, :]
                ci = pick(f1, a) * float(PEER_NKEYS) + pick(f2, c)
                cands.append(sort_desc(cs, ci))
            top_s, top_i = top_of(cands)
            e = jnp.exp(top_s - jnp.max(top_s))
            slots = pl.ds(pl.multiple_of(hd * PEER_TOPK, PEER_TOPK), PEER_TOPK)
            idx_v[i, slots] = top_i.astype(jnp.int32)
            gate_v[i, slots] = e / jnp.sum(e)

        @pl.loop(0, per_worker // SC_TOKENS)
        def _(step):
            tok = pl.multiple_of(base + step * SC_TOKENS, SC_TOKENS)
            pltpu.sync_copy(s_hbm.at[pl.ds(tok, SC_TOKENS)], s_v)
            pltpu.sync_copy(h_hbm.at[pl.ds(tok, SC_TOKENS)], h_v)

            per_token = PEER_HEADS // SC_HEADS_PER_ITER

            @pl.loop(0, SC_TOKENS * per_token)
            def _(n):
                for slot in range(SC_HEADS_PER_ITER):
                    head_body(n // per_token, (n % per_token) * SC_HEADS_PER_ITER + slot, slot)

            pltpu.sync_copy(idx_v, idx_hbm.at[pl.ds(tok, SC_TOKENS)])
            pltpu.sync_copy(gate_v, gate_hbm.at[pl.ds(tok, SC_TOKENS)])
            _sc_gather_loop(tab_hbm, idx_v, ((rows0, sem0), (rows1, sem1)), dots)
            pltpu.sync_copy(acts_v, acts_hbm.at[pl.ds(tok, SC_TOKENS)])

    return body(scores, pair_a, pair_c, pair_pad, tab, h)


def _sc_vaxpy(idx, w, tab, *, first, count):
    half = tab.shape[1]
    d = 2 * half
    per_worker = count // SC_WORKERS
    span = SC_COLS * SC_LANES

    @functools.partial(
        pl.kernel, mesh=_sc_mesh(),
        out_type=jax.ShapeDtypeStruct((count, d), _f32),
        scratch_types=[
            pltpu.VMEM((SC_TOKENS, PEER_SLOTS), jnp.int32),
            pltpu.VMEM((SC_TOKENS, PEER_SLOTS), _f32),
            pltpu.VMEM((SC_TOKENS, d), _f32),
            pltpu.VMEM((SC_GATHER, half), jnp.int32),
            pltpu.VMEM((SC_GATHER, half), jnp.int32),
            pltpu.SemaphoreType.DMA,
            pltpu.SemaphoreType.DMA,
        ],
        compiler_params=pltpu.CompilerParams(needs_layout_passes=False),
        name="sc_vaxpy",
    )
    def body(idx_hbm, w_hbm, tab_hbm, out_hbm, idx_v, w_v, out_v, rows0, rows1, sem0, sem1):
        base = _sc_worker_base(per_worker)

        def accumulate(i, part, rows):
            i_vec = jnp.full((SC_LANES,), i, jnp.int32)

            def span_body(cq, carry):
                def cols(c, offset=0):
                    return pl.ds(pl.multiple_of(offset + cq * span + c * SC_LANES, SC_LANES), SC_LANES)

                def expert_body(e, accs):
                    k_vec = jnp.full((SC_LANES,), part * SC_GATHER + e, jnp.int32)
                    wv = plsc.load_gather(w_v, [i_vec, k_vec])
                    new = []
                    for c in range(SC_COLS):
                        lo, hi = _sc_unpack(rows[e, cols(c)])
                        new += [accs[2 * c] + lo * wv, accs[2 * c + 1] + hi * wv]
                    return tuple(new)

                init = []
                for c in range(SC_COLS):
                    init += [out_v[i, cols(c)], out_v[i, cols(c, half)]]
                accs = lax.fori_loop(0, SC_GATHER, expert_body, tuple(init))
                for c in range(SC_COLS):
                    out_v[i, cols(c)] = accs[2 * c]
                    out_v[i, cols(c, half)] = accs[2 * c + 1]
                return carry

            lax.fori_loop(0, half // span, span_body, 0)

        @pl.loop(0, per_worker // SC_TOKENS)
        def _(step):
            off = pl.multiple_of(base + step * SC_TOKENS, SC_TOKENS)
            pltpu.sync_copy(idx_hbm.at[pl.ds(first + off, SC_TOKENS)], idx_v)
            pltpu.sync_copy(w_hbm.at[pl.ds(first + off, SC_TOKENS)], w_v)

            @pl.loop(0, SC_TOKENS)
            def _(i):
                @pl.loop(0, d // SC_LANES)
                def _(j):
                    out_v[i, pl.ds(pl.multiple_of(j * SC_LANES, SC_LANES), SC_LANES)] = (
                        jnp.zeros((SC_LANES,), _f32))

            _sc_gather_loop(tab_hbm, idx_v, ((rows0, sem0), (rows1, sem1)), accumulate)
            pltpu.sync_copy(out_v, out_hbm.at[pl.ds(off, SC_TOKENS)])

    return body(idx, w, tab)


ROW_CHUNKS = 8
PACK_ROWS = ROW_CHUNKS // 2
ROW_BUFFERS = 4


def _pack_table(tab):
    n, d = tab.shape
    bits = lax.bitcast_convert_type(tab.astype(_bf16), jnp.uint16).astype(jnp.uint32)
    bits = bits.reshape(n, PACK_ROWS, 2, d // ROW_CHUNKS)
    word = bits[:, :, 0, :] | (bits[:, :, 1, :] << 16)
    return lax.bitcast_convert_type(word, jnp.int32)


def _gate_weights_kernel(acts_ref, gate_ref, w_ref):
    a = acts_ref[...]
    gelu = 0.5 * a * (1.0 + lax.erf(a * (2.0 ** -0.5)))
    w_ref[...] = gate_ref[...] * gelu


def _gate_weights(acts, gate, *, tw):
    t, n = acts.shape
    spec = pl.BlockSpec((tw, n), lambda i: (i, 0))
    return pl.pallas_call(
        _gate_weights_kernel,
        grid=(t // tw,),
        in_specs=[spec, spec],
        out_specs=spec,
        out_shape=jax.ShapeDtypeStruct((t, n), _f32),
        compiler_params=pltpu.CompilerParams(dimension_semantics=("parallel",)),
        name="gate_weights",
    )(acts, gate)


def _gather_rows(idx_ref, t, tab_ref, rows_ref):
    for k in range(PEER_SLOTS):
        row = pl.multiple_of(idx_ref[t, k], PACK_ROWS)
        rows_ref[k * PACK_ROWS:(k + 1) * PACK_ROWS, :] = tab_ref[pl.ds(row, PACK_ROWS), :]


def _rows_matrix(rows_ref):
    return pltpu.bitcast(rows_ref[...], _bf16)


def _token_loop(tb, idx_ref, tab_ref, row_bufs, compute):
    n = len(row_bufs)
    ahead = 2
    for j in range(ahead):
        _gather_rows(idx_ref, j, tab_ref, row_bufs[j])

    def body(i, carry):
        t0 = n * i
        for j in range(n):
            nxt = jnp.minimum(t0 + j + ahead, tb - 1)
            _gather_rows(idx_ref, nxt, tab_ref, row_bufs[(j + ahead) % n])
            compute(t0 + j, _rows_matrix(row_bufs[j]))
        return carry

    lax.fori_loop(0, tb // n, body, 0)


def _chunk_diag_mask():
    shape = (ROW_CHUNKS, PEER_SLOTS * ROW_CHUNKS)
    return (_iota(shape, 1) % ROW_CHUNKS) == _iota(shape, 0)


def _vaxpy_kernel(idx_ref, w_ref, tab_ref, out_ref, wrep_ref, *row_bufs):
    tb = out_ref.shape[0]
    diag = _chunk_diag_mask()
    shape = (PEER_SLOTS, PEER_SLOTS * ROW_CHUNKS)
    spread = (_iota(shape, 0) == (_iota(shape, 1) // ROW_CHUNKS)).astype(_bf16)
    wrep_ref[...] = _dot_exact_rhs01(w_ref[...], spread)

    def compute(t, m):
        w_row = jnp.broadcast_to(wrep_ref[pl.ds(t, 1), :], diag.shape)
        w_hi, w_lo = _split2(jnp.where(diag, w_row, 0.0))
        out_ref[t] = _dot(w_hi, m) + _dot(w_lo, m)

    _token_loop(tb, idx_ref, tab_ref, row_bufs, compute)


def _vaxpy(idx, w, tab, *, tb, count):
    return pl.pallas_call(
        _vaxpy_kernel,
        grid=(count // tb,),
        in_specs=[
            pl.BlockSpec((tb, PEER_SLOTS), lambda i: (i, 0), memory_space=pltpu.SMEM),
            pl.BlockSpec((tb, PEER_SLOTS), lambda i: (i, 0)),
            _const_spec(tab.shape),
        ],
        out_specs=pl.BlockSpec((tb, ROW_CHUNKS, 128), lambda i: (i, 0, 0)),
        out_shape=jax.ShapeDtypeStruct((count, ROW_CHUNKS, 128), _f32),
        scratch_shapes=[
            pltpu.VMEM((tb, PEER_SLOTS * ROW_CHUNKS), _f32),
        ] + [pltpu.VMEM((PEER_SLOTS * PACK_ROWS, 128), jnp.int32)
             for _ in range(ROW_BUFFERS)],
        compiler_params=pltpu.CompilerParams(
            dimension_semantics=("parallel",),
            vmem_limit_bytes=VMEM_LIMIT_BYTES),
        name="vaxpy",
    )(idx, w, tab)


def _final_kernel(x_ref, p_ref, g_ref, acc_ref, out_ref, *, normalize):
    del acc_ref
    x = x_ref[...] + p_ref[...]
    if normalize:
        x = x * lax.rsqrt(jnp.mean(x * x, axis=-1, keepdims=True) + EPS) * g_ref[...]
    out_ref[...] = x


def _final(acc, x, peer, g, *, normalize, tf, first, row):
    t, d = peer.shape
    x_off = first // tf
    out_off = row // tf
    return pl.pallas_call(
        functools.partial(_final_kernel, normalize=normalize),
        grid=(t // tf,),
        in_specs=[
            pl.BlockSpec((tf, d), lambda i: (i + x_off, 0)),
            pl.BlockSpec((tf, d), lambda i: (i, 0)),
            _const_spec((1, d)),
            pl.BlockSpec(memory_space=pl.ANY),
        ],
        out_specs=pl.BlockSpec((tf, d), lambda i: (i + out_off, 0)),
        out_shape=jax.ShapeDtypeStruct(acc.shape, _f32),
        input_output_aliases={3: 0},
        compiler_params=pltpu.CompilerParams(dimension_semantics=("parallel",)),
        name="final_norm",
    )(x, peer, g, acc)


def kernel(x, norm_mix_g, w_in, hg_lb_logits, hg_out_norm_g, conv_w, w_branch_hg, w_branch_conv, w_out, norm_ffn_g, peer_w_query, peer_keys1, peer_keys2, peer_u, peer_v, norm_final_g):
    b_, s_, d = x.shape
    depth = w_in.shape[0]
    lb_all = jnp.cumsum(jax.nn.softmax(hg_lb_logits.astype(_f32), axis=0), axis=0)
    n_groups = BATCH_GROUPS if b_ % BATCH_GROUPS == 0 else 1
    bg = b_ // n_groups
    tg = bg * s_
    for l in range(depth):
        wq = peer_w_query[l].astype(_bf16).reshape(d, PEER_HEADS, 2, PEER_HALF).transpose(1, 2, 0, 3)
        u_sc = _pack_halves(peer_u[l])
        last = l == depth - 1
        g = norm_final_g[None] if last else jnp.ones((1, d), _f32)

        def front(x_in):
            xc = _mix(x_in, norm_mix_g[l][None], w_in[l].astype(_bf16), lb_all[l][None],
                      hg_out_norm_g[l][None], conv_w[l], w_branch_hg[l].astype(_bf16),
                      w_branch_conv[l].astype(_bf16), w_out[l].astype(_bf16), ts=MIX_TILE)
            xf = xc.reshape(tg, d)
            h, scores = _scores(xf, norm_ffn_g[l][None], wq,
                                peer_keys1[l].astype(_bf16), peer_keys2[l].astype(_bf16), tr=ROUTE_TILE)
            return (xf, *_sc_route_dot(scores, u_sc, h)), scores

        def back(xf, idx, gate, acts, n_sc):
            n_tc = tg - n_sc
            w = _gate_weights(acts, gate, tw=FINAL_TILE)
            peer_tc = _vaxpy(idx * PACK_ROWS, w, v_tab.reshape(-1, 128), tb=EXPERT_TILE, count=n_tc).reshape(n_tc, d)
            peer_sc = _sc_vaxpy(idx, w, v_sc, first=n_tc, count=n_sc) if n_sc else None
            return peer_tc, peer_sc

        fronts = [front(x[c * bg:(c + 1) * bg]) for c in range(n_groups)]
        v_rows = lax.optimization_barrier((peer_v[l], fronts[0][1]))[0]
        v_tab = _pack_table(v_rows)
        v_sc = _pack_halves(v_rows)
        acc = jnp.zeros((b_ * s_, d), _f32)
        sc_parts = []
        for c, ((xf, idx, gate, acts), _) in enumerate(fronts):
            share = SC_SHARE_LAST if c == n_groups - 1 else SC_SHARE
            n_sc = share if share < tg else 0
            acts = lax.optimization_barrier((acts, acc if c else fronts[-1][1]))[0]
            peer_tc, peer_sc = back(xf, idx, gate, acts, n_sc)
            acc = _final(acc, xf, peer_tc, g, normalize=last, tf=FINAL_TILE, first=0, row=c * tg)
            sc_parts.append((c * tg + tg - n_sc, tg - n_sc, xf, peer_sc))
        for row, first, xf, p_sc in sc_parts:
            if p_sc is not None:
                acc = _final(acc, xf, p_sc, g, normalize=last, tf=FINAL_TILE, first=first, row=row)
        x = acc.reshape(b_, s_, d)
    return x
```
